```python
import math
import jax, jax.numpy as jnp
from jax import lax
import numpy as np

D_MODEL = 1024
BATCH = 32
SEQ = 2048
DEPTH = 1

N_META = 16
HEAD_DIM = 64
N_Q_HEADS = D_MODEL // HEAD_DIM
N_KV_HEADS = N_Q_HEADS // 4
Q_PER_KV = N_Q_HEADS // N_KV_HEADS
WINDOW = 128
BLOCK = 128
ATTN_WIDTH = N_Q_HEADS * HEAD_DIM
KV_WIDTH = N_KV_HEADS * HEAD_DIM
SSM_GROUP = 16
SSM_WIDTH = D_MODEL // 2
SSM_GROUPS = SSM_WIDTH // SSM_GROUP
SSM_STATE = 64
D_FF = ((8 * D_MODEL // 3 + 255) // 256) * 256
NORM_EPS = 1e-6
NEG_INF = -1e30
SPLITS = [ATTN_WIDTH, ATTN_WIDTH + KV_WIDTH, ATTN_WIDTH + 2 * KV_WIDTH,
          ATTN_WIDTH + 2 * KV_WIDTH + SSM_WIDTH,
          ATTN_WIDTH + 2 * KV_WIDTH + SSM_WIDTH + D_MODEL]
IN_WIDTH = SPLITS[-1] + D_MODEL

kernel_name = "hybrid_swa_s5_gated_macaron"


def rmsnorm(x, g):
    xf = x.astype(jnp.float32)
    y = xf * lax.rsqrt(jnp.mean(xf * xf, axis=-1, keepdims=True) + NORM_EPS)
    return (y * g.astype(jnp.float32)).astype(x.dtype)


def swiglu(h, w1, w3, w2):
    return (jax.nn.silu(h @ w1) * (h @ w3)) @ w2


def sink_softmax(scores, mask, sink):
    scores = jnp.where(mask, scores, NEG_INF)
    sink_b = jnp.broadcast_to(sink[:, :, None, None], scores.shape[:-1] + (1,))
    p = jax.nn.softmax(jnp.concatenate([scores, sink_b], axis=-1), axis=-1)
    return p[..., :-1]


def sliding_window_attention(q, k, v, sinks):
    b, l = q.shape[0], q.shape[1]
    s = l - N_META
    nb = s // BLOCK
    q = (q * (HEAD_DIM ** -0.5)).reshape(b, l, N_KV_HEADS, Q_PER_KV, HEAD_DIM)
    k = k.reshape(b, l, N_KV_HEADS, HEAD_DIM)
    v = v.reshape(b, l, N_KV_HEADS, HEAD_DIM)
    sink = sinks.astype(jnp.float32).reshape(N_KV_HEADS, Q_PER_KV)
    q_meta, q_real = q[:, :N_META], q[:, N_META:]
    k_meta, k_real = k[:, :N_META], k[:, N_META:]
    v_meta, v_real = v[:, :N_META], v[:, N_META:]

    sc_m = jnp.einsum('bqkgd,bskd->bkgqs', q_meta, k_meta).astype(jnp.float32)
    mask_m = jnp.tril(jnp.ones((N_META, N_META), dtype=bool))
    p_m = sink_softmax(sc_m, mask_m, sink)
    out_m = jnp.einsum('bkgqs,bskd->bqkgd', p_m.astype(v.dtype), v_meta)

    qb = q_real.reshape(b, nb, BLOCK, N_KV_HEADS, Q_PER_KV, HEAD_DIM)
    kb = k_real.reshape(b, nb, BLOCK, N_KV_HEADS, HEAD_DIM)
    vb = v_real.reshape(b, nb, BLOCK, N_KV_HEADS, HEAD_DIM)
    k_band = jnp.concatenate([jnp.concatenate([jnp.zeros_like(kb[:, :1]), kb[:, :-1]], axis=1), kb], axis=2)
    v_band = jnp.concatenate([jnp.concatenate([jnp.zeros_like(vb[:, :1]), vb[:, :-1]], axis=1), vb], axis=2)
    qi = jnp.arange(BLOCK)[:, None]
    kj = jnp.arange(2 * BLOCK)[None, :]
    rel = BLOCK + qi - kj
    band_ok = (rel >= 0) & (rel < WINDOW)
    meta_ok = jnp.ones((BLOCK, N_META), dtype=bool)

    def block_fn(args):
        n, qn, kn, vn = args
        keys = jnp.concatenate([k_meta, kn], axis=1)
        vals = jnp.concatenate([v_meta, vn], axis=1)
        sc = jnp.einsum('bqkgd,bskd->bkgqs', qn, keys).astype(jnp.float32)
        valid = band_ok & ((n - 1) * BLOCK + kj >= 0)
        mask = jnp.concatenate([meta_ok, valid], axis=1)
        p = sink_softmax(sc, mask, sink)
        return jnp.einsum('bkgqs,bskd->bqkgd', p.astype(vals.dtype), vals)

    out_r = lax.map(block_fn, (jnp.arange(nb, dtype=jnp.int32), jnp.moveaxis(qb, 1, 0),
                               jnp.moveaxis(k_band, 1, 0), jnp.moveaxis(v_band, 1, 0)))
    out_r = jnp.moveaxis(out_r, 0, 1).reshape(b, s, ATTN_WIDTH)
    return jnp.concatenate([out_m.reshape(b, N_META, ATTN_WIDTH), out_r], axis=1)


def s5_ssm(u, a_re, a_im, log_step, b_re, b_im, c_re, c_im, d_skip):
    b, l, _ = u.shape
    uf = u.astype(jnp.float32).reshape(b, l, SSM_GROUPS, SSM_GROUP)
    ar, ai = a_re.astype(jnp.float32), a_im.astype(jnp.float32)
    step = jnp.exp(log_step.astype(jnp.float32))[:, None]
    mag = jnp.exp(ar * step)
    ang = ai * step
    lam_re, lam_im = mag * jnp.cos(ang), mag * jnp.sin(ang)
    den = ar * ar + ai * ai
    nr, ni = lam_re - 1.0, lam_im
    coef_re = (nr * ar + ni * ai) / den
    coef_im = (ni * ar - nr * ai) / den
    br, bi = b_re.astype(jnp.float32), b_im.astype(jnp.float32)
    bb_re = coef_re[..., None] * br - coef_im[..., None] * bi
    bb_im = coef_re[..., None] * bi + coef_im[..., None] * br
    bu_re = jnp.einsum('blgc,gnc->blgn', uf, bb_re)
    bu_im = jnp.einsum('blgc,gnc->blgn', uf, bb_im)
    la_re = jnp.broadcast_to(lam_re[None, None], (1, l, SSM_GROUPS, SSM_STATE))
    la_im = jnp.broadcast_to(lam_im[None, None], (1, l, SSM_GROUPS, SSM_STATE))

    def combine(e1, e2):
        a1r, a1i, b1r, b1i = e1
        a2r, a2i, b2r, b2i = e2
        return (a2r * a1r - a2i * a1i, a2r * a1i + a2i * a1r,
                a2r * b1r - a2i * b1i + b2r, a2r * b1i + a2i * b1r + b2i)

    _, _, x_re, x_im = lax.associative_scan(combine, (la_re, la_im, bu_re, bu_im), axis=1)
    y = (jnp.einsum('blgn,gcn->blgc', x_re, c_re.astype(jnp.float32))
         - jnp.einsum('blgn,gcn->blgc', x_im, c_im.astype(jnp.float32)))
    y = y + d_skip.astype(jnp.float32).reshape(SSM_GROUPS, SSM_GROUP) * uf
    return y.reshape(b, l, SSM_WIDTH).astype(u.dtype)


def _fwd_setup_inputs(seed: int = 0) -> dict:
    key = jax.random.key(seed)
    ks = jax.random.split(key, 32)
    nrm = lambda k, shape, scale: jax.random.normal(k, shape, jnp.float32) * scale
    n_idx = jnp.arange(SSM_STATE, dtype=jnp.float32)
    return {
        "x": nrm(ks[0], (BATCH, SEQ, D_MODEL), 1.0),
        "meta_tokens": nrm(ks[1], (N_META, D_MODEL), 1.0),
        "ffn1_norm": 1.0 + nrm(ks[2], (DEPTH, D_MODEL), 0.02),
        "ffn1_w1": nrm(ks[3], (DEPTH, D_MODEL, D_FF), D_MODEL ** -0.5),
        "ffn1_w3": nrm(ks[4], (DEPTH, D_MODEL, D_FF), D_MODEL ** -0.5),
        "ffn1_w2": nrm(ks[5], (DEPTH, D_FF, D_MODEL), D_FF ** -0.5),
        "mix_norm": 1.0 + nrm(ks[6], (DEPTH, D_MODEL), 0.02),
        "w_in": nrm(ks[7], (DEPTH, D_MODEL, IN_WIDTH), D_MODEL ** -0.5),
        "attn_sinks": nrm(ks[8], (DEPTH, N_Q_HEADS), 0.5),
        "ssm_a_re": -0.5 + nrm(ks[9], (DEPTH, SSM_GROUPS, SSM_STATE), 0.01),
        "ssm_a_im": math.pi * n_idx[None, None, :] + nrm(ks[10], (DEPTH, SSM_GROUPS, SSM_STATE), 0.01),
        "ssm_log_step": jax.random.uniform(ks[11], (DEPTH, SSM_GROUPS), jnp.float32,
                                           math.log(0.001), math.log(0.1)),
        "ssm_b_re": nrm(ks[12], (DEPTH, SSM_GROUPS, SSM_STATE, SSM_GROUP), (2 * SSM_GROUP) ** -0.5),
        "ssm_b_im": nrm(ks[13], (DEPTH, SSM_GROUPS, SSM_STATE, SSM_GROUP), (2 * SSM_GROUP) ** -0.5),
        "ssm_c_re": nrm(ks[14], (DEPTH, SSM_GROUPS, SSM_GROUP, SSM_STATE), SSM_STATE ** -0.5),
        "ssm_c_im": nrm(ks[15], (DEPTH, SSM_GROUPS, SSM_GROUP, SSM_STATE), SSM_STATE ** -0.5),
        "ssm_d": nrm(ks[16], (DEPTH, SSM_WIDTH), 1.0),
        "ssm_glu_a": nrm(ks[17], (DEPTH, SSM_WIDTH, D_MODEL), SSM_WIDTH ** -0.5),
        "ssm_glu_b": nrm(ks[18], (DEPTH, SSM_WIDTH, D_MODEL), SSM_WIDTH ** -0.5),
        "w_out": nrm(ks[19], (DEPTH, D_MODEL, D_MODEL), D_MODEL ** -0.5),
        "ffn2_norm": 1.0 + nrm(ks[20], (DEPTH, D_MODEL), 0.02),
        "ffn2_w1": nrm(ks[21], (DEPTH, D_MODEL, D_FF), D_MODEL ** -0.5),
        "ffn2_w3": nrm(ks[22], (DEPTH, D_MODEL, D_FF), D_MODEL ** -0.5),
        "ffn2_w2": nrm(ks[23], (DEPTH, D_FF, D_MODEL), D_FF ** -0.5),
        "final_norm": 1.0 + nrm(ks[24], (D_MODEL,), 0.02),
    }


def _fwd_reference(x, meta_tokens, ffn1_norm, ffn1_w1, ffn1_w3, ffn1_w2, mix_norm, w_in,
              attn_sinks, ssm_a_re, ssm_a_im, ssm_log_step, ssm_b_re, ssm_b_im,
              ssm_c_re, ssm_c_im, ssm_d, ssm_glu_a, ssm_glu_b, w_out,
              ffn2_norm, ffn2_w1, ffn2_w3, ffn2_w2, final_norm):
    b = x.shape[0]
    meta = jnp.broadcast_to(meta_tokens[None].astype(x.dtype), (b, N_META, D_MODEL))
    h = jnp.concatenate([meta, x], axis=1)
    for i in range(DEPTH):
        h = h + 0.5 * swiglu(rmsnorm(h, ffn1_norm[i]), ffn1_w1[i], ffn1_w3[i], ffn1_w2[i])
        hn = rmsnorm(h, mix_norm[i])
        q, k, v, u, g_attn, g_ssm = jnp.split(hn @ w_in[i], SPLITS, axis=-1)
        attn = sliding_window_attention(q, k, v, attn_sinks[i])
        y = s5_ssm(u, ssm_a_re[i], ssm_a_im[i], ssm_log_step[i], ssm_b_re[i], ssm_b_im[i],
                   ssm_c_re[i], ssm_c_im[i], ssm_d[i])
        y = jax.nn.gelu(y)
        ssm = (y @ ssm_glu_a[i]) * jax.nn.sigmoid(y @ ssm_glu_b[i])
        merged = jax.nn.sigmoid(g_attn) * attn + jax.nn.sigmoid(g_ssm) * ssm
        h = h + merged @ w_out[i]
        h = h + 0.5 * swiglu(rmsnorm(h, ffn2_norm[i]), ffn2_w1[i], ffn2_w3[i], ffn2_w2[i])
    return rmsnorm(h, final_norm)[:, N_META:]


import jax as _jax
import jax.numpy as _jnp

TWIN_FORMAT = 'train_step'
FWD_PARAMS = ['x', 'meta_tokens', 'ffn1_norm', 'ffn1_w1', 'ffn1_w3', 'ffn1_w2', 'mix_norm', 'w_in', 'attn_sinks', 'ssm_a_re', 'ssm_a_im', 'ssm_log_step', 'ssm_b_re', 'ssm_b_im', 'ssm_c_re', 'ssm_c_im', 'ssm_d', 'ssm_glu_a', 'ssm_glu_b', 'w_out', 'ffn2_norm', 'ffn2_w1', 'ffn2_w3', 'ffn2_w2', 'final_norm']
TWIN_WEIGHTS = ['meta_tokens', 'ffn1_norm', 'ffn1_w1', 'ffn1_w3', 'ffn1_w2', 'mix_norm', 'w_in', 'attn_sinks', 'ssm_a_re', 'ssm_a_im', 'ssm_log_step', 'ssm_b_re', 'ssm_b_im', 'ssm_c_re', 'ssm_c_im', 'ssm_d', 'ssm_glu_a', 'ssm_glu_b', 'w_out', 'ffn2_norm', 'ffn2_w1', 'ffn2_w3', 'ffn2_w2', 'final_norm']
TWIN_DIFF_INPUT = 'x'
TWIN_INPUTS = ['x', 'meta_tokens', 'ffn1_norm', 'ffn1_w1', 'ffn1_w3', 'ffn1_w2', 'mix_norm', 'w_in', 'attn_sinks', 'ssm_a_re', 'ssm_a_im', 'ssm_log_step', 'ssm_b_re', 'ssm_b_im', 'ssm_c_re', 'ssm_c_im', 'ssm_d', 'ssm_glu_a', 'ssm_glu_b', 'w_out', 'ffn2_norm', 'ffn2_w1', 'ffn2_w3', 'ffn2_w2', 'final_norm', 'loss_target', 'm_meta_tokens', 'm_ffn1_norm', 'm_ffn1_w1', 'm_ffn1_w3', 'm_ffn1_w2', 'm_mix_norm', 'm_w_in', 'm_attn_sinks', 'm_ssm_a_re', 'm_ssm_a_im', 'm_ssm_log_step', 'm_ssm_b_re', 'm_ssm_b_im', 'm_ssm_c_re', 'm_ssm_c_im', 'm_ssm_d', 'm_ssm_glu_a', 'm_ssm_glu_b', 'm_w_out', 'm_ffn2_norm', 'm_ffn2_w1', 'm_ffn2_w3', 'm_ffn2_w2', 'm_final_norm', 'v_meta_tokens', 'v_ffn1_norm', 'v_ffn1_w1', 'v_ffn1_w3', 'v_ffn1_w2', 'v_mix_norm', 'v_w_in', 'v_attn_sinks', 'v_ssm_a_re', 'v_ssm_a_im', 'v_ssm_log_step', 'v_ssm_b_re', 'v_ssm_b_im', 'v_ssm_c_re', 'v_ssm_c_im', 'v_ssm_d', 'v_ssm_glu_a', 'v_ssm_glu_b', 'v_w_out', 'v_ffn2_norm', 'v_ffn2_w1', 'v_ffn2_w3', 'v_ffn2_w2', 'v_final_norm']
TWIN_OUTPUTS = ['loss', 'grad_x', 'grad_meta_tokens', 'grad_ffn1_norm', 'grad_ffn1_w1', 'grad_ffn1_w3', 'grad_ffn1_w2', 'grad_mix_norm', 'grad_w_in', 'grad_attn_sinks', 'grad_ssm_a_re', 'grad_ssm_a_im', 'grad_ssm_log_step', 'grad_ssm_b_re', 'grad_ssm_b_im', 'grad_ssm_c_re', 'grad_ssm_c_im', 'grad_ssm_d', 'grad_ssm_glu_a', 'grad_ssm_glu_b', 'grad_w_out', 'grad_ffn2_norm', 'grad_ffn2_w1', 'grad_ffn2_w3', 'grad_ffn2_w2', 'grad_final_norm', 'delta_meta_tokens', 'delta_ffn1_norm', 'delta_ffn1_w1', 'delta_ffn1_w3', 'delta_ffn1_w2', 'delta_mix_norm', 'delta_w_in', 'delta_attn_sinks', 'delta_ssm_a_re', 'delta_ssm_a_im', 'delta_ssm_log_step', 'delta_ssm_b_re', 'delta_ssm_b_im', 'delta_ssm_c_re', 'delta_ssm_c_im', 'delta_ssm_d', 'delta_ssm_glu_a', 'delta_ssm_glu_b', 'delta_w_out', 'delta_ffn2_norm', 'delta_ffn2_w1', 'delta_ffn2_w3', 'delta_ffn2_w2', 'delta_final_norm', 'new_m_meta_tokens', 'new_m_ffn1_norm', 'new_m_ffn1_w1', 'new_m_ffn1_w3', 'new_m_ffn1_w2', 'new_m_mix_norm', 'new_m_w_in', 'new_m_attn_sinks', 'new_m_ssm_a_re', 'new_m_ssm_a_im', 'new_m_ssm_log_step', 'new_m_ssm_b_re', 'new_m_ssm_b_im', 'new_m_ssm_c_re', 'new_m_ssm_c_im', 'new_m_ssm_d', 'new_m_ssm_glu_a', 'new_m_ssm_glu_b', 'new_m_w_out', 'new_m_ffn2_norm', 'new_m_ffn2_w1', 'new_m_ffn2_w3', 'new_m_ffn2_w2', 'new_m_final_norm', 'new_v_meta_tokens', 'new_v_ffn1_norm', 'new_v_ffn1_w1', 'new_v_ffn1_w3', 'new_v_ffn1_w2', 'new_v_mix_norm', 'new_v_w_in', 'new_v_attn_sinks', 'new_v_ssm_a_re', 'new_v_ssm_a_im', 'new_v_ssm_log_step', 'new_v_ssm_b_re', 'new_v_ssm_b_im', 'new_v_ssm_c_re', 'new_v_ssm_c_im', 'new_v_ssm_d', 'new_v_ssm_glu_a', 'new_v_ssm_glu_b', 'new_v_w_out', 'new_v_ffn2_norm', 'new_v_ffn2_w1', 'new_v_ffn2_w3', 'new_v_ffn2_w2', 'new_v_final_norm']
TWIN_LEAF_KINDS = {'loss': 'loss', 'grad_x': 'grad_x', 'grad_meta_tokens': 'grad_w', 'grad_ffn1_norm': 'grad_w', 'grad_ffn1_w1': 'grad_w', 'grad_ffn1_w3': 'grad_w', 'grad_ffn1_w2': 'grad_w', 'grad_mix_norm': 'grad_w', 'grad_w_in': 'grad_w', 'grad_attn_sinks': 'grad_w', 'grad_ssm_a_re': 'grad_w', 'grad_ssm_a_im': 'grad_w', 'grad_ssm_log_step': 'grad_w', 'grad_ssm_b_re': 'grad_w', 'grad_ssm_b_im': 'grad_w', 'grad_ssm_c_re': 'grad_w', 'grad_ssm_c_im': 'grad_w', 'grad_ssm_d': 'grad_w', 'grad_ssm_glu_a': 'grad_w', 'grad_ssm_glu_b': 'grad_w', 'grad_w_out': 'grad_w', 'grad_ffn2_norm': 'grad_w', 'grad_ffn2_w1': 'grad_w', 'grad_ffn2_w3': 'grad_w', 'grad_ffn2_w2': 'grad_w', 'grad_final_norm': 'grad_w', 'delta_meta_tokens': 'delta_w', 'delta_ffn1_norm': 'delta_w', 'delta_ffn1_w1': 'delta_w', 'delta_ffn1_w3': 'delta_w', 'delta_ffn1_w2': 'delta_w', 'delta_mix_norm': 'delta_w', 'delta_w_in': 'delta_w', 'delta_attn_sinks': 'delta_w', 'delta_ssm_a_re': 'delta_w', 'delta_ssm_a_im': 'delta_w', 'delta_ssm_log_step': 'delta_w', 'delta_ssm_b_re': 'delta_w', 'delta_ssm_b_im': 'delta_w', 'delta_ssm_c_re': 'delta_w', 'delta_ssm_c_im': 'delta_w', 'delta_ssm_d': 'delta_w', 'delta_ssm_glu_a': 'delta_w', 'delta_ssm_glu_b': 'delta_w', 'delta_w_out': 'delta_w', 'delta_ffn2_norm': 'delta_w', 'delta_ffn2_w1': 'delta_w', 'delta_ffn2_w3': 'delta_w', 'delta_ffn2_w2': 'delta_w', 'delta_final_norm': 'delta_w', 'new_m_meta_tokens': 'new_m', 'new_m_ffn1_norm': 'new_m', 'new_m_ffn1_w1': 'new_m', 'new_m_ffn1_w3': 'new_m', 'new_m_ffn1_w2': 'new_m', 'new_m_mix_norm': 'new_m', 'new_m_w_in': 'new_m', 'new_m_attn_sinks': 'new_m', 'new_m_ssm_a_re': 'new_m', 'new_m_ssm_a_im': 'new_m', 'new_m_ssm_log_step': 'new_m', 'new_m_ssm_b_re': 'new_m', 'new_m_ssm_b_im': 'new_m', 'new_m_ssm_c_re': 'new_m', 'new_m_ssm_c_im': 'new_m', 'new_m_ssm_d': 'new_m', 'new_m_ssm_glu_a': 'new_m', 'new_m_ssm_glu_b': 'new_m', 'new_m_w_out': 'new_m', 'new_m_ffn2_norm': 'new_m', 'new_m_ffn2_w1': 'new_m', 'new_m_ffn2_w3': 'new_m', 'new_m_ffn2_w2': 'new_m', 'new_m_final_norm': 'new_m', 'new_v_meta_tokens': 'new_v', 'new_v_ffn1_norm': 'new_v', 'new_v_ffn1_w1': 'new_v', 'new_v_ffn1_w3': 'new_v', 'new_v_ffn1_w2': 'new_v', 'new_v_mix_norm': 'new_v', 'new_v_w_in': 'new_v', 'new_v_attn_sinks': 'new_v', 'new_v_ssm_a_re': 'new_v', 'new_v_ssm_a_im': 'new_v', 'new_v_ssm_log_step': 'new_v', 'new_v_ssm_b_re': 'new_v', 'new_v_ssm_b_im': 'new_v', 'new_v_ssm_c_re': 'new_v', 'new_v_ssm_c_im': 'new_v', 'new_v_ssm_d': 'new_v', 'new_v_ssm_glu_a': 'new_v', 'new_v_ssm_glu_b': 'new_v', 'new_v_w_out': 'new_v', 'new_v_ffn2_norm': 'new_v', 'new_v_ffn2_w1': 'new_v', 'new_v_ffn2_w3': 'new_v', 'new_v_ffn2_w2': 'new_v', 'new_v_final_norm': 'new_v'}


def _forward(args):
    return _fwd_reference(*[args[k] for k in FWD_PARAMS])


def _output_shape():
    out = _jax.eval_shape(lambda: _forward(_fwd_setup_inputs(0)))
    return out.shape, out.dtype

N_MICROBATCH = 1
ADAM_LR = 0.001
ADAM_B1 = 0.9
ADAM_B2 = 0.999
ADAM_EPS = 1e-08
ADAM_WD = 0.01
ADAM_STEP = 10
PER_EXAMPLE_BATCH_AXIS = {'x': 0, 'loss_target': 0}
SHARED_INPUTS = []
_WEIGHT_DTYPES = {'meta_tokens': _jnp.float32, 'ffn1_norm': _jnp.float32, 'ffn1_w1': _jnp.float32, 'ffn1_w3': _jnp.float32, 'ffn1_w2': _jnp.float32, 'mix_norm': _jnp.float32, 'w_in': _jnp.float32, 'attn_sinks': _jnp.float32, 'ssm_a_re': _jnp.float32, 'ssm_a_im': _jnp.float32, 'ssm_log_step': _jnp.float32, 'ssm_b_re': _jnp.float32, 'ssm_b_im': _jnp.float32, 'ssm_c_re': _jnp.float32, 'ssm_c_im': _jnp.float32, 'ssm_d': _jnp.float32, 'ssm_glu_a': _jnp.float32, 'ssm_glu_b': _jnp.float32, 'w_out': _jnp.float32, 'ffn2_norm': _jnp.float32, 'ffn2_w1': _jnp.float32, 'ffn2_w3': _jnp.float32, 'ffn2_w2': _jnp.float32, 'final_norm': _jnp.float32}
MOMENT_SCALE = {'meta_tokens': 4.382828e-03, 'ffn1_norm': 1.077563e-01, 'ffn1_w1': 4.691154e-02, 'ffn1_w3': 4.538169e-02, 'ffn1_w2': 7.512503e-02, 'mix_norm': 6.672042e-02, 'w_in': 3.239108e-02, 'attn_sinks': 2.192195e-03, 'ssm_a_re': 4.937638e-03, 'ssm_a_im': 6.419089e-03, 'ssm_log_step': 5.502873e+00, 'ssm_b_re': 3.320724e-03, 'ssm_b_im': 3.279992e-03, 'ssm_c_re': 4.917126e-03, 'ssm_c_im': 4.909680e-03, 'ssm_d': 7.419307e-02, 'ssm_glu_a': 4.564495e-02, 'ssm_glu_b': 1.332947e-02, 'w_out': 5.058846e-02, 'ffn2_norm': 1.062467e-01, 'ffn2_w1': 4.233058e-02, 'ffn2_w3': 4.091990e-02, 'ffn2_w2': 6.808134e-02, 'final_norm': 6.396900e+01}


def _to_microbatches(a, axis):
    t = _jnp.moveaxis(a, axis, 0)
    t = t.reshape((N_MICROBATCH, t.shape[0] // N_MICROBATCH) + t.shape[1:])
    return _jnp.moveaxis(t, 1, axis + 1)


def setup_inputs(seed: int = 0) -> dict:
    inp = _fwd_setup_inputs(seed)
    key = _jax.random.fold_in(_jax.random.key(seed), 7919)
    shape, _ = _output_shape()
    out = dict(inp)
    out["loss_target"] = _jax.random.normal(_jax.random.fold_in(key, 0), shape, _jnp.float32)
    for i, name in enumerate(TWIN_WEIGHTS):
        w = inp[name].astype(_jnp.float32)
        if MOMENT_SCALE is None:
            s = _jnp.sqrt(_jnp.mean(_jnp.square(w)) + 1e-30)
        else:
            s = MOMENT_SCALE[name]
        km, kv = _jax.random.split(_jax.random.fold_in(key, i + 1))
        out[name] = w
        out["m_" + name] = s * _jax.random.normal(km, w.shape, _jnp.float32)
        out["v_" + name] = (s * s) * _jax.random.uniform(kv, w.shape, _jnp.float32, 0.5, 1.5)
    if N_MICROBATCH > 1:
        for name, axis in PER_EXAMPLE_BATCH_AXIS.items():
            out[name] = _to_microbatches(out[name], axis)
    return {'x': out['x'], 'meta_tokens': out['meta_tokens'], 'ffn1_norm': out['ffn1_norm'], 'ffn1_w1': out['ffn1_w1'], 'ffn1_w3': out['ffn1_w3'], 'ffn1_w2': out['ffn1_w2'], 'mix_norm': out['mix_norm'], 'w_in': out['w_in'], 'attn_sinks': out['attn_sinks'], 'ssm_a_re': out['ssm_a_re'], 'ssm_a_im': out['ssm_a_im'], 'ssm_log_step': out['ssm_log_step'], 'ssm_b_re': out['ssm_b_re'], 'ssm_b_im': out['ssm_b_im'], 'ssm_c_re': out['ssm_c_re'], 'ssm_c_im': out['ssm_c_im'], 'ssm_d': out['ssm_d'], 'ssm_glu_a': out['ssm_glu_a'], 'ssm_glu_b': out['ssm_glu_b'], 'w_out': out['w_out'], 'ffn2_norm': out['ffn2_norm'], 'ffn2_w1': out['ffn2_w1'], 'ffn2_w3': out['ffn2_w3'], 'ffn2_w2': out['ffn2_w2'], 'final_norm': out['final_norm'], 'loss_target': out['loss_target'], 'm_meta_tokens': out['m_meta_tokens'], 'm_ffn1_norm': out['m_ffn1_norm'], 'm_ffn1_w1': out['m_ffn1_w1'], 'm_ffn1_w3': out['m_ffn1_w3'], 'm_ffn1_w2': out['m_ffn1_w2'], 'm_mix_norm': out['m_mix_norm'], 'm_w_in': out['m_w_in'], 'm_attn_sinks': out['m_attn_sinks'], 'm_ssm_a_re': out['m_ssm_a_re'], 'm_ssm_a_im': out['m_ssm_a_im'], 'm_ssm_log_step': out['m_ssm_log_step'], 'm_ssm_b_re': out['m_ssm_b_re'], 'm_ssm_b_im': out['m_ssm_b_im'], 'm_ssm_c_re': out['m_ssm_c_re'], 'm_ssm_c_im': out['m_ssm_c_im'], 'm_ssm_d': out['m_ssm_d'], 'm_ssm_glu_a': out['m_ssm_glu_a'], 'm_ssm_glu_b': out['m_ssm_glu_b'], 'm_w_out': out['m_w_out'], 'm_ffn2_norm': out['m_ffn2_norm'], 'm_ffn2_w1': out['m_ffn2_w1'], 'm_ffn2_w3': out['m_ffn2_w3'], 'm_ffn2_w2': out['m_ffn2_w2'], 'm_final_norm': out['m_final_norm'], 'v_meta_tokens': out['v_meta_tokens'], 'v_ffn1_norm': out['v_ffn1_norm'], 'v_ffn1_w1': out['v_ffn1_w1'], 'v_ffn1_w3': out['v_ffn1_w3'], 'v_ffn1_w2': out['v_ffn1_w2'], 'v_mix_norm': out['v_mix_norm'], 'v_w_in': out['v_w_in'], 'v_attn_sinks': out['v_attn_sinks'], 'v_ssm_a_re': out['v_ssm_a_re'], 'v_ssm_a_im': out['v_ssm_a_im'], 'v_ssm_log_step': out['v_ssm_log_step'], 'v_ssm_b_re': out['v_ssm_b_re'], 'v_ssm_b_im': out['v_ssm_b_im'], 'v_ssm_c_re': out['v_ssm_c_re'], 'v_ssm_c_im': out['v_ssm_c_im'], 'v_ssm_d': out['v_ssm_d'], 'v_ssm_glu_a': out['v_ssm_glu_a'], 'v_ssm_glu_b': out['v_ssm_glu_b'], 'v_w_out': out['v_w_out'], 'v_ffn2_norm': out['v_ffn2_norm'], 'v_ffn2_w1': out['v_ffn2_w1'], 'v_ffn2_w3': out['v_ffn2_w3'], 'v_ffn2_w2': out['v_ffn2_w2'], 'v_final_norm': out['v_final_norm']}


def _loss(weights, diff, rest, loss_target):
    with _jax.named_scope("forward"):
        args = {**rest, TWIN_DIFF_INPUT: diff, **{k: w.astype(_WEIGHT_DTYPES[k]) for k, w in weights.items()}}
        y = _forward(args)
    with _jax.named_scope("loss_head"):
        err = _jnp.square(y.astype(_jnp.float32) - loss_target)
        return 0.5 * _jnp.sum(_jnp.mean(err, axis=-1)) if err.ndim else 0.5 * err


def _adamw(w, g, m, v):
    m = ADAM_B1 * m + (1.0 - ADAM_B1) * g
    v = ADAM_B2 * v + (1.0 - ADAM_B2) * _jnp.square(g)
    m_hat = m / (1.0 - ADAM_B1 ** ADAM_STEP)
    v_hat = v / (1.0 - ADAM_B2 ** ADAM_STEP)
    delta = -ADAM_LR * (m_hat / (_jnp.sqrt(v_hat) + ADAM_EPS) + ADAM_WD * w)
    return delta, m, v


def reference(x, meta_tokens, ffn1_norm, ffn1_w1, ffn1_w3, ffn1_w2, mix_norm, w_in, attn_sinks, ssm_a_re, ssm_a_im, ssm_log_step, ssm_b_re, ssm_b_im, ssm_c_re, ssm_c_im, ssm_d, ssm_glu_a, ssm_glu_b, w_out, ffn2_norm, ffn2_w1, ffn2_w3, ffn2_w2, final_norm, loss_target, m_meta_tokens, m_ffn1_norm, m_ffn1_w1, m_ffn1_w3, m_ffn1_w2, m_mix_norm, m_w_in, m_attn_sinks, m_ssm_a_re, m_ssm_a_im, m_ssm_log_step, m_ssm_b_re, m_ssm_b_im, m_ssm_c_re, m_ssm_c_im, m_ssm_d, m_ssm_glu_a, m_ssm_glu_b, m_w_out, m_ffn2_norm, m_ffn2_w1, m_ffn2_w3, m_ffn2_w2, m_final_norm, v_meta_tokens, v_ffn1_norm, v_ffn1_w1, v_ffn1_w3, v_ffn1_w2, v_mix_norm, v_w_in, v_attn_sinks, v_ssm_a_re, v_ssm_a_im, v_ssm_log_step, v_ssm_b_re, v_ssm_b_im, v_ssm_c_re, v_ssm_c_im, v_ssm_d, v_ssm_glu_a, v_ssm_glu_b, v_w_out, v_ffn2_norm, v_ffn2_w1, v_ffn2_w3, v_ffn2_w2, v_final_norm):
    given = dict(x=x, meta_tokens=meta_tokens, ffn1_norm=ffn1_norm, ffn1_w1=ffn1_w1, ffn1_w3=ffn1_w3, ffn1_w2=ffn1_w2, mix_norm=mix_norm, w_in=w_in, attn_sinks=attn_sinks, ssm_a_re=ssm_a_re, ssm_a_im=ssm_a_im, ssm_log_step=ssm_log_step, ssm_b_re=ssm_b_re, ssm_b_im=ssm_b_im, ssm_c_re=ssm_c_re, ssm_c_im=ssm_c_im, ssm_d=ssm_d, ssm_glu_a=ssm_glu_a, ssm_glu_b=ssm_glu_b, w_out=w_out, ffn2_norm=ffn2_norm, ffn2_w1=ffn2_w1, ffn2_w3=ffn2_w3, ffn2_w2=ffn2_w2, final_norm=final_norm, loss_target=loss_target, m_meta_tokens=m_meta_tokens, m_ffn1_norm=m_ffn1_norm, m_ffn1_w1=m_ffn1_w1, m_ffn1_w3=m_ffn1_w3, m_ffn1_w2=m_ffn1_w2, m_mix_norm=m_mix_norm, m_w_in=m_w_in, m_attn_sinks=m_attn_sinks, m_ssm_a_re=m_ssm_a_re, m_ssm_a_im=m_ssm_a_im, m_ssm_log_step=m_ssm_log_step, m_ssm_b_re=m_ssm_b_re, m_ssm_b_im=m_ssm_b_im, m_ssm_c_re=m_ssm_c_re, m_ssm_c_im=m_ssm_c_im, m_ssm_d=m_ssm_d, m_ssm_glu_a=m_ssm_glu_a, m_ssm_glu_b=m_ssm_glu_b, m_w_out=m_w_out, m_ffn2_norm=m_ffn2_norm, m_ffn2_w1=m_ffn2_w1, m_ffn2_w3=m_ffn2_w3, m_ffn2_w2=m_ffn2_w2, m_final_norm=m_final_norm, v_meta_tokens=v_meta_tokens, v_ffn1_norm=v_ffn1_norm, v_ffn1_w1=v_ffn1_w1, v_ffn1_w3=v_ffn1_w3, v_ffn1_w2=v_ffn1_w2, v_mix_norm=v_mix_norm, v_w_in=v_w_in, v_attn_sinks=v_attn_sinks, v_ssm_a_re=v_ssm_a_re, v_ssm_a_im=v_ssm_a_im, v_ssm_log_step=v_ssm_log_step, v_ssm_b_re=v_ssm_b_re, v_ssm_b_im=v_ssm_b_im, v_ssm_c_re=v_ssm_c_re, v_ssm_c_im=v_ssm_c_im, v_ssm_d=v_ssm_d, v_ssm_glu_a=v_ssm_glu_a, v_ssm_glu_b=v_ssm_glu_b, v_w_out=v_w_out, v_ffn2_norm=v_ffn2_norm, v_ffn2_w1=v_ffn2_w1, v_ffn2_w3=v_ffn2_w3, v_ffn2_w2=v_ffn2_w2, v_final_norm=v_final_norm)
    weights = {n: given[n] for n in TWIN_WEIGHTS}
    shared = {n: given[n] for n in SHARED_INPUTS}
    per_example = {n: given[n] for n in ['x']}
    grad_fn = _jax.value_and_grad(_loss, argnums=(0, 1))

    def one_microbatch(ex, loss_target):
        ex = dict(ex)
        diff = ex.pop(TWIN_DIFF_INPUT)
        return grad_fn(weights, diff, {**shared, **ex}, loss_target)

    if N_MICROBATCH == 1:
        loss, (grad_w, grad_x) = one_microbatch(per_example, given["loss_target"])
    else:
        def body(carry, xs):
            loss_sum, grad_sum = carry
            l_k, (gw_k, gx_k) = one_microbatch(xs[0], xs[1])
            with _jax.named_scope("update"):
                return (loss_sum + l_k, _jax.tree.map(_jnp.add, grad_sum, gw_k)), gx_k

        init = (_jnp.zeros((), _jnp.float32), _jax.tree.map(_jnp.zeros_like, weights))
        (loss, grad_w), grad_x = _jax.lax.scan(body, init, (per_example, given["loss_target"]))
    with _jax.named_scope("update"):
        delta_w, new_m, new_v = {}, {}, {}
        for n in TWIN_WEIGHTS:
            delta_w[n], new_m[n], new_v[n] = _adamw(weights[n], grad_w[n], given["m_" + n], given["v_" + n])
    return (loss, grad_x, *[grad_w[n] for n in TWIN_WEIGHTS], *[delta_w[n] for n in TWIN_WEIGHTS],
            *[new_m[n] for n in TWIN_WEIGHTS], *[new_v[n] for n in TWIN_WEIGHTS])
```

```python
import functools
import math

import jax
import jax.numpy as jnp
from jax import lax
from jax.experimental import pallas as pl
from jax.experimental.pallas import tpu as pltpu

F32 = jnp.float32
BF16 = jnp.bfloat16
MESH_IDS = pl.DeviceIdType.MESH

N_META = 16
HEAD_DIM = 64
Q_PER_KV = 4
BLOCK = 128
SSM_GROUP = 16
SSM_STATE = 64
SSM_LANES = 128
GROUPS_PER_COL = SSM_LANES // SSM_GROUP
STATE_LANES = GROUPS_PER_COL * SSM_STATE
NORM_EPS = 1e-6
NEG_INF = -1e30
ADAM_LR, ADAM_B1, ADAM_B2, ADAM_EPS, ADAM_WD, ADAM_STEP = 0.001, 0.9, 0.999, 1e-08, 0.01, 10
GELU_C = math.sqrt(2.0 / math.pi)
PACK_W = 1024
ROW_ALIGN = 16
VMEM_LIMIT = 56 * 1024 * 1024
ROW_TILE_CAP = 688


def _params(sem, **kw):
    return pltpu.CompilerParams(dimension_semantics=sem, vmem_limit_bytes=VMEM_LIMIT, **kw)


def _pick_tile(n, cap, mult):
    best = None
    for t in range(mult, min(n, cap) + 1, mult):
        if n % t == 0:
            best = t
    if best is None:
        raise ValueError(f"no tile for {n} (cap {cap}, multiple of {mult})")
    return best


def _sigmoid(x):
    return 1.0 / (1.0 + jnp.exp(-x))


_DIMS = {"nn": (((1,), (0,)), ((), ())), "nt": (((1,), (1,)), ((), ())), "tn": (((0,), (0,)), ((), ()))}


def _matmul(pairs, *, mode, out_dtype, name, tm, tn, tk, scale=1.0, res=None):
    a0, b0 = pairs[0]
    if mode == "nn":
        (M, K), N = a0.shape, b0.shape[1]
    elif mode == "nt":
        (M, K), N = a0.shape, b0.shape[0]
    else:
        (K, M), N = a0.shape, b0.shape[1]
    assert M % tm == 0 and N % tn == 0 and K % tk == 0, (name, M, N, K, tm, tn, tk)
    gk = K // tk
    npairs = len(pairs)
    has_res = res is not None

    def body(*refs):
        ins = refs[:2 * npairs]
        res_ref = refs[2 * npairs] if has_res else None
        o_ref = refs[2 * npairs + has_res]
        tot = None
        for p in range(npairs):
            a = ins[2 * p][...].astype(BF16)
            b = ins[2 * p + 1][...].astype(BF16)
            d = lax.dot_general(a, b, _DIMS[mode], preferred_element_type=F32)
            tot = d if tot is None else tot + d

        def finish(acc):
            r = acc * scale if scale != 1.0 else acc
            if has_res:
                r = res_ref[...] + r
            o_ref[...] = r.astype(o_ref.dtype)

        if gk == 1:
            finish(tot)
        else:
            acc_ref = refs[-1]
            k = pl.program_id(2)

            @pl.when(k == 0)
            def _():
                acc_ref[...] = tot

            @pl.when(k > 0)
            def _():
                acc_ref[...] += tot

            @pl.when(k == gk - 1)
            def _():
                finish(acc_ref[...])

    if mode == "nn":
        a_spec = pl.BlockSpec((tm, tk), lambda i, j, k: (i, k))
        b_spec = pl.BlockSpec((tk, tn), lambda i, j, k: (k, j))
    elif mode == "nt":
        a_spec = pl.BlockSpec((tm, tk), lambda i, j, k: (i, k))
        b_spec = pl.BlockSpec((tn, tk), lambda i, j, k: (j, k))
    else:
        a_spec = pl.BlockSpec((tk, tm), lambda i, j, k: (k, i))
        b_spec = pl.BlockSpec((tk, tn), lambda i, j, k: (k, j))
    o_spec = pl.BlockSpec((tm, tn), lambda i, j, k: (i, j))
    in_specs, args = [], []
    for a, b in pairs:
        in_specs += [a_spec, b_spec]
        args += [a, b]
    if has_res:
        in_specs.append(o_spec)
        args.append(res)
    return pl.pallas_call(
        body, name=name, grid=(M // tm, N // tn, gk),
        in_specs=in_specs, out_specs=o_spec,
        out_shape=jax.ShapeDtypeStruct((M, N), out_dtype),
        scratch_shapes=[pltpu.VMEM((tm, tn), F32)] if gk > 1 else [],
        compiler_params=_params(("parallel", "parallel", "arbitrary")),
    )(*args)


def _col_tile(n, cap):
    return n if n <= cap else _pick_tile(n, cap, 128)


def _mm_act(pairs, mode, out_dtype, name, tm, **kw):
    a0, b0 = pairs[0]
    K = a0.shape[1]
    N = b0.shape[1] if mode == "nn" else b0.shape[0]
    return _matmul(pairs, mode=mode, out_dtype=out_dtype, name=name, tm=tm,
                   tn=_col_tile(N, 1024), tk=_col_tile(K, 1408), **kw)


def _mm_wgrad(a, b, name, tr, scale=1.0):
    return _matmul([(a, b)], mode="tn", out_dtype=F32, name=name, scale=scale,
                   tm=_col_tile(a.shape[1], 1024), tn=_col_tile(b.shape[1], 1408), tk=tr)


def _rmsnorm_fwd(h, g, tm, name):
    T, D = h.shape

    def body(h_ref, g_ref, o_ref):
        x = h_ref[...]
        r = lax.rsqrt(jnp.mean(x * x, axis=-1, keepdims=True) + NORM_EPS)
        o_ref[...] = ((x * r) * g_ref[...]).astype(BF16)

    return pl.pallas_call(
        body, name=name, grid=(T // tm,),
        in_specs=[pl.BlockSpec((tm, D), lambda i: (i, 0)), pl.BlockSpec((1, D), lambda i: (0, 0))],
        out_specs=pl.BlockSpec((tm, D), lambda i: (i, 0)),
        out_shape=jax.ShapeDtypeStruct((T, D), BF16),
        compiler_params=_params(("parallel",)),
    )(h, g)


def _fold8(x):
    return jnp.sum(x.reshape(x.shape[0] // 8, 8, x.shape[1]), axis=0)


def _rmsnorm_bwd(h, g, dn, dres, tm, name):
    T, D = h.shape
    nt = T // tm

    def body(h_ref, g_ref, dn_ref, dres_ref, dh_ref, dhb_ref, dg_ref, acc_ref):
        i = pl.program_id(0)
        x = h_ref[...]
        r = lax.rsqrt(jnp.mean(x * x, axis=-1, keepdims=True) + NORM_EPS)
        xhat = x * r
        dy = dn_ref[...]
        dxhat = dy * g_ref[...]
        dx = r * (dxhat - xhat * jnp.mean(dxhat * xhat, axis=-1, keepdims=True))
        dh = dres_ref[...] + dx
        dh_ref[...] = dh
        dhb_ref[...] = dh.astype(BF16)
        part = _fold8(dy * xhat)

        @pl.when(i == 0)
        def _():
            acc_ref[...] = part

        @pl.when(i > 0)
        def _():
            acc_ref[...] += part

        @pl.when(i == nt - 1)
        def _():
            dg_ref[...] = jnp.sum(acc_ref[...], axis=0, keepdims=True)

    row = pl.BlockSpec((tm, D), lambda i: (i, 0))
    vec = pl.BlockSpec((1, D), lambda i: (0, 0))
    return pl.pallas_call(
        body, name=name, grid=(nt,),
        in_specs=[row, vec, row, row],
        out_specs=[row, row, vec],
        out_shape=[jax.ShapeDtypeStruct((T, D), F32), jax.ShapeDtypeStruct((T, D), BF16),
                   jax.ShapeDtypeStruct((1, D), F32)],
        scratch_shapes=[pltpu.VMEM((8, D), F32)],
        compiler_params=_params(("arbitrary",)),
    )(h, g, dn, dres)


def _ffn_up(n, w1, w3, tm, name):
    T, D = n.shape
    F = w1.shape[1]
    tn = _col_tile(F, 1408)

    def body(n_ref, w1_ref, w3_ref, a_ref, c_ref, s_ref):
        x = n_ref[...]
        a = jnp.dot(x, w1_ref[...], preferred_element_type=F32)
        c = jnp.dot(x, w3_ref[...], preferred_element_type=F32)
        a_ref[...] = a.astype(BF16)
        c_ref[...] = c.astype(BF16)
        s_ref[...] = (a * _sigmoid(a) * c).astype(BF16)

    w_spec = pl.BlockSpec((D, tn), lambda i, j: (0, j))
    o_spec = pl.BlockSpec((tm, tn), lambda i, j: (i, j))
    o_shape = jax.ShapeDtypeStruct((T, F), BF16)
    return pl.pallas_call(
        body, name=name, grid=(T // tm, F // tn),
        in_specs=[pl.BlockSpec((tm, D), lambda i, j: (i, 0)), w_spec, w_spec],
        out_specs=[o_spec, o_spec, o_spec], out_shape=[o_shape, o_shape, o_shape],
        compiler_params=_params(("parallel", "arbitrary")),
    )(n, w1, w3)


def _swiglu_bwd(a, c, ds, tm, name):
    T, F = a.shape
    tn = _col_tile(F, 1408)

    def body(a_ref, c_ref, ds_ref, da_ref, dc_ref):
        av = a_ref[...].astype(F32)
        cv = c_ref[...].astype(F32)
        d = ds_ref[...]
        sg = _sigmoid(av)
        da_ref[...] = (d * cv * (sg * (1.0 + av * (1.0 - sg)))).astype(BF16)
        dc_ref[...] = (d * (av * sg)).astype(BF16)

    spec = pl.BlockSpec((tm, tn), lambda i, j: (i, j))
    o_shape = jax.ShapeDtypeStruct((T, F), BF16)
    return pl.pallas_call(
        body, name=name, grid=(T // tm, F // tn),
        in_specs=[spec, spec, spec], out_specs=[spec, spec], out_shape=[o_shape, o_shape],
        compiler_params=_params(("parallel", "parallel")),
    )(a, c, ds)


def _gelu_parts(x):
    inner = GELU_C * (x + 0.044715 * (x * x * x))
    t = jnp.tanh(inner)
    return t, GELU_C * (1.0 + 3.0 * 0.044715 * (x * x))


def _gelu_fwd(y, tm, name):
    T, W = y.shape

    def body(y_ref, o_ref):
        x = y_ref[...]
        t, _ = _gelu_parts(x)
        o_ref[...] = (0.5 * x * (1.0 + t)).astype(BF16)

    spec = pl.BlockSpec((tm, W), lambda i: (i, 0))
    return pl.pallas_call(body, name=name, grid=(T // tm,), in_specs=[spec], out_specs=spec,
                          out_shape=jax.ShapeDtypeStruct((T, W), BF16),
                          compiler_params=_params(("parallel",)))(y)


def _gelu_bwd(dyg, y, tm, name):
    T, W = y.shape

    def body(d_ref, y_ref, o_ref):
        x = y_ref[...]
        t, dinner = _gelu_parts(x)
        o_ref[...] = d_ref[...] * (0.5 * (1.0 + t) + 0.5 * x * (1.0 - t * t) * dinner)

    spec = pl.BlockSpec((tm, W), lambda i: (i, 0))
    return pl.pallas_call(body, name=name, grid=(T // tm,), in_specs=[spec, spec], out_specs=spec,
                          out_shape=jax.ShapeDtypeStruct((T, W), F32),
                          compiler_params=_params(("parallel",)))(dyg, y)


def _merge_cols(D):
    cb = 512 if D % 512 == 0 else D
    return cb, D // cb


def _merge_fwd(z, attn, ga, gb, tm, name):
    T, D = attn.shape
    nz = z.shape[1] // D
    cb, nc = _merge_cols(D)

    def body(gat_ref, gss_ref, attn_ref, ga_ref, gb_ref, o_ref):
        ssm = ga_ref[...] * _sigmoid(gb_ref[...])
        o_ref[...] = (_sigmoid(gat_ref[...]) * attn_ref[...] + _sigmoid(gss_ref[...]) * ssm).astype(BF16)

    def col(block):
        return pl.BlockSpec((tm, cb), lambda i, j: (i, block * nc + j))

    return pl.pallas_call(
        body, name=name, grid=(T // tm, nc),
        in_specs=[col(nz - 2), col(nz - 1), col(0), col(0), col(0)],
        out_specs=col(0), out_shape=jax.ShapeDtypeStruct((T, D), BF16),
        compiler_params=_params(("parallel", "parallel")),
    )(z, z, attn, ga, gb)


def _merge_bwd(dm, z, attn, ga, gb, tm, name):
    T, D = attn.shape
    nz = z.shape[1] // D
    cb, nc = _merge_cols(D)

    def body(dm_ref, gat_ref, gss_ref, attn_ref, ga_ref, gb_ref, dattn_ref, dgat_ref, dgss_ref, dga_ref, dgb_ref):
        d = dm_ref[...]
        sa = _sigmoid(gat_ref[...])
        ss = _sigmoid(gss_ref[...])
        sb = _sigmoid(gb_ref[...])
        gav = ga_ref[...]
        dattn_ref[...] = d * sa
        dgat_ref[...] = (d * attn_ref[...] * (sa * (1.0 - sa))).astype(BF16)
        dgss_ref[...] = (d * (gav * sb) * (ss * (1.0 - ss))).astype(BF16)
        dssm = d * ss
        dga_ref[...] = (dssm * sb).astype(BF16)
        dgb_ref[...] = (dssm * gav * (sb * (1.0 - sb))).astype(BF16)

    def col(block):
        return pl.BlockSpec((tm, cb), lambda i, j: (i, block * nc + j))

    b16 = jax.ShapeDtypeStruct((T, D), BF16)
    return pl.pallas_call(
        body, name=name, grid=(T // tm, nc),
        in_specs=[col(0), col(nz - 2), col(nz - 1), col(0), col(0), col(0)],
        out_specs=[col(0)] * 5,
        out_shape=[jax.ShapeDtypeStruct((T, D), F32), b16, b16, b16, b16],
        compiler_params=_params(("parallel", "parallel")),
    )(dm, z, z, attn, ga, gb)


def _loss_head(h, g, target, tm, rows_per_example, name):
    T, D = h.shape
    nt = T // tm
    tiles_per_example = rows_per_example // tm

    def body(h_ref, g_ref, t_ref, dh_ref, dhb_ref, dg_ref, loss_ref, acc_g, acc_l):
        i = pl.program_id(0)
        x = h_ref[...]
        gv = g_ref[...]
        r = lax.rsqrt(jnp.mean(x * x, axis=-1, keepdims=True) + NORM_EPS)
        xhat = x * r
        pos = (i % tiles_per_example) * tm + lax.broadcasted_iota(jnp.int32, (tm, 1), 0)
        real = pos >= N_META
        err = jnp.where(real, xhat * gv - t_ref[...], 0.0)
        dy = err * (1.0 / D)
        dxhat = dy * gv
        dh = r * (dxhat - xhat * jnp.mean(dxhat * xhat, axis=-1, keepdims=True))
        dh_ref[...] = dh
        dhb_ref[...] = dh.astype(BF16)
        pg = _fold8(dy * xhat)
        pl_ = _fold8(err * err)

        @pl.when(i == 0)
        def _():
            acc_g[...] = pg
            acc_l[...] = pl_

        @pl.when(i > 0)
        def _():
            acc_g[...] += pg
            acc_l[...] += pl_

        @pl.when(i == nt - 1)
        def _():
            dg_ref[...] = jnp.sum(acc_g[...], axis=0, keepdims=True)
            loss_ref[...] = jnp.full((1, D), (0.5 / D) * jnp.sum(acc_l[...]), F32)

    row = pl.BlockSpec((tm, D), lambda i: (i, 0))
    vec = pl.BlockSpec((1, D), lambda i: (0, 0))
    return pl.pallas_call(
        body, name=name, grid=(nt,),
        in_specs=[row, vec, row], out_specs=[row, row, vec, vec],
        out_shape=[jax.ShapeDtypeStruct((T, D), F32), jax.ShapeDtypeStruct((T, D), BF16),
                   jax.ShapeDtypeStruct((1, D), F32), jax.ShapeDtypeStruct((1, D), F32)],
        scratch_shapes=[pltpu.VMEM((8, D), F32), pltpu.VMEM((8, D), F32)],
        compiler_params=_params(("arbitrary",)),
    )(h, g, target)


_NT = (((1,), (1,)), ((), ()))
_TN = (((0,), (0,)), ((), ()))


def _attn_probs(qb, k_m, k_p, k_c, sink, has_prev):
    R = qb.shape[0]
    s_m = lax.dot_general(qb, k_m, _NT, preferred_element_type=F32)
    s_p = lax.dot_general(qb, k_p, _NT, preferred_element_type=F32)
    s_c = lax.dot_general(qb, k_c, _NT, preferred_element_type=F32)
    qi = lax.broadcasted_iota(jnp.int32, (R, BLOCK), 0) & (BLOCK - 1)
    kj = lax.broadcasted_iota(jnp.int32, (R, BLOCK), 1)
    s_p = jnp.where((kj > qi) & has_prev, s_p, NEG_INF)
    s_c = jnp.where(kj <= qi, s_c, NEG_INF)
    m = jnp.maximum(jnp.maximum(jnp.max(s_m, axis=-1, keepdims=True), jnp.max(s_p, axis=-1, keepdims=True)),
                    jnp.maximum(jnp.max(s_c, axis=-1, keepdims=True), sink))
    e_m, e_p, e_c, e_s = jnp.exp(s_m - m), jnp.exp(s_p - m), jnp.exp(s_c - m), jnp.exp(sink - m)
    inv = 1.0 / (jnp.sum(e_m, axis=-1, keepdims=True) + jnp.sum(e_p, axis=-1, keepdims=True)
                 + jnp.sum(e_c, axis=-1, keepdims=True) + e_s)
    return e_m * inv, e_p * inv, e_c * inv, e_s * inv


def _meta_probs(qm, k_m, sink_m):
    R = qm.shape[0]
    s = lax.dot_general(qm, k_m, _NT, preferred_element_type=F32)
    qi = lax.broadcasted_iota(jnp.int32, (R, N_META), 0) & (N_META - 1)
    kj = lax.broadcasted_iota(jnp.int32, (R, N_META), 1)
    s = jnp.where(kj <= qi, s, NEG_INF)
    m = jnp.maximum(jnp.max(s, axis=-1, keepdims=True), sink_m)
    e, e_s = jnp.exp(s - m), jnp.exp(sink_m - m)
    inv = 1.0 / (jnp.sum(e, axis=-1, keepdims=True) + e_s)
    return e * inv, e_s * inv


def _block_start(n):
    return pl.multiple_of(N_META + n * BLOCK, ROW_ALIGN)


def _attn_fwd(q, k, v, sink_col, sink_meta, name):
    B, H, L, hd = q.shape
    KV = k.shape[1]
    nb = (L - N_META) // BLOCK
    R = Q_PER_KV * BLOCK

    def body(q_ref, k_ref, v_ref, sk_ref, skm_ref, o_ref):
        k_m = k_ref[0, 0, 0:N_META, :]
        v_m = v_ref[0, 0, 0:N_META, :]
        qm = q_ref[0, :, 0:N_META, :].reshape(Q_PER_KV * N_META, hd)
        p, _ = _meta_probs(qm, k_m, skm_ref[0])
        om = jnp.dot(p.astype(BF16), v_m, preferred_element_type=F32)
        o_ref[0, :, 0:N_META, :] = om.reshape(Q_PER_KV, N_META, hd)
        sink = sk_ref[0]

        def step(n, carry):
            cur = _block_start(n)
            prev = _block_start(jnp.maximum(n - 1, 0))
            qb = q_ref[0, :, pl.ds(cur, BLOCK), :].reshape(R, hd)
            p_m, p_p, p_c, _ = _attn_probs(qb, k_m, k_ref[0, 0, pl.ds(prev, BLOCK), :],
                                           k_ref[0, 0, pl.ds(cur, BLOCK), :], sink, n > 0)
            o = (jnp.dot(p_m.astype(BF16), v_m, preferred_element_type=F32)
                 + jnp.dot(p_p.astype(BF16), v_ref[0, 0, pl.ds(prev, BLOCK), :], preferred_element_type=F32)
                 + jnp.dot(p_c.astype(BF16), v_ref[0, 0, pl.ds(cur, BLOCK), :], preferred_element_type=F32))
            o_ref[0, :, pl.ds(cur, BLOCK), :] = o.reshape(Q_PER_KV, BLOCK, hd)
            return carry

        lax.fori_loop(0, nb, step, 0)

    q_spec = pl.BlockSpec((1, Q_PER_KV, L, hd), lambda b, h: (b, h, 0, 0))
    kv_spec = pl.BlockSpec((1, 1, L, hd), lambda b, h: (b, h, 0, 0))
    return pl.pallas_call(
        body, name=name, grid=(B, KV),
        in_specs=[q_spec, kv_spec, kv_spec,
                  pl.BlockSpec((1, R, 1), lambda b, h: (h, 0, 0)),
                  pl.BlockSpec((1, Q_PER_KV * N_META, 1), lambda b, h: (h, 0, 0))],
        out_specs=q_spec, out_shape=jax.ShapeDtypeStruct((B, H, L, hd), F32),
        compiler_params=_params(("parallel", "parallel")),
    )(q, k, v, sink_col, sink_meta)


def _attn_bwd(q, k, v, o, do, sink_col, sink_meta, name):
    B, H, L, hd = q.shape
    KV = k.shape[1]
    nb = (L - N_META) // BLOCK
    R = Q_PER_KV * BLOCK
    RM = Q_PER_KV * N_META

    def head_totals(col, rows_per_head):
        rid = lax.broadcasted_iota(jnp.int32, (8, 128), 0)
        out = jnp.zeros((8, 128), F32)
        for g in range(Q_PER_KV):
            out = out + jnp.where(rid == g, jnp.sum(col[g * rows_per_head:(g + 1) * rows_per_head, :]), 0.0)
        return out

    def body(q_ref, k_ref, v_ref, o_ref, do_ref, sk_ref, skm_ref, dq_ref, dk_ref, dv_ref, dsk_ref, acc_sink):
        b = pl.program_id(1)
        dk_ref[...] = jnp.zeros_like(dk_ref)
        dv_ref[...] = jnp.zeros_like(dv_ref)
        k_m = k_ref[0, 0, 0:N_META, :]
        v_m = v_ref[0, 0, 0:N_META, :]

        qm = q_ref[0, :, 0:N_META, :].reshape(RM, hd)
        dom = do_ref[0, :, 0:N_META, :].reshape(RM, hd)
        delta = jnp.sum(dom * o_ref[0, :, 0:N_META, :].reshape(RM, hd), axis=-1, keepdims=True)
        p, p_s = _meta_probs(qm, k_m, skm_ref[0])
        domb = dom.astype(BF16)
        dp = lax.dot_general(domb, v_m, _NT, preferred_element_type=F32)
        ds = (p * (dp - delta)).astype(BF16)
        dq_ref[0, :, 0:N_META, :] = jnp.dot(ds, k_m, preferred_element_type=F32).reshape(Q_PER_KV, N_META, hd)
        dk_ref[0, 0, 0:N_META, :] += lax.dot_general(ds, qm, _TN, preferred_element_type=F32)
        dv_ref[0, 0, 0:N_META, :] += lax.dot_general(p.astype(BF16), domb, _TN, preferred_element_type=F32)
        sink_tot = head_totals(-p_s * delta, N_META)
        sink = sk_ref[0]
        acc_sink[...] = jnp.zeros_like(acc_sink)

        def step(n, carry):
            cur = _block_start(n)
            prev = _block_start(jnp.maximum(n - 1, 0))
            qb = q_ref[0, :, pl.ds(cur, BLOCK), :].reshape(R, hd)
            k_p, k_c = k_ref[0, 0, pl.ds(prev, BLOCK), :], k_ref[0, 0, pl.ds(cur, BLOCK), :]
            v_p, v_c = v_ref[0, 0, pl.ds(prev, BLOCK), :], v_ref[0, 0, pl.ds(cur, BLOCK), :]
            p_m, p_p, p_c, p_s = _attn_probs(qb, k_m, k_p, k_c, sink, n > 0)
            dob = do_ref[0, :, pl.ds(cur, BLOCK), :].reshape(R, hd)
            delta = jnp.sum(dob * o_ref[0, :, pl.ds(cur, BLOCK), :].reshape(R, hd), axis=-1, keepdims=True)
            dobb = dob.astype(BF16)
            ds_m = (p_m * (lax.dot_general(dobb, v_m, _NT, preferred_element_type=F32) - delta)).astype(BF16)
            ds_p = (p_p * (lax.dot_general(dobb, v_p, _NT, preferred_element_type=F32) - delta)).astype(BF16)
            ds_c = (p_c * (lax.dot_general(dobb, v_c, _NT, preferred_element_type=F32) - delta)).astype(BF16)
            dq = (jnp.dot(ds_m, k_m, preferred_element_type=F32) + jnp.dot(ds_p, k_p, preferred_element_type=F32)
                  + jnp.dot(ds_c, k_c, preferred_element_type=F32))
            dq_ref[0, :, pl.ds(cur, BLOCK), :] = dq.reshape(Q_PER_KV, BLOCK, hd)
            dk_ref[0, 0, 0:N_META, :] += lax.dot_general(ds_m, qb, _TN, preferred_element_type=F32)
            dk_ref[0, 0, pl.ds(prev, BLOCK), :] += lax.dot_general(ds_p, qb, _TN, preferred_element_type=F32)
            dk_ref[0, 0, pl.ds(cur, BLOCK), :] += lax.dot_general(ds_c, qb, _TN, preferred_element_type=F32)
            dv_ref[0, 0, 0:N_META, :] += lax.dot_general(p_m.astype(BF16), dobb, _TN, preferred_element_type=F32)
            dv_ref[0, 0, pl.ds(prev, BLOCK), :] += lax.dot_general(p_p.astype(BF16), dobb, _TN, preferred_element_type=F32)
            dv_ref[0, 0, pl.ds(cur, BLOCK), :] += lax.dot_general(p_c.astype(BF16), dobb, _TN, preferred_element_type=F32)
            acc_sink[...] += -p_s * delta
            return carry

        lax.fori_loop(0, nb, step, 0)
        tot = sink_tot + head_totals(acc_sink[...], BLOCK)

        @pl.when(b == 0)
        def _():
            dsk_ref[0] = tot

        @pl.when(b > 0)
        def _():
            dsk_ref[0] += tot

    q_spec = pl.BlockSpec((1, Q_PER_KV, L, hd), lambda h, b: (b, h, 0, 0))
    kv_spec = pl.BlockSpec((1, 1, L, hd), lambda h, b: (b, h, 0, 0))
    return pl.pallas_call(
        body, name=name, grid=(KV, B),
        in_specs=[q_spec, kv_spec, kv_spec, q_spec, q_spec,
                  pl.BlockSpec((1, R, 1), lambda h, b: (h, 0, 0)),
                  pl.BlockSpec((1, RM, 1), lambda h, b: (h, 0, 0))],
        out_specs=[q_spec, kv_spec, kv_spec, pl.BlockSpec((1, 8, 128), lambda h, b: (h, 0, 0))],
        out_shape=[jax.ShapeDtypeStruct((B, H, L, hd), F32), jax.ShapeDtypeStruct((B, KV, L, hd), F32),
                   jax.ShapeDtypeStruct((B, KV, L, hd), F32), jax.ShapeDtypeStruct((KV, 8, 128), F32)],
        scratch_shapes=[pltpu.VMEM((R, 1), F32)],
        compiler_params=_params(("parallel", "arbitrary")),
    )(q, k, v, o, do, sink_col, sink_meta)


def _cmul_add(acc_r, acc_i, lr, li, xr, xi):
    return acc_r + (lr * xr - li * xi), acc_i + (lr * xi + li * xr)


def _ssm_fwd(u, bmat, cmat, dskip, tables, nbatch, rc, name):
    T, W = u.shape
    ncol = W // SSM_LANES
    nch = T // rc
    S = STATE_LANES
    assert nbatch == 4

    def body(u_ref, b_ref, c_ref, d_ref, tab_ref, y_ref, xs_ref, xp_ref, st_ref, carry_ref):
        ch = pl.program_id(1)

        @pl.when(ch == 0)
        def _():
            carry_ref[...] = jnp.zeros_like(carry_ref)

        uv = u_ref[...]
        st_ref[...] = jnp.dot(uv.astype(BF16), b_ref[0], preferred_element_type=F32)
        tab = tab_ref[0]
        la_r, la_i, lb_r, lb_i = tab[:, 0:S], tab[:, S:2 * S], tab[:, 2 * S:3 * S], tab[:, 3 * S:4 * S]
        low = lax.broadcasted_iota(jnp.int32, (8, S), 0) < nbatch

        def tile(r0, c_r, c_i):
            v_r = st_ref[pl.ds(r0, 8), 0:S]
            v_i = st_ref[pl.ds(r0, 8), S:2 * S]
            v_r, v_i = _cmul_add(v_r, v_i, la_r, la_i, pltpu.roll(v_r, nbatch, 0), pltpu.roll(v_i, nbatch, 0))
            rc_r, rc_i = pltpu.roll(c_r, nbatch, 0), pltpu.roll(c_i, nbatch, 0)
            cb_r, cb_i = jnp.where(low, rc_r, c_r), jnp.where(low, rc_i, c_i)
            v_r, v_i = _cmul_add(v_r, v_i, lb_r, lb_i, cb_r, cb_i)
            st_ref[pl.ds(r0, 8), 0:S] = v_r
            st_ref[pl.ds(r0, 8), S:2 * S] = v_i
            p_r = jnp.where(low, rc_r, pltpu.roll(v_r, nbatch, 0))
            p_i = jnp.where(low, rc_i, pltpu.roll(v_i, nbatch, 0))
            return v_r, v_i, p_r, p_i

        def step(i, carry):
            c_r, c_i = carry
            r0 = pl.multiple_of(i * 16, 16)
            a_r, a_i, pa_r, pa_i = tile(r0, c_r, c_i)
            b_r, b_i, pb_r, pb_i = tile(r0 + 8, a_r, a_i)
            xp_ref[pl.ds(r0, 16), 0:S] = jnp.concatenate([pa_r, pb_r], axis=0).astype(BF16)
            xp_ref[pl.ds(r0, 16), S:2 * S] = jnp.concatenate([pa_i, pb_i], axis=0).astype(BF16)
            return b_r, b_i

        c_r, c_i = lax.fori_loop(0, rc // 16, step, (carry_ref[:, 0:S], carry_ref[:, S:2 * S]))
        carry_ref[:, 0:S] = c_r
        carry_ref[:, S:2 * S] = c_i
        xb = st_ref[...].astype(BF16)
        xs_ref[...] = xb
        y_ref[...] = jnp.dot(xb, c_ref[0], preferred_element_type=F32) + d_ref[...] * uv

    return pl.pallas_call(
        body, name=name, grid=(ncol, nch),
        in_specs=[pl.BlockSpec((rc, SSM_LANES), lambda g, c: (c, g)),
                  pl.BlockSpec((1, SSM_LANES, 2 * S), lambda g, c: (g, 0, 0)),
                  pl.BlockSpec((1, 2 * S, SSM_LANES), lambda g, c: (g, 0, 0)),
                  pl.BlockSpec((1, SSM_LANES), lambda g, c: (0, g)),
                  pl.BlockSpec((1, 8, 4 * S), lambda g, c: (g, 0, 0))],
        out_specs=[pl.BlockSpec((rc, SSM_LANES), lambda g, c: (c, g)),
                   pl.BlockSpec((rc, 2 * S), lambda g, c: (c, g)),
                   pl.BlockSpec((rc, 2 * S), lambda g, c: (c, g))],
        out_shape=[jax.ShapeDtypeStruct((T, W), F32), jax.ShapeDtypeStruct((T, ncol * 2 * S), BF16),
                   jax.ShapeDtypeStruct((T, ncol * 2 * S), BF16)],
        scratch_shapes=[pltpu.VMEM((rc, 2 * S), F32), pltpu.VMEM((8, 2 * S), F32)],
        compiler_params=_params(("parallel", "arbitrary")),
    )(u, bmat, cmat, dskip, tables)


def _ssm_bwd(dy, u, xs, xp, bmat, cmat, dskip, tables, nbatch, rc, name):
    T, W = u.shape
    ncol = W // SSM_LANES
    nch = T // rc
    S = STATE_LANES
    ntile = rc // 16

    def body(dy_ref, u_ref, xs_ref, xp_ref, b_ref, c_ref, d_ref, tab_ref,
             du_ref, db_ref, dc_ref, dl_ref, dd_ref, st_ref, carry_ref, accl_ref, accd_ref):
        ch = pl.program_id(1)

        @pl.when(ch == 0)
        def _():
            carry_ref[...] = jnp.zeros_like(carry_ref)
            accl_ref[...] = jnp.zeros_like(accl_ref)
            accd_ref[...] = jnp.zeros_like(accd_ref)
            db_ref[...] = jnp.zeros_like(db_ref)
            dc_ref[...] = jnp.zeros_like(dc_ref)

        dyv = dy_ref[...]
        uv = u_ref[...]
        dyb = dyv.astype(BF16)
        st_ref[...] = lax.dot_general(dyb, c_ref[0], _NT, preferred_element_type=F32)
        tab = tab_ref[0]
        la_r, la_i, lb_r, lb_i = tab[:, 0:S], tab[:, S:2 * S], tab[:, 2 * S:3 * S], tab[:, 3 * S:4 * S]
        low = lax.broadcasted_iota(jnp.int32, (8, S), 0) < nbatch

        def tile(r0, p_r, p_i, c_r, c_i, al_r, al_i):
            v_r = st_ref[pl.ds(r0, 8), 0:S]
            v_i = st_ref[pl.ds(r0, 8), S:2 * S]
            v_r, v_i = _cmul_add(v_r, v_i, la_r, la_i, pltpu.roll(v_r, nbatch, 0), pltpu.roll(v_i, nbatch, 0))
            cb_r = jnp.where(low, c_r, pltpu.roll(c_r, nbatch, 0))
            cb_i = jnp.where(low, c_i, pltpu.roll(c_i, nbatch, 0))
            v_r, v_i = _cmul_add(v_r, v_i, lb_r, lb_i, cb_r, cb_i)
            st_ref[pl.ds(r0, 8), 0:S] = v_r
            st_ref[pl.ds(r0, 8), S:2 * S] = v_i
            al_r = al_r + (v_r * p_r + v_i * p_i)
            al_i = al_i + (v_i * p_r - v_r * p_i)
            return v_r, v_i, al_r, al_i

        def step(j, carry):
            c_r, c_i, al_r, al_i = carry
            r0 = pl.multiple_of((ntile - 1 - j) * 16, 16)
            p_r = xp_ref[pl.ds(r0, 16), 0:S].astype(F32)
            p_i = xp_ref[pl.ds(r0, 16), S:2 * S].astype(F32)
            c_r, c_i, al_r, al_i = tile(r0 + 8, p_r[8:16], p_i[8:16], c_r, c_i, al_r, al_i)
            return tile(r0, p_r[0:8], p_i[0:8], c_r, c_i, al_r, al_i)

        c_r, c_i, al_r, al_i = lax.fori_loop(
            0, ntile, step,
            (carry_ref[:, 0:S], carry_ref[:, S:2 * S], accl_ref[:, 0:S], accl_ref[:, S:2 * S]))
        carry_ref[:, 0:S] = c_r
        carry_ref[:, S:2 * S] = c_i
        accl_ref[:, 0:S] = al_r
        accl_ref[:, S:2 * S] = al_i
        dsb = st_ref[...].astype(BF16)
        du_ref[...] = lax.dot_general(dsb, b_ref[0], _NT, preferred_element_type=F32) + d_ref[...] * dyv
        db_ref[0] += lax.dot_general(uv.astype(BF16), dsb, _TN, preferred_element_type=F32)
        dc_ref[0] += lax.dot_general(xs_ref[...], dyb, _TN, preferred_element_type=F32)
        accd_ref[...] += _fold8(dyv * uv)

        @pl.when(ch == nch - 1)
        def _():
            dl_ref[0] = jnp.sum(accl_ref[...], axis=0, keepdims=True)
            dd_ref[...] = jnp.sum(accd_ref[...], axis=0, keepdims=True)

    rev = lambda g, c: (nch - 1 - c, g)
    return pl.pallas_call(
        body, name=name, grid=(ncol, nch),
        in_specs=[pl.BlockSpec((rc, SSM_LANES), rev), pl.BlockSpec((rc, SSM_LANES), rev),
                  pl.BlockSpec((rc, 2 * S), rev), pl.BlockSpec((rc, 2 * S), rev),
                  pl.BlockSpec((1, SSM_LANES, 2 * S), lambda g, c: (g, 0, 0)),
                  pl.BlockSpec((1, 2 * S, SSM_LANES), lambda g, c: (g, 0, 0)),
                  pl.BlockSpec((1, SSM_LANES), lambda g, c: (0, g)),
                  pl.BlockSpec((1, 8, 4 * S), lambda g, c: (g, 0, 0))],
        out_specs=[pl.BlockSpec((rc, SSM_LANES), rev),
                   pl.BlockSpec((1, SSM_LANES, 2 * S), lambda g, c: (g, 0, 0)),
                   pl.BlockSpec((1, 2 * S, SSM_LANES), lambda g, c: (g, 0, 0)),
                   pl.BlockSpec((1, 1, 2 * S), lambda g, c: (g, 0, 0)),
                   pl.BlockSpec((1, SSM_LANES), lambda g, c: (0, g))],
        out_shape=[jax.ShapeDtypeStruct((T, W), F32),
                   jax.ShapeDtypeStruct((ncol, SSM_LANES, 2 * S), F32),
                   jax.ShapeDtypeStruct((ncol, 2 * S, SSM_LANES), F32),
                   jax.ShapeDtypeStruct((ncol, 1, 2 * S), F32),
                   jax.ShapeDtypeStruct((1, W), F32)],
        scratch_shapes=[pltpu.VMEM((rc, 2 * S), F32), pltpu.VMEM((8, 2 * S), F32),
                        pltpu.VMEM((8, 2 * S), F32), pltpu.VMEM((8, SSM_LANES), F32)],
        compiler_params=_params(("parallel", "arbitrary")),
    )(dy, u, xs, xp, bmat, cmat, dskip, tables)


def _ssm_matrices(a_re, a_im, log_step, b_re, b_im, c_re, c_im):
    G, N = a_re.shape
    ncol = G // GROUPS_PER_COL
    step = jnp.exp(log_step)[:, None]
    mag = jnp.exp(a_re * step)
    ang = a_im * step
    lam_re, lam_im = mag * jnp.cos(ang), mag * jnp.sin(ang)
    den = a_re * a_re + a_im * a_im
    nr, ni = lam_re - 1.0, lam_im
    coef_re = (nr * a_re + ni * a_im) / den
    coef_im = (ni * a_re - nr * a_im) / den
    bb_re = coef_re[..., None] * b_re - coef_im[..., None] * b_im
    bb_im = coef_re[..., None] * b_im + coef_im[..., None] * b_re
    eye = jnp.eye(GROUPS_PER_COL, dtype=F32)
    bb = jnp.stack([bb_re, bb_im]).reshape(2, ncol, GROUPS_PER_COL, N, SSM_GROUP)
    bmat = jnp.einsum("pbgnc,gh->bgcphn", bb, eye).reshape(ncol, SSM_LANES, 2 * STATE_LANES)
    cc = jnp.stack([c_re, -c_im]).reshape(2, ncol, GROUPS_PER_COL, SSM_GROUP, N)
    cmat = jnp.einsum("pbgcn,gh->bpgnhc", cc, eye).reshape(ncol, 2 * STATE_LANES, SSM_LANES)
    lam = jnp.concatenate([lam_re.reshape(ncol, STATE_LANES), lam_im.reshape(ncol, STATE_LANES)], axis=-1)
    return lam, bmat, cmat


def _scan_tables(lam, nbatch, conj):
    S = STATE_LANES
    lr, li = lam[:, None, 0:S], lam[:, None, S:2 * S]
    if conj:
        li = -li
    l2r, l2i = lr * lr - li * li, 2.0 * lr * li
    first = (jnp.arange(8) < nbatch)[None, :, None]
    zero = jnp.zeros_like(lr)
    if conj:
        parts = [jnp.where(first, lr, zero), jnp.where(first, li, zero), jnp.where(first, l2r, lr), jnp.where(first, l2i, li)]
    else:
        parts = [jnp.where(first, zero, lr), jnp.where(first, zero, li), jnp.where(first, lr, l2r), jnp.where(first, li, l2i)]
    return jnp.concatenate([jnp.broadcast_to(p, (lam.shape[0], 8, S)) for p in parts], axis=-1)


def _adamw(w, g, m, v, name):
    R, C = w.shape
    tr = R if R <= 512 else _pick_tile(R, 512, 8)

    def body(w_ref, g_ref, m_ref, v_ref, d_ref, nm_ref, nv_ref):
        gv = g_ref[...]
        mn = ADAM_B1 * m_ref[...] + (1.0 - ADAM_B1) * gv
        vn = ADAM_B2 * v_ref[...] + (1.0 - ADAM_B2) * (gv * gv)
        m_hat = mn / (1.0 - ADAM_B1 ** ADAM_STEP)
        v_hat = vn / (1.0 - ADAM_B2 ** ADAM_STEP)
        d_ref[...] = -ADAM_LR * (m_hat / (jnp.sqrt(v_hat) + ADAM_EPS) + ADAM_WD * w_ref[...])
        nm_ref[...] = mn
        nv_ref[...] = vn

    spec = pl.BlockSpec((tr, C), lambda i: (i, 0))
    shp = jax.ShapeDtypeStruct((R, C), F32)
    return pl.pallas_call(body, name=name, grid=(R // tr,), in_specs=[spec] * 4, out_specs=[spec] * 3,
                          out_shape=[shp, shp, shp], compiler_params=_params(("parallel",)))(w, g, m, v)


_ANY = pl.BlockSpec(memory_space=pl.ANY)


def _place():
    x, y, c = lax.axis_index("x"), lax.axis_index("y"), lax.axis_index("c")
    chips = [(1 - x, y), (x, 1 - y), (1 - x, 1 - y)]
    return x, y, c, chips


def _remote(src, dst, send_sems, recv_sems, k, to):
    return pltpu.make_async_remote_copy(src_ref=src, dst_ref=dst, send_sem=send_sems.at[k], recv_sem=recv_sems.at[k],
                                        device_id=to, device_id_type=MESH_IDS)


def _gather_weights(packed):
    R, W = packed.shape
    H = R // 2

    def body(src, out, send_sems, recv_sems, local_sem):
        x, y, c, chips = _place()
        sibling = (x, y, 1 - c)

        def piece(px, py, pc):
            return out.at[2 * px + py, pl.ds(pc * H, H), :]

        mine = pltpu.make_async_copy(src, out.at[2 * x + y], local_sem)
        mine.start()
        first = [_remote(src.at[pl.ds(c * H, H), :], piece(x, y, c), send_sems, recv_sems, j, (*chip, c))
                 for j, chip in enumerate(chips)]
        for cp in first:
            cp.start()
        passed = [_remote(piece(*chip, c), piece(*chip, c), send_sems, recv_sems, 3 + j, sibling)
                  for j, chip in enumerate(chips)]
        for j, chip in enumerate(chips):
            _remote(piece(*chip, c), piece(*chip, c), send_sems, recv_sems, j, (*chip, c)).wait_recv()
            passed[j].start()
        for j, chip in enumerate(chips):
            _remote(piece(*chip, 1 - c), piece(*chip, 1 - c), send_sems, recv_sems, 3 + j, sibling).wait_recv()
        for cp in first + passed:
            cp.wait_send()
        mine.wait()

    return pl.pallas_call(
        body, name="gather_weights", in_specs=[_ANY], out_specs=_ANY,
        out_shape=jax.ShapeDtypeStruct((4, R, W), packed.dtype),
        scratch_shapes=[pltpu.SemaphoreType.DMA((6,)), pltpu.SemaphoreType.DMA((6,)), pltpu.SemaphoreType.DMA],
        compiler_params=pltpu.CompilerParams(has_side_effects=True),
    )(packed)


def _swap_halves(g16):
    _, R, W = g16.shape
    H = R // 2

    def body(src, out, send_sems, recv_sems):
        x, y, c, _ = _place()
        cp = _remote(src.at[:, pl.ds((1 - c) * H, H), :], out, send_sems, recv_sems, 0, (x, y, 1 - c))
        cp.start()
        cp.wait()

    return pl.pallas_call(
        body, name="grad_swap_halves", in_specs=[_ANY], out_specs=_ANY,
        out_shape=jax.ShapeDtypeStruct((4, H, W), g16.dtype),
        scratch_shapes=[pltpu.SemaphoreType.DMA((1,)), pltpu.SemaphoreType.DMA((1,))],
        compiler_params=pltpu.CompilerParams(has_side_effects=True),
    )(g16)


def _exchange_chips(p16):
    _, H, W = p16.shape

    def body(src, out, send_sems, recv_sems):
        x, y, c, chips = _place()
        cps = [_remote(src.at[2 * chip[0] + chip[1]], out.at[j], send_sems, recv_sems, j, (*chip, c))
               for j, chip in enumerate(chips)]
        for cp in cps:
            cp.start()
        for cp in cps:
            cp.wait()

    return pl.pallas_call(
        body, name="grad_exchange_chips", in_specs=[_ANY], out_specs=_ANY,
        out_shape=jax.ShapeDtypeStruct((3, H, W), p16.dtype),
        scratch_shapes=[pltpu.SemaphoreType.DMA((3,)), pltpu.SemaphoreType.DMA((3,))],
        compiler_params=pltpu.CompilerParams(has_side_effects=True),
    )(p16)


def _join_halves(f_half):
    H, W = f_half.shape

    def body(src, out, send_sems, recv_sems, local_sem):
        x, y, c, _ = _place()
        rows = out.at[pl.ds(c * H, H), :]
        mine = pltpu.make_async_copy(src, rows, local_sem)
        mine.start()
        cp = _remote(src, rows, send_sems, recv_sems, 0, (x, y, 1 - c))
        cp.start()
        _remote(src, out.at[pl.ds((1 - c) * H, H), :], send_sems, recv_sems, 0, (x, y, 1 - c)).wait_recv()
        cp.wait_send()
        mine.wait()

    return pl.pallas_call(
        body, name="grad_join_halves", in_specs=[_ANY], out_specs=_ANY,
        out_shape=jax.ShapeDtypeStruct((2 * H, W), f_half.dtype),
        scratch_shapes=[pltpu.SemaphoreType.DMA((1,)), pltpu.SemaphoreType.DMA((1,)), pltpu.SemaphoreType.DMA],
        compiler_params=pltpu.CompilerParams(has_side_effects=True),
    )(f_half)


def _sum_halves(g32, r1, c_idx, name):
    _, R, W = g32.shape
    H = R // 2
    tr = _pick_tile(H, 512, ROW_ALIGN)
    nblk = H // tr

    def body(c_ref, g_ref, r_ref, p32_ref, p16_ref):
        s = g_ref[...] + r_ref[...].astype(F32)
        p32_ref[...] = s
        p16_ref[...] = s.astype(BF16)

    half = pl.BlockSpec((1, tr, W), lambda s, i, c_ref: (s, c_ref[0] * nblk + i, 0))
    plain = pl.BlockSpec((1, tr, W), lambda s, i, c_ref: (s, i, 0))
    return pl.pallas_call(
        body, name=name,
        grid_spec=pltpu.PrefetchScalarGridSpec(num_scalar_prefetch=1, grid=(4, nblk), in_specs=[half, plain],
                                               out_specs=[plain, plain]),
        out_shape=[jax.ShapeDtypeStruct((4, H, W), F32), jax.ShapeDtypeStruct((4, H, W), BF16)],
        compiler_params=_params(("parallel", "parallel")),
    )(c_idx, g32, r1)


def _sum_chips(p32, r2, slot_idx, name):
    _, H, W = p32.shape
    tr = _pick_tile(H, 512, ROW_ALIGN)

    def body(s_ref, p_ref, r_ref, o_ref):
        o_ref[...] = ((p_ref[0] + r_ref[0].astype(F32)) + r_ref[1].astype(F32)) + r_ref[2].astype(F32)

    return pl.pallas_call(
        body, name=name,
        grid_spec=pltpu.PrefetchScalarGridSpec(
            num_scalar_prefetch=1, grid=(H // tr,),
            in_specs=[pl.BlockSpec((1, tr, W), lambda i, s_ref: (s_ref[0], i, 0)),
                      pl.BlockSpec((3, tr, W), lambda i, s_ref: (0, i, 0))],
            out_specs=pl.BlockSpec((tr, W), lambda i, s_ref: (i, 0))),
        out_shape=jax.ShapeDtypeStruct((H, W), F32),
        compiler_params=_params(("parallel",)),
    )(slot_idx, p32, r2)


def _all_reduce_small(v, n_fold, fold_rows, fold_at):
    M, N = v.shape

    def body(x_ref, tot_ref, fold_ref, all_ref, send_sems, recv_sems, local_sem):
        x, y, c, chips = _place()
        me, sibling = (x, y, c), (x, y, 1 - c)

        def rows(px, py, pc):
            return all_ref.at[pl.ds((4 * px + 2 * py + pc) * M, M), :]

        def copy(k, block, to, src=None):
            return _remote(rows(*block) if src is None else src, rows(*block), send_sems, recv_sems, k, to)

        mine = pltpu.make_async_copy(x_ref, rows(*me), local_sem)
        mine.start()
        first = [copy(0, me, sibling, src=x_ref)]
        first += [copy(1 + j, me, (*chip, c), src=x_ref) for j, chip in enumerate(chips)]
        for cp in first:
            cp.start()
        passed = [copy(4 + j, (*chip, c), sibling) for j, chip in enumerate(chips)]
        for j, chip in enumerate(chips):
            copy(1 + j, (*chip, c), me).wait_recv()
            passed[j].start()
        copy(0, sibling, me).wait_recv()
        for j, chip in enumerate(chips):
            copy(4 + j, (*chip, 1 - c), me).wait_recv()
        for cp in first + passed:
            cp.wait_send()
        mine.wait()
        tot = all_ref[0:M, :]
        for d in range(1, 8):
            tot = tot + all_ref[d * M:(d + 1) * M, :]
        tot_ref[...] = tot
        f = tot[fold_at:fold_at + fold_rows, :]
        for e in range(1, n_fold):
            f = f + tot[fold_at + e * fold_rows:fold_at + (e + 1) * fold_rows, :]
        fold_ref[...] = f

    vm = pl.BlockSpec(memory_space=pltpu.VMEM)
    return pl.pallas_call(
        body, name="all_reduce_small", in_specs=[vm], out_specs=[vm, vm],
        out_shape=[jax.ShapeDtypeStruct((M, N), F32), jax.ShapeDtypeStruct((fold_rows, N), F32)],
        scratch_shapes=[pltpu.VMEM((8 * M, N), F32), pltpu.SemaphoreType.DMA((7,)), pltpu.SemaphoreType.DMA((7,)),
                        pltpu.SemaphoreType.DMA],
        compiler_params=pltpu.CompilerParams(has_side_effects=True, vmem_limit_bytes=VMEM_LIMIT),
    )(v)


def _pad_rows(a2d, mult):
    pad = (-a2d.shape[0]) % mult
    return a2d if pad == 0 else jnp.concatenate([a2d, jnp.zeros((pad, a2d.shape[1]), a2d.dtype)], axis=0)


def _as_rows(a, width, mult):
    flat = a.reshape(-1)
    pad = (-flat.shape[0]) % width
    if pad:
        flat = jnp.concatenate([flat, jnp.zeros((pad,), flat.dtype)])
    return _pad_rows(flat.reshape(-1, width), mult)


class _Layout:
    def __init__(self, width, mult, total_mult):
        self.width, self.mult, self.total_mult = width, mult, total_mult
        self.offsets, self.shapes, self.rows = {}, {}, 0

    def add(self, name, shape):
        n = math.prod(shape)
        r = -(-n // self.width)
        r = -(-r // self.mult) * self.mult
        self.offsets[name], self.shapes[name] = (self.rows, r), tuple(shape)
        self.rows += r

    def align(self, mult):
        gap = (-self.rows) % mult
        if gap:
            self.offsets[f"_gap{self.rows}"], self.shapes[f"_gap{self.rows}"] = (self.rows, gap), (gap, self.width)
            self.rows += gap
        return self.rows

    @property
    def total(self):
        return -(-self.rows // self.total_mult) * self.total_mult

    def pack(self, pieces, dtype):
        parts = [_as_rows(pieces[n].astype(dtype), self.width, self.mult) if n in pieces
                 else jnp.zeros(self.shapes[n], dtype) for n in self.offsets]
        if self.total > self.rows:
            parts.append(jnp.zeros((self.total - self.rows, self.width), dtype))
        return jnp.concatenate(parts, axis=0)

    def unpack(self, buf, name):
        off, r = self.offsets[name]
        shape = self.shapes[name]
        return buf[off:off + r].reshape(-1)[:math.prod(shape)].reshape(shape)


_BIG = ["ffn1_w1", "ffn1_w3", "ffn1_w2", "w_in", "ssm_glu_a", "ssm_glu_b", "w_out", "ffn2_w1", "ffn2_w3", "ffn2_w2"]
_COL_SHARDED = {"ffn1_w1", "ffn1_w3", "w_in", "ssm_glu_a", "ssm_glu_b", "ffn2_w1", "ffn2_w3"}
_SMALL = ["ffn1_norm", "mix_norm", "ffn2_norm", "final_norm", "attn_sinks", "ssm_a_re", "ssm_a_im", "ssm_log_step",
          "ssm_b_re", "ssm_b_im", "ssm_c_re", "ssm_c_im", "ssm_d"]
_WEIGHTS = ["meta_tokens", "ffn1_norm", "ffn1_w1", "ffn1_w3", "ffn1_w2", "mix_norm", "w_in", "attn_sinks", "ssm_a_re",
            "ssm_a_im", "ssm_log_step", "ssm_b_re", "ssm_b_im", "ssm_c_re", "ssm_c_im", "ssm_d", "ssm_glu_a",
            "ssm_glu_b", "w_out", "ffn2_norm", "ffn2_w1", "ffn2_w3", "ffn2_w2", "final_norm"]


def _full_from_slots(slots, name):
    if name in _COL_SHARDED:
        return jnp.transpose(slots, (1, 0, 2)).reshape(slots.shape[1], 4 * slots.shape[2])
    return slots.reshape(4 * slots.shape[1], slots.shape[2])


def _slots_from_full(full, name):
    if name in _COL_SHARDED:
        r, c = full.shape
        return jnp.transpose(full.reshape(r, 4, c // 4), (1, 0, 2))
    return full.reshape(4, full.shape[0] // 4, full.shape[1])


def _step(x, target, w, m, v):
    B, S, D = x.shape
    L = S + N_META
    T = B * L
    H = D // HEAD_DIM
    KV = H // Q_PER_KV
    KVW = KV * HEAD_DIM
    SW = D // 2
    tm = _pick_tile(L, ROW_TILE_CAP, ROW_ALIGN)
    rc = _pick_tile(L, ROW_TILE_CAP // B, 4) * B
    my_c = lax.axis_index("c")
    my_slot = 2 * lax.axis_index("x") + lax.axis_index("y")

    big = {n: w[n][0] for n in _BIG}
    lay = _Layout(PACK_W, ROW_ALIGN, 2 * ROW_ALIGN)
    for n in _BIG:
        lay.add(n, big[n].shape)
    meta_bits = lax.bitcast_convert_type(w["meta_tokens"], BF16)
    lay.add("meta_bits", meta_bits.shape)
    packed = lay.pack({**big, "meta_bits": meta_bits}, BF16)
    gathered = _gather_weights(packed)
    fullw = {}
    for n in _BIG:
        slots = jnp.stack([lay.unpack(gathered[s], n) for s in range(4)])
        fullw[n] = _full_from_slots(slots, n)
    meta_slots = jnp.stack([lax.bitcast_convert_type(lay.unpack(gathered[s], "meta_bits"), F32) for s in range(4)])
    meta = _full_from_slots(meta_slots, "ssm_glu_a")

    g_ffn1, g_mix, g_ffn2 = w["ffn1_norm"], w["mix_norm"], w["ffn2_norm"]
    g_final = w["final_norm"].reshape(1, D)

    h0 = jnp.concatenate([jnp.broadcast_to(meta[None], (B, N_META, D)), x], axis=1).reshape(T, D)

    def ffn_fwd(h, g, w1, w3, w2, tag):
        n = _rmsnorm_fwd(h, g, tm, f"{tag}_norm")
        a, c, s = _ffn_up(n, w1, w3, tm, f"{tag}_up")
        hout = _mm_act([(s, w2)], "nn", F32, f"{tag}_down", tm, scale=0.5, res=h)
        return hout, (n, a, c, s)

    h1, saved1 = ffn_fwd(h0, g_ffn1, fullw["ffn1_w1"], fullw["ffn1_w3"], fullw["ffn1_w2"], "ffn1")
    hn = _rmsnorm_fwd(h1, g_mix, tm, "mix_norm")
    z = _mm_act([(hn, fullw["w_in"])], "nn", F32, "w_in", tm)

    def heads(a2d, nh, scale=None):
        a = a2d if scale is None else a2d * scale
        return jnp.transpose(a.astype(BF16).reshape(B, L, nh, HEAD_DIM), (0, 2, 1, 3))

    q = heads(z[:, 0:D], H, HEAD_DIM ** -0.5)
    k = heads(z[:, D:D + KVW], KV)
    vv = heads(z[:, D + KVW:D + 2 * KVW], KV)
    sinks = w["attn_sinks"].reshape(KV, Q_PER_KV, 1, 1)
    sink_col = jnp.broadcast_to(sinks, (KV, Q_PER_KV, BLOCK, 1)).reshape(KV, Q_PER_KV * BLOCK, 1)
    sink_meta = jnp.broadcast_to(sinks, (KV, Q_PER_KV, N_META, 1)).reshape(KV, Q_PER_KV * N_META, 1)
    o = _attn_fwd(q, k, vv, sink_col, sink_meta, "attn_fwd")
    attn = jnp.transpose(o, (0, 2, 1, 3)).reshape(T, D)

    def to_time_major(a2d):
        return jnp.transpose(a2d.reshape(B, L, a2d.shape[-1]), (1, 0, 2)).reshape(T, a2d.shape[-1])

    def to_batch_major(a2d):
        return jnp.transpose(a2d.reshape(L, B, a2d.shape[-1]), (1, 0, 2)).reshape(T, a2d.shape[-1])

    ssm_args = (w["ssm_a_re"][0], w["ssm_a_im"][0], w["ssm_log_step"][0], w["ssm_b_re"][0], w["ssm_b_im"][0],
                w["ssm_c_re"][0], w["ssm_c_im"][0])
    (lam, bmat, cmat), ssm_vjp = jax.vjp(_ssm_matrices, *ssm_args)
    bmat16, cmat16 = bmat.astype(BF16), cmat.astype(BF16)
    u_t = to_time_major(z[:, D + 2 * KVW:D + 2 * KVW + SW])
    y_t, xs, xp = _ssm_fwd(u_t, bmat16, cmat16, w["ssm_d"], _scan_tables(lam, B, False), B, rc, "ssm_fwd")
    y0 = to_batch_major(y_t)
    yg = _gelu_fwd(y0, tm, "gelu_fwd")
    ga = _mm_act([(yg, fullw["ssm_glu_a"])], "nn", F32, "glu_a", tm)
    gb = _mm_act([(yg, fullw["ssm_glu_b"])], "nn", F32, "glu_b", tm)
    merged = _merge_fwd(z, attn, ga, gb, tm, "merge_fwd")
    h2 = _mm_act([(merged, fullw["w_out"])], "nn", F32, "w_out", tm, res=h1)
    h3, saved2 = ffn_fwd(h2, g_ffn2, fullw["ffn2_w1"], fullw["ffn2_w3"], fullw["ffn2_w2"], "ffn2")

    target_p = jnp.concatenate([jnp.zeros((B, N_META, D), F32), target], axis=1).reshape(T, D)
    dh3, dh3b, dg_final, loss_row = _loss_head(h3, g_final, target_p, tm, L, "loss_head")

    grads = {}

    def ffn_bwd(h, g, w1, w3, w2, saved, dh, dhb, tag):
        n, a, c, s = saved
        grads[f"{tag}_w2"] = _mm_wgrad(s, dhb, f"{tag}_dw2", tm, scale=0.5)
        ds = _mm_act([(dhb, w2)], "nt", F32, f"{tag}_ds", tm, scale=0.5)
        da, dc = _swiglu_bwd(a, c, ds, tm, f"{tag}_swiglu_bwd")
        grads[f"{tag}_w1"] = _mm_wgrad(n, da, f"{tag}_dw1", tm)
        grads[f"{tag}_w3"] = _mm_wgrad(n, dc, f"{tag}_dw3", tm)
        dn = _mm_act([(da, w1), (dc, w3)], "nt", F32, f"{tag}_dn", tm)
        dh_in, dhb_in, dg = _rmsnorm_bwd(h, g, dn, dh, tm, f"{tag}_norm_bwd")
        grads[f"{tag}_norm"] = dg
        return dh_in, dhb_in

    dh2, dh2b = ffn_bwd(h2, g_ffn2, fullw["ffn2_w1"], fullw["ffn2_w3"], fullw["ffn2_w2"], saved2, dh3, dh3b, "ffn2")

    grads["w_out"] = _mm_wgrad(merged, dh2b, "dw_out", tm)
    dmerged = _mm_act([(dh2b, fullw["w_out"])], "nt", F32, "dmerged", tm)
    dattn, dgat, dgss, dga, dgb = _merge_bwd(dmerged, z, attn, ga, gb, tm, "merge_bwd")
    grads["ssm_glu_a"] = _mm_wgrad(yg, dga, "dglu_a", tm)
    grads["ssm_glu_b"] = _mm_wgrad(yg, dgb, "dglu_b", tm)
    dyg = _mm_act([(dga, fullw["ssm_glu_a"]), (dgb, fullw["ssm_glu_b"])], "nt", F32, "dyg", tm)
    dy0 = _gelu_bwd(dyg, y0, tm, "gelu_bwd")
    du_t, dbmat, dcmat, dlam, dd = _ssm_bwd(to_time_major(dy0), u_t, xs, xp, bmat16, cmat16, w["ssm_d"],
                                            _scan_tables(lam, B, True), B, rc, "ssm_bwd")
    d_ssm = ssm_vjp((dlam[:, 0, :], dbmat, dcmat))
    for n, gval in zip(["ssm_a_re", "ssm_a_im", "ssm_log_step", "ssm_b_re", "ssm_b_im", "ssm_c_re", "ssm_c_im"], d_ssm):
        grads[n] = gval[None]
    grads["ssm_d"] = dd

    do = jnp.transpose(dattn.reshape(B, L, H, HEAD_DIM), (0, 2, 1, 3))
    dq, dk, dv, dsink = _attn_bwd(q, k, vv, o, do, sink_col, sink_meta, "attn_bwd")
    grads["attn_sinks"] = dsink[:, 0:Q_PER_KV, 0].reshape(1, H)

    def tokens(a4d, scale=None):
        a = jnp.transpose(a4d, (0, 2, 1, 3)).reshape(T, a4d.shape[1] * HEAD_DIM)
        return (a if scale is None else a * scale).astype(BF16)

    dz = jnp.concatenate([tokens(dq, HEAD_DIM ** -0.5), tokens(dk), tokens(dv), to_batch_major(du_t).astype(BF16),
                          dgat, dgss], axis=1)
    grads["w_in"] = _mm_wgrad(hn, dz, "dw_in", tm)
    dhn = _mm_act([(dz, fullw["w_in"])], "nt", F32, "dhn", tm)
    dh1, dh1b, grads["mix_norm"] = _rmsnorm_bwd(h1, g_mix, dhn, dh2, tm, "mix_norm_bwd")
    dh0, _ = ffn_bwd(h0, g_ffn1, fullw["ffn1_w1"], fullw["ffn1_w3"], fullw["ffn1_w2"], saved1, dh1, dh1b, "ffn1")
    dh0 = dh0.reshape(B, L, D)
    grad_x = dh0[:, N_META:, :]

    grads["final_norm"] = dg_final
    slay = _Layout(D, 1, 8)
    for n in _SMALL:
        slay.add(n, w[n].shape)
    slay.add("loss", (1, D))
    meta_at = slay.align(8)
    slay.add("meta", (B * N_META, D))
    small = slay.pack({**{n: grads[n] for n in _SMALL}, "loss": loss_row, "meta": dh0[:, :N_META, :]}, F32)
    tot_small, dmeta = _all_reduce_small(small, B, N_META, meta_at)
    loss = slay.unpack(tot_small, "loss")[0, 0]
    for n in _SMALL:
        grads[n] = slay.unpack(tot_small, n)
    cw = D // 4
    grads["meta_tokens"] = lax.dynamic_slice_in_dim(dmeta, my_slot * cw, cw, axis=1)

    glay = _Layout(PACK_W, ROW_ALIGN, 2 * ROW_ALIGN)
    for n in _BIG:
        glay.add(n, big[n].shape)
    slots = {n: _slots_from_full(grads[n], n) for n in _BIG}
    g32 = jnp.stack([glay.pack({n: slots[n][s] for n in _BIG}, F32) for s in range(4)])
    r1 = _swap_halves(g32.astype(BF16))
    p32, p16 = _sum_halves(g32, r1, my_c.reshape(1).astype(jnp.int32), "grad_sum_halves")
    r2 = _exchange_chips(p16)
    f_half = _sum_chips(p32, r2, my_slot.reshape(1).astype(jnp.int32), "grad_sum_chips")
    f_all = _join_halves(f_half)
    for n in _BIG:
        grads[n] = glay.unpack(f_all, n)[None]

    delta, new_m, new_v = {}, {}, {}
    for n in _BIG + ["meta_tokens"]:
        shp = w[n].shape
        two = (shp[-2], shp[-1])
        d_, m_, v_ = _adamw(w[n].reshape(two), grads[n].reshape(two), m[n].reshape(two), v[n].reshape(two), f"adamw_{n}")
        delta[n], new_m[n], new_v[n] = d_.reshape(shp), m_.reshape(shp), v_.reshape(shp)
        grads[n] = grads[n].reshape(shp)
    play = _Layout(D, 1, 8)
    for n in _SMALL:
        play.add(n, w[n].shape)
    d_, m_, v_ = _adamw(play.pack({n: w[n] for n in _SMALL}, F32), play.pack({n: grads[n] for n in _SMALL}, F32),
                        play.pack({n: m[n] for n in _SMALL}, F32), play.pack({n: v[n] for n in _SMALL}, F32), "adamw_small")
    for n in _SMALL:
        delta[n], new_m[n], new_v[n] = play.unpack(d_, n), play.unpack(m_, n), play.unpack(v_, n)
        grads[n] = grads[n].reshape(w[n].shape)

    return (loss, grad_x, *[grads[n] for n in _WEIGHTS], *[delta[n] for n in _WEIGHTS],
            *[new_m[n] for n in _WEIGHTS], *[new_v[n] for n in _WEIGHTS])


def kernel(x, meta_tokens, ffn1_norm, ffn1_w1, ffn1_w3, ffn1_w2, mix_norm, w_in, attn_sinks, ssm_a_re, ssm_a_im, ssm_log_step, ssm_b_re, ssm_b_im, ssm_c_re, ssm_c_im, ssm_d, ssm_glu_a, ssm_glu_b, w_out, ffn2_norm, ffn2_w1, ffn2_w3, ffn2_w2, final_norm, loss_target, m_meta_tokens, m_ffn1_norm, m_ffn1_w1, m_ffn1_w3, m_ffn1_w2, m_mix_norm, m_w_in, m_attn_sinks, m_ssm_a_re, m_ssm_a_im, m_ssm_log_step, m_ssm_b_re, m_ssm_b_im, m_ssm_c_re, m_ssm_c_im, m_ssm_d, m_ssm_glu_a, m_ssm_glu_b, m_w_out, m_ffn2_norm, m_ffn2_w1, m_ffn2_w3, m_ffn2_w2, m_final_norm, v_meta_tokens, v_ffn1_norm, v_ffn1_w1, v_ffn1_w3, v_ffn1_w2, v_mix_norm, v_w_in, v_attn_sinks, v_ssm_a_re, v_ssm_a_im, v_ssm_log_step, v_ssm_b_re, v_ssm_b_im, v_ssm_c_re, v_ssm_c_im, v_ssm_d, v_ssm_glu_a, v_ssm_glu_b, v_w_out, v_ffn2_norm, v_ffn2_w1, v_ffn2_w3, v_ffn2_w2, v_final_norm):
    args = locals()
    w = {n: args[n] for n in _WEIGHTS}
    m = {n: args["m_" + n] for n in _WEIGHTS}
    v = {n: args["v_" + n] for n in _WEIGHTS}
    return _step(x, loss_target, w, m, v)
```

```python
import math

import jax
import jax.numpy as jnp
from jax import lax
from jax.experimental import pallas as pl
from jax.experimental.pallas import tpu as pltpu

F32 = jnp.float32
BF16 = jnp.bfloat16
MESH_IDS = pl.DeviceIdType.MESH

N_CHIPS = 4
N_META = 16
HEAD_DIM = 64
Q_PER_KV = 4
QW = Q_PER_KV * HEAD_DIM
BLOCK = 128
SSM_GROUP = 16
SSM_STATE = 64
SSM_LANES = 128
GROUPS_PER_COL = SSM_LANES // SSM_GROUP
STATE_LANES = GROUPS_PER_COL * SSM_STATE
NORM_EPS = 1e-6
NEG_INF = -1e30
ADAM_LR, ADAM_B1, ADAM_B2, ADAM_EPS, ADAM_WD, ADAM_STEP = 0.001, 0.9, 0.999, 1e-08, 0.01, 10
GELU_C = math.sqrt(2.0 / math.pi)
ROW_ALIGN = 16
VMEM_LIMIT = 56 * 1024 * 1024
ROW_TILE_CAP = 688

_NN = (((1,), (0,)), ((), ()))
_NT = (((1,), (1,)), ((), ()))
_TN = (((0,), (0,)), ((), ()))
_DIMS = {"nn": _NN, "nt": _NT, "tn": _TN}


def _params(sem, **kw):
    return pltpu.CompilerParams(dimension_semantics=sem, vmem_limit_bytes=VMEM_LIMIT, **kw)


def _pick_tile(n, cap, mult):
    best = None
    for t in range(mult, min(n, cap) + 1, mult):
        if n % t == 0:
            best = t
    if best is None:
        raise ValueError(f"no tile for {n} (cap {cap}, multiple of {mult})")
    return best


def _sigmoid(x):
    return 1.0 / (1.0 + jnp.exp(-x))


def _spec(block, index_map):
    return pl.BlockSpec(block, index_map)


def _mm(name, grid, kaxis, mode, pairs, out_shape, out_spec, scale=1.0, res=None):
    npairs = len(pairs)
    has_res = res is not None
    gk = 1 if kaxis is None else grid[kaxis]
    acc_shape = tuple(d for d in out_spec.block_shape if d is not None)

    def body(*refs):
        ins = refs[:2 * npairs]
        res_ref = refs[2 * npairs] if has_res else None
        o_ref = refs[2 * npairs + has_res]
        tot = None
        for p in range(npairs):
            a = ins[2 * p][...].astype(BF16)
            b = ins[2 * p + 1][...].astype(BF16)
            d = lax.dot_general(a, b, _DIMS[mode], preferred_element_type=F32)
            tot = d if tot is None else tot + d

        def finish(acc):
            r = acc * scale if scale != 1.0 else acc
            if has_res:
                r = res_ref[...] + r
            o_ref[...] = r.astype(o_ref.dtype)

        if gk == 1:
            finish(tot)
        else:
            acc_ref = refs[-1]
            k = pl.program_id(kaxis)

            @pl.when(k == 0)
            def _():
                acc_ref[...] = tot

            @pl.when(k > 0)
            def _():
                acc_ref[...] += tot

            @pl.when(k == gk - 1)
            def _():
                finish(acc_ref[...])

    in_specs, args = [], []
    for a, a_spec, b, b_spec in pairs:
        in_specs += [a_spec, b_spec]
        args += [a, b]
    if has_res:
        in_specs.append(res[1])
        args.append(res[0])
    sem = tuple("arbitrary" if ax == kaxis else "parallel" for ax in range(len(grid)))
    return pl.pallas_call(
        body, name=name, grid=grid, in_specs=in_specs, out_specs=out_spec, out_shape=out_shape,
        scratch_shapes=[pltpu.VMEM(acc_shape, F32)] if gk > 1 else [],
        compiler_params=_params(sem),
    )(*args)


def _rows(tm, width):
    return _spec((tm, width), lambda i, s: (i, 0))


def _mm_plain(a, b, mode, out_dtype, name, tm, scale=1.0):
    M, K = a.shape
    N = b.shape[1] if mode == "nn" else b.shape[0]
    return _mm(name, (M // tm,), None, mode,
               [(a, _spec((tm, K), lambda i: (i, 0)), b, _spec(b.shape, lambda i: (0, 0)))],
               jax.ShapeDtypeStruct((M, N), out_dtype), _spec((tm, N), lambda i: (i, 0)), scale=scale)


def _wgrad_plain(a, b, name, tr):
    R, M = a.shape
    N = b.shape[1]
    return _mm(name, (R // tr,), 0, "tn",
               [(a, _spec((tr, M), lambda r: (r, 0)), b, _spec((tr, N), lambda r: (r, 0)))],
               jax.ShapeDtypeStruct((M, N), F32), _spec((M, N), lambda r: (0, 0)))


def _rmsnorm_fwd(h, g, tm, name):
    T, D = h.shape

    def body(h_ref, g_ref, o_ref):
        x = h_ref[...]
        r = lax.rsqrt(jnp.mean(x * x, axis=-1, keepdims=True) + NORM_EPS)
        o_ref[...] = ((x * r) * g_ref[...]).astype(BF16)

    return pl.pallas_call(
        body, name=name, grid=(T // tm,),
        in_specs=[pl.BlockSpec((tm, D), lambda i: (i, 0)), pl.BlockSpec((1, D), lambda i: (0, 0))],
        out_specs=pl.BlockSpec((tm, D), lambda i: (i, 0)),
        out_shape=jax.ShapeDtypeStruct((T, D), BF16),
        compiler_params=_params(("parallel",)),
    )(h, g)


def _fold8(x):
    return jnp.sum(x.reshape(x.shape[0] // 8, 8, x.shape[1]), axis=0)


def _rmsnorm_bwd(h, g, dn, dres, tm, name):
    T, D = h.shape
    nt = T // tm

    def body(h_ref, g_ref, dn_ref, dres_ref, dh_ref, dhb_ref, dg_ref, acc_ref):
        i = pl.program_id(0)
        x = h_ref[...]
        r = lax.rsqrt(jnp.mean(x * x, axis=-1, keepdims=True) + NORM_EPS)
        xhat = x * r
        dy = dn_ref[...]
        dxhat = dy * g_ref[...]
        dx = r * (dxhat - xhat * jnp.mean(dxhat * xhat, axis=-1, keepdims=True))
        dh = dres_ref[...] + dx
        dh_ref[...] = dh
        dhb_ref[...] = dh.astype(BF16)
        part = _fold8(dy * xhat)

        @pl.when(i == 0)
        def _():
            acc_ref[...] = part

        @pl.when(i > 0)
        def _():
            acc_ref[...] += part

        @pl.when(i == nt - 1)
        def _():
            dg_ref[...] = jnp.sum(acc_ref[...], axis=0, keepdims=True)

    row = pl.BlockSpec((tm, D), lambda i: (i, 0))
    vec = pl.BlockSpec((1, D), lambda i: (0, 0))
    return pl.pallas_call(
        body, name=name, grid=(nt,),
        in_specs=[row, vec, row, row],
        out_specs=[row, row, vec],
        out_shape=[jax.ShapeDtypeStruct((T, D), F32), jax.ShapeDtypeStruct((T, D), BF16),
                   jax.ShapeDtypeStruct((1, D), F32)],
        scratch_shapes=[pltpu.VMEM((8, D), F32)],
        compiler_params=_params(("arbitrary",)),
    )(h, g, dn, dres)


def _ffn_up(n, w1, w3, tm, name):
    T, D = n.shape
    Fs = w1.shape[2]

    def body(n_ref, w1_ref, w3_ref, a_ref, c_ref, s_ref):
        x = n_ref[...]
        a = jnp.dot(x, w1_ref[...], preferred_element_type=F32)
        c = jnp.dot(x, w3_ref[...], preferred_element_type=F32)
        a_ref[...] = a.astype(BF16)
        c_ref[...] = c.astype(BF16)
        s_ref[...] = (a * _sigmoid(a) * c).astype(BF16)

    w_spec = _spec((None, D, Fs), lambda s, i: (s, 0, 0))
    o_spec = _spec((None, tm, Fs), lambda s, i: (s, i, 0))
    o_shape = jax.ShapeDtypeStruct((N_CHIPS, T, Fs), BF16)
    return pl.pallas_call(
        body, name=name, grid=(N_CHIPS, T // tm),
        in_specs=[_spec((tm, D), lambda s, i: (i, 0)), w_spec, w_spec],
        out_specs=[o_spec, o_spec, o_spec], out_shape=[o_shape, o_shape, o_shape],
        compiler_params=_params(("parallel", "parallel")),
    )(n, w1, w3)


def _ffn_down(s, w2, h, tm, name):
    _, T, Fs = s.shape
    D = w2.shape[2]
    return _mm(name, (T // tm, N_CHIPS), 1, "nn",
               [(s, _spec((None, tm, Fs), lambda i, k: (k, i, 0)), w2, _spec((None, Fs, D), lambda i, k: (k, 0, 0)))],
               jax.ShapeDtypeStruct((T, D), F32), _rows(tm, D), scale=0.5, res=(h, _rows(tm, D)))


def _ffn_dhidden(dhb, w2, a, c, tm, name):
    T, D = dhb.shape
    Fs = w2.shape[1]

    def body(dh_ref, w2_ref, a_ref, c_ref, da_ref, dc_ref):
        d = 0.5 * lax.dot_general(dh_ref[...], w2_ref[...], _NT, preferred_element_type=F32)
        av = a_ref[...].astype(F32)
        cv = c_ref[...].astype(F32)
        sg = _sigmoid(av)
        da_ref[...] = (d * cv * (sg * (1.0 + av * (1.0 - sg)))).astype(BF16)
        dc_ref[...] = (d * (av * sg)).astype(BF16)

    h_spec = _spec((None, tm, Fs), lambda s, i: (s, i, 0))
    o_shape = jax.ShapeDtypeStruct((N_CHIPS, T, Fs), BF16)
    return pl.pallas_call(
        body, name=name, grid=(N_CHIPS, T // tm),
        in_specs=[_spec((tm, D), lambda s, i: (i, 0)), _spec((None, Fs, D), lambda s, i: (s, 0, 0)), h_spec, h_spec],
        out_specs=[h_spec, h_spec], out_shape=[o_shape, o_shape],
        compiler_params=_params(("parallel", "parallel")),
    )(dhb, w2, a, c)


def _wgrad_hidden_rows(s, dhb, tr, name, scale):
    _, T, Fs = s.shape
    D = dhb.shape[1]
    return _mm(name, (N_CHIPS, T // tr), 1, "tn",
               [(s, _spec((None, tr, Fs), lambda k, r: (k, r, 0)), dhb, _spec((tr, D), lambda k, r: (r, 0)))],
               jax.ShapeDtypeStruct((N_CHIPS, Fs, D), F32), _spec((None, Fs, D), lambda k, r: (k, 0, 0)), scale=scale)


def _wgrad_hidden_cols(n, da, tr, name):
    T, D = n.shape
    Fs = da.shape[2]
    return _mm(name, (N_CHIPS, T // tr), 1, "tn",
               [(n, _spec((tr, D), lambda k, r: (r, 0)), da, _spec((None, tr, Fs), lambda k, r: (k, r, 0)))],
               jax.ShapeDtypeStruct((N_CHIPS, D, Fs), F32), _spec((None, D, Fs), lambda k, r: (k, 0, 0)))


def _ffn_dn(da, w1, dc, w3, tm, name):
    _, T, Fs = da.shape
    D = w1.shape[1]
    h_spec = _spec((None, tm, Fs), lambda i, k: (k, i, 0))
    w_spec = _spec((None, D, Fs), lambda i, k: (k, 0, 0))
    return _mm(name, (T // tm, N_CHIPS), 1, "nt", [(da, h_spec, w1, w_spec), (dc, h_spec, w3, w_spec)],
               jax.ShapeDtypeStruct((T, D), F32), _rows(tm, D))


def _mm_colslots(a, w, out_dtype, name, tm, first=0, count=N_CHIPS, scale=1.0):
    T, K = a.shape
    Ns = w.shape[2]
    return _mm(name, (T // tm, count), None, "nn",
               [(a, _spec((tm, K), lambda i, j: (i, 0)), w, _spec((None, K, Ns), lambda i, j: (first + j, 0, 0)))],
               jax.ShapeDtypeStruct((T, count * Ns), out_dtype), _spec((tm, Ns), lambda i, j: (i, j)), scale=scale)


def _wgrad_colslots(a, d, tr, name):
    T, K = a.shape
    Ns = d.shape[1] // N_CHIPS
    return _mm(name, (N_CHIPS, T // tr), 1, "tn",
               [(a, _spec((tr, K), lambda k, r: (r, 0)), d, _spec((tr, Ns), lambda k, r: (r, k)))],
               jax.ShapeDtypeStruct((N_CHIPS, K, Ns), F32), _spec((None, K, Ns), lambda k, r: (k, 0, 0)))


def _mm_colslots_t(pairs, tm, name):
    d0, w0 = pairs[0]
    T = d0.shape[0]
    K, Ns = w0.shape[1], w0.shape[2]
    d_spec = _spec((tm, Ns), lambda i, k: (i, k))
    w_spec = _spec((None, K, Ns), lambda i, k: (k, 0, 0))
    return _mm(name, (T // tm, N_CHIPS), 1, "nt", [(d, d_spec, w, w_spec) for d, w in pairs],
               jax.ShapeDtypeStruct((T, K), F32), _rows(tm, K))


def _mm_rowslots(a, w, h, tm, name):
    T = a.shape[0]
    Ks, N = w.shape[1], w.shape[2]
    return _mm(name, (T // tm, N_CHIPS), 1, "nn",
               [(a, _spec((tm, Ks), lambda i, k: (i, k)), w, _spec((None, Ks, N), lambda i, k: (k, 0, 0)))],
               jax.ShapeDtypeStruct((T, N), F32), _rows(tm, N), res=(h, _rows(tm, N)))


def _wgrad_rowslots(a, d, tr, name):
    T = a.shape[0]
    Ks = a.shape[1] // N_CHIPS
    N = d.shape[1]
    return _mm(name, (N_CHIPS, T // tr), 1, "tn",
               [(a, _spec((tr, Ks), lambda k, r: (r, k)), d, _spec((tr, N), lambda k, r: (r, 0)))],
               jax.ShapeDtypeStruct((N_CHIPS, Ks, N), F32), _spec((None, Ks, N), lambda k, r: (k, 0, 0)))


def _mm_rowslots_t(d, w, tm, name):
    T, N = d.shape
    Ks = w.shape[1]
    return _mm(name, (T // tm, N_CHIPS), None, "nt",
               [(d, _spec((tm, N), lambda i, j: (i, 0)), w, _spec((None, Ks, N), lambda i, j: (j, 0, 0)))],
               jax.ShapeDtypeStruct((T, N_CHIPS * Ks), F32), _spec((tm, Ks), lambda i, j: (i, j)))


def _gelu_parts(x):
    inner = GELU_C * (x + 0.044715 * (x * x * x))
    t = jnp.tanh(inner)
    return t, GELU_C * (1.0 + 3.0 * 0.044715 * (x * x))


def _gelu_fwd(y, tm, name):
    T, W = y.shape

    def body(y_ref, o_ref):
        x = y_ref[...]
        t, _ = _gelu_parts(x)
        o_ref[...] = (0.5 * x * (1.0 + t)).astype(BF16)

    spec = pl.BlockSpec((tm, W), lambda i: (i, 0))
    return pl.pallas_call(body, name=name, grid=(T // tm,), in_specs=[spec], out_specs=spec,
                          out_shape=jax.ShapeDtypeStruct((T, W), BF16),
                          compiler_params=_params(("parallel",)))(y)


def _gelu_bwd(dyg, y, tm, name):
    T, W = y.shape

    def body(d_ref, y_ref, o_ref):
        x = y_ref[...]
        t, dinner = _gelu_parts(x)
        o_ref[...] = d_ref[...] * (0.5 * (1.0 + t) + 0.5 * x * (1.0 - t * t) * dinner)

    spec = pl.BlockSpec((tm, W), lambda i: (i, 0))
    return pl.pallas_call(body, name=name, grid=(T // tm,), in_specs=[spec, spec], out_specs=spec,
                          out_shape=jax.ShapeDtypeStruct((T, W), F32),
                          compiler_params=_params(("parallel",)))(dyg, y)


def _merge_cols(D):
    cb = 512 if D % 512 == 0 else D
    return cb, D // cb


def _merge_fwd(gates, attn, ga, gb, tm, name):
    T, D = attn.shape
    cb, nc = _merge_cols(D)

    def body(gat_ref, gss_ref, attn_ref, ga_ref, gb_ref, o_ref):
        ssm = ga_ref[...] * _sigmoid(gb_ref[...])
        o_ref[...] = (_sigmoid(gat_ref[...]) * attn_ref[...] + _sigmoid(gss_ref[...]) * ssm).astype(BF16)

    def col(block):
        return pl.BlockSpec((tm, cb), lambda i, j: (i, block * nc + j))

    return pl.pallas_call(
        body, name=name, grid=(T // tm, nc),
        in_specs=[col(0), col(1), col(0), col(0), col(0)],
        out_specs=col(0), out_shape=jax.ShapeDtypeStruct((T, D), BF16),
        compiler_params=_params(("parallel", "parallel")),
    )(gates, gates, attn, ga, gb)


def _merge_bwd(dm, gates, attn, ga, gb, tm, name):
    T, D = attn.shape
    cb, nc = _merge_cols(D)

    def body(dm_ref, gat_ref, gss_ref, attn_ref, ga_ref, gb_ref, dattn_ref, dgat_ref, dgss_ref, dga_ref, dgb_ref):
        d = dm_ref[...]
        sa = _sigmoid(gat_ref[...])
        ss = _sigmoid(gss_ref[...])
        sb = _sigmoid(gb_ref[...])
        gav = ga_ref[...]
        dattn_ref[...] = d * sa
        dgat_ref[...] = (d * attn_ref[...] * (sa * (1.0 - sa))).astype(BF16)
        dgss_ref[...] = (d * (gav * sb) * (ss * (1.0 - ss))).astype(BF16)
        dssm = d * ss
        dga_ref[...] = (dssm * sb).astype(BF16)
        dgb_ref[...] = (dssm * gav * (sb * (1.0 - sb))).astype(BF16)

    def col(block):
        return pl.BlockSpec((tm, cb), lambda i, j: (i, block * nc + j))

    b16 = jax.ShapeDtypeStruct((T, D), BF16)
    return pl.pallas_call(
        body, name=name, grid=(T // tm, nc),
        in_specs=[col(0), col(0), col(1), col(0), col(0), col(0)],
        out_specs=[col(0)] * 5,
        out_shape=[jax.ShapeDtypeStruct((T, D), F32), b16, b16, b16, b16],
        compiler_params=_params(("parallel", "parallel")),
    )(dm, gates, gates, attn, ga, gb)


def _loss_head(h, g, target, tm, name):
    T, D = h.shape
    B, S, _ = target.shape
    L = S + N_META
    nt = T // tm
    tpe = L // tm

    def body(h_ref, g_ref, t_hbm, dh_ref, dhb_ref, dg_ref, loss_ref, tbuf, acc_g, acc_l, sem):
        i = pl.program_id(0)
        b, j = i // tpe, i % tpe

        @pl.when(j == 0)
        def _():
            tbuf[0:N_META, :] = jnp.zeros((N_META, D), F32)
            cp = pltpu.make_async_copy(t_hbm.at[b, pl.ds(0, tm - N_META), :], tbuf.at[pl.ds(N_META, tm - N_META), :], sem)
            cp.start()
            cp.wait()

        @pl.when(j > 0)
        def _():
            cp = pltpu.make_async_copy(t_hbm.at[b, pl.ds(j * tm - N_META, tm), :], tbuf, sem)
            cp.start()
            cp.wait()

        x = h_ref[...]
        gv = g_ref[...]
        r = lax.rsqrt(jnp.mean(x * x, axis=-1, keepdims=True) + NORM_EPS)
        xhat = x * r
        pos = j * tm + lax.broadcasted_iota(jnp.int32, (tm, 1), 0)
        err = jnp.where(pos >= N_META, xhat * gv - tbuf[...], 0.0)
        dy = err * (1.0 / D)
        dxhat = dy * gv
        dh = r * (dxhat - xhat * jnp.mean(dxhat * xhat, axis=-1, keepdims=True))
        dh_ref[...] = dh
        dhb_ref[...] = dh.astype(BF16)
        pg = _fold8(dy * xhat)
        pe = _fold8(err * err)

        @pl.when(i == 0)
        def _():
            acc_g[...] = pg
            acc_l[...] = pe

        @pl.when(i > 0)
        def _():
            acc_g[...] += pg
            acc_l[...] += pe

        @pl.when(i == nt - 1)
        def _():
            dg_ref[...] = jnp.sum(acc_g[...], axis=0, keepdims=True)
            loss_ref[...] = jnp.full((1, D), (0.5 / D) * jnp.sum(acc_l[...]), F32)

    row = pl.BlockSpec((tm, D), lambda i: (i, 0))
    vec = pl.BlockSpec((1, D), lambda i: (0, 0))
    return pl.pallas_call(
        body, name=name, grid=(nt,),
        in_specs=[row, vec, pl.BlockSpec(memory_space=pl.ANY)], out_specs=[row, row, vec, vec],
        out_shape=[jax.ShapeDtypeStruct((T, D), F32), jax.ShapeDtypeStruct((T, D), BF16),
                   jax.ShapeDtypeStruct((1, D), F32), jax.ShapeDtypeStruct((1, D), F32)],
        scratch_shapes=[pltpu.VMEM((tm, D), F32), pltpu.VMEM((8, D), F32), pltpu.VMEM((8, D), F32),
                        pltpu.SemaphoreType.DMA],
        compiler_params=_params(("arbitrary",)),
    )(h, g, target)


def _heads_to_rows(blk):
    return jnp.concatenate([blk[:, g * HEAD_DIM:(g + 1) * HEAD_DIM] for g in range(Q_PER_KV)], axis=0)


def _rows_to_heads(x):
    rows = x.shape[0] // Q_PER_KV
    return jnp.concatenate([x[g * rows:(g + 1) * rows] for g in range(Q_PER_KV)], axis=1)


def _attn_probs(qb, k_m, k_p, k_c, sink, has_prev):
    R = qb.shape[0]
    s_m = lax.dot_general(qb, k_m, _NT, preferred_element_type=F32)
    s_p = lax.dot_general(qb, k_p, _NT, preferred_element_type=F32)
    s_c = lax.dot_general(qb, k_c, _NT, preferred_element_type=F32)
    qi = lax.broadcasted_iota(jnp.int32, (R, BLOCK), 0) & (BLOCK - 1)
    kj = lax.broadcasted_iota(jnp.int32, (R, BLOCK), 1)
    s_p = jnp.where((kj > qi) & has_prev, s_p, NEG_INF)
    s_c = jnp.where(kj <= qi, s_c, NEG_INF)
    m = jnp.maximum(jnp.maximum(jnp.max(s_m, axis=-1, keepdims=True), jnp.max(s_p, axis=-1, keepdims=True)),
                    jnp.maximum(jnp.max(s_c, axis=-1, keepdims=True), sink))
    e_m, e_p, e_c, e_s = jnp.exp(s_m - m), jnp.exp(s_p - m), jnp.exp(s_c - m), jnp.exp(sink - m)
    inv = 1.0 / (jnp.sum(e_m, axis=-1, keepdims=True) + jnp.sum(e_p, axis=-1, keepdims=True)
                 + jnp.sum(e_c, axis=-1, keepdims=True) + e_s)
    return e_m * inv, e_p * inv, e_c * inv, e_s * inv


def _meta_probs(qm, k_m, sink_m):
    R = qm.shape[0]
    s = lax.dot_general(qm, k_m, _NT, preferred_element_type=F32)
    qi = lax.broadcasted_iota(jnp.int32, (R, N_META), 0) & (N_META - 1)
    kj = lax.broadcasted_iota(jnp.int32, (R, N_META), 1)
    s = jnp.where(kj <= qi, s, NEG_INF)
    m = jnp.maximum(jnp.max(s, axis=-1, keepdims=True), sink_m)
    e, e_s = jnp.exp(s - m), jnp.exp(sink_m - m)
    inv = 1.0 / (jnp.sum(e, axis=-1, keepdims=True) + e_s)
    return e * inv, e_s * inv


def _block_start(n):
    return pl.multiple_of(N_META + n * BLOCK, ROW_ALIGN)


def _kv(blk):
    return blk[:, 0:HEAD_DIM], blk[:, HEAD_DIM:2 * HEAD_DIM]


def _attn_fwd(q, kv, sink_col, sink_meta, B, name):
    T, D = q.shape
    L = T // B
    KV = D // QW
    nb = (L - N_META) // BLOCK

    def body(q_ref, kv_ref, sk_ref, skm_ref, o_ref, kvs):
        kvs[...] = kv_ref[...].astype(BF16)
        k_m, v_m = _kv(kvs[0:N_META, :])
        p, _ = _meta_probs(_heads_to_rows(q_ref[0:N_META, :]), k_m, skm_ref[0])
        o_ref[0:N_META, :] = _rows_to_heads(jnp.dot(p.astype(BF16), v_m, preferred_element_type=F32))
        sink = sk_ref[0]

        def step(n, carry):
            cur = _block_start(n)
            prev = _block_start(jnp.maximum(n - 1, 0))
            qb = _heads_to_rows(q_ref[pl.ds(cur, BLOCK), :])
            k_p, v_p = _kv(kvs[pl.ds(prev, BLOCK), :])
            k_c, v_c = _kv(kvs[pl.ds(cur, BLOCK), :])
            p_m, p_p, p_c, _ = _attn_probs(qb, k_m, k_p, k_c, sink, n > 0)
            o = (jnp.dot(p_m.astype(BF16), v_m, preferred_element_type=F32)
                 + jnp.dot(p_p.astype(BF16), v_p, preferred_element_type=F32)
                 + jnp.dot(p_c.astype(BF16), v_c, preferred_element_type=F32))
            o_ref[pl.ds(cur, BLOCK), :] = _rows_to_heads(o)
            return carry

        lax.fori_loop(0, nb, step, 0)

    q_spec = pl.BlockSpec((L, QW), lambda b, h: (b, h))
    return pl.pallas_call(
        body, name=name, grid=(B, KV),
        in_specs=[q_spec, pl.BlockSpec((L, 2 * HEAD_DIM), lambda b, h: (b, h)),
                  pl.BlockSpec((1, Q_PER_KV * BLOCK, 1), lambda b, h: (h, 0, 0)),
                  pl.BlockSpec((1, Q_PER_KV * N_META, 1), lambda b, h: (h, 0, 0))],
        out_specs=q_spec, out_shape=jax.ShapeDtypeStruct((T, D), F32),
        scratch_shapes=[pltpu.VMEM((L, 2 * HEAD_DIM), BF16)],
        compiler_params=_params(("parallel", "parallel")),
    )(q, kv, sink_col, sink_meta)


def _attn_bwd(q, kv, o, do, sink_col, sink_meta, B, name):
    T, D = q.shape
    L = T // B
    KV = D // QW
    nb = (L - N_META) // BLOCK
    R = Q_PER_KV * BLOCK
    scale = HEAD_DIM ** -0.5

    def head_totals(col, rows_per_head):
        rid = lax.broadcasted_iota(jnp.int32, (8, 128), 0)
        out = jnp.zeros((8, 128), F32)
        for g in range(Q_PER_KV):
            out = out + jnp.where(rid == g, jnp.sum(col[g * rows_per_head:(g + 1) * rows_per_head, :]), 0.0)
        return out

    def body(q_ref, kv_ref, o_ref, do_ref, sk_ref, skm_ref, dq_ref, dkv_ref, dsk_ref, kvs, acc, acc_sink):
        b = pl.program_id(1)
        kvs[...] = kv_ref[...].astype(BF16)
        acc[...] = jnp.zeros_like(acc)
        k_m, v_m = _kv(kvs[0:N_META, :])

        qm = _heads_to_rows(q_ref[0:N_META, :])
        dom = _heads_to_rows(do_ref[0:N_META, :])
        delta = jnp.sum(dom * _heads_to_rows(o_ref[0:N_META, :]), axis=-1, keepdims=True)
        p, p_s = _meta_probs(qm, k_m, skm_ref[0])
        domb = dom.astype(BF16)
        ds = (p * (lax.dot_general(domb, v_m, _NT, preferred_element_type=F32) - delta)).astype(BF16)
        dq_ref[0:N_META, :] = _rows_to_heads(jnp.dot(ds, k_m, preferred_element_type=F32) * scale).astype(BF16)
        acc[0:N_META, :] += jnp.concatenate([lax.dot_general(ds, qm, _TN, preferred_element_type=F32),
                                             lax.dot_general(p.astype(BF16), domb, _TN, preferred_element_type=F32)], axis=1)
        sink_tot = head_totals(-p_s * delta, N_META)
        sink = sk_ref[0]
        acc_sink[...] = jnp.zeros_like(acc_sink)

        def step(n, carry):
            cur = _block_start(n)
            prev = _block_start(jnp.maximum(n - 1, 0))
            qb = _heads_to_rows(q_ref[pl.ds(cur, BLOCK), :])
            k_p, v_p = _kv(kvs[pl.ds(prev, BLOCK), :])
            k_c, v_c = _kv(kvs[pl.ds(cur, BLOCK), :])
            p_m, p_p, p_c, p_s = _attn_probs(qb, k_m, k_p, k_c, sink, n > 0)
            dob = _heads_to_rows(do_ref[pl.ds(cur, BLOCK), :])
            delta = jnp.sum(dob * _heads_to_rows(o_ref[pl.ds(cur, BLOCK), :]), axis=-1, keepdims=True)
            dobb = dob.astype(BF16)
            ds_m = (p_m * (lax.dot_general(dobb, v_m, _NT, preferred_element_type=F32) - delta)).astype(BF16)
            ds_p = (p_p * (lax.dot_general(dobb, v_p, _NT, preferred_element_type=F32) - delta)).astype(BF16)
            ds_c = (p_c * (lax.dot_general(dobb, v_c, _NT, preferred_element_type=F32) - delta)).astype(BF16)
            dq = (jnp.dot(ds_m, k_m, preferred_element_type=F32) + jnp.dot(ds_p, k_p, preferred_element_type=F32)
                  + jnp.dot(ds_c, k_c, preferred_element_type=F32))
            dq_ref[pl.ds(cur, BLOCK), :] = _rows_to_heads(dq * scale).astype(BF16)

            def dkv(ds_x, p_x):
                return jnp.concatenate([lax.dot_general(ds_x, qb, _TN, preferred_element_type=F32),
                                        lax.dot_general(p_x.astype(BF16), dobb, _TN, preferred_element_type=F32)], axis=1)

            acc[0:N_META, :] += dkv(ds_m, p_m)
            acc[pl.ds(prev, BLOCK), :] += dkv(ds_p, p_p)
            acc[pl.ds(cur, BLOCK), :] += dkv(ds_c, p_c)
            acc_sink[...] += -p_s * delta
            return carry

        lax.fori_loop(0, nb, step, 0)
        dkv_ref[...] = acc[...].astype(BF16)
        tot = sink_tot + head_totals(acc_sink[...], BLOCK)

        @pl.when(b == 0)
        def _():
            dsk_ref[0] = tot

        @pl.when(b > 0)
        def _():
            dsk_ref[0] += tot

    q_spec = pl.BlockSpec((L, QW), lambda h, b: (b, h))
    kv_spec = pl.BlockSpec((L, 2 * HEAD_DIM), lambda h, b: (b, h))
    return pl.pallas_call(
        body, name=name, grid=(KV, B),
        in_specs=[q_spec, kv_spec, q_spec, q_spec,
                  pl.BlockSpec((1, R, 1), lambda h, b: (h, 0, 0)),
                  pl.BlockSpec((1, Q_PER_KV * N_META, 1), lambda h, b: (h, 0, 0))],
        out_specs=[q_spec, kv_spec, pl.BlockSpec((1, 8, 128), lambda h, b: (h, 0, 0))],
        out_shape=[jax.ShapeDtypeStruct((T, D), BF16), jax.ShapeDtypeStruct((T, KV * 2 * HEAD_DIM), BF16),
                   jax.ShapeDtypeStruct((KV, 8, 128), F32)],
        scratch_shapes=[pltpu.VMEM((L, 2 * HEAD_DIM), BF16), pltpu.VMEM((L, 2 * HEAD_DIM), F32),
                        pltpu.VMEM((R, 1), F32)],
        compiler_params=_params(("parallel", "arbitrary")),
    )(q, kv, o, do, sink_col, sink_meta)


def _cmul_add(acc_r, acc_i, lr, li, xr, xi):
    return acc_r + (lr * xr - li * xi), acc_i + (lr * xi + li * xr)


def _ssm_fwd(u, bmat, cmat, dskip, tables, nbatch, rc, name):
    T, W = u.shape
    ncol = W // SSM_LANES
    nch = T // rc
    S = STATE_LANES
    assert nbatch == 4

    def body(u_ref, b_ref, c_ref, d_ref, tab_ref, y_ref, xs_ref, xp_ref, st_ref, carry_ref):
        ch = pl.program_id(1)

        @pl.when(ch == 0)
        def _():
            carry_ref[...] = jnp.zeros_like(carry_ref)

        uv = u_ref[...]
        st_ref[...] = jnp.dot(uv.astype(BF16), b_ref[0], preferred_element_type=F32)
        tab = tab_ref[0]
        la_r, la_i, lb_r, lb_i = tab[:, 0:S], tab[:, S:2 * S], tab[:, 2 * S:3 * S], tab[:, 3 * S:4 * S]
        low = lax.broadcasted_iota(jnp.int32, (8, S), 0) < nbatch

        def tile(r0, c_r, c_i):
            v_r = st_ref[pl.ds(r0, 8), 0:S]
            v_i = st_ref[pl.ds(r0, 8), S:2 * S]
            v_r, v_i = _cmul_add(v_r, v_i, la_r, la_i, pltpu.roll(v_r, nbatch, 0), pltpu.roll(v_i, nbatch, 0))
            rc_r, rc_i = pltpu.roll(c_r, nbatch, 0), pltpu.roll(c_i, nbatch, 0)
            cb_r, cb_i = jnp.where(low, rc_r, c_r), jnp.where(low, rc_i, c_i)
            v_r, v_i = _cmul_add(v_r, v_i, lb_r, lb_i, cb_r, cb_i)
            st_ref[pl.ds(r0, 8), 0:S] = v_r
            st_ref[pl.ds(r0, 8), S:2 * S] = v_i
            p_r = jnp.where(low, rc_r, pltpu.roll(v_r, nbatch, 0))
            p_i = jnp.where(low, rc_i, pltpu.roll(v_i, nbatch, 0))
            return v_r, v_i, p_r, p_i

        def step(i, carry):
            c_r, c_i = carry
            r0 = pl.multiple_of(i * 16, 16)
            a_r, a_i, pa_r, pa_i = tile(r0, c_r, c_i)
            b_r, b_i, pb_r, pb_i = tile(r0 + 8, a_r, a_i)
            xp_ref[pl.ds(r0, 16), 0:S] = jnp.concatenate([pa_r, pb_r], axis=0).astype(BF16)
            xp_ref[pl.ds(r0, 16), S:2 * S] = jnp.concatenate([pa_i, pb_i], axis=0).astype(BF16)
            return b_r, b_i

        c_r, c_i = lax.fori_loop(0, rc // 16, step, (carry_ref[:, 0:S], carry_ref[:, S:2 * S]))
        carry_ref[:, 0:S] = c_r
        carry_ref[:, S:2 * S] = c_i
        xb = st_ref[...].astype(BF16)
        xs_ref[...] = xb
        y_ref[...] = jnp.dot(xb, c_ref[0], preferred_element_type=F32) + d_ref[...] * uv

    return pl.pallas_call(
        body, name=name, grid=(ncol, nch),
        in_specs=[pl.BlockSpec((rc, SSM_LANES), lambda g, c: (c, g)),
                  pl.BlockSpec((1, SSM_LANES, 2 * S), lambda g, c: (g, 0, 0)),
                  pl.BlockSpec((1, 2 * S, SSM_LANES), lambda g, c: (g, 0, 0)),
                  pl.BlockSpec((1, SSM_LANES), lambda g, c: (0, g)),
                  pl.BlockSpec((1, 8, 4 * S), lambda g, c: (g, 0, 0))],
        out_specs=[pl.BlockSpec((rc, SSM_LANES), lambda g, c: (c, g)),
                   pl.BlockSpec((rc, 2 * S), lambda g, c: (c, g)),
                   pl.BlockSpec((rc, 2 * S), lambda g, c: (c, g))],
        out_shape=[jax.ShapeDtypeStruct((T, W), F32), jax.ShapeDtypeStruct((T, ncol * 2 * S), BF16),
                   jax.ShapeDtypeStruct((T, ncol * 2 * S), BF16)],
        scratch_shapes=[pltpu.VMEM((rc, 2 * S), F32), pltpu.VMEM((8, 2 * S), F32)],
        compiler_params=_params(("parallel", "arbitrary")),
    )(u, bmat, cmat, dskip, tables)


def _ssm_bwd(dy, u, xs, xp, bmat, cmat, dskip, tables, nbatch, rc, name):
    T, W = u.shape
    ncol = W // SSM_LANES
    nch = T // rc
    S = STATE_LANES
    ntile = rc // 16

    def body(dy_ref, u_ref, xs_ref, xp_ref, b_ref, c_ref, d_ref, tab_ref,
             du_ref, db_ref, dc_ref, dl_ref, dd_ref, st_ref, carry_ref, accl_ref, accd_ref):
        ch = pl.program_id(1)

        @pl.when(ch == 0)
        def _():
            carry_ref[...] = jnp.zeros_like(carry_ref)
            accl_ref[...] = jnp.zeros_like(accl_ref)
            accd_ref[...] = jnp.zeros_like(accd_ref)
            db_ref[...] = jnp.zeros_like(db_ref)
            dc_ref[...] = jnp.zeros_like(dc_ref)

        dyv = dy_ref[...]
        uv = u_ref[...]
        dyb = dyv.astype(BF16)
        st_ref[...] = lax.dot_general(dyb, c_ref[0], _NT, preferred_element_type=F32)
        tab = tab_ref[0]
        la_r, la_i, lb_r, lb_i = tab[:, 0:S], tab[:, S:2 * S], tab[:, 2 * S:3 * S], tab[:, 3 * S:4 * S]
        low = lax.broadcasted_iota(jnp.int32, (8, S), 0) < nbatch

        def tile(r0, p_r, p_i, c_r, c_i, al_r, al_i):
            v_r = st_ref[pl.ds(r0, 8), 0:S]
            v_i = st_ref[pl.ds(r0, 8), S:2 * S]
            v_r, v_i = _cmul_add(v_r, v_i, la_r, la_i, pltpu.roll(v_r, nbatch, 0), pltpu.roll(v_i, nbatch, 0))
            cb_r = jnp.where(low, c_r, pltpu.roll(c_r, nbatch, 0))
            cb_i = jnp.where(low, c_i, pltpu.roll(c_i, nbatch, 0))
            v_r, v_i = _cmul_add(v_r, v_i, lb_r, lb_i, cb_r, cb_i)
            st_ref[pl.ds(r0, 8), 0:S] = v_r
            st_ref[pl.ds(r0, 8), S:2 * S] = v_i
            al_r = al_r + (v_r * p_r + v_i * p_i)
            al_i = al_i + (v_i * p_r - v_r * p_i)
            return v_r, v_i, al_r, al_i

        def step(j, carry):
            c_r, c_i, al_r, al_i = carry
            r0 = pl.multiple_of((ntile - 1 - j) * 16, 16)
            p_r = xp_ref[pl.ds(r0, 16), 0:S].astype(F32)
            p_i = xp_ref[pl.ds(r0, 16), S:2 * S].astype(F32)
            c_r, c_i, al_r, al_i = tile(r0 + 8, p_r[8:16], p_i[8:16], c_r, c_i, al_r, al_i)
            return tile(r0, p_r[0:8], p_i[0:8], c_r, c_i, al_r, al_i)

        c_r, c_i, al_r, al_i = lax.fori_loop(
            0, ntile, step,
            (carry_ref[:, 0:S], carry_ref[:, S:2 * S], accl_ref[:, 0:S], accl_ref[:, S:2 * S]))
        carry_ref[:, 0:S] = c_r
        carry_ref[:, S:2 * S] = c_i
        accl_ref[:, 0:S] = al_r
        accl_ref[:, S:2 * S] = al_i
        dsb = st_ref[...].astype(BF16)
        du_ref[...] = lax.dot_general(dsb, b_ref[0], _NT, preferred_element_type=F32) + d_ref[...] * dyv
        db_ref[0] += lax.dot_general(uv.astype(BF16), dsb, _TN, preferred_element_type=F32)
        dc_ref[0] += lax.dot_general(xs_ref[...], dyb, _TN, preferred_element_type=F32)
        accd_ref[...] += _fold8(dyv * uv)

        @pl.when(ch == nch - 1)
        def _():
            dl_ref[0] = jnp.sum(accl_ref[...], axis=0, keepdims=True)
            dd_ref[...] = jnp.sum(accd_ref[...], axis=0, keepdims=True)

    rev = lambda g, c: (nch - 1 - c, g)
    return pl.pallas_call(
        body, name=name, grid=(ncol, nch),
        in_specs=[pl.BlockSpec((rc, SSM_LANES), rev), pl.BlockSpec((rc, SSM_LANES), rev),
                  pl.BlockSpec((rc, 2 * S), rev), pl.BlockSpec((rc, 2 * S), rev),
                  pl.BlockSpec((1, SSM_LANES, 2 * S), lambda g, c: (g, 0, 0)),
                  pl.BlockSpec((1, 2 * S, SSM_LANES), lambda g, c: (g, 0, 0)),
                  pl.BlockSpec((1, SSM_LANES), lambda g, c: (0, g)),
                  pl.BlockSpec((1, 8, 4 * S), lambda g, c: (g, 0, 0))],
        out_specs=[pl.BlockSpec((rc, SSM_LANES), rev),
                   pl.BlockSpec((1, SSM_LANES, 2 * S), lambda g, c: (g, 0, 0)),
                   pl.BlockSpec((1, 2 * S, SSM_LANES), lambda g, c: (g, 0, 0)),
                   pl.BlockSpec((1, 1, 2 * S), lambda g, c: (g, 0, 0)),
                   pl.BlockSpec((1, SSM_LANES), lambda g, c: (0, g))],
        out_shape=[jax.ShapeDtypeStruct((T, W), F32),
                   jax.ShapeDtypeStruct((ncol, SSM_LANES, 2 * S), F32),
                   jax.ShapeDtypeStruct((ncol, 2 * S, SSM_LANES), F32),
                   jax.ShapeDtypeStruct((ncol, 1, 2 * S), F32),
                   jax.ShapeDtypeStruct((1, W), F32)],
        scratch_shapes=[pltpu.VMEM((rc, 2 * S), F32), pltpu.VMEM((8, 2 * S), F32),
                        pltpu.VMEM((8, 2 * S), F32), pltpu.VMEM((8, SSM_LANES), F32)],
        compiler_params=_params(("parallel", "arbitrary")),
    )(dy, u, xs, xp, bmat, cmat, dskip, tables)


def _ssm_matrices(a_re, a_im, log_step, b_re, b_im, c_re, c_im):
    G, N = a_re.shape
    ncol = G // GROUPS_PER_COL
    step = jnp.exp(log_step)[:, None]
    mag = jnp.exp(a_re * step)
    ang = a_im * step
    lam_re, lam_im = mag * jnp.cos(ang), mag * jnp.sin(ang)
    den = a_re * a_re + a_im * a_im
    nr, ni = lam_re - 1.0, lam_im
    coef_re = (nr * a_re + ni * a_im) / den
    coef_im = (ni * a_re - nr * a_im) / den
    bb_re = coef_re[..., None] * b_re - coef_im[..., None] * b_im
    bb_im = coef_re[..., None] * b_im + coef_im[..., None] * b_re
    eye = jnp.eye(GROUPS_PER_COL, dtype=F32)
    bb = jnp.stack([bb_re, bb_im]).reshape(2, ncol, GROUPS_PER_COL, N, SSM_GROUP)
    bmat = jnp.einsum("pbgnc,gh->bgcphn", bb, eye).reshape(ncol, SSM_LANES, 2 * STATE_LANES)
    cc = jnp.stack([c_re, -c_im]).reshape(2, ncol, GROUPS_PER_COL, SSM_GROUP, N)
    cmat = jnp.einsum("pbgcn,gh->bpgnhc", cc, eye).reshape(ncol, 2 * STATE_LANES, SSM_LANES)
    lam = jnp.concatenate([lam_re.reshape(ncol, STATE_LANES), lam_im.reshape(ncol, STATE_LANES)], axis=-1)
    return lam, bmat, cmat


def _scan_tables(lam, nbatch, conj):
    S = STATE_LANES
    lr, li = lam[:, None, 0:S], lam[:, None, S:2 * S]
    if conj:
        li = -li
    l2r, l2i = lr * lr - li * li, 2.0 * lr * li
    first = (jnp.arange(8) < nbatch)[None, :, None]
    zero = jnp.zeros_like(lr)
    if conj:
        parts = [jnp.where(first, lr, zero), jnp.where(first, li, zero), jnp.where(first, l2r, lr), jnp.where(first, l2i, li)]
    else:
        parts = [jnp.where(first, zero, lr), jnp.where(first, zero, li), jnp.where(first, lr, l2r), jnp.where(first, li, l2i)]
    return jnp.concatenate([jnp.broadcast_to(p, (lam.shape[0], 8, S)) for p in parts], axis=-1)


def _adamw(w, g, m, v, name):
    R, C = w.shape
    tr = R if R <= 512 else _pick_tile(R, 512, 8)

    def body(w_ref, g_ref, m_ref, v_ref, d_ref, nm_ref, nv_ref):
        gv = g_ref[...]
        mn = ADAM_B1 * m_ref[...] + (1.0 - ADAM_B1) * gv
        vn = ADAM_B2 * v_ref[...] + (1.0 - ADAM_B2) * (gv * gv)
        m_hat = mn / (1.0 - ADAM_B1 ** ADAM_STEP)
        v_hat = vn / (1.0 - ADAM_B2 ** ADAM_STEP)
        d_ref[...] = -ADAM_LR * (m_hat / (jnp.sqrt(v_hat) + ADAM_EPS) + ADAM_WD * w_ref[...])
        nm_ref[...] = mn
        nv_ref[...] = vn

    spec = pl.BlockSpec((tr, C), lambda i: (i, 0))
    shp = jax.ShapeDtypeStruct((R, C), F32)
    return pl.pallas_call(body, name=name, grid=(R // tr,), in_specs=[spec] * 4, out_specs=[spec] * 3,
                          out_shape=[shp, shp, shp], compiler_params=_params(("parallel",)))(w, g, m, v)


_ANY = pl.BlockSpec(memory_space=pl.ANY)


def _place():
    x, y, c = lax.axis_index("x"), lax.axis_index("y"), lax.axis_index("c")
    chips = [(1 - x, y), (x, 1 - y), (1 - x, 1 - y)]
    return x, y, c, chips


def _remote(src, dst, send_sems, recv_sems, k, to):
    return pltpu.make_async_remote_copy(src_ref=src, dst_ref=dst, send_sem=send_sems.at[k], recv_sem=recv_sems.at[k],
                                        device_id=to, device_id_type=MESH_IDS)


def _gather_weights(shards):
    n = len(shards)

    def body(*refs):
        srcs, outs = refs[:n], refs[n:2 * n]
        send_sems, recv_sems, local_sems = refs[2 * n:]
        x, y, c, chips = _place()
        sibling = (x, y, 1 - c)

        def piece(i, px, py, pc):
            half = shards[i].shape[0] // 2
            return outs[i].at[2 * px + py, pl.ds(pc * half, half), :]

        mine = [pltpu.make_async_copy(srcs[i], outs[i].at[2 * x + y], local_sems.at[i]) for i in range(n)]
        for cp in mine:
            cp.start()
        first = []
        for i in range(n):
            half = shards[i].shape[0] // 2
            for j, chip in enumerate(chips):
                first.append(_remote(srcs[i].at[pl.ds(c * half, half), :], piece(i, x, y, c), send_sems, recv_sems,
                                     6 * i + j, (*chip, c)))
        for cp in first:
            cp.start()
        passed = []
        for i in range(n):
            for j, chip in enumerate(chips):
                _remote(piece(i, *chip, c), piece(i, *chip, c), send_sems, recv_sems, 6 * i + j, (*chip, c)).wait_recv()
                cp = _remote(piece(i, *chip, c), piece(i, *chip, c), send_sems, recv_sems, 6 * i + 3 + j, sibling)
                cp.start()
                passed.append(cp)
        for i in range(n):
            for j, chip in enumerate(chips):
                _remote(piece(i, *chip, 1 - c), piece(i, *chip, 1 - c), send_sems, recv_sems, 6 * i + 3 + j,
                        sibling).wait_recv()
        for cp in first + passed:
            cp.wait_send()
        for cp in mine:
            cp.wait()

    return pl.pallas_call(
        body, name="gather_weights", in_specs=[_ANY] * n, out_specs=[_ANY] * n,
        out_shape=[jax.ShapeDtypeStruct((N_CHIPS,) + s.shape, s.dtype) for s in shards],
        scratch_shapes=[pltpu.SemaphoreType.DMA((6 * n,)), pltpu.SemaphoreType.DMA((6 * n,)), pltpu.SemaphoreType.DMA((n,))],
        compiler_params=pltpu.CompilerParams(has_side_effects=True),
    )(*shards)


def _swap_halves(grads):
    n = len(grads)

    def body(*refs):
        srcs, outs = refs[:n], refs[n:2 * n]
        send_sems, recv_sems = refs[2 * n:]
        x, y, c, _ = _place()
        cps = []
        for i in range(n):
            half = grads[i].shape[1] // 2
            cps.append(_remote(srcs[i].at[:, pl.ds((1 - c) * half, half), :], outs[i], send_sems, recv_sems, i, (x, y, 1 - c)))
        for cp in cps:
            cp.start()
        for cp in cps:
            cp.wait()

    return pl.pallas_call(
        body, name="grad_swap_halves", in_specs=[_ANY] * n, out_specs=[_ANY] * n,
        out_shape=[jax.ShapeDtypeStruct((N_CHIPS, g.shape[1] // 2, g.shape[2]), g.dtype) for g in grads],
        scratch_shapes=[pltpu.SemaphoreType.DMA((n,)), pltpu.SemaphoreType.DMA((n,))],
        compiler_params=pltpu.CompilerParams(has_side_effects=True),
    )(*grads)


def _exchange_chips(parts):
    n = len(parts)

    def body(*refs):
        srcs, outs = refs[:n], refs[n:2 * n]
        send_sems, recv_sems = refs[2 * n:]
        x, y, c, chips = _place()
        cps = [_remote(srcs[i].at[2 * chip[0] + chip[1]], outs[i].at[j], send_sems, recv_sems, 3 * i + j, (*chip, c))
               for i in range(n) for j, chip in enumerate(chips)]
        for cp in cps:
            cp.start()
        for cp in cps:
            cp.wait()

    return pl.pallas_call(
        body, name="grad_exchange_chips", in_specs=[_ANY] * n, out_specs=[_ANY] * n,
        out_shape=[jax.ShapeDtypeStruct((3,) + p.shape[1:], p.dtype) for p in parts],
        scratch_shapes=[pltpu.SemaphoreType.DMA((3 * n,)), pltpu.SemaphoreType.DMA((3 * n,))],
        compiler_params=pltpu.CompilerParams(has_side_effects=True),
    )(*parts)


def _join_halves(halves):
    n = len(halves)

    def body(*refs):
        srcs, outs = refs[:n], refs[n:2 * n]
        send_sems, recv_sems, local_sems = refs[2 * n:]
        x, y, c, _ = _place()
        sibling = (x, y, 1 - c)
        mine, cps = [], []
        for i in range(n):
            h = halves[i].shape[0]
            rows = outs[i].at[pl.ds(c * h, h), :]
            mine.append(pltpu.make_async_copy(srcs[i], rows, local_sems.at[i]))
            cps.append(_remote(srcs[i], rows, send_sems, recv_sems, i, sibling))
        for cp in mine + cps:
            cp.start()
        for i in range(n):
            h = halves[i].shape[0]
            _remote(srcs[i], outs[i].at[pl.ds((1 - c) * h, h), :], send_sems, recv_sems, i, sibling).wait_recv()
        for cp in cps:
            cp.wait_send()
        for cp in mine:
            cp.wait()

    return pl.pallas_call(
        body, name="grad_join_halves", in_specs=[_ANY] * n, out_specs=[_ANY] * n,
        out_shape=[jax.ShapeDtypeStruct((2 * h.shape[0], h.shape[1]), h.dtype) for h in halves],
        scratch_shapes=[pltpu.SemaphoreType.DMA((n,)), pltpu.SemaphoreType.DMA((n,)), pltpu.SemaphoreType.DMA((n,))],
        compiler_params=pltpu.CompilerParams(has_side_effects=True),
    )(*halves)


def _half_tile(h):
    return h if h <= 512 else _pick_tile(h, 512, ROW_ALIGN)


def _sum_halves(g, r1, c_idx, name):
    _, R, C = g.shape
    H = R // 2
    tr = _half_tile(H)
    nblk = H // tr

    def body(c_ref, g_ref, r_ref, p_ref):
        p_ref[...] = (g_ref[...] + r_ref[...]).astype(BF16)

    half = pl.BlockSpec((None, tr, C), lambda s, i, c_ref: (s, c_ref[0] * nblk + i, 0))
    plain = pl.BlockSpec((None, tr, C), lambda s, i, c_ref: (s, i, 0))
    return pl.pallas_call(
        body, name=name,
        grid_spec=pltpu.PrefetchScalarGridSpec(num_scalar_prefetch=1, grid=(N_CHIPS, nblk), in_specs=[half, plain],
                                               out_specs=plain),
        out_shape=jax.ShapeDtypeStruct((N_CHIPS, H, C), BF16),
        compiler_params=_params(("parallel", "parallel")),
    )(c_idx, g, r1)


def _sum_chips(g, r1, r2, idx, name):
    _, R, C = g.shape
    H = R // 2
    tr = _half_tile(H)
    nblk = H // tr

    def body(idx_ref, g_ref, r1_ref, r2_ref, o_ref):
        o_ref[...] = (((g_ref[...] + r1_ref[...]) + r2_ref[0].astype(F32)) + r2_ref[1].astype(F32)) + r2_ref[2].astype(F32)

    return pl.pallas_call(
        body, name=name,
        grid_spec=pltpu.PrefetchScalarGridSpec(
            num_scalar_prefetch=1, grid=(nblk,),
            in_specs=[pl.BlockSpec((None, tr, C), lambda i, idx_ref: (idx_ref[0], idx_ref[1] * nblk + i, 0)),
                      pl.BlockSpec((None, tr, C), lambda i, idx_ref: (idx_ref[0], i, 0)),
                      pl.BlockSpec((3, tr, C), lambda i, idx_ref: (0, i, 0))],
            out_specs=pl.BlockSpec((tr, C), lambda i, idx_ref: (i, 0))),
        out_shape=jax.ShapeDtypeStruct((H, C), F32),
        compiler_params=_params(("parallel",)),
    )(idx, g, r1, r2)


def _all_reduce_small(v, n_fold, fold_rows, fold_at):
    M, N = v.shape

    def body(x_ref, tot_ref, fold_ref, all_ref, send_sems, recv_sems, local_sem):
        x, y, c, chips = _place()
        me, sibling = (x, y, c), (x, y, 1 - c)

        def rows(px, py, pc):
            return all_ref.at[pl.ds((4 * px + 2 * py + pc) * M, M), :]

        def copy(k, block, to, src=None):
            return _remote(rows(*block) if src is None else src, rows(*block), send_sems, recv_sems, k, to)

        mine = pltpu.make_async_copy(x_ref, rows(*me), local_sem)
        mine.start()
        first = [copy(0, me, sibling, src=x_ref)]
        first += [copy(1 + j, me, (*chip, c), src=x_ref) for j, chip in enumerate(chips)]
        for cp in first:
            cp.start()
        passed = [copy(4 + j, (*chip, c), sibling) for j, chip in enumerate(chips)]
        for j, chip in enumerate(chips):
            copy(1 + j, (*chip, c), me).wait_recv()
            passed[j].start()
        copy(0, sibling, me).wait_recv()
        for j, chip in enumerate(chips):
            copy(4 + j, (*chip, 1 - c), me).wait_recv()
        for cp in first + passed:
            cp.wait_send()
        mine.wait()
        tot = all_ref[0:M, :]
        for d in range(1, 8):
            tot = tot + all_ref[d * M:(d + 1) * M, :]
        tot_ref[...] = tot
        f = tot[fold_at:fold_at + fold_rows, :]
        for e in range(1, n_fold):
            f = f + tot[fold_at + e * fold_rows:fold_at + (e + 1) * fold_rows, :]
        fold_ref[...] = f

    vm = pl.BlockSpec(memory_space=pltpu.VMEM)
    return pl.pallas_call(
        body, name="all_reduce_small", in_specs=[vm], out_specs=[vm, vm],
        out_shape=[jax.ShapeDtypeStruct((M, N), F32), jax.ShapeDtypeStruct((fold_rows, N), F32)],
        scratch_shapes=[pltpu.VMEM((8 * M, N), F32), pltpu.SemaphoreType.DMA((7,)), pltpu.SemaphoreType.DMA((7,)),
                        pltpu.SemaphoreType.DMA],
        compiler_params=pltpu.CompilerParams(has_side_effects=True, vmem_limit_bytes=VMEM_LIMIT),
    )(v)


def _as_rows(a, width):
    flat = a.reshape(-1)
    pad = (-flat.shape[0]) % width
    if pad:
        flat = jnp.concatenate([flat, jnp.zeros((pad,), flat.dtype)])
    return flat.reshape(-1, width)


class _Layout:
    def __init__(self, width, total_mult):
        self.width, self.total_mult = width, total_mult
        self.offsets, self.shapes, self.rows = {}, {}, 0

    def add(self, name, shape):
        r = -(-math.prod(shape) // self.width)
        self.offsets[name], self.shapes[name] = (self.rows, r), tuple(shape)
        self.rows += r

    def align(self, mult):
        gap = (-self.rows) % mult
        if gap:
            self.offsets[f"_gap{self.rows}"], self.shapes[f"_gap{self.rows}"] = (self.rows, gap), (gap, self.width)
            self.rows += gap
        return self.rows

    def pack(self, pieces):
        self.align(self.total_mult)
        parts = [_as_rows(pieces[n].astype(F32), self.width) if n in pieces else jnp.zeros(self.shapes[n], F32)
                 for n in self.offsets]
        return jnp.concatenate(parts, axis=0)

    def unpack(self, buf, name):
        off, r = self.offsets[name]
        shape = self.shapes[name]
        return buf[off:off + r].reshape(-1)[:math.prod(shape)].reshape(shape)


_BIG = ["ffn1_w1", "ffn1_w3", "ffn1_w2", "w_in", "ssm_glu_a", "ssm_glu_b", "w_out", "ffn2_w1", "ffn2_w3", "ffn2_w2"]
_SMALL = ["ffn1_norm", "mix_norm", "ffn2_norm", "final_norm", "attn_sinks", "ssm_a_re", "ssm_a_im", "ssm_log_step",
          "ssm_b_re", "ssm_b_im", "ssm_c_re", "ssm_c_im", "ssm_d"]
_WEIGHTS = ["meta_tokens", "ffn1_norm", "ffn1_w1", "ffn1_w3", "ffn1_w2", "mix_norm", "w_in", "attn_sinks", "ssm_a_re",
            "ssm_a_im", "ssm_log_step", "ssm_b_re", "ssm_b_im", "ssm_c_re", "ssm_c_im", "ssm_d", "ssm_glu_a",
            "ssm_glu_b", "w_out", "ffn2_norm", "ffn2_w1", "ffn2_w3", "ffn2_w2", "final_norm"]


def _kv_interleave(w, kv_heads):
    kvw = kv_heads * HEAD_DIM
    lead = w.shape[:-1]
    k = w[..., 0:kvw].reshape(lead + (kv_heads, 1, HEAD_DIM))
    v = w[..., kvw:2 * kvw].reshape(lead + (kv_heads, 1, HEAD_DIM))
    return jnp.concatenate([jnp.concatenate([k, v], axis=-2).reshape(lead + (2 * kvw,)), w[..., 2 * kvw:]], axis=-1)


def _kv_deinterleave(w, kv_heads):
    kvw = kv_heads * HEAD_DIM
    lead = w.shape[:-1]
    kv = w[..., 0:2 * kvw].reshape(lead + (kv_heads, 2, HEAD_DIM))
    return jnp.concatenate([kv[..., 0, :].reshape(lead + (kvw,)), kv[..., 1, :].reshape(lead + (kvw,)), w[..., 2 * kvw:]],
                           axis=-1)


def _step(x, target, w, m, v):
    B, S, D = x.shape
    L = S + N_META
    T = B * L
    H = D // HEAD_DIM
    KV = H // Q_PER_KV
    SW = D // 2
    tm = _pick_tile(L, ROW_TILE_CAP, ROW_ALIGN)
    rc = _pick_tile(L, ROW_TILE_CAP // B, 4) * B
    my_c = lax.axis_index("c")
    my_slot = 2 * lax.axis_index("x") + lax.axis_index("y")

    shards = [w[n][0].astype(BF16) for n in _BIG] + [w["meta_tokens"]]
    gathered = _gather_weights(shards)
    ws = dict(zip(_BIG, gathered[:-1]))
    meta = jnp.transpose(gathered[-1], (1, 0, 2)).reshape(N_META, D)
    w_kvu = _kv_interleave(ws["w_in"][1], KV)

    g_ffn1, g_mix, g_ffn2 = w["ffn1_norm"], w["mix_norm"], w["ffn2_norm"]
    g_final = w["final_norm"].reshape(1, D)

    h0 = jnp.concatenate([jnp.broadcast_to(meta[None], (B, N_META, D)), x], axis=1).reshape(T, D)

    def ffn_fwd(h, g, tag):
        n = _rmsnorm_fwd(h, g, tm, f"{tag}_norm")
        a, c, s = _ffn_up(n, ws[f"{tag}_w1"], ws[f"{tag}_w3"], tm, f"{tag}_up")
        return _ffn_down(s, ws[f"{tag}_w2"], h, tm, f"{tag}_down"), (n, a, c, s)

    h1, saved1 = ffn_fwd(h0, g_ffn1, "ffn1")
    hn = _rmsnorm_fwd(h1, g_mix, tm, "mix_norm")
    q = _mm_colslots(hn, ws["w_in"], BF16, "w_in_q", tm, first=0, count=1, scale=HEAD_DIM ** -0.5)
    kvu = _mm_plain(hn, w_kvu, "nn", F32, "w_in_kvu", tm)
    gates = _mm_colslots(hn, ws["w_in"], F32, "w_in_gates", tm, first=2, count=2)

    sinks = w["attn_sinks"].reshape(KV, Q_PER_KV, 1, 1)
    sink_col = jnp.broadcast_to(sinks, (KV, Q_PER_KV, BLOCK, 1)).reshape(KV, Q_PER_KV * BLOCK, 1)
    sink_meta = jnp.broadcast_to(sinks, (KV, Q_PER_KV, N_META, 1)).reshape(KV, Q_PER_KV * N_META, 1)
    attn = _attn_fwd(q, kvu, sink_col, sink_meta, B, "attn_fwd")

    def to_time_major(a2d):
        return jnp.transpose(a2d.reshape(B, L, a2d.shape[-1]), (1, 0, 2)).reshape(T, a2d.shape[-1])

    def to_batch_major(a2d):
        return jnp.transpose(a2d.reshape(L, B, a2d.shape[-1]), (1, 0, 2)).reshape(T, a2d.shape[-1])

    ssm_args = (w["ssm_a_re"][0], w["ssm_a_im"][0], w["ssm_log_step"][0], w["ssm_b_re"][0], w["ssm_b_im"][0],
                w["ssm_c_re"][0], w["ssm_c_im"][0])
    (lam, bmat, cmat), ssm_vjp = jax.vjp(_ssm_matrices, *ssm_args)
    bmat16, cmat16 = bmat.astype(BF16), cmat.astype(BF16)
    u_t = to_time_major(kvu[:, SW:])
    y_t, xs, xp = _ssm_fwd(u_t, bmat16, cmat16, w["ssm_d"], _scan_tables(lam, B, False), B, rc, "ssm_fwd")
    y0 = to_batch_major(y_t)
    yg = _gelu_fwd(y0, tm, "gelu_fwd")
    ga = _mm_colslots(yg, ws["ssm_glu_a"], F32, "glu_a", tm)
    gb = _mm_colslots(yg, ws["ssm_glu_b"], F32, "glu_b", tm)
    merged = _merge_fwd(gates, attn, ga, gb, tm, "merge_fwd")
    h2 = _mm_rowslots(merged, ws["w_out"], h1, tm, "w_out")
    h3, saved2 = ffn_fwd(h2, g_ffn2, "ffn2")
    dh3, dh3b, dg_final, loss_row = _loss_head(h3, g_final, target, tm, "loss_head")

    grads = {}

    def ffn_bwd(h, g, saved, dh, dhb, tag):
        n, a, c, s = saved
        w1, w3, w2 = ws[f"{tag}_w1"], ws[f"{tag}_w3"], ws[f"{tag}_w2"]
        grads[f"{tag}_w2"] = _wgrad_hidden_rows(s, dhb, tm, f"{tag}_dw2", 0.5)
        da, dc = _ffn_dhidden(dhb, w2, a, c, tm, f"{tag}_dhidden")
        grads[f"{tag}_w1"] = _wgrad_hidden_cols(n, da, tm, f"{tag}_dw1")
        grads[f"{tag}_w3"] = _wgrad_hidden_cols(n, dc, tm, f"{tag}_dw3")
        dn = _ffn_dn(da, w1, dc, w3, tm, f"{tag}_dn")
        dh_in, dhb_in, grads[f"{tag}_norm"] = _rmsnorm_bwd(h, g, dn, dh, tm, f"{tag}_norm_bwd")
        return dh_in, dhb_in

    dh2, dh2b = ffn_bwd(h2, g_ffn2, saved2, dh3, dh3b, "ffn2")

    grads["w_out"] = _wgrad_rowslots(merged, dh2b, tm, "dw_out")
    dmerged = _mm_rowslots_t(dh2b, ws["w_out"], tm, "dmerged")
    dattn, dgat, dgss, dga, dgb = _merge_bwd(dmerged, gates, attn, ga, gb, tm, "merge_bwd")
    grads["ssm_glu_a"] = _wgrad_colslots(yg, dga, tm, "dglu_a")
    grads["ssm_glu_b"] = _wgrad_colslots(yg, dgb, tm, "dglu_b")
    dyg = _mm_colslots_t([(dga, ws["ssm_glu_a"]), (dgb, ws["ssm_glu_b"])], tm, "dyg")
    dy0 = _gelu_bwd(dyg, y0, tm, "gelu_bwd")
    du_t, dbmat, dcmat, dlam, dd = _ssm_bwd(to_time_major(dy0), u_t, xs, xp, bmat16, cmat16, w["ssm_d"],
                                            _scan_tables(lam, B, True), B, rc, "ssm_bwd")
    d_ssm = ssm_vjp((dlam[:, 0, :], dbmat, dcmat))
    for n, gval in zip(["ssm_a_re", "ssm_a_im", "ssm_log_step", "ssm_b_re", "ssm_b_im", "ssm_c_re", "ssm_c_im"], d_ssm):
        grads[n] = gval[None]
    grads["ssm_d"] = dd

    dq, dkv, dsink = _attn_bwd(q, kvu, attn, dattn, sink_col, sink_meta, B, "attn_bwd")
    grads["attn_sinks"] = dsink[:, 0:Q_PER_KV, 0].reshape(1, H)
    dkvu = jnp.concatenate([dkv, to_batch_major(du_t).astype(BF16)], axis=1)
    pieces = [dq, dkvu, dgat, dgss]
    dw_in = [_wgrad_plain(hn, p, f"dw_in_{k}", tm) for k, p in enumerate(pieces)]
    dw_in[1] = _kv_deinterleave(dw_in[1], KV)
    grads["w_in"] = jnp.stack(dw_in)
    w_in_parts = [ws["w_in"][0], w_kvu, ws["w_in"][2], ws["w_in"][3]]
    whole = _spec((D, D), lambda i: (0, 0))
    dhn = _mm("dhn", (T // tm,), None, "nt",
              [(p, _spec((tm, D), lambda i: (i, 0)), wp, whole) for p, wp in zip(pieces, w_in_parts)],
              jax.ShapeDtypeStruct((T, D), F32), _spec((tm, D), lambda i: (i, 0)))
    dh1, dh1b, grads["mix_norm"] = _rmsnorm_bwd(h1, g_mix, dhn, dh2, tm, "mix_norm_bwd")
    dh0, _ = ffn_bwd(h0, g_ffn1, saved1, dh1, dh1b, "ffn1")
    dh0 = dh0.reshape(B, L, D)
    grad_x = dh0[:, N_META:, :]

    grads["final_norm"] = dg_final
    slay = _Layout(D, 8)
    for n in _SMALL:
        slay.add(n, w[n].shape)
    slay.add("loss", (1, D))
    meta_at = slay.align(8)
    slay.add("meta", (B * N_META, D))
    small = slay.pack({**{n: grads[n] for n in _SMALL}, "loss": loss_row, "meta": dh0[:, :N_META, :]})
    tot_small, dmeta = _all_reduce_small(small, B, N_META, meta_at)
    loss = slay.unpack(tot_small, "loss")[0, 0]
    for n in _SMALL:
        grads[n] = slay.unpack(tot_small, n)
    cw = D // N_CHIPS
    grads["meta_tokens"] = lax.dynamic_slice_in_dim(dmeta, my_slot * cw, cw, axis=1)

    glist = [grads[n] for n in _BIG]
    c_idx = my_c.reshape(1).astype(jnp.int32)
    idx = jnp.stack([my_slot, my_c]).astype(jnp.int32)
    r1 = _swap_halves(glist)
    parts = [_sum_halves(g, r, c_idx, f"grad_sum_halves_{n}") for n, g, r in zip(_BIG, glist, r1)]
    r2 = _exchange_chips(parts)
    halves = [_sum_chips(g, ra, rb, idx, f"grad_sum_chips_{n}") for n, g, ra, rb in zip(_BIG, glist, r1, r2)]
    for n, f in zip(_BIG, _join_halves(halves)):
        grads[n] = f[None]

    delta, new_m, new_v = {}, {}, {}
    for n in _BIG + ["meta_tokens"]:
        shp = w[n].shape
        two = (shp[-2], shp[-1])
        d_, m_, v_ = _adamw(w[n].reshape(two), grads[n].reshape(two), m[n].reshape(two), v[n].reshape(two), f"adamw_{n}")
        delta[n], new_m[n], new_v[n] = d_.reshape(shp), m_.reshape(shp), v_.reshape(shp)
        grads[n] = grads[n].reshape(shp)
    play = _Layout(D, 8)
    for n in _SMALL:
        play.add(n, w[n].shape)
    d_, m_, v_ = _adamw(play.pack({n: w[n] for n in _SMALL}), play.pack({n: grads[n] for n in _SMALL}),
                        play.pack({n: m[n] for n in _SMALL}), play.pack({n: v[n] for n in _SMALL}), "adamw_small")
    for n in _SMALL:
        delta[n], new_m[n], new_v[n] = play.unpack(d_, n), play.unpack(m_, n), play.unpack(v_, n)
        grads[n] = grads[n].reshape(w[n].shape)

    return (loss, grad_x, *[grads[n] for n in _WEIGHTS], *[delta[n] for n in _WEIGHTS],
            *[new_m[n] for n in _WEIGHTS], *[new_v[n] for n in _WEIGHTS])


def kernel(x, meta_tokens, ffn1_norm, ffn1_w1, ffn1_w3, ffn1_w2, mix_norm, w_in, attn_sinks, ssm_a_re, ssm_a_im, ssm_log_step, ssm_b_re, ssm_b_im, ssm_c_re, ssm_c_im, ssm_d, ssm_glu_a, ssm_glu_b, w_out, ffn2_norm, ffn2_w1, ffn2_w3, ffn2_w2, final_norm, loss_target, m_meta_tokens, m_ffn1_norm, m_ffn1_w1, m_ffn1_w3, m_ffn1_w2, m_mix_norm, m_w_in, m_attn_sinks, m_ssm_a_re, m_ssm_a_im, m_ssm_log_step, m_ssm_b_re, m_ssm_b_im, m_ssm_c_re, m_ssm_c_im, m_ssm_d, m_ssm_glu_a, m_ssm_glu_b, m_w_out, m_ffn2_norm, m_ffn2_w1, m_ffn2_w3, m_ffn2_w2, m_final_norm, v_meta_tokens, v_ffn1_norm, v_ffn1_w1, v_ffn1_w3, v_ffn1_w2, v_mix_norm, v_w_in, v_attn_sinks, v_ssm_a_re, v_ssm_a_im, v_ssm_log_step, v_ssm_b_re, v_ssm_b_im, v_ssm_c_re, v_ssm_c_im, v_ssm_d, v_ssm_glu_a, v_ssm_glu_b, v_w_out, v_ffn2_norm, v_ffn2_w1, v_ffn2_w3, v_ffn2_w2, v_final_norm):
    args = locals()
    w = {n: args[n] for n in _WEIGHTS}
    m = {n: args["m_" + n] for n in _WEIGHTS}
    v = {n: args["v_" + n] for n in _WEIGHTS}
    return _step(x, loss_target, w, m, v)
```

```python
import math

import jax
import jax.numpy as jnp
from jax import lax
from jax.experimental import pallas as pl
from jax.experimental.pallas import tpu as pltpu

F32 = jnp.float32
BF16 = jnp.bfloat16
MESH_IDS = pl.DeviceIdType.MESH

N_CHIPS = 4
N_META = 16
HEAD_DIM = 64
Q_PER_KV = 4
QW = Q_PER_KV * HEAD_DIM
BLOCK = 128
SSM_GROUP = 16
SSM_STATE = 64
SSM_LANES = 128
GROUPS_PER_COL = SSM_LANES // SSM_GROUP
STATE_LANES = GROUPS_PER_COL * SSM_STATE
NORM_EPS = 1e-6
NEG_INF = -1e30
ADAM_LR, ADAM_B1, ADAM_B2, ADAM_EPS, ADAM_WD, ADAM_STEP = 0.001, 0.9, 0.999, 1e-08, 0.01, 10
GELU_C = math.sqrt(2.0 / math.pi)
ROW_ALIGN = 16
VMEM_LIMIT = 56 * 1024 * 1024
ROW_TILE_CAP = 688

_NN = (((1,), (0,)), ((), ()))
_NT = (((1,), (1,)), ((), ()))
_TN = (((0,), (0,)), ((), ()))
_DIMS = {"nn": _NN, "nt": _NT, "tn": _TN}


def _params(sem, **kw):
    return pltpu.CompilerParams(dimension_semantics=sem, vmem_limit_bytes=VMEM_LIMIT, **kw)


def _pick_tile(n, cap, mult):
    best = None
    for t in range(mult, min(n, cap) + 1, mult):
        if n % t == 0:
            best = t
    if best is None:
        raise ValueError(f"no tile for {n} (cap {cap}, multiple of {mult})")
    return best


def _sigmoid(x):
    return 1.0 / (1.0 + jnp.exp(-x))


def _spec(block, index_map):
    return pl.BlockSpec(block, index_map)


def _mm(name, grid, kaxis, mode, pairs, out_shape, out_spec, scale=1.0, res=None):
    npairs = len(pairs)
    has_res = res is not None
    gk = 1 if kaxis is None else grid[kaxis]
    acc_shape = tuple(d for d in out_spec.block_shape if d is not None)

    def body(*refs):
        ins = refs[:2 * npairs]
        res_ref = refs[2 * npairs] if has_res else None
        o_ref = refs[2 * npairs + has_res]
        tot = None
        for p in range(npairs):
            a_ref, b_ref = ins[2 * p], ins[2 * p + 1]
            for sl in ([None] if len(a_ref.shape) == 2 else range(a_ref.shape[0])):
                a = (a_ref[...] if sl is None else a_ref[sl]).astype(BF16)
                b = (b_ref[...] if sl is None else b_ref[sl]).astype(BF16)
                d = lax.dot_general(a, b, _DIMS[mode], preferred_element_type=F32)
                tot = d if tot is None else tot + d

        def finish(acc):
            r = acc * scale if scale != 1.0 else acc
            if has_res:
                r = res_ref[...] + r
            o_ref[...] = r.astype(o_ref.dtype)

        if gk == 1:
            finish(tot)
        else:
            acc_ref = refs[-1]
            k = pl.program_id(kaxis)

            @pl.when(k == 0)
            def _():
                acc_ref[...] = tot

            @pl.when(k > 0)
            def _():
                acc_ref[...] += tot

            @pl.when(k == gk - 1)
            def _():
                finish(acc_ref[...])

    in_specs, args = [], []
    for a, a_spec, b, b_spec in pairs:
        in_specs += [a_spec, b_spec]
        args += [a, b]
    if has_res:
        in_specs.append(res[1])
        args.append(res[0])
    sem = tuple("arbitrary" if ax == kaxis else "parallel" for ax in range(len(grid)))
    return pl.pallas_call(
        body, name=name, grid=grid, in_specs=in_specs, out_specs=out_spec, out_shape=out_shape,
        scratch_shapes=[pltpu.VMEM(acc_shape, F32)] if gk > 1 else [],
        compiler_params=_params(sem),
    )(*args)


def _rows(tm, width):
    return _spec((tm, width), lambda i, s: (i, 0))


def _mm_plain(a, b, mode, out_dtype, name, tm, scale=1.0):
    M, K = a.shape
    N = b.shape[1] if mode == "nn" else b.shape[0]
    return _mm(name, (M // tm,), None, mode,
               [(a, _spec((tm, K), lambda i: (i, 0)), b, _spec(b.shape, lambda i: (0, 0)))],
               jax.ShapeDtypeStruct((M, N), out_dtype), _spec((tm, N), lambda i: (i, 0)), scale=scale)


def _wgrad_plain(a, b, name, tr):
    R, M = a.shape
    N = b.shape[1]
    return _mm(name, (R // tr,), 0, "tn",
               [(a, _spec((tr, M), lambda r: (r, 0)), b, _spec((tr, N), lambda r: (r, 0)))],
               jax.ShapeDtypeStruct((M, N), F32), _spec((M, N), lambda r: (0, 0)))


def _rmsnorm_fwd(h, g, tm, name):
    T, D = h.shape

    def body(h_ref, g_ref, o_ref):
        x = h_ref[...]
        r = lax.rsqrt(jnp.mean(x * x, axis=-1, keepdims=True) + NORM_EPS)
        o_ref[...] = ((x * r) * g_ref[...]).astype(BF16)

    return pl.pallas_call(
        body, name=name, grid=(T // tm,),
        in_specs=[pl.BlockSpec((tm, D), lambda i: (i, 0)), pl.BlockSpec((1, D), lambda i: (0, 0))],
        out_specs=pl.BlockSpec((tm, D), lambda i: (i, 0)),
        out_shape=jax.ShapeDtypeStruct((T, D), BF16),
        compiler_params=_params(("parallel",)),
    )(h, g)


def _fold8(x):
    return jnp.sum(x.reshape(x.shape[0] // 8, 8, x.shape[1]), axis=0)


def _rmsnorm_bwd(h, g, dn, dres, tm, name):
    T, D = h.shape
    nt = T // tm

    def body(h_ref, g_ref, dn_ref, dres_ref, dh_ref, dhb_ref, dg_ref, acc_ref):
        i = pl.program_id(0)
        x = h_ref[...]
        r = lax.rsqrt(jnp.mean(x * x, axis=-1, keepdims=True) + NORM_EPS)
        xhat = x * r
        dy = dn_ref[...]
        dxhat = dy * g_ref[...]
        dx = r * (dxhat - xhat * jnp.mean(dxhat * xhat, axis=-1, keepdims=True))
        dh = dres_ref[...] + dx
        dh_ref[...] = dh
        dhb_ref[...] = dh.astype(BF16)
        part = _fold8(dy * xhat)

        @pl.when(i == 0)
        def _():
            acc_ref[...] = part

        @pl.when(i > 0)
        def _():
            acc_ref[...] += part

        @pl.when(i == nt - 1)
        def _():
            dg_ref[...] = jnp.sum(acc_ref[...], axis=0, keepdims=True)

    row = pl.BlockSpec((tm, D), lambda i: (i, 0))
    vec = pl.BlockSpec((1, D), lambda i: (0, 0))
    return pl.pallas_call(
        body, name=name, grid=(nt,),
        in_specs=[row, vec, row, row],
        out_specs=[row, row, vec],
        out_shape=[jax.ShapeDtypeStruct((T, D), F32), jax.ShapeDtypeStruct((T, D), BF16),
                   jax.ShapeDtypeStruct((1, D), F32)],
        scratch_shapes=[pltpu.VMEM((8, D), F32)],
        compiler_params=_params(("arbitrary",)),
    )(h, g, dn, dres)


def _ffn_up(n, w1, w3, tm, name):
    T, D = n.shape
    Fs = w1.shape[2]

    def body(n_ref, w1_ref, w3_ref, a_ref, c_ref, s_ref):
        x = n_ref[...]
        a = jnp.dot(x, w1_ref[...], preferred_element_type=F32)
        c = jnp.dot(x, w3_ref[...], preferred_element_type=F32)
        a_ref[...] = a.astype(BF16)
        c_ref[...] = c.astype(BF16)
        s_ref[...] = (a * _sigmoid(a) * c).astype(BF16)

    w_spec = _spec((None, D, Fs), lambda s, i: (s, 0, 0))
    o_spec = _spec((None, tm, Fs), lambda s, i: (s, i, 0))
    o_shape = jax.ShapeDtypeStruct((N_CHIPS, T, Fs), BF16)
    return pl.pallas_call(
        body, name=name, grid=(N_CHIPS, T // tm),
        in_specs=[_spec((tm, D), lambda s, i: (i, 0)), w_spec, w_spec],
        out_specs=[o_spec, o_spec, o_spec], out_shape=[o_shape, o_shape, o_shape],
        compiler_params=_params(("parallel", "parallel")),
    )(n, w1, w3)


def _ffn_down(s, w2, h, tm, name):
    _, T, Fs = s.shape
    D = w2.shape[2]
    row = _spec((tm, D), lambda i: (i, 0))
    return _mm(name, (T // tm,), None, "nn",
               [(s, _spec((N_CHIPS, tm, Fs), lambda i: (0, i, 0)), w2, _spec((N_CHIPS, Fs, D), lambda i: (0, 0, 0)))],
               jax.ShapeDtypeStruct((T, D), F32), row, scale=0.5, res=(h, row))


def _ffn_dhidden(dhb, w2, a, c, tm, name):
    T, D = dhb.shape
    Fs = w2.shape[1]

    def body(dh_ref, w2_ref, a_ref, c_ref, da_ref, dc_ref):
        d = 0.5 * lax.dot_general(dh_ref[...], w2_ref[...], _NT, preferred_element_type=F32)
        av = a_ref[...].astype(F32)
        cv = c_ref[...].astype(F32)
        sg = _sigmoid(av)
        da_ref[...] = (d * cv * (sg * (1.0 + av * (1.0 - sg)))).astype(BF16)
        dc_ref[...] = (d * (av * sg)).astype(BF16)

    h_spec = _spec((None, tm, Fs), lambda s, i: (s, i, 0))
    o_shape = jax.ShapeDtypeStruct((N_CHIPS, T, Fs), BF16)
    return pl.pallas_call(
        body, name=name, grid=(N_CHIPS, T // tm),
        in_specs=[_spec((tm, D), lambda s, i: (i, 0)), _spec((None, Fs, D), lambda s, i: (s, 0, 0)), h_spec, h_spec],
        out_specs=[h_spec, h_spec], out_shape=[o_shape, o_shape],
        compiler_params=_params(("parallel", "parallel")),
    )(dhb, w2, a, c)


def _wgrad_hidden_rows(s, dhb, tr, name, scale):
    _, T, Fs = s.shape
    D = dhb.shape[1]
    return _mm(name, (N_CHIPS, T // tr), 1, "tn",
               [(s, _spec((None, tr, Fs), lambda k, r: (k, r, 0)), dhb, _spec((tr, D), lambda k, r: (r, 0)))],
               jax.ShapeDtypeStruct((N_CHIPS, Fs, D), F32), _spec((None, Fs, D), lambda k, r: (k, 0, 0)), scale=scale)


def _wgrad_hidden_cols(n, da, tr, name):
    T, D = n.shape
    Fs = da.shape[2]
    return _mm(name, (N_CHIPS, T // tr), 1, "tn",
               [(n, _spec((tr, D), lambda k, r: (r, 0)), da, _spec((None, tr, Fs), lambda k, r: (k, r, 0)))],
               jax.ShapeDtypeStruct((N_CHIPS, D, Fs), F32), _spec((None, D, Fs), lambda k, r: (k, 0, 0)))


def _ffn_dn(da, w1, dc, w3, tm, name):
    _, T, Fs = da.shape
    D = w1.shape[1]
    h_spec = _spec((N_CHIPS, tm, Fs), lambda i: (0, i, 0))
    w_spec = _spec((N_CHIPS, D, Fs), lambda i: (0, 0, 0))
    return _mm(name, (T // tm,), None, "nt", [(da, h_spec, w1, w_spec), (dc, h_spec, w3, w_spec)],
               jax.ShapeDtypeStruct((T, D), F32), _spec((tm, D), lambda i: (i, 0)))


def _mm_colslots(a, w, out_dtype, name, tm, first=0, count=N_CHIPS, scale=1.0):
    T, K = a.shape
    Ns = w.shape[2]
    return _mm(name, (T // tm, count), None, "nn",
               [(a, _spec((tm, K), lambda i, j: (i, 0)), w, _spec((None, K, Ns), lambda i, j: (first + j, 0, 0)))],
               jax.ShapeDtypeStruct((T, count * Ns), out_dtype), _spec((tm, Ns), lambda i, j: (i, j)), scale=scale)


def _wgrad_colslots(a, d, tr, name):
    T, K = a.shape
    Ns = d.shape[1] // N_CHIPS
    return _mm(name, (N_CHIPS, T // tr), 1, "tn",
               [(a, _spec((tr, K), lambda k, r: (r, 0)), d, _spec((tr, Ns), lambda k, r: (r, k)))],
               jax.ShapeDtypeStruct((N_CHIPS, K, Ns), F32), _spec((None, K, Ns), lambda k, r: (k, 0, 0)))


def _mm_colslots_t(pairs, tm, name):
    d0, w0 = pairs[0]
    T = d0.shape[0]
    K, Ns = w0.shape[1], w0.shape[2]
    d_spec = _spec((tm, Ns), lambda i, k: (i, k))
    w_spec = _spec((None, K, Ns), lambda i, k: (k, 0, 0))
    return _mm(name, (T // tm, N_CHIPS), 1, "nt", [(d, d_spec, w, w_spec) for d, w in pairs],
               jax.ShapeDtypeStruct((T, K), F32), _rows(tm, K))


def _mm_rowslots(a, w, h, tm, name):
    T = a.shape[0]
    Ks, N = w.shape[1], w.shape[2]
    return _mm(name, (T // tm, N_CHIPS), 1, "nn",
               [(a, _spec((tm, Ks), lambda i, k: (i, k)), w, _spec((None, Ks, N), lambda i, k: (k, 0, 0)))],
               jax.ShapeDtypeStruct((T, N), F32), _rows(tm, N), res=(h, _rows(tm, N)))


def _wgrad_rowslots(a, d, tr, name):
    T = a.shape[0]
    Ks = a.shape[1] // N_CHIPS
    N = d.shape[1]
    return _mm(name, (N_CHIPS, T // tr), 1, "tn",
               [(a, _spec((tr, Ks), lambda k, r: (r, k)), d, _spec((tr, N), lambda k, r: (r, 0)))],
               jax.ShapeDtypeStruct((N_CHIPS, Ks, N), F32), _spec((None, Ks, N), lambda k, r: (k, 0, 0)))


def _mm_rowslots_t(d, w, tm, name):
    T, N = d.shape
    Ks = w.shape[1]
    return _mm(name, (T // tm, N_CHIPS), None, "nt",
               [(d, _spec((tm, N), lambda i, j: (i, 0)), w, _spec((None, Ks, N), lambda i, j: (j, 0, 0)))],
               jax.ShapeDtypeStruct((T, N_CHIPS * Ks), F32), _spec((tm, Ks), lambda i, j: (i, j)))


def _gelu_parts(x):
    inner = GELU_C * (x + 0.044715 * (x * x * x))
    t = jnp.tanh(inner)
    return t, GELU_C * (1.0 + 3.0 * 0.044715 * (x * x))


def _gelu_fwd(y, tm, name):
    T, W = y.shape

    def body(y_ref, o_ref):
        x = y_ref[...]
        t, _ = _gelu_parts(x)
        o_ref[...] = (0.5 * x * (1.0 + t)).astype(BF16)

    spec = pl.BlockSpec((tm, W), lambda i: (i, 0))
    return pl.pallas_call(body, name=name, grid=(T // tm,), in_specs=[spec], out_specs=spec,
                          out_shape=jax.ShapeDtypeStruct((T, W), BF16),
                          compiler_params=_params(("parallel",)))(y)


def _gelu_bwd(dyg, y, tm, name):
    T, W = y.shape

    def body(d_ref, y_ref, o_ref):
        x = y_ref[...]
        t, dinner = _gelu_parts(x)
        o_ref[...] = d_ref[...] * (0.5 * (1.0 + t) + 0.5 * x * (1.0 - t * t) * dinner)

    spec = pl.BlockSpec((tm, W), lambda i: (i, 0))
    return pl.pallas_call(body, name=name, grid=(T // tm,), in_specs=[spec, spec], out_specs=spec,
                          out_shape=jax.ShapeDtypeStruct((T, W), F32),
                          compiler_params=_params(("parallel",)))(dyg, y)


def _merge_cols(D):
    cb = 512 if D % 512 == 0 else D
    return cb, D // cb


def _merge_fwd(gates, attn, ga, gb, tm, name):
    T, D = attn.shape
    cb, nc = _merge_cols(D)

    def body(gat_ref, gss_ref, attn_ref, ga_ref, gb_ref, o_ref):
        ssm = ga_ref[...] * _sigmoid(gb_ref[...])
        o_ref[...] = (_sigmoid(gat_ref[...]) * attn_ref[...] + _sigmoid(gss_ref[...]) * ssm).astype(BF16)

    def col(block):
        return pl.BlockSpec((tm, cb), lambda i, j: (i, block * nc + j))

    return pl.pallas_call(
        body, name=name, grid=(T // tm, nc),
        in_specs=[col(0), col(1), col(0), col(0), col(0)],
        out_specs=col(0), out_shape=jax.ShapeDtypeStruct((T, D), BF16),
        compiler_params=_params(("parallel", "parallel")),
    )(gates, gates, attn, ga, gb)


def _merge_bwd(dm, gates, attn, ga, gb, tm, name):
    T, D = attn.shape
    cb, nc = _merge_cols(D)

    def body(dm_ref, gat_ref, gss_ref, attn_ref, ga_ref, gb_ref, dattn_ref, dgat_ref, dgss_ref, dga_ref, dgb_ref):
        d = dm_ref[...]
        sa = _sigmoid(gat_ref[...])
        ss = _sigmoid(gss_ref[...])
        sb = _sigmoid(gb_ref[...])
        gav = ga_ref[...]
        dattn_ref[...] = d * sa
        dgat_ref[...] = (d * attn_ref[...] * (sa * (1.0 - sa))).astype(BF16)
        dgss_ref[...] = (d * (gav * sb) * (ss * (1.0 - ss))).astype(BF16)
        dssm = d * ss
        dga_ref[...] = (dssm * sb).astype(BF16)
        dgb_ref[...] = (dssm * gav * (sb * (1.0 - sb))).astype(BF16)

    def col(block):
        return pl.BlockSpec((tm, cb), lambda i, j: (i, block * nc + j))

    b16 = jax.ShapeDtypeStruct((T, D), BF16)
    return pl.pallas_call(
        body, name=name, grid=(T // tm, nc),
        in_specs=[col(0), col(0), col(1), col(0), col(0), col(0)],
        out_specs=[col(0)] * 5,
        out_shape=[jax.ShapeDtypeStruct((T, D), F32), b16, b16, b16, b16],
        compiler_params=_params(("parallel", "parallel")),
    )(dm, gates, gates, attn, ga, gb)


def _loss_head(h, g, target, tm, name):
    T, D = h.shape
    B, S, _ = target.shape
    L = S + N_META
    nt = T // tm
    tpe = L // tm

    def body(h_ref, g_ref, t_hbm, dh_ref, dhb_ref, dg_ref, loss_ref, tbuf, acc_g, acc_l, sem):
        i = pl.program_id(0)
        b, j = i // tpe, i % tpe

        @pl.when(j == 0)
        def _():
            tbuf[0:N_META, :] = jnp.zeros((N_META, D), F32)
            cp = pltpu.make_async_copy(t_hbm.at[b, pl.ds(0, tm - N_META), :], tbuf.at[pl.ds(N_META, tm - N_META), :], sem)
            cp.start()
            cp.wait()

        @pl.when(j > 0)
        def _():
            cp = pltpu.make_async_copy(t_hbm.at[b, pl.ds(j * tm - N_META, tm), :], tbuf, sem)
            cp.start()
            cp.wait()

        x = h_ref[...]
        gv = g_ref[...]
        r = lax.rsqrt(jnp.mean(x * x, axis=-1, keepdims=True) + NORM_EPS)
        xhat = x * r
        pos = j * tm + lax.broadcasted_iota(jnp.int32, (tm, 1), 0)
        err = jnp.where(pos >= N_META, xhat * gv - tbuf[...], 0.0)
        dy = err * (1.0 / D)
        dxhat = dy * gv
        dh = r * (dxhat - xhat * jnp.mean(dxhat * xhat, axis=-1, keepdims=True))
        dh_ref[...] = dh
        dhb_ref[...] = dh.astype(BF16)
        pg = _fold8(dy * xhat)
        pe = _fold8(err * err)

        @pl.when(i == 0)
        def _():
            acc_g[...] = pg
            acc_l[...] = pe

        @pl.when(i > 0)
        def _():
            acc_g[...] += pg
            acc_l[...] += pe

        @pl.when(i == nt - 1)
        def _():
            dg_ref[...] = jnp.sum(acc_g[...], axis=0, keepdims=True)
            loss_ref[...] = jnp.full((1, D), (0.5 / D) * jnp.sum(acc_l[...]), F32)

    row = pl.BlockSpec((tm, D), lambda i: (i, 0))
    vec = pl.BlockSpec((1, D), lambda i: (0, 0))
    return pl.pallas_call(
        body, name=name, grid=(nt,),
        in_specs=[row, vec, pl.BlockSpec(memory_space=pl.ANY)], out_specs=[row, row, vec, vec],
        out_shape=[jax.ShapeDtypeStruct((T, D), F32), jax.ShapeDtypeStruct((T, D), BF16),
                   jax.ShapeDtypeStruct((1, D), F32), jax.ShapeDtypeStruct((1, D), F32)],
        scratch_shapes=[pltpu.VMEM((tm, D), F32), pltpu.VMEM((8, D), F32), pltpu.VMEM((8, D), F32),
                        pltpu.SemaphoreType.DMA],
        compiler_params=_params(("arbitrary",)),
    )(h, g, target)


def _heads_to_rows(blk):
    return jnp.concatenate([blk[:, g * HEAD_DIM:(g + 1) * HEAD_DIM] for g in range(Q_PER_KV)], axis=0)


def _rows_to_heads(x):
    rows = x.shape[0] // Q_PER_KV
    return jnp.concatenate([x[g * rows:(g + 1) * rows] for g in range(Q_PER_KV)], axis=1)


def _causal(R):
    qi = lax.broadcasted_iota(jnp.int32, (R, BLOCK), 0) & (BLOCK - 1)
    kj = lax.broadcasted_iota(jnp.int32, (R, BLOCK), 1)
    return kj <= qi


def _band_probs(s_band, s_m, sink):
    m = jnp.maximum(jnp.maximum(jnp.max(s_band, axis=-1, keepdims=True), jnp.max(s_m, axis=-1, keepdims=True)), sink)
    e_b, e_m, e_s = jnp.exp(s_band - m), jnp.exp(s_m - m), jnp.exp(sink - m)
    inv = 1.0 / (jnp.sum(e_b, axis=-1, keepdims=True) + jnp.sum(e_m, axis=-1, keepdims=True) + e_s)
    return e_b * inv, e_m * inv, e_s * inv


def _fold_band(tri, two):
    return jnp.where(tri, two[:, BLOCK:2 * BLOCK], two[:, 0:BLOCK])


def _unfold_band(tri, band):
    return jnp.concatenate([jnp.where(tri, 0.0, band), jnp.where(tri, band, 0.0)], axis=1)


def _meta_probs(qm, k_m, sink_m):
    R = qm.shape[0]
    s = lax.dot_general(qm, k_m, _NT, preferred_element_type=F32)
    qi = lax.broadcasted_iota(jnp.int32, (R, N_META), 0) & (N_META - 1)
    kj = lax.broadcasted_iota(jnp.int32, (R, N_META), 1)
    s = jnp.where(kj <= qi, s, NEG_INF)
    m = jnp.maximum(jnp.max(s, axis=-1, keepdims=True), sink_m)
    e, e_s = jnp.exp(s - m), jnp.exp(sink_m - m)
    inv = 1.0 / (jnp.sum(e, axis=-1, keepdims=True) + e_s)
    return e * inv, e_s * inv


def _block_start(n):
    return pl.multiple_of(N_META + n * BLOCK, ROW_ALIGN)


def _kv(blk):
    return blk[:, 0:HEAD_DIM], blk[:, HEAD_DIM:2 * HEAD_DIM]


def _unroll(trips):
    for u in (3, 2):
        if trips % u == 0:
            return u
    return 1


def _attn_fwd(q, kv, sink_col, sink_meta, B, name):
    T, D = q.shape
    L = T // B
    KV = D // QW
    nb = (L - N_META) // BLOCK

    def body(q_ref, kv_ref, sk_ref, skm_ref, o_ref, kvs):
        kvs[...] = kv_ref[...].astype(BF16)
        k_m, v_m = _kv(kvs[0:N_META, :])
        p, _ = _meta_probs(_heads_to_rows(q_ref[0:N_META, :]), k_m, skm_ref[0])
        o_ref[0:N_META, :] = _rows_to_heads(jnp.dot(p.astype(BF16), v_m, preferred_element_type=F32))
        sink = sk_ref[0]
        tri = _causal(Q_PER_KV * BLOCK)

        qb = _heads_to_rows(q_ref[N_META:N_META + BLOCK, :])
        k_c, v_c = _kv(kvs[N_META:N_META + BLOCK, :])
        s_band = jnp.where(tri, lax.dot_general(qb, k_c, _NT, preferred_element_type=F32), NEG_INF)
        p_b, p_m, _ = _band_probs(s_band, lax.dot_general(qb, k_m, _NT, preferred_element_type=F32), sink)
        o = (jnp.dot(p_b.astype(BF16), v_c, preferred_element_type=F32)
             + jnp.dot(p_m.astype(BF16), v_m, preferred_element_type=F32))
        o_ref[N_META:N_META + BLOCK, :] = _rows_to_heads(o)

        def step(n, carry):
            cur = _block_start(n)
            qb = _heads_to_rows(q_ref[pl.ds(cur, BLOCK), :])
            k2, v2 = _kv(kvs[pl.ds(_block_start(n - 1), 2 * BLOCK), :])
            s_band = _fold_band(tri, lax.dot_general(qb, k2, _NT, preferred_element_type=F32))
            p_b, p_m, _ = _band_probs(s_band, lax.dot_general(qb, k_m, _NT, preferred_element_type=F32), sink)
            o = (jnp.dot(_unfold_band(tri, p_b).astype(BF16), v2, preferred_element_type=F32)
                 + jnp.dot(p_m.astype(BF16), v_m, preferred_element_type=F32))
            o_ref[pl.ds(cur, BLOCK), :] = _rows_to_heads(o)
            return carry

        lax.fori_loop(1, nb, step, 0, unroll=_unroll(nb - 1))

    q_spec = pl.BlockSpec((L, QW), lambda b, h: (b, h))
    return pl.pallas_call(
        body, name=name, grid=(B, KV),
        in_specs=[q_spec, pl.BlockSpec((L, 2 * HEAD_DIM), lambda b, h: (b, h)),
                  pl.BlockSpec((1, Q_PER_KV * BLOCK, 1), lambda b, h: (h, 0, 0)),
                  pl.BlockSpec((1, Q_PER_KV * N_META, 1), lambda b, h: (h, 0, 0))],
        out_specs=q_spec, out_shape=jax.ShapeDtypeStruct((T, D), F32),
        scratch_shapes=[pltpu.VMEM((L, 2 * HEAD_DIM), BF16)],
        compiler_params=_params(("parallel", "parallel")),
    )(q, kv, sink_col, sink_meta)


def _attn_bwd(q, kv, o, do, sink_col, sink_meta, B, name):
    T, D = q.shape
    L = T // B
    KV = D // QW
    nb = (L - N_META) // BLOCK
    R = Q_PER_KV * BLOCK
    scale = HEAD_DIM ** -0.5

    def head_totals(col, rows_per_head):
        rid = lax.broadcasted_iota(jnp.int32, (8, 128), 0)
        out = jnp.zeros((8, 128), F32)
        for g in range(Q_PER_KV):
            out = out + jnp.where(rid == g, jnp.sum(col[g * rows_per_head:(g + 1) * rows_per_head, :]), 0.0)
        return out

    def body(q_ref, kv_ref, o_ref, do_ref, sk_ref, skm_ref, dq_ref, dkv_ref, dsk_ref, kvs, acc, acc_sink):
        b = pl.program_id(1)
        kvs[...] = kv_ref[...].astype(BF16)
        acc[...] = jnp.zeros_like(acc)
        k_m, v_m = _kv(kvs[0:N_META, :])

        qm = _heads_to_rows(q_ref[0:N_META, :])
        dom = _heads_to_rows(do_ref[0:N_META, :])
        delta = jnp.sum(dom * _heads_to_rows(o_ref[0:N_META, :]), axis=-1, keepdims=True)
        p, p_s = _meta_probs(qm, k_m, skm_ref[0])
        domb = dom.astype(BF16)
        ds = (p * (lax.dot_general(domb, v_m, _NT, preferred_element_type=F32) - delta)).astype(BF16)
        dq_ref[0:N_META, :] = _rows_to_heads(jnp.dot(ds, k_m, preferred_element_type=F32) * scale).astype(BF16)
        acc[0:N_META, :] += jnp.concatenate([lax.dot_general(ds, qm, _TN, preferred_element_type=F32),
                                             lax.dot_general(p.astype(BF16), domb, _TN, preferred_element_type=F32)], axis=1)
        sink_tot = head_totals(-p_s * delta, N_META)
        sink = sk_ref[0]
        tri = _causal(R)

        def dkv(ds_x, p_x, qb, dobb):
            return jnp.concatenate([lax.dot_general(ds_x, qb, _TN, preferred_element_type=F32),
                                    lax.dot_general(p_x, dobb, _TN, preferred_element_type=F32)], axis=1)

        def block(cur, first, keys):
            qb = _heads_to_rows(q_ref[pl.ds(cur, BLOCK), :])
            k2, v2 = _kv(kvs[keys, :])
            s2 = lax.dot_general(qb, k2, _NT, preferred_element_type=F32)
            s_band = jnp.where(tri, s2, NEG_INF) if first else _fold_band(tri, s2)
            p_b, p_m, p_s = _band_probs(s_band, lax.dot_general(qb, k_m, _NT, preferred_element_type=F32), sink)
            dob = _heads_to_rows(do_ref[pl.ds(cur, BLOCK), :])
            delta = jnp.sum(dob * _heads_to_rows(o_ref[pl.ds(cur, BLOCK), :]), axis=-1, keepdims=True)
            dobb = dob.astype(BF16)
            dp2 = lax.dot_general(dobb, v2, _NT, preferred_element_type=F32)
            ds_b = p_b * ((dp2 if first else _fold_band(tri, dp2)) - delta)
            ds_m = (p_m * (lax.dot_general(dobb, v_m, _NT, preferred_element_type=F32) - delta)).astype(BF16)
            ds2 = (ds_b if first else _unfold_band(tri, ds_b)).astype(BF16)
            p2 = (p_b if first else _unfold_band(tri, p_b)).astype(BF16)
            dq = jnp.dot(ds2, k2, preferred_element_type=F32) + jnp.dot(ds_m, k_m, preferred_element_type=F32)
            dq_ref[pl.ds(cur, BLOCK), :] = _rows_to_heads(dq * scale).astype(BF16)
            acc[keys, :] += dkv(ds2, p2, qb, dobb)
            acc[0:N_META, :] += dkv(ds_m, p_m.astype(BF16), qb, dobb)
            return -p_s * delta

        acc_sink[...] = block(N_META, True, pl.ds(N_META, BLOCK))

        def step(n, carry):
            acc_sink[...] += block(_block_start(n), False, pl.ds(_block_start(n - 1), 2 * BLOCK))
            return carry

        lax.fori_loop(1, nb, step, 0)
        dkv_ref[...] = acc[...].astype(BF16)
        tot = sink_tot + head_totals(acc_sink[...], BLOCK)

        @pl.when(b == 0)
        def _():
            dsk_ref[0] = tot

        @pl.when(b > 0)
        def _():
            dsk_ref[0] += tot

    q_spec = pl.BlockSpec((L, QW), lambda h, b: (b, h))
    kv_spec = pl.BlockSpec((L, 2 * HEAD_DIM), lambda h, b: (b, h))
    return pl.pallas_call(
        body, name=name, grid=(KV, B),
        in_specs=[q_spec, kv_spec, q_spec, q_spec,
                  pl.BlockSpec((1, R, 1), lambda h, b: (h, 0, 0)),
                  pl.BlockSpec((1, Q_PER_KV * N_META, 1), lambda h, b: (h, 0, 0))],
        out_specs=[q_spec, kv_spec, pl.BlockSpec((1, 8, 128), lambda h, b: (h, 0, 0))],
        out_shape=[jax.ShapeDtypeStruct((T, D), BF16), jax.ShapeDtypeStruct((T, KV * 2 * HEAD_DIM), BF16),
                   jax.ShapeDtypeStruct((KV, 8, 128), F32)],
        scratch_shapes=[pltpu.VMEM((L, 2 * HEAD_DIM), BF16), pltpu.VMEM((L, 2 * HEAD_DIM), F32),
                        pltpu.VMEM((R, 1), F32)],
        compiler_params=_params(("parallel", "arbitrary")),
    )(q, kv, o, do, sink_col, sink_meta)


def _cmul_add(acc_r, acc_i, lr, li, xr, xi):
    return acc_r + (lr * xr - li * xi), acc_i + (lr * xi + li * xr)


def _ssm_fwd(u, bmat, cmat, dskip, tables, nbatch, rc, name):
    T, W = u.shape
    ncol = W // SSM_LANES
    nch = T // rc
    S = STATE_LANES
    assert nbatch == 4

    def body(u_ref, b_ref, c_ref, d_ref, tab_ref, y_ref, xs_ref, xp_ref, st_ref, carry_ref):
        ch = pl.program_id(1)

        @pl.when(ch == 0)
        def _():
            carry_ref[...] = jnp.zeros_like(carry_ref)

        uv = u_ref[...]
        st_ref[...] = jnp.dot(uv.astype(BF16), b_ref[0], preferred_element_type=F32)
        tab = tab_ref[0]
        la_r, la_i, lb_r, lb_i = tab[:, 0:S], tab[:, S:2 * S], tab[:, 2 * S:3 * S], tab[:, 3 * S:4 * S]
        low = lax.broadcasted_iota(jnp.int32, (8, S), 0) < nbatch

        def tile(r0, c_r, c_i):
            v_r = st_ref[pl.ds(r0, 8), 0:S]
            v_i = st_ref[pl.ds(r0, 8), S:2 * S]
            v_r, v_i = _cmul_add(v_r, v_i, la_r, la_i, pltpu.roll(v_r, nbatch, 0), pltpu.roll(v_i, nbatch, 0))
            rc_r, rc_i = pltpu.roll(c_r, nbatch, 0), pltpu.roll(c_i, nbatch, 0)
            cb_r, cb_i = jnp.where(low, rc_r, c_r), jnp.where(low, rc_i, c_i)
            v_r, v_i = _cmul_add(v_r, v_i, lb_r, lb_i, cb_r, cb_i)
            st_ref[pl.ds(r0, 8), 0:S] = v_r
            st_ref[pl.ds(r0, 8), S:2 * S] = v_i
            p_r = jnp.where(low, rc_r, pltpu.roll(v_r, nbatch, 0))
            p_i = jnp.where(low, rc_i, pltpu.roll(v_i, nbatch, 0))
            return v_r, v_i, p_r, p_i

        def step(i, carry):
            c_r, c_i = carry
            r0 = pl.multiple_of(i * 16, 16)
            a_r, a_i, pa_r, pa_i = tile(r0, c_r, c_i)
            b_r, b_i, pb_r, pb_i = tile(r0 + 8, a_r, a_i)
            xp_ref[pl.ds(r0, 16), 0:S] = jnp.concatenate([pa_r, pb_r], axis=0).astype(BF16)
            xp_ref[pl.ds(r0, 16), S:2 * S] = jnp.concatenate([pa_i, pb_i], axis=0).astype(BF16)
            return b_r, b_i

        c_r, c_i = lax.fori_loop(0, rc // 16, step, (carry_ref[:, 0:S], carry_ref[:, S:2 * S]))
        carry_ref[:, 0:S] = c_r
        carry_ref[:, S:2 * S] = c_i
        xb = st_ref[...].astype(BF16)
        xs_ref[...] = xb
        y_ref[...] = jnp.dot(xb, c_ref[0], preferred_element_type=F32) + d_ref[...] * uv

    return pl.pallas_call(
        body, name=name, grid=(ncol, nch),
        in_specs=[pl.BlockSpec((rc, SSM_LANES), lambda g, c: (c, g)),
                  pl.BlockSpec((1, SSM_LANES, 2 * S), lambda g, c: (g, 0, 0)),
                  pl.BlockSpec((1, 2 * S, SSM_LANES), lambda g, c: (g, 0, 0)),
                  pl.BlockSpec((1, SSM_LANES), lambda g, c: (0, g)),
                  pl.BlockSpec((1, 8, 4 * S), lambda g, c: (g, 0, 0))],
        out_specs=[pl.BlockSpec((rc, SSM_LANES), lambda g, c: (c, g)),
                   pl.BlockSpec((rc, 2 * S), lambda g, c: (c, g)),
                   pl.BlockSpec((rc, 2 * S), lambda g, c: (c, g))],
        out_shape=[jax.ShapeDtypeStruct((T, W), F32), jax.ShapeDtypeStruct((T, ncol * 2 * S), BF16),
                   jax.ShapeDtypeStruct((T, ncol * 2 * S), BF16)],
        scratch_shapes=[pltpu.VMEM((rc, 2 * S), F32), pltpu.VMEM((8, 2 * S), F32)],
        compiler_params=_params(("parallel", "arbitrary")),
    )(u, bmat, cmat, dskip, tables)


def _ssm_bwd(dy, u, xs, xp, bmat, cmat, dskip, tables, nbatch, rc, name):
    T, W = u.shape
    ncol = W // SSM_LANES
    nch = T // rc
    S = STATE_LANES
    ntile = rc // 16

    def body(dy_ref, u_ref, xs_ref, xp_ref, b_ref, c_ref, d_ref, tab_ref,
             du_ref, db_ref, dc_ref, dl_ref, dd_ref, st_ref, carry_ref, accl_ref, accd_ref):
        ch = pl.program_id(1)

        @pl.when(ch == 0)
        def _():
            carry_ref[...] = jnp.zeros_like(carry_ref)
            accl_ref[...] = jnp.zeros_like(accl_ref)
            accd_ref[...] = jnp.zeros_like(accd_ref)
            db_ref[...] = jnp.zeros_like(db_ref)
            dc_ref[...] = jnp.zeros_like(dc_ref)

        dyv = dy_ref[...]
        uv = u_ref[...]
        dyb = dyv.astype(BF16)
        st_ref[...] = lax.dot_general(dyb, c_ref[0], _NT, preferred_element_type=F32)
        tab = tab_ref[0]
        la_r, la_i, lb_r, lb_i = tab[:, 0:S], tab[:, S:2 * S], tab[:, 2 * S:3 * S], tab[:, 3 * S:4 * S]
        low = lax.broadcasted_iota(jnp.int32, (8, S), 0) < nbatch

        def tile(r0, p_r, p_i, c_r, c_i, al_r, al_i):
            v_r = st_ref[pl.ds(r0, 8), 0:S]
            v_i = st_ref[pl.ds(r0, 8), S:2 * S]
            v_r, v_i = _cmul_add(v_r, v_i, la_r, la_i, pltpu.roll(v_r, nbatch, 0), pltpu.roll(v_i, nbatch, 0))
            cb_r = jnp.where(low, c_r, pltpu.roll(c_r, nbatch, 0))
            cb_i = jnp.where(low, c_i, pltpu.roll(c_i, nbatch, 0))
            v_r, v_i = _cmul_add(v_r, v_i, lb_r, lb_i, cb_r, cb_i)
            st_ref[pl.ds(r0, 8), 0:S] = v_r
            st_ref[pl.ds(r0, 8), S:2 * S] = v_i
            al_r = al_r + (v_r * p_r + v_i * p_i)
            al_i = al_i + (v_i * p_r - v_r * p_i)
            return v_r, v_i, al_r, al_i

        def step(j, carry):
            c_r, c_i, al_r, al_i = carry
            r0 = pl.multiple_of((ntile - 1 - j) * 16, 16)
            p_r = xp_ref[pl.ds(r0, 16), 0:S].astype(F32)
            p_i = xp_ref[pl.ds(r0, 16), S:2 * S].astype(F32)
            c_r, c_i, al_r, al_i = tile(r0 + 8, p_r[8:16], p_i[8:16], c_r, c_i, al_r, al_i)
            return tile(r0, p_r[0:8], p_i[0:8], c_r, c_i, al_r, al_i)

        c_r, c_i, al_r, al_i = lax.fori_loop(
            0, ntile, step,
            (carry_ref[:, 0:S], carry_ref[:, S:2 * S], accl_ref[:, 0:S], accl_ref[:, S:2 * S]))
        carry_ref[:, 0:S] = c_r
        carry_ref[:, S:2 * S] = c_i
        accl_ref[:, 0:S] = al_r
        accl_ref[:, S:2 * S] = al_i
        dsb = st_ref[...].astype(BF16)
        du_ref[...] = lax.dot_general(dsb, b_ref[0], _NT, preferred_element_type=F32) + d_ref[...] * dyv
        db_ref[0] += lax.dot_general(uv.astype(BF16), dsb, _TN, preferred_element_type=F32)
        dc_ref[0] += lax.dot_general(xs_ref[...], dyb, _TN, preferred_element_type=F32)
        accd_ref[...] += _fold8(dyv * uv)

        @pl.when(ch == nch - 1)
        def _():
            dl_ref[0] = jnp.sum(accl_ref[...], axis=0, keepdims=True)
            dd_ref[...] = jnp.sum(accd_ref[...], axis=0, keepdims=True)

    rev = lambda g, c: (nch - 1 - c, g)
    return pl.pallas_call(
        body, name=name, grid=(ncol, nch),
        in_specs=[pl.BlockSpec((rc, SSM_LANES), rev), pl.BlockSpec((rc, SSM_LANES), rev),
                  pl.BlockSpec((rc, 2 * S), rev), pl.BlockSpec((rc, 2 * S), rev),
                  pl.BlockSpec((1, SSM_LANES, 2 * S), lambda g, c: (g, 0, 0)),
                  pl.BlockSpec((1, 2 * S, SSM_LANES), lambda g, c: (g, 0, 0)),
                  pl.BlockSpec((1, SSM_LANES), lambda g, c: (0, g)),
                  pl.BlockSpec((1, 8, 4 * S), lambda g, c: (g, 0, 0))],
        out_specs=[pl.BlockSpec((rc, SSM_LANES), rev),
                   pl.BlockSpec((1, SSM_LANES, 2 * S), lambda g, c: (g, 0, 0)),
                   pl.BlockSpec((1, 2 * S, SSM_LANES), lambda g, c: (g, 0, 0)),
                   pl.BlockSpec((1, 1, 2 * S), lambda g, c: (g, 0, 0)),
                   pl.BlockSpec((1, SSM_LANES), lambda g, c: (0, g))],
        out_shape=[jax.ShapeDtypeStruct((T, W), F32),
                   jax.ShapeDtypeStruct((ncol, SSM_LANES, 2 * S), F32),
                   jax.ShapeDtypeStruct((ncol, 2 * S, SSM_LANES), F32),
                   jax.ShapeDtypeStruct((ncol, 1, 2 * S), F32),
                   jax.ShapeDtypeStruct((1, W), F32)],
        scratch_shapes=[pltpu.VMEM((rc, 2 * S), F32), pltpu.VMEM((8, 2 * S), F32),
                        pltpu.VMEM((8, 2 * S), F32), pltpu.VMEM((8, SSM_LANES), F32)],
        compiler_params=_params(("parallel", "arbitrary")),
    )(dy, u, xs, xp, bmat, cmat, dskip, tables)


def _ssm_matrices(a_re, a_im, log_step, b_re, b_im, c_re, c_im):
    G, N = a_re.shape
    ncol = G // GROUPS_PER_COL
    step = jnp.exp(log_step)[:, None]
    mag = jnp.exp(a_re * step)
    ang = a_im * step
    lam_re, lam_im = mag * jnp.cos(ang), mag * jnp.sin(ang)
    den = a_re * a_re + a_im * a_im
    nr, ni = lam_re - 1.0, lam_im
    coef_re = (nr * a_re + ni * a_im) / den
    coef_im = (ni * a_re - nr * a_im) / den
    bb_re = coef_re[..., None] * b_re - coef_im[..., None] * b_im
    bb_im = coef_re[..., None] * b_im + coef_im[..., None] * b_re
    eye = jnp.eye(GROUPS_PER_COL, dtype=F32)
    bb = jnp.stack([bb_re, bb_im]).reshape(2, ncol, GROUPS_PER_COL, N, SSM_GROUP)
    bmat = jnp.einsum("pbgnc,gh->bgcphn", bb, eye).reshape(ncol, SSM_LANES, 2 * STATE_LANES)
    cc = jnp.stack([c_re, -c_im]).reshape(2, ncol, GROUPS_PER_COL, SSM_GROUP, N)
    cmat = jnp.einsum("pbgcn,gh->bpgnhc", cc, eye).reshape(ncol, 2 * STATE_LANES, SSM_LANES)
    lam = jnp.concatenate([lam_re.reshape(ncol, STATE_LANES), lam_im.reshape(ncol, STATE_LANES)], axis=-1)
    return lam, bmat, cmat


def _scan_tables(lam, nbatch, conj):
    S = STATE_LANES
    lr, li = lam[:, None, 0:S], lam[:, None, S:2 * S]
    if conj:
        li = -li
    l2r, l2i = lr * lr - li * li, 2.0 * lr * li
    first = (jnp.arange(8) < nbatch)[None, :, None]
    zero = jnp.zeros_like(lr)
    if conj:
        parts = [jnp.where(first, lr, zero), jnp.where(first, li, zero), jnp.where(first, l2r, lr), jnp.where(first, l2i, li)]
    else:
        parts = [jnp.where(first, zero, lr), jnp.where(first, zero, li), jnp.where(first, lr, l2r), jnp.where(first, li, l2i)]
    return jnp.concatenate([jnp.broadcast_to(p, (lam.shape[0], 8, S)) for p in parts], axis=-1)


def _adamw(w, g, m, v, name):
    R, C = w.shape
    tr = R if R <= 512 else _pick_tile(R, 512, 8)

    def body(w_ref, g_ref, m_ref, v_ref, d_ref, nm_ref, nv_ref):
        gv = g_ref[...]
        mn = ADAM_B1 * m_ref[...] + (1.0 - ADAM_B1) * gv
        vn = ADAM_B2 * v_ref[...] + (1.0 - ADAM_B2) * (gv * gv)
        m_hat = mn / (1.0 - ADAM_B1 ** ADAM_STEP)
        v_hat = vn / (1.0 - ADAM_B2 ** ADAM_STEP)
        d_ref[...] = -ADAM_LR * (m_hat / (jnp.sqrt(v_hat) + ADAM_EPS) + ADAM_WD * w_ref[...])
        nm_ref[...] = mn
        nv_ref[...] = vn

    spec = pl.BlockSpec((tr, C), lambda i: (i, 0))
    shp = jax.ShapeDtypeStruct((R, C), F32)
    return pl.pallas_call(body, name=name, grid=(R // tr,), in_specs=[spec] * 4, out_specs=[spec] * 3,
                          out_shape=[shp, shp, shp], compiler_params=_params(("parallel",)))(w, g, m, v)


_ANY = pl.BlockSpec(memory_space=pl.ANY)


def _place():
    x, y, c = lax.axis_index("x"), lax.axis_index("y"), lax.axis_index("c")
    chips = [(1 - x, y), (x, 1 - y), (1 - x, 1 - y)]
    return x, y, c, chips


def _remote(src, dst, send_sems, recv_sems, k, to):
    return pltpu.make_async_remote_copy(src_ref=src, dst_ref=dst, send_sem=send_sems.at[k], recv_sem=recv_sems.at[k],
                                        device_id=to, device_id_type=MESH_IDS)


def _gather_weights(shards):
    n = len(shards)

    def body(*refs):
        srcs, outs = refs[:n], refs[n:2 * n]
        send_sems, recv_sems = refs[2 * n:]
        x, y, c, chips = _place()
        sibling = (x, y, 1 - c)

        def piece(i, px, py, pc):
            half = shards[i].shape[0] // 2
            return outs[i].at[2 * px + py, pl.ds(pc * half, half), :]

        first = []
        for i in range(n):
            half = shards[i].shape[0] // 2
            for j, chip in enumerate(chips):
                first.append(_remote(srcs[i].at[pl.ds(c * half, half), :], piece(i, x, y, c), send_sems, recv_sems,
                                     6 * i + j, (*chip, c)))
        for cp in first:
            cp.start()
        passed = []
        for i in range(n):
            for j, chip in enumerate(chips):
                _remote(piece(i, *chip, c), piece(i, *chip, c), send_sems, recv_sems, 6 * i + j, (*chip, c)).wait_recv()
                cp = _remote(piece(i, *chip, c), piece(i, *chip, c), send_sems, recv_sems, 6 * i + 3 + j, sibling)
                cp.start()
                passed.append(cp)
        for i in range(n):
            for j, chip in enumerate(chips):
                _remote(piece(i, *chip, 1 - c), piece(i, *chip, 1 - c), send_sems, recv_sems, 6 * i + 3 + j,
                        sibling).wait_recv()
        for cp in first + passed:
            cp.wait_send()

    outs = pl.pallas_call(
        body, name="gather_weights", in_specs=[_ANY] * n, out_specs=[_ANY] * n,
        out_shape=[jax.ShapeDtypeStruct((N_CHIPS,) + s.shape, s.dtype) for s in shards],
        scratch_shapes=[pltpu.SemaphoreType.DMA((6 * n,)), pltpu.SemaphoreType.DMA((6 * n,))],
        compiler_params=pltpu.CompilerParams(has_side_effects=True),
    )(*shards)
    slot = 2 * lax.axis_index("x") + lax.axis_index("y")
    return [lax.dynamic_update_slice(o, s[None], (slot, 0, 0)) for o, s in zip(outs, shards)]


def _swap_halves(grads):
    n = len(grads)

    def body(*refs):
        srcs, outs = refs[:n], refs[n:2 * n]
        send_sems, recv_sems = refs[2 * n:]
        x, y, c, _ = _place()
        cps = []
        for i in range(n):
            half = grads[i].shape[1] // 2
            cps.append(_remote(srcs[i].at[:, pl.ds((1 - c) * half, half), :], outs[i], send_sems, recv_sems, i, (x, y, 1 - c)))
        for cp in cps:
            cp.start()
        for cp in cps:
            cp.wait()

    return pl.pallas_call(
        body, name="grad_swap_halves", in_specs=[_ANY] * n, out_specs=[_ANY] * n,
        out_shape=[jax.ShapeDtypeStruct((N_CHIPS, g.shape[1] // 2, g.shape[2]), g.dtype) for g in grads],
        scratch_shapes=[pltpu.SemaphoreType.DMA((n,)), pltpu.SemaphoreType.DMA((n,))],
        compiler_params=pltpu.CompilerParams(has_side_effects=True),
    )(*grads)


def _exchange_chips(parts):
    n = len(parts)

    def body(*refs):
        srcs, outs = refs[:n], refs[n:2 * n]
        send_sems, recv_sems = refs[2 * n:]
        x, y, c, chips = _place()
        cps = [_remote(srcs[i].at[2 * chip[0] + chip[1]], outs[i].at[j], send_sems, recv_sems, 3 * i + j, (*chip, c))
               for i in range(n) for j, chip in enumerate(chips)]
        for cp in cps:
            cp.start()
        for cp in cps:
            cp.wait()

    return pl.pallas_call(
        body, name="grad_exchange_chips", in_specs=[_ANY] * n, out_specs=[_ANY] * n,
        out_shape=[jax.ShapeDtypeStruct((3,) + p.shape[1:], p.dtype) for p in parts],
        scratch_shapes=[pltpu.SemaphoreType.DMA((3 * n,)), pltpu.SemaphoreType.DMA((3 * n,))],
        compiler_params=pltpu.CompilerParams(has_side_effects=True),
    )(*parts)


def _join_halves(fulls):
    n = len(fulls)

    def body(*refs):
        srcs, outs = refs[:n], refs[n:2 * n]
        send_sems, recv_sems = refs[2 * n:]
        x, y, c, _ = _place()
        sibling = (x, y, 1 - c)
        cps = []
        for i in range(n):
            h = fulls[i].shape[0] // 2
            cps.append(_remote(srcs[i].at[pl.ds(c * h, h), :], outs[i].at[pl.ds(c * h, h), :], send_sems, recv_sems, i,
                               sibling))
        for cp in cps:
            cp.start()
        for i in range(n):
            h = fulls[i].shape[0] // 2
            theirs = outs[i].at[pl.ds((1 - c) * h, h), :]
            _remote(theirs, theirs, send_sems, recv_sems, i, sibling).wait_recv()
        for cp in cps:
            cp.wait_send()

    return pl.pallas_call(
        body, name="grad_join_halves", in_specs=[_ANY] * n, out_specs=[_ANY] * n,
        out_shape=[jax.ShapeDtypeStruct(f.shape, f.dtype) for f in fulls],
        input_output_aliases={i: i for i in range(n)},
        scratch_shapes=[pltpu.SemaphoreType.DMA((n,)), pltpu.SemaphoreType.DMA((n,))],
        compiler_params=pltpu.CompilerParams(has_side_effects=True),
    )(*fulls)


def _half_tile(h):
    return h if h <= 512 else _pick_tile(h, 512, ROW_ALIGN)


def _sum_halves(g, r1, c_idx, name):
    _, R, C = g.shape
    H = R // 2
    tr = _half_tile(H)
    nblk = H // tr

    def body(c_ref, g_ref, r_ref, p_ref):
        p_ref[...] = (g_ref[...] + r_ref[...]).astype(BF16)

    half = pl.BlockSpec((None, tr, C), lambda s, i, c_ref: (s, c_ref[0] * nblk + i, 0))
    plain = pl.BlockSpec((None, tr, C), lambda s, i, c_ref: (s, i, 0))
    return pl.pallas_call(
        body, name=name,
        grid_spec=pltpu.PrefetchScalarGridSpec(num_scalar_prefetch=1, grid=(N_CHIPS, nblk), in_specs=[half, plain],
                                               out_specs=plain),
        out_shape=jax.ShapeDtypeStruct((N_CHIPS, H, C), BF16),
        compiler_params=_params(("parallel", "parallel")),
    )(c_idx, g, r1)


def _sum_chips(g, r1, r2, idx, name):
    _, R, C = g.shape
    H = R // 2
    tr = _half_tile(H)
    nblk = H // tr

    def body(idx_ref, g_ref, r1_ref, r2_ref, o_ref):
        o_ref[...] = (((g_ref[...] + r1_ref[...]) + r2_ref[0].astype(F32)) + r2_ref[1].astype(F32)) + r2_ref[2].astype(F32)

    return pl.pallas_call(
        body, name=name,
        grid_spec=pltpu.PrefetchScalarGridSpec(
            num_scalar_prefetch=1, grid=(nblk,),
            in_specs=[pl.BlockSpec((None, tr, C), lambda i, idx_ref: (idx_ref[0], idx_ref[1] * nblk + i, 0)),
                      pl.BlockSpec((None, tr, C), lambda i, idx_ref: (idx_ref[0], i, 0)),
                      pl.BlockSpec((3, tr, C), lambda i, idx_ref: (0, i, 0))],
            out_specs=pl.BlockSpec((tr, C), lambda i, idx_ref: (idx_ref[1] * nblk + i, 0))),
        out_shape=jax.ShapeDtypeStruct((R, C), F32),
        compiler_params=_params(("parallel",)),
    )(idx, g, r1, r2)


def _all_reduce_small(v, n_fold, fold_rows, fold_at):
    M, N = v.shape

    def body(x_ref, tot_ref, fold_ref, all_ref, send_sems, recv_sems, local_sem):
        x, y, c, chips = _place()
        me, sibling = (x, y, c), (x, y, 1 - c)

        def rows(px, py, pc):
            return all_ref.at[pl.ds((4 * px + 2 * py + pc) * M, M), :]

        def copy(k, block, to, src=None):
            return _remote(rows(*block) if src is None else src, rows(*block), send_sems, recv_sems, k, to)

        mine = pltpu.make_async_copy(x_ref, rows(*me), local_sem)
        mine.start()
        first = [copy(0, me, sibling, src=x_ref)]
        first += [copy(1 + j, me, (*chip, c), src=x_ref) for j, chip in enumerate(chips)]
        for cp in first:
            cp.start()
        passed = [copy(4 + j, (*chip, c), sibling) for j, chip in enumerate(chips)]
        for j, chip in enumerate(chips):
            copy(1 + j, (*chip, c), me).wait_recv()
            passed[j].start()
        copy(0, sibling, me).wait_recv()
        for j, chip in enumerate(chips):
            copy(4 + j, (*chip, 1 - c), me).wait_recv()
        for cp in first + passed:
            cp.wait_send()
        mine.wait()
        tot = all_ref[0:M, :]
        for d in range(1, 8):
            tot = tot + all_ref[d * M:(d + 1) * M, :]
        tot_ref[...] = tot
        f = tot[fold_at:fold_at + fold_rows, :]
        for e in range(1, n_fold):
            f = f + tot[fold_at + e * fold_rows:fold_at + (e + 1) * fold_rows, :]
        fold_ref[...] = f

    vm = pl.BlockSpec(memory_space=pltpu.VMEM)
    return pl.pallas_call(
        body, name="all_reduce_small", in_specs=[vm], out_specs=[vm, vm],
        out_shape=[jax.ShapeDtypeStruct((M, N), F32), jax.ShapeDtypeStruct((fold_rows, N), F32)],
        scratch_shapes=[pltpu.VMEM((8 * M, N), F32), pltpu.SemaphoreType.DMA((7,)), pltpu.SemaphoreType.DMA((7,)),
                        pltpu.SemaphoreType.DMA],
        compiler_params=pltpu.CompilerParams(has_side_effects=True, vmem_limit_bytes=VMEM_LIMIT),
    )(v)


def _as_rows(a, width):
    flat = a.reshape(-1)
    pad = (-flat.shape[0]) % width
    if pad:
        flat = jnp.concatenate([flat, jnp.zeros((pad,), flat.dtype)])
    return flat.reshape(-1, width)


class _Layout:
    def __init__(self, width, total_mult):
        self.width, self.total_mult = width, total_mult
        self.offsets, self.shapes, self.rows = {}, {}, 0

    def add(self, name, shape):
        r = -(-math.prod(shape) // self.width)
        self.offsets[name], self.shapes[name] = (self.rows, r), tuple(shape)
        self.rows += r

    def align(self, mult):
        gap = (-self.rows) % mult
        if gap:
            self.offsets[f"_gap{self.rows}"], self.shapes[f"_gap{self.rows}"] = (self.rows, gap), (gap, self.width)
            self.rows += gap
        return self.rows

    def pack(self, pieces):
        self.align(self.total_mult)
        parts = [_as_rows(pieces[n].astype(F32), self.width) if n in pieces else jnp.zeros(self.shapes[n], F32)
                 for n in self.offsets]
        return jnp.concatenate(parts, axis=0)

    def unpack(self, buf, name):
        off, r = self.offsets[name]
        shape = self.shapes[name]
        return buf[off:off + r].reshape(-1)[:math.prod(shape)].reshape(shape)


_BIG = ["ffn1_w1", "ffn1_w3", "ffn1_w2", "w_in", "ssm_glu_a", "ssm_glu_b", "w_out", "ffn2_w1", "ffn2_w3", "ffn2_w2"]
_SMALL = ["ffn1_norm", "mix_norm", "ffn2_norm", "final_norm", "attn_sinks", "ssm_a_re", "ssm_a_im", "ssm_log_step",
          "ssm_b_re", "ssm_b_im", "ssm_c_re", "ssm_c_im", "ssm_d"]
_WEIGHTS = ["meta_tokens", "ffn1_norm", "ffn1_w1", "ffn1_w3", "ffn1_w2", "mix_norm", "w_in", "attn_sinks", "ssm_a_re",
            "ssm_a_im", "ssm_log_step", "ssm_b_re", "ssm_b_im", "ssm_c_re", "ssm_c_im", "ssm_d", "ssm_glu_a",
            "ssm_glu_b", "w_out", "ffn2_norm", "ffn2_w1", "ffn2_w3", "ffn2_w2", "final_norm"]


def _kv_interleave(w, kv_heads):
    kvw = kv_heads * HEAD_DIM
    lead = w.shape[:-1]
    k = w[..., 0:kvw].reshape(lead + (kv_heads, 1, HEAD_DIM))
    v = w[..., kvw:2 * kvw].reshape(lead + (kv_heads, 1, HEAD_DIM))
    return jnp.concatenate([jnp.concatenate([k, v], axis=-2).reshape(lead + (2 * kvw,)), w[..., 2 * kvw:]], axis=-1)


def _kv_deinterleave(w, kv_heads):
    kvw = kv_heads * HEAD_DIM
    lead = w.shape[:-1]
    kv = w[..., 0:2 * kvw].reshape(lead + (kv_heads, 2, HEAD_DIM))
    return jnp.concatenate([kv[..., 0, :].reshape(lead + (kvw,)), kv[..., 1, :].reshape(lead + (kvw,)), w[..., 2 * kvw:]],
                           axis=-1)


def _step(x, target, w, m, v):
    B, S, D = x.shape
    L = S + N_META
    T = B * L
    H = D // HEAD_DIM
    KV = H // Q_PER_KV
    SW = D // 2
    tm = _pick_tile(L, ROW_TILE_CAP, ROW_ALIGN)
    rc = _pick_tile(L, ROW_TILE_CAP // B, 4) * B
    tw = _pick_tile(T, 3 * ROW_TILE_CAP, ROW_ALIGN)
    my_c = lax.axis_index("c")
    my_slot = 2 * lax.axis_index("x") + lax.axis_index("y")

    shards = [w[n][0].astype(BF16) for n in _BIG] + [w["meta_tokens"]]
    gathered = _gather_weights(shards)
    ws = dict(zip(_BIG, gathered[:-1]))
    meta = jnp.transpose(gathered[-1], (1, 0, 2)).reshape(N_META, D)
    w_kvu = _kv_interleave(ws["w_in"][1], KV)

    g_ffn1, g_mix, g_ffn2 = w["ffn1_norm"], w["mix_norm"], w["ffn2_norm"]
    g_final = w["final_norm"].reshape(1, D)

    h0 = jnp.concatenate([jnp.broadcast_to(meta[None], (B, N_META, D)), x], axis=1).reshape(T, D)

    def ffn_fwd(h, g, tag):
        n = _rmsnorm_fwd(h, g, tm, f"{tag}_norm")
        a, c, s = _ffn_up(n, ws[f"{tag}_w1"], ws[f"{tag}_w3"], tm, f"{tag}_up")
        return _ffn_down(s, ws[f"{tag}_w2"], h, tm, f"{tag}_down"), (n, a, c, s)

    h1, saved1 = ffn_fwd(h0, g_ffn1, "ffn1")
    hn = _rmsnorm_fwd(h1, g_mix, tm, "mix_norm")
    q = _mm_colslots(hn, ws["w_in"], BF16, "w_in_q", tm, first=0, count=1, scale=HEAD_DIM ** -0.5)
    kvu = _mm_plain(hn, w_kvu, "nn", F32, "w_in_kvu", tm)
    gates = _mm_colslots(hn, ws["w_in"], F32, "w_in_gates", tm, first=2, count=2)

    sinks = w["attn_sinks"].reshape(KV, Q_PER_KV, 1, 1)
    sink_col = jnp.broadcast_to(sinks, (KV, Q_PER_KV, BLOCK, 1)).reshape(KV, Q_PER_KV * BLOCK, 1)
    sink_meta = jnp.broadcast_to(sinks, (KV, Q_PER_KV, N_META, 1)).reshape(KV, Q_PER_KV * N_META, 1)
    attn = _attn_fwd(q, kvu, sink_col, sink_meta, B, "attn_fwd")

    def to_time_major(a2d):
        return jnp.transpose(a2d.reshape(B, L, a2d.shape[-1]), (1, 0, 2)).reshape(T, a2d.shape[-1])

    def to_batch_major(a2d):
        return jnp.transpose(a2d.reshape(L, B, a2d.shape[-1]), (1, 0, 2)).reshape(T, a2d.shape[-1])

    ssm_args = (w["ssm_a_re"][0], w["ssm_a_im"][0], w["ssm_log_step"][0], w["ssm_b_re"][0], w["ssm_b_im"][0],
                w["ssm_c_re"][0], w["ssm_c_im"][0])
    (lam, bmat, cmat), ssm_vjp = jax.vjp(_ssm_matrices, *ssm_args)
    bmat16, cmat16 = bmat.astype(BF16), cmat.astype(BF16)
    u_t = to_time_major(kvu[:, SW:])
    y_t, xs, xp = _ssm_fwd(u_t, bmat16, cmat16, w["ssm_d"], _scan_tables(lam, B, False), B, rc, "ssm_fwd")
    y0 = to_batch_major(y_t)
    yg = _gelu_fwd(y0, tm, "gelu_fwd")
    ga = _mm_colslots(yg, ws["ssm_glu_a"], F32, "glu_a", tm)
    gb = _mm_colslots(yg, ws["ssm_glu_b"], F32, "glu_b", tm)
    merged = _merge_fwd(gates, attn, ga, gb, tm, "merge_fwd")
    h2 = _mm_rowslots(merged, ws["w_out"], h1, tm, "w_out")
    h3, saved2 = ffn_fwd(h2, g_ffn2, "ffn2")
    dh3, dh3b, dg_final, loss_row = _loss_head(h3, g_final, target, tm, "loss_head")

    grads = {}

    def ffn_bwd(h, g, saved, dh, dhb, tag):
        n, a, c, s = saved
        w1, w3, w2 = ws[f"{tag}_w1"], ws[f"{tag}_w3"], ws[f"{tag}_w2"]
        grads[f"{tag}_w2"] = _wgrad_hidden_rows(s, dhb, tw, f"{tag}_dw2", 0.5)
        da, dc = _ffn_dhidden(dhb, w2, a, c, tm, f"{tag}_dhidden")
        grads[f"{tag}_w1"] = _wgrad_hidden_cols(n, da, tw, f"{tag}_dw1")
        grads[f"{tag}_w3"] = _wgrad_hidden_cols(n, dc, tw, f"{tag}_dw3")
        dn = _ffn_dn(da, w1, dc, w3, tm, f"{tag}_dn")
        dh_in, dhb_in, grads[f"{tag}_norm"] = _rmsnorm_bwd(h, g, dn, dh, tm, f"{tag}_norm_bwd")
        return dh_in, dhb_in

    dh2, dh2b = ffn_bwd(h2, g_ffn2, saved2, dh3, dh3b, "ffn2")

    grads["w_out"] = _wgrad_rowslots(merged, dh2b, tw, "dw_out")
    dmerged = _mm_rowslots_t(dh2b, ws["w_out"], tm, "dmerged")
    dattn, dgat, dgss, dga, dgb = _merge_bwd(dmerged, gates, attn, ga, gb, tm, "merge_bwd")
    grads["ssm_glu_a"] = _wgrad_colslots(yg, dga, tw, "dglu_a")
    grads["ssm_glu_b"] = _wgrad_colslots(yg, dgb, tw, "dglu_b")
    dyg = _mm_colslots_t([(dga, ws["ssm_glu_a"]), (dgb, ws["ssm_glu_b"])], tm, "dyg")
    dy0 = _gelu_bwd(dyg, y0, tm, "gelu_bwd")
    du_t, dbmat, dcmat, dlam, dd = _ssm_bwd(to_time_major(dy0), u_t, xs, xp, bmat16, cmat16, w["ssm_d"],
                                            _scan_tables(lam, B, True), B, rc, "ssm_bwd")
    d_ssm = ssm_vjp((dlam[:, 0, :], dbmat, dcmat))
    for n, gval in zip(["ssm_a_re", "ssm_a_im", "ssm_log_step", "ssm_b_re", "ssm_b_im", "ssm_c_re", "ssm_c_im"], d_ssm):
        grads[n] = gval[None]
    grads["ssm_d"] = dd

    dq, dkv, dsink = _attn_bwd(q, kvu, attn, dattn, sink_col, sink_meta, B, "attn_bwd")
    grads["attn_sinks"] = dsink[:, 0:Q_PER_KV, 0].reshape(1, H)
    dkvu = jnp.concatenate([dkv, to_batch_major(du_t).astype(BF16)], axis=1)
    pieces = [dq, dkvu, dgat, dgss]
    dw_in = [_wgrad_plain(hn, p, f"dw_in_{k}", tw) for k, p in enumerate(pieces)]
    dw_in[1] = _kv_deinterleave(dw_in[1], KV)
    grads["w_in"] = jnp.stack(dw_in)
    w_in_parts = [ws["w_in"][0], w_kvu, ws["w_in"][2], ws["w_in"][3]]
    whole = _spec((D, D), lambda i: (0, 0))
    dhn = _mm("dhn", (T // tm,), None, "nt",
              [(p, _spec((tm, D), lambda i: (i, 0)), wp, whole) for p, wp in zip(pieces, w_in_parts)],
              jax.ShapeDtypeStruct((T, D), F32), _spec((tm, D), lambda i: (i, 0)))
    dh1, dh1b, grads["mix_norm"] = _rmsnorm_bwd(h1, g_mix, dhn, dh2, tm, "mix_norm_bwd")
    dh0, _ = ffn_bwd(h0, g_ffn1, saved1, dh1, dh1b, "ffn1")
    dh0 = dh0.reshape(B, L, D)
    grad_x = dh0[:, N_META:, :]

    grads["final_norm"] = dg_final
    slay = _Layout(D, 8)
    for n in _SMALL:
        slay.add(n, w[n].shape)
    slay.add("loss", (1, D))
    meta_at = slay.align(8)
    slay.add("meta", (B * N_META, D))
    small = slay.pack({**{n: grads[n] for n in _SMALL}, "loss": loss_row, "meta": dh0[:, :N_META, :]})
    tot_small, dmeta = _all_reduce_small(small, B, N_META, meta_at)
    loss = slay.unpack(tot_small, "loss")[0, 0]
    for n in _SMALL:
        grads[n] = slay.unpack(tot_small, n)
    cw = D // N_CHIPS
    grads["meta_tokens"] = lax.dynamic_slice_in_dim(dmeta, my_slot * cw, cw, axis=1)

    glist = [grads[n] for n in _BIG]
    c_idx = my_c.reshape(1).astype(jnp.int32)
    idx = jnp.stack([my_slot, my_c]).astype(jnp.int32)
    r1 = _swap_halves(glist)
    parts = [_sum_halves(g, r, c_idx, f"grad_sum_halves_{n}") for n, g, r in zip(_BIG, glist, r1)]
    r2 = _exchange_chips(parts)
    halves = [_sum_chips(g, ra, rb, idx, f"grad_sum_chips_{n}") for n, g, ra, rb in zip(_BIG, glist, r1, r2)]
    for n, f in zip(_BIG, _join_halves(halves)):
        grads[n] = f[None]

    delta, new_m, new_v = {}, {}, {}
    for n in _BIG + ["meta_tokens"]:
        shp = w[n].shape
        two = (shp[-2], shp[-1])
        d_, m_, v_ = _adamw(w[n].reshape(two), grads[n].reshape(two), m[n].reshape(two), v[n].reshape(two), f"adamw_{n}")
        delta[n], new_m[n], new_v[n] = d_.reshape(shp), m_.reshape(shp), v_.reshape(shp)
        grads[n] = grads[n].reshape(shp)
    play = _Layout(D, 8)
    for n in _SMALL:
        play.add(n, w[n].shape)
    d_, m_, v_ = _adamw(play.pack({n: w[n] for n in _SMALL}), play.pack({n: grads[n] for n in _SMALL}),
                        play.pack({n: m[n] for n in _SMALL}), play.pack({n: v[n] for n in _SMALL}), "adamw_small")
    for n in _SMALL:
        delta[n], new_m[n], new_v[n] = play.unpack(d_, n), play.unpack(m_, n), play.unpack(v_, n)
        grads[n] = grads[n].reshape(w[n].shape)

    return (loss, grad_x, *[grads[n] for n in _WEIGHTS], *[delta[n] for n in _WEIGHTS],
            *[new_m[n] for n in _WEIGHTS], *[new_v[n] for n in _WEIGHTS])


def kernel(x, meta_tokens, ffn1_norm, ffn1_w1, ffn1_w3, ffn1_w2, mix_norm, w_in, attn_sinks, ssm_a_re, ssm_a_im, ssm_log_step, ssm_b_re, ssm_b_im, ssm_c_re, ssm_c_im, ssm_d, ssm_glu_a, ssm_glu_b, w_out, ffn2_norm, ffn2_w1, ffn2_w3, ffn2_w2, final_norm, loss_target, m_meta_tokens, m_ffn1_norm, m_ffn1_w1, m_ffn1_w3, m_ffn1_w2, m_mix_norm, m_w_in, m_attn_sinks, m_ssm_a_re, m_ssm_a_im, m_ssm_log_step, m_ssm_b_re, m_ssm_b_im, m_ssm_c_re, m_ssm_c_im, m_ssm_d, m_ssm_glu_a, m_ssm_glu_b, m_w_out, m_ffn2_norm, m_ffn2_w1, m_ffn2_w3, m_ffn2_w2, m_final_norm, v_meta_tokens, v_ffn1_norm, v_ffn1_w1, v_ffn1_w3, v_ffn1_w2, v_mix_norm, v_w_in, v_attn_sinks, v_ssm_a_re, v_ssm_a_im, v_ssm_log_step, v_ssm_b_re, v_ssm_b_im, v_ssm_c_re, v_ssm_c_im, v_ssm_d, v_ssm_glu_a, v_ssm_glu_b, v_w_out, v_ffn2_norm, v_ffn2_w1, v_ffn2_w3, v_ffn2_w2, v_final_norm):
    args = locals()
    w = {n: args[n] for n in _WEIGHTS}
    m = {n: args["m_" + n] for n in _WEIGHTS}
    v = {n: args["v_" + n] for n in _WEIGHTS}
    return _step(x, loss_target, w, m, v)
```

```python
import math

import jax
import jax.numpy as jnp
from jax import lax
from jax.experimental import pallas as pl
from jax.experimental.pallas import tpu as pltpu

F32 = jnp.float32
BF16 = jnp.bfloat16
MESH_IDS = pl.DeviceIdType.MESH

N_CHIPS = 4
N_META = 16
HEAD_DIM = 64
Q_PER_KV = 4
QW = Q_PER_KV * HEAD_DIM
BLOCK = 128
SSM_GROUP = 16
SSM_STATE = 64
SSM_LANES = 128
GROUPS_PER_COL = SSM_LANES // SSM_GROUP
STATE_LANES = GROUPS_PER_COL * SSM_STATE
NORM_EPS = 1e-6
NEG_INF = -1e30
ADAM_LR, ADAM_B1, ADAM_B2, ADAM_EPS, ADAM_WD, ADAM_STEP = 0.001, 0.9, 0.999, 1e-08, 0.01, 10
GELU_C = math.sqrt(2.0 / math.pi)
ROW_ALIGN = 16
VMEM_LIMIT = 56 * 1024 * 1024
ROW_TILE_CAP = 688

_NN = (((1,), (0,)), ((), ()))
_NT = (((1,), (1,)), ((), ()))
_TN = (((0,), (0,)), ((), ()))
_DIMS = {"nn": _NN, "nt": _NT, "tn": _TN}


def _params(sem, **kw):
    return pltpu.CompilerParams(dimension_semantics=sem, vmem_limit_bytes=VMEM_LIMIT, **kw)


def _pick_tile(n, cap, mult):
    best = None
    for t in range(mult, min(n, cap) + 1, mult):
        if n % t == 0:
            best = t
    if best is None:
        raise ValueError(f"no tile for {n} (cap {cap}, multiple of {mult})")
    return best


def _sigmoid(x):
    return 1.0 / (1.0 + jnp.exp(-x))


def _spec(block, index_map):
    return pl.BlockSpec(block, index_map)


def _sum_dots(ins, mode):
    tot = None
    for p in range(len(ins) // 2):
        a_ref, b_ref = ins[2 * p], ins[2 * p + 1]
        for sl in ([None] if len(a_ref.shape) == 2 else range(a_ref.shape[0])):
            a = (a_ref[...] if sl is None else a_ref[sl]).astype(BF16)
            b = (b_ref[...] if sl is None else b_ref[sl]).astype(BF16)
            d = lax.dot_general(a, b, _DIMS[mode], preferred_element_type=F32)
            tot = d if tot is None else tot + d
    return tot


def _mm(name, grid, kaxis, mode, pairs, out_shape, out_spec, scale=1.0, res=None):
    npairs = len(pairs)
    has_res = res is not None
    gk = 1 if kaxis is None else grid[kaxis]
    acc_shape = tuple(d for d in out_spec.block_shape if d is not None)

    def body(*refs):
        res_ref = refs[2 * npairs] if has_res else None
        o_ref = refs[2 * npairs + has_res]
        tot = _sum_dots(refs[:2 * npairs], mode)

        def finish(acc):
            r = acc * scale if scale != 1.0 else acc
            if has_res:
                r = res_ref[...] + r
            o_ref[...] = r.astype(o_ref.dtype)

        if gk == 1:
            finish(tot)
        else:
            acc_ref = refs[-1]
            k = pl.program_id(kaxis)

            @pl.when(k == 0)
            def _():
                acc_ref[...] = tot

            @pl.when(k > 0)
            def _():
                acc_ref[...] += tot

            @pl.when(k == gk - 1)
            def _():
                finish(acc_ref[...])

    in_specs, args = [], []
    for a, a_spec, b, b_spec in pairs:
        in_specs += [a_spec, b_spec]
        args += [a, b]
    if has_res:
        in_specs.append(res[1])
        args.append(res[0])
    sem = tuple("arbitrary" if ax == kaxis else "parallel" for ax in range(len(grid)))
    return pl.pallas_call(
        body, name=name, grid=grid, in_specs=in_specs, out_specs=out_spec, out_shape=out_shape,
        scratch_shapes=[pltpu.VMEM(acc_shape, F32)] if gk > 1 else [],
        compiler_params=_params(sem),
    )(*args)


def _rows(tm, width):
    return _spec((tm, width), lambda i, s: (i, 0))


def _mm_plain(a, b, mode, out_dtype, name, tm, scale=1.0):
    M, K = a.shape
    N = b.shape[1] if mode == "nn" else b.shape[0]
    return _mm(name, (M // tm,), None, mode,
               [(a, _spec((tm, K), lambda i: (i, 0)), b, _spec(b.shape, lambda i: (0, 0)))],
               jax.ShapeDtypeStruct((M, N), out_dtype), _spec((tm, N), lambda i: (i, 0)), scale=scale)


def _wgrad_plain(a, b, name, tr):
    R, M = a.shape
    N = b.shape[1]
    return _mm(name, (R // tr,), 0, "tn",
               [(a, _spec((tr, M), lambda r: (r, 0)), b, _spec((tr, N), lambda r: (r, 0)))],
               jax.ShapeDtypeStruct((M, N), F32), _spec((M, N), lambda r: (0, 0)))


def _rmsnorm_fwd(h, g, tm, name):
    T, D = h.shape

    def body(h_ref, g_ref, o_ref):
        x = h_ref[...]
        r = lax.rsqrt(jnp.mean(x * x, axis=-1, keepdims=True) + NORM_EPS)
        o_ref[...] = ((x * r) * g_ref[...]).astype(BF16)

    return pl.pallas_call(
        body, name=name, grid=(T // tm,),
        in_specs=[pl.BlockSpec((tm, D), lambda i: (i, 0)), pl.BlockSpec((1, D), lambda i: (0, 0))],
        out_specs=pl.BlockSpec((tm, D), lambda i: (i, 0)),
        out_shape=jax.ShapeDtypeStruct((T, D), BF16),
        compiler_params=_params(("parallel",)),
    )(h, g)


def _fold8(x):
    return jnp.sum(x.reshape(x.shape[0] // 8, 8, x.shape[1]), axis=0)


def _mm_norm_bwd(name, mode, pairs, h, g, dres, tm):
    T, D = h.shape
    nt = T // tm
    npairs = len(pairs)

    def body(*refs):
        h_ref, g_ref, dres_ref, dh_ref, dhb_ref, dg_ref, acc_ref = refs[2 * npairs:]
        i = pl.program_id(0)
        x = h_ref[...]
        r = lax.rsqrt(jnp.mean(x * x, axis=-1, keepdims=True) + NORM_EPS)
        xhat = x * r
        dy = _sum_dots(refs[:2 * npairs], mode)
        dxhat = dy * g_ref[...]
        dx = r * (dxhat - xhat * jnp.mean(dxhat * xhat, axis=-1, keepdims=True))
        dh = dres_ref[...] + dx
        dh_ref[...] = dh
        dhb_ref[...] = dh.astype(BF16)
        part = _fold8(dy * xhat)

        @pl.when(i == 0)
        def _():
            acc_ref[...] = part

        @pl.when(i > 0)
        def _():
            acc_ref[...] += part

        @pl.when(i == nt - 1)
        def _():
            dg_ref[...] = jnp.sum(acc_ref[...], axis=0, keepdims=True)

    row = pl.BlockSpec((tm, D), lambda i: (i, 0))
    vec = pl.BlockSpec((1, D), lambda i: (0, 0))
    in_specs, args = [], []
    for a, a_spec, b, b_spec in pairs:
        in_specs += [a_spec, b_spec]
        args += [a, b]
    return pl.pallas_call(
        body, name=name, grid=(nt,),
        in_specs=in_specs + [row, vec, row],
        out_specs=[row, row, vec],
        out_shape=[jax.ShapeDtypeStruct((T, D), F32), jax.ShapeDtypeStruct((T, D), BF16),
                   jax.ShapeDtypeStruct((1, D), F32)],
        scratch_shapes=[pltpu.VMEM((8, D), F32)],
        compiler_params=_params(("arbitrary",)),
    )(*args, h, g, dres)


def _ffn_up(n, w1, w3, tm, name):
    T, D = n.shape
    Fs = w1.shape[2]

    def body(n_ref, w1_ref, w3_ref, a_ref, c_ref, s_ref):
        x = n_ref[...]
        a = jnp.dot(x, w1_ref[...], preferred_element_type=F32)
        c = jnp.dot(x, w3_ref[...], preferred_element_type=F32)
        a_ref[...] = a.astype(BF16)
        c_ref[...] = c.astype(BF16)
        s_ref[...] = (a * _sigmoid(a) * c).astype(BF16)

    w_spec = _spec((None, D, Fs), lambda s, i: (s, 0, 0))
    o_spec = _spec((None, tm, Fs), lambda s, i: (s, i, 0))
    o_shape = jax.ShapeDtypeStruct((N_CHIPS, T, Fs), BF16)
    return pl.pallas_call(
        body, name=name, grid=(N_CHIPS, T // tm),
        in_specs=[_spec((tm, D), lambda s, i: (i, 0)), w_spec, w_spec],
        out_specs=[o_spec, o_spec, o_spec], out_shape=[o_shape, o_shape, o_shape],
        compiler_params=_params(("parallel", "parallel")),
    )(n, w1, w3)


def _ffn_down(s, w2, h, tm, name):
    _, T, Fs = s.shape
    D = w2.shape[2]
    row = _spec((tm, D), lambda i: (i, 0))
    return _mm(name, (T // tm,), None, "nn",
               [(s, _spec((N_CHIPS, tm, Fs), lambda i: (0, i, 0)), w2, _spec((N_CHIPS, Fs, D), lambda i: (0, 0, 0)))],
               jax.ShapeDtypeStruct((T, D), F32), row, scale=0.5, res=(h, row))


def _ffn_dhidden(dhb, w2, a, c, tm, name):
    T, D = dhb.shape
    Fs = w2.shape[1]

    def body(dh_ref, w2_ref, a_ref, c_ref, da_ref, dc_ref):
        d = 0.5 * lax.dot_general(dh_ref[...], w2_ref[...], _NT, preferred_element_type=F32)
        av = a_ref[...].astype(F32)
        cv = c_ref[...].astype(F32)
        sg = _sigmoid(av)
        da_ref[...] = (d * cv * (sg * (1.0 + av * (1.0 - sg)))).astype(BF16)
        dc_ref[...] = (d * (av * sg)).astype(BF16)

    h_spec = _spec((None, tm, Fs), lambda s, i: (s, i, 0))
    o_shape = jax.ShapeDtypeStruct((N_CHIPS, T, Fs), BF16)
    return pl.pallas_call(
        body, name=name, grid=(N_CHIPS, T // tm),
        in_specs=[_spec((tm, D), lambda s, i: (i, 0)), _spec((None, Fs, D), lambda s, i: (s, 0, 0)), h_spec, h_spec],
        out_specs=[h_spec, h_spec], out_shape=[o_shape, o_shape],
        compiler_params=_params(("parallel", "parallel")),
    )(dhb, w2, a, c)


def _wgrad_hidden_rows(s, dhb, tr, name, scale):
    _, T, Fs = s.shape
    D = dhb.shape[1]
    return _mm(name, (N_CHIPS, T // tr), 1, "tn",
               [(s, _spec((None, tr, Fs), lambda k, r: (k, r, 0)), dhb, _spec((tr, D), lambda k, r: (r, 0)))],
               jax.ShapeDtypeStruct((N_CHIPS, Fs, D), F32), _spec((None, Fs, D), lambda k, r: (k, 0, 0)), scale=scale)


def _wgrad_hidden_cols(n, da, tr, name):
    T, D = n.shape
    Fs = da.shape[2]
    return _mm(name, (N_CHIPS, T // tr), 1, "tn",
               [(n, _spec((tr, D), lambda k, r: (r, 0)), da, _spec((None, tr, Fs), lambda k, r: (k, r, 0)))],
               jax.ShapeDtypeStruct((N_CHIPS, D, Fs), F32), _spec((None, D, Fs), lambda k, r: (k, 0, 0)))


def _once(block, index_map):
    return pl.BlockSpec(block, index_map, pipeline_mode=pl.Buffered(1))


def _ffn_dn(da, w1, dc, w3, h, g, dres, tm, name):
    _, T, Fs = da.shape
    D = w1.shape[1]
    h_spec = _spec((N_CHIPS, tm, Fs), lambda i: (0, i, 0))
    w_spec = _once((N_CHIPS, D, Fs), lambda i: (0, 0, 0))
    return _mm_norm_bwd(name, "nt", [(da, h_spec, w1, w_spec), (dc, h_spec, w3, w_spec)], h, g, dres, tm)


def _mm_colslots(a, w, out_dtype, name, tm, first=0, count=N_CHIPS, scale=1.0):
    T, K = a.shape
    Ns = w.shape[2]
    return _mm(name, (T // tm, count), None, "nn",
               [(a, _spec((tm, K), lambda i, j: (i, 0)), w, _spec((None, K, Ns), lambda i, j: (first + j, 0, 0)))],
               jax.ShapeDtypeStruct((T, count * Ns), out_dtype), _spec((tm, Ns), lambda i, j: (i, j)), scale=scale)


def _wgrad_colslots(a, d, tr, name):
    T, K = a.shape
    Ns = d.shape[1] // N_CHIPS
    return _mm(name, (N_CHIPS, T // tr), 1, "tn",
               [(a, _spec((tr, K), lambda k, r: (r, 0)), d, _spec((tr, Ns), lambda k, r: (r, k)))],
               jax.ShapeDtypeStruct((N_CHIPS, K, Ns), F32), _spec((None, K, Ns), lambda k, r: (k, 0, 0)))


def _mm_colslots_t(pairs, tm, name):
    d0, w0 = pairs[0]
    T = d0.shape[0]
    K, Ns = w0.shape[1], w0.shape[2]
    d_spec = _spec((tm, Ns), lambda i, k: (i, k))
    w_spec = _spec((None, K, Ns), lambda i, k: (k, 0, 0))
    return _mm(name, (T // tm, N_CHIPS), 1, "nt", [(d, d_spec, w, w_spec) for d, w in pairs],
               jax.ShapeDtypeStruct((T, K), F32), _rows(tm, K))


def _mm_rowslots(a, w, h, tm, name):
    T = a.shape[0]
    Ks, N = w.shape[1], w.shape[2]
    return _mm(name, (T // tm, N_CHIPS), 1, "nn",
               [(a, _spec((tm, Ks), lambda i, k: (i, k)), w, _spec((None, Ks, N), lambda i, k: (k, 0, 0)))],
               jax.ShapeDtypeStruct((T, N), F32), _rows(tm, N), res=(h, _rows(tm, N)))


def _wgrad_rowslots(a, d, tr, name):
    T = a.shape[0]
    Ks = a.shape[1] // N_CHIPS
    N = d.shape[1]
    return _mm(name, (N_CHIPS, T // tr), 1, "tn",
               [(a, _spec((tr, Ks), lambda k, r: (r, k)), d, _spec((tr, N), lambda k, r: (r, 0)))],
               jax.ShapeDtypeStruct((N_CHIPS, Ks, N), F32), _spec((None, Ks, N), lambda k, r: (k, 0, 0)))


def _mm_rowslots_t(d, w, tm, name):
    T, N = d.shape
    Ks = w.shape[1]
    return _mm(name, (T // tm, N_CHIPS), None, "nt",
               [(d, _spec((tm, N), lambda i, j: (i, 0)), w, _spec((None, Ks, N), lambda i, j: (j, 0, 0)))],
               jax.ShapeDtypeStruct((T, N_CHIPS * Ks), F32), _spec((tm, Ks), lambda i, j: (i, j)))


def _gelu_parts(x):
    inner = GELU_C * (x + 0.044715 * (x * x * x))
    t = jnp.tanh(inner)
    return t, GELU_C * (1.0 + 3.0 * 0.044715 * (x * x))


def _gelu_fwd(y, tm, name):
    T, W = y.shape

    def body(y_ref, o_ref):
        x = y_ref[...]
        t, _ = _gelu_parts(x)
        o_ref[...] = (0.5 * x * (1.0 + t)).astype(BF16)

    spec = pl.BlockSpec((tm, W), lambda i: (i, 0))
    return pl.pallas_call(body, name=name, grid=(T // tm,), in_specs=[spec], out_specs=spec,
                          out_shape=jax.ShapeDtypeStruct((T, W), BF16),
                          compiler_params=_params(("parallel",)))(y)


def _gelu_bwd(dyg, y, tm, name):
    T, W = y.shape

    def body(d_ref, y_ref, o_ref):
        x = y_ref[...]
        t, dinner = _gelu_parts(x)
        o_ref[...] = d_ref[...] * (0.5 * (1.0 + t) + 0.5 * x * (1.0 - t * t) * dinner)

    spec = pl.BlockSpec((tm, W), lambda i: (i, 0))
    return pl.pallas_call(body, name=name, grid=(T // tm,), in_specs=[spec, spec], out_specs=spec,
                          out_shape=jax.ShapeDtypeStruct((T, W), F32),
                          compiler_params=_params(("parallel",)))(dyg, y)


def _merge_cols(D):
    cb = 512 if D % 512 == 0 else D
    return cb, D // cb


def _merge_fwd(gates, attn, ga, gb, tm, name):
    T, D = attn.shape
    cb, nc = _merge_cols(D)

    def body(gat_ref, gss_ref, attn_ref, ga_ref, gb_ref, o_ref):
        ssm = ga_ref[...] * _sigmoid(gb_ref[...])
        o_ref[...] = (_sigmoid(gat_ref[...]) * attn_ref[...] + _sigmoid(gss_ref[...]) * ssm).astype(BF16)

    def col(block):
        return pl.BlockSpec((tm, cb), lambda i, j: (i, block * nc + j))

    return pl.pallas_call(
        body, name=name, grid=(T // tm, nc),
        in_specs=[col(0), col(1), col(0), col(0), col(0)],
        out_specs=col(0), out_shape=jax.ShapeDtypeStruct((T, D), BF16),
        compiler_params=_params(("parallel", "parallel")),
    )(gates, gates, attn, ga, gb)


def _merge_bwd(dm, gates, attn, ga, gb, tm, name):
    T, D = attn.shape
    cb, nc = _merge_cols(D)

    def body(dm_ref, gat_ref, gss_ref, attn_ref, ga_ref, gb_ref, dattn_ref, dgat_ref, dgss_ref, dga_ref, dgb_ref):
        d = dm_ref[...]
        sa = _sigmoid(gat_ref[...])
        ss = _sigmoid(gss_ref[...])
        sb = _sigmoid(gb_ref[...])
        gav = ga_ref[...]
        dattn_ref[...] = d * sa
        dgat_ref[...] = (d * attn_ref[...] * (sa * (1.0 - sa))).astype(BF16)
        dgss_ref[...] = (d * (gav * sb) * (ss * (1.0 - ss))).astype(BF16)
        dssm = d * ss
        dga_ref[...] = (dssm * sb).astype(BF16)
        dgb_ref[...] = (dssm * gav * (sb * (1.0 - sb))).astype(BF16)

    def col(block):
        return pl.BlockSpec((tm, cb), lambda i, j: (i, block * nc + j))

    b16 = jax.ShapeDtypeStruct((T, D), BF16)
    return pl.pallas_call(
        body, name=name, grid=(T // tm, nc),
        in_specs=[col(0), col(0), col(1), col(0), col(0), col(0)],
        out_specs=[col(0)] * 5,
        out_shape=[jax.ShapeDtypeStruct((T, D), F32), b16, b16, b16, b16],
        compiler_params=_params(("parallel", "parallel")),
    )(dm, gates, gates, attn, ga, gb)


def _loss_head(h, g, target, tm, name):
    T, D = h.shape
    B, S, _ = target.shape
    L = S + N_META
    nt = T // tm
    tpe = L // tm

    def body(h_ref, g_ref, t_hbm, dh_ref, dhb_ref, dg_ref, loss_ref, tbuf, acc_g, acc_l, sem):
        i = pl.program_id(0)
        b, j = i // tpe, i % tpe

        @pl.when(j == 0)
        def _():
            tbuf[0:N_META, :] = jnp.zeros((N_META, D), F32)
            cp = pltpu.make_async_copy(t_hbm.at[b, pl.ds(0, tm - N_META), :], tbuf.at[pl.ds(N_META, tm - N_META), :], sem)
            cp.start()
            cp.wait()

        @pl.when(j > 0)
        def _():
            cp = pltpu.make_async_copy(t_hbm.at[b, pl.ds(j * tm - N_META, tm), :], tbuf, sem)
            cp.start()
            cp.wait()

        x = h_ref[...]
        gv = g_ref[...]
        r = lax.rsqrt(jnp.mean(x * x, axis=-1, keepdims=True) + NORM_EPS)
        xhat = x * r
        pos = j * tm + lax.broadcasted_iota(jnp.int32, (tm, 1), 0)
        err = jnp.where(pos >= N_META, xhat * gv - tbuf[...], 0.0)
        dy = err * (1.0 / D)
        dxhat = dy * gv
        dh = r * (dxhat - xhat * jnp.mean(dxhat * xhat, axis=-1, keepdims=True))
        dh_ref[...] = dh
        dhb_ref[...] = dh.astype(BF16)
        pg = _fold8(dy * xhat)
        pe = _fold8(err * err)

        @pl.when(i == 0)
        def _():
            acc_g[...] = pg
            acc_l[...] = pe

        @pl.when(i > 0)
        def _():
            acc_g[...] += pg
            acc_l[...] += pe

        @pl.when(i == nt - 1)
        def _():
            dg_ref[...] = jnp.sum(acc_g[...], axis=0, keepdims=True)
            loss_ref[...] = jnp.full((1, D), (0.5 / D) * jnp.sum(acc_l[...]), F32)

    row = pl.BlockSpec((tm, D), lambda i: (i, 0))
    vec = pl.BlockSpec((1, D), lambda i: (0, 0))
    return pl.pallas_call(
        body, name=name, grid=(nt,),
        in_specs=[row, vec, pl.BlockSpec(memory_space=pl.ANY)], out_specs=[row, row, vec, vec],
        out_shape=[jax.ShapeDtypeStruct((T, D), F32), jax.ShapeDtypeStruct((T, D), BF16),
                   jax.ShapeDtypeStruct((1, D), F32), jax.ShapeDtypeStruct((1, D), F32)],
        scratch_shapes=[pltpu.VMEM((tm, D), F32), pltpu.VMEM((8, D), F32), pltpu.VMEM((8, D), F32),
                        pltpu.SemaphoreType.DMA],
        compiler_params=_params(("arbitrary",)),
    )(h, g, target)


def _heads_to_rows(blk):
    return jnp.concatenate([blk[:, g * HEAD_DIM:(g + 1) * HEAD_DIM] for g in range(Q_PER_KV)], axis=0)


def _rows_to_heads(x):
    rows = x.shape[0] // Q_PER_KV
    return jnp.concatenate([x[g * rows:(g + 1) * rows] for g in range(Q_PER_KV)], axis=1)


def _causal(R):
    qi = lax.broadcasted_iota(jnp.int32, (R, BLOCK), 0) & (BLOCK - 1)
    kj = lax.broadcasted_iota(jnp.int32, (R, BLOCK), 1)
    return kj <= qi


def _band_probs(s_band, s_m, sink):
    m = jnp.maximum(jnp.maximum(jnp.max(s_band, axis=-1, keepdims=True), jnp.max(s_m, axis=-1, keepdims=True)), sink)
    e_b, e_m, e_s = jnp.exp(s_band - m), jnp.exp(s_m - m), jnp.exp(sink - m)
    inv = 1.0 / (jnp.sum(e_b, axis=-1, keepdims=True) + jnp.sum(e_m, axis=-1, keepdims=True) + e_s)
    return e_b * inv, e_m * inv, e_s * inv


def _fold_band(tri, two):
    return jnp.where(tri, two[:, BLOCK:2 * BLOCK], two[:, 0:BLOCK])


def _unfold_band(tri, band):
    return jnp.concatenate([jnp.where(tri, 0.0, band), jnp.where(tri, band, 0.0)], axis=1)


def _meta_probs(qm, k_m, sink_m):
    R = qm.shape[0]
    s = lax.dot_general(qm, k_m, _NT, preferred_element_type=F32)
    qi = lax.broadcasted_iota(jnp.int32, (R, N_META), 0) & (N_META - 1)
    kj = lax.broadcasted_iota(jnp.int32, (R, N_META), 1)
    s = jnp.where(kj <= qi, s, NEG_INF)
    m = jnp.maximum(jnp.max(s, axis=-1, keepdims=True), sink_m)
    e, e_s = jnp.exp(s - m), jnp.exp(sink_m - m)
    inv = 1.0 / (jnp.sum(e, axis=-1, keepdims=True) + e_s)
    return e * inv, e_s * inv


def _block_start(n):
    return pl.multiple_of(N_META + n * BLOCK, ROW_ALIGN)


def _kv(blk):
    return blk[:, 0:HEAD_DIM], blk[:, HEAD_DIM:2 * HEAD_DIM]


def _attn_fwd(q, kv, sink_col, sink_meta, B, name):
    T, D = q.shape
    L = T // B
    KV = D // QW
    nb = (L - N_META) // BLOCK

    def body(q_ref, kv_ref, sk_ref, skm_ref, o_ref, kvs):
        kvs[...] = kv_ref[...].astype(BF16)
        k_m, v_m = _kv(kvs[0:N_META, :])
        p, _ = _meta_probs(_heads_to_rows(q_ref[0:N_META, :]), k_m, skm_ref[0])
        o_ref[0:N_META, :] = _rows_to_heads(jnp.dot(p.astype(BF16), v_m, preferred_element_type=F32))
        tri = _causal(BLOCK)

        def block(cur, first, keys):
            k2, v2 = _kv(kvs[keys, :])
            qb = _heads_to_rows(q_ref[pl.ds(cur, BLOCK), :])
            s2 = lax.dot_general(qb, k2, _NT, preferred_element_type=F32)
            sm = lax.dot_general(qb, k_m, _NT, preferred_element_type=F32)
            p2s, pms = [], []
            for g in range(Q_PER_KV):
                sl = slice(g * BLOCK, (g + 1) * BLOCK)
                s_band = jnp.where(tri, s2[sl], NEG_INF) if first else _fold_band(tri, s2[sl])
                p_b, p_m, _ = _band_probs(s_band, sm[sl], sk_ref[0, sl, :])
                p2s.append((p_b if first else _unfold_band(tri, p_b)).astype(BF16))
                pms.append(p_m.astype(BF16))
            o = (jnp.dot(jnp.concatenate(p2s, axis=0), v2, preferred_element_type=F32)
                 + jnp.dot(jnp.concatenate(pms, axis=0), v_m, preferred_element_type=F32))
            o_ref[pl.ds(cur, BLOCK), :] = _rows_to_heads(o)

        block(N_META, True, pl.ds(N_META, BLOCK))

        def step(n, carry):
            block(_block_start(n), False, pl.ds(_block_start(n - 1), 2 * BLOCK))
            return carry

        lax.fori_loop(1, nb, step, 0)

    q_spec = pl.BlockSpec((L, QW), lambda b, h: (b, h))
    return pl.pallas_call(
        body, name=name, grid=(B, KV),
        in_specs=[q_spec, pl.BlockSpec((L, 2 * HEAD_DIM), lambda b, h: (b, h)),
                  pl.BlockSpec((1, Q_PER_KV * BLOCK, 1), lambda b, h: (h, 0, 0)),
                  pl.BlockSpec((1, Q_PER_KV * N_META, 1), lambda b, h: (h, 0, 0))],
        out_specs=q_spec, out_shape=jax.ShapeDtypeStruct((T, D), F32),
        scratch_shapes=[pltpu.VMEM((L, 2 * HEAD_DIM), BF16)],
        compiler_params=_params(("parallel", "parallel")),
    )(q, kv, sink_col, sink_meta)


def _attn_bwd(q, kv, o, do, sink_col, sink_meta, B, name):
    T, D = q.shape
    L = T // B
    KV = D // QW
    nb = (L - N_META) // BLOCK
    R = Q_PER_KV * BLOCK
    scale = HEAD_DIM ** -0.5

    def head_totals(col, rows_per_head):
        rid = lax.broadcasted_iota(jnp.int32, (8, 128), 0)
        out = jnp.zeros((8, 128), F32)
        for g in range(Q_PER_KV):
            out = out + jnp.where(rid == g, jnp.sum(col[g * rows_per_head:(g + 1) * rows_per_head, :]), 0.0)
        return out

    def body(q_ref, kv_ref, o_ref, do_ref, sk_ref, skm_ref, dq_ref, dkv_ref, dsk_ref, kvs, acc, acc_sink):
        b = pl.program_id(1)
        kvs[...] = kv_ref[...].astype(BF16)
        acc[...] = jnp.zeros_like(acc)
        k_m, v_m = _kv(kvs[0:N_META, :])

        qm = _heads_to_rows(q_ref[0:N_META, :])
        dom = _heads_to_rows(do_ref[0:N_META, :])
        delta = jnp.sum(dom * _heads_to_rows(o_ref[0:N_META, :]), axis=-1, keepdims=True)
        p, p_s = _meta_probs(qm, k_m, skm_ref[0])
        domb = dom.astype(BF16)
        ds = (p * (lax.dot_general(domb, v_m, _NT, preferred_element_type=F32) - delta)).astype(BF16)
        dq_ref[0:N_META, :] = _rows_to_heads(jnp.dot(ds, k_m, preferred_element_type=F32) * scale).astype(BF16)
        acc[0:N_META, :] += jnp.concatenate([lax.dot_general(ds, qm, _TN, preferred_element_type=F32),
                                             lax.dot_general(p.astype(BF16), domb, _TN, preferred_element_type=F32)], axis=1)
        sink_tot = head_totals(-p_s * delta, N_META)
        tri = _causal(BLOCK)
        acc_sink[...] = jnp.zeros_like(acc_sink)

        def block(cur, first, keys):
            k2, v2 = _kv(kvs[keys, :])
            rows = pl.ds(cur, BLOCK)
            qb = _heads_to_rows(q_ref[rows, :])
            dob = _heads_to_rows(do_ref[rows, :])
            delta = jnp.sum(dob * _heads_to_rows(o_ref[rows, :]), axis=-1, keepdims=True)
            dobb = dob.astype(BF16)
            s2 = lax.dot_general(qb, k2, _NT, preferred_element_type=F32)
            sm = lax.dot_general(qb, k_m, _NT, preferred_element_type=F32)
            dp2 = lax.dot_general(dobb, v2, _NT, preferred_element_type=F32)
            dpm = lax.dot_general(dobb, v_m, _NT, preferred_element_type=F32)
            ds2s, p2s, dsms, pms = [], [], [], []
            for g in range(Q_PER_KV):
                sl = slice(g * BLOCK, (g + 1) * BLOCK)
                s_band = jnp.where(tri, s2[sl], NEG_INF) if first else _fold_band(tri, s2[sl])
                p_b, p_m, p_s = _band_probs(s_band, sm[sl], sk_ref[0, sl, :])
                ds_b = p_b * ((dp2[sl] if first else _fold_band(tri, dp2[sl])) - delta[sl])
                ds2s.append((ds_b if first else _unfold_band(tri, ds_b)).astype(BF16))
                p2s.append((p_b if first else _unfold_band(tri, p_b)).astype(BF16))
                dsms.append((p_m * (dpm[sl] - delta[sl])).astype(BF16))
                pms.append(p_m.astype(BF16))
                acc_sink[sl, :] += -p_s * delta[sl]
            ds2, p2 = jnp.concatenate(ds2s, axis=0), jnp.concatenate(p2s, axis=0)
            dsm, pm = jnp.concatenate(dsms, axis=0), jnp.concatenate(pms, axis=0)
            dq = jnp.dot(ds2, k2, preferred_element_type=F32) + jnp.dot(dsm, k_m, preferred_element_type=F32)
            dq_ref[rows, :] = _rows_to_heads(dq * scale).astype(BF16)
            acc[keys, :] += jnp.concatenate([lax.dot_general(ds2, qb, _TN, preferred_element_type=F32),
                                             lax.dot_general(p2, dobb, _TN, preferred_element_type=F32)], axis=1)
            acc[0:N_META, :] += jnp.concatenate([lax.dot_general(dsm, qb, _TN, preferred_element_type=F32),
                                                 lax.dot_general(pm, dobb, _TN, preferred_element_type=F32)], axis=1)

        block(N_META, True, pl.ds(N_META, BLOCK))

        def step(n, carry):
            block(_block_start(n), False, pl.ds(_block_start(n - 1), 2 * BLOCK))
            return carry

        lax.fori_loop(1, nb, step, 0)
        dkv_ref[...] = acc[...].astype(BF16)
        tot = sink_tot + head_totals(acc_sink[...], BLOCK)

        @pl.when(b == 0)
        def _():
            dsk_ref[0] = tot

        @pl.when(b > 0)
        def _():
            dsk_ref[0] += tot

    q_spec = pl.BlockSpec((L, QW), lambda h, b: (b, h))
    kv_spec = pl.BlockSpec((L, 2 * HEAD_DIM), lambda h, b: (b, h))
    return pl.pallas_call(
        body, name=name, grid=(KV, B),
        in_specs=[q_spec, kv_spec, q_spec, q_spec,
                  pl.BlockSpec((1, R, 1), lambda h, b: (h, 0, 0)),
                  pl.BlockSpec((1, Q_PER_KV * N_META, 1), lambda h, b: (h, 0, 0))],
        out_specs=[q_spec, kv_spec, pl.BlockSpec((1, 8, 128), lambda h, b: (h, 0, 0))],
        out_shape=[jax.ShapeDtypeStruct((T, D), BF16), jax.ShapeDtypeStruct((T, KV * 2 * HEAD_DIM), BF16),
                   jax.ShapeDtypeStruct((KV, 8, 128), F32)],
        scratch_shapes=[pltpu.VMEM((L, 2 * HEAD_DIM), BF16), pltpu.VMEM((L, 2 * HEAD_DIM), F32),
                        pltpu.VMEM((R, 1), F32)],
        compiler_params=_params(("parallel", "arbitrary")),
    )(q, kv, o, do, sink_col, sink_meta)


def _cmul_add(acc_r, acc_i, lr, li, xr, xi):
    return acc_r + (lr * xr - li * xi), acc_i + (lr * xi + li * xr)


def _cols_per_step(ncol):
    return 2 if ncol % 2 == 0 else 1


def _ssm_fwd(u, bmat, cmat, dskip, tables, nbatch, rc, name):
    T, W = u.shape
    ncol = W // SSM_LANES
    nch = T // rc
    S = STATE_LANES
    cps = _cols_per_step(ncol)
    assert nbatch == 4

    def body(u_ref, b_ref, c_ref, d_ref, tab_ref, y_ref, xs_ref, xp_ref, st_ref, carry_ref):
        ch = pl.program_id(1)

        @pl.when(ch == 0)
        def _():
            carry_ref[...] = jnp.zeros_like(carry_ref)

        uv = u_ref[...]
        for k in range(cps):
            st_ref[:, 2 * S * k:2 * S * (k + 1)] = jnp.dot(uv[:, SSM_LANES * k:SSM_LANES * (k + 1)].astype(BF16), b_ref[k],
                                                           preferred_element_type=F32)
        low = lax.broadcasted_iota(jnp.int32, (8, S), 0) < nbatch

        def tile(k, r0, c_r, c_i):
            re, im = slice(2 * S * k, 2 * S * k + S), slice(2 * S * k + S, 2 * S * (k + 1))
            la_r, la_i = tab_ref[k, :, 0:S], tab_ref[k, :, S:2 * S]
            lb_r, lb_i = tab_ref[k, :, 2 * S:3 * S], tab_ref[k, :, 3 * S:4 * S]
            v_r = st_ref[pl.ds(r0, 8), re]
            v_i = st_ref[pl.ds(r0, 8), im]
            v_r, v_i = _cmul_add(v_r, v_i, la_r, la_i, pltpu.roll(v_r, nbatch, 0), pltpu.roll(v_i, nbatch, 0))
            rc_r, rc_i = pltpu.roll(c_r, nbatch, 0), pltpu.roll(c_i, nbatch, 0)
            cb_r, cb_i = jnp.where(low, rc_r, c_r), jnp.where(low, rc_i, c_i)
            v_r, v_i = _cmul_add(v_r, v_i, lb_r, lb_i, cb_r, cb_i)
            st_ref[pl.ds(r0, 8), re] = v_r
            st_ref[pl.ds(r0, 8), im] = v_i
            p_r = jnp.where(low, rc_r, pltpu.roll(v_r, nbatch, 0))
            p_i = jnp.where(low, rc_i, pltpu.roll(v_i, nbatch, 0))
            return v_r, v_i, p_r, p_i

        def step(i, carry):
            r0 = pl.multiple_of(i * 16, 16)
            out = []
            for k in range(cps):
                re, im = slice(2 * S * k, 2 * S * k + S), slice(2 * S * k + S, 2 * S * (k + 1))
                a_r, a_i, pa_r, pa_i = tile(k, r0, carry[2 * k], carry[2 * k + 1])
                b_r, b_i, pb_r, pb_i = tile(k, r0 + 8, a_r, a_i)
                xp_ref[pl.ds(r0, 16), re] = jnp.concatenate([pa_r, pb_r], axis=0).astype(BF16)
                xp_ref[pl.ds(r0, 16), im] = jnp.concatenate([pa_i, pb_i], axis=0).astype(BF16)
                out += [b_r, b_i]
            return tuple(out)

        halves = tuple(carry_ref[:, S * j:S * (j + 1)] for j in range(2 * cps))
        halves = lax.fori_loop(0, rc // 16, step, halves)
        for j in range(2 * cps):
            carry_ref[:, S * j:S * (j + 1)] = halves[j]
        xb = st_ref[...].astype(BF16)
        xs_ref[...] = xb
        for k in range(cps):
            cols = slice(SSM_LANES * k, SSM_LANES * (k + 1))
            y_ref[:, cols] = (jnp.dot(xb[:, 2 * S * k:2 * S * (k + 1)], c_ref[k], preferred_element_type=F32)
                              + d_ref[:, cols] * uv[:, cols])

    return pl.pallas_call(
        body, name=name, grid=(ncol // cps, nch),
        in_specs=[pl.BlockSpec((rc, cps * SSM_LANES), lambda g, c: (c, g)),
                  pl.BlockSpec((cps, SSM_LANES, 2 * S), lambda g, c: (g, 0, 0)),
                  pl.BlockSpec((cps, 2 * S, SSM_LANES), lambda g, c: (g, 0, 0)),
                  pl.BlockSpec((1, cps * SSM_LANES), lambda g, c: (0, g)),
                  pl.BlockSpec((cps, 8, 4 * S), lambda g, c: (g, 0, 0))],
        out_specs=[pl.BlockSpec((rc, cps * SSM_LANES), lambda g, c: (c, g)),
                   pl.BlockSpec((rc, cps * 2 * S), lambda g, c: (c, g)),
                   pl.BlockSpec((rc, cps * 2 * S), lambda g, c: (c, g))],
        out_shape=[jax.ShapeDtypeStruct((T, W), F32), jax.ShapeDtypeStruct((T, ncol * 2 * S), BF16),
                   jax.ShapeDtypeStruct((T, ncol * 2 * S), BF16)],
        scratch_shapes=[pltpu.VMEM((rc, cps * 2 * S), F32), pltpu.VMEM((8, cps * 2 * S), F32)],
        compiler_params=_params(("parallel", "arbitrary")),
    )(u, bmat, cmat, dskip, tables)


def _ssm_bwd(dy, u, xs, xp, bmat, cmat, dskip, tables, nbatch, rc, name):
    T, W = u.shape
    ncol = W // SSM_LANES
    nch = T // rc
    S = STATE_LANES
    ntile = rc // 16
    cps = _cols_per_step(ncol)

    def body(dy_ref, u_ref, xs_ref, xp_ref, b_ref, c_ref, d_ref, tab_ref,
             du_ref, db_ref, dc_ref, dl_ref, dd_ref, st_ref, carry_ref, accl_ref, accd_ref):
        ch = pl.program_id(1)

        @pl.when(ch == 0)
        def _():
            carry_ref[...] = jnp.zeros_like(carry_ref)
            accl_ref[...] = jnp.zeros_like(accl_ref)
            accd_ref[...] = jnp.zeros_like(accd_ref)
            db_ref[...] = jnp.zeros_like(db_ref)
            dc_ref[...] = jnp.zeros_like(dc_ref)

        dyv = dy_ref[...]
        uv = u_ref[...]
        dyb = dyv.astype(BF16)
        for k in range(cps):
            st_ref[:, 2 * S * k:2 * S * (k + 1)] = lax.dot_general(dyb[:, SSM_LANES * k:SSM_LANES * (k + 1)], c_ref[k], _NT,
                                                                   preferred_element_type=F32)
        low = lax.broadcasted_iota(jnp.int32, (8, S), 0) < nbatch

        def tile(k, r0, p_r, p_i, c_r, c_i, al_r, al_i):
            re, im = slice(2 * S * k, 2 * S * k + S), slice(2 * S * k + S, 2 * S * (k + 1))
            la_r, la_i = tab_ref[k, :, 0:S], tab_ref[k, :, S:2 * S]
            lb_r, lb_i = tab_ref[k, :, 2 * S:3 * S], tab_ref[k, :, 3 * S:4 * S]
            v_r = st_ref[pl.ds(r0, 8), re]
            v_i = st_ref[pl.ds(r0, 8), im]
            v_r, v_i = _cmul_add(v_r, v_i, la_r, la_i, pltpu.roll(v_r, nbatch, 0), pltpu.roll(v_i, nbatch, 0))
            cb_r = jnp.where(low, c_r, pltpu.roll(c_r, nbatch, 0))
            cb_i = jnp.where(low, c_i, pltpu.roll(c_i, nbatch, 0))
            v_r, v_i = _cmul_add(v_r, v_i, lb_r, lb_i, cb_r, cb_i)
            st_ref[pl.ds(r0, 8), re] = v_r
            st_ref[pl.ds(r0, 8), im] = v_i
            al_r = al_r + (v_r * p_r + v_i * p_i)
            al_i = al_i + (v_i * p_r - v_r * p_i)
            return v_r, v_i, al_r, al_i

        def step(j, carry):
            r0 = pl.multiple_of((ntile - 1 - j) * 16, 16)
            out = []
            for k in range(cps):
                re, im = slice(2 * S * k, 2 * S * k + S), slice(2 * S * k + S, 2 * S * (k + 1))
                p_r = xp_ref[pl.ds(r0, 16), re].astype(F32)
                p_i = xp_ref[pl.ds(r0, 16), im].astype(F32)
                mid = tile(k, r0 + 8, p_r[8:16], p_i[8:16], *carry[4 * k:4 * k + 4])
                out += list(tile(k, r0, p_r[0:8], p_i[0:8], *mid))
            return tuple(out)

        init = []
        for k in range(cps):
            init += [carry_ref[:, 2 * S * k:2 * S * k + S], carry_ref[:, 2 * S * k + S:2 * S * (k + 1)],
                     accl_ref[:, 2 * S * k:2 * S * k + S], accl_ref[:, 2 * S * k + S:2 * S * (k + 1)]]
        fin = lax.fori_loop(0, ntile, step, tuple(init))
        for k in range(cps):
            carry_ref[:, 2 * S * k:2 * S * k + S] = fin[4 * k]
            carry_ref[:, 2 * S * k + S:2 * S * (k + 1)] = fin[4 * k + 1]
            accl_ref[:, 2 * S * k:2 * S * k + S] = fin[4 * k + 2]
            accl_ref[:, 2 * S * k + S:2 * S * (k + 1)] = fin[4 * k + 3]
        dsb = st_ref[...].astype(BF16)
        ub = uv.astype(BF16)
        for k in range(cps):
            cols, lanes = slice(SSM_LANES * k, SSM_LANES * (k + 1)), slice(2 * S * k, 2 * S * (k + 1))
            du_ref[:, cols] = (lax.dot_general(dsb[:, lanes], b_ref[k], _NT, preferred_element_type=F32)
                               + d_ref[:, cols] * dyv[:, cols])
            db_ref[k] += lax.dot_general(ub[:, cols], dsb[:, lanes], _TN, preferred_element_type=F32)
            dc_ref[k] += lax.dot_general(xs_ref[:, lanes], dyb[:, cols], _TN, preferred_element_type=F32)
        accd_ref[...] += _fold8(dyv * uv)

        @pl.when(ch == nch - 1)
        def _():
            for k in range(cps):
                dl_ref[k] = jnp.sum(accl_ref[:, 2 * S * k:2 * S * (k + 1)], axis=0, keepdims=True)
            dd_ref[...] = jnp.sum(accd_ref[...], axis=0, keepdims=True)

    rev = lambda g, c: (nch - 1 - c, g)
    return pl.pallas_call(
        body, name=name, grid=(ncol // cps, nch),
        in_specs=[pl.BlockSpec((rc, cps * SSM_LANES), rev), pl.BlockSpec((rc, cps * SSM_LANES), rev),
                  pl.BlockSpec((rc, cps * 2 * S), rev), pl.BlockSpec((rc, cps * 2 * S), rev),
                  pl.BlockSpec((cps, SSM_LANES, 2 * S), lambda g, c: (g, 0, 0)),
                  pl.BlockSpec((cps, 2 * S, SSM_LANES), lambda g, c: (g, 0, 0)),
                  pl.BlockSpec((1, cps * SSM_LANES), lambda g, c: (0, g)),
                  pl.BlockSpec((cps, 8, 4 * S), lambda g, c: (g, 0, 0))],
        out_specs=[pl.BlockSpec((rc, cps * SSM_LANES), rev),
                   pl.BlockSpec((cps, SSM_LANES, 2 * S), lambda g, c: (g, 0, 0)),
                   pl.BlockSpec((cps, 2 * S, SSM_LANES), lambda g, c: (g, 0, 0)),
                   pl.BlockSpec((cps, 1, 2 * S), lambda g, c: (g, 0, 0)),
                   pl.BlockSpec((1, cps * SSM_LANES), lambda g, c: (0, g))],
        out_shape=[jax.ShapeDtypeStruct((T, W), F32),
                   jax.ShapeDtypeStruct((ncol, SSM_LANES, 2 * S), F32),
                   jax.ShapeDtypeStruct((ncol, 2 * S, SSM_LANES), F32),
                   jax.ShapeDtypeStruct((ncol, 1, 2 * S), F32),
                   jax.ShapeDtypeStruct((1, W), F32)],
        scratch_shapes=[pltpu.VMEM((rc, cps * 2 * S), F32), pltpu.VMEM((8, cps * 2 * S), F32),
                        pltpu.VMEM((8, cps * 2 * S), F32), pltpu.VMEM((8, cps * SSM_LANES), F32)],
        compiler_params=_params(("parallel", "arbitrary")),
    )(dy, u, xs, xp, bmat, cmat, dskip, tables)


def _ssm_matrices(a_re, a_im, log_step, b_re, b_im, c_re, c_im):
    G, N = a_re.shape
    ncol = G // GROUPS_PER_COL
    step = jnp.exp(log_step)[:, None]
    mag = jnp.exp(a_re * step)
    ang = a_im * step
    lam_re, lam_im = mag * jnp.cos(ang), mag * jnp.sin(ang)
    den = a_re * a_re + a_im * a_im
    nr, ni = lam_re - 1.0, lam_im
    coef_re = (nr * a_re + ni * a_im) / den
    coef_im = (ni * a_re - nr * a_im) / den
    bb_re = coef_re[..., None] * b_re - coef_im[..., None] * b_im
    bb_im = coef_re[..., None] * b_im + coef_im[..., None] * b_re
    eye = jnp.eye(GROUPS_PER_COL, dtype=F32)
    bb = jnp.stack([bb_re, bb_im]).reshape(2, ncol, GROUPS_PER_COL, N, SSM_GROUP)
    bmat = jnp.einsum("pbgnc,gh->bgcphn", bb, eye).reshape(ncol, SSM_LANES, 2 * STATE_LANES)
    cc = jnp.stack([c_re, -c_im]).reshape(2, ncol, GROUPS_PER_COL, SSM_GROUP, N)
    cmat = jnp.einsum("pbgcn,gh->bpgnhc", cc, eye).reshape(ncol, 2 * STATE_LANES, SSM_LANES)
    lam = jnp.concatenate([lam_re.reshape(ncol, STATE_LANES), lam_im.reshape(ncol, STATE_LANES)], axis=-1)
    return lam, bmat, cmat


def _scan_tables(lam, nbatch, conj):
    S = STATE_LANES
    lr, li = lam[:, None, 0:S], lam[:, None, S:2 * S]
    if conj:
        li = -li
    l2r, l2i = lr * lr - li * li, 2.0 * lr * li
    first = (jnp.arange(8) < nbatch)[None, :, None]
    zero = jnp.zeros_like(lr)
    if conj:
        parts = [jnp.where(first, lr, zero), jnp.where(first, li, zero), jnp.where(first, l2r, lr), jnp.where(first, l2i, li)]
    else:
        parts = [jnp.where(first, zero, lr), jnp.where(first, zero, li), jnp.where(first, lr, l2r), jnp.where(first, li, l2i)]
    return jnp.concatenate([jnp.broadcast_to(p, (lam.shape[0], 8, S)) for p in parts], axis=-1)


def _adamw(w, g, m, v, name):
    R, C = w.shape
    tr = R if R <= 512 else _pick_tile(R, 512, 8)

    def body(w_ref, g_ref, m_ref, v_ref, d_ref, nm_ref, nv_ref):
        gv = g_ref[...]
        mn = ADAM_B1 * m_ref[...] + (1.0 - ADAM_B1) * gv
        vn = ADAM_B2 * v_ref[...] + (1.0 - ADAM_B2) * (gv * gv)
        m_hat = mn / (1.0 - ADAM_B1 ** ADAM_STEP)
        v_hat = vn / (1.0 - ADAM_B2 ** ADAM_STEP)
        d_ref[...] = -ADAM_LR * (m_hat / (jnp.sqrt(v_hat) + ADAM_EPS) + ADAM_WD * w_ref[...])
        nm_ref[...] = mn
        nv_ref[...] = vn

    spec = pl.BlockSpec((tr, C), lambda i: (i, 0))
    shp = jax.ShapeDtypeStruct((R, C), F32)
    return pl.pallas_call(body, name=name, grid=(R // tr,), in_specs=[spec] * 4, out_specs=[spec] * 3,
                          out_shape=[shp, shp, shp], compiler_params=_params(("parallel",)))(w, g, m, v)


_ANY = pl.BlockSpec(memory_space=pl.ANY)


def _place():
    x, y, c = lax.axis_index("x"), lax.axis_index("y"), lax.axis_index("c")
    chips = [(1 - x, y), (x, 1 - y), (1 - x, 1 - y)]
    return x, y, c, chips


def _remote(src, dst, send_sems, recv_sems, k, to):
    return pltpu.make_async_remote_copy(src_ref=src, dst_ref=dst, send_sem=send_sems.at[k], recv_sem=recv_sems.at[k],
                                        device_id=to, device_id_type=MESH_IDS)


def _gather_weights(shards):
    n = len(shards)

    def body(*refs):
        srcs, outs = refs[:n], refs[n:2 * n]
        send_sems, recv_sems = refs[2 * n:]
        x, y, c, chips = _place()
        sibling = (x, y, 1 - c)

        def piece(i, px, py, pc):
            half = shards[i].shape[0] // 2
            return outs[i].at[2 * px + py, pl.ds(pc * half, half), :]

        first = []
        for i in range(n):
            half = shards[i].shape[0] // 2
            for j, chip in enumerate(chips):
                first.append(_remote(srcs[i].at[pl.ds(c * half, half), :], piece(i, x, y, c), send_sems, recv_sems,
                                     6 * i + j, (*chip, c)))
        for cp in first:
            cp.start()
        passed = []
        for i in range(n):
            for j, chip in enumerate(chips):
                _remote(piece(i, *chip, c), piece(i, *chip, c), send_sems, recv_sems, 6 * i + j, (*chip, c)).wait_recv()
                cp = _remote(piece(i, *chip, c), piece(i, *chip, c), send_sems, recv_sems, 6 * i + 3 + j, sibling)
                cp.start()
                passed.append(cp)
        for i in range(n):
            for j, chip in enumerate(chips):
                _remote(piece(i, *chip, 1 - c), piece(i, *chip, 1 - c), send_sems, recv_sems, 6 * i + 3 + j,
                        sibling).wait_recv()
        for cp in first + passed:
            cp.wait_send()

    outs = pl.pallas_call(
        body, name="gather_weights", in_specs=[_ANY] * n, out_specs=[_ANY] * n,
        out_shape=[jax.ShapeDtypeStruct((N_CHIPS,) + s.shape, s.dtype) for s in shards],
        scratch_shapes=[pltpu.SemaphoreType.DMA((6 * n,)), pltpu.SemaphoreType.DMA((6 * n,))],
        compiler_params=pltpu.CompilerParams(has_side_effects=True),
    )(*shards)
    slot = 2 * lax.axis_index("x") + lax.axis_index("y")
    return [lax.dynamic_update_slice(o, s[None], (slot, 0, 0)) for o, s in zip(outs, shards)]


def _swap_halves(grads):
    n = len(grads)

    def body(*refs):
        srcs, outs = refs[:n], refs[n:2 * n]
        send_sems, recv_sems = refs[2 * n:]
        x, y, c, _ = _place()
        cps = []
        for i in range(n):
            half = grads[i].shape[1] // 2
            cps.append(_remote(srcs[i].at[:, pl.ds((1 - c) * half, half), :], outs[i], send_sems, recv_sems, i, (x, y, 1 - c)))
        for cp in cps:
            cp.start()
        for cp in cps:
            cp.wait()

    return pl.pallas_call(
        body, name="grad_swap_halves", in_specs=[_ANY] * n, out_specs=[_ANY] * n,
        out_shape=[jax.ShapeDtypeStruct((N_CHIPS, g.shape[1] // 2, g.shape[2]), g.dtype) for g in grads],
        scratch_shapes=[pltpu.SemaphoreType.DMA((n,)), pltpu.SemaphoreType.DMA((n,))],
        compiler_params=pltpu.CompilerParams(has_side_effects=True),
    )(*grads)


def _exchange_chips(parts):
    n = len(parts)

    def body(*refs):
        srcs, outs = refs[:n], refs[n:2 * n]
        send_sems, recv_sems = refs[2 * n:]
        x, y, c, chips = _place()
        cps = [_remote(srcs[i].at[2 * chip[0] + chip[1]], outs[i].at[j], send_sems, recv_sems, 3 * i + j, (*chip, c))
               for i in range(n) for j, chip in enumerate(chips)]
        for cp in cps:
            cp.start()
        for cp in cps:
            cp.wait()

    return pl.pallas_call(
        body, name="grad_exchange_chips", in_specs=[_ANY] * n, out_specs=[_ANY] * n,
        out_shape=[jax.ShapeDtypeStruct((3,) + p.shape[1:], p.dtype) for p in parts],
        scratch_shapes=[pltpu.SemaphoreType.DMA((3 * n,)), pltpu.SemaphoreType.DMA((3 * n,))],
        compiler_params=pltpu.CompilerParams(has_side_effects=True),
    )(*parts)


def _join_halves(fulls):
    n = len(fulls)

    def body(*refs):
        srcs, outs = refs[:n], refs[n:2 * n]
        send_sems, recv_sems = refs[2 * n:]
        x, y, c, _ = _place()
        sibling = (x, y, 1 - c)
        cps = []
        for i in range(n):
            h = fulls[i].shape[0] // 2
            cps.append(_remote(srcs[i].at[pl.ds(c * h, h), :], outs[i].at[pl.ds(c * h, h), :], send_sems, recv_sems, i,
                               sibling))
        for cp in cps:
            cp.start()
        for i in range(n):
            h = fulls[i].shape[0] // 2
            theirs = outs[i].at[pl.ds((1 - c) * h, h), :]
            _remote(theirs, theirs, send_sems, recv_sems, i, sibling).wait_recv()
        for cp in cps:
            cp.wait_send()

    return pl.pallas_call(
        body, name="grad_join_halves", in_specs=[_ANY] * n, out_specs=[_ANY] * n,
        out_shape=[jax.ShapeDtypeStruct(f.shape, f.dtype) for f in fulls],
        input_output_aliases={i: i for i in range(n)},
        scratch_shapes=[pltpu.SemaphoreType.DMA((n,)), pltpu.SemaphoreType.DMA((n,))],
        compiler_params=pltpu.CompilerParams(has_side_effects=True),
    )(*fulls)


def _half_tile(h):
    return h if h <= 512 else _pick_tile(h, 512, ROW_ALIGN)


def _sum_halves(g, r1, c_idx, name):
    _, R, C = g.shape
    H = R // 2
    tr = _half_tile(H)
    nblk = H // tr

    def body(c_ref, g_ref, r_ref, p_ref):
        p_ref[...] = (g_ref[...] + r_ref[...]).astype(BF16)

    half = pl.BlockSpec((None, tr, C), lambda s, i, c_ref: (s, c_ref[0] * nblk + i, 0))
    plain = pl.BlockSpec((None, tr, C), lambda s, i, c_ref: (s, i, 0))
    return pl.pallas_call(
        body, name=name,
        grid_spec=pltpu.PrefetchScalarGridSpec(num_scalar_prefetch=1, grid=(N_CHIPS, nblk), in_specs=[half, plain],
                                               out_specs=plain),
        out_shape=jax.ShapeDtypeStruct((N_CHIPS, H, C), BF16),
        compiler_params=_params(("parallel", "parallel")),
    )(c_idx, g, r1)


def _sum_chips(g, r1, r2, idx, name):
    _, R, C = g.shape
    H = R // 2
    tr = _half_tile(H)
    nblk = H // tr

    def body(idx_ref, g_ref, r1_ref, r2_ref, o_ref):
        o_ref[...] = (((g_ref[...] + r1_ref[...]) + r2_ref[0].astype(F32)) + r2_ref[1].astype(F32)) + r2_ref[2].astype(F32)

    return pl.pallas_call(
        body, name=name,
        grid_spec=pltpu.PrefetchScalarGridSpec(
            num_scalar_prefetch=1, grid=(nblk,),
            in_specs=[pl.BlockSpec((None, tr, C), lambda i, idx_ref: (idx_ref[0], idx_ref[1] * nblk + i, 0)),
                      pl.BlockSpec((None, tr, C), lambda i, idx_ref: (idx_ref[0], i, 0)),
                      pl.BlockSpec((3, tr, C), lambda i, idx_ref: (0, i, 0))],
            out_specs=pl.BlockSpec((tr, C), lambda i, idx_ref: (idx_ref[1] * nblk + i, 0))),
        out_shape=jax.ShapeDtypeStruct((R, C), F32),
        compiler_params=_params(("parallel",)),
    )(idx, g, r1, r2)


def _all_reduce_small(v, n_fold, fold_rows, fold_at):
    M, N = v.shape

    def body(x_ref, tot_ref, fold_ref, all_ref, send_sems, recv_sems, local_sem):
        x, y, c, chips = _place()
        me, sibling = (x, y, c), (x, y, 1 - c)

        def rows(px, py, pc):
            return all_ref.at[pl.ds((4 * px + 2 * py + pc) * M, M), :]

        def copy(k, block, to, src=None):
            return _remote(rows(*block) if src is None else src, rows(*block), send_sems, recv_sems, k, to)

        mine = pltpu.make_async_copy(x_ref, rows(*me), local_sem)
        mine.start()
        first = [copy(0, me, sibling, src=x_ref)]
        first += [copy(1 + j, me, (*chip, c), src=x_ref) for j, chip in enumerate(chips)]
        for cp in first:
            cp.start()
        passed = [copy(4 + j, (*chip, c), sibling) for j, chip in enumerate(chips)]
        for j, chip in enumerate(chips):
            copy(1 + j, (*chip, c), me).wait_recv()
            passed[j].start()
        copy(0, sibling, me).wait_recv()
        for j, chip in enumerate(chips):
            copy(4 + j, (*chip, 1 - c), me).wait_recv()
        for cp in first + passed:
            cp.wait_send()
        mine.wait()
        tot = all_ref[0:M, :]
        for d in range(1, 8):
            tot = tot + all_ref[d * M:(d + 1) * M, :]
        tot_ref[...] = tot
        f = tot[fold_at:fold_at + fold_rows, :]
        for e in range(1, n_fold):
            f = f + tot[fold_at + e * fold_rows:fold_at + (e + 1) * fold_rows, :]
        fold_ref[...] = f

    vm = pl.BlockSpec(memory_space=pltpu.VMEM)
    return pl.pallas_call(
        body, name="all_reduce_small", in_specs=[vm], out_specs=[vm, vm],
        out_shape=[jax.ShapeDtypeStruct((M, N), F32), jax.ShapeDtypeStruct((fold_rows, N), F32)],
        scratch_shapes=[pltpu.VMEM((8 * M, N), F32), pltpu.SemaphoreType.DMA((7,)), pltpu.SemaphoreType.DMA((7,)),
                        pltpu.SemaphoreType.DMA],
        compiler_params=pltpu.CompilerParams(has_side_effects=True, vmem_limit_bytes=VMEM_LIMIT),
    )(v)


def _as_rows(a, width):
    flat = a.reshape(-1)
    pad = (-flat.shape[0]) % width
    if pad:
        flat = jnp.concatenate([flat, jnp.zeros((pad,), flat.dtype)])
    return flat.reshape(-1, width)


class _Layout:
    def __init__(self, width, total_mult):
        self.width, self.total_mult = width, total_mult
        self.offsets, self.shapes, self.rows = {}, {}, 0

    def add(self, name, shape):
        r = -(-math.prod(shape) // self.width)
        self.offsets[name], self.shapes[name] = (self.rows, r), tuple(shape)
        self.rows += r

    def align(self, mult):
        gap = (-self.rows) % mult
        if gap:
            self.offsets[f"_gap{self.rows}"], self.shapes[f"_gap{self.rows}"] = (self.rows, gap), (gap, self.width)
            self.rows += gap
        return self.rows

    def pack(self, pieces):
        self.align(self.total_mult)
        parts = [_as_rows(pieces[n].astype(F32), self.width) if n in pieces else jnp.zeros(self.shapes[n], F32)
                 for n in self.offsets]
        return jnp.concatenate(parts, axis=0)

    def unpack(self, buf, name):
        off, r = self.offsets[name]
        shape = self.shapes[name]
        return buf[off:off + r].reshape(-1)[:math.prod(shape)].reshape(shape)


_BIG = ["ffn1_w1", "ffn1_w3", "ffn1_w2", "w_in", "ssm_glu_a", "ssm_glu_b", "w_out", "ffn2_w1", "ffn2_w3", "ffn2_w2"]
_SMALL = ["ffn1_norm", "mix_norm", "ffn2_norm", "final_norm", "attn_sinks", "ssm_a_re", "ssm_a_im", "ssm_log_step",
          "ssm_b_re", "ssm_b_im", "ssm_c_re", "ssm_c_im", "ssm_d"]
_WEIGHTS = ["meta_tokens", "ffn1_norm", "ffn1_w1", "ffn1_w3", "ffn1_w2", "mix_norm", "w_in", "attn_sinks", "ssm_a_re",
            "ssm_a_im", "ssm_log_step", "ssm_b_re", "ssm_b_im", "ssm_c_re", "ssm_c_im", "ssm_d", "ssm_glu_a",
            "ssm_glu_b", "w_out", "ffn2_norm", "ffn2_w1", "ffn2_w3", "ffn2_w2", "final_norm"]


def _kv_interleave(w, kv_heads):
    kvw = kv_heads * HEAD_DIM
    lead = w.shape[:-1]
    k = w[..., 0:kvw].reshape(lead + (kv_heads, 1, HEAD_DIM))
    v = w[..., kvw:2 * kvw].reshape(lead + (kv_heads, 1, HEAD_DIM))
    return jnp.concatenate([jnp.concatenate([k, v], axis=-2).reshape(lead + (2 * kvw,)), w[..., 2 * kvw:]], axis=-1)


def _kv_deinterleave(w, kv_heads):
    kvw = kv_heads * HEAD_DIM
    lead = w.shape[:-1]
    kv = w[..., 0:2 * kvw].reshape(lead + (kv_heads, 2, HEAD_DIM))
    return jnp.concatenate([kv[..., 0, :].reshape(lead + (kvw,)), kv[..., 1, :].reshape(lead + (kvw,)), w[..., 2 * kvw:]],
                           axis=-1)


def _step(x, target, w, m, v):
    B, S, D = x.shape
    L = S + N_META
    T = B * L
    H = D // HEAD_DIM
    KV = H // Q_PER_KV
    SW = D // 2
    tm = _pick_tile(L, ROW_TILE_CAP, ROW_ALIGN)
    rc = _pick_tile(L, ROW_TILE_CAP // B, 4) * B
    tw = _pick_tile(T, 3 * ROW_TILE_CAP, ROW_ALIGN)
    my_c = lax.axis_index("c")
    my_slot = 2 * lax.axis_index("x") + lax.axis_index("y")

    shards = [w[n][0].astype(BF16) for n in _BIG] + [w["meta_tokens"]]
    gathered = _gather_weights(shards)
    ws = dict(zip(_BIG, gathered[:-1]))
    meta = jnp.transpose(gathered[-1], (1, 0, 2)).reshape(N_META, D)
    w_kvu = _kv_interleave(ws["w_in"][1], KV)

    g_ffn1, g_mix, g_ffn2 = w["ffn1_norm"], w["mix_norm"], w["ffn2_norm"]
    g_final = w["final_norm"].reshape(1, D)

    h0 = jnp.concatenate([jnp.broadcast_to(meta[None], (B, N_META, D)), x], axis=1).reshape(T, D)

    def ffn_fwd(h, g, tag):
        n = _rmsnorm_fwd(h, g, tm, f"{tag}_norm")
        a, c, s = _ffn_up(n, ws[f"{tag}_w1"], ws[f"{tag}_w3"], tm, f"{tag}_up")
        return _ffn_down(s, ws[f"{tag}_w2"], h, tm, f"{tag}_down"), (n, a, c, s)

    h1, saved1 = ffn_fwd(h0, g_ffn1, "ffn1")
    hn = _rmsnorm_fwd(h1, g_mix, tm, "mix_norm")
    q = _mm_colslots(hn, ws["w_in"], BF16, "w_in_q", tm, first=0, count=1, scale=HEAD_DIM ** -0.5)
    kvu = _mm_plain(hn, w_kvu, "nn", F32, "w_in_kvu", tm)
    gates = _mm_colslots(hn, ws["w_in"], F32, "w_in_gates", tm, first=2, count=2)

    sinks = w["attn_sinks"].reshape(KV, Q_PER_KV, 1, 1)
    sink_col = jnp.broadcast_to(sinks, (KV, Q_PER_KV, BLOCK, 1)).reshape(KV, Q_PER_KV * BLOCK, 1)
    sink_meta = jnp.broadcast_to(sinks, (KV, Q_PER_KV, N_META, 1)).reshape(KV, Q_PER_KV * N_META, 1)
    attn = _attn_fwd(q, kvu, sink_col, sink_meta, B, "attn_fwd")

    def to_time_major(a2d):
        return jnp.transpose(a2d.reshape(B, L, a2d.shape[-1]), (1, 0, 2)).reshape(T, a2d.shape[-1])

    def to_batch_major(a2d):
        return jnp.transpose(a2d.reshape(L, B, a2d.shape[-1]), (1, 0, 2)).reshape(T, a2d.shape[-1])

    ssm_args = (w["ssm_a_re"][0], w["ssm_a_im"][0], w["ssm_log_step"][0], w["ssm_b_re"][0], w["ssm_b_im"][0],
                w["ssm_c_re"][0], w["ssm_c_im"][0])
    (lam, bmat, cmat), ssm_vjp = jax.vjp(_ssm_matrices, *ssm_args)
    bmat16, cmat16 = bmat.astype(BF16), cmat.astype(BF16)
    u_t = to_time_major(kvu[:, SW:])
    y_t, xs, xp = _ssm_fwd(u_t, bmat16, cmat16, w["ssm_d"], _scan_tables(lam, B, False), B, rc, "ssm_fwd")
    y0 = to_batch_major(y_t)
    yg = _gelu_fwd(y0, tm, "gelu_fwd")
    ga = _mm_colslots(yg, ws["ssm_glu_a"], F32, "glu_a", tm)
    gb = _mm_colslots(yg, ws["ssm_glu_b"], F32, "glu_b", tm)
    merged = _merge_fwd(gates, attn, ga, gb, tm, "merge_fwd")
    h2 = _mm_rowslots(merged, ws["w_out"], h1, tm, "w_out")
    h3, saved2 = ffn_fwd(h2, g_ffn2, "ffn2")
    dh3, dh3b, dg_final, loss_row = _loss_head(h3, g_final, target, tm, "loss_head")

    grads = {}

    def ffn_bwd(h, g, saved, dh, dhb, tag):
        n, a, c, s = saved
        w1, w3, w2 = ws[f"{tag}_w1"], ws[f"{tag}_w3"], ws[f"{tag}_w2"]
        grads[f"{tag}_w2"] = _wgrad_hidden_rows(s, dhb, tw, f"{tag}_dw2", 0.5)
        da, dc = _ffn_dhidden(dhb, w2, a, c, tm, f"{tag}_dhidden")
        grads[f"{tag}_w1"] = _wgrad_hidden_cols(n, da, tw, f"{tag}_dw1")
        grads[f"{tag}_w3"] = _wgrad_hidden_cols(n, dc, tw, f"{tag}_dw3")
        dh_in, dhb_in, grads[f"{tag}_norm"] = _ffn_dn(da, w1, dc, w3, h, g, dh, tm, f"{tag}_dn")
        return dh_in, dhb_in

    dh2, dh2b = ffn_bwd(h2, g_ffn2, saved2, dh3, dh3b, "ffn2")

    grads["w_out"] = _wgrad_rowslots(merged, dh2b, tw, "dw_out")
    dmerged = _mm_rowslots_t(dh2b, ws["w_out"], tm, "dmerged")
    dattn, dgat, dgss, dga, dgb = _merge_bwd(dmerged, gates, attn, ga, gb, tm, "merge_bwd")
    grads["ssm_glu_a"] = _wgrad_colslots(yg, dga, tw, "dglu_a")
    grads["ssm_glu_b"] = _wgrad_colslots(yg, dgb, tw, "dglu_b")
    dyg = _mm_colslots_t([(dga, ws["ssm_glu_a"]), (dgb, ws["ssm_glu_b"])], tm, "dyg")
    dy0 = _gelu_bwd(dyg, y0, tm, "gelu_bwd")
    du_t, dbmat, dcmat, dlam, dd = _ssm_bwd(to_time_major(dy0), u_t, xs, xp, bmat16, cmat16, w["ssm_d"],
                                            _scan_tables(lam, B, True), B, rc, "ssm_bwd")
    d_ssm = ssm_vjp((dlam[:, 0, :], dbmat, dcmat))
    for n, gval in zip(["ssm_a_re", "ssm_a_im", "ssm_log_step", "ssm_b_re", "ssm_b_im", "ssm_c_re", "ssm_c_im"], d_ssm):
        grads[n] = gval[None]
    grads["ssm_d"] = dd

    dq, dkv, dsink = _attn_bwd(q, kvu, attn, dattn, sink_col, sink_meta, B, "attn_bwd")
    grads["attn_sinks"] = dsink[:, 0:Q_PER_KV, 0].reshape(1, H)
    dkvu = jnp.concatenate([dkv, to_batch_major(du_t).astype(BF16)], axis=1)
    pieces = [dq, dkvu, dgat, dgss]
    dw_in = [_wgrad_plain(hn, p, f"dw_in_{k}", tw) for k, p in enumerate(pieces)]
    dw_in[1] = _kv_deinterleave(dw_in[1], KV)
    grads["w_in"] = jnp.stack(dw_in)
    w_in_parts = [ws["w_in"][0], w_kvu, ws["w_in"][2], ws["w_in"][3]]
    whole = _once((D, D), lambda i: (0, 0))
    dh1, dh1b, grads["mix_norm"] = _mm_norm_bwd(
        "dhn", "nt", [(p, _spec((tm, D), lambda i: (i, 0)), wp, whole) for p, wp in zip(pieces, w_in_parts)],
        h1, g_mix, dh2, tm)
    dh0, _ = ffn_bwd(h0, g_ffn1, saved1, dh1, dh1b, "ffn1")
    dh0 = dh0.reshape(B, L, D)
    grad_x = dh0[:, N_META:, :]

    grads["final_norm"] = dg_final
    slay = _Layout(D, 8)
    for n in _SMALL:
        slay.add(n, w[n].shape)
    slay.add("loss", (1, D))
    meta_at = slay.align(8)
    slay.add("meta", (B * N_META, D))
    small = slay.pack({**{n: grads[n] for n in _SMALL}, "loss": loss_row, "meta": dh0[:, :N_META, :]})
    tot_small, dmeta = _all_reduce_small(small, B, N_META, meta_at)
    loss = slay.unpack(tot_small, "loss")[0, 0]
    for n in _SMALL:
        grads[n] = slay.unpack(tot_small, n)
    cw = D // N_CHIPS
    grads["meta_tokens"] = lax.dynamic_slice_in_dim(dmeta, my_slot * cw, cw, axis=1)

    glist = [grads[n] for n in _BIG]
    c_idx = my_c.reshape(1).astype(jnp.int32)
    idx = jnp.stack([my_slot, my_c]).astype(jnp.int32)
    r1 = _swap_halves(glist)
    parts = [_sum_halves(g, r, c_idx, f"grad_sum_halves_{n}") for n, g, r in zip(_BIG, glist, r1)]
    r2 = _exchange_chips(parts)
    halves = [_sum_chips(g, ra, rb, idx, f"grad_sum_chips_{n}") for n, g, ra, rb in zip(_BIG, glist, r1, r2)]
    for n, f in zip(_BIG, _join_halves(halves)):
        grads[n] = f[None]

    delta, new_m, new_v = {}, {}, {}
    for n in _BIG + ["meta_tokens"]:
        shp = w[n].shape
        two = (shp[-2], shp[-1])
        d_, m_, v_ = _adamw(w[n].reshape(two), grads[n].reshape(two), m[n].reshape(two), v[n].reshape(two), f"adamw_{n}")
        delta[n], new_m[n], new_v[n] = d_.reshape(shp), m_.reshape(shp), v_.reshape(shp)
        grads[n] = grads[n].reshape(shp)
    play = _Layout(D, 8)
    for n in _SMALL:
        play.add(n, w[n].shape)
    d_, m_, v_ = _adamw(play.pack({n: w[n] for n in _SMALL}), play.pack({n: grads[n] for n in _SMALL}),
                        play.pack({n: m[n] for n in _SMALL}), play.pack({n: v[n] for n in _SMALL}), "adamw_small")
    for n in _SMALL:
        delta[n], new_m[n], new_v[n] = play.unpack(d_, n), play.unpack(m_, n), play.unpack(v_, n)
        grads[n] = grads[n].reshape(w[n].shape)

    return (loss, grad_x, *[grads[n] for n in _WEIGHTS], *[delta[n] for n in _WEIGHTS],
            *[new_m[n] for n in _WEIGHTS], *[new_v[n] for n in _WEIGHTS])


def kernel(x, meta_tokens, ffn1_norm, ffn1_w1, ffn1_w3, ffn1_w2, mix_norm, w_in, attn_sinks, ssm_a_re, ssm_a_im, ssm_log_step, ssm_b_re, ssm_b_im, ssm_c_re, ssm_c_im, ssm_d, ssm_glu_a, ssm_glu_b, w_out, ffn2_norm, ffn2_w1, ffn2_w3, ffn2_w2, final_norm, loss_target, m_meta_tokens, m_ffn1_norm, m_ffn1_w1, m_ffn1_w3, m_ffn1_w2, m_mix_norm, m_w_in, m_attn_sinks, m_ssm_a_re, m_ssm_a_im, m_ssm_log_step, m_ssm_b_re, m_ssm_b_im, m_ssm_c_re, m_ssm_c_im, m_ssm_d, m_ssm_glu_a, m_ssm_glu_b, m_w_out, m_ffn2_norm, m_ffn2_w1, m_ffn2_w3, m_ffn2_w2, m_final_norm, v_meta_tokens, v_ffn1_norm, v_ffn1_w1, v_ffn1_w3, v_ffn1_w2, v_mix_norm, v_w_in, v_attn_sinks, v_ssm_a_re, v_ssm_a_im, v_ssm_log_step, v_ssm_b_re, v_ssm_b_im, v_ssm_c_re, v_ssm_c_im, v_ssm_d, v_ssm_glu_a, v_ssm_glu_b, v_w_out, v_ffn2_norm, v_ffn2_w1, v_ffn2_w3, v_ffn2_w2, v_final_norm):
    args = locals()
    w = {n: args[n] for n in _WEIGHTS}
    m = {n: args["m_" + n] for n in _WEIGHTS}
    v = {n: args["v_" + n] for n in _WEIGHTS}
    return _step(x, loss_target, w, m, v)
```

```python
import math

import jax
import jax.numpy as jnp
from jax import lax
from jax.experimental import pallas as pl
from jax.experimental.pallas import tpu as pltpu

F32 = jnp.float32
BF16 = jnp.bfloat16
MESH_IDS = pl.DeviceIdType.MESH

N_CHIPS = 4
N_META = 16
HEAD_DIM = 64
Q_PER_KV = 4
QW = Q_PER_KV * HEAD_DIM
BLOCK = 128
SSM_GROUP = 16
SSM_STATE = 64
SSM_LANES = 128
GROUPS_PER_COL = SSM_LANES // SSM_GROUP
STATE_LANES = GROUPS_PER_COL * SSM_STATE
NORM_EPS = 1e-6
NEG_INF = -1e30
ADAM_LR, ADAM_B1, ADAM_B2, ADAM_EPS, ADAM_WD, ADAM_STEP = 0.001, 0.9, 0.999, 1e-08, 0.01, 10
GELU_C = math.sqrt(2.0 / math.pi)
ROW_ALIGN = 16
VMEM_LIMIT = 56 * 1024 * 1024
ROW_TILE_CAP = 688

_NN = (((1,), (0,)), ((), ()))
_NT = (((1,), (1,)), ((), ()))
_TN = (((0,), (0,)), ((), ()))
_DIMS = {"nn": _NN, "nt": _NT, "tn": _TN}


def _params(sem, **kw):
    return pltpu.CompilerParams(dimension_semantics=sem, vmem_limit_bytes=VMEM_LIMIT, **kw)


def _pick_tile(n, cap, mult):
    best = None
    for t in range(mult, min(n, cap) + 1, mult):
        if n % t == 0:
            best = t
    if best is None:
        raise ValueError(f"no tile for {n} (cap {cap}, multiple of {mult})")
    return best


def _sigmoid(x):
    return 1.0 / (1.0 + jnp.exp(-x))


def _spec(block, index_map):
    return pl.BlockSpec(block, index_map)


def _sum_dots(ins, mode):
    tot = None
    for p in range(len(ins) // 2):
        a_ref, b_ref = ins[2 * p], ins[2 * p + 1]
        for sl in ([None] if len(a_ref.shape) == 2 else range(a_ref.shape[0])):
            a = (a_ref[...] if sl is None else a_ref[sl]).astype(BF16)
            b = (b_ref[...] if sl is None else b_ref[sl]).astype(BF16)
            d = lax.dot_general(a, b, _DIMS[mode], preferred_element_type=F32)
            tot = d if tot is None else tot + d
    return tot


def _mm(name, grid, kaxis, mode, pairs, out_shape, out_spec, scale=1.0, res=None):
    npairs = len(pairs)
    has_res = res is not None
    gk = 1 if kaxis is None else grid[kaxis]
    acc_shape = tuple(d for d in out_spec.block_shape if d is not None)

    def body(*refs):
        res_ref = refs[2 * npairs] if has_res else None
        o_ref = refs[2 * npairs + has_res]
        tot = _sum_dots(refs[:2 * npairs], mode)

        def finish(acc):
            r = acc * scale if scale != 1.0 else acc
            if has_res:
                r = res_ref[...] + r
            o_ref[...] = r.astype(o_ref.dtype)

        if gk == 1:
            finish(tot)
        else:
            acc_ref = refs[-1]
            k = pl.program_id(kaxis)

            @pl.when(k == 0)
            def _():
                acc_ref[...] = tot

            @pl.when(k > 0)
            def _():
                acc_ref[...] += tot

            @pl.when(k == gk - 1)
            def _():
                finish(acc_ref[...])

    in_specs, args = [], []
    for a, a_spec, b, b_spec in pairs:
        in_specs += [a_spec, b_spec]
        args += [a, b]
    if has_res:
        in_specs.append(res[1])
        args.append(res[0])
    sem = tuple("arbitrary" if ax == kaxis else "parallel" for ax in range(len(grid)))
    return pl.pallas_call(
        body, name=name, grid=grid, in_specs=in_specs, out_specs=out_spec, out_shape=out_shape,
        scratch_shapes=[pltpu.VMEM(acc_shape, F32)] if gk > 1 else [],
        compiler_params=_params(sem),
    )(*args)


def _rows(tm, width):
    return _spec((tm, width), lambda i, s: (i, 0))


def _mm_plain(a, b, mode, out_dtype, name, tm, scale=1.0):
    M, K = a.shape
    N = b.shape[1] if mode == "nn" else b.shape[0]
    return _mm(name, (M // tm,), None, mode,
               [(a, _spec((tm, K), lambda i: (i, 0)), b, _spec(b.shape, lambda i: (0, 0)))],
               jax.ShapeDtypeStruct((M, N), out_dtype), _spec((tm, N), lambda i: (i, 0)), scale=scale)


def _wgrad_plain(a, b, name, tr):
    R, M = a.shape
    N = b.shape[1]
    return _mm(name, (R // tr,), 0, "tn",
               [(a, _spec((tr, M), lambda r: (r, 0)), b, _spec((tr, N), lambda r: (r, 0)))],
               jax.ShapeDtypeStruct((M, N), F32), _spec((M, N), lambda r: (0, 0)))


def _rmsnorm_fwd(h, g, tm, name):
    T, D = h.shape

    def body(h_ref, g_ref, o_ref):
        x = h_ref[...]
        r = lax.rsqrt(jnp.mean(x * x, axis=-1, keepdims=True) + NORM_EPS)
        o_ref[...] = ((x * r) * g_ref[...]).astype(BF16)

    return pl.pallas_call(
        body, name=name, grid=(T // tm,),
        in_specs=[pl.BlockSpec((tm, D), lambda i: (i, 0)), pl.BlockSpec((1, D), lambda i: (0, 0))],
        out_specs=pl.BlockSpec((tm, D), lambda i: (i, 0)),
        out_shape=jax.ShapeDtypeStruct((T, D), BF16),
        compiler_params=_params(("parallel",)),
    )(h, g)


def _fold8(x):
    return jnp.sum(x.reshape(x.shape[0] // 8, 8, x.shape[1]), axis=0)


def _mm_norm_bwd(name, mode, pairs, h, g, dres, tm):
    T, D = h.shape
    nt = T // tm
    npairs = len(pairs)

    def body(*refs):
        h_ref, g_ref, dres_ref, dh_ref, dhb_ref, dg_ref, acc_ref = refs[2 * npairs:]
        i = pl.program_id(0)
        x = h_ref[...]
        r = lax.rsqrt(jnp.mean(x * x, axis=-1, keepdims=True) + NORM_EPS)
        xhat = x * r
        dy = _sum_dots(refs[:2 * npairs], mode)
        dxhat = dy * g_ref[...]
        dx = r * (dxhat - xhat * jnp.mean(dxhat * xhat, axis=-1, keepdims=True))
        dh = dres_ref[...] + dx
        dh_ref[...] = dh
        dhb_ref[...] = dh.astype(BF16)
        part = _fold8(dy * xhat)

        @pl.when(i == 0)
        def _():
            acc_ref[...] = part

        @pl.when(i > 0)
        def _():
            acc_ref[...] += part

        @pl.when(i == nt - 1)
        def _():
            dg_ref[...] = jnp.sum(acc_ref[...], axis=0, keepdims=True)

    row = pl.BlockSpec((tm, D), lambda i: (i, 0))
    vec = pl.BlockSpec((1, D), lambda i: (0, 0))
    in_specs, args = [], []
    for a, a_spec, b, b_spec in pairs:
        in_specs += [a_spec, b_spec]
        args += [a, b]
    return pl.pallas_call(
        body, name=name, grid=(nt,),
        in_specs=in_specs + [row, vec, row],
        out_specs=[row, row, vec],
        out_shape=[jax.ShapeDtypeStruct((T, D), F32), jax.ShapeDtypeStruct((T, D), BF16),
                   jax.ShapeDtypeStruct((1, D), F32)],
        scratch_shapes=[pltpu.VMEM((8, D), F32)],
        compiler_params=_params(("arbitrary",)),
    )(*args, h, g, dres)


def _ffn_up(n, w1, w3, tm, name, riders=None):
    T, D = n.shape
    Fs = w1.shape[2]

    def body(n_ref, w1_ref, w3_ref, a_ref, c_ref, s_ref):
        x = n_ref[...]
        a = jnp.dot(x, w1_ref[...], preferred_element_type=F32)
        c = jnp.dot(x, w3_ref[...], preferred_element_type=F32)
        a_ref[...] = a.astype(BF16)
        c_ref[...] = c.astype(BF16)
        s_ref[...] = (a * _sigmoid(a) * c).astype(BF16)

    w_spec = _spec((None, D, Fs), lambda s, i: (s, 0, 0))
    o_spec = _spec((None, tm, Fs), lambda s, i: (s, i, 0))
    o_shape = jax.ShapeDtypeStruct((N_CHIPS, T, Fs), BF16)
    return _call(body, name, (N_CHIPS, T // tm), [_spec((tm, D), lambda s, i: (i, 0)), w_spec, w_spec],
                 [o_spec, o_spec, o_spec], [o_shape, o_shape, o_shape], [], ("parallel", "parallel"), (n, w1, w3), riders)


def _ffn_down(s, w2, h, tm, name):
    _, T, Fs = s.shape
    D = w2.shape[2]
    row = _spec((tm, D), lambda i: (i, 0))
    return _mm(name, (T // tm,), None, "nn",
               [(s, _spec((N_CHIPS, tm, Fs), lambda i: (0, i, 0)), w2, _spec((N_CHIPS, Fs, D), lambda i: (0, 0, 0)))],
               jax.ShapeDtypeStruct((T, D), F32), row, scale=0.5, res=(h, row))


def _ffn_dhidden(dhb, w2, a, c, tm, name, riders=None):
    T, D = dhb.shape
    Fs = w2.shape[1]

    def body(dh_ref, w2_ref, a_ref, c_ref, da_ref, dc_ref):
        d = 0.5 * lax.dot_general(dh_ref[...], w2_ref[...], _NT, preferred_element_type=F32)
        av = a_ref[...].astype(F32)
        cv = c_ref[...].astype(F32)
        sg = _sigmoid(av)
        da_ref[...] = (d * cv * (sg * (1.0 + av * (1.0 - sg)))).astype(BF16)
        dc_ref[...] = (d * (av * sg)).astype(BF16)

    h_spec = _spec((None, tm, Fs), lambda s, i: (s, i, 0))
    o_shape = jax.ShapeDtypeStruct((N_CHIPS, T, Fs), BF16)
    return _call(body, name, (N_CHIPS, T // tm),
                 [_spec((tm, D), lambda s, i: (i, 0)), _spec((None, Fs, D), lambda s, i: (s, 0, 0)), h_spec, h_spec],
                 [h_spec, h_spec], [o_shape, o_shape], [], ("parallel", "parallel"), (dhb, w2, a, c), riders)


def _wgrad_hidden_rows(s, dhb, tr, name, scale):
    _, T, Fs = s.shape
    D = dhb.shape[1]
    return _mm(name, (N_CHIPS, T // tr), 1, "tn",
               [(s, _spec((None, tr, Fs), lambda k, r: (k, r, 0)), dhb, _spec((tr, D), lambda k, r: (r, 0)))],
               jax.ShapeDtypeStruct((N_CHIPS, Fs, D), F32), _spec((None, Fs, D), lambda k, r: (k, 0, 0)), scale=scale)


def _wgrad_hidden_cols(n, da, tr, name):
    T, D = n.shape
    Fs = da.shape[2]
    return _mm(name, (N_CHIPS, T // tr), 1, "tn",
               [(n, _spec((tr, D), lambda k, r: (r, 0)), da, _spec((None, tr, Fs), lambda k, r: (k, r, 0)))],
               jax.ShapeDtypeStruct((N_CHIPS, D, Fs), F32), _spec((None, D, Fs), lambda k, r: (k, 0, 0)))


def _once(block, index_map):
    return pl.BlockSpec(block, index_map, pipeline_mode=pl.Buffered(1))


def _ffn_dn(da, w1, dc, w3, h, g, dres, tm, name):
    _, T, Fs = da.shape
    D = w1.shape[1]
    h_spec = _spec((N_CHIPS, tm, Fs), lambda i: (0, i, 0))
    w_spec = _once((N_CHIPS, D, Fs), lambda i: (0, 0, 0))
    return _mm_norm_bwd(name, "nt", [(da, h_spec, w1, w_spec), (dc, h_spec, w3, w_spec)], h, g, dres, tm)


def _mm_colslots(a, w, out_dtype, name, tm, first=0, count=N_CHIPS, scale=1.0):
    T, K = a.shape
    Ns = w.shape[2]
    return _mm(name, (T // tm, count), None, "nn",
               [(a, _spec((tm, K), lambda i, j: (i, 0)), w, _spec((None, K, Ns), lambda i, j: (first + j, 0, 0)))],
               jax.ShapeDtypeStruct((T, count * Ns), out_dtype), _spec((tm, Ns), lambda i, j: (i, j)), scale=scale)


def _wgrad_colslots(a, d, tr, name):
    T, K = a.shape
    Ns = d.shape[1] // N_CHIPS
    return _mm(name, (N_CHIPS, T // tr), 1, "tn",
               [(a, _spec((tr, K), lambda k, r: (r, 0)), d, _spec((tr, Ns), lambda k, r: (r, k)))],
               jax.ShapeDtypeStruct((N_CHIPS, K, Ns), F32), _spec((None, K, Ns), lambda k, r: (k, 0, 0)))


def _mm_colslots_t(pairs, tm, name):
    d0, w0 = pairs[0]
    T = d0.shape[0]
    K, Ns = w0.shape[1], w0.shape[2]
    d_spec = _spec((tm, Ns), lambda i, k: (i, k))
    w_spec = _spec((None, K, Ns), lambda i, k: (k, 0, 0))
    return _mm(name, (T // tm, N_CHIPS), 1, "nt", [(d, d_spec, w, w_spec) for d, w in pairs],
               jax.ShapeDtypeStruct((T, K), F32), _rows(tm, K))


def _mm_rowslots(a, w, h, tm, name):
    T = a.shape[0]
    Ks, N = w.shape[1], w.shape[2]
    return _mm(name, (T // tm, N_CHIPS), 1, "nn",
               [(a, _spec((tm, Ks), lambda i, k: (i, k)), w, _spec((None, Ks, N), lambda i, k: (k, 0, 0)))],
               jax.ShapeDtypeStruct((T, N), F32), _rows(tm, N), res=(h, _rows(tm, N)))


def _wgrad_rowslots(a, d, tr, name):
    T = a.shape[0]
    Ks = a.shape[1] // N_CHIPS
    N = d.shape[1]
    return _mm(name, (N_CHIPS, T // tr), 1, "tn",
               [(a, _spec((tr, Ks), lambda k, r: (r, k)), d, _spec((tr, N), lambda k, r: (r, 0)))],
               jax.ShapeDtypeStruct((N_CHIPS, Ks, N), F32), _spec((None, Ks, N), lambda k, r: (k, 0, 0)))


def _mm_rowslots_t(d, w, tm, name):
    T, N = d.shape
    Ks = w.shape[1]
    return _mm(name, (T // tm, N_CHIPS), None, "nt",
               [(d, _spec((tm, N), lambda i, j: (i, 0)), w, _spec((None, Ks, N), lambda i, j: (j, 0, 0)))],
               jax.ShapeDtypeStruct((T, N_CHIPS * Ks), F32), _spec((tm, Ks), lambda i, j: (i, j)))


def _gelu_parts(x):
    inner = GELU_C * (x + 0.044715 * (x * x * x))
    t = jnp.tanh(inner)
    return t, GELU_C * (1.0 + 3.0 * 0.044715 * (x * x))


def _gelu_fwd(y, tm, name):
    T, W = y.shape

    def body(y_ref, o_ref):
        x = y_ref[...]
        t, _ = _gelu_parts(x)
        o_ref[...] = (0.5 * x * (1.0 + t)).astype(BF16)

    spec = pl.BlockSpec((tm, W), lambda i: (i, 0))
    return pl.pallas_call(body, name=name, grid=(T // tm,), in_specs=[spec], out_specs=spec,
                          out_shape=jax.ShapeDtypeStruct((T, W), BF16),
                          compiler_params=_params(("parallel",)))(y)


def _gelu_bwd(dyg, y, tm, name):
    T, W = y.shape

    def body(d_ref, y_ref, o_ref):
        x = y_ref[...]
        t, dinner = _gelu_parts(x)
        o_ref[...] = d_ref[...] * (0.5 * (1.0 + t) + 0.5 * x * (1.0 - t * t) * dinner)

    spec = pl.BlockSpec((tm, W), lambda i: (i, 0))
    return pl.pallas_call(body, name=name, grid=(T // tm,), in_specs=[spec, spec], out_specs=spec,
                          out_shape=jax.ShapeDtypeStruct((T, W), F32),
                          compiler_params=_params(("parallel",)))(dyg, y)


def _merge_cols(D):
    cb = 512 if D % 512 == 0 else D
    return cb, D // cb


def _merge_fwd(gates, attn, ga, gb, tm, name):
    T, D = attn.shape
    cb, nc = _merge_cols(D)

    def body(gat_ref, gss_ref, attn_ref, ga_ref, gb_ref, o_ref):
        ssm = ga_ref[...] * _sigmoid(gb_ref[...])
        o_ref[...] = (_sigmoid(gat_ref[...]) * attn_ref[...] + _sigmoid(gss_ref[...]) * ssm).astype(BF16)

    def col(block):
        return pl.BlockSpec((tm, cb), lambda i, j: (i, block * nc + j))

    return pl.pallas_call(
        body, name=name, grid=(T // tm, nc),
        in_specs=[col(0), col(1), col(0), col(0), col(0)],
        out_specs=col(0), out_shape=jax.ShapeDtypeStruct((T, D), BF16),
        compiler_params=_params(("parallel", "parallel")),
    )(gates, gates, attn, ga, gb)


def _merge_bwd(dm, gates, attn, ga, gb, tm, name):
    T, D = attn.shape
    cb, nc = _merge_cols(D)

    def body(dm_ref, gat_ref, gss_ref, attn_ref, ga_ref, gb_ref, dattn_ref, dgat_ref, dgss_ref, dga_ref, dgb_ref):
        d = dm_ref[...]
        sa = _sigmoid(gat_ref[...])
        ss = _sigmoid(gss_ref[...])
        sb = _sigmoid(gb_ref[...])
        gav = ga_ref[...]
        dattn_ref[...] = d * sa
        dgat_ref[...] = (d * attn_ref[...] * (sa * (1.0 - sa))).astype(BF16)
        dgss_ref[...] = (d * (gav * sb) * (ss * (1.0 - ss))).astype(BF16)
        dssm = d * ss
        dga_ref[...] = (dssm * sb).astype(BF16)
        dgb_ref[...] = (dssm * gav * (sb * (1.0 - sb))).astype(BF16)

    def col(block):
        return pl.BlockSpec((tm, cb), lambda i, j: (i, block * nc + j))

    b16 = jax.ShapeDtypeStruct((T, D), BF16)
    return pl.pallas_call(
        body, name=name, grid=(T // tm, nc),
        in_specs=[col(0), col(0), col(1), col(0), col(0), col(0)],
        out_specs=[col(0)] * 5,
        out_shape=[jax.ShapeDtypeStruct((T, D), F32), b16, b16, b16, b16],
        compiler_params=_params(("parallel", "parallel")),
    )(dm, gates, gates, attn, ga, gb)


def _loss_head(h, g, target, tm, name):
    T, D = h.shape
    B, S, _ = target.shape
    L = S + N_META
    nt = T // tm
    tpe = L // tm

    def body(h_ref, g_ref, t_hbm, dh_ref, dhb_ref, dg_ref, loss_ref, tbuf, acc_g, acc_l, sem):
        i = pl.program_id(0)
        b, j = i // tpe, i % tpe

        @pl.when(j == 0)
        def _():
            tbuf[0:N_META, :] = jnp.zeros((N_META, D), F32)
            cp = pltpu.make_async_copy(t_hbm.at[b, pl.ds(0, tm - N_META), :], tbuf.at[pl.ds(N_META, tm - N_META), :], sem)
            cp.start()
            cp.wait()

        @pl.when(j > 0)
        def _():
            cp = pltpu.make_async_copy(t_hbm.at[b, pl.ds(j * tm - N_META, tm), :], tbuf, sem)
            cp.start()
            cp.wait()

        x = h_ref[...]
        gv = g_ref[...]
        r = lax.rsqrt(jnp.mean(x * x, axis=-1, keepdims=True) + NORM_EPS)
        xhat = x * r
        pos = j * tm + lax.broadcasted_iota(jnp.int32, (tm, 1), 0)
        err = jnp.where(pos >= N_META, xhat * gv - tbuf[...], 0.0)
        dy = err * (1.0 / D)
        dxhat = dy * gv
        dh = r * (dxhat - xhat * jnp.mean(dxhat * xhat, axis=-1, keepdims=True))
        dh_ref[...] = dh
        dhb_ref[...] = dh.astype(BF16)
        pg = _fold8(dy * xhat)
        pe = _fold8(err * err)

        @pl.when(i == 0)
        def _():
            acc_g[...] = pg
            acc_l[...] = pe

        @pl.when(i > 0)
        def _():
            acc_g[...] += pg
            acc_l[...] += pe

        @pl.when(i == nt - 1)
        def _():
            dg_ref[...] = jnp.sum(acc_g[...], axis=0, keepdims=True)
            loss_ref[...] = jnp.full((1, D), (0.5 / D) * jnp.sum(acc_l[...]), F32)

    row = pl.BlockSpec((tm, D), lambda i: (i, 0))
    vec = pl.BlockSpec((1, D), lambda i: (0, 0))
    return pl.pallas_call(
        body, name=name, grid=(nt,),
        in_specs=[row, vec, pl.BlockSpec(memory_space=pl.ANY)], out_specs=[row, row, vec, vec],
        out_shape=[jax.ShapeDtypeStruct((T, D), F32), jax.ShapeDtypeStruct((T, D), BF16),
                   jax.ShapeDtypeStruct((1, D), F32), jax.ShapeDtypeStruct((1, D), F32)],
        scratch_shapes=[pltpu.VMEM((tm, D), F32), pltpu.VMEM((8, D), F32), pltpu.VMEM((8, D), F32),
                        pltpu.SemaphoreType.DMA],
        compiler_params=_params(("arbitrary",)),
    )(h, g, target)


def _heads_to_rows(blk):
    return jnp.concatenate([blk[:, g * HEAD_DIM:(g + 1) * HEAD_DIM] for g in range(Q_PER_KV)], axis=0)


def _rows_to_heads(x):
    rows = x.shape[0] // Q_PER_KV
    return jnp.concatenate([x[g * rows:(g + 1) * rows] for g in range(Q_PER_KV)], axis=1)


def _causal(R):
    qi = lax.broadcasted_iota(jnp.int32, (R, BLOCK), 0) & (BLOCK - 1)
    kj = lax.broadcasted_iota(jnp.int32, (R, BLOCK), 1)
    return kj <= qi


def _band_probs(s_band, s_m, sink):
    m = jnp.maximum(jnp.maximum(jnp.max(s_band, axis=-1, keepdims=True), jnp.max(s_m, axis=-1, keepdims=True)), sink)
    e_b, e_m, e_s = jnp.exp(s_band - m), jnp.exp(s_m - m), jnp.exp(sink - m)
    inv = 1.0 / (jnp.sum(e_b, axis=-1, keepdims=True) + jnp.sum(e_m, axis=-1, keepdims=True) + e_s)
    return e_b * inv, e_m * inv, e_s * inv


def _fold_band(tri, two):
    return jnp.where(tri, two[:, BLOCK:2 * BLOCK], two[:, 0:BLOCK])


def _unfold_band(tri, band):
    return jnp.concatenate([jnp.where(tri, 0.0, band), jnp.where(tri, band, 0.0)], axis=1)


def _meta_probs(qm, k_m, sink_m):
    R = qm.shape[0]
    s = lax.dot_general(qm, k_m, _NT, preferred_element_type=F32)
    qi = lax.broadcasted_iota(jnp.int32, (R, N_META), 0) & (N_META - 1)
    kj = lax.broadcasted_iota(jnp.int32, (R, N_META), 1)
    s = jnp.where(kj <= qi, s, NEG_INF)
    m = jnp.maximum(jnp.max(s, axis=-1, keepdims=True), sink_m)
    e, e_s = jnp.exp(s - m), jnp.exp(sink_m - m)
    inv = 1.0 / (jnp.sum(e, axis=-1, keepdims=True) + e_s)
    return e * inv, e_s * inv


def _block_start(n):
    return pl.multiple_of(N_META + n * BLOCK, ROW_ALIGN)


def _kv(blk):
    return blk[:, 0:HEAD_DIM], blk[:, HEAD_DIM:2 * HEAD_DIM]


def _attn_fwd(q, kv, sink_col, sink_meta, B, name, riders=None):
    T, D = q.shape
    L = T // B
    KV = D // QW
    nb = (L - N_META) // BLOCK

    def body(q_ref, kv_ref, sk_ref, skm_ref, o_ref, kvs):
        kvs[...] = kv_ref[...].astype(BF16)
        k_m, v_m = _kv(kvs[0:N_META, :])
        p, _ = _meta_probs(_heads_to_rows(q_ref[0:N_META, :]), k_m, skm_ref[0])
        o_ref[0:N_META, :] = _rows_to_heads(jnp.dot(p.astype(BF16), v_m, preferred_element_type=F32))
        tri = _causal(BLOCK)

        def block(cur, first, keys):
            k2, v2 = _kv(kvs[keys, :])
            qb = _heads_to_rows(q_ref[pl.ds(cur, BLOCK), :])
            s2 = lax.dot_general(qb, k2, _NT, preferred_element_type=F32)
            sm = lax.dot_general(qb, k_m, _NT, preferred_element_type=F32)
            p2s, pms = [], []
            for g in range(Q_PER_KV):
                sl = slice(g * BLOCK, (g + 1) * BLOCK)
                s_band = jnp.where(tri, s2[sl], NEG_INF) if first else _fold_band(tri, s2[sl])
                p_b, p_m, _ = _band_probs(s_band, sm[sl], sk_ref[0, sl, :])
                p2s.append((p_b if first else _unfold_band(tri, p_b)).astype(BF16))
                pms.append(p_m.astype(BF16))
            o = (jnp.dot(jnp.concatenate(p2s, axis=0), v2, preferred_element_type=F32)
                 + jnp.dot(jnp.concatenate(pms, axis=0), v_m, preferred_element_type=F32))
            o_ref[pl.ds(cur, BLOCK), :] = _rows_to_heads(o)

        block(N_META, True, pl.ds(N_META, BLOCK))

        def step(n, carry):
            block(_block_start(n), False, pl.ds(_block_start(n - 1), 2 * BLOCK))
            return carry

        lax.fori_loop(1, nb, step, 0)

    q_spec = pl.BlockSpec((L, QW), lambda b, h: (b, h))
    return _call(body, name, (B, KV),
                 [q_spec, pl.BlockSpec((L, 2 * HEAD_DIM), lambda b, h: (b, h)),
                  pl.BlockSpec((1, Q_PER_KV * BLOCK, 1), lambda b, h: (h, 0, 0)),
                  pl.BlockSpec((1, Q_PER_KV * N_META, 1), lambda b, h: (h, 0, 0))],
                 [q_spec], [jax.ShapeDtypeStruct((T, D), F32)], [pltpu.VMEM((L, 2 * HEAD_DIM), BF16)],
                 ("parallel", "parallel"), (q, kv, sink_col, sink_meta), riders)


def _attn_bwd(q, kv, o, do, sink_col, sink_meta, B, name, riders=None):
    T, D = q.shape
    L = T // B
    KV = D // QW
    nb = (L - N_META) // BLOCK
    R = Q_PER_KV * BLOCK
    scale = HEAD_DIM ** -0.5

    def head_totals(col, rows_per_head):
        rid = lax.broadcasted_iota(jnp.int32, (8, 128), 0)
        out = jnp.zeros((8, 128), F32)
        for g in range(Q_PER_KV):
            out = out + jnp.where(rid == g, jnp.sum(col[g * rows_per_head:(g + 1) * rows_per_head, :]), 0.0)
        return out

    def body(q_ref, kv_ref, o_ref, do_ref, sk_ref, skm_ref, dq_ref, dkv_ref, dsk_ref, kvs, acc, acc_sink):
        b = pl.program_id(1)
        kvs[...] = kv_ref[...].astype(BF16)
        acc[...] = jnp.zeros_like(acc)
        k_m, v_m = _kv(kvs[0:N_META, :])

        qm = _heads_to_rows(q_ref[0:N_META, :])
        dom = _heads_to_rows(do_ref[0:N_META, :])
        delta = jnp.sum(dom * _heads_to_rows(o_ref[0:N_META, :]), axis=-1, keepdims=True)
        p, p_s = _meta_probs(qm, k_m, skm_ref[0])
        domb = dom.astype(BF16)
        ds = (p * (lax.dot_general(domb, v_m, _NT, preferred_element_type=F32) - delta)).astype(BF16)
        dq_ref[0:N_META, :] = _rows_to_heads(jnp.dot(ds, k_m, preferred_element_type=F32) * scale).astype(BF16)
        acc[0:N_META, :] += jnp.concatenate([lax.dot_general(ds, qm, _TN, preferred_element_type=F32),
                                             lax.dot_general(p.astype(BF16), domb, _TN, preferred_element_type=F32)], axis=1)
        sink_tot = head_totals(-p_s * delta, N_META)
        tri = _causal(BLOCK)
        acc_sink[...] = jnp.zeros_like(acc_sink)

        def block(cur, first, keys):
            k2, v2 = _kv(kvs[keys, :])
            rows = pl.ds(cur, BLOCK)
            qb = _heads_to_rows(q_ref[rows, :])
            dob = _heads_to_rows(do_ref[rows, :])
            delta = jnp.sum(dob * _heads_to_rows(o_ref[rows, :]), axis=-1, keepdims=True)
            dobb = dob.astype(BF16)
            s2 = lax.dot_general(qb, k2, _NT, preferred_element_type=F32)
            sm = lax.dot_general(qb, k_m, _NT, preferred_element_type=F32)
            dp2 = lax.dot_general(dobb, v2, _NT, preferred_element_type=F32)
            dpm = lax.dot_general(dobb, v_m, _NT, preferred_element_type=F32)
            ds2s, p2s, dsms, pms = [], [], [], []
            for g in range(Q_PER_KV):
                sl = slice(g * BLOCK, (g + 1) * BLOCK)
                s_band = jnp.where(tri, s2[sl], NEG_INF) if first else _fold_band(tri, s2[sl])
                p_b, p_m, p_s = _band_probs(s_band, sm[sl], sk_ref[0, sl, :])
                ds_b = p_b * ((dp2[sl] if first else _fold_band(tri, dp2[sl])) - delta[sl])
                ds2s.append((ds_b if first else _unfold_band(tri, ds_b)).astype(BF16))
                p2s.append((p_b if first else _unfold_band(tri, p_b)).astype(BF16))
                dsms.append((p_m * (dpm[sl] - delta[sl])).astype(BF16))
                pms.append(p_m.astype(BF16))
                acc_sink[sl, :] += -p_s * delta[sl]
            ds2, p2 = jnp.concatenate(ds2s, axis=0), jnp.concatenate(p2s, axis=0)
            dsm, pm = jnp.concatenate(dsms, axis=0), jnp.concatenate(pms, axis=0)
            dq = jnp.dot(ds2, k2, preferred_element_type=F32) + jnp.dot(dsm, k_m, preferred_element_type=F32)
            dq_ref[rows, :] = _rows_to_heads(dq * scale).astype(BF16)
            acc[keys, :] += jnp.concatenate([lax.dot_general(ds2, qb, _TN, preferred_element_type=F32),
                                             lax.dot_general(p2, dobb, _TN, preferred_element_type=F32)], axis=1)
            acc[0:N_META, :] += jnp.concatenate([lax.dot_general(dsm, qb, _TN, preferred_element_type=F32),
                                                 lax.dot_general(pm, dobb, _TN, preferred_element_type=F32)], axis=1)

        block(N_META, True, pl.ds(N_META, BLOCK))

        def step(n, carry):
            block(_block_start(n), False, pl.ds(_block_start(n - 1), 2 * BLOCK))
            return carry

        lax.fori_loop(1, nb, step, 0)
        dkv_ref[...] = acc[...].astype(BF16)
        tot = sink_tot + head_totals(acc_sink[...], BLOCK)

        @pl.when(b == 0)
        def _():
            dsk_ref[0] = tot

        @pl.when(b > 0)
        def _():
            dsk_ref[0] += tot

    q_spec = pl.BlockSpec((L, QW), lambda h, b: (b, h))
    kv_spec = pl.BlockSpec((L, 2 * HEAD_DIM), lambda h, b: (b, h))
    return _call(body, name, (KV, B),
                 [q_spec, kv_spec, q_spec, q_spec,
                  pl.BlockSpec((1, R, 1), lambda h, b: (h, 0, 0)),
                  pl.BlockSpec((1, Q_PER_KV * N_META, 1), lambda h, b: (h, 0, 0))],
                 [q_spec, kv_spec, pl.BlockSpec((1, 8, 128), lambda h, b: (h, 0, 0))],
                 [jax.ShapeDtypeStruct((T, D), BF16), jax.ShapeDtypeStruct((T, KV * 2 * HEAD_DIM), BF16),
                  jax.ShapeDtypeStruct((KV, 8, 128), F32)],
                 [pltpu.VMEM((L, 2 * HEAD_DIM), BF16), pltpu.VMEM((L, 2 * HEAD_DIM), F32), pltpu.VMEM((R, 1), F32)],
                 ("parallel", "arbitrary"), (q, kv, o, do, sink_col, sink_meta), riders)


def _cmul_add(acc_r, acc_i, lr, li, xr, xi):
    return acc_r + (lr * xr - li * xi), acc_i + (lr * xi + li * xr)


def _cols_per_step(ncol):
    return 2 if ncol % 2 == 0 else 1


def _ssm_fwd(u, bmat, cmat, dskip, tables, nbatch, rc, name):
    T, W = u.shape
    ncol = W // SSM_LANES
    nch = T // rc
    S = STATE_LANES
    cps = _cols_per_step(ncol)
    assert nbatch == 4

    def body(u_ref, b_ref, c_ref, d_ref, tab_ref, y_ref, xs_ref, xp_ref, st_ref, carry_ref):
        ch = pl.program_id(1)

        @pl.when(ch == 0)
        def _():
            carry_ref[...] = jnp.zeros_like(carry_ref)

        uv = u_ref[...]
        for k in range(cps):
            st_ref[:, 2 * S * k:2 * S * (k + 1)] = jnp.dot(uv[:, SSM_LANES * k:SSM_LANES * (k + 1)].astype(BF16), b_ref[k],
                                                           preferred_element_type=F32)
        low = lax.broadcasted_iota(jnp.int32, (8, S), 0) < nbatch

        def tile(k, r0, c_r, c_i):
            re, im = slice(2 * S * k, 2 * S * k + S), slice(2 * S * k + S, 2 * S * (k + 1))
            la_r, la_i = tab_ref[k, :, 0:S], tab_ref[k, :, S:2 * S]
            lb_r, lb_i = tab_ref[k, :, 2 * S:3 * S], tab_ref[k, :, 3 * S:4 * S]
            v_r = st_ref[pl.ds(r0, 8), re]
            v_i = st_ref[pl.ds(r0, 8), im]
            v_r, v_i = _cmul_add(v_r, v_i, la_r, la_i, pltpu.roll(v_r, nbatch, 0), pltpu.roll(v_i, nbatch, 0))
            rc_r, rc_i = pltpu.roll(c_r, nbatch, 0), pltpu.roll(c_i, nbatch, 0)
            cb_r, cb_i = jnp.where(low, rc_r, c_r), jnp.where(low, rc_i, c_i)
            v_r, v_i = _cmul_add(v_r, v_i, lb_r, lb_i, cb_r, cb_i)
            st_ref[pl.ds(r0, 8), re] = v_r
            st_ref[pl.ds(r0, 8), im] = v_i
            p_r = jnp.where(low, rc_r, pltpu.roll(v_r, nbatch, 0))
            p_i = jnp.where(low, rc_i, pltpu.roll(v_i, nbatch, 0))
            return v_r, v_i, p_r, p_i

        def step(i, carry):
            r0 = pl.multiple_of(i * 16, 16)
            out = []
            for k in range(cps):
                re, im = slice(2 * S * k, 2 * S * k + S), slice(2 * S * k + S, 2 * S * (k + 1))
                a_r, a_i, pa_r, pa_i = tile(k, r0, carry[2 * k], carry[2 * k + 1])
                b_r, b_i, pb_r, pb_i = tile(k, r0 + 8, a_r, a_i)
                xp_ref[pl.ds(r0, 16), re] = jnp.concatenate([pa_r, pb_r], axis=0).astype(BF16)
                xp_ref[pl.ds(r0, 16), im] = jnp.concatenate([pa_i, pb_i], axis=0).astype(BF16)
                out += [b_r, b_i]
            return tuple(out)

        halves = tuple(carry_ref[:, S * j:S * (j + 1)] for j in range(2 * cps))
        halves = lax.fori_loop(0, rc // 16, step, halves)
        for j in range(2 * cps):
            carry_ref[:, S * j:S * (j + 1)] = halves[j]
        xb = st_ref[...].astype(BF16)
        xs_ref[...] = xb
        for k in range(cps):
            cols = slice(SSM_LANES * k, SSM_LANES * (k + 1))
            y_ref[:, cols] = (jnp.dot(xb[:, 2 * S * k:2 * S * (k + 1)], c_ref[k], preferred_element_type=F32)
                              + d_ref[:, cols] * uv[:, cols])

    return pl.pallas_call(
        body, name=name, grid=(ncol // cps, nch),
        in_specs=[pl.BlockSpec((rc, cps * SSM_LANES), lambda g, c: (c, g)),
                  pl.BlockSpec((cps, SSM_LANES, 2 * S), lambda g, c: (g, 0, 0)),
                  pl.BlockSpec((cps, 2 * S, SSM_LANES), lambda g, c: (g, 0, 0)),
                  pl.BlockSpec((1, cps * SSM_LANES), lambda g, c: (0, g)),
                  pl.BlockSpec((cps, 8, 4 * S), lambda g, c: (g, 0, 0))],
        out_specs=[pl.BlockSpec((rc, cps * SSM_LANES), lambda g, c: (c, g)),
                   pl.BlockSpec((rc, cps * 2 * S), lambda g, c: (c, g)),
                   pl.BlockSpec((rc, cps * 2 * S), lambda g, c: (c, g))],
        out_shape=[jax.ShapeDtypeStruct((T, W), F32), jax.ShapeDtypeStruct((T, ncol * 2 * S), BF16),
                   jax.ShapeDtypeStruct((T, ncol * 2 * S), BF16)],
        scratch_shapes=[pltpu.VMEM((rc, cps * 2 * S), F32), pltpu.VMEM((8, cps * 2 * S), F32)],
        compiler_params=_params(("parallel", "arbitrary")),
    )(u, bmat, cmat, dskip, tables)


def _ssm_bwd(dy, u, xs, xp, bmat, cmat, dskip, tables, nbatch, rc, name):
    T, W = u.shape
    ncol = W // SSM_LANES
    nch = T // rc
    S = STATE_LANES
    ntile = rc // 16
    cps = _cols_per_step(ncol)

    def body(dy_ref, u_ref, xs_ref, xp_ref, b_ref, c_ref, d_ref, tab_ref,
             du_ref, db_ref, dc_ref, dl_ref, dd_ref, st_ref, carry_ref, accl_ref, accd_ref):
        ch = pl.program_id(1)

        @pl.when(ch == 0)
        def _():
            carry_ref[...] = jnp.zeros_like(carry_ref)
            accl_ref[...] = jnp.zeros_like(accl_ref)
            accd_ref[...] = jnp.zeros_like(accd_ref)
            db_ref[...] = jnp.zeros_like(db_ref)
            dc_ref[...] = jnp.zeros_like(dc_ref)

        dyv = dy_ref[...]
        uv = u_ref[...]
        dyb = dyv.astype(BF16)
        for k in range(cps):
            st_ref[:, 2 * S * k:2 * S * (k + 1)] = lax.dot_general(dyb[:, SSM_LANES * k:SSM_LANES * (k + 1)], c_ref[k], _NT,
                                                                   preferred_element_type=F32)
        low = lax.broadcasted_iota(jnp.int32, (8, S), 0) < nbatch

        def tile(k, r0, p_r, p_i, c_r, c_i, al_r, al_i):
            re, im = slice(2 * S * k, 2 * S * k + S), slice(2 * S * k + S, 2 * S * (k + 1))
            la_r, la_i = tab_ref[k, :, 0:S], tab_ref[k, :, S:2 * S]
            lb_r, lb_i = tab_ref[k, :, 2 * S:3 * S], tab_ref[k, :, 3 * S:4 * S]
            v_r = st_ref[pl.ds(r0, 8), re]
            v_i = st_ref[pl.ds(r0, 8), im]
            v_r, v_i = _cmul_add(v_r, v_i, la_r, la_i, pltpu.roll(v_r, nbatch, 0), pltpu.roll(v_i, nbatch, 0))
            cb_r = jnp.where(low, c_r, pltpu.roll(c_r, nbatch, 0))
            cb_i = jnp.where(low, c_i, pltpu.roll(c_i, nbatch, 0))
            v_r, v_i = _cmul_add(v_r, v_i, lb_r, lb_i, cb_r, cb_i)
            st_ref[pl.ds(r0, 8), re] = v_r
            st_ref[pl.ds(r0, 8), im] = v_i
            al_r = al_r + (v_r * p_r + v_i * p_i)
            al_i = al_i + (v_i * p_r - v_r * p_i)
            return v_r, v_i, al_r, al_i

        def step(j, carry):
            r0 = pl.multiple_of((ntile - 1 - j) * 16, 16)
            out = []
            for k in range(cps):
                re, im = slice(2 * S * k, 2 * S * k + S), slice(2 * S * k + S, 2 * S * (k + 1))
                p_r = xp_ref[pl.ds(r0, 16), re].astype(F32)
                p_i = xp_ref[pl.ds(r0, 16), im].astype(F32)
                mid = tile(k, r0 + 8, p_r[8:16], p_i[8:16], *carry[4 * k:4 * k + 4])
                out += list(tile(k, r0, p_r[0:8], p_i[0:8], *mid))
            return tuple(out)

        init = []
        for k in range(cps):
            init += [carry_ref[:, 2 * S * k:2 * S * k + S], carry_ref[:, 2 * S * k + S:2 * S * (k + 1)],
                     accl_ref[:, 2 * S * k:2 * S * k + S], accl_ref[:, 2 * S * k + S:2 * S * (k + 1)]]
        fin = lax.fori_loop(0, ntile, step, tuple(init))
        for k in range(cps):
            carry_ref[:, 2 * S * k:2 * S * k + S] = fin[4 * k]
            carry_ref[:, 2 * S * k + S:2 * S * (k + 1)] = fin[4 * k + 1]
            accl_ref[:, 2 * S * k:2 * S * k + S] = fin[4 * k + 2]
            accl_ref[:, 2 * S * k + S:2 * S * (k + 1)] = fin[4 * k + 3]
        dsb = st_ref[...].astype(BF16)
        ub = uv.astype(BF16)
        for k in range(cps):
            cols, lanes = slice(SSM_LANES * k, SSM_LANES * (k + 1)), slice(2 * S * k, 2 * S * (k + 1))
            du_ref[:, cols] = (lax.dot_general(dsb[:, lanes], b_ref[k], _NT, preferred_element_type=F32)
                               + d_ref[:, cols] * dyv[:, cols])
            db_ref[k] += lax.dot_general(ub[:, cols], dsb[:, lanes], _TN, preferred_element_type=F32)
            dc_ref[k] += lax.dot_general(xs_ref[:, lanes], dyb[:, cols], _TN, preferred_element_type=F32)
        accd_ref[...] += _fold8(dyv * uv)

        @pl.when(ch == nch - 1)
        def _():
            for k in range(cps):
                dl_ref[k] = jnp.sum(accl_ref[:, 2 * S * k:2 * S * (k + 1)], axis=0, keepdims=True)
            dd_ref[...] = jnp.sum(accd_ref[...], axis=0, keepdims=True)

    rev = lambda g, c: (nch - 1 - c, g)
    return pl.pallas_call(
        body, name=name, grid=(ncol // cps, nch),
        in_specs=[pl.BlockSpec((rc, cps * SSM_LANES), rev), pl.BlockSpec((rc, cps * SSM_LANES), rev),
                  pl.BlockSpec((rc, cps * 2 * S), rev), pl.BlockSpec((rc, cps * 2 * S), rev),
                  pl.BlockSpec((cps, SSM_LANES, 2 * S), lambda g, c: (g, 0, 0)),
                  pl.BlockSpec((cps, 2 * S, SSM_LANES), lambda g, c: (g, 0, 0)),
                  pl.BlockSpec((1, cps * SSM_LANES), lambda g, c: (0, g)),
                  pl.BlockSpec((cps, 8, 4 * S), lambda g, c: (g, 0, 0))],
        out_specs=[pl.BlockSpec((rc, cps * SSM_LANES), rev),
                   pl.BlockSpec((cps, SSM_LANES, 2 * S), lambda g, c: (g, 0, 0)),
                   pl.BlockSpec((cps, 2 * S, SSM_LANES), lambda g, c: (g, 0, 0)),
                   pl.BlockSpec((cps, 1, 2 * S), lambda g, c: (g, 0, 0)),
                   pl.BlockSpec((1, cps * SSM_LANES), lambda g, c: (0, g))],
        out_shape=[jax.ShapeDtypeStruct((T, W), F32),
                   jax.ShapeDtypeStruct((ncol, SSM_LANES, 2 * S), F32),
                   jax.ShapeDtypeStruct((ncol, 2 * S, SSM_LANES), F32),
                   jax.ShapeDtypeStruct((ncol, 1, 2 * S), F32),
                   jax.ShapeDtypeStruct((1, W), F32)],
        scratch_shapes=[pltpu.VMEM((rc, cps * 2 * S), F32), pltpu.VMEM((8, cps * 2 * S), F32),
                        pltpu.VMEM((8, cps * 2 * S), F32), pltpu.VMEM((8, cps * SSM_LANES), F32)],
        compiler_params=_params(("parallel", "arbitrary")),
    )(dy, u, xs, xp, bmat, cmat, dskip, tables)


def _ssm_matrices(a_re, a_im, log_step, b_re, b_im, c_re, c_im):
    G, N = a_re.shape
    ncol = G // GROUPS_PER_COL
    step = jnp.exp(log_step)[:, None]
    mag = jnp.exp(a_re * step)
    ang = a_im * step
    lam_re, lam_im = mag * jnp.cos(ang), mag * jnp.sin(ang)
    den = a_re * a_re + a_im * a_im
    nr, ni = lam_re - 1.0, lam_im
    coef_re = (nr * a_re + ni * a_im) / den
    coef_im = (ni * a_re - nr * a_im) / den
    bb_re = coef_re[..., None] * b_re - coef_im[..., None] * b_im
    bb_im = coef_re[..., None] * b_im + coef_im[..., None] * b_re
    eye = jnp.eye(GROUPS_PER_COL, dtype=F32)
    bb = jnp.stack([bb_re, bb_im]).reshape(2, ncol, GROUPS_PER_COL, N, SSM_GROUP)
    bmat = jnp.einsum("pbgnc,gh->bgcphn", bb, eye).reshape(ncol, SSM_LANES, 2 * STATE_LANES)
    cc = jnp.stack([c_re, -c_im]).reshape(2, ncol, GROUPS_PER_COL, SSM_GROUP, N)
    cmat = jnp.einsum("pbgcn,gh->bpgnhc", cc, eye).reshape(ncol, 2 * STATE_LANES, SSM_LANES)
    lam = jnp.concatenate([lam_re.reshape(ncol, STATE_LANES), lam_im.reshape(ncol, STATE_LANES)], axis=-1)
    return lam, bmat, cmat


def _scan_tables(lam, nbatch, conj):
    S = STATE_LANES
    lr, li = lam[:, None, 0:S], lam[:, None, S:2 * S]
    if conj:
        li = -li
    l2r, l2i = lr * lr - li * li, 2.0 * lr * li
    first = (jnp.arange(8) < nbatch)[None, :, None]
    zero = jnp.zeros_like(lr)
    if conj:
        parts = [jnp.where(first, lr, zero), jnp.where(first, li, zero), jnp.where(first, l2r, lr), jnp.where(first, l2i, li)]
    else:
        parts = [jnp.where(first, zero, lr), jnp.where(first, zero, li), jnp.where(first, lr, l2r), jnp.where(first, li, l2i)]
    return jnp.concatenate([jnp.broadcast_to(p, (lam.shape[0], 8, S)) for p in parts], axis=-1)


def _adamw(w, g, m, v, name):
    R, C = w.shape
    tr = R if R <= 512 else _pick_tile(R, 512, 8)

    def body(w_ref, g_ref, m_ref, v_ref, d_ref, nm_ref, nv_ref):
        gv = g_ref[...]
        mn = ADAM_B1 * m_ref[...] + (1.0 - ADAM_B1) * gv
        vn = ADAM_B2 * v_ref[...] + (1.0 - ADAM_B2) * (gv * gv)
        m_hat = mn / (1.0 - ADAM_B1 ** ADAM_STEP)
        v_hat = vn / (1.0 - ADAM_B2 ** ADAM_STEP)
        d_ref[...] = -ADAM_LR * (m_hat / (jnp.sqrt(v_hat) + ADAM_EPS) + ADAM_WD * w_ref[...])
        nm_ref[...] = mn
        nv_ref[...] = vn

    spec = pl.BlockSpec((tr, C), lambda i: (i, 0))
    shp = jax.ShapeDtypeStruct((R, C), F32)
    return pl.pallas_call(body, name=name, grid=(R // tr,), in_specs=[spec] * 4, out_specs=[spec] * 3,
                          out_shape=[shp, shp, shp], compiler_params=_params(("parallel",)))(w, g, m, v)


_ANY = pl.BlockSpec(memory_space=pl.ANY)


def _place():
    x, y, c = lax.axis_index("x"), lax.axis_index("y"), lax.axis_index("c")
    chips = [(1 - x, y), (x, 1 - y), (1 - x, 1 - y)]
    return x, y, c, chips


def _remote(src, dst, send_sems, recv_sems, k, to):
    return pltpu.make_async_remote_copy(src_ref=src, dst_ref=dst, send_sem=send_sems.at[k], recv_sem=recv_sems.at[k],
                                        device_id=to, device_id_type=MESH_IDS)


class _Riders:
    def __init__(self, srcs, out_shapes, n_sems, copies):
        self.srcs, self.out_shapes, self.n_sems, self.copies = list(srcs), list(out_shapes), n_sems, copies


def _call(body, name, grid, in_specs, out_specs, out_shape, scratch_shapes, sem, args, riders=None):
    if riders is None:
        return pl.pallas_call(body, name=name, grid=grid, in_specs=in_specs, out_specs=out_specs, out_shape=out_shape,
                              scratch_shapes=scratch_shapes, compiler_params=_params(sem))(*args)
    n_in, n_out, n_scr = len(in_specs), len(out_specs), len(scratch_shapes)
    r_in, r_out = len(riders.srcs), len(riders.out_shapes)

    def carrying(*refs):
        a, b = n_in, n_in + r_in
        c, d = b + n_out, b + n_out + r_out
        e = d + n_scr
        sends, arrivals = riders.copies(refs[a:b], refs[c:d], refs[e], refs[e + 1])
        first, last = None, None
        for ax, size in enumerate(grid):
            at0, at1 = pl.program_id(ax) == 0, pl.program_id(ax) == size - 1
            first = at0 if first is None else first & at0
            last = at1 if last is None else last & at1

        @pl.when(first)
        def _():
            for cp in sends:
                cp.start()

        body(*refs[:a], *refs[b:c], *refs[d:e])

        @pl.when(last)
        def _():
            for cp in arrivals:
                cp.wait_recv()
            for cp in sends:
                cp.wait_send()

    outs = pl.pallas_call(
        carrying, name=name, grid=grid, in_specs=list(in_specs) + [_ANY] * r_in,
        out_specs=list(out_specs) + [_ANY] * r_out, out_shape=list(out_shape) + riders.out_shapes,
        scratch_shapes=list(scratch_shapes) + [pltpu.SemaphoreType.DMA((riders.n_sems,)),
                                               pltpu.SemaphoreType.DMA((riders.n_sems,))],
        compiler_params=pltpu.CompilerParams(dimension_semantics=("arbitrary",) * len(grid),
                                             vmem_limit_bytes=VMEM_LIMIT, has_side_effects=True),
    )(*args, *riders.srcs)
    return outs[:n_out], outs[n_out:]


def _gather_riders(shards):
    def copies(srcs, outs, send_sems, recv_sems):
        x, y, c, chips = _place()
        sends, arrivals = [], []
        for i, s in enumerate(shards):
            half = s.shape[0] // 2
            rows = pl.ds(c * half, half)
            for j, chip in enumerate(chips):
                sends.append(_remote(srcs[i].at[rows, :], outs[i].at[2 * x + y, rows, :], send_sems, recv_sems,
                                     3 * i + j, (*chip, c)))
                landed = outs[i].at[2 * chip[0] + chip[1], rows, :]
                arrivals.append(_remote(landed, landed, send_sems, recv_sems, 3 * i + j, (*chip, c)))
        return sends, arrivals

    return _Riders(shards, [jax.ShapeDtypeStruct((N_CHIPS,) + s.shape, s.dtype) for s in shards], 3 * len(shards), copies)


def _exchange_riders(parts):
    def copies(srcs, outs, send_sems, recv_sems):
        x, y, c, chips = _place()
        sends = [_remote(srcs[i].at[2 * chip[0] + chip[1]], outs[i].at[j], send_sems, recv_sems, 3 * i + j, (*chip, c))
                 for i in range(len(parts)) for j, chip in enumerate(chips)]
        return sends, sends

    return _Riders(parts, [jax.ShapeDtypeStruct((3,) + p.shape[1:], p.dtype) for p in parts], 3 * len(parts), copies)


def _forward_halves(gathered, shards, tag):
    n = len(gathered)

    def body(*refs):
        srcs, outs = refs[:n], refs[n:2 * n]
        send_sems, recv_sems = refs[2 * n:]
        x, y, c, chips = _place()
        sibling = (x, y, 1 - c)
        cps = []
        for i in range(n):
            half = gathered[i].shape[1] // 2
            for j, chip in enumerate(chips):
                slot = 2 * chip[0] + chip[1]
                cps.append(_remote(srcs[i].at[slot, pl.ds(c * half, half), :], outs[i].at[slot, pl.ds(c * half, half), :],
                                   send_sems, recv_sems, 3 * i + j, sibling))
        for cp in cps:
            cp.start()
        for i in range(n):
            half = gathered[i].shape[1] // 2
            for j, chip in enumerate(chips):
                theirs = outs[i].at[2 * chip[0] + chip[1], pl.ds((1 - c) * half, half), :]
                _remote(theirs, theirs, send_sems, recv_sems, 3 * i + j, sibling).wait_recv()
        for cp in cps:
            cp.wait_send()

    outs = pl.pallas_call(
        body, name=f"gather_forward_{tag}", in_specs=[_ANY] * n, out_specs=[_ANY] * n,
        out_shape=[jax.ShapeDtypeStruct(g.shape, g.dtype) for g in gathered],
        input_output_aliases={i: i for i in range(n)},
        scratch_shapes=[pltpu.SemaphoreType.DMA((3 * n,)), pltpu.SemaphoreType.DMA((3 * n,))],
        compiler_params=pltpu.CompilerParams(has_side_effects=True),
    )(*gathered)
    slot = 2 * lax.axis_index("x") + lax.axis_index("y")
    return [lax.dynamic_update_slice(o, s[None], (slot, 0, 0)) for o, s in zip(outs, shards)]


def _gather_weights(shards):
    n = len(shards)

    def body(*refs):
        srcs, outs = refs[:n], refs[n:2 * n]
        send_sems, recv_sems = refs[2 * n:]
        x, y, c, chips = _place()
        sibling = (x, y, 1 - c)

        def piece(i, px, py, pc):
            half = shards[i].shape[0] // 2
            return outs[i].at[2 * px + py, pl.ds(pc * half, half), :]

        first = []
        for i in range(n):
            half = shards[i].shape[0] // 2
            for j, chip in enumerate(chips):
                first.append(_remote(srcs[i].at[pl.ds(c * half, half), :], piece(i, x, y, c), send_sems, recv_sems,
                                     6 * i + j, (*chip, c)))
        for cp in first:
            cp.start()
        passed = []
        for i in range(n):
            for j, chip in enumerate(chips):
                _remote(piece(i, *chip, c), piece(i, *chip, c), send_sems, recv_sems, 6 * i + j, (*chip, c)).wait_recv()
                cp = _remote(piece(i, *chip, c), piece(i, *chip, c), send_sems, recv_sems, 6 * i + 3 + j, sibling)
                cp.start()
                passed.append(cp)
        for i in range(n):
            for j, chip in enumerate(chips):
                _remote(piece(i, *chip, 1 - c), piece(i, *chip, 1 - c), send_sems, recv_sems, 6 * i + 3 + j,
                        sibling).wait_recv()
        for cp in first + passed:
            cp.wait_send()

    outs = pl.pallas_call(
        body, name="gather_weights", in_specs=[_ANY] * n, out_specs=[_ANY] * n,
        out_shape=[jax.ShapeDtypeStruct((N_CHIPS,) + s.shape, s.dtype) for s in shards],
        scratch_shapes=[pltpu.SemaphoreType.DMA((6 * n,)), pltpu.SemaphoreType.DMA((6 * n,))],
        compiler_params=pltpu.CompilerParams(has_side_effects=True),
    )(*shards)
    slot = 2 * lax.axis_index("x") + lax.axis_index("y")
    return [lax.dynamic_update_slice(o, s[None], (slot, 0, 0)) for o, s in zip(outs, shards)]


def _swap_halves(grads, tag):
    n = len(grads)

    def body(*refs):
        srcs, outs = refs[:n], refs[n:2 * n]
        send_sems, recv_sems = refs[2 * n:]
        x, y, c, _ = _place()
        cps = []
        for i in range(n):
            half = grads[i].shape[1] // 2
            cps.append(_remote(srcs[i].at[:, pl.ds((1 - c) * half, half), :], outs[i], send_sems, recv_sems, i, (x, y, 1 - c)))
        for cp in cps:
            cp.start()
        for cp in cps:
            cp.wait()

    return pl.pallas_call(
        body, name=f"grad_swap_halves_{tag}", in_specs=[_ANY] * n, out_specs=[_ANY] * n,
        out_shape=[jax.ShapeDtypeStruct((N_CHIPS, g.shape[1] // 2, g.shape[2]), g.dtype) for g in grads],
        scratch_shapes=[pltpu.SemaphoreType.DMA((n,)), pltpu.SemaphoreType.DMA((n,))],
        compiler_params=pltpu.CompilerParams(has_side_effects=True),
    )(*grads)


def _exchange_chips(parts):
    n = len(parts)

    def body(*refs):
        srcs, outs = refs[:n], refs[n:2 * n]
        send_sems, recv_sems = refs[2 * n:]
        x, y, c, chips = _place()
        cps = [_remote(srcs[i].at[2 * chip[0] + chip[1]], outs[i].at[j], send_sems, recv_sems, 3 * i + j, (*chip, c))
               for i in range(n) for j, chip in enumerate(chips)]
        for cp in cps:
            cp.start()
        for cp in cps:
            cp.wait()

    return pl.pallas_call(
        body, name="grad_exchange_chips", in_specs=[_ANY] * n, out_specs=[_ANY] * n,
        out_shape=[jax.ShapeDtypeStruct((3,) + p.shape[1:], p.dtype) for p in parts],
        scratch_shapes=[pltpu.SemaphoreType.DMA((3 * n,)), pltpu.SemaphoreType.DMA((3 * n,))],
        compiler_params=pltpu.CompilerParams(has_side_effects=True),
    )(*parts)


def _join_halves(fulls):
    n = len(fulls)

    def body(*refs):
        srcs, outs = refs[:n], refs[n:2 * n]
        send_sems, recv_sems = refs[2 * n:]
        x, y, c, _ = _place()
        sibling = (x, y, 1 - c)
        cps = []
        for i in range(n):
            h = fulls[i].shape[0] // 2
            cps.append(_remote(srcs[i].at[pl.ds(c * h, h), :], outs[i].at[pl.ds(c * h, h), :], send_sems, recv_sems, i,
                               sibling))
        for cp in cps:
            cp.start()
        for i in range(n):
            h = fulls[i].shape[0] // 2
            theirs = outs[i].at[pl.ds((1 - c) * h, h), :]
            _remote(theirs, theirs, send_sems, recv_sems, i, sibling).wait_recv()
        for cp in cps:
            cp.wait_send()

    return pl.pallas_call(
        body, name="grad_join_halves", in_specs=[_ANY] * n, out_specs=[_ANY] * n,
        out_shape=[jax.ShapeDtypeStruct(f.shape, f.dtype) for f in fulls],
        input_output_aliases={i: i for i in range(n)},
        scratch_shapes=[pltpu.SemaphoreType.DMA((n,)), pltpu.SemaphoreType.DMA((n,))],
        compiler_params=pltpu.CompilerParams(has_side_effects=True),
    )(*fulls)


def _half_tile(h):
    return h if h <= 512 else _pick_tile(h, 512, ROW_ALIGN)


def _sum_halves(g, r1, c_idx, name):
    _, R, C = g.shape
    H = R // 2
    tr = _half_tile(H)
    nblk = H // tr

    def body(c_ref, g_ref, r_ref, p_ref):
        p_ref[...] = (g_ref[...] + r_ref[...]).astype(BF16)

    half = pl.BlockSpec((None, tr, C), lambda s, i, c_ref: (s, c_ref[0] * nblk + i, 0))
    plain = pl.BlockSpec((None, tr, C), lambda s, i, c_ref: (s, i, 0))
    return pl.pallas_call(
        body, name=name,
        grid_spec=pltpu.PrefetchScalarGridSpec(num_scalar_prefetch=1, grid=(N_CHIPS, nblk), in_specs=[half, plain],
                                               out_specs=plain),
        out_shape=jax.ShapeDtypeStruct((N_CHIPS, H, C), BF16),
        compiler_params=_params(("parallel", "parallel")),
    )(c_idx, g, r1)


def _sum_chips(g, r1, r2, idx, name):
    _, R, C = g.shape
    H = R // 2
    tr = _half_tile(H)
    nblk = H // tr

    def body(idx_ref, g_ref, r1_ref, r2_ref, o_ref):
        o_ref[...] = (((g_ref[...] + r1_ref[...]) + r2_ref[0].astype(F32)) + r2_ref[1].astype(F32)) + r2_ref[2].astype(F32)

    return pl.pallas_call(
        body, name=name,
        grid_spec=pltpu.PrefetchScalarGridSpec(
            num_scalar_prefetch=1, grid=(nblk,),
            in_specs=[pl.BlockSpec((None, tr, C), lambda i, idx_ref: (idx_ref[0], idx_ref[1] * nblk + i, 0)),
                      pl.BlockSpec((None, tr, C), lambda i, idx_ref: (idx_ref[0], i, 0)),
                      pl.BlockSpec((3, tr, C), lambda i, idx_ref: (0, i, 0))],
            out_specs=pl.BlockSpec((tr, C), lambda i, idx_ref: (idx_ref[1] * nblk + i, 0))),
        out_shape=jax.ShapeDtypeStruct((R, C), F32),
        compiler_params=_params(("parallel",)),
    )(idx, g, r1, r2)


def _all_reduce_small(v, n_fold, fold_rows, fold_at):
    M, N = v.shape

    def body(x_ref, tot_ref, fold_ref, all_ref, send_sems, recv_sems, local_sem):
        x, y, c, chips = _place()
        me, sibling = (x, y, c), (x, y, 1 - c)

        def rows(px, py, pc):
            return all_ref.at[pl.ds((4 * px + 2 * py + pc) * M, M), :]

        def copy(k, block, to, src=None):
            return _remote(rows(*block) if src is None else src, rows(*block), send_sems, recv_sems, k, to)

        mine = pltpu.make_async_copy(x_ref, rows(*me), local_sem)
        mine.start()
        first = [copy(0, me, sibling, src=x_ref)]
        first += [copy(1 + j, me, (*chip, c), src=x_ref) for j, chip in enumerate(chips)]
        for cp in first:
            cp.start()
        passed = [copy(4 + j, (*chip, c), sibling) for j, chip in enumerate(chips)]
        for j, chip in enumerate(chips):
            copy(1 + j, (*chip, c), me).wait_recv()
            passed[j].start()
        copy(0, sibling, me).wait_recv()
        for j, chip in enumerate(chips):
            copy(4 + j, (*chip, 1 - c), me).wait_recv()
        for cp in first + passed:
            cp.wait_send()
        mine.wait()
        tot = all_ref[0:M, :]
        for d in range(1, 8):
            tot = tot + all_ref[d * M:(d + 1) * M, :]
        tot_ref[...] = tot
        f = tot[fold_at:fold_at + fold_rows, :]
        for e in range(1, n_fold):
            f = f + tot[fold_at + e * fold_rows:fold_at + (e + 1) * fold_rows, :]
        fold_ref[...] = f

    vm = pl.BlockSpec(memory_space=pltpu.VMEM)
    return pl.pallas_call(
        body, name="all_reduce_small", in_specs=[vm], out_specs=[vm, vm],
        out_shape=[jax.ShapeDtypeStruct((M, N), F32), jax.ShapeDtypeStruct((fold_rows, N), F32)],
        scratch_shapes=[pltpu.VMEM((8 * M, N), F32), pltpu.SemaphoreType.DMA((7,)), pltpu.SemaphoreType.DMA((7,)),
                        pltpu.SemaphoreType.DMA],
        compiler_params=pltpu.CompilerParams(has_side_effects=True, vmem_limit_bytes=VMEM_LIMIT),
    )(v)


def _as_rows(a, width):
    flat = a.reshape(-1)
    pad = (-flat.shape[0]) % width
    if pad:
        flat = jnp.concatenate([flat, jnp.zeros((pad,), flat.dtype)])
    return flat.reshape(-1, width)


class _Layout:
    def __init__(self, width, total_mult):
        self.width, self.total_mult = width, total_mult
        self.offsets, self.shapes, self.rows = {}, {}, 0

    def add(self, name, shape):
        r = -(-math.prod(shape) // self.width)
        self.offsets[name], self.shapes[name] = (self.rows, r), tuple(shape)
        self.rows += r

    def align(self, mult):
        gap = (-self.rows) % mult
        if gap:
            self.offsets[f"_gap{self.rows}"], self.shapes[f"_gap{self.rows}"] = (self.rows, gap), (gap, self.width)
            self.rows += gap
        return self.rows

    def pack(self, pieces):
        self.align(self.total_mult)
        parts = [_as_rows(pieces[n].astype(F32), self.width) if n in pieces else jnp.zeros(self.shapes[n], F32)
                 for n in self.offsets]
        return jnp.concatenate(parts, axis=0)

    def unpack(self, buf, name):
        off, r = self.offsets[name]
        shape = self.shapes[name]
        return buf[off:off + r].reshape(-1)[:math.prod(shape)].reshape(shape)


_BIG = ["ffn1_w1", "ffn1_w3", "ffn1_w2", "w_in", "ssm_glu_a", "ssm_glu_b", "w_out", "ffn2_w1", "ffn2_w3", "ffn2_w2"]
_SMALL = ["ffn1_norm", "mix_norm", "ffn2_norm", "final_norm", "attn_sinks", "ssm_a_re", "ssm_a_im", "ssm_log_step",
          "ssm_b_re", "ssm_b_im", "ssm_c_re", "ssm_c_im", "ssm_d"]
_WEIGHTS = ["meta_tokens", "ffn1_norm", "ffn1_w1", "ffn1_w3", "ffn1_w2", "mix_norm", "w_in", "attn_sinks", "ssm_a_re",
            "ssm_a_im", "ssm_log_step", "ssm_b_re", "ssm_b_im", "ssm_c_re", "ssm_c_im", "ssm_d", "ssm_glu_a",
            "ssm_glu_b", "w_out", "ffn2_norm", "ffn2_w1", "ffn2_w3", "ffn2_w2", "final_norm"]


def _kv_interleave(w, kv_heads):
    kvw = kv_heads * HEAD_DIM
    lead = w.shape[:-1]
    k = w[..., 0:kvw].reshape(lead + (kv_heads, 1, HEAD_DIM))
    v = w[..., kvw:2 * kvw].reshape(lead + (kv_heads, 1, HEAD_DIM))
    return jnp.concatenate([jnp.concatenate([k, v], axis=-2).reshape(lead + (2 * kvw,)), w[..., 2 * kvw:]], axis=-1)


def _kv_deinterleave(w, kv_heads):
    kvw = kv_heads * HEAD_DIM
    lead = w.shape[:-1]
    kv = w[..., 0:2 * kvw].reshape(lead + (kv_heads, 2, HEAD_DIM))
    return jnp.concatenate([kv[..., 0, :].reshape(lead + (kvw,)), kv[..., 1, :].reshape(lead + (kvw,)), w[..., 2 * kvw:]],
                           axis=-1)


def _step(x, target, w, m, v):
    B, S, D = x.shape
    L = S + N_META
    T = B * L
    H = D // HEAD_DIM
    KV = H // Q_PER_KV
    SW = D // 2
    tm = _pick_tile(L, ROW_TILE_CAP, ROW_ALIGN)
    rc = _pick_tile(L, ROW_TILE_CAP // B, 4) * B
    tw = _pick_tile(T, 3 * ROW_TILE_CAP, ROW_ALIGN)
    my_c = lax.axis_index("c")
    my_slot = 2 * lax.axis_index("x") + lax.axis_index("y")

    groups = {"ffn1": ["ffn1_w1", "ffn1_w3", "ffn1_w2"], "mix": ["w_in", "ssm_glu_a", "ssm_glu_b", "w_out"],
              "ffn2": ["ffn2_w1", "ffn2_w3", "ffn2_w2"]}
    shards = {n: w[n][0].astype(BF16) for n in _BIG}
    gathered = _gather_weights([shards[n] for n in groups["ffn1"]] + [w["meta_tokens"]])
    ws = dict(zip(groups["ffn1"], gathered[:-1]))
    meta = jnp.transpose(gathered[-1], (1, 0, 2)).reshape(N_META, D)

    def arrive(group, landed):
        mine = [shards[n] for n in groups[group]]
        ws.update(zip(groups[group], _forward_halves(landed, mine, group)))

    g_ffn1, g_mix, g_ffn2 = w["ffn1_norm"], w["mix_norm"], w["ffn2_norm"]
    g_final = w["final_norm"].reshape(1, D)

    h0 = jnp.concatenate([jnp.broadcast_to(meta[None], (B, N_META, D)), x], axis=1).reshape(T, D)

    def ffn_fwd(h, g, tag, carry=None):
        n = _rmsnorm_fwd(h, g, tm, f"{tag}_norm")
        riders = None if carry is None else _gather_riders([shards[k] for k in groups[carry]])
        out = _ffn_up(n, ws[f"{tag}_w1"], ws[f"{tag}_w3"], tm, f"{tag}_up", riders)
        if carry is not None:
            out, landed = out
            arrive(carry, landed)
        a, c, s = out
        return _ffn_down(s, ws[f"{tag}_w2"], h, tm, f"{tag}_down"), (n, a, c, s)

    h1, saved1 = ffn_fwd(h0, g_ffn1, "ffn1", carry="mix")
    w_kvu = _kv_interleave(ws["w_in"][1], KV)
    hn = _rmsnorm_fwd(h1, g_mix, tm, "mix_norm")
    q = _mm_colslots(hn, ws["w_in"], BF16, "w_in_q", tm, first=0, count=1, scale=HEAD_DIM ** -0.5)
    kvu = _mm_plain(hn, w_kvu, "nn", F32, "w_in_kvu", tm)
    gates = _mm_colslots(hn, ws["w_in"], F32, "w_in_gates", tm, first=2, count=2)

    sinks = w["attn_sinks"].reshape(KV, Q_PER_KV, 1, 1)
    sink_col = jnp.broadcast_to(sinks, (KV, Q_PER_KV, BLOCK, 1)).reshape(KV, Q_PER_KV * BLOCK, 1)
    sink_meta = jnp.broadcast_to(sinks, (KV, Q_PER_KV, N_META, 1)).reshape(KV, Q_PER_KV * N_META, 1)
    (attn,), landed = _attn_fwd(q, kvu, sink_col, sink_meta, B, "attn_fwd",
                                _gather_riders([shards[k] for k in groups["ffn2"]]))
    arrive("ffn2", landed)

    def to_time_major(a2d):
        return jnp.transpose(a2d.reshape(B, L, a2d.shape[-1]), (1, 0, 2)).reshape(T, a2d.shape[-1])

    def to_batch_major(a2d):
        return jnp.transpose(a2d.reshape(L, B, a2d.shape[-1]), (1, 0, 2)).reshape(T, a2d.shape[-1])

    ssm_args = (w["ssm_a_re"][0], w["ssm_a_im"][0], w["ssm_log_step"][0], w["ssm_b_re"][0], w["ssm_b_im"][0],
                w["ssm_c_re"][0], w["ssm_c_im"][0])
    (lam, bmat, cmat), ssm_vjp = jax.vjp(_ssm_matrices, *ssm_args)
    bmat16, cmat16 = bmat.astype(BF16), cmat.astype(BF16)
    u_t = to_time_major(kvu[:, SW:])
    y_t, xs, xp = _ssm_fwd(u_t, bmat16, cmat16, w["ssm_d"], _scan_tables(lam, B, False), B, rc, "ssm_fwd")
    y0 = to_batch_major(y_t)
    yg = _gelu_fwd(y0, tm, "gelu_fwd")
    ga = _mm_colslots(yg, ws["ssm_glu_a"], F32, "glu_a", tm)
    gb = _mm_colslots(yg, ws["ssm_glu_b"], F32, "glu_b", tm)
    merged = _merge_fwd(gates, attn, ga, gb, tm, "merge_fwd")
    h2 = _mm_rowslots(merged, ws["w_out"], h1, tm, "w_out")
    h3, saved2 = ffn_fwd(h2, g_ffn2, "ffn2")
    dh3, dh3b, dg_final, loss_row = _loss_head(h3, g_final, target, tm, "loss_head")

    grads, swapped, received = {}, {}, {}
    c_idx = my_c.reshape(1).astype(jnp.int32)
    idx = jnp.stack([my_slot, my_c]).astype(jnp.int32)

    def chip_sums(group):
        names = groups[group]
        for n, r in zip(names, _swap_halves([grads[n] for n in names], group)):
            swapped[n] = r
        return [_sum_halves(grads[n], swapped[n], c_idx, f"grad_sum_halves_{n}") for n in names]

    def ffn_bwd(h, g, saved, dh, dhb, tag, carry=None):
        n, a, c, s = saved
        w1, w3, w2 = ws[f"{tag}_w1"], ws[f"{tag}_w3"], ws[f"{tag}_w2"]
        grads[f"{tag}_w2"] = _wgrad_hidden_rows(s, dhb, tw, f"{tag}_dw2", 0.5)
        if carry is None:
            da, dc = _ffn_dhidden(dhb, w2, a, c, tm, f"{tag}_dhidden")
        else:
            (da, dc), got = _ffn_dhidden(dhb, w2, a, c, tm, f"{tag}_dhidden", _exchange_riders(chip_sums(carry)))
            received.update(zip(groups[carry], got))
        grads[f"{tag}_w1"] = _wgrad_hidden_cols(n, da, tw, f"{tag}_dw1")
        grads[f"{tag}_w3"] = _wgrad_hidden_cols(n, dc, tw, f"{tag}_dw3")
        dh_in, dhb_in, grads[f"{tag}_norm"] = _ffn_dn(da, w1, dc, w3, h, g, dh, tm, f"{tag}_dn")
        return dh_in, dhb_in

    dh2, dh2b = ffn_bwd(h2, g_ffn2, saved2, dh3, dh3b, "ffn2")

    grads["w_out"] = _wgrad_rowslots(merged, dh2b, tw, "dw_out")
    dmerged = _mm_rowslots_t(dh2b, ws["w_out"], tm, "dmerged")
    dattn, dgat, dgss, dga, dgb = _merge_bwd(dmerged, gates, attn, ga, gb, tm, "merge_bwd")
    grads["ssm_glu_a"] = _wgrad_colslots(yg, dga, tw, "dglu_a")
    grads["ssm_glu_b"] = _wgrad_colslots(yg, dgb, tw, "dglu_b")
    dyg = _mm_colslots_t([(dga, ws["ssm_glu_a"]), (dgb, ws["ssm_glu_b"])], tm, "dyg")
    dy0 = _gelu_bwd(dyg, y0, tm, "gelu_bwd")
    du_t, dbmat, dcmat, dlam, dd = _ssm_bwd(to_time_major(dy0), u_t, xs, xp, bmat16, cmat16, w["ssm_d"],
                                            _scan_tables(lam, B, True), B, rc, "ssm_bwd")
    d_ssm = ssm_vjp((dlam[:, 0, :], dbmat, dcmat))
    for n, gval in zip(["ssm_a_re", "ssm_a_im", "ssm_log_step", "ssm_b_re", "ssm_b_im", "ssm_c_re", "ssm_c_im"], d_ssm):
        grads[n] = gval[None]
    grads["ssm_d"] = dd

    (dq, dkv, dsink), got = _attn_bwd(q, kvu, attn, dattn, sink_col, sink_meta, B, "attn_bwd",
                                      _exchange_riders(chip_sums("ffn2")))
    received.update(zip(groups["ffn2"], got))
    grads["attn_sinks"] = dsink[:, 0:Q_PER_KV, 0].reshape(1, H)
    dkvu = jnp.concatenate([dkv, to_batch_major(du_t).astype(BF16)], axis=1)
    pieces = [dq, dkvu, dgat, dgss]
    dw_in = [_wgrad_plain(hn, p, f"dw_in_{k}", tw) for k, p in enumerate(pieces)]
    dw_in[1] = _kv_deinterleave(dw_in[1], KV)
    grads["w_in"] = jnp.stack(dw_in)
    w_in_parts = [ws["w_in"][0], w_kvu, ws["w_in"][2], ws["w_in"][3]]
    whole = _once((D, D), lambda i: (0, 0))
    dh1, dh1b, grads["mix_norm"] = _mm_norm_bwd(
        "dhn", "nt", [(p, _spec((tm, D), lambda i: (i, 0)), wp, whole) for p, wp in zip(pieces, w_in_parts)],
        h1, g_mix, dh2, tm)
    dh0, _ = ffn_bwd(h0, g_ffn1, saved1, dh1, dh1b, "ffn1", carry="mix")
    dh0 = dh0.reshape(B, L, D)
    grad_x = dh0[:, N_META:, :]

    grads["final_norm"] = dg_final
    slay = _Layout(D, 8)
    for n in _SMALL:
        slay.add(n, w[n].shape)
    slay.add("loss", (1, D))
    meta_at = slay.align(8)
    slay.add("meta", (B * N_META, D))
    small = slay.pack({**{n: grads[n] for n in _SMALL}, "loss": loss_row, "meta": dh0[:, :N_META, :]})
    tot_small, dmeta = _all_reduce_small(small, B, N_META, meta_at)
    loss = slay.unpack(tot_small, "loss")[0, 0]
    for n in _SMALL:
        grads[n] = slay.unpack(tot_small, n)
    cw = D // N_CHIPS
    grads["meta_tokens"] = lax.dynamic_slice_in_dim(dmeta, my_slot * cw, cw, axis=1)

    received.update(zip(groups["ffn1"], _exchange_chips(chip_sums("ffn1"))))
    fulls = [_sum_chips(grads[n], swapped[n], received[n], idx, f"grad_sum_chips_{n}") for n in _BIG]
    for n, f in zip(_BIG, _join_halves(fulls)):
        grads[n] = f[None]

    delta, new_m, new_v = {}, {}, {}
    for n in _BIG + ["meta_tokens"]:
        shp = w[n].shape
        two = (shp[-2], shp[-1])
        d_, m_, v_ = _adamw(w[n].reshape(two), grads[n].reshape(two), m[n].reshape(two), v[n].reshape(two), f"adamw_{n}")
        delta[n], new_m[n], new_v[n] = d_.reshape(shp), m_.reshape(shp), v_.reshape(shp)
        grads[n] = grads[n].reshape(shp)
    play = _Layout(D, 8)
    for n in _SMALL:
        play.add(n, w[n].shape)
    d_, m_, v_ = _adamw(play.pack({n: w[n] for n in _SMALL}), play.pack({n: grads[n] for n in _SMALL}),
                        play.pack({n: m[n] for n in _SMALL}), play.pack({n: v[n] for n in _SMALL}), "adamw_small")
    for n in _SMALL:
        delta[n], new_m[n], new_v[n] = play.unpack(d_, n), play.unpack(m_, n), play.unpack(v_, n)
        grads[n] = grads[n].reshape(w[n].shape)

    return (loss, grad_x, *[grads[n] for n in _WEIGHTS], *[delta[n] for n in _WEIGHTS],
            *[new_m[n] for n in _WEIGHTS], *[new_v[n] for n in _WEIGHTS])


def kernel(x, meta_tokens, ffn1_norm, ffn1_w1, ffn1_w3, ffn1_w2, mix_norm, w_in, attn_sinks, ssm_a_re, ssm_a_im, ssm_log_step, ssm_b_re, ssm_b_im, ssm_c_re, ssm_c_im, ssm_d, ssm_glu_a, ssm_glu_b, w_out, ffn2_norm, ffn2_w1, ffn2_w3, ffn2_w2, final_norm, loss_target, m_meta_tokens, m_ffn1_norm, m_ffn1_w1, m_ffn1_w3, m_ffn1_w2, m_mix_norm, m_w_in, m_attn_sinks, m_ssm_a_re, m_ssm_a_im, m_ssm_log_step, m_ssm_b_re, m_ssm_b_im, m_ssm_c_re, m_ssm_c_im, m_ssm_d, m_ssm_glu_a, m_ssm_glu_b, m_w_out, m_ffn2_norm, m_ffn2_w1, m_ffn2_w3, m_ffn2_w2, m_final_norm, v_meta_tokens, v_ffn1_norm, v_ffn1_w1, v_ffn1_w3, v_ffn1_w2, v_mix_norm, v_w_in, v_attn_sinks, v_ssm_a_re, v_ssm_a_im, v_ssm_log_step, v_ssm_b_re, v_ssm_b_im, v_ssm_c_re, v_ssm_c_im, v_ssm_d, v_ssm_glu_a, v_ssm_glu_b, v_w_out, v_ffn2_norm, v_ffn2_w1, v_ffn2_w3, v_ffn2_w2, v_final_norm):
    args = locals()
    w = {n: args[n] for n in _WEIGHTS}
    m = {n: args["m_" + n] for n in _WEIGHTS}
    v = {n: args["v_" + n] for n in _WEIGHTS}
    return _step(x, loss_target, w, m, v)
```

```python
import functools
import math

import jax
import jax.numpy as jnp
from jax import lax
from jax.experimental import pallas as pl
from jax.experimental.pallas import tpu as pltpu

F32 = jnp.float32
BF16 = jnp.bfloat16
MESH_IDS = pl.DeviceIdType.MESH

N_CHIPS = 4
N_META = 16
HEAD_DIM = 64
Q_PER_KV = 4
QW = Q_PER_KV * HEAD_DIM
BLOCK = 128
SSM_GROUP = 16
SSM_STATE = 64
SSM_LANES = 128
GROUPS_PER_COL = SSM_LANES // SSM_GROUP
STATE_LANES = GROUPS_PER_COL * SSM_STATE
NORM_EPS = 1e-6
NEG_INF = -1e30
ADAM_LR, ADAM_B1, ADAM_B2, ADAM_EPS, ADAM_WD, ADAM_STEP = 0.001, 0.9, 0.999, 1e-08, 0.01, 10
GELU_C = math.sqrt(2.0 / math.pi)
ROW_ALIGN = 16
VMEM_LIMIT = 56 * 1024 * 1024
ROW_TILE_CAP = 688

_NN = (((1,), (0,)), ((), ()))
_NT = (((1,), (1,)), ((), ()))
_TN = (((0,), (0,)), ((), ()))
_DIMS = {"nn": _NN, "nt": _NT, "tn": _TN}


def _params(sem, **kw):
    return pltpu.CompilerParams(dimension_semantics=sem, vmem_limit_bytes=VMEM_LIMIT, **kw)


def _pick_tile(n, cap, mult):
    best = None
    for t in range(mult, min(n, cap) + 1, mult):
        if n % t == 0:
            best = t
    if best is None:
        raise ValueError(f"no tile for {n} (cap {cap}, multiple of {mult})")
    return best


def _sigmoid(x):
    return 1.0 / (1.0 + jnp.exp(-x))


def _spec(block, index_map):
    return pl.BlockSpec(block, index_map)


def _sum_dots(ins, mode):
    tot = None
    for p in range(len(ins) // 2):
        a_ref, b_ref = ins[2 * p], ins[2 * p + 1]
        for sl in ([None] if len(b_ref.shape) == 2 else range(b_ref.shape[0])):
            if sl is None:
                a, b = a_ref[...], b_ref[...]
            elif len(a_ref.shape) == 3:
                a, b = a_ref[sl], b_ref[sl]
            else:
                width = a_ref.shape[1] // b_ref.shape[0]
                a, b = a_ref[:, sl * width:(sl + 1) * width], b_ref[sl]
            d = lax.dot_general(a.astype(BF16), b.astype(BF16), _DIMS[mode], preferred_element_type=F32)
            tot = d if tot is None else tot + d
    return tot


def _mm(name, grid, kaxis, mode, pairs, out_shape, out_spec, scale=1.0, res=None):
    npairs = len(pairs)
    has_res = res is not None
    gk = 1 if kaxis is None else grid[kaxis]
    acc_shape = tuple(d for d in out_spec.block_shape if d is not None)

    def body(*refs):
        res_ref = refs[2 * npairs] if has_res else None
        o_ref = refs[2 * npairs + has_res]
        tot = _sum_dots(refs[:2 * npairs], mode)

        def finish(acc):
            r = acc * scale if scale != 1.0 else acc
            if has_res:
                r = res_ref[...] + r
            o_ref[...] = r.astype(o_ref.dtype)

        if gk == 1:
            finish(tot)
        else:
            acc_ref = refs[-1]
            k = pl.program_id(kaxis)

            @pl.when(k == 0)
            def _():
                acc_ref[...] = tot

            @pl.when(k > 0)
            def _():
                acc_ref[...] += tot

            @pl.when(k == gk - 1)
            def _():
                finish(acc_ref[...])

    in_specs, args = [], []
    for a, a_spec, b, b_spec in pairs:
        in_specs += [a_spec, b_spec]
        args += [a, b]
    if has_res:
        in_specs.append(res[1])
        args.append(res[0])
    sem = tuple("arbitrary" if ax == kaxis else "parallel" for ax in range(len(grid)))
    return pl.pallas_call(
        body, name=name, grid=grid, in_specs=in_specs, out_specs=out_spec, out_shape=out_shape,
        scratch_shapes=[pltpu.VMEM(acc_shape, F32)] if gk > 1 else [],
        compiler_params=_params(sem),
    )(*args)


def _mm_plain(a, b, mode, out_dtype, name, tm, scale=1.0):
    M, K = a.shape
    N = b.shape[1] if mode == "nn" else b.shape[0]
    return _mm(name, (M // tm,), None, mode,
               [(a, _spec((tm, K), lambda i: (i, 0)), b, _spec(b.shape, lambda i: (0, 0)))],
               jax.ShapeDtypeStruct((M, N), out_dtype), _spec((tm, N), lambda i: (i, 0)), scale=scale)


def _wgrad_plain(a, b, name, tr):
    R, M = a.shape
    N = b.shape[1]
    return _mm(name, (R // tr,), 0, "tn",
               [(a, _spec((tr, M), lambda r: (r, 0)), b, _spec((tr, N), lambda r: (r, 0)))],
               jax.ShapeDtypeStruct((M, N), F32), _spec((M, N), lambda r: (0, 0)))


def _rmsnorm_fwd(h, g, tm, name):
    T, D = h.shape

    def body(h_ref, g_ref, o_ref):
        x = h_ref[...]
        r = lax.rsqrt(jnp.mean(x * x, axis=-1, keepdims=True) + NORM_EPS)
        o_ref[...] = ((x * r) * g_ref[...]).astype(BF16)

    return pl.pallas_call(
        body, name=name, grid=(T // tm,),
        in_specs=[pl.BlockSpec((tm, D), lambda i: (i, 0)), pl.BlockSpec((1, D), lambda i: (0, 0))],
        out_specs=pl.BlockSpec((tm, D), lambda i: (i, 0)),
        out_shape=jax.ShapeDtypeStruct((T, D), BF16),
        compiler_params=_params(("parallel",)),
    )(h, g)


def _fold8(x):
    return jnp.sum(x.reshape(x.shape[0] // 8, 8, x.shape[1]), axis=0)


def _mm_norm_bwd(name, mode, pairs, h, g, dres, tm):
    T, D = h.shape
    nt = T // tm
    npairs = len(pairs)

    def body(*refs):
        h_ref, g_ref, dres_ref, dh_ref, dhb_ref, dg_ref, acc_ref = refs[2 * npairs:]
        i = pl.program_id(0)
        x = h_ref[...]
        r = lax.rsqrt(jnp.mean(x * x, axis=-1, keepdims=True) + NORM_EPS)
        xhat = x * r
        dy = _sum_dots(refs[:2 * npairs], mode)
        dxhat = dy * g_ref[...]
        dx = r * (dxhat - xhat * jnp.mean(dxhat * xhat, axis=-1, keepdims=True))
        dh = dres_ref[...] + dx
        dh_ref[...] = dh
        dhb_ref[...] = dh.astype(BF16)
        part = _fold8(dy * xhat)

        @pl.when(i == 0)
        def _():
            acc_ref[...] = part

        @pl.when(i > 0)
        def _():
            acc_ref[...] += part

        @pl.when(i == nt - 1)
        def _():
            dg_ref[...] = jnp.sum(acc_ref[...], axis=0, keepdims=True)

    row = pl.BlockSpec((tm, D), lambda i: (i, 0))
    vec = pl.BlockSpec((1, D), lambda i: (0, 0))
    in_specs, args = [], []
    for a, a_spec, b, b_spec in pairs:
        in_specs += [a_spec, b_spec]
        args += [a, b]
    return pl.pallas_call(
        body, name=name, grid=(nt,),
        in_specs=in_specs + [row, vec, row],
        out_specs=[row, row, vec],
        out_shape=[jax.ShapeDtypeStruct((T, D), F32), jax.ShapeDtypeStruct((T, D), BF16),
                   jax.ShapeDtypeStruct((1, D), F32)],
        scratch_shapes=[pltpu.VMEM((8, D), F32)],
        compiler_params=_params(("arbitrary",)),
    )(*args, h, g, dres)


def _ffn_up(n, w1, w3, tm, name, riders=None):
    T, D = n.shape
    Fs = w1.shape[2]

    def body(n_ref, w1_ref, w3_ref, a_ref, c_ref, s_ref):
        x = n_ref[...]
        a = jnp.dot(x, w1_ref[...], preferred_element_type=F32)
        c = jnp.dot(x, w3_ref[...], preferred_element_type=F32)
        a_ref[...] = a.astype(BF16)
        c_ref[...] = c.astype(BF16)
        s_ref[...] = (a * _sigmoid(a) * c).astype(BF16)

    w_spec = _spec((None, D, Fs), lambda s, i: (s, 0, 0))
    o_spec = _spec((None, tm, Fs), lambda s, i: (s, i, 0))
    o_shape = jax.ShapeDtypeStruct((N_CHIPS, T, Fs), BF16)
    return _call(body, name, (N_CHIPS, T // tm), [_spec((tm, D), lambda s, i: (i, 0)), w_spec, w_spec],
                 [o_spec, o_spec, o_spec], [o_shape, o_shape, o_shape], [], ("parallel", "parallel"), (n, w1, w3), riders)


def _ffn_down(s, w2, h, tm, name):
    _, T, Fs = s.shape
    D = w2.shape[2]
    row = _spec((tm, D), lambda i: (i, 0))
    return _mm(name, (T // tm,), None, "nn",
               [(s, _spec((N_CHIPS, tm, Fs), lambda i: (0, i, 0)), w2, _spec((N_CHIPS, Fs, D), lambda i: (0, 0, 0)))],
               jax.ShapeDtypeStruct((T, D), F32), row, scale=0.5, res=(h, row))


def _ffn_dhidden(dhb, w2, a, c, tm, name, riders=None):
    T, D = dhb.shape
    Fs = w2.shape[1]

    def body(dh_ref, w2_ref, a_ref, c_ref, da_ref, dc_ref):
        d = 0.5 * lax.dot_general(dh_ref[...], w2_ref[...], _NT, preferred_element_type=F32)
        av = a_ref[...].astype(F32)
        cv = c_ref[...].astype(F32)
        sg = _sigmoid(av)
        da_ref[...] = (d * cv * (sg * (1.0 + av * (1.0 - sg)))).astype(BF16)
        dc_ref[...] = (d * (av * sg)).astype(BF16)

    h_spec = _spec((None, tm, Fs), lambda s, i: (s, i, 0))
    o_shape = jax.ShapeDtypeStruct((N_CHIPS, T, Fs), BF16)
    return _call(body, name, (N_CHIPS, T // tm),
                 [_spec((tm, D), lambda s, i: (i, 0)), _spec((None, Fs, D), lambda s, i: (s, 0, 0)), h_spec, h_spec],
                 [h_spec, h_spec], [o_shape, o_shape], [], ("parallel", "parallel"), (dhb, w2, a, c), riders)


def _wgrad_hidden_rows(s, dhb, tr, name, scale):
    _, T, Fs = s.shape
    D = dhb.shape[1]
    return _mm(name, (N_CHIPS, T // tr), 1, "tn",
               [(s, _spec((None, tr, Fs), lambda k, r: (k, r, 0)), dhb, _spec((tr, D), lambda k, r: (r, 0)))],
               jax.ShapeDtypeStruct((N_CHIPS, Fs, D), F32), _spec((None, Fs, D), lambda k, r: (k, 0, 0)), scale=scale)


def _wgrad_hidden_cols(n, da, tr, name):
    T, D = n.shape
    Fs = da.shape[2]
    return _mm(name, (N_CHIPS, T // tr), 1, "tn",
               [(n, _spec((tr, D), lambda k, r: (r, 0)), da, _spec((None, tr, Fs), lambda k, r: (k, r, 0)))],
               jax.ShapeDtypeStruct((N_CHIPS, D, Fs), F32), _spec((None, D, Fs), lambda k, r: (k, 0, 0)))


def _once(block, index_map):
    return pl.BlockSpec(block, index_map, pipeline_mode=pl.Buffered(1))


def _ffn_dn(da, w1, dc, w3, h, g, dres, tm, name):
    _, T, Fs = da.shape
    D = w1.shape[1]
    h_spec = _spec((N_CHIPS, tm, Fs), lambda i: (0, i, 0))
    w_spec = _once((N_CHIPS, D, Fs), lambda i: (0, 0, 0))
    return _mm_norm_bwd(name, "nt", [(da, h_spec, w1, w_spec), (dc, h_spec, w3, w_spec)], h, g, dres, tm)


def _mm_side_by_side(a, w, mode, out_dtype, name, tm, first=0, count=N_CHIPS, scale=1.0):
    T, K = a.shape
    assert first % count == 0
    n = w.shape[2] if mode == "nn" else w.shape[1]

    def body(a_ref, w_ref, o_ref):
        av = a_ref[...].astype(BF16)
        for j in range(count):
            r = lax.dot_general(av, w_ref[j].astype(BF16), _DIMS[mode], preferred_element_type=F32)
            o_ref[:, j * n:(j + 1) * n] = (r * scale if scale != 1.0 else r).astype(o_ref.dtype)

    return pl.pallas_call(
        body, name=name, grid=(T // tm,),
        in_specs=[_spec((tm, K), lambda i: (i, 0)), _once((count,) + w.shape[1:], lambda i: (first // count, 0, 0))],
        out_specs=_spec((tm, count * n), lambda i: (i, 0)),
        out_shape=jax.ShapeDtypeStruct((T, count * n), out_dtype),
        compiler_params=_params(("parallel",)),
    )(a, w)


def _mm_colslots(a, w, out_dtype, name, tm, first=0, count=N_CHIPS, scale=1.0):
    return _mm_side_by_side(a, w, "nn", out_dtype, name, tm, first, count, scale)


def _wgrad_colslots(a, d, tr, name):
    T, K = a.shape
    Ns = d.shape[1] // N_CHIPS
    return _mm(name, (N_CHIPS, T // tr), 1, "tn",
               [(a, _spec((tr, K), lambda k, r: (r, 0)), d, _spec((tr, Ns), lambda k, r: (r, k)))],
               jax.ShapeDtypeStruct((N_CHIPS, K, Ns), F32), _spec((None, K, Ns), lambda k, r: (k, 0, 0)))


def _mm_colslots_t(pairs, tm, name):
    d0, w0 = pairs[0]
    T = d0.shape[0]
    K = w0.shape[1]
    d_spec = _spec((tm, d0.shape[1]), lambda i: (i, 0))
    w_spec = _once(w0.shape, lambda i: (0, 0, 0))
    return _mm(name, (T // tm,), None, "nt", [(d, d_spec, w, w_spec) for d, w in pairs],
               jax.ShapeDtypeStruct((T, K), F32), _spec((tm, K), lambda i: (i, 0)))


def _mm_rowslots(a, w, h, tm, name):
    T = a.shape[0]
    N = w.shape[2]
    row = _spec((tm, N), lambda i: (i, 0))
    return _mm(name, (T // tm,), None, "nn",
               [(a, _spec((tm, a.shape[1]), lambda i: (i, 0)), w, _once(w.shape, lambda i: (0, 0, 0)))],
               jax.ShapeDtypeStruct((T, N), F32), row, res=(h, row))


def _wgrad_rowslots(a, d, tr, name):
    T = a.shape[0]
    Ks = a.shape[1] // N_CHIPS
    N = d.shape[1]
    return _mm(name, (N_CHIPS, T // tr), 1, "tn",
               [(a, _spec((tr, Ks), lambda k, r: (r, k)), d, _spec((tr, N), lambda k, r: (r, 0)))],
               jax.ShapeDtypeStruct((N_CHIPS, Ks, N), F32), _spec((None, Ks, N), lambda k, r: (k, 0, 0)))


def _mm_rowslots_t(d, w, tm, name):
    return _mm_side_by_side(d, w, "nt", F32, name, tm)


def _gelu_parts(x):
    inner = GELU_C * (x + 0.044715 * (x * x * x))
    t = jnp.tanh(inner)
    return t, GELU_C * (1.0 + 3.0 * 0.044715 * (x * x))


def _gelu_fwd(y, tm, name):
    T, W = y.shape

    def body(y_ref, o_ref):
        x = y_ref[...]
        t, _ = _gelu_parts(x)
        o_ref[...] = (0.5 * x * (1.0 + t)).astype(BF16)

    spec = pl.BlockSpec((tm, W), lambda i: (i, 0))
    return pl.pallas_call(body, name=name, grid=(T // tm,), in_specs=[spec], out_specs=spec,
                          out_shape=jax.ShapeDtypeStruct((T, W), BF16),
                          compiler_params=_params(("parallel",)))(y)


def _gelu_bwd(dyg, y, tm, name):
    T, W = y.shape

    def body(d_ref, y_ref, o_ref):
        x = y_ref[...]
        t, dinner = _gelu_parts(x)
        o_ref[...] = d_ref[...] * (0.5 * (1.0 + t) + 0.5 * x * (1.0 - t * t) * dinner)

    spec = pl.BlockSpec((tm, W), lambda i: (i, 0))
    return pl.pallas_call(body, name=name, grid=(T // tm,), in_specs=[spec, spec], out_specs=spec,
                          out_shape=jax.ShapeDtypeStruct((T, W), F32),
                          compiler_params=_params(("parallel",)))(dyg, y)


def _merge_cols(D):
    cb = 512 if D % 512 == 0 else D
    return cb, D // cb


def _merge_fwd(gates, attn, ga, gb, tm, name):
    T, D = attn.shape
    cb, nc = _merge_cols(D)

    def body(gat_ref, gss_ref, attn_ref, ga_ref, gb_ref, o_ref):
        ssm = ga_ref[...] * _sigmoid(gb_ref[...])
        o_ref[...] = (_sigmoid(gat_ref[...]) * attn_ref[...] + _sigmoid(gss_ref[...]) * ssm).astype(BF16)

    def col(block):
        return pl.BlockSpec((tm, cb), lambda i, j: (i, block * nc + j))

    return pl.pallas_call(
        body, name=name, grid=(T // tm, nc),
        in_specs=[col(0), col(1), col(0), col(0), col(0)],
        out_specs=col(0), out_shape=jax.ShapeDtypeStruct((T, D), BF16),
        compiler_params=_params(("parallel", "parallel")),
    )(gates, gates, attn, ga, gb)


def _merge_bwd(dm, gates, attn, ga, gb, tm, name):
    T, D = attn.shape
    cb, nc = _merge_cols(D)

    def body(dm_ref, gat_ref, gss_ref, attn_ref, ga_ref, gb_ref, dattn_ref, dgat_ref, dgss_ref, dga_ref, dgb_ref):
        d = dm_ref[...]
        sa = _sigmoid(gat_ref[...])
        ss = _sigmoid(gss_ref[...])
        sb = _sigmoid(gb_ref[...])
        gav = ga_ref[...]
        dattn_ref[...] = d * sa
        dgat_ref[...] = (d * attn_ref[...] * (sa * (1.0 - sa))).astype(BF16)
        dgss_ref[...] = (d * (gav * sb) * (ss * (1.0 - ss))).astype(BF16)
        dssm = d * ss
        dga_ref[...] = (dssm * sb).astype(BF16)
        dgb_ref[...] = (dssm * gav * (sb * (1.0 - sb))).astype(BF16)

    def col(block):
        return pl.BlockSpec((tm, cb), lambda i, j: (i, block * nc + j))

    b16 = jax.ShapeDtypeStruct((T, D), BF16)
    return pl.pallas_call(
        body, name=name, grid=(T // tm, nc),
        in_specs=[col(0), col(0), col(1), col(0), col(0), col(0)],
        out_specs=[col(0)] * 5,
        out_shape=[jax.ShapeDtypeStruct((T, D), F32), b16, b16, b16, b16],
        compiler_params=_params(("parallel", "parallel")),
    )(dm, gates, gates, attn, ga, gb)


def _loss_head(h, g, target, tm, name):
    T, D = h.shape
    B, S, _ = target.shape
    L = S + N_META
    nt = T // tm
    tpe = L // tm

    def body(h_ref, g_ref, t_hbm, dh_ref, dhb_ref, dg_ref, loss_ref, tbuf, acc_g, acc_l, sem):
        i = pl.program_id(0)
        b, j = i // tpe, i % tpe

        @pl.when(j == 0)
        def _():
            tbuf[0:N_META, :] = jnp.zeros((N_META, D), F32)
            cp = pltpu.make_async_copy(t_hbm.at[b, pl.ds(0, tm - N_META), :], tbuf.at[pl.ds(N_META, tm - N_META), :], sem)
            cp.start()
            cp.wait()

        @pl.when(j > 0)
        def _():
            cp = pltpu.make_async_copy(t_hbm.at[b, pl.ds(j * tm - N_META, tm), :], tbuf, sem)
            cp.start()
            cp.wait()

        x = h_ref[...]
        gv = g_ref[...]
        r = lax.rsqrt(jnp.mean(x * x, axis=-1, keepdims=True) + NORM_EPS)
        xhat = x * r
        pos = j * tm + lax.broadcasted_iota(jnp.int32, (tm, 1), 0)
        err = jnp.where(pos >= N_META, xhat * gv - tbuf[...], 0.0)
        dy = err * (1.0 / D)
        dxhat = dy * gv
        dh = r * (dxhat - xhat * jnp.mean(dxhat * xhat, axis=-1, keepdims=True))
        dh_ref[...] = dh
        dhb_ref[...] = dh.astype(BF16)
        pg = _fold8(dy * xhat)
        pe = _fold8(err * err)

        @pl.when(i == 0)
        def _():
            acc_g[...] = pg
            acc_l[...] = pe

        @pl.when(i > 0)
        def _():
            acc_g[...] += pg
            acc_l[...] += pe

        @pl.when(i == nt - 1)
        def _():
            dg_ref[...] = jnp.sum(acc_g[...], axis=0, keepdims=True)
            loss_ref[...] = jnp.full((1, D), (0.5 / D) * jnp.sum(acc_l[...]), F32)

    row = pl.BlockSpec((tm, D), lambda i: (i, 0))
    vec = pl.BlockSpec((1, D), lambda i: (0, 0))
    return pl.pallas_call(
        body, name=name, grid=(nt,),
        in_specs=[row, vec, pl.BlockSpec(memory_space=pl.ANY)], out_specs=[row, row, vec, vec],
        out_shape=[jax.ShapeDtypeStruct((T, D), F32), jax.ShapeDtypeStruct((T, D), BF16),
                   jax.ShapeDtypeStruct((1, D), F32), jax.ShapeDtypeStruct((1, D), F32)],
        scratch_shapes=[pltpu.VMEM((tm, D), F32), pltpu.VMEM((8, D), F32), pltpu.VMEM((8, D), F32),
                        pltpu.SemaphoreType.DMA],
        compiler_params=_params(("arbitrary",)),
    )(h, g, target)


def _heads_to_rows(blk):
    return jnp.concatenate([blk[:, g * HEAD_DIM:(g + 1) * HEAD_DIM] for g in range(Q_PER_KV)], axis=0)


def _rows_to_heads(x):
    rows = x.shape[0] // Q_PER_KV
    return jnp.concatenate([x[g * rows:(g + 1) * rows] for g in range(Q_PER_KV)], axis=1)


def _causal(R):
    qi = lax.broadcasted_iota(jnp.int32, (R, BLOCK), 0) & (BLOCK - 1)
    kj = lax.broadcasted_iota(jnp.int32, (R, BLOCK), 1)
    return kj <= qi


def _band_probs(s_band, s_m, sink):
    m = jnp.maximum(jnp.maximum(jnp.max(s_band, axis=-1, keepdims=True), jnp.max(s_m, axis=-1, keepdims=True)), sink)
    e_b, e_m, e_s = jnp.exp(s_band - m), jnp.exp(s_m - m), jnp.exp(sink - m)
    inv = 1.0 / (jnp.sum(e_b, axis=-1, keepdims=True) + jnp.sum(e_m, axis=-1, keepdims=True) + e_s)
    return e_b * inv, e_m * inv, e_s * inv


def _fold_band(tri, two):
    return jnp.where(tri, two[:, BLOCK:2 * BLOCK], two[:, 0:BLOCK])


def _unfold_band(tri, band):
    return jnp.concatenate([jnp.where(tri, 0.0, band), jnp.where(tri, band, 0.0)], axis=1)


def _meta_probs(qm, k_m, sink_m):
    R = qm.shape[0]
    s = lax.dot_general(qm, k_m, _NT, preferred_element_type=F32)
    qi = lax.broadcasted_iota(jnp.int32, (R, N_META), 0) & (N_META - 1)
    kj = lax.broadcasted_iota(jnp.int32, (R, N_META), 1)
    s = jnp.where(kj <= qi, s, NEG_INF)
    m = jnp.maximum(jnp.max(s, axis=-1, keepdims=True), sink_m)
    e, e_s = jnp.exp(s - m), jnp.exp(sink_m - m)
    inv = 1.0 / (jnp.sum(e, axis=-1, keepdims=True) + e_s)
    return e * inv, e_s * inv


def _block_start(n):
    return pl.multiple_of(N_META + n * BLOCK, ROW_ALIGN)


def _kv(blk):
    return blk[:, 0:HEAD_DIM], blk[:, HEAD_DIM:2 * HEAD_DIM]


def _attn_fwd(q, kv, sink_col, sink_meta, B, name, riders=None):
    T, D = q.shape
    L = T // B
    KV = D // QW
    nb = (L - N_META) // BLOCK

    def body(q_ref, kv_ref, sk_ref, skm_ref, o_ref, kvs):
        kvs[...] = kv_ref[...].astype(BF16)
        k_m, v_m = _kv(kvs[0:N_META, :])
        p, _ = _meta_probs(_heads_to_rows(q_ref[0:N_META, :]), k_m, skm_ref[0])
        o_ref[0:N_META, :] = _rows_to_heads(jnp.dot(p.astype(BF16), v_m, preferred_element_type=F32))
        tri = _causal(BLOCK)

        def block(cur, first, keys):
            k2, v2 = _kv(kvs[keys, :])
            qb = _heads_to_rows(q_ref[pl.ds(cur, BLOCK), :])
            s2 = lax.dot_general(qb, k2, _NT, preferred_element_type=F32)
            sm = lax.dot_general(qb, k_m, _NT, preferred_element_type=F32)
            p2s, pms = [], []
            for g in range(Q_PER_KV):
                sl = slice(g * BLOCK, (g + 1) * BLOCK)
                s_band = jnp.where(tri, s2[sl], NEG_INF) if first else _fold_band(tri, s2[sl])
                p_b, p_m, _ = _band_probs(s_band, sm[sl], sk_ref[0, sl, :])
                p2s.append((p_b if first else _unfold_band(tri, p_b)).astype(BF16))
                pms.append(p_m.astype(BF16))
            o = (jnp.dot(jnp.concatenate(p2s, axis=0), v2, preferred_element_type=F32)
                 + jnp.dot(jnp.concatenate(pms, axis=0), v_m, preferred_element_type=F32))
            o_ref[pl.ds(cur, BLOCK), :] = _rows_to_heads(o)

        block(N_META, True, pl.ds(N_META, BLOCK))

        def step(n, carry):
            block(_block_start(n), False, pl.ds(_block_start(n - 1), 2 * BLOCK))
            return carry

        lax.fori_loop(1, nb, step, 0)

    q_spec = pl.BlockSpec((L, QW), lambda b, h: (b, h))
    return _call(body, name, (B, KV),
                 [q_spec, pl.BlockSpec((L, 2 * HEAD_DIM), lambda b, h: (b, h)),
                  pl.BlockSpec((1, Q_PER_KV * BLOCK, 1), lambda b, h: (h, 0, 0)),
                  pl.BlockSpec((1, Q_PER_KV * N_META, 1), lambda b, h: (h, 0, 0))],
                 [q_spec], [jax.ShapeDtypeStruct((T, D), F32)], [pltpu.VMEM((L, 2 * HEAD_DIM), BF16)],
                 ("parallel", "parallel"), (q, kv, sink_col, sink_meta), riders)


def _attn_bwd(q, kv, o, do, sink_col, sink_meta, B, name, riders=None):
    T, D = q.shape
    L = T // B
    KV = D // QW
    nb = (L - N_META) // BLOCK
    R = Q_PER_KV * BLOCK
    scale = HEAD_DIM ** -0.5

    def head_totals(col, rows_per_head):
        rid = lax.broadcasted_iota(jnp.int32, (8, 128), 0)
        out = jnp.zeros((8, 128), F32)
        for g in range(Q_PER_KV):
            out = out + jnp.where(rid == g, jnp.sum(col[g * rows_per_head:(g + 1) * rows_per_head, :]), 0.0)
        return out

    def body(q_ref, kv_ref, o_ref, do_ref, sk_ref, skm_ref, dq_ref, dkv_ref, dsk_ref, kvs, acc, acc_sink):
        b = pl.program_id(1)
        kvs[...] = kv_ref[...].astype(BF16)
        acc[...] = jnp.zeros_like(acc)
        k_m, v_m = _kv(kvs[0:N_META, :])

        qm = _heads_to_rows(q_ref[0:N_META, :])
        dom = _heads_to_rows(do_ref[0:N_META, :])
        delta = jnp.sum(dom * _heads_to_rows(o_ref[0:N_META, :]), axis=-1, keepdims=True)
        p, p_s = _meta_probs(qm, k_m, skm_ref[0])
        domb = dom.astype(BF16)
        ds = (p * (lax.dot_general(domb, v_m, _NT, preferred_element_type=F32) - delta)).astype(BF16)
        dq_ref[0:N_META, :] = _rows_to_heads(jnp.dot(ds, k_m, preferred_element_type=F32) * scale).astype(BF16)
        acc[0:N_META, :] += jnp.concatenate([lax.dot_general(ds, qm, _TN, preferred_element_type=F32),
                                             lax.dot_general(p.astype(BF16), domb, _TN, preferred_element_type=F32)], axis=1)
        sink_tot = head_totals(-p_s * delta, N_META)
        tri = _causal(BLOCK)
        acc_sink[...] = jnp.zeros_like(acc_sink)

        def block(cur, first, keys):
            k2, v2 = _kv(kvs[keys, :])
            rows = pl.ds(cur, BLOCK)
            qb = _heads_to_rows(q_ref[rows, :])
            dob = _heads_to_rows(do_ref[rows, :])
            delta = jnp.sum(dob * _heads_to_rows(o_ref[rows, :]), axis=-1, keepdims=True)
            dobb = dob.astype(BF16)
            s2 = lax.dot_general(qb, k2, _NT, preferred_element_type=F32)
            sm = lax.dot_general(qb, k_m, _NT, preferred_element_type=F32)
            dp2 = lax.dot_general(dobb, v2, _NT, preferred_element_type=F32)
            dpm = lax.dot_general(dobb, v_m, _NT, preferred_element_type=F32)
            ds2s, p2s, dsms, pms = [], [], [], []
            for g in range(Q_PER_KV):
                sl = slice(g * BLOCK, (g + 1) * BLOCK)
                s_band = jnp.where(tri, s2[sl], NEG_INF) if first else _fold_band(tri, s2[sl])
                p_b, p_m, p_s = _band_probs(s_band, sm[sl], sk_ref[0, sl, :])
                ds_b = p_b * ((dp2[sl] if first else _fold_band(tri, dp2[sl])) - delta[sl])
                ds2s.append((ds_b if first else _unfold_band(tri, ds_b)).astype(BF16))
                p2s.append((p_b if first else _unfold_band(tri, p_b)).astype(BF16))
                dsms.append((p_m * (dpm[sl] - delta[sl])).astype(BF16))
                pms.append(p_m.astype(BF16))
                acc_sink[sl, :] += -p_s * delta[sl]
            ds2, p2 = jnp.concatenate(ds2s, axis=0), jnp.concatenate(p2s, axis=0)
            dsm, pm = jnp.concatenate(dsms, axis=0), jnp.concatenate(pms, axis=0)
            dq = jnp.dot(ds2, k2, preferred_element_type=F32) + jnp.dot(dsm, k_m, preferred_element_type=F32)
            dq_ref[rows, :] = _rows_to_heads(dq * scale).astype(BF16)
            acc[keys, :] += jnp.concatenate([lax.dot_general(ds2, qb, _TN, preferred_element_type=F32),
                                             lax.dot_general(p2, dobb, _TN, preferred_element_type=F32)], axis=1)
            acc[0:N_META, :] += jnp.concatenate([lax.dot_general(dsm, qb, _TN, preferred_element_type=F32),
                                                 lax.dot_general(pm, dobb, _TN, preferred_element_type=F32)], axis=1)

        block(N_META, True, pl.ds(N_META, BLOCK))

        def step(n, carry):
            block(_block_start(n), False, pl.ds(_block_start(n - 1), 2 * BLOCK))
            return carry

        lax.fori_loop(1, nb, step, 0)
        dkv_ref[...] = acc[...].astype(BF16)
        tot = sink_tot + head_totals(acc_sink[...], BLOCK)

        @pl.when(b == 0)
        def _():
            dsk_ref[0] = tot

        @pl.when(b > 0)
        def _():
            dsk_ref[0] += tot

    q_spec = pl.BlockSpec((L, QW), lambda h, b: (b, h))
    kv_spec = pl.BlockSpec((L, 2 * HEAD_DIM), lambda h, b: (b, h))
    return _call(body, name, (KV, B),
                 [q_spec, kv_spec, q_spec, q_spec,
                  pl.BlockSpec((1, R, 1), lambda h, b: (h, 0, 0)),
                  pl.BlockSpec((1, Q_PER_KV * N_META, 1), lambda h, b: (h, 0, 0))],
                 [q_spec, kv_spec, pl.BlockSpec((1, 8, 128), lambda h, b: (h, 0, 0))],
                 [jax.ShapeDtypeStruct((T, D), BF16), jax.ShapeDtypeStruct((T, KV * 2 * HEAD_DIM), BF16),
                  jax.ShapeDtypeStruct((KV, 8, 128), F32)],
                 [pltpu.VMEM((L, 2 * HEAD_DIM), BF16), pltpu.VMEM((L, 2 * HEAD_DIM), F32), pltpu.VMEM((R, 1), F32)],
                 ("parallel", "arbitrary"), (q, kv, o, do, sink_col, sink_meta), riders)


def _cmul_add(acc_r, acc_i, lr, li, xr, xi):
    return acc_r + (lr * xr - li * xi), acc_i + (lr * xi + li * xr)


def _cols_per_step(ncol):
    return 2 if ncol % 2 == 0 else 1


def _ssm_fwd(u, bmat, cmat, dskip, tables, nbatch, rc, name):
    T, W = u.shape
    ncol = W // SSM_LANES
    nch = T // rc
    S = STATE_LANES
    cps = _cols_per_step(ncol)
    assert nbatch == 4

    def body(u_ref, b_ref, c_ref, d_ref, tab_ref, y_ref, xs_ref, xp_ref, st_ref, carry_ref):
        ch = pl.program_id(1)

        @pl.when(ch == 0)
        def _():
            carry_ref[...] = jnp.zeros_like(carry_ref)

        uv = u_ref[...]
        for k in range(cps):
            st_ref[:, 2 * S * k:2 * S * (k + 1)] = jnp.dot(uv[:, SSM_LANES * k:SSM_LANES * (k + 1)].astype(BF16), b_ref[k],
                                                           preferred_element_type=F32)
        low = lax.broadcasted_iota(jnp.int32, (8, S), 0) < nbatch

        def tile(k, r0, c_r, c_i):
            re, im = slice(2 * S * k, 2 * S * k + S), slice(2 * S * k + S, 2 * S * (k + 1))
            la_r, la_i = tab_ref[k, :, 0:S], tab_ref[k, :, S:2 * S]
            lb_r, lb_i = tab_ref[k, :, 2 * S:3 * S], tab_ref[k, :, 3 * S:4 * S]
            v_r = st_ref[pl.ds(r0, 8), re]
            v_i = st_ref[pl.ds(r0, 8), im]
            v_r, v_i = _cmul_add(v_r, v_i, la_r, la_i, pltpu.roll(v_r, nbatch, 0), pltpu.roll(v_i, nbatch, 0))
            rc_r, rc_i = pltpu.roll(c_r, nbatch, 0), pltpu.roll(c_i, nbatch, 0)
            cb_r, cb_i = jnp.where(low, rc_r, c_r), jnp.where(low, rc_i, c_i)
            v_r, v_i = _cmul_add(v_r, v_i, lb_r, lb_i, cb_r, cb_i)
            st_ref[pl.ds(r0, 8), re] = v_r
            st_ref[pl.ds(r0, 8), im] = v_i
            p_r = jnp.where(low, rc_r, pltpu.roll(v_r, nbatch, 0))
            p_i = jnp.where(low, rc_i, pltpu.roll(v_i, nbatch, 0))
            return v_r, v_i, p_r, p_i

        def step(i, carry):
            r0 = pl.multiple_of(i * 16, 16)
            out = []
            for k in range(cps):
                re, im = slice(2 * S * k, 2 * S * k + S), slice(2 * S * k + S, 2 * S * (k + 1))
                a_r, a_i, pa_r, pa_i = tile(k, r0, carry[2 * k], carry[2 * k + 1])
                b_r, b_i, pb_r, pb_i = tile(k, r0 + 8, a_r, a_i)
                xp_ref[pl.ds(r0, 16), re] = jnp.concatenate([pa_r, pb_r], axis=0).astype(BF16)
                xp_ref[pl.ds(r0, 16), im] = jnp.concatenate([pa_i, pb_i], axis=0).astype(BF16)
                out += [b_r, b_i]
            return tuple(out)

        halves = tuple(carry_ref[:, S * j:S * (j + 1)] for j in range(2 * cps))
        halves = lax.fori_loop(0, rc // 16, step, halves)
        for j in range(2 * cps):
            carry_ref[:, S * j:S * (j + 1)] = halves[j]
        xb = st_ref[...].astype(BF16)
        xs_ref[...] = xb
        for k in range(cps):
            cols = slice(SSM_LANES * k, SSM_LANES * (k + 1))
            y_ref[:, cols] = (jnp.dot(xb[:, 2 * S * k:2 * S * (k + 1)], c_ref[k], preferred_element_type=F32)
                              + d_ref[:, cols] * uv[:, cols])

    return pl.pallas_call(
        body, name=name, grid=(ncol // cps, nch),
        in_specs=[pl.BlockSpec((rc, cps * SSM_LANES), lambda g, c: (c, g)),
                  pl.BlockSpec((cps, SSM_LANES, 2 * S), lambda g, c: (g, 0, 0)),
                  pl.BlockSpec((cps, 2 * S, SSM_LANES), lambda g, c: (g, 0, 0)),
                  pl.BlockSpec((1, cps * SSM_LANES), lambda g, c: (0, g)),
                  pl.BlockSpec((cps, 8, 4 * S), lambda g, c: (g, 0, 0))],
        out_specs=[pl.BlockSpec((rc, cps * SSM_LANES), lambda g, c: (c, g)),
                   pl.BlockSpec((rc, cps * 2 * S), lambda g, c: (c, g)),
                   pl.BlockSpec((rc, cps * 2 * S), lambda g, c: (c, g))],
        out_shape=[jax.ShapeDtypeStruct((T, W), F32), jax.ShapeDtypeStruct((T, ncol * 2 * S), BF16),
                   jax.ShapeDtypeStruct((T, ncol * 2 * S), BF16)],
        scratch_shapes=[pltpu.VMEM((rc, cps * 2 * S), F32), pltpu.VMEM((8, cps * 2 * S), F32)],
        compiler_params=_params(("parallel", "arbitrary")),
    )(u, bmat, cmat, dskip, tables)


def _ssm_bwd(dy, u, xs, xp, bmat, cmat, dskip, tables, nbatch, rc, name):
    T, W = u.shape
    ncol = W // SSM_LANES
    nch = T // rc
    S = STATE_LANES
    ntile = rc // 16
    cps = _cols_per_step(ncol)

    def body(dy_ref, u_ref, xs_ref, xp_ref, b_ref, c_ref, d_ref, tab_ref,
             du_ref, db_ref, dc_ref, dl_ref, dd_ref, st_ref, carry_ref, accl_ref, accd_ref):
        ch = pl.program_id(1)

        @pl.when(ch == 0)
        def _():
            carry_ref[...] = jnp.zeros_like(carry_ref)
            accl_ref[...] = jnp.zeros_like(accl_ref)
            accd_ref[...] = jnp.zeros_like(accd_ref)
            db_ref[...] = jnp.zeros_like(db_ref)
            dc_ref[...] = jnp.zeros_like(dc_ref)

        dyv = dy_ref[...]
        uv = u_ref[...]
        dyb = dyv.astype(BF16)
        for k in range(cps):
            st_ref[:, 2 * S * k:2 * S * (k + 1)] = lax.dot_general(dyb[:, SSM_LANES * k:SSM_LANES * (k + 1)], c_ref[k], _NT,
                                                                   preferred_element_type=F32)
        low = lax.broadcasted_iota(jnp.int32, (8, S), 0) < nbatch

        def tile(k, r0, p_r, p_i, c_r, c_i, al_r, al_i):
            re, im = slice(2 * S * k, 2 * S * k + S), slice(2 * S * k + S, 2 * S * (k + 1))
            la_r, la_i = tab_ref[k, :, 0:S], tab_ref[k, :, S:2 * S]
            lb_r, lb_i = tab_ref[k, :, 2 * S:3 * S], tab_ref[k, :, 3 * S:4 * S]
            v_r = st_ref[pl.ds(r0, 8), re]
            v_i = st_ref[pl.ds(r0, 8), im]
            v_r, v_i = _cmul_add(v_r, v_i, la_r, la_i, pltpu.roll(v_r, nbatch, 0), pltpu.roll(v_i, nbatch, 0))
            cb_r = jnp.where(low, c_r, pltpu.roll(c_r, nbatch, 0))
            cb_i = jnp.where(low, c_i, pltpu.roll(c_i, nbatch, 0))
            v_r, v_i = _cmul_add(v_r, v_i, lb_r, lb_i, cb_r, cb_i)
            st_ref[pl.ds(r0, 8), re] = v_r
            st_ref[pl.ds(r0, 8), im] = v_i
            al_r = al_r + (v_r * p_r + v_i * p_i)
            al_i = al_i + (v_i * p_r - v_r * p_i)
            return v_r, v_i, al_r, al_i

        def step(j, carry):
            r0 = pl.multiple_of((ntile - 1 - j) * 16, 16)
            out = []
            for k in range(cps):
                re, im = slice(2 * S * k, 2 * S * k + S), slice(2 * S * k + S, 2 * S * (k + 1))
                p_r = xp_ref[pl.ds(r0, 16), re].astype(F32)
                p_i = xp_ref[pl.ds(r0, 16), im].astype(F32)
                mid = tile(k, r0 + 8, p_r[8:16], p_i[8:16], *carry[4 * k:4 * k + 4])
                out += list(tile(k, r0, p_r[0:8], p_i[0:8], *mid))
            return tuple(out)

        init = []
        for k in range(cps):
            init += [carry_ref[:, 2 * S * k:2 * S * k + S], carry_ref[:, 2 * S * k + S:2 * S * (k + 1)],
                     accl_ref[:, 2 * S * k:2 * S * k + S], accl_ref[:, 2 * S * k + S:2 * S * (k + 1)]]
        fin = lax.fori_loop(0, ntile, step, tuple(init))
        for k in range(cps):
            carry_ref[:, 2 * S * k:2 * S * k + S] = fin[4 * k]
            carry_ref[:, 2 * S * k + S:2 * S * (k + 1)] = fin[4 * k + 1]
            accl_ref[:, 2 * S * k:2 * S * k + S] = fin[4 * k + 2]
            accl_ref[:, 2 * S * k + S:2 * S * (k + 1)] = fin[4 * k + 3]
        dsb = st_ref[...].astype(BF16)
        ub = uv.astype(BF16)
        for k in range(cps):
            cols, lanes = slice(SSM_LANES * k, SSM_LANES * (k + 1)), slice(2 * S * k, 2 * S * (k + 1))
            du_ref[:, cols] = (lax.dot_general(dsb[:, lanes], b_ref[k], _NT, preferred_element_type=F32)
                               + d_ref[:, cols] * dyv[:, cols])
            db_ref[k] += lax.dot_general(ub[:, cols], dsb[:, lanes], _TN, preferred_element_type=F32)
            dc_ref[k] += lax.dot_general(xs_ref[:, lanes], dyb[:, cols], _TN, preferred_element_type=F32)
        accd_ref[...] += _fold8(dyv * uv)

        @pl.when(ch == nch - 1)
        def _():
            for k in range(cps):
                dl_ref[k] = jnp.sum(accl_ref[:, 2 * S * k:2 * S * (k + 1)], axis=0, keepdims=True)
            dd_ref[...] = jnp.sum(accd_ref[...], axis=0, keepdims=True)

    rev = lambda g, c: (nch - 1 - c, g)
    return pl.pallas_call(
        body, name=name, grid=(ncol // cps, nch),
        in_specs=[pl.BlockSpec((rc, cps * SSM_LANES), rev), pl.BlockSpec((rc, cps * SSM_LANES), rev),
                  pl.BlockSpec((rc, cps * 2 * S), rev), pl.BlockSpec((rc, cps * 2 * S), rev),
                  pl.BlockSpec((cps, SSM_LANES, 2 * S), lambda g, c: (g, 0, 0)),
                  pl.BlockSpec((cps, 2 * S, SSM_LANES), lambda g, c: (g, 0, 0)),
                  pl.BlockSpec((1, cps * SSM_LANES), lambda g, c: (0, g)),
                  pl.BlockSpec((cps, 8, 4 * S), lambda g, c: (g, 0, 0))],
        out_specs=[pl.BlockSpec((rc, cps * SSM_LANES), rev),
                   pl.BlockSpec((cps, SSM_LANES, 2 * S), lambda g, c: (g, 0, 0)),
                   pl.BlockSpec((cps, 2 * S, SSM_LANES), lambda g, c: (g, 0, 0)),
                   pl.BlockSpec((cps, 1, 2 * S), lambda g, c: (g, 0, 0)),
                   pl.BlockSpec((1, cps * SSM_LANES), lambda g, c: (0, g))],
        out_shape=[jax.ShapeDtypeStruct((T, W), F32),
                   jax.ShapeDtypeStruct((ncol, SSM_LANES, 2 * S), F32),
                   jax.ShapeDtypeStruct((ncol, 2 * S, SSM_LANES), F32),
                   jax.ShapeDtypeStruct((ncol, 1, 2 * S), F32),
                   jax.ShapeDtypeStruct((1, W), F32)],
        scratch_shapes=[pltpu.VMEM((rc, cps * 2 * S), F32), pltpu.VMEM((8, cps * 2 * S), F32),
                        pltpu.VMEM((8, cps * 2 * S), F32), pltpu.VMEM((8, cps * SSM_LANES), F32)],
        compiler_params=_params(("parallel", "arbitrary")),
    )(dy, u, xs, xp, bmat, cmat, dskip, tables)


def _ssm_matrices(a_re, a_im, log_step, b_re, b_im, c_re, c_im):
    G, N = a_re.shape
    ncol = G // GROUPS_PER_COL
    step = jnp.exp(log_step)[:, None]
    mag = jnp.exp(a_re * step)
    ang = a_im * step
    lam_re, lam_im = mag * jnp.cos(ang), mag * jnp.sin(ang)
    den = a_re * a_re + a_im * a_im
    nr, ni = lam_re - 1.0, lam_im
    coef_re = (nr * a_re + ni * a_im) / den
    coef_im = (ni * a_re - nr * a_im) / den
    bb_re = coef_re[..., None] * b_re - coef_im[..., None] * b_im
    bb_im = coef_re[..., None] * b_im + coef_im[..., None] * b_re
    eye = jnp.eye(GROUPS_PER_COL, dtype=F32)
    bb = jnp.stack([bb_re, bb_im]).reshape(2, ncol, GROUPS_PER_COL, N, SSM_GROUP)
    bmat = jnp.einsum("pbgnc,gh->bgcphn", bb, eye).reshape(ncol, SSM_LANES, 2 * STATE_LANES)
    cc = jnp.stack([c_re, -c_im]).reshape(2, ncol, GROUPS_PER_COL, SSM_GROUP, N)
    cmat = jnp.einsum("pbgcn,gh->bpgnhc", cc, eye).reshape(ncol, 2 * STATE_LANES, SSM_LANES)
    lam = jnp.concatenate([lam_re.reshape(ncol, STATE_LANES), lam_im.reshape(ncol, STATE_LANES)], axis=-1)
    return lam, bmat, cmat


def _scan_tables(lam, nbatch, conj):
    S = STATE_LANES
    lr, li = lam[:, None, 0:S], lam[:, None, S:2 * S]
    if conj:
        li = -li
    l2r, l2i = lr * lr - li * li, 2.0 * lr * li
    first = (jnp.arange(8) < nbatch)[None, :, None]
    zero = jnp.zeros_like(lr)
    if conj:
        parts = [jnp.where(first, lr, zero), jnp.where(first, li, zero), jnp.where(first, l2r, lr), jnp.where(first, l2i, li)]
    else:
        parts = [jnp.where(first, zero, lr), jnp.where(first, zero, li), jnp.where(first, lr, l2r), jnp.where(first, li, l2i)]
    return jnp.concatenate([jnp.broadcast_to(p, (lam.shape[0], 8, S)) for p in parts], axis=-1)


def _adamw_update(w_ref, g_ref, m_ref, v_ref, d_ref, nm_ref, nv_ref):
    gv = g_ref[...]
    mn = ADAM_B1 * m_ref[...] + (1.0 - ADAM_B1) * gv
    vn = ADAM_B2 * v_ref[...] + (1.0 - ADAM_B2) * (gv * gv)
    m_hat = mn / (1.0 - ADAM_B1 ** ADAM_STEP)
    v_hat = vn / (1.0 - ADAM_B2 ** ADAM_STEP)
    d_ref[...] = -ADAM_LR * (m_hat / (jnp.sqrt(v_hat) + ADAM_EPS) + ADAM_WD * w_ref[...])
    nm_ref[...] = mn
    nv_ref[...] = vn


def _adamw_small(ws, gs, ms, vs, name):
    n = len(ws)

    def body(*refs):
        for i in range(n):
            _adamw_update(refs[i], refs[n + i], refs[2 * n + i], refs[3 * n + i],
                          refs[4 * n + i], refs[5 * n + i], refs[6 * n + i])

    vm = pl.BlockSpec(memory_space=pltpu.VMEM)
    shapes = [jax.ShapeDtypeStruct(a.shape, F32) for a in ws]
    outs = pl.pallas_call(body, name=name, in_specs=[vm] * (4 * n), out_specs=[vm] * (3 * n), out_shape=shapes * 3,
                          compiler_params=pltpu.CompilerParams(vmem_limit_bytes=VMEM_LIMIT))(*ws, *gs, *ms, *vs)
    return outs[:n], outs[n:2 * n], outs[2 * n:]


def _adamw(w, g, m, v, name):
    R, C = w.shape
    tr = R if R <= 512 else _pick_tile(R, 512, 8)
    body = functools.partial(_adamw_update)

    spec = pl.BlockSpec((tr, C), lambda i: (i, 0))
    shp = jax.ShapeDtypeStruct((R, C), F32)
    return pl.pallas_call(body, name=name, grid=(R // tr,), in_specs=[spec] * 4, out_specs=[spec] * 3,
                          out_shape=[shp, shp, shp], compiler_params=_params(("parallel",)))(w, g, m, v)


_ANY = pl.BlockSpec(memory_space=pl.ANY)


def _place():
    x, y, c = lax.axis_index("x"), lax.axis_index("y"), lax.axis_index("c")
    chips = [(1 - x, y), (x, 1 - y), (1 - x, 1 - y)]
    return x, y, c, chips


def _remote(src, dst, send_sems, recv_sems, k, to):
    return pltpu.make_async_remote_copy(src_ref=src, dst_ref=dst, send_sem=send_sems.at[k], recv_sem=recv_sems.at[k],
                                        device_id=to, device_id_type=MESH_IDS)


class _Riders:
    def __init__(self, srcs, out_shapes, n_sems, copies):
        self.srcs, self.out_shapes, self.n_sems, self.copies = list(srcs), list(out_shapes), n_sems, copies


def _call(body, name, grid, in_specs, out_specs, out_shape, scratch_shapes, sem, args, riders=None):
    if riders is None:
        return pl.pallas_call(body, name=name, grid=grid, in_specs=in_specs, out_specs=out_specs, out_shape=out_shape,
                              scratch_shapes=scratch_shapes, compiler_params=_params(sem))(*args)
    n_in, n_out, n_scr = len(in_specs), len(out_specs), len(scratch_shapes)
    r_in, r_out = len(riders.srcs), len(riders.out_shapes)

    def carrying(*refs):
        a, b = n_in, n_in + r_in
        c, d = b + n_out, b + n_out + r_out
        e = d + n_scr
        sends, arrivals = riders.copies(refs[a:b], refs[c:d], refs[e], refs[e + 1])
        first, last = None, None
        for ax, size in enumerate(grid):
            at0, at1 = pl.program_id(ax) == 0, pl.program_id(ax) == size - 1
            first = at0 if first is None else first & at0
            last = at1 if last is None else last & at1

        @pl.when(first)
        def _():
            for cp in sends:
                cp.start()

        body(*refs[:a], *refs[b:c], *refs[d:e])

        @pl.when(last)
        def _():
            for cp in arrivals:
                cp.wait_recv()
            for cp in sends:
                cp.wait_send()

    outs = pl.pallas_call(
        carrying, name=name, grid=grid, in_specs=list(in_specs) + [_ANY] * r_in,
        out_specs=list(out_specs) + [_ANY] * r_out, out_shape=list(out_shape) + riders.out_shapes,
        scratch_shapes=list(scratch_shapes) + [pltpu.SemaphoreType.DMA((riders.n_sems,)),
                                               pltpu.SemaphoreType.DMA((riders.n_sems,))],
        compiler_params=pltpu.CompilerParams(dimension_semantics=("arbitrary",) * len(grid),
                                             vmem_limit_bytes=VMEM_LIMIT, has_side_effects=True),
    )(*args, *riders.srcs)
    return outs[:n_out], outs[n_out:]


def _gather_riders(shards):
    def copies(srcs, outs, send_sems, recv_sems):
        x, y, c, chips = _place()
        sends, arrivals = [], []
        for i, s in enumerate(shards):
            half = s.shape[0] // 2
            rows = pl.ds(c * half, half)
            for j, chip in enumerate(chips):
                sends.append(_remote(srcs[i].at[rows, :], outs[i].at[2 * x + y, rows, :], send_sems, recv_sems,
                                     3 * i + j, (*chip, c)))
                landed = outs[i].at[2 * chip[0] + chip[1], rows, :]
                arrivals.append(_remote(landed, landed, send_sems, recv_sems, 3 * i + j, (*chip, c)))
        return sends, arrivals

    return _Riders(shards, [jax.ShapeDtypeStruct((N_CHIPS,) + s.shape, s.dtype) for s in shards], 3 * len(shards), copies)


def _exchange_riders(parts):
    def copies(srcs, outs, send_sems, recv_sems):
        x, y, c, chips = _place()
        sends = [_remote(srcs[i].at[2 * chip[0] + chip[1]], outs[i].at[j], send_sems, recv_sems, 3 * i + j, (*chip, c))
                 for i in range(len(parts)) for j, chip in enumerate(chips)]
        return sends, sends

    return _Riders(parts, [jax.ShapeDtypeStruct((3,) + p.shape[1:], p.dtype) for p in parts], 3 * len(parts), copies)


def _forward_halves(gathered, shards, tag):
    n = len(gathered)

    def body(*refs):
        srcs, outs = refs[:n], refs[n:2 * n]
        send_sems, recv_sems = refs[2 * n:]
        x, y, c, chips = _place()
        sibling = (x, y, 1 - c)
        cps = []
        for i in range(n):
            half = gathered[i].shape[1] // 2
            for j, chip in enumerate(chips):
                slot = 2 * chip[0] + chip[1]
                cps.append(_remote(srcs[i].at[slot, pl.ds(c * half, half), :], outs[i].at[slot, pl.ds(c * half, half), :],
                                   send_sems, recv_sems, 3 * i + j, sibling))
        for cp in cps:
            cp.start()
        for i in range(n):
            half = gathered[i].shape[1] // 2
            for j, chip in enumerate(chips):
                theirs = outs[i].at[2 * chip[0] + chip[1], pl.ds((1 - c) * half, half), :]
                _remote(theirs, theirs, send_sems, recv_sems, 3 * i + j, sibling).wait_recv()
        for cp in cps:
            cp.wait_send()

    outs = pl.pallas_call(
        body, name=f"gather_forward_{tag}", in_specs=[_ANY] * n, out_specs=[_ANY] * n,
        out_shape=[jax.ShapeDtypeStruct(g.shape, g.dtype) for g in gathered],
        input_output_aliases={i: i for i in range(n)},
        scratch_shapes=[pltpu.SemaphoreType.DMA((3 * n,)), pltpu.SemaphoreType.DMA((3 * n,))],
        compiler_params=pltpu.CompilerParams(has_side_effects=True),
    )(*gathered)
    slot = 2 * lax.axis_index("x") + lax.axis_index("y")
    return [lax.dynamic_update_slice(o, s[None], (slot, 0, 0)) for o, s in zip(outs, shards)]


def _gather_weights(shards):
    n = len(shards)

    def body(*refs):
        srcs, outs = refs[:n], refs[n:2 * n]
        send_sems, recv_sems = refs[2 * n:]
        x, y, c, chips = _place()
        sibling = (x, y, 1 - c)

        def piece(i, px, py, pc):
            half = shards[i].shape[0] // 2
            return outs[i].at[2 * px + py, pl.ds(pc * half, half), :]

        first = []
        for i in range(n):
            half = shards[i].shape[0] // 2
            for j, chip in enumerate(chips):
                first.append(_remote(srcs[i].at[pl.ds(c * half, half), :], piece(i, x, y, c), send_sems, recv_sems,
                                     6 * i + j, (*chip, c)))
        for cp in first:
            cp.start()
        passed = []
        for i in range(n):
            for j, chip in enumerate(chips):
                _remote(piece(i, *chip, c), piece(i, *chip, c), send_sems, recv_sems, 6 * i + j, (*chip, c)).wait_recv()
                cp = _remote(piece(i, *chip, c), piece(i, *chip, c), send_sems, recv_sems, 6 * i + 3 + j, sibling)
                cp.start()
                passed.append(cp)
        for i in range(n):
            for j, chip in enumerate(chips):
                _remote(piece(i, *chip, 1 - c), piece(i, *chip, 1 - c), send_sems, recv_sems, 6 * i + 3 + j,
                        sibling).wait_recv()
        for cp in first + passed:
            cp.wait_send()

    outs = pl.pallas_call(
        body, name="gather_weights", in_specs=[_ANY] * n, out_specs=[_ANY] * n,
        out_shape=[jax.ShapeDtypeStruct((N_CHIPS,) + s.shape, s.dtype) for s in shards],
        scratch_shapes=[pltpu.SemaphoreType.DMA((6 * n,)), pltpu.SemaphoreType.DMA((6 * n,))],
        compiler_params=pltpu.CompilerParams(has_side_effects=True),
    )(*shards)
    slot = 2 * lax.axis_index("x") + lax.axis_index("y")
    return [lax.dynamic_update_slice(o, s[None], (slot, 0, 0)) for o, s in zip(outs, shards)]


def _swap_halves(grads, tag):
    n = len(grads)

    def body(*refs):
        srcs, outs = refs[:n], refs[n:2 * n]
        send_sems, recv_sems = refs[2 * n:]
        x, y, c, _ = _place()
        cps = []
        for i in range(n):
            half = grads[i].shape[1] // 2
            cps.append(_remote(srcs[i].at[:, pl.ds((1 - c) * half, half), :], outs[i], send_sems, recv_sems, i, (x, y, 1 - c)))
        for cp in cps:
            cp.start()
        for cp in cps:
            cp.wait()

    return pl.pallas_call(
        body, name=f"grad_swap_halves_{tag}", in_specs=[_ANY] * n, out_specs=[_ANY] * n,
        out_shape=[jax.ShapeDtypeStruct((N_CHIPS, g.shape[1] // 2, g.shape[2]), g.dtype) for g in grads],
        scratch_shapes=[pltpu.SemaphoreType.DMA((n,)), pltpu.SemaphoreType.DMA((n,))],
        compiler_params=pltpu.CompilerParams(has_side_effects=True),
    )(*grads)


def _exchange_chips(parts):
    n = len(parts)

    def body(*refs):
        srcs, outs = refs[:n], refs[n:2 * n]
        send_sems, recv_sems = refs[2 * n:]
        x, y, c, chips = _place()
        cps = [_remote(srcs[i].at[2 * chip[0] + chip[1]], outs[i].at[j], send_sems, recv_sems, 3 * i + j, (*chip, c))
               for i in range(n) for j, chip in enumerate(chips)]
        for cp in cps:
            cp.start()
        for cp in cps:
            cp.wait()

    return pl.pallas_call(
        body, name="grad_exchange_chips", in_specs=[_ANY] * n, out_specs=[_ANY] * n,
        out_shape=[jax.ShapeDtypeStruct((3,) + p.shape[1:], p.dtype) for p in parts],
        scratch_shapes=[pltpu.SemaphoreType.DMA((3 * n,)), pltpu.SemaphoreType.DMA((3 * n,))],
        compiler_params=pltpu.CompilerParams(has_side_effects=True),
    )(*parts)


def _join_halves(fulls):
    n = len(fulls)

    def body(*refs):
        srcs, outs = refs[:n], refs[n:2 * n]
        send_sems, recv_sems = refs[2 * n:]
        x, y, c, _ = _place()
        sibling = (x, y, 1 - c)
        cps = []
        for i in range(n):
            h = fulls[i].shape[0] // 2
            cps.append(_remote(srcs[i].at[pl.ds(c * h, h), :], outs[i].at[pl.ds(c * h, h), :], send_sems, recv_sems, i,
                               sibling))
        for cp in cps:
            cp.start()
        for i in range(n):
            h = fulls[i].shape[0] // 2
            theirs = outs[i].at[pl.ds((1 - c) * h, h), :]
            _remote(theirs, theirs, send_sems, recv_sems, i, sibling).wait_recv()
        for cp in cps:
            cp.wait_send()

    return pl.pallas_call(
        body, name="grad_join_halves", in_specs=[_ANY] * n, out_specs=[_ANY] * n,
        out_shape=[jax.ShapeDtypeStruct(f.shape, f.dtype) for f in fulls],
        input_output_aliases={i: i for i in range(n)},
        scratch_shapes=[pltpu.SemaphoreType.DMA((n,)), pltpu.SemaphoreType.DMA((n,))],
        compiler_params=pltpu.CompilerParams(has_side_effects=True),
    )(*fulls)


def _half_tile(h):
    return h if h <= 512 else _pick_tile(h, 512, ROW_ALIGN)


def _sum_halves(g, r1, c_idx, name):
    _, R, C = g.shape
    H = R // 2
    tr = _half_tile(H)
    nblk = H // tr

    def body(c_ref, g_ref, r_ref, p_ref):
        p_ref[...] = (g_ref[...] + r_ref[...]).astype(BF16)

    half = pl.BlockSpec((None, tr, C), lambda s, i, c_ref: (s, c_ref[0] * nblk + i, 0))
    plain = pl.BlockSpec((None, tr, C), lambda s, i, c_ref: (s, i, 0))
    return pl.pallas_call(
        body, name=name,
        grid_spec=pltpu.PrefetchScalarGridSpec(num_scalar_prefetch=1, grid=(N_CHIPS, nblk), in_specs=[half, plain],
                                               out_specs=plain),
        out_shape=jax.ShapeDtypeStruct((N_CHIPS, H, C), BF16),
        compiler_params=_params(("parallel", "parallel")),
    )(c_idx, g, r1)


def _sum_chips(g, r1, r2, idx, name):
    _, R, C = g.shape
    H = R // 2
    tr = _half_tile(H)
    nblk = H // tr

    def body(idx_ref, g_ref, r1_ref, r2_ref, o_ref):
        o_ref[...] = (((g_ref[...] + r1_ref[...]) + r2_ref[0].astype(F32)) + r2_ref[1].astype(F32)) + r2_ref[2].astype(F32)

    return pl.pallas_call(
        body, name=name,
        grid_spec=pltpu.PrefetchScalarGridSpec(
            num_scalar_prefetch=1, grid=(nblk,),
            in_specs=[pl.BlockSpec((None, tr, C), lambda i, idx_ref: (idx_ref[0], idx_ref[1] * nblk + i, 0)),
                      pl.BlockSpec((None, tr, C), lambda i, idx_ref: (idx_ref[0], i, 0)),
                      pl.BlockSpec((3, tr, C), lambda i, idx_ref: (0, i, 0))],
            out_specs=pl.BlockSpec((tr, C), lambda i, idx_ref: (idx_ref[1] * nblk + i, 0))),
        out_shape=jax.ShapeDtypeStruct((R, C), F32),
        compiler_params=_params(("parallel",)),
    )(idx, g, r1, r2)


def _all_reduce_small(v, n_fold, fold_rows, fold_at):
    M, N = v.shape

    def body(x_ref, tot_ref, fold_ref, all_ref, send_sems, recv_sems, local_sem):
        x, y, c, chips = _place()
        me, sibling = (x, y, c), (x, y, 1 - c)

        def rows(px, py, pc):
            return all_ref.at[pl.ds((4 * px + 2 * py + pc) * M, M), :]

        def copy(k, block, to, src=None):
            return _remote(rows(*block) if src is None else src, rows(*block), send_sems, recv_sems, k, to)

        mine = pltpu.make_async_copy(x_ref, rows(*me), local_sem)
        mine.start()
        first = [copy(0, me, sibling, src=x_ref)]
        first += [copy(1 + j, me, (*chip, c), src=x_ref) for j, chip in enumerate(chips)]
        for cp in first:
            cp.start()
        passed = [copy(4 + j, (*chip, c), sibling) for j, chip in enumerate(chips)]
        for j, chip in enumerate(chips):
            copy(1 + j, (*chip, c), me).wait_recv()
            passed[j].start()
        copy(0, sibling, me).wait_recv()
        for j, chip in enumerate(chips):
            copy(4 + j, (*chip, 1 - c), me).wait_recv()
        for cp in first + passed:
            cp.wait_send()
        mine.wait()
        tot = all_ref[0:M, :]
        for d in range(1, 8):
            tot = tot + all_ref[d * M:(d + 1) * M, :]
        tot_ref[...] = tot
        f = tot[fold_at:fold_at + fold_rows, :]
        for e in range(1, n_fold):
            f = f + tot[fold_at + e * fold_rows:fold_at + (e + 1) * fold_rows, :]
        fold_ref[...] = f

    vm = pl.BlockSpec(memory_space=pltpu.VMEM)
    return pl.pallas_call(
        body, name="all_reduce_small", in_specs=[vm], out_specs=[vm, vm],
        out_shape=[jax.ShapeDtypeStruct((M, N), F32), jax.ShapeDtypeStruct((fold_rows, N), F32)],
        scratch_shapes=[pltpu.VMEM((8 * M, N), F32), pltpu.SemaphoreType.DMA((7,)), pltpu.SemaphoreType.DMA((7,)),
                        pltpu.SemaphoreType.DMA],
        compiler_params=pltpu.CompilerParams(has_side_effects=True, vmem_limit_bytes=VMEM_LIMIT),
    )(v)


def _as_rows(a, width):
    flat = a.reshape(-1)
    pad = (-flat.shape[0]) % width
    if pad:
        flat = jnp.concatenate([flat, jnp.zeros((pad,), flat.dtype)])
    return flat.reshape(-1, width)


class _Layout:
    def __init__(self, width, total_mult):
        self.width, self.total_mult = width, total_mult
        self.offsets, self.shapes, self.rows = {}, {}, 0

    def add(self, name, shape):
        r = -(-math.prod(shape) // self.width)
        self.offsets[name], self.shapes[name] = (self.rows, r), tuple(shape)
        self.rows += r

    def align(self, mult):
        gap = (-self.rows) % mult
        if gap:
            self.offsets[f"_gap{self.rows}"], self.shapes[f"_gap{self.rows}"] = (self.rows, gap), (gap, self.width)
            self.rows += gap
        return self.rows

    def pack(self, pieces):
        self.align(self.total_mult)
        parts = [_as_rows(pieces[n].astype(F32), self.width) if n in pieces else jnp.zeros(self.shapes[n], F32)
                 for n in self.offsets]
        return jnp.concatenate(parts, axis=0)

    def unpack(self, buf, name):
        off, r = self.offsets[name]
        shape = self.shapes[name]
        return buf[off:off + r].reshape(-1)[:math.prod(shape)].reshape(shape)


_BIG = ["ffn1_w1", "ffn1_w3", "ffn1_w2", "w_in", "ssm_glu_a", "ssm_glu_b", "w_out", "ffn2_w1", "ffn2_w3", "ffn2_w2"]
_SMALL = ["ffn1_norm", "mix_norm", "ffn2_norm", "final_norm", "attn_sinks", "ssm_a_re", "ssm_a_im", "ssm_log_step",
          "ssm_b_re", "ssm_b_im", "ssm_c_re", "ssm_c_im", "ssm_d"]
_WEIGHTS = ["meta_tokens", "ffn1_norm", "ffn1_w1", "ffn1_w3", "ffn1_w2", "mix_norm", "w_in", "attn_sinks", "ssm_a_re",
            "ssm_a_im", "ssm_log_step", "ssm_b_re", "ssm_b_im", "ssm_c_re", "ssm_c_im", "ssm_d", "ssm_glu_a",
            "ssm_glu_b", "w_out", "ffn2_norm", "ffn2_w1", "ffn2_w3", "ffn2_w2", "final_norm"]


def _kv_interleave(w, kv_heads):
    kvw = kv_heads * HEAD_DIM
    lead = w.shape[:-1]
    k = w[..., 0:kvw].reshape(lead + (kv_heads, 1, HEAD_DIM))
    v = w[..., kvw:2 * kvw].reshape(lead + (kv_heads, 1, HEAD_DIM))
    return jnp.concatenate([jnp.concatenate([k, v], axis=-2).reshape(lead + (2 * kvw,)), w[..., 2 * kvw:]], axis=-1)


def _kv_deinterleave(w, kv_heads):
    kvw = kv_heads * HEAD_DIM
    lead = w.shape[:-1]
    kv = w[..., 0:2 * kvw].reshape(lead + (kv_heads, 2, HEAD_DIM))
    return jnp.concatenate([kv[..., 0, :].reshape(lead + (kvw,)), kv[..., 1, :].reshape(lead + (kvw,)), w[..., 2 * kvw:]],
                           axis=-1)


def _step(x, target, w, m, v):
    B, S, D = x.shape
    L = S + N_META
    T = B * L
    H = D // HEAD_DIM
    KV = H // Q_PER_KV
    SW = D // 2
    tm = _pick_tile(L, ROW_TILE_CAP, ROW_ALIGN)
    rc = _pick_tile(L, ROW_TILE_CAP // B, 4) * B
    tw = _pick_tile(T, 3 * ROW_TILE_CAP, ROW_ALIGN)
    my_c = lax.axis_index("c")
    my_slot = 2 * lax.axis_index("x") + lax.axis_index("y")

    groups = {"ffn1": ["ffn1_w1", "ffn1_w3", "ffn1_w2"], "mix": ["w_in", "ssm_glu_a", "ssm_glu_b", "w_out"],
              "ffn2": ["ffn2_w1", "ffn2_w3", "ffn2_w2"]}
    waves = {"first": ["ffn1_w1", "ffn1_w3"], "early": ["ffn1_w2"] + groups["mix"], "late": groups["ffn2"]}
    shards = {n: w[n][0].astype(BF16) for n in _BIG}
    gathered = _gather_weights([shards[n] for n in waves["first"]] + [w["meta_tokens"]])
    ws = dict(zip(waves["first"], gathered[:-1]))
    meta = jnp.transpose(gathered[-1], (1, 0, 2)).reshape(N_META, D)

    def arrive(wave, landed):
        mine = [shards[n] for n in waves[wave]]
        ws.update(zip(waves[wave], _forward_halves(landed, mine, wave)))

    g_ffn1, g_mix, g_ffn2 = w["ffn1_norm"], w["mix_norm"], w["ffn2_norm"]
    g_final = w["final_norm"].reshape(1, D)

    h0 = jnp.concatenate([jnp.broadcast_to(meta[None], (B, N_META, D)), x], axis=1).reshape(T, D)

    def ffn_fwd(h, g, tag, carry=None):
        n = _rmsnorm_fwd(h, g, tm, f"{tag}_norm")
        riders = None if carry is None else _gather_riders([shards[k] for k in waves[carry]])
        out = _ffn_up(n, ws[f"{tag}_w1"], ws[f"{tag}_w3"], tm, f"{tag}_up", riders)
        if carry is not None:
            out, landed = out
            arrive(carry, landed)
        a, c, s = out
        return _ffn_down(s, ws[f"{tag}_w2"], h, tm, f"{tag}_down"), (n, a, c, s)

    h1, saved1 = ffn_fwd(h0, g_ffn1, "ffn1", carry="early")
    w_kvu = _kv_interleave(ws["w_in"][1], KV)
    hn = _rmsnorm_fwd(h1, g_mix, tm, "mix_norm")
    q = _mm_colslots(hn, ws["w_in"], BF16, "w_in_q", tm, first=0, count=1, scale=HEAD_DIM ** -0.5)
    kvu = _mm_plain(hn, w_kvu, "nn", F32, "w_in_kvu", tm)
    gates = _mm_colslots(hn, ws["w_in"], F32, "w_in_gates", tm, first=2, count=2)

    sinks = w["attn_sinks"].reshape(KV, Q_PER_KV, 1, 1)
    sink_col = jnp.broadcast_to(sinks, (KV, Q_PER_KV, BLOCK, 1)).reshape(KV, Q_PER_KV * BLOCK, 1)
    sink_meta = jnp.broadcast_to(sinks, (KV, Q_PER_KV, N_META, 1)).reshape(KV, Q_PER_KV * N_META, 1)
    (attn,), landed = _attn_fwd(q, kvu, sink_col, sink_meta, B, "attn_fwd",
                                _gather_riders([shards[k] for k in waves["late"]]))
    arrive("late", landed)

    def to_time_major(a2d):
        return jnp.transpose(a2d.reshape(B, L, a2d.shape[-1]), (1, 0, 2)).reshape(T, a2d.shape[-1])

    def to_batch_major(a2d):
        return jnp.transpose(a2d.reshape(L, B, a2d.shape[-1]), (1, 0, 2)).reshape(T, a2d.shape[-1])

    ssm_args = (w["ssm_a_re"][0], w["ssm_a_im"][0], w["ssm_log_step"][0], w["ssm_b_re"][0], w["ssm_b_im"][0],
                w["ssm_c_re"][0], w["ssm_c_im"][0])
    (lam, bmat, cmat), ssm_vjp = jax.vjp(_ssm_matrices, *ssm_args)
    bmat16, cmat16 = bmat.astype(BF16), cmat.astype(BF16)
    u_t = to_time_major(kvu[:, SW:])
    y_t, xs, xp = _ssm_fwd(u_t, bmat16, cmat16, w["ssm_d"], _scan_tables(lam, B, False), B, rc, "ssm_fwd")
    y0 = to_batch_major(y_t)
    yg = _gelu_fwd(y0, tm, "gelu_fwd")
    ga = _mm_colslots(yg, ws["ssm_glu_a"], F32, "glu_a", tm)
    gb = _mm_colslots(yg, ws["ssm_glu_b"], F32, "glu_b", tm)
    merged = _merge_fwd(gates, attn, ga, gb, tm, "merge_fwd")
    h2 = _mm_rowslots(merged, ws["w_out"], h1, tm, "w_out")
    h3, saved2 = ffn_fwd(h2, g_ffn2, "ffn2")
    dh3, dh3b, dg_final, loss_row = _loss_head(h3, g_final, target, tm, "loss_head")

    grads, swapped, received = {}, {}, {}
    c_idx = my_c.reshape(1).astype(jnp.int32)
    idx = jnp.stack([my_slot, my_c]).astype(jnp.int32)

    def chip_sums(group):
        names = groups[group]
        for n, r in zip(names, _swap_halves([grads[n] for n in names], group)):
            swapped[n] = r
        return [_sum_halves(grads[n], swapped[n], c_idx, f"grad_sum_halves_{n}") for n in names]

    def ffn_bwd(h, g, saved, dh, dhb, tag, carry=None):
        n, a, c, s = saved
        w1, w3, w2 = ws[f"{tag}_w1"], ws[f"{tag}_w3"], ws[f"{tag}_w2"]
        grads[f"{tag}_w2"] = _wgrad_hidden_rows(s, dhb, tw, f"{tag}_dw2", 0.5)
        if carry is None:
            da, dc = _ffn_dhidden(dhb, w2, a, c, tm, f"{tag}_dhidden")
        else:
            (da, dc), got = _ffn_dhidden(dhb, w2, a, c, tm, f"{tag}_dhidden", _exchange_riders(chip_sums(carry)))
            received.update(zip(groups[carry], got))
        grads[f"{tag}_w1"] = _wgrad_hidden_cols(n, da, tw, f"{tag}_dw1")
        grads[f"{tag}_w3"] = _wgrad_hidden_cols(n, dc, tw, f"{tag}_dw3")
        dh_in, dhb_in, grads[f"{tag}_norm"] = _ffn_dn(da, w1, dc, w3, h, g, dh, tm, f"{tag}_dn")
        return dh_in, dhb_in

    dh2, dh2b = ffn_bwd(h2, g_ffn2, saved2, dh3, dh3b, "ffn2")

    grads["w_out"] = _wgrad_rowslots(merged, dh2b, tw, "dw_out")
    dmerged = _mm_rowslots_t(dh2b, ws["w_out"], tm, "dmerged")
    dattn, dgat, dgss, dga, dgb = _merge_bwd(dmerged, gates, attn, ga, gb, tm, "merge_bwd")
    grads["ssm_glu_a"] = _wgrad_colslots(yg, dga, tw, "dglu_a")
    grads["ssm_glu_b"] = _wgrad_colslots(yg, dgb, tw, "dglu_b")
    dyg = _mm_colslots_t([(dga, ws["ssm_glu_a"]), (dgb, ws["ssm_glu_b"])], tm, "dyg")
    dy0 = _gelu_bwd(dyg, y0, tm, "gelu_bwd")
    du_t, dbmat, dcmat, dlam, dd = _ssm_bwd(to_time_major(dy0), u_t, xs, xp, bmat16, cmat16, w["ssm_d"],
                                            _scan_tables(lam, B, True), B, rc, "ssm_bwd")
    d_ssm = ssm_vjp((dlam[:, 0, :], dbmat, dcmat))
    for n, gval in zip(["ssm_a_re", "ssm_a_im", "ssm_log_step", "ssm_b_re", "ssm_b_im", "ssm_c_re", "ssm_c_im"], d_ssm):
        grads[n] = gval[None]
    grads["ssm_d"] = dd

    (dq, dkv, dsink), got = _attn_bwd(q, kvu, attn, dattn, sink_col, sink_meta, B, "attn_bwd",
                                      _exchange_riders(chip_sums("ffn2")))
    received.update(zip(groups["ffn2"], got))
    grads["attn_sinks"] = dsink[:, 0:Q_PER_KV, 0].reshape(1, H)
    dkvu = jnp.concatenate([dkv, to_batch_major(du_t).astype(BF16)], axis=1)
    pieces = [dq, dkvu, dgat, dgss]
    dw_in = [_wgrad_plain(hn, p, f"dw_in_{k}", tw) for k, p in enumerate(pieces)]
    dw_in[1] = _kv_deinterleave(dw_in[1], KV)
    grads["w_in"] = jnp.stack(dw_in)
    w_in_parts = [ws["w_in"][0], w_kvu, ws["w_in"][2], ws["w_in"][3]]
    whole = _once((D, D), lambda i: (0, 0))
    dh1, dh1b, grads["mix_norm"] = _mm_norm_bwd(
        "dhn", "nt", [(p, _spec((tm, D), lambda i: (i, 0)), wp, whole) for p, wp in zip(pieces, w_in_parts)],
        h1, g_mix, dh2, tm)
    dh0, _ = ffn_bwd(h0, g_ffn1, saved1, dh1, dh1b, "ffn1", carry="mix")
    dh0 = dh0.reshape(B, L, D)
    grad_x = dh0[:, N_META:, :]

    grads["final_norm"] = dg_final
    slay = _Layout(D, 8)
    for n in _SMALL:
        slay.add(n, w[n].shape)
    slay.add("loss", (1, D))
    meta_at = slay.align(8)
    slay.add("meta", (B * N_META, D))
    small = slay.pack({**{n: grads[n] for n in _SMALL}, "loss": loss_row, "meta": dh0[:, :N_META, :]})
    tot_small, dmeta = _all_reduce_small(small, B, N_META, meta_at)
    loss = slay.unpack(tot_small, "loss")[0, 0]
    for n in _SMALL:
        grads[n] = slay.unpack(tot_small, n)
    cw = D // N_CHIPS
    grads["meta_tokens"] = lax.dynamic_slice_in_dim(dmeta, my_slot * cw, cw, axis=1)

    received.update(zip(groups["ffn1"], _exchange_chips(chip_sums("ffn1"))))
    fulls = [_sum_chips(grads[n], swapped[n], received[n], idx, f"grad_sum_chips_{n}") for n in _BIG]
    for n, f in zip(_BIG, _join_halves(fulls)):
        grads[n] = f[None]

    delta, new_m, new_v = {}, {}, {}
    for n in _BIG + ["meta_tokens"]:
        shp = w[n].shape
        two = (shp[-2], shp[-1])
        d_, m_, v_ = _adamw(w[n].reshape(two), grads[n].reshape(two), m[n].reshape(two), v[n].reshape(two), f"adamw_{n}")
        delta[n], new_m[n], new_v[n] = d_.reshape(shp), m_.reshape(shp), v_.reshape(shp)
        grads[n] = grads[n].reshape(shp)
    def flat2d(a):
        return a.reshape(-1, D) if a.size % D == 0 else a.reshape(1, -1)

    d_, m_, v_ = _adamw_small([flat2d(w[n]) for n in _SMALL], [flat2d(grads[n]) for n in _SMALL],
                              [flat2d(m[n]) for n in _SMALL], [flat2d(v[n]) for n in _SMALL], "adamw_small")
    for i, n in enumerate(_SMALL):
        shp = w[n].shape
        delta[n], new_m[n], new_v[n] = d_[i].reshape(shp), m_[i].reshape(shp), v_[i].reshape(shp)
        grads[n] = grads[n].reshape(shp)

    return (loss, grad_x, *[grads[n] for n in _WEIGHTS], *[delta[n] for n in _WEIGHTS],
            *[new_m[n] for n in _WEIGHTS], *[new_v[n] for n in _WEIGHTS])


def kernel(x, meta_tokens, ffn1_norm, ffn1_w1, ffn1_w3, ffn1_w2, mix_norm, w_in, attn_sinks, ssm_a_re, ssm_a_im, ssm_log_step, ssm_b_re, ssm_b_im, ssm_c_re, ssm_c_im, ssm_d, ssm_glu_a, ssm_glu_b, w_out, ffn2_norm, ffn2_w1, ffn2_w3, ffn2_w2, final_norm, loss_target, m_meta_tokens, m_ffn1_norm, m_ffn1_w1, m_ffn1_w3, m_ffn1_w2, m_mix_norm, m_w_in, m_attn_sinks, m_ssm_a_re, m_ssm_a_im, m_ssm_log_step, m_ssm_b_re, m_ssm_b_im, m_ssm_c_re, m_ssm_c_im, m_ssm_d, m_ssm_glu_a, m_ssm_glu_b, m_w_out, m_ffn2_norm, m_ffn2_w1, m_ffn2_w3, m_ffn2_w2, m_final_norm, v_meta_tokens, v_ffn1_norm, v_ffn1_w1, v_ffn1_w3, v_ffn1_w2, v_mix_norm, v_w_in, v_attn_sinks, v_ssm_a_re, v_ssm_a_im, v_ssm_log_step, v_ssm_b_re, v_ssm_b_im, v_ssm_c_re, v_ssm_c_im, v_ssm_d, v_ssm_glu_a, v_ssm_glu_b, v_w_out, v_ffn2_norm, v_ffn2_w1, v_ffn2_w3, v_ffn2_w2, v_final_norm):
    args = locals()
    w = {n: args[n] for n in _WEIGHTS}
    m = {n: args["m_" + n] for n in _WEIGHTS}
    v = {n: args["v_" + n] for n in _WEIGHTS}
    return _step(x, loss_target, w, m, v)
```

```python
import functools
import math

import jax
import jax.numpy as jnp
from jax import lax
from jax.experimental import pallas as pl
from jax.experimental.pallas import tpu as pltpu

F32 = jnp.float32
BF16 = jnp.bfloat16
MESH_IDS = pl.DeviceIdType.MESH

N_CHIPS = 4
N_META = 16
HEAD_DIM = 64
Q_PER_KV = 4
QW = Q_PER_KV * HEAD_DIM
BLOCK = 128
SSM_GROUP = 16
SSM_STATE = 64
SSM_LANES = 128
GROUPS_PER_COL = SSM_LANES // SSM_GROUP
STATE_LANES = GROUPS_PER_COL * SSM_STATE
NORM_EPS = 1e-6
NEG_INF = -1e30
ADAM_LR, ADAM_B1, ADAM_B2, ADAM_EPS, ADAM_WD, ADAM_STEP = 0.001, 0.9, 0.999, 1e-08, 0.01, 10
GELU_C = math.sqrt(2.0 / math.pi)
ROW_ALIGN = 16
VMEM_LIMIT = 56 * 1024 * 1024
ROW_TILE_CAP = 688

_NN = (((1,), (0,)), ((), ()))
_NT = (((1,), (1,)), ((), ()))
_TN = (((0,), (0,)), ((), ()))
_DIMS = {"nn": _NN, "nt": _NT, "tn": _TN}


def _params(sem, **kw):
    return pltpu.CompilerParams(dimension_semantics=sem, vmem_limit_bytes=VMEM_LIMIT, **kw)


def _pick_tile(n, cap, mult):
    best = None
    for t in range(mult, min(n, cap) + 1, mult):
        if n % t == 0:
            best = t
    if best is None:
        raise ValueError(f"no tile for {n} (cap {cap}, multiple of {mult})")
    return best


def _sigmoid(x):
    return 1.0 / (1.0 + jnp.exp(-x))


def _spec(block, index_map):
    return pl.BlockSpec(block, index_map)


def _sum_dots(ins, mode):
    tot = None
    for p in range(len(ins) // 2):
        a_ref, b_ref = ins[2 * p], ins[2 * p + 1]
        for sl in ([None] if len(b_ref.shape) == 2 else range(b_ref.shape[0])):
            if sl is None:
                a, b = a_ref[...], b_ref[...]
            elif len(a_ref.shape) == 3:
                a, b = a_ref[sl], b_ref[sl]
            else:
                width = a_ref.shape[1] // b_ref.shape[0]
                a, b = a_ref[:, sl * width:(sl + 1) * width], b_ref[sl]
            d = lax.dot_general(a.astype(BF16), b.astype(BF16), _DIMS[mode], preferred_element_type=F32)
            tot = d if tot is None else tot + d
    return tot


def _mm(name, grid, kaxis, mode, pairs, out_shape, out_spec, scale=1.0, res=None):
    npairs = len(pairs)
    has_res = res is not None
    gk = 1 if kaxis is None else grid[kaxis]
    acc_shape = tuple(d for d in out_spec.block_shape if d is not None)

    def body(*refs):
        res_ref = refs[2 * npairs] if has_res else None
        o_ref = refs[2 * npairs + has_res]
        tot = _sum_dots(refs[:2 * npairs], mode)

        def finish(acc):
            r = acc * scale if scale != 1.0 else acc
            if has_res:
                r = res_ref[...] + r
            o_ref[...] = r.astype(o_ref.dtype)

        if gk == 1:
            finish(tot)
        else:
            acc_ref = refs[-1]
            k = pl.program_id(kaxis)

            @pl.when(k == 0)
            def _():
                acc_ref[...] = tot

            @pl.when(k > 0)
            def _():
                acc_ref[...] += tot

            @pl.when(k == gk - 1)
            def _():
                finish(acc_ref[...])

    in_specs, args = [], []
    for a, a_spec, b, b_spec in pairs:
        in_specs += [a_spec, b_spec]
        args += [a, b]
    if has_res:
        in_specs.append(res[1])
        args.append(res[0])
    sem = tuple("arbitrary" if ax == kaxis else "parallel" for ax in range(len(grid)))
    return pl.pallas_call(
        body, name=name, grid=grid, in_specs=in_specs, out_specs=out_spec, out_shape=out_shape,
        scratch_shapes=[pltpu.VMEM(acc_shape, F32)] if gk > 1 else [],
        compiler_params=_params(sem),
    )(*args)


def _mm_plain(a, b, mode, out_dtype, name, tm, scale=1.0):
    M, K = a.shape
    N = b.shape[1] if mode == "nn" else b.shape[0]
    return _mm(name, (M // tm,), None, mode,
               [(a, _spec((tm, K), lambda i: (i, 0)), b, _spec(b.shape, lambda i: (0, 0)))],
               jax.ShapeDtypeStruct((M, N), out_dtype), _spec((tm, N), lambda i: (i, 0)), scale=scale)


def _wgrad_plain(a, b, name, tr):
    R, M = a.shape
    N = b.shape[1]
    return _mm(name, (R // tr,), 0, "tn",
               [(a, _spec((tr, M), lambda r: (r, 0)), b, _spec((tr, N), lambda r: (r, 0)))],
               jax.ShapeDtypeStruct((M, N), F32), _spec((M, N), lambda r: (0, 0)))


def _rmsnorm_fwd(h, g, tm, name):
    T, D = h.shape

    def body(h_ref, g_ref, o_ref):
        x = h_ref[...]
        r = lax.rsqrt(jnp.mean(x * x, axis=-1, keepdims=True) + NORM_EPS)
        o_ref[...] = ((x * r) * g_ref[...]).astype(BF16)

    return pl.pallas_call(
        body, name=name, grid=(T // tm,),
        in_specs=[pl.BlockSpec((tm, D), lambda i: (i, 0)), pl.BlockSpec((1, D), lambda i: (0, 0))],
        out_specs=pl.BlockSpec((tm, D), lambda i: (i, 0)),
        out_shape=jax.ShapeDtypeStruct((T, D), BF16),
        compiler_params=_params(("parallel",)),
    )(h, g)


def _fold8(x):
    return jnp.sum(x.reshape(x.shape[0] // 8, 8, x.shape[1]), axis=0)


def _mm_norm_bwd(name, mode, pairs, h, g, dres, tm):
    T, D = h.shape
    nt = T // tm
    npairs = len(pairs)

    def body(*refs):
        h_ref, g_ref, dres_ref, dh_ref, dhb_ref, dg_ref, acc_ref = refs[2 * npairs:]
        i = pl.program_id(0)
        x = h_ref[...]
        r = lax.rsqrt(jnp.mean(x * x, axis=-1, keepdims=True) + NORM_EPS)
        xhat = x * r
        dy = _sum_dots(refs[:2 * npairs], mode)
        dxhat = dy * g_ref[...]
        dx = r * (dxhat - xhat * jnp.mean(dxhat * xhat, axis=-1, keepdims=True))
        dh = dres_ref[...] + dx
        dh_ref[...] = dh
        dhb_ref[...] = dh.astype(BF16)
        part = _fold8(dy * xhat)

        @pl.when(i == 0)
        def _():
            acc_ref[...] = part

        @pl.when(i > 0)
        def _():
            acc_ref[...] += part

        @pl.when(i == nt - 1)
        def _():
            dg_ref[...] = jnp.sum(acc_ref[...], axis=0, keepdims=True)

    row = pl.BlockSpec((tm, D), lambda i: (i, 0))
    vec = pl.BlockSpec((1, D), lambda i: (0, 0))
    in_specs, args = [], []
    for a, a_spec, b, b_spec in pairs:
        in_specs += [a_spec, b_spec]
        args += [a, b]
    return pl.pallas_call(
        body, name=name, grid=(nt,),
        in_specs=in_specs + [row, vec, row],
        out_specs=[row, row, vec],
        out_shape=[jax.ShapeDtypeStruct((T, D), F32), jax.ShapeDtypeStruct((T, D), BF16),
                   jax.ShapeDtypeStruct((1, D), F32)],
        scratch_shapes=[pltpu.VMEM((8, D), F32)],
        compiler_params=_params(("arbitrary",)),
    )(*args, h, g, dres)


def _ffn_up(n, w1, w3, tm, name, riders=None):
    T, D = n.shape
    Fs = w1.shape[2]

    def body(n_ref, w1_ref, w3_ref, a_ref, c_ref, s_ref):
        x = n_ref[...]
        a = jnp.dot(x, w1_ref[...], preferred_element_type=F32)
        c = jnp.dot(x, w3_ref[...], preferred_element_type=F32)
        a_ref[...] = a.astype(BF16)
        c_ref[...] = c.astype(BF16)
        s_ref[...] = (a * _sigmoid(a) * c).astype(BF16)

    w_spec = _spec((None, D, Fs), lambda s, i: (s, 0, 0))
    o_spec = _spec((None, tm, Fs), lambda s, i: (s, i, 0))
    o_shape = jax.ShapeDtypeStruct((N_CHIPS, T, Fs), BF16)
    return _call(body, name, (N_CHIPS, T // tm), [_spec((tm, D), lambda s, i: (i, 0)), w_spec, w_spec],
                 [o_spec, o_spec, o_spec], [o_shape, o_shape, o_shape], [], ("parallel", "parallel"), (n, w1, w3), riders)


def _ffn_down(s, w2, h, tm, name):
    _, T, Fs = s.shape
    D = w2.shape[2]
    row = _spec((tm, D), lambda i: (i, 0))
    return _mm(name, (T // tm,), None, "nn",
               [(s, _spec((N_CHIPS, tm, Fs), lambda i: (0, i, 0)), w2, _spec((N_CHIPS, Fs, D), lambda i: (0, 0, 0)))],
               jax.ShapeDtypeStruct((T, D), F32), row, scale=0.5, res=(h, row))


def _ffn_dhidden(dhb, w2, a, c, tm, name, riders=None):
    T, D = dhb.shape
    Fs = w2.shape[1]

    def body(dh_ref, w2_ref, a_ref, c_ref, da_ref, dc_ref):
        d = 0.5 * lax.dot_general(dh_ref[...], w2_ref[...], _NT, preferred_element_type=F32)
        av = a_ref[...].astype(F32)
        cv = c_ref[...].astype(F32)
        sg = _sigmoid(av)
        da_ref[...] = (d * cv * (sg * (1.0 + av * (1.0 - sg)))).astype(BF16)
        dc_ref[...] = (d * (av * sg)).astype(BF16)

    h_spec = _spec((None, tm, Fs), lambda s, i: (s, i, 0))
    o_shape = jax.ShapeDtypeStruct((N_CHIPS, T, Fs), BF16)
    return _call(body, name, (N_CHIPS, T // tm),
                 [_spec((tm, D), lambda s, i: (i, 0)), _spec((None, Fs, D), lambda s, i: (s, 0, 0)), h_spec, h_spec],
                 [h_spec, h_spec], [o_shape, o_shape], [], ("parallel", "parallel"), (dhb, w2, a, c), riders)


def _wgrad_hidden_rows(s, dhb, tr, name, scale):
    _, T, Fs = s.shape
    D = dhb.shape[1]
    return _mm(name, (N_CHIPS, T // tr), 1, "tn",
               [(s, _spec((None, tr, Fs), lambda k, r: (k, r, 0)), dhb, _spec((tr, D), lambda k, r: (r, 0)))],
               jax.ShapeDtypeStruct((N_CHIPS, Fs, D), F32), _spec((None, Fs, D), lambda k, r: (k, 0, 0)), scale=scale)


def _wgrad_hidden_cols(n, da, tr, name):
    T, D = n.shape
    Fs = da.shape[2]
    return _mm(name, (N_CHIPS, T // tr), 1, "tn",
               [(n, _spec((tr, D), lambda k, r: (r, 0)), da, _spec((None, tr, Fs), lambda k, r: (k, r, 0)))],
               jax.ShapeDtypeStruct((N_CHIPS, D, Fs), F32), _spec((None, D, Fs), lambda k, r: (k, 0, 0)))


def _once(block, index_map):
    return pl.BlockSpec(block, index_map, pipeline_mode=pl.Buffered(1))


def _ffn_dn(da, w1, dc, w3, h, g, dres, tm, name):
    _, T, Fs = da.shape
    D = w1.shape[1]
    h_spec = _spec((N_CHIPS, tm, Fs), lambda i: (0, i, 0))
    w_spec = _once((N_CHIPS, D, Fs), lambda i: (0, 0, 0))
    return _mm_norm_bwd(name, "nt", [(da, h_spec, w1, w_spec), (dc, h_spec, w3, w_spec)], h, g, dres, tm)


def _mm_side_by_side(a, w, mode, out_dtype, name, tm, first=0, count=N_CHIPS, scale=1.0):
    T, K = a.shape
    assert first % count == 0
    n = w.shape[2] if mode == "nn" else w.shape[1]

    def body(a_ref, w_ref, o_ref):
        av = a_ref[...].astype(BF16)
        for j in range(count):
            r = lax.dot_general(av, w_ref[j].astype(BF16), _DIMS[mode], preferred_element_type=F32)
            o_ref[:, j * n:(j + 1) * n] = (r * scale if scale != 1.0 else r).astype(o_ref.dtype)

    return pl.pallas_call(
        body, name=name, grid=(T // tm,),
        in_specs=[_spec((tm, K), lambda i: (i, 0)), _once((count,) + w.shape[1:], lambda i: (first // count, 0, 0))],
        out_specs=_spec((tm, count * n), lambda i: (i, 0)),
        out_shape=jax.ShapeDtypeStruct((T, count * n), out_dtype),
        compiler_params=_params(("parallel",)),
    )(a, w)


def _mm_colslots(a, w, out_dtype, name, tm, first=0, count=N_CHIPS, scale=1.0):
    return _mm_side_by_side(a, w, "nn", out_dtype, name, tm, first, count, scale)


def _wgrad_colslots(a, d, tr, name):
    T, K = a.shape
    Ns = d.shape[1] // N_CHIPS
    return _mm(name, (N_CHIPS, T // tr), 1, "tn",
               [(a, _spec((tr, K), lambda k, r: (r, 0)), d, _spec((tr, Ns), lambda k, r: (r, k)))],
               jax.ShapeDtypeStruct((N_CHIPS, K, Ns), F32), _spec((None, K, Ns), lambda k, r: (k, 0, 0)))


def _mm_rowslots(a, w, h, tm, name):
    T = a.shape[0]
    N = w.shape[2]
    row = _spec((tm, N), lambda i: (i, 0))
    return _mm(name, (T // tm,), None, "nn",
               [(a, _spec((tm, a.shape[1]), lambda i: (i, 0)), w, _once(w.shape, lambda i: (0, 0, 0)))],
               jax.ShapeDtypeStruct((T, N), F32), row, res=(h, row))


def _wgrad_rowslots(a, d, tr, name):
    T = a.shape[0]
    Ks = a.shape[1] // N_CHIPS
    N = d.shape[1]
    return _mm(name, (N_CHIPS, T // tr), 1, "tn",
               [(a, _spec((tr, Ks), lambda k, r: (r, k)), d, _spec((tr, N), lambda k, r: (r, 0)))],
               jax.ShapeDtypeStruct((N_CHIPS, Ks, N), F32), _spec((None, Ks, N), lambda k, r: (k, 0, 0)))


def _gelu_parts(x):
    inner = GELU_C * (x + 0.044715 * (x * x * x))
    t = jnp.tanh(inner)
    return t, GELU_C * (1.0 + 3.0 * 0.044715 * (x * x))


def _gelu_fwd(y, tm, name):
    T, W = y.shape

    def body(y_ref, o_ref):
        x = y_ref[...]
        t, _ = _gelu_parts(x)
        o_ref[...] = (0.5 * x * (1.0 + t)).astype(BF16)

    spec = pl.BlockSpec((tm, W), lambda i: (i, 0))
    return pl.pallas_call(body, name=name, grid=(T // tm,), in_specs=[spec], out_specs=spec,
                          out_shape=jax.ShapeDtypeStruct((T, W), BF16),
                          compiler_params=_params(("parallel",)))(y)


def _gelu_bwd(pairs, y, tm, name):
    T, W = y.shape
    npairs = len(pairs)

    def body(*refs):
        y_ref, o_ref = refs[2 * npairs], refs[2 * npairs + 1]
        x = y_ref[...]
        t, dinner = _gelu_parts(x)
        o_ref[...] = _sum_dots(refs[:2 * npairs], "nt") * (0.5 * (1.0 + t) + 0.5 * x * (1.0 - t * t) * dinner)

    spec = pl.BlockSpec((tm, W), lambda i: (i, 0))
    in_specs, args = [], []
    for d, w in pairs:
        in_specs += [_spec((tm, d.shape[1]), lambda i: (i, 0)), _once(w.shape, lambda i: (0, 0, 0))]
        args += [d, w]
    return pl.pallas_call(body, name=name, grid=(T // tm,), in_specs=in_specs + [spec], out_specs=spec,
                          out_shape=jax.ShapeDtypeStruct((T, W), F32),
                          compiler_params=_params(("parallel",)))(*args, y)


def _merge_cols(D):
    cb = 512 if D % 512 == 0 else D
    return cb, D // cb


def _merge_fwd(gates, attn, ga, gb, tm, name):
    T, D = attn.shape
    cb, nc = _merge_cols(D)

    def body(gat_ref, gss_ref, attn_ref, ga_ref, gb_ref, o_ref):
        ssm = ga_ref[...] * _sigmoid(gb_ref[...])
        o_ref[...] = (_sigmoid(gat_ref[...]) * attn_ref[...] + _sigmoid(gss_ref[...]) * ssm).astype(BF16)

    def col(block):
        return pl.BlockSpec((tm, cb), lambda i, j: (i, block * nc + j))

    return pl.pallas_call(
        body, name=name, grid=(T // tm, nc),
        in_specs=[col(0), col(1), col(0), col(0), col(0)],
        out_specs=col(0), out_shape=jax.ShapeDtypeStruct((T, D), BF16),
        compiler_params=_params(("parallel", "parallel")),
    )(gates, gates, attn, ga, gb)


def _merge_bwd(dhb, w_out, gates, attn, ga, gb, tm, name):
    T, D = attn.shape
    cb, nc = _merge_cols(D)
    Ks = w_out.shape[1]
    spb = cb // Ks

    def body(dh_ref, w_ref, gat_ref, gss_ref, attn_ref, ga_ref, gb_ref, dattn_ref, dgat_ref, dgss_ref, dga_ref, dgb_ref):
        dh = dh_ref[...]
        d = jnp.concatenate([lax.dot_general(dh, w_ref[s], _NT, preferred_element_type=F32) for s in range(spb)], axis=1)
        sa = _sigmoid(gat_ref[...])
        ss = _sigmoid(gss_ref[...])
        sb = _sigmoid(gb_ref[...])
        gav = ga_ref[...]
        dattn_ref[...] = d * sa
        dgat_ref[...] = (d * attn_ref[...] * (sa * (1.0 - sa))).astype(BF16)
        dgss_ref[...] = (d * (gav * sb) * (ss * (1.0 - ss))).astype(BF16)
        dssm = d * ss
        dga_ref[...] = (dssm * sb).astype(BF16)
        dgb_ref[...] = (dssm * gav * (sb * (1.0 - sb))).astype(BF16)

    def col(block):
        return pl.BlockSpec((tm, cb), lambda i, j: (i, block * nc + j))

    b16 = jax.ShapeDtypeStruct((T, D), BF16)
    return pl.pallas_call(
        body, name=name, grid=(T // tm, nc),
        in_specs=[pl.BlockSpec((tm, D), lambda i, j: (i, 0)), pl.BlockSpec((spb, Ks, D), lambda i, j: (j, 0, 0)),
                  col(0), col(1), col(0), col(0), col(0)],
        out_specs=[col(0)] * 5,
        out_shape=[jax.ShapeDtypeStruct((T, D), F32), b16, b16, b16, b16],
        compiler_params=_params(("parallel", "parallel")),
    )(dhb, w_out, gates, gates, attn, ga, gb)


def _loss_head(h, g, target, tm, name):
    T, D = h.shape
    B, S, _ = target.shape
    L = S + N_META
    nt = T // tm
    tpe = L // tm

    def body(h_ref, g_ref, t_hbm, dh_ref, dhb_ref, dg_ref, loss_ref, tbuf, acc_g, acc_l, sem):
        i = pl.program_id(0)
        b, j = i // tpe, i % tpe

        @pl.when(j == 0)
        def _():
            tbuf[0:N_META, :] = jnp.zeros((N_META, D), F32)
            cp = pltpu.make_async_copy(t_hbm.at[b, pl.ds(0, tm - N_META), :], tbuf.at[pl.ds(N_META, tm - N_META), :], sem)
            cp.start()
            cp.wait()

        @pl.when(j > 0)
        def _():
            cp = pltpu.make_async_copy(t_hbm.at[b, pl.ds(j * tm - N_META, tm), :], tbuf, sem)
            cp.start()
            cp.wait()

        x = h_ref[...]
        gv = g_ref[...]
        r = lax.rsqrt(jnp.mean(x * x, axis=-1, keepdims=True) + NORM_EPS)
        xhat = x * r
        pos = j * tm + lax.broadcasted_iota(jnp.int32, (tm, 1), 0)
        err = jnp.where(pos >= N_META, xhat * gv - tbuf[...], 0.0)
        dy = err * (1.0 / D)
        dxhat = dy * gv
        dh = r * (dxhat - xhat * jnp.mean(dxhat * xhat, axis=-1, keepdims=True))
        dh_ref[...] = dh
        dhb_ref[...] = dh.astype(BF16)
        pg = _fold8(dy * xhat)
        pe = _fold8(err * err)

        @pl.when(i == 0)
        def _():
            acc_g[...] = pg
            acc_l[...] = pe

        @pl.when(i > 0)
        def _():
            acc_g[...] += pg
            acc_l[...] += pe

        @pl.when(i == nt - 1)
        def _():
            dg_ref[...] = jnp.sum(acc_g[...], axis=0, keepdims=True)
            loss_ref[...] = jnp.full((1, D), (0.5 / D) * jnp.sum(acc_l[...]), F32)

    row = pl.BlockSpec((tm, D), lambda i: (i, 0))
    vec = pl.BlockSpec((1, D), lambda i: (0, 0))
    return pl.pallas_call(
        body, name=name, grid=(nt,),
        in_specs=[row, vec, pl.BlockSpec(memory_space=pl.ANY)], out_specs=[row, row, vec, vec],
        out_shape=[jax.ShapeDtypeStruct((T, D), F32), jax.ShapeDtypeStruct((T, D), BF16),
                   jax.ShapeDtypeStruct((1, D), F32), jax.ShapeDtypeStruct((1, D), F32)],
        scratch_shapes=[pltpu.VMEM((tm, D), F32), pltpu.VMEM((8, D), F32), pltpu.VMEM((8, D), F32),
                        pltpu.SemaphoreType.DMA],
        compiler_params=_params(("arbitrary",)),
    )(h, g, target)


def _heads_to_rows(blk):
    return jnp.concatenate([blk[:, g * HEAD_DIM:(g + 1) * HEAD_DIM] for g in range(Q_PER_KV)], axis=0)


def _rows_to_heads(x):
    rows = x.shape[0] // Q_PER_KV
    return jnp.concatenate([x[g * rows:(g + 1) * rows] for g in range(Q_PER_KV)], axis=1)


def _causal(R):
    qi = lax.broadcasted_iota(jnp.int32, (R, BLOCK), 0) & (BLOCK - 1)
    kj = lax.broadcasted_iota(jnp.int32, (R, BLOCK), 1)
    return kj <= qi


def _band_probs(s_band, s_m, sink):
    m = jnp.maximum(jnp.maximum(jnp.max(s_band, axis=-1, keepdims=True), jnp.max(s_m, axis=-1, keepdims=True)), sink)
    e_b, e_m, e_s = jnp.exp(s_band - m), jnp.exp(s_m - m), jnp.exp(sink - m)
    inv = 1.0 / (jnp.sum(e_b, axis=-1, keepdims=True) + jnp.sum(e_m, axis=-1, keepdims=True) + e_s)
    return e_b * inv, e_m * inv, e_s * inv


def _fold_band(tri, two):
    return jnp.where(tri, two[:, BLOCK:2 * BLOCK], two[:, 0:BLOCK])


def _unfold_band(tri, band):
    return jnp.concatenate([jnp.where(tri, 0.0, band), jnp.where(tri, band, 0.0)], axis=1)


def _meta_probs(qm, k_m, sink_m):
    R = qm.shape[0]
    s = lax.dot_general(qm, k_m, _NT, preferred_element_type=F32)
    qi = lax.broadcasted_iota(jnp.int32, (R, N_META), 0) & (N_META - 1)
    kj = lax.broadcasted_iota(jnp.int32, (R, N_META), 1)
    s = jnp.where(kj <= qi, s, NEG_INF)
    m = jnp.maximum(jnp.max(s, axis=-1, keepdims=True), sink_m)
    e, e_s = jnp.exp(s - m), jnp.exp(sink_m - m)
    inv = 1.0 / (jnp.sum(e, axis=-1, keepdims=True) + e_s)
    return e * inv, e_s * inv


def _block_start(n):
    return pl.multiple_of(N_META + n * BLOCK, ROW_ALIGN)


def _kv(blk):
    return blk[:, 0:HEAD_DIM], blk[:, HEAD_DIM:2 * HEAD_DIM]


def _attn_fwd(q, kv, sink_col, sink_meta, B, name, riders=None):
    T, D = q.shape
    L = T // B
    KV = D // QW
    nb = (L - N_META) // BLOCK

    def body(q_ref, kv_ref, sk_ref, skm_ref, o_ref, kvs):
        kvs[...] = kv_ref[...].astype(BF16)
        k_m, v_m = _kv(kvs[0:N_META, :])
        p, _ = _meta_probs(_heads_to_rows(q_ref[0:N_META, :]), k_m, skm_ref[0])
        o_ref[0:N_META, :] = _rows_to_heads(jnp.dot(p.astype(BF16), v_m, preferred_element_type=F32))
        tri = _causal(BLOCK)

        def block(cur, first, keys):
            k2, v2 = _kv(kvs[keys, :])
            qb = _heads_to_rows(q_ref[pl.ds(cur, BLOCK), :])
            s2 = lax.dot_general(qb, k2, _NT, preferred_element_type=F32)
            sm = lax.dot_general(qb, k_m, _NT, preferred_element_type=F32)
            p2s, pms = [], []
            for g in range(Q_PER_KV):
                sl = slice(g * BLOCK, (g + 1) * BLOCK)
                s_band = jnp.where(tri, s2[sl], NEG_INF) if first else _fold_band(tri, s2[sl])
                p_b, p_m, _ = _band_probs(s_band, sm[sl], sk_ref[0, sl, :])
                p2s.append((p_b if first else _unfold_band(tri, p_b)).astype(BF16))
                pms.append(p_m.astype(BF16))
            o = (jnp.dot(jnp.concatenate(p2s, axis=0), v2, preferred_element_type=F32)
                 + jnp.dot(jnp.concatenate(pms, axis=0), v_m, preferred_element_type=F32))
            o_ref[pl.ds(cur, BLOCK), :] = _rows_to_heads(o)

        block(N_META, True, pl.ds(N_META, BLOCK))

        def step(n, carry):
            block(_block_start(n), False, pl.ds(_block_start(n - 1), 2 * BLOCK))
            return carry

        lax.fori_loop(1, nb, step, 0)

    q_spec = pl.BlockSpec((L, QW), lambda b, h: (b, h))
    return _call(body, name, (B, KV),
                 [q_spec, pl.BlockSpec((L, 2 * HEAD_DIM), lambda b, h: (b, h)),
                  pl.BlockSpec((1, Q_PER_KV * BLOCK, 1), lambda b, h: (h, 0, 0)),
                  pl.BlockSpec((1, Q_PER_KV * N_META, 1), lambda b, h: (h, 0, 0))],
                 [q_spec], [jax.ShapeDtypeStruct((T, D), F32)], [pltpu.VMEM((L, 2 * HEAD_DIM), BF16)],
                 ("parallel", "parallel"), (q, kv, sink_col, sink_meta), riders)


def _attn_bwd(q, kv, o, do, sink_col, sink_meta, B, name, riders=None):
    T, D = q.shape
    L = T // B
    KV = D // QW
    nb = (L - N_META) // BLOCK
    R = Q_PER_KV * BLOCK
    scale = HEAD_DIM ** -0.5

    def head_totals(col, rows_per_head):
        rid = lax.broadcasted_iota(jnp.int32, (8, 128), 0)
        out = jnp.zeros((8, 128), F32)
        for g in range(Q_PER_KV):
            out = out + jnp.where(rid == g, jnp.sum(col[g * rows_per_head:(g + 1) * rows_per_head, :]), 0.0)
        return out

    def body(q_ref, kv_ref, o_ref, do_ref, sk_ref, skm_ref, dq_ref, dkv_ref, dsk_ref, kvs, acc, acc_sink):
        b = pl.program_id(1)
        kvs[...] = kv_ref[...].astype(BF16)
        acc[...] = jnp.zeros_like(acc)
        k_m, v_m = _kv(kvs[0:N_META, :])

        qm = _heads_to_rows(q_ref[0:N_META, :])
        dom = _heads_to_rows(do_ref[0:N_META, :])
        delta = jnp.sum(dom * _heads_to_rows(o_ref[0:N_META, :]), axis=-1, keepdims=True)
        p, p_s = _meta_probs(qm, k_m, skm_ref[0])
        domb = dom.astype(BF16)
        ds = (p * (lax.dot_general(domb, v_m, _NT, preferred_element_type=F32) - delta)).astype(BF16)
        dq_ref[0:N_META, :] = _rows_to_heads(jnp.dot(ds, k_m, preferred_element_type=F32) * scale).astype(BF16)
        acc[0:N_META, :] += jnp.concatenate([lax.dot_general(ds, qm, _TN, preferred_element_type=F32),
                                             lax.dot_general(p.astype(BF16), domb, _TN, preferred_element_type=F32)], axis=1)
        sink_tot = head_totals(-p_s * delta, N_META)
        tri = _causal(BLOCK)
        acc_sink[...] = jnp.zeros_like(acc_sink)

        def block(cur, first, keys):
            k2, v2 = _kv(kvs[keys, :])
            rows = pl.ds(cur, BLOCK)
            qb = _heads_to_rows(q_ref[rows, :])
            dob = _heads_to_rows(do_ref[rows, :])
            delta = jnp.sum(dob * _heads_to_rows(o_ref[rows, :]), axis=-1, keepdims=True)
            dobb = dob.astype(BF16)
            s2 = lax.dot_general(qb, k2, _NT, preferred_element_type=F32)
            sm = lax.dot_general(qb, k_m, _NT, preferred_element_type=F32)
            dp2 = lax.dot_general(dobb, v2, _NT, preferred_element_type=F32)
            dpm = lax.dot_general(dobb, v_m, _NT, preferred_element_type=F32)
            ds2s, p2s, dsms, pms = [], [], [], []
            for g in range(Q_PER_KV):
                sl = slice(g * BLOCK, (g + 1) * BLOCK)
                s_band = jnp.where(tri, s2[sl], NEG_INF) if first else _fold_band(tri, s2[sl])
                p_b, p_m, p_s = _band_probs(s_band, sm[sl], sk_ref[0, sl, :])
                ds_b = p_b * ((dp2[sl] if first else _fold_band(tri, dp2[sl])) - delta[sl])
                ds2s.append((ds_b if first else _unfold_band(tri, ds_b)).astype(BF16))
                p2s.append((p_b if first else _unfold_band(tri, p_b)).astype(BF16))
                dsms.append((p_m * (dpm[sl] - delta[sl])).astype(BF16))
                pms.append(p_m.astype(BF16))
                acc_sink[sl, :] += -p_s * delta[sl]
            ds2, p2 = jnp.concatenate(ds2s, axis=0), jnp.concatenate(p2s, axis=0)
            dsm, pm = jnp.concatenate(dsms, axis=0), jnp.concatenate(pms, axis=0)
            dq = jnp.dot(ds2, k2, preferred_element_type=F32) + jnp.dot(dsm, k_m, preferred_element_type=F32)
            dq_ref[rows, :] = _rows_to_heads(dq * scale).astype(BF16)
            acc[keys, :] += jnp.concatenate([lax.dot_general(ds2, qb, _TN, preferred_element_type=F32),
                                             lax.dot_general(p2, dobb, _TN, preferred_element_type=F32)], axis=1)
            acc[0:N_META, :] += jnp.concatenate([lax.dot_general(dsm, qb, _TN, preferred_element_type=F32),
                                                 lax.dot_general(pm, dobb, _TN, preferred_element_type=F32)], axis=1)

        block(N_META, True, pl.ds(N_META, BLOCK))

        def step(n, carry):
            block(_block_start(n), False, pl.ds(_block_start(n - 1), 2 * BLOCK))
            return carry

        lax.fori_loop(1, nb, step, 0)
        dkv_ref[...] = acc[...].astype(BF16)
        tot = sink_tot + head_totals(acc_sink[...], BLOCK)

        @pl.when(b == 0)
        def _():
            dsk_ref[0] = tot

        @pl.when(b > 0)
        def _():
            dsk_ref[0] += tot

    q_spec = pl.BlockSpec((L, QW), lambda h, b: (b, h))
    kv_spec = pl.BlockSpec((L, 2 * HEAD_DIM), lambda h, b: (b, h))
    return _call(body, name, (KV, B),
                 [q_spec, kv_spec, q_spec, q_spec,
                  pl.BlockSpec((1, R, 1), lambda h, b: (h, 0, 0)),
                  pl.BlockSpec((1, Q_PER_KV * N_META, 1), lambda h, b: (h, 0, 0))],
                 [q_spec, kv_spec, pl.BlockSpec((1, 8, 128), lambda h, b: (h, 0, 0))],
                 [jax.ShapeDtypeStruct((T, D), BF16), jax.ShapeDtypeStruct((T, KV * 2 * HEAD_DIM), BF16),
                  jax.ShapeDtypeStruct((KV, 8, 128), F32)],
                 [pltpu.VMEM((L, 2 * HEAD_DIM), BF16), pltpu.VMEM((L, 2 * HEAD_DIM), F32), pltpu.VMEM((R, 1), F32)],
                 ("parallel", "arbitrary"), (q, kv, o, do, sink_col, sink_meta), riders)


def _cmul_add(acc_r, acc_i, lr, li, xr, xi):
    return acc_r + (lr * xr - li * xi), acc_i + (lr * xi + li * xr)


def _cols_per_step(ncol):
    for cps in (4, 2):
        if ncol % cps == 0:
            return cps
    return 1


def _ssm_fwd(u, bmat, cmat, dskip, tables, nbatch, rc, name):
    T, W = u.shape
    ncol = W // SSM_LANES
    nch = T // rc
    S = STATE_LANES
    cps = _cols_per_step(ncol)
    assert nbatch == 4

    def body(u_ref, b_ref, c_ref, d_ref, tab_ref, y_ref, xs_ref, xp_ref, st_ref, carry_ref):
        ch = pl.program_id(1)

        @pl.when(ch == 0)
        def _():
            carry_ref[...] = jnp.zeros_like(carry_ref)

        uv = u_ref[...]
        for k in range(cps):
            st_ref[:, 2 * S * k:2 * S * (k + 1)] = jnp.dot(uv[:, SSM_LANES * k:SSM_LANES * (k + 1)].astype(BF16), b_ref[k],
                                                           preferred_element_type=F32)
        low = lax.broadcasted_iota(jnp.int32, (8, S), 0) < nbatch

        def tile(k, r0, c_r, c_i):
            re, im = slice(2 * S * k, 2 * S * k + S), slice(2 * S * k + S, 2 * S * (k + 1))
            la_r, la_i = tab_ref[k, :, 0:S], tab_ref[k, :, S:2 * S]
            lb_r, lb_i = tab_ref[k, :, 2 * S:3 * S], tab_ref[k, :, 3 * S:4 * S]
            v_r = st_ref[pl.ds(r0, 8), re]
            v_i = st_ref[pl.ds(r0, 8), im]
            v_r, v_i = _cmul_add(v_r, v_i, la_r, la_i, pltpu.roll(v_r, nbatch, 0), pltpu.roll(v_i, nbatch, 0))
            rc_r, rc_i = pltpu.roll(c_r, nbatch, 0), pltpu.roll(c_i, nbatch, 0)
            cb_r, cb_i = jnp.where(low, rc_r, c_r), jnp.where(low, rc_i, c_i)
            v_r, v_i = _cmul_add(v_r, v_i, lb_r, lb_i, cb_r, cb_i)
            st_ref[pl.ds(r0, 8), re] = v_r
            st_ref[pl.ds(r0, 8), im] = v_i
            p_r = jnp.where(low, rc_r, pltpu.roll(v_r, nbatch, 0))
            p_i = jnp.where(low, rc_i, pltpu.roll(v_i, nbatch, 0))
            return v_r, v_i, p_r, p_i

        def step(i, carry):
            r0 = pl.multiple_of(i * 16, 16)
            out = []
            for k in range(cps):
                re, im = slice(2 * S * k, 2 * S * k + S), slice(2 * S * k + S, 2 * S * (k + 1))
                a_r, a_i, pa_r, pa_i = tile(k, r0, carry[2 * k], carry[2 * k + 1])
                b_r, b_i, pb_r, pb_i = tile(k, r0 + 8, a_r, a_i)
                xp_ref[pl.ds(r0, 16), re] = jnp.concatenate([pa_r, pb_r], axis=0).astype(BF16)
                xp_ref[pl.ds(r0, 16), im] = jnp.concatenate([pa_i, pb_i], axis=0).astype(BF16)
                out += [b_r, b_i]
            return tuple(out)

        halves = tuple(carry_ref[:, S * j:S * (j + 1)] for j in range(2 * cps))
        halves = lax.fori_loop(0, rc // 16, step, halves)
        for j in range(2 * cps):
            carry_ref[:, S * j:S * (j + 1)] = halves[j]
        xb = st_ref[...].astype(BF16)
        xs_ref[...] = xb
        for k in range(cps):
            cols = slice(SSM_LANES * k, SSM_LANES * (k + 1))
            y_ref[:, cols] = (jnp.dot(xb[:, 2 * S * k:2 * S * (k + 1)], c_ref[k], preferred_element_type=F32)
                              + d_ref[:, cols] * uv[:, cols])

    return pl.pallas_call(
        body, name=name, grid=(ncol // cps, nch),
        in_specs=[pl.BlockSpec((rc, cps * SSM_LANES), lambda g, c: (c, g)),
                  pl.BlockSpec((cps, SSM_LANES, 2 * S), lambda g, c: (g, 0, 0)),
                  pl.BlockSpec((cps, 2 * S, SSM_LANES), lambda g, c: (g, 0, 0)),
                  pl.BlockSpec((1, cps * SSM_LANES), lambda g, c: (0, g)),
                  pl.BlockSpec((cps, 8, 4 * S), lambda g, c: (g, 0, 0))],
        out_specs=[pl.BlockSpec((rc, cps * SSM_LANES), lambda g, c: (c, g)),
                   pl.BlockSpec((rc, cps * 2 * S), lambda g, c: (c, g)),
                   pl.BlockSpec((rc, cps * 2 * S), lambda g, c: (c, g))],
        out_shape=[jax.ShapeDtypeStruct((T, W), F32), jax.ShapeDtypeStruct((T, ncol * 2 * S), BF16),
                   jax.ShapeDtypeStruct((T, ncol * 2 * S), BF16)],
        scratch_shapes=[pltpu.VMEM((rc, cps * 2 * S), F32), pltpu.VMEM((8, cps * 2 * S), F32)],
        compiler_params=_params(("parallel", "arbitrary")),
    )(u, bmat, cmat, dskip, tables)


def _ssm_bwd(dy, u, xs, xp, bmat, cmat, dskip, tables, nbatch, rc, name):
    T, W = u.shape
    ncol = W // SSM_LANES
    nch = T // rc
    S = STATE_LANES
    ntile = rc // 16
    cps = _cols_per_step(ncol)

    def body(dy_ref, u_ref, xs_ref, xp_ref, b_ref, c_ref, d_ref, tab_ref,
             du_ref, db_ref, dc_ref, dl_ref, dd_ref, st_ref, carry_ref, accl_ref, accd_ref):
        ch = pl.program_id(1)

        @pl.when(ch == 0)
        def _():
            carry_ref[...] = jnp.zeros_like(carry_ref)
            accl_ref[...] = jnp.zeros_like(accl_ref)
            accd_ref[...] = jnp.zeros_like(accd_ref)
            db_ref[...] = jnp.zeros_like(db_ref)
            dc_ref[...] = jnp.zeros_like(dc_ref)

        dyv = dy_ref[...]
        uv = u_ref[...]
        dyb = dyv.astype(BF16)
        for k in range(cps):
            st_ref[:, 2 * S * k:2 * S * (k + 1)] = lax.dot_general(dyb[:, SSM_LANES * k:SSM_LANES * (k + 1)], c_ref[k], _NT,
                                                                   preferred_element_type=F32)
        low = lax.broadcasted_iota(jnp.int32, (8, S), 0) < nbatch

        def tile(k, r0, p_r, p_i, c_r, c_i, al_r, al_i):
            re, im = slice(2 * S * k, 2 * S * k + S), slice(2 * S * k + S, 2 * S * (k + 1))
            la_r, la_i = tab_ref[k, :, 0:S], tab_ref[k, :, S:2 * S]
            lb_r, lb_i = tab_ref[k, :, 2 * S:3 * S], tab_ref[k, :, 3 * S:4 * S]
            v_r = st_ref[pl.ds(r0, 8), re]
            v_i = st_ref[pl.ds(r0, 8), im]
            v_r, v_i = _cmul_add(v_r, v_i, la_r, la_i, pltpu.roll(v_r, nbatch, 0), pltpu.roll(v_i, nbatch, 0))
            cb_r = jnp.where(low, c_r, pltpu.roll(c_r, nbatch, 0))
            cb_i = jnp.where(low, c_i, pltpu.roll(c_i, nbatch, 0))
            v_r, v_i = _cmul_add(v_r, v_i, lb_r, lb_i, cb_r, cb_i)
            st_ref[pl.ds(r0, 8), re] = v_r
            st_ref[pl.ds(r0, 8), im] = v_i
            al_r = al_r + (v_r * p_r + v_i * p_i)
            al_i = al_i + (v_i * p_r - v_r * p_i)
            return v_r, v_i, al_r, al_i

        def step(j, carry):
            r0 = pl.multiple_of((ntile - 1 - j) * 16, 16)
            out = []
            for k in range(cps):
                re, im = slice(2 * S * k, 2 * S * k + S), slice(2 * S * k + S, 2 * S * (k + 1))
                p_r = xp_ref[pl.ds(r0, 16), re].astype(F32)
                p_i = xp_ref[pl.ds(r0, 16), im].astype(F32)
                mid = tile(k, r0 + 8, p_r[8:16], p_i[8:16], *carry[4 * k:4 * k + 4])
                out += list(tile(k, r0, p_r[0:8], p_i[0:8], *mid))
            return tuple(out)

        init = []
        for k in range(cps):
            init += [carry_ref[:, 2 * S * k:2 * S * k + S], carry_ref[:, 2 * S * k + S:2 * S * (k + 1)],
                     accl_ref[:, 2 * S * k:2 * S * k + S], accl_ref[:, 2 * S * k + S:2 * S * (k + 1)]]
        fin = lax.fori_loop(0, ntile, step, tuple(init))
        for k in range(cps):
            carry_ref[:, 2 * S * k:2 * S * k + S] = fin[4 * k]
            carry_ref[:, 2 * S * k + S:2 * S * (k + 1)] = fin[4 * k + 1]
            accl_ref[:, 2 * S * k:2 * S * k + S] = fin[4 * k + 2]
            accl_ref[:, 2 * S * k + S:2 * S * (k + 1)] = fin[4 * k + 3]
        dsb = st_ref[...].astype(BF16)
        ub = uv.astype(BF16)
        for k in range(cps):
            cols, lanes = slice(SSM_LANES * k, SSM_LANES * (k + 1)), slice(2 * S * k, 2 * S * (k + 1))
            du_ref[:, cols] = (lax.dot_general(dsb[:, lanes], b_ref[k], _NT, preferred_element_type=F32)
                               + d_ref[:, cols] * dyv[:, cols])
            db_ref[k] += lax.dot_general(ub[:, cols], dsb[:, lanes], _TN, preferred_element_type=F32)
            dc_ref[k] += lax.dot_general(xs_ref[:, lanes], dyb[:, cols], _TN, preferred_element_type=F32)
        accd_ref[...] += _fold8(dyv * uv)

        @pl.when(ch == nch - 1)
        def _():
            for k in range(cps):
                dl_ref[k] = jnp.sum(accl_ref[:, 2 * S * k:2 * S * (k + 1)], axis=0, keepdims=True)
            dd_ref[...] = jnp.sum(accd_ref[...], axis=0, keepdims=True)

    rev = lambda g, c: (nch - 1 - c, g)
    return pl.pallas_call(
        body, name=name, grid=(ncol // cps, nch),
        in_specs=[pl.BlockSpec((rc, cps * SSM_LANES), rev), pl.BlockSpec((rc, cps * SSM_LANES), rev),
                  pl.BlockSpec((rc, cps * 2 * S), rev), pl.BlockSpec((rc, cps * 2 * S), rev),
                  pl.BlockSpec((cps, SSM_LANES, 2 * S), lambda g, c: (g, 0, 0)),
                  pl.BlockSpec((cps, 2 * S, SSM_LANES), lambda g, c: (g, 0, 0)),
                  pl.BlockSpec((1, cps * SSM_LANES), lambda g, c: (0, g)),
                  pl.BlockSpec((cps, 8, 4 * S), lambda g, c: (g, 0, 0))],
        out_specs=[pl.BlockSpec((rc, cps * SSM_LANES), rev),
                   pl.BlockSpec((cps, SSM_LANES, 2 * S), lambda g, c: (g, 0, 0)),
                   pl.BlockSpec((cps, 2 * S, SSM_LANES), lambda g, c: (g, 0, 0)),
                   pl.BlockSpec((cps, 1, 2 * S), lambda g, c: (g, 0, 0)),
                   pl.BlockSpec((1, cps * SSM_LANES), lambda g, c: (0, g))],
        out_shape=[jax.ShapeDtypeStruct((T, W), F32),
                   jax.ShapeDtypeStruct((ncol, SSM_LANES, 2 * S), F32),
                   jax.ShapeDtypeStruct((ncol, 2 * S, SSM_LANES), F32),
                   jax.ShapeDtypeStruct((ncol, 1, 2 * S), F32),
                   jax.ShapeDtypeStruct((1, W), F32)],
        scratch_shapes=[pltpu.VMEM((rc, cps * 2 * S), F32), pltpu.VMEM((8, cps * 2 * S), F32),
                        pltpu.VMEM((8, cps * 2 * S), F32), pltpu.VMEM((8, cps * SSM_LANES), F32)],
        compiler_params=_params(("parallel", "arbitrary")),
    )(dy, u, xs, xp, bmat, cmat, dskip, tables)


def _ssm_matrices(a_re, a_im, log_step, b_re, b_im, c_re, c_im):
    G, N = a_re.shape
    ncol = G // GROUPS_PER_COL
    step = jnp.exp(log_step)[:, None]
    mag = jnp.exp(a_re * step)
    ang = a_im * step
    lam_re, lam_im = mag * jnp.cos(ang), mag * jnp.sin(ang)
    den = a_re * a_re + a_im * a_im
    nr, ni = lam_re - 1.0, lam_im
    coef_re = (nr * a_re + ni * a_im) / den
    coef_im = (ni * a_re - nr * a_im) / den
    bb_re = coef_re[..., None] * b_re - coef_im[..., None] * b_im
    bb_im = coef_re[..., None] * b_im + coef_im[..., None] * b_re
    eye = jnp.eye(GROUPS_PER_COL, dtype=F32)
    bb = jnp.stack([bb_re, bb_im]).reshape(2, ncol, GROUPS_PER_COL, N, SSM_GROUP)
    bmat = jnp.einsum("pbgnc,gh->bgcphn", bb, eye).reshape(ncol, SSM_LANES, 2 * STATE_LANES)
    cc = jnp.stack([c_re, -c_im]).reshape(2, ncol, GROUPS_PER_COL, SSM_GROUP, N)
    cmat = jnp.einsum("pbgcn,gh->bpgnhc", cc, eye).reshape(ncol, 2 * STATE_LANES, SSM_LANES)
    lam = jnp.concatenate([lam_re.reshape(ncol, STATE_LANES), lam_im.reshape(ncol, STATE_LANES)], axis=-1)
    return lam, bmat, cmat


def _scan_tables(lam, nbatch, conj):
    S = STATE_LANES
    lr, li = lam[:, None, 0:S], lam[:, None, S:2 * S]
    if conj:
        li = -li
    l2r, l2i = lr * lr - li * li, 2.0 * lr * li
    first = (jnp.arange(8) < nbatch)[None, :, None]
    zero = jnp.zeros_like(lr)
    if conj:
        parts = [jnp.where(first, lr, zero), jnp.where(first, li, zero), jnp.where(first, l2r, lr), jnp.where(first, l2i, li)]
    else:
        parts = [jnp.where(first, zero, lr), jnp.where(first, zero, li), jnp.where(first, lr, l2r), jnp.where(first, li, l2i)]
    return jnp.concatenate([jnp.broadcast_to(p, (lam.shape[0], 8, S)) for p in parts], axis=-1)


def _adamw_update(w_ref, g_ref, m_ref, v_ref, d_ref, nm_ref, nv_ref):
    gv = g_ref[...]
    mn = ADAM_B1 * m_ref[...] + (1.0 - ADAM_B1) * gv
    vn = ADAM_B2 * v_ref[...] + (1.0 - ADAM_B2) * (gv * gv)
    m_hat = mn / (1.0 - ADAM_B1 ** ADAM_STEP)
    v_hat = vn / (1.0 - ADAM_B2 ** ADAM_STEP)
    d_ref[...] = -ADAM_LR * (m_hat / (jnp.sqrt(v_hat) + ADAM_EPS) + ADAM_WD * w_ref[...])
    nm_ref[...] = mn
    nv_ref[...] = vn


def _adamw_small(ws, gs, ms, vs, name):
    n = len(ws)

    def body(*refs):
        for i in range(n):
            _adamw_update(refs[i], refs[n + i], refs[2 * n + i], refs[3 * n + i],
                          refs[4 * n + i], refs[5 * n + i], refs[6 * n + i])

    vm = pl.BlockSpec(memory_space=pltpu.VMEM)
    shapes = [jax.ShapeDtypeStruct(a.shape, F32) for a in ws]
    outs = pl.pallas_call(body, name=name, in_specs=[vm] * (4 * n), out_specs=[vm] * (3 * n), out_shape=shapes * 3,
                          compiler_params=pltpu.CompilerParams(vmem_limit_bytes=VMEM_LIMIT))(*ws, *gs, *ms, *vs)
    return outs[:n], outs[n:2 * n], outs[2 * n:]


def _adamw(w, g, m, v, name):
    R, C = w.shape
    tr = R if R <= 512 else _pick_tile(R, 512, 8)
    body = functools.partial(_adamw_update)

    spec = pl.BlockSpec((tr, C), lambda i: (i, 0))
    shp = jax.ShapeDtypeStruct((R, C), F32)
    return pl.pallas_call(body, name=name, grid=(R // tr,), in_specs=[spec] * 4, out_specs=[spec] * 3,
                          out_shape=[shp, shp, shp], compiler_params=_params(("parallel",)))(w, g, m, v)


_ANY = pl.BlockSpec(memory_space=pl.ANY)


def _place():
    x, y, c = lax.axis_index("x"), lax.axis_index("y"), lax.axis_index("c")
    chips = [(1 - x, y), (x, 1 - y), (1 - x, 1 - y)]
    return x, y, c, chips


def _remote(src, dst, send_sems, recv_sems, k, to):
    return pltpu.make_async_remote_copy(src_ref=src, dst_ref=dst, send_sem=send_sems.at[k], recv_sem=recv_sems.at[k],
                                        device_id=to, device_id_type=MESH_IDS)


class _Riders:
    def __init__(self, srcs, out_shapes, n_sems, copies):
        self.srcs, self.out_shapes, self.n_sems, self.copies = list(srcs), list(out_shapes), n_sems, copies


def _call(body, name, grid, in_specs, out_specs, out_shape, scratch_shapes, sem, args, riders=None):
    if riders is None:
        return pl.pallas_call(body, name=name, grid=grid, in_specs=in_specs, out_specs=out_specs, out_shape=out_shape,
                              scratch_shapes=scratch_shapes, compiler_params=_params(sem))(*args)
    n_in, n_out, n_scr = len(in_specs), len(out_specs), len(scratch_shapes)
    r_in, r_out = len(riders.srcs), len(riders.out_shapes)

    def carrying(*refs):
        a, b = n_in, n_in + r_in
        c, d = b + n_out, b + n_out + r_out
        e = d + n_scr
        sends, arrivals = riders.copies(refs[a:b], refs[c:d], refs[e], refs[e + 1])
        first, last = None, None
        for ax, size in enumerate(grid):
            at0, at1 = pl.program_id(ax) == 0, pl.program_id(ax) == size - 1
            first = at0 if first is None else first & at0
            last = at1 if last is None else last & at1

        @pl.when(first)
        def _():
            for cp in sends:
                cp.start()

        body(*refs[:a], *refs[b:c], *refs[d:e])

        @pl.when(last)
        def _():
            for cp in arrivals:
                cp.wait_recv()
            for cp in sends:
                cp.wait_send()

    outs = pl.pallas_call(
        carrying, name=name, grid=grid, in_specs=list(in_specs) + [_ANY] * r_in,
        out_specs=list(out_specs) + [_ANY] * r_out, out_shape=list(out_shape) + riders.out_shapes,
        scratch_shapes=list(scratch_shapes) + [pltpu.SemaphoreType.DMA((riders.n_sems,)),
                                               pltpu.SemaphoreType.DMA((riders.n_sems,))],
        compiler_params=pltpu.CompilerParams(dimension_semantics=("arbitrary",) * len(grid),
                                             vmem_limit_bytes=VMEM_LIMIT, has_side_effects=True),
    )(*args, *riders.srcs)
    return outs[:n_out], outs[n_out:]


def _gather_riders(shards):
    def copies(srcs, outs, send_sems, recv_sems):
        x, y, c, chips = _place()
        sends, arrivals = [], []
        for i, s in enumerate(shards):
            half = s.shape[0] // 2
            rows = pl.ds(c * half, half)
            for j, chip in enumerate(chips):
                sends.append(_remote(srcs[i].at[rows, :], outs[i].at[2 * x + y, rows, :], send_sems, recv_sems,
                                     3 * i + j, (*chip, c)))
                landed = outs[i].at[2 * chip[0] + chip[1], rows, :]
                arrivals.append(_remote(landed, landed, send_sems, recv_sems, 3 * i + j, (*chip, c)))
        return sends, arrivals

    return _Riders(shards, [jax.ShapeDtypeStruct((N_CHIPS,) + s.shape, s.dtype) for s in shards], 3 * len(shards), copies)


def _exchange_riders(parts):
    def copies(srcs, outs, send_sems, recv_sems):
        x, y, c, chips = _place()
        sends = [_remote(srcs[i].at[2 * chip[0] + chip[1]], outs[i].at[j], send_sems, recv_sems, 3 * i + j, (*chip, c))
                 for i in range(len(parts)) for j, chip in enumerate(chips)]
        return sends, sends

    return _Riders(parts, [jax.ShapeDtypeStruct((3,) + p.shape[1:], p.dtype) for p in parts], 3 * len(parts), copies)


def _forward_halves(gathered, shards, tag):
    n = len(gathered)

    def body(*refs):
        srcs, outs = refs[:n], refs[n:2 * n]
        send_sems, recv_sems = refs[2 * n:]
        x, y, c, chips = _place()
        sibling = (x, y, 1 - c)
        cps = []
        for i in range(n):
            half = gathered[i].shape[1] // 2
            for j, chip in enumerate(chips):
                slot = 2 * chip[0] + chip[1]
                cps.append(_remote(srcs[i].at[slot, pl.ds(c * half, half), :], outs[i].at[slot, pl.ds(c * half, half), :],
                                   send_sems, recv_sems, 3 * i + j, sibling))
        for cp in cps:
            cp.start()
        for i in range(n):
            half = gathered[i].shape[1] // 2
            for j, chip in enumerate(chips):
                theirs = outs[i].at[2 * chip[0] + chip[1], pl.ds((1 - c) * half, half), :]
                _remote(theirs, theirs, send_sems, recv_sems, 3 * i + j, sibling).wait_recv()
        for cp in cps:
            cp.wait_send()

    outs = pl.pallas_call(
        body, name=f"gather_forward_{tag}", in_specs=[_ANY] * n, out_specs=[_ANY] * n,
        out_shape=[jax.ShapeDtypeStruct(g.shape, g.dtype) for g in gathered],
        input_output_aliases={i: i for i in range(n)},
        scratch_shapes=[pltpu.SemaphoreType.DMA((3 * n,)), pltpu.SemaphoreType.DMA((3 * n,))],
        compiler_params=pltpu.CompilerParams(has_side_effects=True),
    )(*gathered)
    slot = 2 * lax.axis_index("x") + lax.axis_index("y")
    return [lax.dynamic_update_slice(o, s[None], (slot, 0, 0)) for o, s in zip(outs, shards)]


def _gather_weights(shards):
    n = len(shards)

    def body(*refs):
        srcs, outs = refs[:n], refs[n:2 * n]
        send_sems, recv_sems = refs[2 * n:]
        x, y, c, chips = _place()
        sibling = (x, y, 1 - c)

        def piece(i, px, py, pc):
            half = shards[i].shape[0] // 2
            return outs[i].at[2 * px + py, pl.ds(pc * half, half), :]

        first = []
        for i in range(n):
            half = shards[i].shape[0] // 2
            for j, chip in enumerate(chips):
                first.append(_remote(srcs[i].at[pl.ds(c * half, half), :], piece(i, x, y, c), send_sems, recv_sems,
                                     6 * i + j, (*chip, c)))
        for cp in first:
            cp.start()
        passed = []
        for i in range(n):
            for j, chip in enumerate(chips):
                _remote(piece(i, *chip, c), piece(i, *chip, c), send_sems, recv_sems, 6 * i + j, (*chip, c)).wait_recv()
                cp = _remote(piece(i, *chip, c), piece(i, *chip, c), send_sems, recv_sems, 6 * i + 3 + j, sibling)
                cp.start()
                passed.append(cp)
        for i in range(n):
            for j, chip in enumerate(chips):
                _remote(piece(i, *chip, 1 - c), piece(i, *chip, 1 - c), send_sems, recv_sems, 6 * i + 3 + j,
                        sibling).wait_recv()
        for cp in first + passed:
            cp.wait_send()

    outs = pl.pallas_call(
        body, name="gather_weights", in_specs=[_ANY] * n, out_specs=[_ANY] * n,
        out_shape=[jax.ShapeDtypeStruct((N_CHIPS,) + s.shape, s.dtype) for s in shards],
        scratch_shapes=[pltpu.SemaphoreType.DMA((6 * n,)), pltpu.SemaphoreType.DMA((6 * n,))],
        compiler_params=pltpu.CompilerParams(has_side_effects=True),
    )(*shards)
    slot = 2 * lax.axis_index("x") + lax.axis_index("y")
    return [lax.dynamic_update_slice(o, s[None], (slot, 0, 0)) for o, s in zip(outs, shards)]


def _swap_halves(grads, tag):
    n = len(grads)

    def body(*refs):
        srcs, outs = refs[:n], refs[n:2 * n]
        send_sems, recv_sems = refs[2 * n:]
        x, y, c, _ = _place()
        cps = []
        for i in range(n):
            half = grads[i].shape[1] // 2
            cps.append(_remote(srcs[i].at[:, pl.ds((1 - c) * half, half), :], outs[i], send_sems, recv_sems, i, (x, y, 1 - c)))
        for cp in cps:
            cp.start()
        for cp in cps:
            cp.wait()

    return pl.pallas_call(
        body, name=f"grad_swap_halves_{tag}", in_specs=[_ANY] * n, out_specs=[_ANY] * n,
        out_shape=[jax.ShapeDtypeStruct((N_CHIPS, g.shape[1] // 2, g.shape[2]), g.dtype) for g in grads],
        scratch_shapes=[pltpu.SemaphoreType.DMA((n,)), pltpu.SemaphoreType.DMA((n,))],
        compiler_params=pltpu.CompilerParams(has_side_effects=True),
    )(*grads)


def _exchange_chips(parts):
    n = len(parts)

    def body(*refs):
        srcs, outs = refs[:n], refs[n:2 * n]
        send_sems, recv_sems = refs[2 * n:]
        x, y, c, chips = _place()
        cps = [_remote(srcs[i].at[2 * chip[0] + chip[1]], outs[i].at[j], send_sems, recv_sems, 3 * i + j, (*chip, c))
               for i in range(n) for j, chip in enumerate(chips)]
        for cp in cps:
            cp.start()
        for cp in cps:
            cp.wait()

    return pl.pallas_call(
        body, name="grad_exchange_chips", in_specs=[_ANY] * n, out_specs=[_ANY] * n,
        out_shape=[jax.ShapeDtypeStruct((3,) + p.shape[1:], p.dtype) for p in parts],
        scratch_shapes=[pltpu.SemaphoreType.DMA((3 * n,)), pltpu.SemaphoreType.DMA((3 * n,))],
        compiler_params=pltpu.CompilerParams(has_side_effects=True),
    )(*parts)


def _join_halves(fulls):
    n = len(fulls)

    def body(*refs):
        srcs, outs = refs[:n], refs[n:2 * n]
        send_sems, recv_sems = refs[2 * n:]
        x, y, c, _ = _place()
        sibling = (x, y, 1 - c)
        cps = []
        for i in range(n):
            h = fulls[i].shape[0] // 2
            cps.append(_remote(srcs[i].at[pl.ds(c * h, h), :], outs[i].at[pl.ds(c * h, h), :], send_sems, recv_sems, i,
                               sibling))
        for cp in cps:
            cp.start()
        for i in range(n):
            h = fulls[i].shape[0] // 2
            theirs = outs[i].at[pl.ds((1 - c) * h, h), :]
            _remote(theirs, theirs, send_sems, recv_sems, i, sibling).wait_recv()
        for cp in cps:
            cp.wait_send()

    return pl.pallas_call(
        body, name="grad_join_halves", in_specs=[_ANY] * n, out_specs=[_ANY] * n,
        out_shape=[jax.ShapeDtypeStruct(f.shape, f.dtype) for f in fulls],
        input_output_aliases={i: i for i in range(n)},
        scratch_shapes=[pltpu.SemaphoreType.DMA((n,)), pltpu.SemaphoreType.DMA((n,))],
        compiler_params=pltpu.CompilerParams(has_side_effects=True),
    )(*fulls)


def _half_tile(h):
    return h if h <= 512 else _pick_tile(h, 512, ROW_ALIGN)


def _sum_halves(g, r1, c_idx, name):
    _, R, C = g.shape
    H = R // 2
    tr = _half_tile(H)
    nblk = H // tr

    def body(c_ref, g_ref, r_ref, p_ref):
        p_ref[...] = (g_ref[...] + r_ref[...]).astype(BF16)

    half = pl.BlockSpec((None, tr, C), lambda s, i, c_ref: (s, c_ref[0] * nblk + i, 0))
    plain = pl.BlockSpec((None, tr, C), lambda s, i, c_ref: (s, i, 0))
    return pl.pallas_call(
        body, name=name,
        grid_spec=pltpu.PrefetchScalarGridSpec(num_scalar_prefetch=1, grid=(N_CHIPS, nblk), in_specs=[half, plain],
                                               out_specs=plain),
        out_shape=jax.ShapeDtypeStruct((N_CHIPS, H, C), BF16),
        compiler_params=_params(("parallel", "parallel")),
    )(c_idx, g, r1)


def _sum_chips(g, r1, r2, idx, name):
    _, R, C = g.shape
    H = R // 2
    tr = _half_tile(H)
    nblk = H // tr

    def body(idx_ref, g_ref, r1_ref, r2_ref, o_ref):
        o_ref[...] = (((g_ref[...] + r1_ref[...]) + r2_ref[0].astype(F32)) + r2_ref[1].astype(F32)) + r2_ref[2].astype(F32)

    return pl.pallas_call(
        body, name=name,
        grid_spec=pltpu.PrefetchScalarGridSpec(
            num_scalar_prefetch=1, grid=(nblk,),
            in_specs=[pl.BlockSpec((None, tr, C), lambda i, idx_ref: (idx_ref[0], idx_ref[1] * nblk + i, 0)),
                      pl.BlockSpec((None, tr, C), lambda i, idx_ref: (idx_ref[0], i, 0)),
                      pl.BlockSpec((3, tr, C), lambda i, idx_ref: (0, i, 0))],
            out_specs=pl.BlockSpec((tr, C), lambda i, idx_ref: (idx_ref[1] * nblk + i, 0))),
        out_shape=jax.ShapeDtypeStruct((R, C), F32),
        compiler_params=_params(("parallel",)),
    )(idx, g, r1, r2)


def _all_reduce_small(v, n_fold, fold_rows, fold_at):
    M, N = v.shape

    def body(x_ref, tot_ref, fold_ref, all_ref, send_sems, recv_sems, local_sem):
        x, y, c, chips = _place()
        me, sibling = (x, y, c), (x, y, 1 - c)

        def rows(px, py, pc):
            return all_ref.at[pl.ds((4 * px + 2 * py + pc) * M, M), :]

        def copy(k, block, to, src=None):
            return _remote(rows(*block) if src is None else src, rows(*block), send_sems, recv_sems, k, to)

        mine = pltpu.make_async_copy(x_ref, rows(*me), local_sem)
        mine.start()
        first = [copy(0, me, sibling, src=x_ref)]
        first += [copy(1 + j, me, (*chip, c), src=x_ref) for j, chip in enumerate(chips)]
        for cp in first:
            cp.start()
        passed = [copy(4 + j, (*chip, c), sibling) for j, chip in enumerate(chips)]
        for j, chip in enumerate(chips):
            copy(1 + j, (*chip, c), me).wait_recv()
            passed[j].start()
        copy(0, sibling, me).wait_recv()
        for j, chip in enumerate(chips):
            copy(4 + j, (*chip, 1 - c), me).wait_recv()
        for cp in first + passed:
            cp.wait_send()
        mine.wait()
        tot = all_ref[0:M, :]
        for d in range(1, 8):
            tot = tot + all_ref[d * M:(d + 1) * M, :]
        tot_ref[...] = tot
        f = tot[fold_at:fold_at + fold_rows, :]
        for e in range(1, n_fold):
            f = f + tot[fold_at + e * fold_rows:fold_at + (e + 1) * fold_rows, :]
        fold_ref[...] = f

    vm = pl.BlockSpec(memory_space=pltpu.VMEM)
    return pl.pallas_call(
        body, name="all_reduce_small", in_specs=[vm], out_specs=[vm, vm],
        out_shape=[jax.ShapeDtypeStruct((M, N), F32), jax.ShapeDtypeStruct((fold_rows, N), F32)],
        scratch_shapes=[pltpu.VMEM((8 * M, N), F32), pltpu.SemaphoreType.DMA((7,)), pltpu.SemaphoreType.DMA((7,)),
                        pltpu.SemaphoreType.DMA],
        compiler_params=pltpu.CompilerParams(has_side_effects=True, vmem_limit_bytes=VMEM_LIMIT),
    )(v)


def _as_rows(a, width):
    flat = a.reshape(-1)
    pad = (-flat.shape[0]) % width
    if pad:
        flat = jnp.concatenate([flat, jnp.zeros((pad,), flat.dtype)])
    return flat.reshape(-1, width)


class _Layout:
    def __init__(self, width, total_mult):
        self.width, self.total_mult = width, total_mult
        self.offsets, self.shapes, self.rows = {}, {}, 0

    def add(self, name, shape):
        r = -(-math.prod(shape) // self.width)
        self.offsets[name], self.shapes[name] = (self.rows, r), tuple(shape)
        self.rows += r

    def align(self, mult):
        gap = (-self.rows) % mult
        if gap:
            self.offsets[f"_gap{self.rows}"], self.shapes[f"_gap{self.rows}"] = (self.rows, gap), (gap, self.width)
            self.rows += gap
        return self.rows

    def pack(self, pieces):
        self.align(self.total_mult)
        parts = [_as_rows(pieces[n].astype(F32), self.width) if n in pieces else jnp.zeros(self.shapes[n], F32)
                 for n in self.offsets]
        return jnp.concatenate(parts, axis=0)

    def unpack(self, buf, name):
        off, r = self.offsets[name]
        shape = self.shapes[name]
        return buf[off:off + r].reshape(-1)[:math.prod(shape)].reshape(shape)


_BIG = ["ffn1_w1", "ffn1_w3", "ffn1_w2", "w_in", "ssm_glu_a", "ssm_glu_b", "w_out", "ffn2_w1", "ffn2_w3", "ffn2_w2"]
_SMALL = ["ffn1_norm", "mix_norm", "ffn2_norm", "final_norm", "attn_sinks", "ssm_a_re", "ssm_a_im", "ssm_log_step",
          "ssm_b_re", "ssm_b_im", "ssm_c_re", "ssm_c_im", "ssm_d"]
_WEIGHTS = ["meta_tokens", "ffn1_norm", "ffn1_w1", "ffn1_w3", "ffn1_w2", "mix_norm", "w_in", "attn_sinks", "ssm_a_re",
            "ssm_a_im", "ssm_log_step", "ssm_b_re", "ssm_b_im", "ssm_c_re", "ssm_c_im", "ssm_d", "ssm_glu_a",
            "ssm_glu_b", "w_out", "ffn2_norm", "ffn2_w1", "ffn2_w3", "ffn2_w2", "final_norm"]


def _kv_interleave(w, kv_heads):
    kvw = kv_heads * HEAD_DIM
    lead = w.shape[:-1]
    k = w[..., 0:kvw].reshape(lead + (kv_heads, 1, HEAD_DIM))
    v = w[..., kvw:2 * kvw].reshape(lead + (kv_heads, 1, HEAD_DIM))
    return jnp.concatenate([jnp.concatenate([k, v], axis=-2).reshape(lead + (2 * kvw,)), w[..., 2 * kvw:]], axis=-1)


def _kv_deinterleave(w, kv_heads):
    kvw = kv_heads * HEAD_DIM
    lead = w.shape[:-1]
    kv = w[..., 0:2 * kvw].reshape(lead + (kv_heads, 2, HEAD_DIM))
    return jnp.concatenate([kv[..., 0, :].reshape(lead + (kvw,)), kv[..., 1, :].reshape(lead + (kvw,)), w[..., 2 * kvw:]],
                           axis=-1)


def _step(x, target, w, m, v):
    B, S, D = x.shape
    L = S + N_META
    T = B * L
    H = D // HEAD_DIM
    KV = H // Q_PER_KV
    SW = D // 2
    tm = _pick_tile(L, ROW_TILE_CAP, ROW_ALIGN)
    rc = _pick_tile(L, ROW_TILE_CAP // B, 4) * B
    tw = _pick_tile(T, 3 * ROW_TILE_CAP, ROW_ALIGN)
    my_c = lax.axis_index("c")
    my_slot = 2 * lax.axis_index("x") + lax.axis_index("y")

    groups = {"ffn1": ["ffn1_w1", "ffn1_w3", "ffn1_w2"], "mix": ["w_in", "ssm_glu_a", "ssm_glu_b", "w_out"],
              "ffn2": ["ffn2_w1", "ffn2_w3", "ffn2_w2"]}
    waves = {"first": ["ffn1_w1", "ffn1_w3"], "early": ["ffn1_w2"] + groups["mix"], "late": groups["ffn2"]}
    shards = {n: w[n][0].astype(BF16) for n in _BIG}
    gathered = _gather_weights([shards[n] for n in waves["first"]] + [w["meta_tokens"]])
    ws = dict(zip(waves["first"], gathered[:-1]))
    meta = jnp.transpose(gathered[-1], (1, 0, 2)).reshape(N_META, D)

    def arrive(wave, landed):
        mine = [shards[n] for n in waves[wave]]
        ws.update(zip(waves[wave], _forward_halves(landed, mine, wave)))

    g_ffn1, g_mix, g_ffn2 = w["ffn1_norm"], w["mix_norm"], w["ffn2_norm"]
    g_final = w["final_norm"].reshape(1, D)

    h0 = jnp.concatenate([jnp.broadcast_to(meta[None], (B, N_META, D)), x], axis=1).reshape(T, D)

    def ffn_fwd(h, g, tag, carry=None):
        n = _rmsnorm_fwd(h, g, tm, f"{tag}_norm")
        riders = None if carry is None else _gather_riders([shards[k] for k in waves[carry]])
        out = _ffn_up(n, ws[f"{tag}_w1"], ws[f"{tag}_w3"], tm, f"{tag}_up", riders)
        if carry is not None:
            out, landed = out
            arrive(carry, landed)
        a, c, s = out
        return _ffn_down(s, ws[f"{tag}_w2"], h, tm, f"{tag}_down"), (n, a, c, s)

    h1, saved1 = ffn_fwd(h0, g_ffn1, "ffn1", carry="early")
    w_kvu = _kv_interleave(ws["w_in"][1], KV)
    hn = _rmsnorm_fwd(h1, g_mix, tm, "mix_norm")
    q = _mm_colslots(hn, ws["w_in"], BF16, "w_in_q", tm, first=0, count=1, scale=HEAD_DIM ** -0.5)
    kvu = _mm_plain(hn, w_kvu, "nn", F32, "w_in_kvu", tm)
    gates = _mm_colslots(hn, ws["w_in"], F32, "w_in_gates", tm, first=2, count=2)

    sinks = w["attn_sinks"].reshape(KV, Q_PER_KV, 1, 1)
    sink_col = jnp.broadcast_to(sinks, (KV, Q_PER_KV, BLOCK, 1)).reshape(KV, Q_PER_KV * BLOCK, 1)
    sink_meta = jnp.broadcast_to(sinks, (KV, Q_PER_KV, N_META, 1)).reshape(KV, Q_PER_KV * N_META, 1)
    (attn,), landed = _attn_fwd(q, kvu, sink_col, sink_meta, B, "attn_fwd",
                                _gather_riders([shards[k] for k in waves["late"]]))
    arrive("late", landed)

    def to_time_major(a2d):
        return jnp.transpose(a2d.reshape(B, L, a2d.shape[-1]), (1, 0, 2)).reshape(T, a2d.shape[-1])

    def to_batch_major(a2d):
        return jnp.transpose(a2d.reshape(L, B, a2d.shape[-1]), (1, 0, 2)).reshape(T, a2d.shape[-1])

    ssm_args = (w["ssm_a_re"][0], w["ssm_a_im"][0], w["ssm_log_step"][0], w["ssm_b_re"][0], w["ssm_b_im"][0],
                w["ssm_c_re"][0], w["ssm_c_im"][0])
    (lam, bmat, cmat), ssm_vjp = jax.vjp(_ssm_matrices, *ssm_args)
    bmat16, cmat16 = bmat.astype(BF16), cmat.astype(BF16)
    u_t = to_time_major(kvu[:, SW:])
    y_t, xs, xp = _ssm_fwd(u_t, bmat16, cmat16, w["ssm_d"], _scan_tables(lam, B, False), B, rc, "ssm_fwd")
    y0 = to_batch_major(y_t)
    yg = _gelu_fwd(y0, tm, "gelu_fwd")
    ga = _mm_colslots(yg, ws["ssm_glu_a"], F32, "glu_a", tm)
    gb = _mm_colslots(yg, ws["ssm_glu_b"], F32, "glu_b", tm)
    merged = _merge_fwd(gates, attn, ga, gb, tm, "merge_fwd")
    h2 = _mm_rowslots(merged, ws["w_out"], h1, tm, "w_out")
    h3, saved2 = ffn_fwd(h2, g_ffn2, "ffn2")
    dh3, dh3b, dg_final, loss_row = _loss_head(h3, g_final, target, tm, "loss_head")

    grads, swapped, received = {}, {}, {}
    c_idx = my_c.reshape(1).astype(jnp.int32)
    idx = jnp.stack([my_slot, my_c]).astype(jnp.int32)

    def chip_sums(group):
        names = groups[group]
        for n, r in zip(names, _swap_halves([grads[n] for n in names], group)):
            swapped[n] = r
        return [_sum_halves(grads[n], swapped[n], c_idx, f"grad_sum_halves_{n}") for n in names]

    def ffn_bwd(h, g, saved, dh, dhb, tag, carry=None):
        n, a, c, s = saved
        w1, w3, w2 = ws[f"{tag}_w1"], ws[f"{tag}_w3"], ws[f"{tag}_w2"]
        grads[f"{tag}_w2"] = _wgrad_hidden_rows(s, dhb, tw, f"{tag}_dw2", 0.5)
        if carry is None:
            da, dc = _ffn_dhidden(dhb, w2, a, c, tm, f"{tag}_dhidden")
        else:
            (da, dc), got = _ffn_dhidden(dhb, w2, a, c, tm, f"{tag}_dhidden", _exchange_riders(chip_sums(carry)))
            received.update(zip(groups[carry], got))
        grads[f"{tag}_w1"] = _wgrad_hidden_cols(n, da, tw, f"{tag}_dw1")
        grads[f"{tag}_w3"] = _wgrad_hidden_cols(n, dc, tw, f"{tag}_dw3")
        dh_in, dhb_in, grads[f"{tag}_norm"] = _ffn_dn(da, w1, dc, w3, h, g, dh, tm, f"{tag}_dn")
        return dh_in, dhb_in

    dh2, dh2b = ffn_bwd(h2, g_ffn2, saved2, dh3, dh3b, "ffn2")

    grads["w_out"] = _wgrad_rowslots(merged, dh2b, tw, "dw_out")
    dattn, dgat, dgss, dga, dgb = _merge_bwd(dh2b, ws["w_out"], gates, attn, ga, gb, tm, "merge_bwd")
    grads["ssm_glu_a"] = _wgrad_colslots(yg, dga, tw, "dglu_a")
    grads["ssm_glu_b"] = _wgrad_colslots(yg, dgb, tw, "dglu_b")
    dy0 = _gelu_bwd([(dga, ws["ssm_glu_a"]), (dgb, ws["ssm_glu_b"])], y0, tm, "gelu_bwd")
    du_t, dbmat, dcmat, dlam, dd = _ssm_bwd(to_time_major(dy0), u_t, xs, xp, bmat16, cmat16, w["ssm_d"],
                                            _scan_tables(lam, B, True), B, rc, "ssm_bwd")
    d_ssm = ssm_vjp((dlam[:, 0, :], dbmat, dcmat))
    for n, gval in zip(["ssm_a_re", "ssm_a_im", "ssm_log_step", "ssm_b_re", "ssm_b_im", "ssm_c_re", "ssm_c_im"], d_ssm):
        grads[n] = gval[None]
    grads["ssm_d"] = dd

    (dq, dkv, dsink), got = _attn_bwd(q, kvu, attn, dattn, sink_col, sink_meta, B, "attn_bwd",
                                      _exchange_riders(chip_sums("ffn2")))
    received.update(zip(groups["ffn2"], got))
    grads["attn_sinks"] = dsink[:, 0:Q_PER_KV, 0].reshape(1, H)
    dkvu = jnp.concatenate([dkv, to_batch_major(du_t).astype(BF16)], axis=1)
    pieces = [dq, dkvu, dgat, dgss]
    dw_in = [_wgrad_plain(hn, p, f"dw_in_{k}", tw) for k, p in enumerate(pieces)]
    dw_in[1] = _kv_deinterleave(dw_in[1], KV)
    grads["w_in"] = jnp.stack(dw_in)
    w_in_parts = [ws["w_in"][0], w_kvu, ws["w_in"][2], ws["w_in"][3]]
    whole = _once((D, D), lambda i: (0, 0))
    dh1, dh1b, grads["mix_norm"] = _mm_norm_bwd(
        "dhn", "nt", [(p, _spec((tm, D), lambda i: (i, 0)), wp, whole) for p, wp in zip(pieces, w_in_parts)],
        h1, g_mix, dh2, tm)
    dh0, _ = ffn_bwd(h0, g_ffn1, saved1, dh1, dh1b, "ffn1", carry="mix")
    dh0 = dh0.reshape(B, L, D)
    grad_x = dh0[:, N_META:, :]

    grads["final_norm"] = dg_final
    slay = _Layout(D, 8)
    for n in _SMALL:
        slay.add(n, w[n].shape)
    slay.add("loss", (1, D))
    meta_at = slay.align(8)
    slay.add("meta", (B * N_META, D))
    small = slay.pack({**{n: grads[n] for n in _SMALL}, "loss": loss_row, "meta": dh0[:, :N_META, :]})
    tot_small, dmeta = _all_reduce_small(small, B, N_META, meta_at)
    loss = slay.unpack(tot_small, "loss")[0, 0]
    for n in _SMALL:
        grads[n] = slay.unpack(tot_small, n)
    cw = D // N_CHIPS
    grads["meta_tokens"] = lax.dynamic_slice_in_dim(dmeta, my_slot * cw, cw, axis=1)

    received.update(zip(groups["ffn1"], _exchange_chips(chip_sums("ffn1"))))
    fulls = [_sum_chips(grads[n], swapped[n], received[n], idx, f"grad_sum_chips_{n}") for n in _BIG]
    for n, f in zip(_BIG, _join_halves(fulls)):
        grads[n] = f[None]

    delta, new_m, new_v = {}, {}, {}
    for n in _BIG + ["meta_tokens"]:
        shp = w[n].shape
        two = (shp[-2], shp[-1])
        d_, m_, v_ = _adamw(w[n].reshape(two), grads[n].reshape(two), m[n].reshape(two), v[n].reshape(two), f"adamw_{n}")
        delta[n], new_m[n], new_v[n] = d_.reshape(shp), m_.reshape(shp), v_.reshape(shp)
        grads[n] = grads[n].reshape(shp)
    def flat2d(a):
        return a.reshape(-1, D) if a.size % D == 0 else a.reshape(1, -1)

    d_, m_, v_ = _adamw_small([flat2d(w[n]) for n in _SMALL], [flat2d(grads[n]) for n in _SMALL],
                              [flat2d(m[n]) for n in _SMALL], [flat2d(v[n]) for n in _SMALL], "adamw_small")
    for i, n in enumerate(_SMALL):
        shp = w[n].shape
        delta[n], new_m[n], new_v[n] = d_[i].reshape(shp), m_[i].reshape(shp), v_[i].reshape(shp)
        grads[n] = grads[n].reshape(shp)

    return (loss, grad_x, *[grads[n] for n in _WEIGHTS], *[delta[n] for n in _WEIGHTS],
            *[new_m[n] for n in _WEIGHTS], *[new_v[n] for n in _WEIGHTS])


def kernel(x, meta_tokens, ffn1_norm, ffn1_w1, ffn1_w3, ffn1_w2, mix_norm, w_in, attn_sinks, ssm_a_re, ssm_a_im, ssm_log_step, ssm_b_re, ssm_b_im, ssm_c_re, ssm_c_im, ssm_d, ssm_glu_a, ssm_glu_b, w_out, ffn2_norm, ffn2_w1, ffn2_w3, ffn2_w2, final_norm, loss_target, m_meta_tokens, m_ffn1_norm, m_ffn1_w1, m_ffn1_w3, m_ffn1_w2, m_mix_norm, m_w_in, m_attn_sinks, m_ssm_a_re, m_ssm_a_im, m_ssm_log_step, m_ssm_b_re, m_ssm_b_im, m_ssm_c_re, m_ssm_c_im, m_ssm_d, m_ssm_glu_a, m_ssm_glu_b, m_w_out, m_ffn2_norm, m_ffn2_w1, m_ffn2_w3, m_ffn2_w2, m_final_norm, v_meta_tokens, v_ffn1_norm, v_ffn1_w1, v_ffn1_w3, v_ffn1_w2, v_mix_norm, v_w_in, v_attn_sinks, v_ssm_a_re, v_ssm_a_im, v_ssm_log_step, v_ssm_b_re, v_ssm_b_im, v_ssm_c_re, v_ssm_c_im, v_ssm_d, v_ssm_glu_a, v_ssm_glu_b, v_w_out, v_ffn2_norm, v_ffn2_w1, v_ffn2_w3, v_ffn2_w2, v_final_norm):
    args = locals()
    w = {n: args[n] for n in _WEIGHTS}
    m = {n: args["m_" + n] for n in _WEIGHTS}
    v = {n: args["v_" + n] for n in _WEIGHTS}
    return _step(x, loss_target, w, m, v)
```

```python
import functools
import math

import jax
import jax.numpy as jnp
from jax import lax
from jax.experimental import pallas as pl
from jax.experimental.pallas import tpu as pltpu

F32 = jnp.float32
BF16 = jnp.bfloat16
MESH_IDS = pl.DeviceIdType.MESH

N_CHIPS = 4
N_META = 16
HEAD_DIM = 64
Q_PER_KV = 4
QW = Q_PER_KV * HEAD_DIM
BLOCK = 128
SSM_GROUP = 16
SSM_STATE = 64
SSM_LANES = 128
GROUPS_PER_COL = SSM_LANES // SSM_GROUP
STATE_LANES = GROUPS_PER_COL * SSM_STATE
NORM_EPS = 1e-6
NEG_INF = -1e30
ADAM_LR, ADAM_B1, ADAM_B2, ADAM_EPS, ADAM_WD, ADAM_STEP = 0.001, 0.9, 0.999, 1e-08, 0.01, 10
GELU_C = math.sqrt(2.0 / math.pi)
ROW_ALIGN = 16
VMEM_LIMIT = 56 * 1024 * 1024
ROW_TILE_CAP = 688

_NN = (((1,), (0,)), ((), ()))
_NT = (((1,), (1,)), ((), ()))
_TN = (((0,), (0,)), ((), ()))
_DIMS = {"nn": _NN, "nt": _NT, "tn": _TN}


def _params(sem, **kw):
    return pltpu.CompilerParams(dimension_semantics=sem, vmem_limit_bytes=VMEM_LIMIT, **kw)


def _pick_tile(n, cap, mult):
    best = None
    for t in range(mult, min(n, cap) + 1, mult):
        if n % t == 0:
            best = t
    if best is None:
        raise ValueError(f"no tile for {n} (cap {cap}, multiple of {mult})")
    return best


def _sigmoid(x):
    return 1.0 / (1.0 + jnp.exp(-x))


def _spec(block, index_map):
    return pl.BlockSpec(block, index_map)


def _sum_dots(ins, mode):
    tot = None
    for p in range(len(ins) // 2):
        a_ref, b_ref = ins[2 * p], ins[2 * p + 1]
        for sl in ([None] if len(b_ref.shape) == 2 else range(b_ref.shape[0])):
            if sl is None:
                a, b = a_ref[...], b_ref[...]
            elif len(a_ref.shape) == 3:
                a, b = a_ref[sl], b_ref[sl]
            else:
                width = a_ref.shape[1] // b_ref.shape[0]
                a, b = a_ref[:, sl * width:(sl + 1) * width], b_ref[sl]
            d = lax.dot_general(a.astype(BF16), b.astype(BF16), _DIMS[mode], preferred_element_type=F32)
            tot = d if tot is None else tot + d
    return tot


def _mm(name, grid, kaxis, mode, pairs, out_shape, out_spec, scale=1.0, res=None):
    npairs = len(pairs)
    has_res = res is not None
    gk = 1 if kaxis is None else grid[kaxis]
    acc_shape = tuple(d for d in out_spec.block_shape if d is not None)

    def body(*refs):
        res_ref = refs[2 * npairs] if has_res else None
        o_ref = refs[2 * npairs + has_res]
        tot = _sum_dots(refs[:2 * npairs], mode)

        def finish(acc):
            r = acc * scale if scale != 1.0 else acc
            if has_res:
                r = res_ref[...] + r
            o_ref[...] = r.astype(o_ref.dtype)

        if gk == 1:
            finish(tot)
        else:
            acc_ref = refs[-1]
            k = pl.program_id(kaxis)

            @pl.when(k == 0)
            def _():
                acc_ref[...] = tot

            @pl.when(k > 0)
            def _():
                acc_ref[...] += tot

            @pl.when(k == gk - 1)
            def _():
                finish(acc_ref[...])

    in_specs, args = [], []
    for a, a_spec, b, b_spec in pairs:
        in_specs += [a_spec, b_spec]
        args += [a, b]
    if has_res:
        in_specs.append(res[1])
        args.append(res[0])
    sem = tuple("arbitrary" if ax == kaxis else "parallel" for ax in range(len(grid)))
    return pl.pallas_call(
        body, name=name, grid=grid, in_specs=in_specs, out_specs=out_spec, out_shape=out_shape,
        scratch_shapes=[pltpu.VMEM(acc_shape, F32)] if gk > 1 else [],
        compiler_params=_params(sem),
    )(*args)


def _mm_plain(a, b, mode, out_dtype, name, tm, scale=1.0):
    M, K = a.shape
    N = b.shape[1] if mode == "nn" else b.shape[0]
    return _mm(name, (M // tm,), None, mode,
               [(a, _spec((tm, K), lambda i: (i, 0)), b, _spec(b.shape, lambda i: (0, 0)))],
               jax.ShapeDtypeStruct((M, N), out_dtype), _spec((tm, N), lambda i: (i, 0)), scale=scale)


def _wgrad_plain(a, b, name, tr):
    R, M = a.shape
    N = b.shape[1]
    return _mm(name, (R // tr,), 0, "tn",
               [(a, _spec((tr, M), lambda r: (r, 0)), b, _spec((tr, N), lambda r: (r, 0)))],
               jax.ShapeDtypeStruct((M, N), F32), _spec((M, N), lambda r: (0, 0)))


def _rmsnorm_fwd(h, g, tm, name):
    T, D = h.shape

    def body(h_ref, g_ref, o_ref):
        x = h_ref[...]
        r = lax.rsqrt(jnp.mean(x * x, axis=-1, keepdims=True) + NORM_EPS)
        o_ref[...] = ((x * r) * g_ref[...]).astype(BF16)

    return pl.pallas_call(
        body, name=name, grid=(T // tm,),
        in_specs=[pl.BlockSpec((tm, D), lambda i: (i, 0)), pl.BlockSpec((1, D), lambda i: (0, 0))],
        out_specs=pl.BlockSpec((tm, D), lambda i: (i, 0)),
        out_shape=jax.ShapeDtypeStruct((T, D), BF16),
        compiler_params=_params(("parallel",)),
    )(h, g)


def _fold8(x):
    return jnp.sum(x.reshape(x.shape[0] // 8, 8, x.shape[1]), axis=0)


def _mm_norm_bwd(name, mode, pairs, h, g, dres, tm):
    T, D = h.shape
    nt = T // tm
    npairs = len(pairs)

    def body(*refs):
        h_ref, g_ref, dres_ref, dh_ref, dhb_ref, dg_ref, acc_ref = refs[2 * npairs:]
        i = pl.program_id(0)
        x = h_ref[...]
        r = lax.rsqrt(jnp.mean(x * x, axis=-1, keepdims=True) + NORM_EPS)
        xhat = x * r
        dy = _sum_dots(refs[:2 * npairs], mode)
        dxhat = dy * g_ref[...]
        dx = r * (dxhat - xhat * jnp.mean(dxhat * xhat, axis=-1, keepdims=True))
        dh = dres_ref[...] + dx
        dh_ref[...] = dh
        dhb_ref[...] = dh.astype(BF16)
        part = _fold8(dy * xhat)

        @pl.when(i == 0)
        def _():
            acc_ref[...] = part

        @pl.when(i > 0)
        def _():
            acc_ref[...] += part

        @pl.when(i == nt - 1)
        def _():
            dg_ref[...] = jnp.sum(acc_ref[...], axis=0, keepdims=True)

    row = pl.BlockSpec((tm, D), lambda i: (i, 0))
    vec = pl.BlockSpec((1, D), lambda i: (0, 0))
    in_specs, args = [], []
    for a, a_spec, b, b_spec in pairs:
        in_specs += [a_spec, b_spec]
        args += [a, b]
    return pl.pallas_call(
        body, name=name, grid=(nt,),
        in_specs=in_specs + [row, vec, row],
        out_specs=[row, row, vec],
        out_shape=[jax.ShapeDtypeStruct((T, D), F32), jax.ShapeDtypeStruct((T, D), BF16),
                   jax.ShapeDtypeStruct((1, D), F32)],
        scratch_shapes=[pltpu.VMEM((8, D), F32)],
        compiler_params=_params(("arbitrary",)),
    )(*args, h, g, dres)


def _ffn_up(n, w1, w3, tm, name, riders=None):
    T, D = n.shape
    Fs = w1.shape[2]

    def body(n_ref, w1_ref, w3_ref, a_ref, c_ref, s_ref):
        x = n_ref[...]
        a = jnp.dot(x, w1_ref[...], preferred_element_type=F32)
        c = jnp.dot(x, w3_ref[...], preferred_element_type=F32)
        a_ref[...] = a.astype(BF16)
        c_ref[...] = c.astype(BF16)
        s_ref[...] = (a * _sigmoid(a) * c).astype(BF16)

    w_spec = _spec((None, D, Fs), lambda s, i: (s, 0, 0))
    o_spec = _spec((None, tm, Fs), lambda s, i: (s, i, 0))
    o_shape = jax.ShapeDtypeStruct((N_CHIPS, T, Fs), BF16)
    return _call(body, name, (N_CHIPS, T // tm), [_spec((tm, D), lambda s, i: (i, 0)), w_spec, w_spec],
                 [o_spec, o_spec, o_spec], [o_shape, o_shape, o_shape], [], ("parallel", "parallel"), (n, w1, w3), riders)


def _ffn_down(s, w2, h, tm, name):
    _, T, Fs = s.shape
    D = w2.shape[2]
    row = _spec((tm, D), lambda i: (i, 0))
    return _mm(name, (T // tm,), None, "nn",
               [(s, _spec((N_CHIPS, tm, Fs), lambda i: (0, i, 0)), w2, _spec((N_CHIPS, Fs, D), lambda i: (0, 0, 0)))],
               jax.ShapeDtypeStruct((T, D), F32), row, scale=0.5, res=(h, row))


def _ffn_dhidden(dhb, w2, a, c, tm, name, riders=None):
    T, D = dhb.shape
    Fs = w2.shape[1]

    def body(dh_ref, w2_ref, a_ref, c_ref, da_ref, dc_ref):
        d = 0.5 * lax.dot_general(dh_ref[...], w2_ref[...], _NT, preferred_element_type=F32)
        av = a_ref[...].astype(F32)
        cv = c_ref[...].astype(F32)
        sg = _sigmoid(av)
        da_ref[...] = (d * cv * (sg * (1.0 + av * (1.0 - sg)))).astype(BF16)
        dc_ref[...] = (d * (av * sg)).astype(BF16)

    h_spec = _spec((None, tm, Fs), lambda s, i: (s, i, 0))
    o_shape = jax.ShapeDtypeStruct((N_CHIPS, T, Fs), BF16)
    return _call(body, name, (N_CHIPS, T // tm),
                 [_spec((tm, D), lambda s, i: (i, 0)), _spec((None, Fs, D), lambda s, i: (s, 0, 0)), h_spec, h_spec],
                 [h_spec, h_spec], [o_shape, o_shape], [], ("parallel", "parallel"), (dhb, w2, a, c), riders)


def _wgrad_hidden_rows(s, dhb, tr, name, scale):
    _, T, Fs = s.shape
    D = dhb.shape[1]
    return _mm(name, (N_CHIPS, T // tr), 1, "tn",
               [(s, _spec((None, tr, Fs), lambda k, r: (k, r, 0)), dhb, _spec((tr, D), lambda k, r: (r, 0)))],
               jax.ShapeDtypeStruct((N_CHIPS, Fs, D), F32), _spec((None, Fs, D), lambda k, r: (k, 0, 0)), scale=scale)


def _wgrad_hidden_cols(n, da, tr, name):
    T, D = n.shape
    Fs = da.shape[2]
    return _mm(name, (N_CHIPS, T // tr), 1, "tn",
               [(n, _spec((tr, D), lambda k, r: (r, 0)), da, _spec((None, tr, Fs), lambda k, r: (k, r, 0)))],
               jax.ShapeDtypeStruct((N_CHIPS, D, Fs), F32), _spec((None, D, Fs), lambda k, r: (k, 0, 0)))


def _once(block, index_map):
    return pl.BlockSpec(block, index_map, pipeline_mode=pl.Buffered(1))


def _ffn_dn(da, w1, dc, w3, h, g, dres, tm, name):
    _, T, Fs = da.shape
    D = w1.shape[1]
    h_spec = _spec((N_CHIPS, tm, Fs), lambda i: (0, i, 0))
    w_spec = _once((N_CHIPS, D, Fs), lambda i: (0, 0, 0))
    return _mm_norm_bwd(name, "nt", [(da, h_spec, w1, w_spec), (dc, h_spec, w3, w_spec)], h, g, dres, tm)


def _mm_side_by_side(a, w, mode, out_dtype, name, tm, first=0, count=N_CHIPS, scale=1.0):
    T, K = a.shape
    assert first % count == 0
    n = w.shape[2] if mode == "nn" else w.shape[1]

    def body(a_ref, w_ref, o_ref):
        av = a_ref[...].astype(BF16)
        for j in range(count):
            r = lax.dot_general(av, w_ref[j].astype(BF16), _DIMS[mode], preferred_element_type=F32)
            o_ref[:, j * n:(j + 1) * n] = (r * scale if scale != 1.0 else r).astype(o_ref.dtype)

    return pl.pallas_call(
        body, name=name, grid=(T // tm,),
        in_specs=[_spec((tm, K), lambda i: (i, 0)), _once((count,) + w.shape[1:], lambda i: (first // count, 0, 0))],
        out_specs=_spec((tm, count * n), lambda i: (i, 0)),
        out_shape=jax.ShapeDtypeStruct((T, count * n), out_dtype),
        compiler_params=_params(("parallel",)),
    )(a, w)


def _mm_colslots(a, w, out_dtype, name, tm, first=0, count=N_CHIPS, scale=1.0):
    return _mm_side_by_side(a, w, "nn", out_dtype, name, tm, first, count, scale)


def _wgrad_colslots(a, d, tr, name):
    T, K = a.shape
    Ns = d.shape[1] // N_CHIPS
    return _mm(name, (N_CHIPS, T // tr), 1, "tn",
               [(a, _spec((tr, K), lambda k, r: (r, 0)), d, _spec((tr, Ns), lambda k, r: (r, k)))],
               jax.ShapeDtypeStruct((N_CHIPS, K, Ns), F32), _spec((None, K, Ns), lambda k, r: (k, 0, 0)))


def _mm_rowslots(a, w, h, tm, name):
    T = a.shape[0]
    N = w.shape[2]
    row = _spec((tm, N), lambda i: (i, 0))
    return _mm(name, (T // tm,), None, "nn",
               [(a, _spec((tm, a.shape[1]), lambda i: (i, 0)), w, _once(w.shape, lambda i: (0, 0, 0)))],
               jax.ShapeDtypeStruct((T, N), F32), row, res=(h, row))


def _wgrad_rowslots(a, d, tr, name):
    T = a.shape[0]
    Ks = a.shape[1] // N_CHIPS
    N = d.shape[1]
    return _mm(name, (N_CHIPS, T // tr), 1, "tn",
               [(a, _spec((tr, Ks), lambda k, r: (r, k)), d, _spec((tr, N), lambda k, r: (r, 0)))],
               jax.ShapeDtypeStruct((N_CHIPS, Ks, N), F32), _spec((None, Ks, N), lambda k, r: (k, 0, 0)))


def _gelu_parts(x):
    inner = GELU_C * (x + 0.044715 * (x * x * x))
    t = jnp.tanh(inner)
    return t, GELU_C * (1.0 + 3.0 * 0.044715 * (x * x))


def _gelu_fwd(y, tm, name):
    T, W = y.shape

    def body(y_ref, o_ref):
        x = y_ref[...]
        t, _ = _gelu_parts(x)
        o_ref[...] = (0.5 * x * (1.0 + t)).astype(BF16)

    spec = pl.BlockSpec((tm, W), lambda i: (i, 0))
    return pl.pallas_call(body, name=name, grid=(T // tm,), in_specs=[spec], out_specs=spec,
                          out_shape=jax.ShapeDtypeStruct((T, W), BF16),
                          compiler_params=_params(("parallel",)))(y)


def _gelu_bwd(pairs, y, tm, name):
    T, W = y.shape
    npairs = len(pairs)

    def body(*refs):
        y_ref, o_ref = refs[2 * npairs], refs[2 * npairs + 1]
        x = y_ref[...]
        t, dinner = _gelu_parts(x)
        o_ref[...] = _sum_dots(refs[:2 * npairs], "nt") * (0.5 * (1.0 + t) + 0.5 * x * (1.0 - t * t) * dinner)

    spec = pl.BlockSpec((tm, W), lambda i: (i, 0))
    in_specs, args = [], []
    for d, w in pairs:
        in_specs += [_spec((tm, d.shape[1]), lambda i: (i, 0)), _once(w.shape, lambda i: (0, 0, 0))]
        args += [d, w]
    return pl.pallas_call(body, name=name, grid=(T // tm,), in_specs=in_specs + [spec], out_specs=spec,
                          out_shape=jax.ShapeDtypeStruct((T, W), F32),
                          compiler_params=_params(("parallel",)))(*args, y)


def _merge_cols(D):
    cb = 512 if D % 512 == 0 else D
    return cb, D // cb


def _merge_fwd(gates, attn, ga, gb, tm, name):
    T, D = attn.shape
    cb, nc = _merge_cols(D)

    def body(gat_ref, gss_ref, attn_ref, ga_ref, gb_ref, o_ref):
        ssm = ga_ref[...] * _sigmoid(gb_ref[...])
        o_ref[...] = (_sigmoid(gat_ref[...]) * attn_ref[...] + _sigmoid(gss_ref[...]) * ssm).astype(BF16)

    def col(block):
        return pl.BlockSpec((tm, cb), lambda i, j: (i, block * nc + j))

    return pl.pallas_call(
        body, name=name, grid=(T // tm, nc),
        in_specs=[col(0), col(1), col(0), col(0), col(0)],
        out_specs=col(0), out_shape=jax.ShapeDtypeStruct((T, D), BF16),
        compiler_params=_params(("parallel", "parallel")),
    )(gates, gates, attn, ga, gb)


def _merge_bwd(dhb, w_out, gates, attn, ga, gb, tm, name):
    T, D = attn.shape
    cb, nc = _merge_cols(D)
    Ks = w_out.shape[1]
    spb = cb // Ks

    def body(dh_ref, w_ref, gat_ref, gss_ref, attn_ref, ga_ref, gb_ref, dattn_ref, dgat_ref, dgss_ref, dga_ref, dgb_ref):
        dh = dh_ref[...]
        d = jnp.concatenate([lax.dot_general(dh, w_ref[s], _NT, preferred_element_type=F32) for s in range(spb)], axis=1)
        sa = _sigmoid(gat_ref[...])
        ss = _sigmoid(gss_ref[...])
        sb = _sigmoid(gb_ref[...])
        gav = ga_ref[...]
        dattn_ref[...] = d * sa
        dgat_ref[...] = (d * attn_ref[...] * (sa * (1.0 - sa))).astype(BF16)
        dgss_ref[...] = (d * (gav * sb) * (ss * (1.0 - ss))).astype(BF16)
        dssm = d * ss
        dga_ref[...] = (dssm * sb).astype(BF16)
        dgb_ref[...] = (dssm * gav * (sb * (1.0 - sb))).astype(BF16)

    def col(block):
        return pl.BlockSpec((tm, cb), lambda i, j: (i, block * nc + j))

    b16 = jax.ShapeDtypeStruct((T, D), BF16)
    return pl.pallas_call(
        body, name=name, grid=(T // tm, nc),
        in_specs=[pl.BlockSpec((tm, D), lambda i, j: (i, 0)), pl.BlockSpec((spb, Ks, D), lambda i, j: (j, 0, 0)),
                  col(0), col(1), col(0), col(0), col(0)],
        out_specs=[col(0)] * 5,
        out_shape=[jax.ShapeDtypeStruct((T, D), F32), b16, b16, b16, b16],
        compiler_params=_params(("parallel", "parallel")),
    )(dhb, w_out, gates, gates, attn, ga, gb)


def _loss_head(h, g, target, tm, name):
    T, D = h.shape
    B, S, _ = target.shape
    L = S + N_META
    nt = T // tm
    tpe = L // tm

    def body(h_ref, g_ref, t_hbm, dh_ref, dhb_ref, dg_ref, loss_ref, tbuf, acc_g, acc_l, sem):
        i = pl.program_id(0)
        b, j = i // tpe, i % tpe

        @pl.when(j == 0)
        def _():
            tbuf[0:N_META, :] = jnp.zeros((N_META, D), F32)
            cp = pltpu.make_async_copy(t_hbm.at[b, pl.ds(0, tm - N_META), :], tbuf.at[pl.ds(N_META, tm - N_META), :], sem)
            cp.start()
            cp.wait()

        @pl.when(j > 0)
        def _():
            cp = pltpu.make_async_copy(t_hbm.at[b, pl.ds(j * tm - N_META, tm), :], tbuf, sem)
            cp.start()
            cp.wait()

        x = h_ref[...]
        gv = g_ref[...]
        r = lax.rsqrt(jnp.mean(x * x, axis=-1, keepdims=True) + NORM_EPS)
        xhat = x * r
        pos = j * tm + lax.broadcasted_iota(jnp.int32, (tm, 1), 0)
        err = jnp.where(pos >= N_META, xhat * gv - tbuf[...], 0.0)
        dy = err * (1.0 / D)
        dxhat = dy * gv
        dh = r * (dxhat - xhat * jnp.mean(dxhat * xhat, axis=-1, keepdims=True))
        dh_ref[...] = dh
        dhb_ref[...] = dh.astype(BF16)
        pg = _fold8(dy * xhat)
        pe = _fold8(err * err)

        @pl.when(i == 0)
        def _():
            acc_g[...] = pg
            acc_l[...] = pe

        @pl.when(i > 0)
        def _():
            acc_g[...] += pg
            acc_l[...] += pe

        @pl.when(i == nt - 1)
        def _():
            dg_ref[...] = jnp.sum(acc_g[...], axis=0, keepdims=True)
            loss_ref[...] = jnp.full((1, D), (0.5 / D) * jnp.sum(acc_l[...]), F32)

    row = pl.BlockSpec((tm, D), lambda i: (i, 0))
    vec = pl.BlockSpec((1, D), lambda i: (0, 0))
    return pl.pallas_call(
        body, name=name, grid=(nt,),
        in_specs=[row, vec, pl.BlockSpec(memory_space=pl.ANY)], out_specs=[row, row, vec, vec],
        out_shape=[jax.ShapeDtypeStruct((T, D), F32), jax.ShapeDtypeStruct((T, D), BF16),
                   jax.ShapeDtypeStruct((1, D), F32), jax.ShapeDtypeStruct((1, D), F32)],
        scratch_shapes=[pltpu.VMEM((tm, D), F32), pltpu.VMEM((8, D), F32), pltpu.VMEM((8, D), F32),
                        pltpu.SemaphoreType.DMA],
        compiler_params=_params(("arbitrary",)),
    )(h, g, target)


def _heads_to_rows(blk):
    return jnp.concatenate([blk[:, g * HEAD_DIM:(g + 1) * HEAD_DIM] for g in range(Q_PER_KV)], axis=0)


def _rows_to_heads(x):
    rows = x.shape[0] // Q_PER_KV
    return jnp.concatenate([x[g * rows:(g + 1) * rows] for g in range(Q_PER_KV)], axis=1)


def _causal(R):
    qi = lax.broadcasted_iota(jnp.int32, (R, BLOCK), 0) & (BLOCK - 1)
    kj = lax.broadcasted_iota(jnp.int32, (R, BLOCK), 1)
    return kj <= qi


def _band_probs(s_band, s_m, sink):
    m = jnp.maximum(jnp.maximum(jnp.max(s_band, axis=-1, keepdims=True), jnp.max(s_m, axis=-1, keepdims=True)), sink)
    e_b, e_m, e_s = jnp.exp(s_band - m), jnp.exp(s_m - m), jnp.exp(sink - m)
    inv = 1.0 / (jnp.sum(e_b, axis=-1, keepdims=True) + jnp.sum(e_m, axis=-1, keepdims=True) + e_s)
    return e_b * inv, e_m * inv, e_s * inv


def _fold_band(tri, two):
    return jnp.where(tri, two[:, BLOCK:2 * BLOCK], two[:, 0:BLOCK])


def _unfold_band(tri, band):
    return jnp.concatenate([jnp.where(tri, 0.0, band), jnp.where(tri, band, 0.0)], axis=1)


def _meta_probs(qm, k_m, sink_m):
    R = qm.shape[0]
    s = lax.dot_general(qm, k_m, _NT, preferred_element_type=F32)
    qi = lax.broadcasted_iota(jnp.int32, (R, N_META), 0) & (N_META - 1)
    kj = lax.broadcasted_iota(jnp.int32, (R, N_META), 1)
    s = jnp.where(kj <= qi, s, NEG_INF)
    m = jnp.maximum(jnp.max(s, axis=-1, keepdims=True), sink_m)
    e, e_s = jnp.exp(s - m), jnp.exp(sink_m - m)
    inv = 1.0 / (jnp.sum(e, axis=-1, keepdims=True) + e_s)
    return e * inv, e_s * inv


def _block_start(n):
    return pl.multiple_of(N_META + n * BLOCK, ROW_ALIGN)


def _kv(blk):
    return blk[:, 0:HEAD_DIM], blk[:, HEAD_DIM:2 * HEAD_DIM]


def _attn_fwd(q, kv, sink_col, sink_meta, B, name, riders=None):
    T, D = q.shape
    L = T // B
    KV = D // QW
    nb = (L - N_META) // BLOCK

    def body(q_ref, kv_ref, sk_ref, skm_ref, o_ref, kvs):
        kvs[...] = kv_ref[...].astype(BF16)
        k_m, v_m = _kv(kvs[0:N_META, :])
        p, _ = _meta_probs(_heads_to_rows(q_ref[0:N_META, :]), k_m, skm_ref[0])
        o_ref[0:N_META, :] = _rows_to_heads(jnp.dot(p.astype(BF16), v_m, preferred_element_type=F32))
        tri = _causal(BLOCK)

        def block(cur, first, keys):
            k2, v2 = _kv(kvs[keys, :])
            qb = _heads_to_rows(q_ref[pl.ds(cur, BLOCK), :])
            s2 = lax.dot_general(qb, k2, _NT, preferred_element_type=F32)
            sm = lax.dot_general(qb, k_m, _NT, preferred_element_type=F32)
            p2s, pms = [], []
            for g in range(Q_PER_KV):
                sl = slice(g * BLOCK, (g + 1) * BLOCK)
                s_band = jnp.where(tri, s2[sl], NEG_INF) if first else _fold_band(tri, s2[sl])
                p_b, p_m, _ = _band_probs(s_band, sm[sl], sk_ref[0, sl, :])
                p2s.append((p_b if first else _unfold_band(tri, p_b)).astype(BF16))
                pms.append(p_m.astype(BF16))
            o = (jnp.dot(jnp.concatenate(p2s, axis=0), v2, preferred_element_type=F32)
                 + jnp.dot(jnp.concatenate(pms, axis=0), v_m, preferred_element_type=F32))
            o_ref[pl.ds(cur, BLOCK), :] = _rows_to_heads(o)

        block(N_META, True, pl.ds(N_META, BLOCK))

        def step(n, carry):
            block(_block_start(n), False, pl.ds(_block_start(n - 1), 2 * BLOCK))
            return carry

        lax.fori_loop(1, nb, step, 0)

    q_spec = pl.BlockSpec((L, QW), lambda b, h: (b, h))
    return _call(body, name, (B, KV),
                 [q_spec, pl.BlockSpec((L, 2 * HEAD_DIM), lambda b, h: (b, h)),
                  pl.BlockSpec((1, Q_PER_KV * BLOCK, 1), lambda b, h: (h, 0, 0)),
                  pl.BlockSpec((1, Q_PER_KV * N_META, 1), lambda b, h: (h, 0, 0))],
                 [q_spec], [jax.ShapeDtypeStruct((T, D), F32)], [pltpu.VMEM((L, 2 * HEAD_DIM), BF16)],
                 ("parallel", "parallel"), (q, kv, sink_col, sink_meta), riders)


def _attn_bwd(q, kv, o, do, sink_col, sink_meta, B, name, riders=None):
    T, D = q.shape
    L = T // B
    KV = D // QW
    nb = (L - N_META) // BLOCK
    R = Q_PER_KV * BLOCK
    scale = HEAD_DIM ** -0.5

    def head_totals(col, rows_per_head):
        rid = lax.broadcasted_iota(jnp.int32, (8, 128), 0)
        out = jnp.zeros((8, 128), F32)
        for g in range(Q_PER_KV):
            out = out + jnp.where(rid == g, jnp.sum(col[g * rows_per_head:(g + 1) * rows_per_head, :]), 0.0)
        return out

    def body(q_ref, kv_ref, o_ref, do_ref, sk_ref, skm_ref, dq_ref, dkv_ref, dsk_ref, kvs, acc, acc_sink):
        b = pl.program_id(1)
        kvs[...] = kv_ref[...].astype(BF16)
        acc[...] = jnp.zeros_like(acc)
        k_m, v_m = _kv(kvs[0:N_META, :])

        qm = _heads_to_rows(q_ref[0:N_META, :])
        dom = _heads_to_rows(do_ref[0:N_META, :])
        delta = jnp.sum(dom * _heads_to_rows(o_ref[0:N_META, :]), axis=-1, keepdims=True)
        p, p_s = _meta_probs(qm, k_m, skm_ref[0])
        domb = dom.astype(BF16)
        ds = (p * (lax.dot_general(domb, v_m, _NT, preferred_element_type=F32) - delta)).astype(BF16)
        dq_ref[0:N_META, :] = _rows_to_heads(jnp.dot(ds, k_m, preferred_element_type=F32) * scale).astype(BF16)
        acc[0:N_META, :] += jnp.concatenate([lax.dot_general(ds, qm, _TN, preferred_element_type=F32),
                                             lax.dot_general(p.astype(BF16), domb, _TN, preferred_element_type=F32)], axis=1)
        sink_tot = head_totals(-p_s * delta, N_META)
        tri = _causal(BLOCK)
        acc_sink[...] = jnp.zeros_like(acc_sink)

        def block(cur, first, keys):
            k2, v2 = _kv(kvs[keys, :])
            rows = pl.ds(cur, BLOCK)
            qb = _heads_to_rows(q_ref[rows, :])
            dob = _heads_to_rows(do_ref[rows, :])
            delta = jnp.sum(dob * _heads_to_rows(o_ref[rows, :]), axis=-1, keepdims=True)
            dobb = dob.astype(BF16)
            s2 = lax.dot_general(qb, k2, _NT, preferred_element_type=F32)
            sm = lax.dot_general(qb, k_m, _NT, preferred_element_type=F32)
            dp2 = lax.dot_general(dobb, v2, _NT, preferred_element_type=F32)
            dpm = lax.dot_general(dobb, v_m, _NT, preferred_element_type=F32)
            ds2s, p2s, dsms, pms = [], [], [], []
            for g in range(Q_PER_KV):
                sl = slice(g * BLOCK, (g + 1) * BLOCK)
                s_band = jnp.where(tri, s2[sl], NEG_INF) if first else _fold_band(tri, s2[sl])
                p_b, p_m, p_s = _band_probs(s_band, sm[sl], sk_ref[0, sl, :])
                ds_b = p_b * ((dp2[sl] if first else _fold_band(tri, dp2[sl])) - delta[sl])
                ds2s.append((ds_b if first else _unfold_band(tri, ds_b)).astype(BF16))
                p2s.append((p_b if first else _unfold_band(tri, p_b)).astype(BF16))
                dsms.append((p_m * (dpm[sl] - delta[sl])).astype(BF16))
                pms.append(p_m.astype(BF16))
                acc_sink[sl, :] += -p_s * delta[sl]
            ds2, p2 = jnp.concatenate(ds2s, axis=0), jnp.concatenate(p2s, axis=0)
            dsm, pm = jnp.concatenate(dsms, axis=0), jnp.concatenate(pms, axis=0)
            dq = jnp.dot(ds2, k2, preferred_element_type=F32) + jnp.dot(dsm, k_m, preferred_element_type=F32)
            dq_ref[rows, :] = _rows_to_heads(dq * scale).astype(BF16)
            acc[keys, :] += jnp.concatenate([lax.dot_general(ds2, qb, _TN, preferred_element_type=F32),
                                             lax.dot_general(p2, dobb, _TN, preferred_element_type=F32)], axis=1)
            acc[0:N_META, :] += jnp.concatenate([lax.dot_general(dsm, qb, _TN, preferred_element_type=F32),
                                                 lax.dot_general(pm, dobb, _TN, preferred_element_type=F32)], axis=1)

        block(N_META, True, pl.ds(N_META, BLOCK))

        def step(n, carry):
            block(_block_start(n), False, pl.ds(_block_start(n - 1), 2 * BLOCK))
            return carry

        lax.fori_loop(1, nb, step, 0)
        dkv_ref[...] = acc[...].astype(BF16)
        tot = sink_tot + head_totals(acc_sink[...], BLOCK)

        @pl.when(b == 0)
        def _():
            dsk_ref[0] = tot

        @pl.when(b > 0)
        def _():
            dsk_ref[0] += tot

    q_spec = pl.BlockSpec((L, QW), lambda h, b: (b, h))
    kv_spec = pl.BlockSpec((L, 2 * HEAD_DIM), lambda h, b: (b, h))
    return _call(body, name, (KV, B),
                 [q_spec, kv_spec, q_spec, q_spec,
                  pl.BlockSpec((1, R, 1), lambda h, b: (h, 0, 0)),
                  pl.BlockSpec((1, Q_PER_KV * N_META, 1), lambda h, b: (h, 0, 0))],
                 [q_spec, kv_spec, pl.BlockSpec((1, 8, 128), lambda h, b: (h, 0, 0))],
                 [jax.ShapeDtypeStruct((T, D), BF16), jax.ShapeDtypeStruct((T, KV * 2 * HEAD_DIM), BF16),
                  jax.ShapeDtypeStruct((KV, 8, 128), F32)],
                 [pltpu.VMEM((L, 2 * HEAD_DIM), BF16), pltpu.VMEM((L, 2 * HEAD_DIM), F32), pltpu.VMEM((R, 1), F32)],
                 ("parallel", "arbitrary"), (q, kv, o, do, sink_col, sink_meta), riders)


def _cmul_add(acc_r, acc_i, lr, li, xr, xi):
    return acc_r + (lr * xr - li * xi), acc_i + (lr * xi + li * xr)


def _cols_per_step(ncol):
    for cps in (4, 2):
        if ncol % cps == 0:
            return cps
    return 1


def _ssm_fwd(u, bmat, cmat, dskip, tables, nbatch, rc, name):
    T, W = u.shape
    ncol = W // SSM_LANES
    nch = T // rc
    S = STATE_LANES
    cps = _cols_per_step(ncol)
    assert nbatch == 4

    def body(u_ref, b_ref, c_ref, d_ref, tab_ref, y_ref, xs_ref, st_ref, carry_ref):
        ch = pl.program_id(1)

        @pl.when(ch == 0)
        def _():
            carry_ref[...] = jnp.zeros_like(carry_ref)

        uv = u_ref[...]
        for k in range(cps):
            st_ref[:, 2 * S * k:2 * S * (k + 1)] = jnp.dot(uv[:, SSM_LANES * k:SSM_LANES * (k + 1)].astype(BF16), b_ref[k],
                                                           preferred_element_type=F32)
        low = lax.broadcasted_iota(jnp.int32, (8, S), 0) < nbatch

        def tile(k, r0, c_r, c_i):
            re, im = slice(2 * S * k, 2 * S * k + S), slice(2 * S * k + S, 2 * S * (k + 1))
            la_r, la_i = tab_ref[k, :, 0:S], tab_ref[k, :, S:2 * S]
            lb_r, lb_i = tab_ref[k, :, 2 * S:3 * S], tab_ref[k, :, 3 * S:4 * S]
            v_r = st_ref[pl.ds(r0, 8), re]
            v_i = st_ref[pl.ds(r0, 8), im]
            v_r, v_i = _cmul_add(v_r, v_i, la_r, la_i, pltpu.roll(v_r, nbatch, 0), pltpu.roll(v_i, nbatch, 0))
            rc_r, rc_i = pltpu.roll(c_r, nbatch, 0), pltpu.roll(c_i, nbatch, 0)
            cb_r, cb_i = jnp.where(low, rc_r, c_r), jnp.where(low, rc_i, c_i)
            v_r, v_i = _cmul_add(v_r, v_i, lb_r, lb_i, cb_r, cb_i)
            st_ref[pl.ds(r0, 8), re] = v_r
            st_ref[pl.ds(r0, 8), im] = v_i
            return v_r, v_i

        def step(i, carry):
            r0 = pl.multiple_of(i * 8, 8)
            out = []
            for k in range(cps):
                out += list(tile(k, r0, carry[2 * k], carry[2 * k + 1]))
            return tuple(out)

        halves = tuple(carry_ref[:, S * j:S * (j + 1)] for j in range(2 * cps))
        halves = lax.fori_loop(0, rc // 8, step, halves)
        for j in range(2 * cps):
            carry_ref[:, S * j:S * (j + 1)] = halves[j]
        xb = st_ref[...].astype(BF16)
        xs_ref[...] = xb
        for k in range(cps):
            cols = slice(SSM_LANES * k, SSM_LANES * (k + 1))
            y_ref[:, cols] = (jnp.dot(xb[:, 2 * S * k:2 * S * (k + 1)], c_ref[k], preferred_element_type=F32)
                              + d_ref[:, cols] * uv[:, cols])

    return pl.pallas_call(
        body, name=name, grid=(ncol // cps, nch),
        in_specs=[pl.BlockSpec((rc, cps * SSM_LANES), lambda g, c: (c, g)),
                  pl.BlockSpec((cps, SSM_LANES, 2 * S), lambda g, c: (g, 0, 0)),
                  pl.BlockSpec((cps, 2 * S, SSM_LANES), lambda g, c: (g, 0, 0)),
                  pl.BlockSpec((1, cps * SSM_LANES), lambda g, c: (0, g)),
                  pl.BlockSpec((cps, 8, 4 * S), lambda g, c: (g, 0, 0))],
        out_specs=[pl.BlockSpec((rc, cps * SSM_LANES), lambda g, c: (c, g)),
                   pl.BlockSpec((rc, cps * 2 * S), lambda g, c: (c, g))],
        out_shape=[jax.ShapeDtypeStruct((T, W), F32), jax.ShapeDtypeStruct((T, ncol * 2 * S), BF16)],
        scratch_shapes=[pltpu.VMEM((rc, cps * 2 * S), F32), pltpu.VMEM((8, cps * 2 * S), F32)],
        compiler_params=_params(("parallel", "arbitrary")),
    )(u, bmat, cmat, dskip, tables)


def _ssm_bwd(dy, u, xs, bmat, cmat, dskip, tables, nbatch, rc, name):
    T, W = u.shape
    ncol = W // SSM_LANES
    nch = T // rc
    S = STATE_LANES
    ntile = rc // 16
    cps = _cols_per_step(ncol)

    def body(dy_ref, u_ref, xs_ref, b_ref, c_ref, d_ref, tab_ref,
             du_ref, db_ref, dc_ref, dl_ref, dd_ref, st_ref, carry_ref, accl_ref, accd_ref):
        ch = pl.program_id(1)

        @pl.when(ch == 0)
        def _():
            carry_ref[...] = jnp.zeros_like(carry_ref)
            accl_ref[...] = jnp.zeros_like(accl_ref)
            accd_ref[...] = jnp.zeros_like(accd_ref)
            db_ref[...] = jnp.zeros_like(db_ref)
            dc_ref[...] = jnp.zeros_like(dc_ref)

        dyv = dy_ref[...]
        uv = u_ref[...]
        dyb = dyv.astype(BF16)
        for k in range(cps):
            st_ref[:, 2 * S * k:2 * S * (k + 1)] = lax.dot_general(dyb[:, SSM_LANES * k:SSM_LANES * (k + 1)], c_ref[k], _NT,
                                                                   preferred_element_type=F32)
        low = lax.broadcasted_iota(jnp.int32, (8, S), 0) < nbatch

        def tile(k, r0, x_r, x_i, c_r, c_i, al_r, al_i):
            re, im = slice(2 * S * k, 2 * S * k + S), slice(2 * S * k + S, 2 * S * (k + 1))
            la_r, la_i = tab_ref[k, :, 0:S], tab_ref[k, :, S:2 * S]
            lb_r, lb_i = tab_ref[k, :, 2 * S:3 * S], tab_ref[k, :, 3 * S:4 * S]
            v_r = st_ref[pl.ds(r0, 8), re]
            v_i = st_ref[pl.ds(r0, 8), im]
            v_r, v_i = _cmul_add(v_r, v_i, la_r, la_i, pltpu.roll(v_r, nbatch, 0), pltpu.roll(v_i, nbatch, 0))
            cb_r = jnp.where(low, c_r, pltpu.roll(c_r, nbatch, 0))
            cb_i = jnp.where(low, c_i, pltpu.roll(c_i, nbatch, 0))
            v_r, v_i = _cmul_add(v_r, v_i, lb_r, lb_i, cb_r, cb_i)
            st_ref[pl.ds(r0, 8), re] = v_r
            st_ref[pl.ds(r0, 8), im] = v_i
            n_r = jnp.where(low, pltpu.roll(v_r, nbatch, 0), cb_r)
            n_i = jnp.where(low, pltpu.roll(v_i, nbatch, 0), cb_i)
            al_r = al_r + (n_r * x_r + n_i * x_i)
            al_i = al_i + (n_i * x_r - n_r * x_i)
            return v_r, v_i, al_r, al_i

        def step(j, carry):
            r0 = pl.multiple_of((ntile - 1 - j) * 16, 16)
            out = []
            for k in range(cps):
                re, im = slice(2 * S * k, 2 * S * k + S), slice(2 * S * k + S, 2 * S * (k + 1))
                x_r = xs_ref[pl.ds(r0, 16), re].astype(F32)
                x_i = xs_ref[pl.ds(r0, 16), im].astype(F32)
                mid = tile(k, r0 + 8, x_r[8:16], x_i[8:16], *carry[4 * k:4 * k + 4])
                out += list(tile(k, r0, x_r[0:8], x_i[0:8], *mid))
            return tuple(out)

        init = []
        for k in range(cps):
            init += [carry_ref[:, 2 * S * k:2 * S * k + S], carry_ref[:, 2 * S * k + S:2 * S * (k + 1)],
                     accl_ref[:, 2 * S * k:2 * S * k + S], accl_ref[:, 2 * S * k + S:2 * S * (k + 1)]]
        fin = lax.fori_loop(0, ntile, step, tuple(init))
        for k in range(cps):
            carry_ref[:, 2 * S * k:2 * S * k + S] = fin[4 * k]
            carry_ref[:, 2 * S * k + S:2 * S * (k + 1)] = fin[4 * k + 1]
            accl_ref[:, 2 * S * k:2 * S * k + S] = fin[4 * k + 2]
            accl_ref[:, 2 * S * k + S:2 * S * (k + 1)] = fin[4 * k + 3]
        dsb = st_ref[...].astype(BF16)
        ub = uv.astype(BF16)
        for k in range(cps):
            cols, lanes = slice(SSM_LANES * k, SSM_LANES * (k + 1)), slice(2 * S * k, 2 * S * (k + 1))
            du_ref[:, cols] = (lax.dot_general(dsb[:, lanes], b_ref[k], _NT, preferred_element_type=F32)
                               + d_ref[:, cols] * dyv[:, cols])
            db_ref[k] += lax.dot_general(ub[:, cols], dsb[:, lanes], _TN, preferred_element_type=F32)
            dc_ref[k] += lax.dot_general(xs_ref[:, lanes], dyb[:, cols], _TN, preferred_element_type=F32)
        accd_ref[...] += _fold8(dyv * uv)

        @pl.when(ch == nch - 1)
        def _():
            for k in range(cps):
                dl_ref[k] = jnp.sum(accl_ref[:, 2 * S * k:2 * S * (k + 1)], axis=0, keepdims=True)
            dd_ref[...] = jnp.sum(accd_ref[...], axis=0, keepdims=True)

    rev = lambda g, c: (nch - 1 - c, g)
    return pl.pallas_call(
        body, name=name, grid=(ncol // cps, nch),
        in_specs=[pl.BlockSpec((rc, cps * SSM_LANES), rev), pl.BlockSpec((rc, cps * SSM_LANES), rev),
                  pl.BlockSpec((rc, cps * 2 * S), rev),
                  pl.BlockSpec((cps, SSM_LANES, 2 * S), lambda g, c: (g, 0, 0)),
                  pl.BlockSpec((cps, 2 * S, SSM_LANES), lambda g, c: (g, 0, 0)),
                  pl.BlockSpec((1, cps * SSM_LANES), lambda g, c: (0, g)),
                  pl.BlockSpec((cps, 8, 4 * S), lambda g, c: (g, 0, 0))],
        out_specs=[pl.BlockSpec((rc, cps * SSM_LANES), rev),
                   pl.BlockSpec((cps, SSM_LANES, 2 * S), lambda g, c: (g, 0, 0)),
                   pl.BlockSpec((cps, 2 * S, SSM_LANES), lambda g, c: (g, 0, 0)),
                   pl.BlockSpec((cps, 1, 2 * S), lambda g, c: (g, 0, 0)),
                   pl.BlockSpec((1, cps * SSM_LANES), lambda g, c: (0, g))],
        out_shape=[jax.ShapeDtypeStruct((T, W), F32),
                   jax.ShapeDtypeStruct((ncol, SSM_LANES, 2 * S), F32),
                   jax.ShapeDtypeStruct((ncol, 2 * S, SSM_LANES), F32),
                   jax.ShapeDtypeStruct((ncol, 1, 2 * S), F32),
                   jax.ShapeDtypeStruct((1, W), F32)],
        scratch_shapes=[pltpu.VMEM((rc, cps * 2 * S), F32), pltpu.VMEM((8, cps * 2 * S), F32),
                        pltpu.VMEM((8, cps * 2 * S), F32), pltpu.VMEM((8, cps * SSM_LANES), F32)],
        compiler_params=_params(("parallel", "arbitrary")),
    )(dy, u, xs, bmat, cmat, dskip, tables)


def _ssm_matrices(a_re, a_im, log_step, b_re, b_im, c_re, c_im):
    G, N = a_re.shape
    ncol = G // GROUPS_PER_COL
    step = jnp.exp(log_step)[:, None]
    mag = jnp.exp(a_re * step)
    ang = a_im * step
    lam_re, lam_im = mag * jnp.cos(ang), mag * jnp.sin(ang)
    den = a_re * a_re + a_im * a_im
    nr, ni = lam_re - 1.0, lam_im
    coef_re = (nr * a_re + ni * a_im) / den
    coef_im = (ni * a_re - nr * a_im) / den
    bb_re = coef_re[..., None] * b_re - coef_im[..., None] * b_im
    bb_im = coef_re[..., None] * b_im + coef_im[..., None] * b_re
    eye = jnp.eye(GROUPS_PER_COL, dtype=F32)
    bb = jnp.stack([bb_re, bb_im]).reshape(2, ncol, GROUPS_PER_COL, N, SSM_GROUP)
    bmat = jnp.einsum("pbgnc,gh->bgcphn", bb, eye).reshape(ncol, SSM_LANES, 2 * STATE_LANES)
    cc = jnp.stack([c_re, -c_im]).reshape(2, ncol, GROUPS_PER_COL, SSM_GROUP, N)
    cmat = jnp.einsum("pbgcn,gh->bpgnhc", cc, eye).reshape(ncol, 2 * STATE_LANES, SSM_LANES)
    lam = jnp.concatenate([lam_re.reshape(ncol, STATE_LANES), lam_im.reshape(ncol, STATE_LANES)], axis=-1)
    return lam, bmat, cmat


def _scan_tables(lam, nbatch, conj):
    S = STATE_LANES
    lr, li = lam[:, None, 0:S], lam[:, None, S:2 * S]
    if conj:
        li = -li
    l2r, l2i = lr * lr - li * li, 2.0 * lr * li
    first = (jnp.arange(8) < nbatch)[None, :, None]
    zero = jnp.zeros_like(lr)
    if conj:
        parts = [jnp.where(first, lr, zero), jnp.where(first, li, zero), jnp.where(first, l2r, lr), jnp.where(first, l2i, li)]
    else:
        parts = [jnp.where(first, zero, lr), jnp.where(first, zero, li), jnp.where(first, lr, l2r), jnp.where(first, li, l2i)]
    return jnp.concatenate([jnp.broadcast_to(p, (lam.shape[0], 8, S)) for p in parts], axis=-1)


def _adamw_update(w_ref, g_ref, m_ref, v_ref, d_ref, nm_ref, nv_ref):
    gv = g_ref[...]
    mn = ADAM_B1 * m_ref[...] + (1.0 - ADAM_B1) * gv
    vn = ADAM_B2 * v_ref[...] + (1.0 - ADAM_B2) * (gv * gv)
    m_hat = mn / (1.0 - ADAM_B1 ** ADAM_STEP)
    v_hat = vn / (1.0 - ADAM_B2 ** ADAM_STEP)
    d_ref[...] = -ADAM_LR * (m_hat / (jnp.sqrt(v_hat) + ADAM_EPS) + ADAM_WD * w_ref[...])
    nm_ref[...] = mn
    nv_ref[...] = vn


def _adamw_small(ws, gs, ms, vs, name):
    n = len(ws)

    def body(*refs):
        for i in range(n):
            _adamw_update(refs[i], refs[n + i], refs[2 * n + i], refs[3 * n + i],
                          refs[4 * n + i], refs[5 * n + i], refs[6 * n + i])

    vm = pl.BlockSpec(memory_space=pltpu.VMEM)
    shapes = [jax.ShapeDtypeStruct(a.shape, F32) for a in ws]
    outs = pl.pallas_call(body, name=name, in_specs=[vm] * (4 * n), out_specs=[vm] * (3 * n), out_shape=shapes * 3,
                          compiler_params=pltpu.CompilerParams(vmem_limit_bytes=VMEM_LIMIT))(*ws, *gs, *ms, *vs)
    return outs[:n], outs[n:2 * n], outs[2 * n:]


def _adamw(w, g, m, v, name):
    R, C = w.shape[-2], w.shape[-1]
    tr = R if R <= 512 else _pick_tile(R, 512, 8)
    body = functools.partial(_adamw_update)

    def spec_for(a):
        if len(a.shape) == 2:
            return pl.BlockSpec((tr, C), lambda i: (i, 0))
        return pl.BlockSpec((None, tr, C), lambda i: (0, i, 0))

    spec = spec_for(w)
    shp = jax.ShapeDtypeStruct(w.shape, F32)
    return pl.pallas_call(body, name=name, grid=(R // tr,), in_specs=[spec, spec_for(g), spec, spec], out_specs=[spec] * 3,
                          out_shape=[shp, shp, shp], compiler_params=_params(("parallel",)))(w, g, m, v)


_ANY = pl.BlockSpec(memory_space=pl.ANY)


def _place():
    x, y, c = lax.axis_index("x"), lax.axis_index("y"), lax.axis_index("c")
    chips = [(1 - x, y), (x, 1 - y), (1 - x, 1 - y)]
    return x, y, c, chips


def _remote(src, dst, send_sems, recv_sems, k, to):
    return pltpu.make_async_remote_copy(src_ref=src, dst_ref=dst, send_sem=send_sems.at[k], recv_sem=recv_sems.at[k],
                                        device_id=to, device_id_type=MESH_IDS)


class _Riders:
    def __init__(self, srcs, out_shapes, n_sems, copies):
        self.srcs, self.out_shapes, self.n_sems, self.copies = list(srcs), list(out_shapes), n_sems, copies


def _call(body, name, grid, in_specs, out_specs, out_shape, scratch_shapes, sem, args, riders=None):
    if riders is None:
        return pl.pallas_call(body, name=name, grid=grid, in_specs=in_specs, out_specs=out_specs, out_shape=out_shape,
                              scratch_shapes=scratch_shapes, compiler_params=_params(sem))(*args)
    n_in, n_out, n_scr = len(in_specs), len(out_specs), len(scratch_shapes)
    r_in, r_out = len(riders.srcs), len(riders.out_shapes)

    def carrying(*refs):
        a, b = n_in, n_in + r_in
        c, d = b + n_out, b + n_out + r_out
        e = d + n_scr
        sends, arrivals = riders.copies(refs[a:b], refs[c:d], refs[e], refs[e + 1])
        first, last = None, None
        for ax, size in enumerate(grid):
            at0, at1 = pl.program_id(ax) == 0, pl.program_id(ax) == size - 1
            first = at0 if first is None else first & at0
            last = at1 if last is None else last & at1

        @pl.when(first)
        def _():
            for cp in sends:
                cp.start()

        body(*refs[:a], *refs[b:c], *refs[d:e])

        @pl.when(last)
        def _():
            for cp in arrivals:
                cp.wait_recv()
            for cp in sends:
                cp.wait_send()

    outs = pl.pallas_call(
        carrying, name=name, grid=grid, in_specs=list(in_specs) + [_ANY] * r_in,
        out_specs=list(out_specs) + [_ANY] * r_out, out_shape=list(out_shape) + riders.out_shapes,
        scratch_shapes=list(scratch_shapes) + [pltpu.SemaphoreType.DMA((riders.n_sems,)),
                                               pltpu.SemaphoreType.DMA((riders.n_sems,))],
        compiler_params=pltpu.CompilerParams(dimension_semantics=("arbitrary",) * len(grid),
                                             vmem_limit_bytes=VMEM_LIMIT, has_side_effects=True),
    )(*args, *riders.srcs)
    return outs[:n_out], outs[n_out:]


def _gather_riders(shards):
    def copies(srcs, outs, send_sems, recv_sems):
        x, y, c, chips = _place()
        sends, arrivals = [], []
        for i, s in enumerate(shards):
            half = s.shape[0] // 2
            rows = pl.ds(c * half, half)
            for j, chip in enumerate(chips):
                sends.append(_remote(srcs[i].at[rows, :], outs[i].at[2 * x + y, rows, :], send_sems, recv_sems,
                                     3 * i + j, (*chip, c)))
                landed = outs[i].at[2 * chip[0] + chip[1], rows, :]
                arrivals.append(_remote(landed, landed, send_sems, recv_sems, 3 * i + j, (*chip, c)))
        return sends, arrivals

    return _Riders(shards, [jax.ShapeDtypeStruct((N_CHIPS,) + s.shape, s.dtype) for s in shards], 3 * len(shards), copies)


def _exchange_riders(parts):
    def copies(srcs, outs, send_sems, recv_sems):
        x, y, c, chips = _place()
        sends = [_remote(srcs[i].at[2 * chip[0] + chip[1]], outs[i].at[j], send_sems, recv_sems, 3 * i + j, (*chip, c))
                 for i in range(len(parts)) for j, chip in enumerate(chips)]
        return sends, sends

    return _Riders(parts, [jax.ShapeDtypeStruct((3,) + p.shape[1:], p.dtype) for p in parts], 3 * len(parts), copies)


def _forward_halves(gathered, shards, tag):
    n = len(gathered)

    def body(*refs):
        srcs, outs = refs[:n], refs[n:2 * n]
        send_sems, recv_sems = refs[2 * n:]
        x, y, c, chips = _place()
        sibling = (x, y, 1 - c)
        cps = []
        for i in range(n):
            half = gathered[i].shape[1] // 2
            for j, chip in enumerate(chips):
                slot = 2 * chip[0] + chip[1]
                cps.append(_remote(srcs[i].at[slot, pl.ds(c * half, half), :], outs[i].at[slot, pl.ds(c * half, half), :],
                                   send_sems, recv_sems, 3 * i + j, sibling))
        for cp in cps:
            cp.start()
        for i in range(n):
            half = gathered[i].shape[1] // 2
            for j, chip in enumerate(chips):
                theirs = outs[i].at[2 * chip[0] + chip[1], pl.ds((1 - c) * half, half), :]
                _remote(theirs, theirs, send_sems, recv_sems, 3 * i + j, sibling).wait_recv()
        for cp in cps:
            cp.wait_send()

    outs = pl.pallas_call(
        body, name=f"gather_forward_{tag}", in_specs=[_ANY] * n, out_specs=[_ANY] * n,
        out_shape=[jax.ShapeDtypeStruct(g.shape, g.dtype) for g in gathered],
        input_output_aliases={i: i for i in range(n)},
        scratch_shapes=[pltpu.SemaphoreType.DMA((3 * n,)), pltpu.SemaphoreType.DMA((3 * n,))],
        compiler_params=pltpu.CompilerParams(has_side_effects=True),
    )(*gathered)
    slot = 2 * lax.axis_index("x") + lax.axis_index("y")
    return [lax.dynamic_update_slice(o, s[None], (slot, 0, 0)) for o, s in zip(outs, shards)]


def _gather_weights(shards):
    n = len(shards)

    def body(*refs):
        srcs, outs = refs[:n], refs[n:2 * n]
        send_sems, recv_sems = refs[2 * n:]
        x, y, c, chips = _place()
        sibling = (x, y, 1 - c)

        def piece(i, px, py, pc):
            half = shards[i].shape[0] // 2
            return outs[i].at[2 * px + py, pl.ds(pc * half, half), :]

        first = []
        for i in range(n):
            half = shards[i].shape[0] // 2
            for j, chip in enumerate(chips):
                first.append(_remote(srcs[i].at[pl.ds(c * half, half), :], piece(i, x, y, c), send_sems, recv_sems,
                                     6 * i + j, (*chip, c)))
        for cp in first:
            cp.start()
        passed = []
        for i in range(n):
            for j, chip in enumerate(chips):
                _remote(piece(i, *chip, c), piece(i, *chip, c), send_sems, recv_sems, 6 * i + j, (*chip, c)).wait_recv()
                cp = _remote(piece(i, *chip, c), piece(i, *chip, c), send_sems, recv_sems, 6 * i + 3 + j, sibling)
                cp.start()
                passed.append(cp)
        for i in range(n):
            for j, chip in enumerate(chips):
                _remote(piece(i, *chip, 1 - c), piece(i, *chip, 1 - c), send_sems, recv_sems, 6 * i + 3 + j,
                        sibling).wait_recv()
        for cp in first + passed:
            cp.wait_send()

    outs = pl.pallas_call(
        body, name="gather_weights", in_specs=[_ANY] * n, out_specs=[_ANY] * n,
        out_shape=[jax.ShapeDtypeStruct((N_CHIPS,) + s.shape, s.dtype) for s in shards],
        scratch_shapes=[pltpu.SemaphoreType.DMA((6 * n,)), pltpu.SemaphoreType.DMA((6 * n,))],
        compiler_params=pltpu.CompilerParams(has_side_effects=True),
    )(*shards)
    slot = 2 * lax.axis_index("x") + lax.axis_index("y")
    return [lax.dynamic_update_slice(o, s[None], (slot, 0, 0)) for o, s in zip(outs, shards)]


def _swap_halves(grads, tag):
    n = len(grads)

    def body(*refs):
        srcs, outs = refs[:n], refs[n:2 * n]
        send_sems, recv_sems = refs[2 * n:]
        x, y, c, _ = _place()
        cps = []
        for i in range(n):
            half = grads[i].shape[1] // 2
            cps.append(_remote(srcs[i].at[:, pl.ds((1 - c) * half, half), :], outs[i], send_sems, recv_sems, i, (x, y, 1 - c)))
        for cp in cps:
            cp.start()
        for cp in cps:
            cp.wait()

    return pl.pallas_call(
        body, name=f"grad_swap_halves_{tag}", in_specs=[_ANY] * n, out_specs=[_ANY] * n,
        out_shape=[jax.ShapeDtypeStruct((N_CHIPS, g.shape[1] // 2, g.shape[2]), g.dtype) for g in grads],
        scratch_shapes=[pltpu.SemaphoreType.DMA((n,)), pltpu.SemaphoreType.DMA((n,))],
        compiler_params=pltpu.CompilerParams(has_side_effects=True),
    )(*grads)


def _exchange_chips(parts):
    n = len(parts)

    def body(*refs):
        srcs, outs = refs[:n], refs[n:2 * n]
        send_sems, recv_sems = refs[2 * n:]
        x, y, c, chips = _place()
        cps = [_remote(srcs[i].at[2 * chip[0] + chip[1]], outs[i].at[j], send_sems, recv_sems, 3 * i + j, (*chip, c))
               for i in range(n) for j, chip in enumerate(chips)]
        for cp in cps:
            cp.start()
        for cp in cps:
            cp.wait()

    return pl.pallas_call(
        body, name="grad_exchange_chips", in_specs=[_ANY] * n, out_specs=[_ANY] * n,
        out_shape=[jax.ShapeDtypeStruct((3,) + p.shape[1:], p.dtype) for p in parts],
        scratch_shapes=[pltpu.SemaphoreType.DMA((3 * n,)), pltpu.SemaphoreType.DMA((3 * n,))],
        compiler_params=pltpu.CompilerParams(has_side_effects=True),
    )(*parts)


def _join_halves(fulls):
    n = len(fulls)

    def body(*refs):
        srcs, outs = refs[:n], refs[n:2 * n]
        send_sems, recv_sems = refs[2 * n:]
        x, y, c, _ = _place()
        sibling = (x, y, 1 - c)
        cps = []
        for i in range(n):
            h = fulls[i].shape[0] // 2
            cps.append(_remote(srcs[i].at[pl.ds(c * h, h), :], outs[i].at[pl.ds(c * h, h), :], send_sems, recv_sems, i,
                               sibling))
        for cp in cps:
            cp.start()
        for i in range(n):
            h = fulls[i].shape[0] // 2
            theirs = outs[i].at[pl.ds((1 - c) * h, h), :]
            _remote(theirs, theirs, send_sems, recv_sems, i, sibling).wait_recv()
        for cp in cps:
            cp.wait_send()

    return pl.pallas_call(
        body, name="grad_join_halves", in_specs=[_ANY] * n, out_specs=[_ANY] * n,
        out_shape=[jax.ShapeDtypeStruct(f.shape, f.dtype) for f in fulls],
        input_output_aliases={i: i for i in range(n)},
        scratch_shapes=[pltpu.SemaphoreType.DMA((n,)), pltpu.SemaphoreType.DMA((n,))],
        compiler_params=pltpu.CompilerParams(has_side_effects=True),
    )(*fulls)


def _half_tile(h):
    return h if h <= 512 else _pick_tile(h, 512, ROW_ALIGN)


def _sum_halves(g, r1, c_idx, name):
    _, R, C = g.shape
    H = R // 2
    tr = _half_tile(H)
    nblk = H // tr

    def body(c_ref, g_ref, r_ref, p_ref):
        p_ref[...] = (g_ref[...] + r_ref[...]).astype(BF16)

    half = pl.BlockSpec((None, tr, C), lambda s, i, c_ref: (s, c_ref[0] * nblk + i, 0))
    plain = pl.BlockSpec((None, tr, C), lambda s, i, c_ref: (s, i, 0))
    return pl.pallas_call(
        body, name=name,
        grid_spec=pltpu.PrefetchScalarGridSpec(num_scalar_prefetch=1, grid=(N_CHIPS, nblk), in_specs=[half, plain],
                                               out_specs=plain),
        out_shape=jax.ShapeDtypeStruct((N_CHIPS, H, C), BF16),
        compiler_params=_params(("parallel", "parallel")),
    )(c_idx, g, r1)


def _sum_chips(g, r1, r2, idx, name):
    _, R, C = g.shape
    H = R // 2
    tr = _half_tile(H)
    nblk = H // tr

    def body(idx_ref, g_ref, r1_ref, r2_ref, o_ref):
        o_ref[...] = (((g_ref[...] + r1_ref[...]) + r2_ref[0].astype(F32)) + r2_ref[1].astype(F32)) + r2_ref[2].astype(F32)

    return pl.pallas_call(
        body, name=name,
        grid_spec=pltpu.PrefetchScalarGridSpec(
            num_scalar_prefetch=1, grid=(nblk,),
            in_specs=[pl.BlockSpec((None, tr, C), lambda i, idx_ref: (idx_ref[0], idx_ref[1] * nblk + i, 0)),
                      pl.BlockSpec((None, tr, C), lambda i, idx_ref: (idx_ref[0], i, 0)),
                      pl.BlockSpec((3, tr, C), lambda i, idx_ref: (0, i, 0))],
            out_specs=pl.BlockSpec((tr, C), lambda i, idx_ref: (idx_ref[1] * nblk + i, 0))),
        out_shape=jax.ShapeDtypeStruct((R, C), F32),
        compiler_params=_params(("parallel",)),
    )(idx, g, r1, r2)


def _all_reduce_small(v, n_fold, fold_rows, fold_at):
    M, N = v.shape

    def body(x_ref, tot_ref, fold_ref, all_ref, send_sems, recv_sems, local_sem):
        x, y, c, chips = _place()
        me, sibling = (x, y, c), (x, y, 1 - c)

        def rows(px, py, pc):
            return all_ref.at[pl.ds((4 * px + 2 * py + pc) * M, M), :]

        def copy(k, block, to, src=None):
            return _remote(rows(*block) if src is None else src, rows(*block), send_sems, recv_sems, k, to)

        mine = pltpu.make_async_copy(x_ref, rows(*me), local_sem)
        mine.start()
        first = [copy(0, me, sibling, src=x_ref)]
        first += [copy(1 + j, me, (*chip, c), src=x_ref) for j, chip in enumerate(chips)]
        for cp in first:
            cp.start()
        passed = [copy(4 + j, (*chip, c), sibling) for j, chip in enumerate(chips)]
        for j, chip in enumerate(chips):
            copy(1 + j, (*chip, c), me).wait_recv()
            passed[j].start()
        copy(0, sibling, me).wait_recv()
        for j, chip in enumerate(chips):
            copy(4 + j, (*chip, 1 - c), me).wait_recv()
        for cp in first + passed:
            cp.wait_send()
        mine.wait()
        tot = all_ref[0:M, :]
        for d in range(1, 8):
            tot = tot + all_ref[d * M:(d + 1) * M, :]
        tot_ref[...] = tot
        f = tot[fold_at:fold_at + fold_rows, :]
        for e in range(1, n_fold):
            f = f + tot[fold_at + e * fold_rows:fold_at + (e + 1) * fold_rows, :]
        fold_ref[...] = f

    vm = pl.BlockSpec(memory_space=pltpu.VMEM)
    return pl.pallas_call(
        body, name="all_reduce_small", in_specs=[vm], out_specs=[vm, vm],
        out_shape=[jax.ShapeDtypeStruct((M, N), F32), jax.ShapeDtypeStruct((fold_rows, N), F32)],
        scratch_shapes=[pltpu.VMEM((8 * M, N), F32), pltpu.SemaphoreType.DMA((7,)), pltpu.SemaphoreType.DMA((7,)),
                        pltpu.SemaphoreType.DMA],
        compiler_params=pltpu.CompilerParams(has_side_effects=True, vmem_limit_bytes=VMEM_LIMIT),
    )(v)


def _as_rows(a, width):
    flat = a.reshape(-1)
    pad = (-flat.shape[0]) % width
    if pad:
        flat = jnp.concatenate([flat, jnp.zeros((pad,), flat.dtype)])
    return flat.reshape(-1, width)


class _Layout:
    def __init__(self, width, total_mult):
        self.width, self.total_mult = width, total_mult
        self.offsets, self.shapes, self.rows = {}, {}, 0

    def add(self, name, shape):
        r = -(-math.prod(shape) // self.width)
        self.offsets[name], self.shapes[name] = (self.rows, r), tuple(shape)
        self.rows += r

    def align(self, mult):
        gap = (-self.rows) % mult
        if gap:
            self.offsets[f"_gap{self.rows}"], self.shapes[f"_gap{self.rows}"] = (self.rows, gap), (gap, self.width)
            self.rows += gap
        return self.rows

    def pack(self, pieces):
        self.align(self.total_mult)
        parts = [_as_rows(pieces[n].astype(F32), self.width) if n in pieces else jnp.zeros(self.shapes[n], F32)
                 for n in self.offsets]
        return jnp.concatenate(parts, axis=0)

    def unpack(self, buf, name):
        off, r = self.offsets[name]
        shape = self.shapes[name]
        return buf[off:off + r].reshape(-1)[:math.prod(shape)].reshape(shape)


_BIG = ["ffn1_w1", "ffn1_w3", "ffn1_w2", "w_in", "ssm_glu_a", "ssm_glu_b", "w_out", "ffn2_w1", "ffn2_w3", "ffn2_w2"]
_SMALL = ["ffn1_norm", "mix_norm", "ffn2_norm", "final_norm", "attn_sinks", "ssm_a_re", "ssm_a_im", "ssm_log_step",
          "ssm_b_re", "ssm_b_im", "ssm_c_re", "ssm_c_im", "ssm_d"]
_WEIGHTS = ["meta_tokens", "ffn1_norm", "ffn1_w1", "ffn1_w3", "ffn1_w2", "mix_norm", "w_in", "attn_sinks", "ssm_a_re",
            "ssm_a_im", "ssm_log_step", "ssm_b_re", "ssm_b_im", "ssm_c_re", "ssm_c_im", "ssm_d", "ssm_glu_a",
            "ssm_glu_b", "w_out", "ffn2_norm", "ffn2_w1", "ffn2_w3", "ffn2_w2", "final_norm"]


def _kv_interleave(w, kv_heads):
    kvw = kv_heads * HEAD_DIM
    lead = w.shape[:-1]
    k = w[..., 0:kvw].reshape(lead + (kv_heads, 1, HEAD_DIM))
    v = w[..., kvw:2 * kvw].reshape(lead + (kv_heads, 1, HEAD_DIM))
    return jnp.concatenate([jnp.concatenate([k, v], axis=-2).reshape(lead + (2 * kvw,)), w[..., 2 * kvw:]], axis=-1)


def _kv_deinterleave(w, kv_heads):
    kvw = kv_heads * HEAD_DIM
    lead = w.shape[:-1]
    kv = w[..., 0:2 * kvw].reshape(lead + (kv_heads, 2, HEAD_DIM))
    return jnp.concatenate([kv[..., 0, :].reshape(lead + (kvw,)), kv[..., 1, :].reshape(lead + (kvw,)), w[..., 2 * kvw:]],
                           axis=-1)


def _step(x, target, w, m, v):
    B, S, D = x.shape
    L = S + N_META
    T = B * L
    H = D // HEAD_DIM
    KV = H // Q_PER_KV
    SW = D // 2
    tm = _pick_tile(L, ROW_TILE_CAP, ROW_ALIGN)
    rc = _pick_tile(L, ROW_TILE_CAP // B, 4) * B
    tw = _pick_tile(T, 3 * ROW_TILE_CAP, ROW_ALIGN)
    my_c = lax.axis_index("c")
    my_slot = 2 * lax.axis_index("x") + lax.axis_index("y")

    groups = {"ffn1": ["ffn1_w1", "ffn1_w3", "ffn1_w2"], "mix": ["w_in", "ssm_glu_a", "ssm_glu_b", "w_out"],
              "ffn2": ["ffn2_w1", "ffn2_w3", "ffn2_w2"]}
    waves = {"first": ["ffn1_w1", "ffn1_w3"], "early": ["ffn1_w2"] + groups["mix"], "late": groups["ffn2"]}
    shards = {n: w[n][0].astype(BF16) for n in _BIG}
    gathered = _gather_weights([shards[n] for n in waves["first"]] + [w["meta_tokens"]])
    ws = dict(zip(waves["first"], gathered[:-1]))
    meta = jnp.transpose(gathered[-1], (1, 0, 2)).reshape(N_META, D)

    def arrive(wave, landed):
        mine = [shards[n] for n in waves[wave]]
        ws.update(zip(waves[wave], _forward_halves(landed, mine, wave)))

    g_ffn1, g_mix, g_ffn2 = w["ffn1_norm"], w["mix_norm"], w["ffn2_norm"]
    g_final = w["final_norm"].reshape(1, D)

    h0 = jnp.concatenate([jnp.broadcast_to(meta[None], (B, N_META, D)), x], axis=1).reshape(T, D)

    def ffn_fwd(h, g, tag, carry=None):
        n = _rmsnorm_fwd(h, g, tm, f"{tag}_norm")
        riders = None if carry is None else _gather_riders([shards[k] for k in waves[carry]])
        out = _ffn_up(n, ws[f"{tag}_w1"], ws[f"{tag}_w3"], tm, f"{tag}_up", riders)
        if carry is not None:
            out, landed = out
            arrive(carry, landed)
        a, c, s = out
        return _ffn_down(s, ws[f"{tag}_w2"], h, tm, f"{tag}_down"), (n, a, c, s)

    h1, saved1 = ffn_fwd(h0, g_ffn1, "ffn1", carry="early")
    w_kvu = _kv_interleave(ws["w_in"][1], KV)
    hn = _rmsnorm_fwd(h1, g_mix, tm, "mix_norm")
    q = _mm_colslots(hn, ws["w_in"], BF16, "w_in_q", tm, first=0, count=1, scale=HEAD_DIM ** -0.5)
    kvu = _mm_plain(hn, w_kvu, "nn", F32, "w_in_kvu", tm)
    gates = _mm_colslots(hn, ws["w_in"], F32, "w_in_gates", tm, first=2, count=2)

    sinks = w["attn_sinks"].reshape(KV, Q_PER_KV, 1, 1)
    sink_col = jnp.broadcast_to(sinks, (KV, Q_PER_KV, BLOCK, 1)).reshape(KV, Q_PER_KV * BLOCK, 1)
    sink_meta = jnp.broadcast_to(sinks, (KV, Q_PER_KV, N_META, 1)).reshape(KV, Q_PER_KV * N_META, 1)
    (attn,), landed = _attn_fwd(q, kvu, sink_col, sink_meta, B, "attn_fwd",
                                _gather_riders([shards[k] for k in waves["late"]]))
    arrive("late", landed)

    def to_time_major(a2d):
        return jnp.transpose(a2d.reshape(B, L, a2d.shape[-1]), (1, 0, 2)).reshape(T, a2d.shape[-1])

    def to_batch_major(a2d):
        return jnp.transpose(a2d.reshape(L, B, a2d.shape[-1]), (1, 0, 2)).reshape(T, a2d.shape[-1])

    ssm_args = (w["ssm_a_re"][0], w["ssm_a_im"][0], w["ssm_log_step"][0], w["ssm_b_re"][0], w["ssm_b_im"][0],
                w["ssm_c_re"][0], w["ssm_c_im"][0])
    (lam, bmat, cmat), ssm_vjp = jax.vjp(_ssm_matrices, *ssm_args)
    bmat16, cmat16 = bmat.astype(BF16), cmat.astype(BF16)
    u_t = to_time_major(kvu[:, SW:])
    y_t, xs = _ssm_fwd(u_t, bmat16, cmat16, w["ssm_d"], _scan_tables(lam, B, False), B, rc, "ssm_fwd")
    y0 = to_batch_major(y_t)
    yg = _gelu_fwd(y0, tm, "gelu_fwd")
    ga = _mm_colslots(yg, ws["ssm_glu_a"], F32, "glu_a", tm)
    gb = _mm_colslots(yg, ws["ssm_glu_b"], F32, "glu_b", tm)
    merged = _merge_fwd(gates, attn, ga, gb, tm, "merge_fwd")
    h2 = _mm_rowslots(merged, ws["w_out"], h1, tm, "w_out")
    h3, saved2 = ffn_fwd(h2, g_ffn2, "ffn2")
    dh3, dh3b, dg_final, loss_row = _loss_head(h3, g_final, target, tm, "loss_head")

    grads, swapped, received = {}, {}, {}
    c_idx = my_c.reshape(1).astype(jnp.int32)
    idx = jnp.stack([my_slot, my_c]).astype(jnp.int32)

    def chip_sums(group):
        names = groups[group]
        for n, r in zip(names, _swap_halves([grads[n] for n in names], group)):
            swapped[n] = r
        return [_sum_halves(grads[n], swapped[n], c_idx, f"grad_sum_halves_{n}") for n in names]

    def ffn_bwd(h, g, saved, dh, dhb, tag, carry=None):
        n, a, c, s = saved
        w1, w3, w2 = ws[f"{tag}_w1"], ws[f"{tag}_w3"], ws[f"{tag}_w2"]
        grads[f"{tag}_w2"] = _wgrad_hidden_rows(s, dhb, tw, f"{tag}_dw2", 0.5)
        if carry is None:
            da, dc = _ffn_dhidden(dhb, w2, a, c, tm, f"{tag}_dhidden")
        else:
            (da, dc), got = _ffn_dhidden(dhb, w2, a, c, tm, f"{tag}_dhidden", _exchange_riders(chip_sums(carry)))
            received.update(zip(groups[carry], got))
        grads[f"{tag}_w1"] = _wgrad_hidden_cols(n, da, tw, f"{tag}_dw1")
        grads[f"{tag}_w3"] = _wgrad_hidden_cols(n, dc, tw, f"{tag}_dw3")
        dh_in, dhb_in, grads[f"{tag}_norm"] = _ffn_dn(da, w1, dc, w3, h, g, dh, tm, f"{tag}_dn")
        return dh_in, dhb_in

    dh2, dh2b = ffn_bwd(h2, g_ffn2, saved2, dh3, dh3b, "ffn2")

    grads["w_out"] = _wgrad_rowslots(merged, dh2b, tw, "dw_out")
    dattn, dgat, dgss, dga, dgb = _merge_bwd(dh2b, ws["w_out"], gates, attn, ga, gb, tm, "merge_bwd")
    grads["ssm_glu_a"] = _wgrad_colslots(yg, dga, tw, "dglu_a")
    grads["ssm_glu_b"] = _wgrad_colslots(yg, dgb, tw, "dglu_b")
    dy0 = _gelu_bwd([(dga, ws["ssm_glu_a"]), (dgb, ws["ssm_glu_b"])], y0, tm, "gelu_bwd")
    du_t, dbmat, dcmat, dlam, dd = _ssm_bwd(to_time_major(dy0), u_t, xs, bmat16, cmat16, w["ssm_d"],
                                            _scan_tables(lam, B, True), B, rc, "ssm_bwd")
    d_ssm = ssm_vjp((dlam[:, 0, :], dbmat, dcmat))
    for n, gval in zip(["ssm_a_re", "ssm_a_im", "ssm_log_step", "ssm_b_re", "ssm_b_im", "ssm_c_re", "ssm_c_im"], d_ssm):
        grads[n] = gval[None]
    grads["ssm_d"] = dd

    (dq, dkv, dsink), got = _attn_bwd(q, kvu, attn, dattn, sink_col, sink_meta, B, "attn_bwd",
                                      _exchange_riders(chip_sums("ffn2")))
    received.update(zip(groups["ffn2"], got))
    grads["attn_sinks"] = dsink[:, 0:Q_PER_KV, 0].reshape(1, H)
    dkvu = jnp.concatenate([dkv, to_batch_major(du_t).astype(BF16)], axis=1)
    pieces = [dq, dkvu, dgat, dgss]
    dw_in = [_wgrad_plain(hn, p, f"dw_in_{k}", tw) for k, p in enumerate(pieces)]
    dw_in[1] = _kv_deinterleave(dw_in[1], KV)
    grads["w_in"] = jnp.stack(dw_in)
    w_in_parts = [ws["w_in"][0], w_kvu, ws["w_in"][2], ws["w_in"][3]]
    whole = _once((D, D), lambda i: (0, 0))
    dh1, dh1b, grads["mix_norm"] = _mm_norm_bwd(
        "dhn", "nt", [(p, _spec((tm, D), lambda i: (i, 0)), wp, whole) for p, wp in zip(pieces, w_in_parts)],
        h1, g_mix, dh2, tm)
    dh0, _ = ffn_bwd(h0, g_ffn1, saved1, dh1, dh1b, "ffn1", carry="mix")
    dh0 = dh0.reshape(B, L, D)
    grad_x = dh0[:, N_META:, :]

    grads["final_norm"] = dg_final
    slay = _Layout(D, 8)
    for n in _SMALL:
        slay.add(n, w[n].shape)
    slay.add("loss", (1, D))
    meta_at = slay.align(8)
    slay.add("meta", (B * N_META, D))
    small = slay.pack({**{n: grads[n] for n in _SMALL}, "loss": loss_row, "meta": dh0[:, :N_META, :]})
    tot_small, dmeta = _all_reduce_small(small, B, N_META, meta_at)
    loss = slay.unpack(tot_small, "loss")[0, 0]
    for n in _SMALL:
        grads[n] = slay.unpack(tot_small, n)
    cw = D // N_CHIPS
    grads["meta_tokens"] = lax.dynamic_slice_in_dim(dmeta, my_slot * cw, cw, axis=1)

    received.update(zip(groups["ffn1"], _exchange_chips(chip_sums("ffn1"))))
    fulls = [_sum_chips(grads[n], swapped[n], received[n], idx, f"grad_sum_chips_{n}") for n in _BIG]
    for n, f in zip(_BIG, _join_halves(fulls)):
        grads[n] = f

    delta, new_m, new_v = {}, {}, {}
    for n in _BIG + ["meta_tokens"]:
        delta[n], new_m[n], new_v[n] = _adamw(w[n], grads[n], m[n], v[n], f"adamw_{n}")
        grads[n] = grads[n].reshape(w[n].shape)

    def flat2d(a):
        return a.reshape(-1, a.shape[-1])

    d_, m_, v_ = _adamw_small([flat2d(w[n]) for n in _SMALL], [flat2d(grads[n]) for n in _SMALL],
                              [flat2d(m[n]) for n in _SMALL], [flat2d(v[n]) for n in _SMALL], "adamw_small")
    for i, n in enumerate(_SMALL):
        shp = w[n].shape
        delta[n], new_m[n], new_v[n] = d_[i].reshape(shp), m_[i].reshape(shp), v_[i].reshape(shp)
        grads[n] = grads[n].reshape(shp)

    return (loss, grad_x, *[grads[n] for n in _WEIGHTS], *[delta[n] for n in _WEIGHTS],
            *[new_m[n] for n in _WEIGHTS], *[new_v[n] for n in _WEIGHTS])


def kernel(x, meta_tokens, ffn1_norm, ffn1_w1, ffn1_w3, ffn1_w2, mix_norm, w_in, attn_sinks, ssm_a_re, ssm_a_im, ssm_log_step, ssm_b_re, ssm_b_im, ssm_c_re, ssm_c_im, ssm_d, ssm_glu_a, ssm_glu_b, w_out, ffn2_norm, ffn2_w1, ffn2_w3, ffn2_w2, final_norm, loss_target, m_meta_tokens, m_ffn1_norm, m_ffn1_w1, m_ffn1_w3, m_ffn1_w2, m_mix_norm, m_w_in, m_attn_sinks, m_ssm_a_re, m_ssm_a_im, m_ssm_log_step, m_ssm_b_re, m_ssm_b_im, m_ssm_c_re, m_ssm_c_im, m_ssm_d, m_ssm_glu_a, m_ssm_glu_b, m_w_out, m_ffn2_norm, m_ffn2_w1, m_ffn2_w3, m_ffn2_w2, m_final_norm, v_meta_tokens, v_ffn1_norm, v_ffn1_w1, v_ffn1_w3, v_ffn1_w2, v_mix_norm, v_w_in, v_attn_sinks, v_ssm_a_re, v_ssm_a_im, v_ssm_log_step, v_ssm_b_re, v_ssm_b_im, v_ssm_c_re, v_ssm_c_im, v_ssm_d, v_ssm_glu_a, v_ssm_glu_b, v_w_out, v_ffn2_norm, v_ffn2_w1, v_ffn2_w3, v_ffn2_w2, v_final_norm):
    args = locals()
    w = {n: args[n] for n in _WEIGHTS}
    m = {n: args["m_" + n] for n in _WEIGHTS}
    v = {n: args["v_" + n] for n in _WEIGHTS}
    return _step(x, loss_target, w, m, v)
```

```python
import functools
import math

import jax
import jax.numpy as jnp
from jax import lax
from jax.experimental import pallas as pl
from jax.experimental.pallas import tpu as pltpu

F32 = jnp.float32
BF16 = jnp.bfloat16
MESH_IDS = pl.DeviceIdType.MESH

N_CHIPS = 4
N_META = 16
HEAD_DIM = 64
Q_PER_KV = 4
QW = Q_PER_KV * HEAD_DIM
BLOCK = 128
SSM_GROUP = 16
SSM_STATE = 64
SSM_LANES = 128
GROUPS_PER_COL = SSM_LANES // SSM_GROUP
STATE_LANES = GROUPS_PER_COL * SSM_STATE
NORM_EPS = 1e-6
NEG_INF = -1e30
ADAM_LR, ADAM_B1, ADAM_B2, ADAM_EPS, ADAM_WD, ADAM_STEP = 0.001, 0.9, 0.999, 1e-08, 0.01, 10
GELU_C = math.sqrt(2.0 / math.pi)
ROW_ALIGN = 16
VMEM_LIMIT = 56 * 1024 * 1024
ROW_TILE_CAP = 688

_NN = (((1,), (0,)), ((), ()))
_NT = (((1,), (1,)), ((), ()))
_TN = (((0,), (0,)), ((), ()))
_DIMS = {"nn": _NN, "nt": _NT, "tn": _TN}


def _params(sem, **kw):
    return pltpu.CompilerParams(dimension_semantics=sem, vmem_limit_bytes=VMEM_LIMIT, **kw)


def _pick_tile(n, cap, mult):
    best = None
    for t in range(mult, min(n, cap) + 1, mult):
        if n % t == 0:
            best = t
    if best is None:
        raise ValueError(f"no tile for {n} (cap {cap}, multiple of {mult})")
    return best


def _sigmoid(x):
    return 1.0 / (1.0 + jnp.exp(-x))


def _spec(block, index_map):
    return pl.BlockSpec(block, index_map)


def _sum_dots(ins, mode):
    tot = None
    for p in range(len(ins) // 2):
        a_ref, b_ref = ins[2 * p], ins[2 * p + 1]
        for sl in ([None] if len(b_ref.shape) == 2 else range(b_ref.shape[0])):
            if sl is None:
                a, b = a_ref[...], b_ref[...]
            elif len(a_ref.shape) == 3:
                a, b = a_ref[sl], b_ref[sl]
            else:
                width = a_ref.shape[1] // b_ref.shape[0]
                a, b = a_ref[:, sl * width:(sl + 1) * width], b_ref[sl]
            d = lax.dot_general(a.astype(BF16), b.astype(BF16), _DIMS[mode], preferred_element_type=F32)
            tot = d if tot is None else tot + d
    return tot


def _mm(name, grid, kaxis, mode, pairs, out_shape, out_spec, scale=1.0, res=None):
    npairs = len(pairs)
    has_res = res is not None
    gk = 1 if kaxis is None else grid[kaxis]
    acc_shape = tuple(d for d in out_spec.block_shape if d is not None)

    def body(*refs):
        res_ref = refs[2 * npairs] if has_res else None
        o_ref = refs[2 * npairs + has_res]
        tot = _sum_dots(refs[:2 * npairs], mode)

        def finish(acc):
            r = acc * scale if scale != 1.0 else acc
            if has_res:
                r = res_ref[...] + r
            o_ref[...] = r.astype(o_ref.dtype)

        if gk == 1:
            finish(tot)
        else:
            acc_ref = refs[-1]
            k = pl.program_id(kaxis)

            @pl.when(k == 0)
            def _():
                acc_ref[...] = tot

            @pl.when(k > 0)
            def _():
                acc_ref[...] += tot

            @pl.when(k == gk - 1)
            def _():
                finish(acc_ref[...])

    in_specs, args = [], []
    for a, a_spec, b, b_spec in pairs:
        in_specs += [a_spec, b_spec]
        args += [a, b]
    if has_res:
        in_specs.append(res[1])
        args.append(res[0])
    sem = tuple("arbitrary" if ax == kaxis else "parallel" for ax in range(len(grid)))
    return pl.pallas_call(
        body, name=name, grid=grid, in_specs=in_specs, out_specs=out_spec, out_shape=out_shape,
        scratch_shapes=[pltpu.VMEM(acc_shape, F32)] if gk > 1 else [],
        compiler_params=_params(sem),
    )(*args)


def _mm_plain(a, b, mode, out_dtype, name, tm, scale=1.0):
    M, K = a.shape
    N = b.shape[1] if mode == "nn" else b.shape[0]
    return _mm(name, (M // tm,), None, mode,
               [(a, _spec((tm, K), lambda i: (i, 0)), b, _spec(b.shape, lambda i: (0, 0)))],
               jax.ShapeDtypeStruct((M, N), out_dtype), _spec((tm, N), lambda i: (i, 0)), scale=scale)


def _wgrad_plain(a, b, name, tr):
    R, M = a.shape
    N = b.shape[1]
    return _mm(name, (R // tr,), 0, "tn",
               [(a, _spec((tr, M), lambda r: (r, 0)), b, _spec((tr, N), lambda r: (r, 0)))],
               jax.ShapeDtypeStruct((M, N), F32), _spec((M, N), lambda r: (0, 0)))


def _rmsnorm_fwd(h, g, tm, name):
    T, D = h.shape

    def body(h_ref, g_ref, o_ref):
        x = h_ref[...]
        r = lax.rsqrt(jnp.mean(x * x, axis=-1, keepdims=True) + NORM_EPS)
        o_ref[...] = ((x * r) * g_ref[...]).astype(BF16)

    return pl.pallas_call(
        body, name=name, grid=(T // tm,),
        in_specs=[pl.BlockSpec((tm, D), lambda i: (i, 0)), pl.BlockSpec((1, D), lambda i: (0, 0))],
        out_specs=pl.BlockSpec((tm, D), lambda i: (i, 0)),
        out_shape=jax.ShapeDtypeStruct((T, D), BF16),
        compiler_params=_params(("parallel",)),
    )(h, g)


def _fold8(x):
    return jnp.sum(x.reshape(x.shape[0] // 8, 8, x.shape[1]), axis=0)


def _mm_norm_bwd(name, mode, pairs, h, g, dres, tm, riders=None):
    T, D = h.shape
    nt = T // tm
    npairs = len(pairs)

    def body(*refs):
        h_ref, g_ref, dres_ref, dh_ref, dhb_ref, dg_ref, acc_ref = refs[2 * npairs:]
        i = pl.program_id(0)
        x = h_ref[...]
        r = lax.rsqrt(jnp.mean(x * x, axis=-1, keepdims=True) + NORM_EPS)
        xhat = x * r
        dy = _sum_dots(refs[:2 * npairs], mode)
        dxhat = dy * g_ref[...]
        dx = r * (dxhat - xhat * jnp.mean(dxhat * xhat, axis=-1, keepdims=True))
        dh = dres_ref[...] + dx
        dh_ref[...] = dh
        dhb_ref[...] = dh.astype(BF16)
        part = _fold8(dy * xhat)

        @pl.when(i == 0)
        def _():
            acc_ref[...] = part

        @pl.when(i > 0)
        def _():
            acc_ref[...] += part

        @pl.when(i == nt - 1)
        def _():
            dg_ref[...] = jnp.sum(acc_ref[...], axis=0, keepdims=True)

    row = pl.BlockSpec((tm, D), lambda i: (i, 0))
    vec = pl.BlockSpec((1, D), lambda i: (0, 0))
    in_specs, args = [], []
    for a, a_spec, b, b_spec in pairs:
        in_specs += [a_spec, b_spec]
        args += [a, b]
    return _call(body, name, (nt,), in_specs + [row, vec, row], [row, row, vec],
                 [jax.ShapeDtypeStruct((T, D), F32), jax.ShapeDtypeStruct((T, D), BF16), jax.ShapeDtypeStruct((1, D), F32)],
                 [pltpu.VMEM((8, D), F32)], ("arbitrary",), (*args, h, g, dres), riders)


def _ffn_up(n, w1, w3, tm, name, riders=None):
    T, D = n.shape
    Fs = w1.shape[2]

    def body(n_ref, w1_ref, w3_ref, a_ref, c_ref, s_ref):
        x = n_ref[...]
        a = jnp.dot(x, w1_ref[...], preferred_element_type=F32)
        c = jnp.dot(x, w3_ref[...], preferred_element_type=F32)
        a_ref[...] = a.astype(BF16)
        c_ref[...] = c.astype(BF16)
        s_ref[...] = (a * _sigmoid(a) * c).astype(BF16)

    w_spec = _spec((None, D, Fs), lambda s, i: (s, 0, 0))
    o_spec = _spec((None, tm, Fs), lambda s, i: (s, i, 0))
    o_shape = jax.ShapeDtypeStruct((N_CHIPS, T, Fs), BF16)
    return _call(body, name, (N_CHIPS, T // tm), [_spec((tm, D), lambda s, i: (i, 0)), w_spec, w_spec],
                 [o_spec, o_spec, o_spec], [o_shape, o_shape, o_shape], [], ("parallel", "parallel"), (n, w1, w3), riders)


def _ffn_down(s, w2, h, tm, name):
    _, T, Fs = s.shape
    D = w2.shape[2]
    row = _spec((tm, D), lambda i: (i, 0))
    return _mm(name, (T // tm,), None, "nn",
               [(s, _spec((N_CHIPS, tm, Fs), lambda i: (0, i, 0)), w2, _spec((N_CHIPS, Fs, D), lambda i: (0, 0, 0)))],
               jax.ShapeDtypeStruct((T, D), F32), row, scale=0.5, res=(h, row))


def _ffn_dhidden(dhb, w2, a, c, tm, name, riders=None):
    T, D = dhb.shape
    Fs = w2.shape[1]

    def body(dh_ref, w2_ref, a_ref, c_ref, da_ref, dc_ref):
        d = 0.5 * lax.dot_general(dh_ref[...], w2_ref[...], _NT, preferred_element_type=F32)
        av = a_ref[...].astype(F32)
        cv = c_ref[...].astype(F32)
        sg = _sigmoid(av)
        da_ref[...] = (d * cv * (sg * (1.0 + av * (1.0 - sg)))).astype(BF16)
        dc_ref[...] = (d * (av * sg)).astype(BF16)

    h_spec = _spec((None, tm, Fs), lambda s, i: (s, i, 0))
    o_shape = jax.ShapeDtypeStruct((N_CHIPS, T, Fs), BF16)
    return _call(body, name, (N_CHIPS, T // tm),
                 [_spec((tm, D), lambda s, i: (i, 0)), _spec((None, Fs, D), lambda s, i: (s, 0, 0)), h_spec, h_spec],
                 [h_spec, h_spec], [o_shape, o_shape], [], ("parallel", "parallel"), (dhb, w2, a, c), riders)


def _wgrad_hidden_rows(s, dhb, tr, name, scale):
    _, T, Fs = s.shape
    D = dhb.shape[1]
    return _mm(name, (N_CHIPS, T // tr), 1, "tn",
               [(s, _spec((None, tr, Fs), lambda k, r: (k, r, 0)), dhb, _spec((tr, D), lambda k, r: (r, 0)))],
               jax.ShapeDtypeStruct((N_CHIPS, Fs, D), F32), _spec((None, Fs, D), lambda k, r: (k, 0, 0)), scale=scale)


def _wgrad_hidden_cols(n, da, tr, name):
    T, D = n.shape
    Fs = da.shape[2]
    return _mm(name, (N_CHIPS, T // tr), 1, "tn",
               [(n, _spec((tr, D), lambda k, r: (r, 0)), da, _spec((None, tr, Fs), lambda k, r: (k, r, 0)))],
               jax.ShapeDtypeStruct((N_CHIPS, D, Fs), F32), _spec((None, D, Fs), lambda k, r: (k, 0, 0)))


def _once(block, index_map):
    return pl.BlockSpec(block, index_map, pipeline_mode=pl.Buffered(1))


def _ffn_dn(da, w1, dc, w3, h, g, dres, tm, name, riders=None):
    _, T, Fs = da.shape
    D = w1.shape[1]
    h_spec = _spec((N_CHIPS, tm, Fs), lambda i: (0, i, 0))
    w_spec = _once((N_CHIPS, D, Fs), lambda i: (0, 0, 0))
    return _mm_norm_bwd(name, "nt", [(da, h_spec, w1, w_spec), (dc, h_spec, w3, w_spec)], h, g, dres, tm, riders)


def _mm_side_by_side(a, w, mode, out_dtype, name, tm, first=0, count=N_CHIPS, scale=1.0):
    T, K = a.shape
    assert first % count == 0
    n = w.shape[2] if mode == "nn" else w.shape[1]

    def body(a_ref, w_ref, o_ref):
        av = a_ref[...].astype(BF16)
        for j in range(count):
            r = lax.dot_general(av, w_ref[j].astype(BF16), _DIMS[mode], preferred_element_type=F32)
            o_ref[:, j * n:(j + 1) * n] = (r * scale if scale != 1.0 else r).astype(o_ref.dtype)

    return pl.pallas_call(
        body, name=name, grid=(T // tm,),
        in_specs=[_spec((tm, K), lambda i: (i, 0)), _once((count,) + w.shape[1:], lambda i: (first // count, 0, 0))],
        out_specs=_spec((tm, count * n), lambda i: (i, 0)),
        out_shape=jax.ShapeDtypeStruct((T, count * n), out_dtype),
        compiler_params=_params(("parallel",)),
    )(a, w)


def _mm_colslots(a, w, out_dtype, name, tm, first=0, count=N_CHIPS, scale=1.0):
    return _mm_side_by_side(a, w, "nn", out_dtype, name, tm, first, count, scale)


def _wgrad_colslots(a, d, tr, name):
    T, K = a.shape
    Ns = d.shape[1] // N_CHIPS
    return _mm(name, (N_CHIPS, T // tr), 1, "tn",
               [(a, _spec((tr, K), lambda k, r: (r, 0)), d, _spec((tr, Ns), lambda k, r: (r, k)))],
               jax.ShapeDtypeStruct((N_CHIPS, K, Ns), F32), _spec((None, K, Ns), lambda k, r: (k, 0, 0)))


def _mm_rowslots(a, w, h, tm, name):
    T = a.shape[0]
    N = w.shape[2]
    row = _spec((tm, N), lambda i: (i, 0))
    return _mm(name, (T // tm,), None, "nn",
               [(a, _spec((tm, a.shape[1]), lambda i: (i, 0)), w, _once(w.shape, lambda i: (0, 0, 0)))],
               jax.ShapeDtypeStruct((T, N), F32), row, res=(h, row))


def _wgrad_rowslots(a, d, tr, name):
    T = a.shape[0]
    Ks = a.shape[1] // N_CHIPS
    N = d.shape[1]
    return _mm(name, (N_CHIPS, T // tr), 1, "tn",
               [(a, _spec((tr, Ks), lambda k, r: (r, k)), d, _spec((tr, N), lambda k, r: (r, 0)))],
               jax.ShapeDtypeStruct((N_CHIPS, Ks, N), F32), _spec((None, Ks, N), lambda k, r: (k, 0, 0)))


def _gelu_parts(x):
    inner = GELU_C * (x + 0.044715 * (x * x * x))
    t = jnp.tanh(inner)
    return t, GELU_C * (1.0 + 3.0 * 0.044715 * (x * x))


def _gelu_fwd(y, tm, name):
    T, W = y.shape

    def body(y_ref, o_ref):
        x = y_ref[...]
        t, _ = _gelu_parts(x)
        o_ref[...] = (0.5 * x * (1.0 + t)).astype(BF16)

    spec = pl.BlockSpec((tm, W), lambda i: (i, 0))
    return pl.pallas_call(body, name=name, grid=(T // tm,), in_specs=[spec], out_specs=spec,
                          out_shape=jax.ShapeDtypeStruct((T, W), BF16),
                          compiler_params=_params(("parallel",)))(y)


def _gelu_bwd(pairs, y, tm, name):
    T, W = y.shape
    npairs = len(pairs)

    def body(*refs):
        y_ref, o_ref = refs[2 * npairs], refs[2 * npairs + 1]
        x = y_ref[...]
        t, dinner = _gelu_parts(x)
        o_ref[...] = _sum_dots(refs[:2 * npairs], "nt") * (0.5 * (1.0 + t) + 0.5 * x * (1.0 - t * t) * dinner)

    spec = pl.BlockSpec((tm, W), lambda i: (i, 0))
    in_specs, args = [], []
    for d, w in pairs:
        in_specs += [_spec((tm, d.shape[1]), lambda i: (i, 0)), _once(w.shape, lambda i: (0, 0, 0))]
        args += [d, w]
    return pl.pallas_call(body, name=name, grid=(T // tm,), in_specs=in_specs + [spec], out_specs=spec,
                          out_shape=jax.ShapeDtypeStruct((T, W), F32),
                          compiler_params=_params(("parallel",)))(*args, y)


def _merge_cols(D):
    cb = 512 if D % 512 == 0 else D
    return cb, D // cb


def _merge_fwd(gates, attn, ga, gb, tm, name):
    T, D = attn.shape
    cb, nc = _merge_cols(D)

    def body(gat_ref, gss_ref, attn_ref, ga_ref, gb_ref, o_ref):
        ssm = ga_ref[...] * _sigmoid(gb_ref[...])
        o_ref[...] = (_sigmoid(gat_ref[...]) * attn_ref[...] + _sigmoid(gss_ref[...]) * ssm).astype(BF16)

    def col(block):
        return pl.BlockSpec((tm, cb), lambda i, j: (i, block * nc + j))

    return pl.pallas_call(
        body, name=name, grid=(T // tm, nc),
        in_specs=[col(0), col(1), col(0), col(0), col(0)],
        out_specs=col(0), out_shape=jax.ShapeDtypeStruct((T, D), BF16),
        compiler_params=_params(("parallel", "parallel")),
    )(gates, gates, attn, ga, gb)


def _merge_bwd(dhb, w_out, gates, attn, ga, gb, tm, name):
    T, D = attn.shape
    cb, nc = _merge_cols(D)
    Ks = w_out.shape[1]
    spb = cb // Ks

    def body(dh_ref, w_ref, gat_ref, gss_ref, attn_ref, ga_ref, gb_ref, dattn_ref, dgat_ref, dgss_ref, dga_ref, dgb_ref):
        dh = dh_ref[...]
        d = jnp.concatenate([lax.dot_general(dh, w_ref[s], _NT, preferred_element_type=F32) for s in range(spb)], axis=1)
        sa = _sigmoid(gat_ref[...])
        ss = _sigmoid(gss_ref[...])
        sb = _sigmoid(gb_ref[...])
        gav = ga_ref[...]
        dattn_ref[...] = d * sa
        dgat_ref[...] = (d * attn_ref[...] * (sa * (1.0 - sa))).astype(BF16)
        dgss_ref[...] = (d * (gav * sb) * (ss * (1.0 - ss))).astype(BF16)
        dssm = d * ss
        dga_ref[...] = (dssm * sb).astype(BF16)
        dgb_ref[...] = (dssm * gav * (sb * (1.0 - sb))).astype(BF16)

    def col(block):
        return pl.BlockSpec((tm, cb), lambda i, j: (i, block * nc + j))

    b16 = jax.ShapeDtypeStruct((T, D), BF16)
    return pl.pallas_call(
        body, name=name, grid=(T // tm, nc),
        in_specs=[pl.BlockSpec((tm, D), lambda i, j: (i, 0)), pl.BlockSpec((spb, Ks, D), lambda i, j: (j, 0, 0)),
                  col(0), col(1), col(0), col(0), col(0)],
        out_specs=[col(0)] * 5,
        out_shape=[jax.ShapeDtypeStruct((T, D), F32), b16, b16, b16, b16],
        compiler_params=_params(("parallel", "parallel")),
    )(dhb, w_out, gates, gates, attn, ga, gb)


def _loss_head(h, g, target, tm, name):
    T, D = h.shape
    B, S, _ = target.shape
    L = S + N_META
    nt = T // tm
    tpe = L // tm

    def body(h_ref, g_ref, t_hbm, dh_ref, dhb_ref, dg_ref, loss_ref, tbuf, acc_g, acc_l, sem):
        i = pl.program_id(0)
        b, j = i // tpe, i % tpe

        @pl.when(j == 0)
        def _():
            tbuf[0:N_META, :] = jnp.zeros((N_META, D), F32)
            cp = pltpu.make_async_copy(t_hbm.at[b, pl.ds(0, tm - N_META), :], tbuf.at[pl.ds(N_META, tm - N_META), :], sem)
            cp.start()
            cp.wait()

        @pl.when(j > 0)
        def _():
            cp = pltpu.make_async_copy(t_hbm.at[b, pl.ds(j * tm - N_META, tm), :], tbuf, sem)
            cp.start()
            cp.wait()

        x = h_ref[...]
        gv = g_ref[...]
        r = lax.rsqrt(jnp.mean(x * x, axis=-1, keepdims=True) + NORM_EPS)
        xhat = x * r
        pos = j * tm + lax.broadcasted_iota(jnp.int32, (tm, 1), 0)
        err = jnp.where(pos >= N_META, xhat * gv - tbuf[...], 0.0)
        dy = err * (1.0 / D)
        dxhat = dy * gv
        dh = r * (dxhat - xhat * jnp.mean(dxhat * xhat, axis=-1, keepdims=True))
        dh_ref[...] = dh
        dhb_ref[...] = dh.astype(BF16)
        pg = _fold8(dy * xhat)
        pe = _fold8(err * err)

        @pl.when(i == 0)
        def _():
            acc_g[...] = pg
            acc_l[...] = pe

        @pl.when(i > 0)
        def _():
            acc_g[...] += pg
            acc_l[...] += pe

        @pl.when(i == nt - 1)
        def _():
            dg_ref[...] = jnp.sum(acc_g[...], axis=0, keepdims=True)
            loss_ref[...] = jnp.full((1, D), (0.5 / D) * jnp.sum(acc_l[...]), F32)

    row = pl.BlockSpec((tm, D), lambda i: (i, 0))
    vec = pl.BlockSpec((1, D), lambda i: (0, 0))
    return pl.pallas_call(
        body, name=name, grid=(nt,),
        in_specs=[row, vec, pl.BlockSpec(memory_space=pl.ANY)], out_specs=[row, row, vec, vec],
        out_shape=[jax.ShapeDtypeStruct((T, D), F32), jax.ShapeDtypeStruct((T, D), BF16),
                   jax.ShapeDtypeStruct((1, D), F32), jax.ShapeDtypeStruct((1, D), F32)],
        scratch_shapes=[pltpu.VMEM((tm, D), F32), pltpu.VMEM((8, D), F32), pltpu.VMEM((8, D), F32),
                        pltpu.SemaphoreType.DMA],
        compiler_params=_params(("arbitrary",)),
    )(h, g, target)


def _heads_to_rows(blk):
    return jnp.concatenate([blk[:, g * HEAD_DIM:(g + 1) * HEAD_DIM] for g in range(Q_PER_KV)], axis=0)


def _rows_to_heads(x):
    rows = x.shape[0] // Q_PER_KV
    return jnp.concatenate([x[g * rows:(g + 1) * rows] for g in range(Q_PER_KV)], axis=1)


def _causal(R):
    qi = lax.broadcasted_iota(jnp.int32, (R, BLOCK), 0) & (BLOCK - 1)
    kj = lax.broadcasted_iota(jnp.int32, (R, BLOCK), 1)
    return kj <= qi


def _band_probs(s_band, s_m, sink):
    m = jnp.maximum(jnp.maximum(jnp.max(s_band, axis=-1, keepdims=True), jnp.max(s_m, axis=-1, keepdims=True)), sink)
    e_b, e_m, e_s = jnp.exp(s_band - m), jnp.exp(s_m - m), jnp.exp(sink - m)
    inv = 1.0 / (jnp.sum(e_b, axis=-1, keepdims=True) + jnp.sum(e_m, axis=-1, keepdims=True) + e_s)
    return e_b * inv, e_m * inv, e_s * inv


def _fold_band(tri, two):
    return jnp.where(tri, two[:, BLOCK:2 * BLOCK], two[:, 0:BLOCK])


def _unfold_band(tri, band):
    return jnp.concatenate([jnp.where(tri, 0.0, band), jnp.where(tri, band, 0.0)], axis=1)


def _meta_probs(qm, k_m, sink_m):
    R = qm.shape[0]
    s = lax.dot_general(qm, k_m, _NT, preferred_element_type=F32)
    qi = lax.broadcasted_iota(jnp.int32, (R, N_META), 0) & (N_META - 1)
    kj = lax.broadcasted_iota(jnp.int32, (R, N_META), 1)
    s = jnp.where(kj <= qi, s, NEG_INF)
    m = jnp.maximum(jnp.max(s, axis=-1, keepdims=True), sink_m)
    e, e_s = jnp.exp(s - m), jnp.exp(sink_m - m)
    inv = 1.0 / (jnp.sum(e, axis=-1, keepdims=True) + e_s)
    return e * inv, e_s * inv


def _block_start(n):
    return pl.multiple_of(N_META + n * BLOCK, ROW_ALIGN)


def _kv(blk):
    return blk[:, 0:HEAD_DIM], blk[:, HEAD_DIM:2 * HEAD_DIM]


def _attn_fwd(q, kv, sink_col, sink_meta, B, name, riders=None):
    T, D = q.shape
    L = T // B
    KV = D // QW
    nb = (L - N_META) // BLOCK

    def body(q_ref, kv_ref, sk_ref, skm_ref, o_ref, kvs):
        kvs[...] = kv_ref[...].astype(BF16)
        k_m, v_m = _kv(kvs[0:N_META, :])
        p, _ = _meta_probs(_heads_to_rows(q_ref[0:N_META, :]), k_m, skm_ref[0])
        o_ref[0:N_META, :] = _rows_to_heads(jnp.dot(p.astype(BF16), v_m, preferred_element_type=F32))
        tri = _causal(BLOCK)

        def block(cur, first, keys):
            k2, v2 = _kv(kvs[keys, :])
            qb = _heads_to_rows(q_ref[pl.ds(cur, BLOCK), :])
            s2 = lax.dot_general(qb, k2, _NT, preferred_element_type=F32)
            sm = lax.dot_general(qb, k_m, _NT, preferred_element_type=F32)
            p2s, pms = [], []
            for g in range(Q_PER_KV):
                sl = slice(g * BLOCK, (g + 1) * BLOCK)
                s_band = jnp.where(tri, s2[sl], NEG_INF) if first else _fold_band(tri, s2[sl])
                p_b, p_m, _ = _band_probs(s_band, sm[sl], sk_ref[0, sl, :])
                p2s.append((p_b if first else _unfold_band(tri, p_b)).astype(BF16))
                pms.append(p_m.astype(BF16))
            o = (jnp.dot(jnp.concatenate(p2s, axis=0), v2, preferred_element_type=F32)
                 + jnp.dot(jnp.concatenate(pms, axis=0), v_m, preferred_element_type=F32))
            o_ref[pl.ds(cur, BLOCK), :] = _rows_to_heads(o)

        block(N_META, True, pl.ds(N_META, BLOCK))

        def step(n, carry):
            block(_block_start(n), False, pl.ds(_block_start(n - 1), 2 * BLOCK))
            return carry

        lax.fori_loop(1, nb, step, 0)

    q_spec = pl.BlockSpec((L, QW), lambda b, h: (b, h))
    return _call(body, name, (B, KV),
                 [q_spec, pl.BlockSpec((L, 2 * HEAD_DIM), lambda b, h: (b, h)),
                  pl.BlockSpec((1, Q_PER_KV * BLOCK, 1), lambda b, h: (h, 0, 0)),
                  pl.BlockSpec((1, Q_PER_KV * N_META, 1), lambda b, h: (h, 0, 0))],
                 [q_spec], [jax.ShapeDtypeStruct((T, D), F32)], [pltpu.VMEM((L, 2 * HEAD_DIM), BF16)],
                 ("parallel", "parallel"), (q, kv, sink_col, sink_meta), riders)


def _attn_bwd(q, kv, o, do, sink_col, sink_meta, B, name, riders=None):
    T, D = q.shape
    L = T // B
    KV = D // QW
    nb = (L - N_META) // BLOCK
    R = Q_PER_KV * BLOCK
    scale = HEAD_DIM ** -0.5

    def head_totals(col, rows_per_head):
        rid = lax.broadcasted_iota(jnp.int32, (8, 128), 0)
        out = jnp.zeros((8, 128), F32)
        for g in range(Q_PER_KV):
            out = out + jnp.where(rid == g, jnp.sum(col[g * rows_per_head:(g + 1) * rows_per_head, :]), 0.0)
        return out

    def body(q_ref, kv_ref, o_ref, do_ref, sk_ref, skm_ref, dq_ref, dkv_ref, dsk_ref, kvs, acc, acc_sink):
        b = pl.program_id(1)
        kvs[...] = kv_ref[...].astype(BF16)
        acc[...] = jnp.zeros_like(acc)
        k_m, v_m = _kv(kvs[0:N_META, :])

        qm = _heads_to_rows(q_ref[0:N_META, :])
        dom = _heads_to_rows(do_ref[0:N_META, :])
        delta = jnp.sum(dom * _heads_to_rows(o_ref[0:N_META, :]), axis=-1, keepdims=True)
        p, p_s = _meta_probs(qm, k_m, skm_ref[0])
        domb = dom.astype(BF16)
        ds = (p * (lax.dot_general(domb, v_m, _NT, preferred_element_type=F32) - delta)).astype(BF16)
        dq_ref[0:N_META, :] = _rows_to_heads(jnp.dot(ds, k_m, preferred_element_type=F32) * scale).astype(BF16)
        acc[0:N_META, :] += jnp.concatenate([lax.dot_general(ds, qm, _TN, preferred_element_type=F32),
                                             lax.dot_general(p.astype(BF16), domb, _TN, preferred_element_type=F32)], axis=1)
        sink_tot = head_totals(-p_s * delta, N_META)
        tri = _causal(BLOCK)
        acc_sink[...] = jnp.zeros_like(acc_sink)

        def block(cur, first, keys):
            k2, v2 = _kv(kvs[keys, :])
            rows = pl.ds(cur, BLOCK)
            qb = _heads_to_rows(q_ref[rows, :])
            dob = _heads_to_rows(do_ref[rows, :])
            delta = jnp.sum(dob * _heads_to_rows(o_ref[rows, :]), axis=-1, keepdims=True)
            dobb = dob.astype(BF16)
            s2 = lax.dot_general(qb, k2, _NT, preferred_element_type=F32)
            sm = lax.dot_general(qb, k_m, _NT, preferred_element_type=F32)
            dp2 = lax.dot_general(dobb, v2, _NT, preferred_element_type=F32)
            dpm = lax.dot_general(dobb, v_m, _NT, preferred_element_type=F32)
            ds2s, p2s, dsms, pms = [], [], [], []
            for g in range(Q_PER_KV):
                sl = slice(g * BLOCK, (g + 1) * BLOCK)
                s_band = jnp.where(tri, s2[sl], NEG_INF) if first else _fold_band(tri, s2[sl])
                p_b, p_m, p_s = _band_probs(s_band, sm[sl], sk_ref[0, sl, :])
                ds_b = p_b * ((dp2[sl] if first else _fold_band(tri, dp2[sl])) - delta[sl])
                ds2s.append((ds_b if first else _unfold_band(tri, ds_b)).astype(BF16))
                p2s.append((p_b if first else _unfold_band(tri, p_b)).astype(BF16))
                dsms.append((p_m * (dpm[sl] - delta[sl])).astype(BF16))
                pms.append(p_m.astype(BF16))
                acc_sink[sl, :] += -p_s * delta[sl]
            ds2, p2 = jnp.concatenate(ds2s, axis=0), jnp.concatenate(p2s, axis=0)
            dsm, pm = jnp.concatenate(dsms, axis=0), jnp.concatenate(pms, axis=0)
            dq = jnp.dot(ds2, k2, preferred_element_type=F32) + jnp.dot(dsm, k_m, preferred_element_type=F32)
            dq_ref[rows, :] = _rows_to_heads(dq * scale).astype(BF16)
            acc[keys, :] += jnp.concatenate([lax.dot_general(ds2, qb, _TN, preferred_element_type=F32),
                                             lax.dot_general(p2, dobb, _TN, preferred_element_type=F32)], axis=1)
            acc[0:N_META, :] += jnp.concatenate([lax.dot_general(dsm, qb, _TN, preferred_element_type=F32),
                                                 lax.dot_general(pm, dobb, _TN, preferred_element_type=F32)], axis=1)

        block(N_META, True, pl.ds(N_META, BLOCK))

        def step(n, carry):
            block(_block_start(n), False, pl.ds(_block_start(n - 1), 2 * BLOCK))
            return carry

        lax.fori_loop(1, nb, step, 0)
        dkv_ref[...] = acc[...].astype(BF16)
        tot = sink_tot + head_totals(acc_sink[...], BLOCK)

        @pl.when(b == 0)
        def _():
            dsk_ref[0] = tot

        @pl.when(b > 0)
        def _():
            dsk_ref[0] += tot

    q_spec = pl.BlockSpec((L, QW), lambda h, b: (b, h))
    kv_spec = pl.BlockSpec((L, 2 * HEAD_DIM), lambda h, b: (b, h))
    return _call(body, name, (KV, B),
                 [q_spec, kv_spec, q_spec, q_spec,
                  pl.BlockSpec((1, R, 1), lambda h, b: (h, 0, 0)),
                  pl.BlockSpec((1, Q_PER_KV * N_META, 1), lambda h, b: (h, 0, 0))],
                 [q_spec, kv_spec, pl.BlockSpec((1, 8, 128), lambda h, b: (h, 0, 0))],
                 [jax.ShapeDtypeStruct((T, D), BF16), jax.ShapeDtypeStruct((T, KV * 2 * HEAD_DIM), BF16),
                  jax.ShapeDtypeStruct((KV, 8, 128), F32)],
                 [pltpu.VMEM((L, 2 * HEAD_DIM), BF16), pltpu.VMEM((L, 2 * HEAD_DIM), F32), pltpu.VMEM((R, 1), F32)],
                 ("parallel", "arbitrary"), (q, kv, o, do, sink_col, sink_meta), riders)


def _cmul_add(acc_r, acc_i, lr, li, xr, xi):
    return acc_r + (lr * xr - li * xi), acc_i + (lr * xi + li * xr)


def _cols_per_step(ncol):
    for cps in (4, 2):
        if ncol % cps == 0:
            return cps
    return 1


def _ssm_fwd(u, bmat, cmat, dskip, tables, nbatch, rc, name):
    T, W = u.shape
    ncol = W // SSM_LANES
    nch = T // rc
    S = STATE_LANES
    cps = _cols_per_step(ncol)
    assert nbatch == 4

    def body(u_ref, b_ref, c_ref, d_ref, tab_ref, y_ref, xs_ref, st_ref, carry_ref):
        ch = pl.program_id(1)

        @pl.when(ch == 0)
        def _():
            carry_ref[...] = jnp.zeros_like(carry_ref)

        uv = u_ref[...]
        for k in range(cps):
            st_ref[:, 2 * S * k:2 * S * (k + 1)] = jnp.dot(uv[:, SSM_LANES * k:SSM_LANES * (k + 1)].astype(BF16), b_ref[k],
                                                           preferred_element_type=F32)
        low = lax.broadcasted_iota(jnp.int32, (8, S), 0) < nbatch

        def tile(k, r0, c_r, c_i):
            re, im = slice(2 * S * k, 2 * S * k + S), slice(2 * S * k + S, 2 * S * (k + 1))
            la_r, la_i = tab_ref[k, :, 0:S], tab_ref[k, :, S:2 * S]
            lb_r, lb_i = tab_ref[k, :, 2 * S:3 * S], tab_ref[k, :, 3 * S:4 * S]
            v_r = st_ref[pl.ds(r0, 8), re]
            v_i = st_ref[pl.ds(r0, 8), im]
            v_r, v_i = _cmul_add(v_r, v_i, la_r, la_i, pltpu.roll(v_r, nbatch, 0), pltpu.roll(v_i, nbatch, 0))
            rc_r, rc_i = pltpu.roll(c_r, nbatch, 0), pltpu.roll(c_i, nbatch, 0)
            cb_r, cb_i = jnp.where(low, rc_r, c_r), jnp.where(low, rc_i, c_i)
            v_r, v_i = _cmul_add(v_r, v_i, lb_r, lb_i, cb_r, cb_i)
            st_ref[pl.ds(r0, 8), re] = v_r
            st_ref[pl.ds(r0, 8), im] = v_i
            return v_r, v_i

        def step(i, carry):
            r0 = pl.multiple_of(i * 8, 8)
            out = []
            for k in range(cps):
                out += list(tile(k, r0, carry[2 * k], carry[2 * k + 1]))
            return tuple(out)

        halves = tuple(carry_ref[:, S * j:S * (j + 1)] for j in range(2 * cps))
        halves = lax.fori_loop(0, rc // 8, step, halves)
        for j in range(2 * cps):
            carry_ref[:, S * j:S * (j + 1)] = halves[j]
        xb = st_ref[...].astype(BF16)
        xs_ref[...] = xb
        for k in range(cps):
            cols = slice(SSM_LANES * k, SSM_LANES * (k + 1))
            y_ref[:, cols] = (jnp.dot(xb[:, 2 * S * k:2 * S * (k + 1)], c_ref[k], preferred_element_type=F32)
                              + d_ref[:, cols] * uv[:, cols])

    return pl.pallas_call(
        body, name=name, grid=(ncol // cps, nch),
        in_specs=[pl.BlockSpec((rc, cps * SSM_LANES), lambda g, c: (c, g)),
                  pl.BlockSpec((cps, SSM_LANES, 2 * S), lambda g, c: (g, 0, 0)),
                  pl.BlockSpec((cps, 2 * S, SSM_LANES), lambda g, c: (g, 0, 0)),
                  pl.BlockSpec((1, cps * SSM_LANES), lambda g, c: (0, g)),
                  pl.BlockSpec((cps, 8, 4 * S), lambda g, c: (g, 0, 0))],
        out_specs=[pl.BlockSpec((rc, cps * SSM_LANES), lambda g, c: (c, g)),
                   pl.BlockSpec((rc, cps * 2 * S), lambda g, c: (c, g))],
        out_shape=[jax.ShapeDtypeStruct((T, W), F32), jax.ShapeDtypeStruct((T, ncol * 2 * S), BF16)],
        scratch_shapes=[pltpu.VMEM((rc, cps * 2 * S), F32), pltpu.VMEM((8, cps * 2 * S), F32)],
        compiler_params=_params(("parallel", "arbitrary")),
    )(u, bmat, cmat, dskip, tables)


def _ssm_bwd(dy, u, xs, bmat, cmat, dskip, tables, nbatch, rc, name):
    T, W = u.shape
    ncol = W // SSM_LANES
    nch = T // rc
    S = STATE_LANES
    ntile = rc // 16
    cps = _cols_per_step(ncol)

    def body(dy_ref, u_ref, xs_ref, b_ref, c_ref, d_ref, tab_ref,
             du_ref, db_ref, dc_ref, dl_ref, dd_ref, st_ref, carry_ref, accl_ref, accd_ref):
        ch = pl.program_id(1)

        @pl.when(ch == 0)
        def _():
            carry_ref[...] = jnp.zeros_like(carry_ref)
            accl_ref[...] = jnp.zeros_like(accl_ref)
            accd_ref[...] = jnp.zeros_like(accd_ref)
            db_ref[...] = jnp.zeros_like(db_ref)
            dc_ref[...] = jnp.zeros_like(dc_ref)

        dyv = dy_ref[...]
        uv = u_ref[...]
        dyb = dyv.astype(BF16)
        for k in range(cps):
            st_ref[:, 2 * S * k:2 * S * (k + 1)] = lax.dot_general(dyb[:, SSM_LANES * k:SSM_LANES * (k + 1)], c_ref[k], _NT,
                                                                   preferred_element_type=F32)
        low = lax.broadcasted_iota(jnp.int32, (8, S), 0) < nbatch

        def tile(k, r0, x_r, x_i, c_r, c_i, al_r, al_i):
            re, im = slice(2 * S * k, 2 * S * k + S), slice(2 * S * k + S, 2 * S * (k + 1))
            la_r, la_i = tab_ref[k, :, 0:S], tab_ref[k, :, S:2 * S]
            lb_r, lb_i = tab_ref[k, :, 2 * S:3 * S], tab_ref[k, :, 3 * S:4 * S]
            v_r = st_ref[pl.ds(r0, 8), re]
            v_i = st_ref[pl.ds(r0, 8), im]
            v_r, v_i = _cmul_add(v_r, v_i, la_r, la_i, pltpu.roll(v_r, nbatch, 0), pltpu.roll(v_i, nbatch, 0))
            cb_r = jnp.where(low, c_r, pltpu.roll(c_r, nbatch, 0))
            cb_i = jnp.where(low, c_i, pltpu.roll(c_i, nbatch, 0))
            v_r, v_i = _cmul_add(v_r, v_i, lb_r, lb_i, cb_r, cb_i)
            st_ref[pl.ds(r0, 8), re] = v_r
            st_ref[pl.ds(r0, 8), im] = v_i
            n_r = jnp.where(low, pltpu.roll(v_r, nbatch, 0), cb_r)
            n_i = jnp.where(low, pltpu.roll(v_i, nbatch, 0), cb_i)
            al_r = al_r + (n_r * x_r + n_i * x_i)
            al_i = al_i + (n_i * x_r - n_r * x_i)
            return v_r, v_i, al_r, al_i

        def step(j, carry):
            r0 = pl.multiple_of((ntile - 1 - j) * 16, 16)
            out = []
            for k in range(cps):
                re, im = slice(2 * S * k, 2 * S * k + S), slice(2 * S * k + S, 2 * S * (k + 1))
                x_r = xs_ref[pl.ds(r0, 16), re].astype(F32)
                x_i = xs_ref[pl.ds(r0, 16), im].astype(F32)
                mid = tile(k, r0 + 8, x_r[8:16], x_i[8:16], *carry[4 * k:4 * k + 4])
                out += list(tile(k, r0, x_r[0:8], x_i[0:8], *mid))
            return tuple(out)

        init = []
        for k in range(cps):
            init += [carry_ref[:, 2 * S * k:2 * S * k + S], carry_ref[:, 2 * S * k + S:2 * S * (k + 1)],
                     accl_ref[:, 2 * S * k:2 * S * k + S], accl_ref[:, 2 * S * k + S:2 * S * (k + 1)]]
        fin = lax.fori_loop(0, ntile, step, tuple(init))
        for k in range(cps):
            carry_ref[:, 2 * S * k:2 * S * k + S] = fin[4 * k]
            carry_ref[:, 2 * S * k + S:2 * S * (k + 1)] = fin[4 * k + 1]
            accl_ref[:, 2 * S * k:2 * S * k + S] = fin[4 * k + 2]
            accl_ref[:, 2 * S * k + S:2 * S * (k + 1)] = fin[4 * k + 3]
        dsb = st_ref[...].astype(BF16)
        ub = uv.astype(BF16)
        for k in range(cps):
            cols, lanes = slice(SSM_LANES * k, SSM_LANES * (k + 1)), slice(2 * S * k, 2 * S * (k + 1))
            du_ref[:, cols] = (lax.dot_general(dsb[:, lanes], b_ref[k], _NT, preferred_element_type=F32)
                               + d_ref[:, cols] * dyv[:, cols])
            db_ref[k] += lax.dot_general(ub[:, cols], dsb[:, lanes], _TN, preferred_element_type=F32)
            dc_ref[k] += lax.dot_general(xs_ref[:, lanes], dyb[:, cols], _TN, preferred_element_type=F32)
        accd_ref[...] += _fold8(dyv * uv)

        @pl.when(ch == nch - 1)
        def _():
            for k in range(cps):
                dl_ref[k] = jnp.sum(accl_ref[:, 2 * S * k:2 * S * (k + 1)], axis=0, keepdims=True)
            dd_ref[...] = jnp.sum(accd_ref[...], axis=0, keepdims=True)

    rev = lambda g, c: (nch - 1 - c, g)
    return pl.pallas_call(
        body, name=name, grid=(ncol // cps, nch),
        in_specs=[pl.BlockSpec((rc, cps * SSM_LANES), rev), pl.BlockSpec((rc, cps * SSM_LANES), rev),
                  pl.BlockSpec((rc, cps * 2 * S), rev),
                  pl.BlockSpec((cps, SSM_LANES, 2 * S), lambda g, c: (g, 0, 0)),
                  pl.BlockSpec((cps, 2 * S, SSM_LANES), lambda g, c: (g, 0, 0)),
                  pl.BlockSpec((1, cps * SSM_LANES), lambda g, c: (0, g)),
                  pl.BlockSpec((cps, 8, 4 * S), lambda g, c: (g, 0, 0))],
        out_specs=[pl.BlockSpec((rc, cps * SSM_LANES), rev),
                   pl.BlockSpec((cps, SSM_LANES, 2 * S), lambda g, c: (g, 0, 0)),
                   pl.BlockSpec((cps, 2 * S, SSM_LANES), lambda g, c: (g, 0, 0)),
                   pl.BlockSpec((cps, 1, 2 * S), lambda g, c: (g, 0, 0)),
                   pl.BlockSpec((1, cps * SSM_LANES), lambda g, c: (0, g))],
        out_shape=[jax.ShapeDtypeStruct((T, W), F32),
                   jax.ShapeDtypeStruct((ncol, SSM_LANES, 2 * S), F32),
                   jax.ShapeDtypeStruct((ncol, 2 * S, SSM_LANES), F32),
                   jax.ShapeDtypeStruct((ncol, 1, 2 * S), F32),
                   jax.ShapeDtypeStruct((1, W), F32)],
        scratch_shapes=[pltpu.VMEM((rc, cps * 2 * S), F32), pltpu.VMEM((8, cps * 2 * S), F32),
                        pltpu.VMEM((8, cps * 2 * S), F32), pltpu.VMEM((8, cps * SSM_LANES), F32)],
        compiler_params=_params(("parallel", "arbitrary")),
    )(dy, u, xs, bmat, cmat, dskip, tables)


def _ssm_matrices(a_re, a_im, log_step, b_re, b_im, c_re, c_im):
    G, N = a_re.shape
    ncol = G // GROUPS_PER_COL
    step = jnp.exp(log_step)[:, None]
    mag = jnp.exp(a_re * step)
    ang = a_im * step
    lam_re, lam_im = mag * jnp.cos(ang), mag * jnp.sin(ang)
    den = a_re * a_re + a_im * a_im
    nr, ni = lam_re - 1.0, lam_im
    coef_re = (nr * a_re + ni * a_im) / den
    coef_im = (ni * a_re - nr * a_im) / den
    bb_re = coef_re[..., None] * b_re - coef_im[..., None] * b_im
    bb_im = coef_re[..., None] * b_im + coef_im[..., None] * b_re
    eye = jnp.eye(GROUPS_PER_COL, dtype=F32)
    bb = jnp.stack([bb_re, bb_im]).reshape(2, ncol, GROUPS_PER_COL, N, SSM_GROUP)
    bmat = jnp.einsum("pbgnc,gh->bgcphn", bb, eye).reshape(ncol, SSM_LANES, 2 * STATE_LANES)
    cc = jnp.stack([c_re, -c_im]).reshape(2, ncol, GROUPS_PER_COL, SSM_GROUP, N)
    cmat = jnp.einsum("pbgcn,gh->bpgnhc", cc, eye).reshape(ncol, 2 * STATE_LANES, SSM_LANES)
    lam = jnp.concatenate([lam_re.reshape(ncol, STATE_LANES), lam_im.reshape(ncol, STATE_LANES)], axis=-1)
    return lam, bmat, cmat


def _scan_tables(lam, nbatch, conj):
    S = STATE_LANES
    lr, li = lam[:, None, 0:S], lam[:, None, S:2 * S]
    if conj:
        li = -li
    l2r, l2i = lr * lr - li * li, 2.0 * lr * li
    first = (jnp.arange(8) < nbatch)[None, :, None]
    zero = jnp.zeros_like(lr)
    if conj:
        parts = [jnp.where(first, lr, zero), jnp.where(first, li, zero), jnp.where(first, l2r, lr), jnp.where(first, l2i, li)]
    else:
        parts = [jnp.where(first, zero, lr), jnp.where(first, zero, li), jnp.where(first, lr, l2r), jnp.where(first, li, l2i)]
    return jnp.concatenate([jnp.broadcast_to(p, (lam.shape[0], 8, S)) for p in parts], axis=-1)


def _adamw_update(w_ref, g_ref, m_ref, v_ref, d_ref, nm_ref, nv_ref):
    gv = g_ref[...]
    mn = ADAM_B1 * m_ref[...] + (1.0 - ADAM_B1) * gv
    vn = ADAM_B2 * v_ref[...] + (1.0 - ADAM_B2) * (gv * gv)
    m_hat = mn / (1.0 - ADAM_B1 ** ADAM_STEP)
    v_hat = vn / (1.0 - ADAM_B2 ** ADAM_STEP)
    d_ref[...] = -ADAM_LR * (m_hat / (jnp.sqrt(v_hat) + ADAM_EPS) + ADAM_WD * w_ref[...])
    nm_ref[...] = mn
    nv_ref[...] = vn


def _adamw_small(ws, gs, ms, vs, name):
    n = len(ws)

    def body(*refs):
        for i in range(n):
            _adamw_update(refs[i], refs[n + i], refs[2 * n + i], refs[3 * n + i],
                          refs[4 * n + i], refs[5 * n + i], refs[6 * n + i])

    vm = pl.BlockSpec(memory_space=pltpu.VMEM)
    shapes = [jax.ShapeDtypeStruct(a.shape, F32) for a in ws]
    outs = pl.pallas_call(body, name=name, in_specs=[vm] * (4 * n), out_specs=[vm] * (3 * n), out_shape=shapes * 3,
                          compiler_params=pltpu.CompilerParams(vmem_limit_bytes=VMEM_LIMIT))(*ws, *gs, *ms, *vs)
    return outs[:n], outs[n:2 * n], outs[2 * n:]


def _adamw(w, g, m, v, name):
    R, C = w.shape[-2], w.shape[-1]
    tr = R if R <= 512 else _pick_tile(R, 512, 8)
    body = functools.partial(_adamw_update)

    def spec_for(a):
        if len(a.shape) == 2:
            return pl.BlockSpec((tr, C), lambda i: (i, 0))
        return pl.BlockSpec((None, tr, C), lambda i: (0, i, 0))

    spec = spec_for(w)
    shp = jax.ShapeDtypeStruct(w.shape, F32)
    return pl.pallas_call(body, name=name, grid=(R // tr,), in_specs=[spec, spec_for(g), spec, spec], out_specs=[spec] * 3,
                          out_shape=[shp, shp, shp], compiler_params=_params(("parallel",)))(w, g, m, v)


_ANY = pl.BlockSpec(memory_space=pl.ANY)


def _place():
    x, y, c = lax.axis_index("x"), lax.axis_index("y"), lax.axis_index("c")
    chips = [(1 - x, y), (x, 1 - y), (1 - x, 1 - y)]
    return x, y, c, chips


def _remote(src, dst, send_sems, recv_sems, k, to):
    return pltpu.make_async_remote_copy(src_ref=src, dst_ref=dst, send_sem=send_sems.at[k], recv_sem=recv_sems.at[k],
                                        device_id=to, device_id_type=MESH_IDS)


class _Riders:
    def __init__(self, srcs, out_shapes, n_sems, copies):
        self.srcs, self.out_shapes, self.n_sems, self.copies = list(srcs), list(out_shapes), n_sems, copies


def _call(body, name, grid, in_specs, out_specs, out_shape, scratch_shapes, sem, args, riders=None):
    if riders is None:
        return pl.pallas_call(body, name=name, grid=grid, in_specs=in_specs, out_specs=out_specs, out_shape=out_shape,
                              scratch_shapes=scratch_shapes, compiler_params=_params(sem))(*args)
    n_in, n_out, n_scr = len(in_specs), len(out_specs), len(scratch_shapes)
    r_in, r_out = len(riders.srcs), len(riders.out_shapes)

    def carrying(*refs):
        a, b = n_in, n_in + r_in
        c, d = b + n_out, b + n_out + r_out
        e = d + n_scr
        sends, arrivals = riders.copies(refs[a:b], refs[c:d], refs[e], refs[e + 1])
        first, last = None, None
        for ax, size in enumerate(grid):
            at0, at1 = pl.program_id(ax) == 0, pl.program_id(ax) == size - 1
            first = at0 if first is None else first & at0
            last = at1 if last is None else last & at1

        @pl.when(first)
        def _():
            for cp in sends:
                cp.start()

        body(*refs[:a], *refs[b:c], *refs[d:e])

        @pl.when(last)
        def _():
            for cp in arrivals:
                cp.wait_recv()
            for cp in sends:
                cp.wait_send()

    outs = pl.pallas_call(
        carrying, name=name, grid=grid, in_specs=list(in_specs) + [_ANY] * r_in,
        out_specs=list(out_specs) + [_ANY] * r_out, out_shape=list(out_shape) + riders.out_shapes,
        scratch_shapes=list(scratch_shapes) + [pltpu.SemaphoreType.DMA((riders.n_sems,)),
                                               pltpu.SemaphoreType.DMA((riders.n_sems,))],
        compiler_params=pltpu.CompilerParams(dimension_semantics=("arbitrary",) * len(grid),
                                             vmem_limit_bytes=VMEM_LIMIT, has_side_effects=True),
    )(*args, *riders.srcs)
    return outs[:n_out], outs[n_out:]


def _gather_riders(shards):
    def copies(srcs, outs, send_sems, recv_sems):
        x, y, c, chips = _place()
        sends, arrivals = [], []
        for i, s in enumerate(shards):
            half = s.shape[0] // 2
            rows = pl.ds(c * half, half)
            for j, chip in enumerate(chips):
                sends.append(_remote(srcs[i].at[rows, :], outs[i].at[2 * x + y, rows, :], send_sems, recv_sems,
                                     3 * i + j, (*chip, c)))
                landed = outs[i].at[2 * chip[0] + chip[1], rows, :]
                arrivals.append(_remote(landed, landed, send_sems, recv_sems, 3 * i + j, (*chip, c)))
        return sends, arrivals

    return _Riders(shards, [jax.ShapeDtypeStruct((N_CHIPS,) + s.shape, s.dtype) for s in shards], 3 * len(shards), copies)


def _exchange_riders(parts):
    def copies(srcs, outs, send_sems, recv_sems):
        x, y, c, chips = _place()
        sends = [_remote(srcs[i].at[2 * chip[0] + chip[1]], outs[i].at[j], send_sems, recv_sems, 3 * i + j, (*chip, c))
                 for i in range(len(parts)) for j, chip in enumerate(chips)]
        return sends, sends

    return _Riders(parts, [jax.ShapeDtypeStruct((3,) + p.shape[1:], p.dtype) for p in parts], 3 * len(parts), copies)


def _swap_riders(grads):
    def copies(srcs, outs, send_sems, recv_sems):
        x, y, c, _ = _place()
        sends = []
        for i, g in enumerate(grads):
            half = g.shape[1] // 2
            sends.append(_remote(srcs[i].at[:, pl.ds((1 - c) * half, half), :], outs[i], send_sems, recv_sems, i,
                                 (x, y, 1 - c)))
        return sends, sends

    return _Riders(grads, [jax.ShapeDtypeStruct((N_CHIPS, g.shape[1] // 2, g.shape[2]), g.dtype) for g in grads],
                   len(grads), copies)


def _forward_halves(gathered, shards, tag):
    n = len(gathered)

    def body(*refs):
        srcs, outs = refs[:n], refs[n:2 * n]
        send_sems, recv_sems = refs[2 * n:]
        x, y, c, chips = _place()
        sibling = (x, y, 1 - c)
        cps = []
        for i in range(n):
            half = gathered[i].shape[1] // 2
            for j, chip in enumerate(chips):
                slot = 2 * chip[0] + chip[1]
                cps.append(_remote(srcs[i].at[slot, pl.ds(c * half, half), :], outs[i].at[slot, pl.ds(c * half, half), :],
                                   send_sems, recv_sems, 3 * i + j, sibling))
        for cp in cps:
            cp.start()
        for i in range(n):
            half = gathered[i].shape[1] // 2
            for j, chip in enumerate(chips):
                theirs = outs[i].at[2 * chip[0] + chip[1], pl.ds((1 - c) * half, half), :]
                _remote(theirs, theirs, send_sems, recv_sems, 3 * i + j, sibling).wait_recv()
        for cp in cps:
            cp.wait_send()

    outs = pl.pallas_call(
        body, name=f"gather_forward_{tag}", in_specs=[_ANY] * n, out_specs=[_ANY] * n,
        out_shape=[jax.ShapeDtypeStruct(g.shape, g.dtype) for g in gathered],
        input_output_aliases={i: i for i in range(n)},
        scratch_shapes=[pltpu.SemaphoreType.DMA((3 * n,)), pltpu.SemaphoreType.DMA((3 * n,))],
        compiler_params=pltpu.CompilerParams(has_side_effects=True),
    )(*gathered)
    slot = 2 * lax.axis_index("x") + lax.axis_index("y")
    return [lax.dynamic_update_slice(o, s[None], (slot, 0, 0)) for o, s in zip(outs, shards)]


def _gather_weights(shards):
    n = len(shards)

    def body(*refs):
        srcs, outs = refs[:n], refs[n:2 * n]
        send_sems, recv_sems = refs[2 * n:]
        x, y, c, chips = _place()
        sibling = (x, y, 1 - c)

        def piece(i, px, py, pc):
            half = shards[i].shape[0] // 2
            return outs[i].at[2 * px + py, pl.ds(pc * half, half), :]

        first = []
        for i in range(n):
            half = shards[i].shape[0] // 2
            for j, chip in enumerate(chips):
                first.append(_remote(srcs[i].at[pl.ds(c * half, half), :], piece(i, x, y, c), send_sems, recv_sems,
                                     6 * i + j, (*chip, c)))
        for cp in first:
            cp.start()
        passed = []
        for i in range(n):
            for j, chip in enumerate(chips):
                _remote(piece(i, *chip, c), piece(i, *chip, c), send_sems, recv_sems, 6 * i + j, (*chip, c)).wait_recv()
                cp = _remote(piece(i, *chip, c), piece(i, *chip, c), send_sems, recv_sems, 6 * i + 3 + j, sibling)
                cp.start()
                passed.append(cp)
        for i in range(n):
            for j, chip in enumerate(chips):
                _remote(piece(i, *chip, 1 - c), piece(i, *chip, 1 - c), send_sems, recv_sems, 6 * i + 3 + j,
                        sibling).wait_recv()
        for cp in first + passed:
            cp.wait_send()

    outs = pl.pallas_call(
        body, name="gather_weights", in_specs=[_ANY] * n, out_specs=[_ANY] * n,
        out_shape=[jax.ShapeDtypeStruct((N_CHIPS,) + s.shape, s.dtype) for s in shards],
        scratch_shapes=[pltpu.SemaphoreType.DMA((6 * n,)), pltpu.SemaphoreType.DMA((6 * n,))],
        compiler_params=pltpu.CompilerParams(has_side_effects=True),
    )(*shards)
    slot = 2 * lax.axis_index("x") + lax.axis_index("y")
    return [lax.dynamic_update_slice(o, s[None], (slot, 0, 0)) for o, s in zip(outs, shards)]


def _swap_halves(grads, tag):
    n = len(grads)

    def body(*refs):
        srcs, outs = refs[:n], refs[n:2 * n]
        send_sems, recv_sems = refs[2 * n:]
        x, y, c, _ = _place()
        cps = []
        for i in range(n):
            half = grads[i].shape[1] // 2
            cps.append(_remote(srcs[i].at[:, pl.ds((1 - c) * half, half), :], outs[i], send_sems, recv_sems, i, (x, y, 1 - c)))
        for cp in cps:
            cp.start()
        for cp in cps:
            cp.wait()

    return pl.pallas_call(
        body, name=f"grad_swap_halves_{tag}", in_specs=[_ANY] * n, out_specs=[_ANY] * n,
        out_shape=[jax.ShapeDtypeStruct((N_CHIPS, g.shape[1] // 2, g.shape[2]), g.dtype) for g in grads],
        scratch_shapes=[pltpu.SemaphoreType.DMA((n,)), pltpu.SemaphoreType.DMA((n,))],
        compiler_params=pltpu.CompilerParams(has_side_effects=True),
    )(*grads)


def _join_halves(fulls):
    n = len(fulls)

    def body(*refs):
        srcs, outs = refs[:n], refs[n:2 * n]
        send_sems, recv_sems = refs[2 * n:]
        x, y, c, _ = _place()
        sibling = (x, y, 1 - c)
        cps = []
        for i in range(n):
            h = fulls[i].shape[0] // 2
            cps.append(_remote(srcs[i].at[pl.ds(c * h, h), :], outs[i].at[pl.ds(c * h, h), :], send_sems, recv_sems, i,
                               sibling))
        for cp in cps:
            cp.start()
        for i in range(n):
            h = fulls[i].shape[0] // 2
            theirs = outs[i].at[pl.ds((1 - c) * h, h), :]
            _remote(theirs, theirs, send_sems, recv_sems, i, sibling).wait_recv()
        for cp in cps:
            cp.wait_send()

    return pl.pallas_call(
        body, name="grad_join_halves", in_specs=[_ANY] * n, out_specs=[_ANY] * n,
        out_shape=[jax.ShapeDtypeStruct(f.shape, f.dtype) for f in fulls],
        input_output_aliases={i: i for i in range(n)},
        scratch_shapes=[pltpu.SemaphoreType.DMA((n,)), pltpu.SemaphoreType.DMA((n,))],
        compiler_params=pltpu.CompilerParams(has_side_effects=True),
    )(*fulls)


def _half_tile(h):
    return h if h <= 512 else _pick_tile(h, 512, ROW_ALIGN)


def _sum_halves(g, r1, c_idx, name):
    _, R, C = g.shape
    H = R // 2
    tr = _half_tile(H)
    nblk = H // tr

    def body(c_ref, g_ref, r_ref, p_ref):
        p_ref[...] = (g_ref[...] + r_ref[...]).astype(BF16)

    half = pl.BlockSpec((None, tr, C), lambda s, i, c_ref: (s, c_ref[0] * nblk + i, 0))
    plain = pl.BlockSpec((None, tr, C), lambda s, i, c_ref: (s, i, 0))
    return pl.pallas_call(
        body, name=name,
        grid_spec=pltpu.PrefetchScalarGridSpec(num_scalar_prefetch=1, grid=(N_CHIPS, nblk), in_specs=[half, plain],
                                               out_specs=plain),
        out_shape=jax.ShapeDtypeStruct((N_CHIPS, H, C), BF16),
        compiler_params=_params(("parallel", "parallel")),
    )(c_idx, g, r1)


def _sum_chips(g, r1, r2, idx, name):
    _, R, C = g.shape
    H = R // 2
    tr = _half_tile(H)
    nblk = H // tr

    def body(idx_ref, g_ref, r1_ref, r2_ref, o_ref):
        o_ref[...] = (((g_ref[...] + r1_ref[...]) + r2_ref[0].astype(F32)) + r2_ref[1].astype(F32)) + r2_ref[2].astype(F32)

    return pl.pallas_call(
        body, name=name,
        grid_spec=pltpu.PrefetchScalarGridSpec(
            num_scalar_prefetch=1, grid=(nblk,),
            in_specs=[pl.BlockSpec((None, tr, C), lambda i, idx_ref: (idx_ref[0], idx_ref[1] * nblk + i, 0)),
                      pl.BlockSpec((None, tr, C), lambda i, idx_ref: (idx_ref[0], i, 0)),
                      pl.BlockSpec((3, tr, C), lambda i, idx_ref: (0, i, 0))],
            out_specs=pl.BlockSpec((tr, C), lambda i, idx_ref: (idx_ref[1] * nblk + i, 0))),
        out_shape=jax.ShapeDtypeStruct((R, C), F32),
        compiler_params=_params(("parallel",)),
    )(idx, g, r1, r2)


def _all_reduce_small(v, n_fold, fold_rows, fold_at):
    M, N = v.shape

    def body(x_ref, tot_ref, fold_ref, all_ref, send_sems, recv_sems, local_sem):
        x, y, c, chips = _place()
        me, sibling = (x, y, c), (x, y, 1 - c)

        def rows(px, py, pc):
            return all_ref.at[pl.ds((4 * px + 2 * py + pc) * M, M), :]

        def copy(k, block, to, src=None):
            return _remote(rows(*block) if src is None else src, rows(*block), send_sems, recv_sems, k, to)

        mine = pltpu.make_async_copy(x_ref, rows(*me), local_sem)
        mine.start()
        first = [copy(0, me, sibling, src=x_ref)]
        first += [copy(1 + j, me, (*chip, c), src=x_ref) for j, chip in enumerate(chips)]
        for cp in first:
            cp.start()
        passed = [copy(4 + j, (*chip, c), sibling) for j, chip in enumerate(chips)]
        for j, chip in enumerate(chips):
            copy(1 + j, (*chip, c), me).wait_recv()
            passed[j].start()
        copy(0, sibling, me).wait_recv()
        for j, chip in enumerate(chips):
            copy(4 + j, (*chip, 1 - c), me).wait_recv()
        for cp in first + passed:
            cp.wait_send()
        mine.wait()
        tot = all_ref[0:M, :]
        for d in range(1, 8):
            tot = tot + all_ref[d * M:(d + 1) * M, :]
        tot_ref[...] = tot
        f = tot[fold_at:fold_at + fold_rows, :]
        for e in range(1, n_fold):
            f = f + tot[fold_at + e * fold_rows:fold_at + (e + 1) * fold_rows, :]
        fold_ref[...] = f

    vm = pl.BlockSpec(memory_space=pltpu.VMEM)
    return pl.pallas_call(
        body, name="all_reduce_small", in_specs=[vm], out_specs=[vm, vm],
        out_shape=[jax.ShapeDtypeStruct((M, N), F32), jax.ShapeDtypeStruct((fold_rows, N), F32)],
        scratch_shapes=[pltpu.VMEM((8 * M, N), F32), pltpu.SemaphoreType.DMA((7,)), pltpu.SemaphoreType.DMA((7,)),
                        pltpu.SemaphoreType.DMA],
        compiler_params=pltpu.CompilerParams(has_side_effects=True, vmem_limit_bytes=VMEM_LIMIT),
    )(v)


def _as_rows(a, width):
    flat = a.reshape(-1)
    pad = (-flat.shape[0]) % width
    if pad:
        flat = jnp.concatenate([flat, jnp.zeros((pad,), flat.dtype)])
    return flat.reshape(-1, width)


class _Layout:
    def __init__(self, width, total_mult):
        self.width, self.total_mult = width, total_mult
        self.offsets, self.shapes, self.rows = {}, {}, 0

    def add(self, name, shape):
        r = -(-math.prod(shape) // self.width)
        self.offsets[name], self.shapes[name] = (self.rows, r), tuple(shape)
        self.rows += r

    def align(self, mult):
        gap = (-self.rows) % mult
        if gap:
            self.offsets[f"_gap{self.rows}"], self.shapes[f"_gap{self.rows}"] = (self.rows, gap), (gap, self.width)
            self.rows += gap
        return self.rows

    def pack(self, pieces):
        self.align(self.total_mult)
        parts = [_as_rows(pieces[n].astype(F32), self.width) if n in pieces else jnp.zeros(self.shapes[n], F32)
                 for n in self.offsets]
        return jnp.concatenate(parts, axis=0)

    def unpack(self, buf, name):
        off, r = self.offsets[name]
        shape = self.shapes[name]
        return buf[off:off + r].reshape(-1)[:math.prod(shape)].reshape(shape)


_BIG = ["ffn1_w1", "ffn1_w3", "ffn1_w2", "w_in", "ssm_glu_a", "ssm_glu_b", "w_out", "ffn2_w1", "ffn2_w3", "ffn2_w2"]
_SMALL = ["ffn1_norm", "mix_norm", "ffn2_norm", "final_norm", "attn_sinks", "ssm_a_re", "ssm_a_im", "ssm_log_step",
          "ssm_b_re", "ssm_b_im", "ssm_c_re", "ssm_c_im", "ssm_d"]
_WEIGHTS = ["meta_tokens", "ffn1_norm", "ffn1_w1", "ffn1_w3", "ffn1_w2", "mix_norm", "w_in", "attn_sinks", "ssm_a_re",
            "ssm_a_im", "ssm_log_step", "ssm_b_re", "ssm_b_im", "ssm_c_re", "ssm_c_im", "ssm_d", "ssm_glu_a",
            "ssm_glu_b", "w_out", "ffn2_norm", "ffn2_w1", "ffn2_w3", "ffn2_w2", "final_norm"]


def _kv_interleave(w, kv_heads):
    kvw = kv_heads * HEAD_DIM
    lead = w.shape[:-1]
    k = w[..., 0:kvw].reshape(lead + (kv_heads, 1, HEAD_DIM))
    v = w[..., kvw:2 * kvw].reshape(lead + (kv_heads, 1, HEAD_DIM))
    return jnp.concatenate([jnp.concatenate([k, v], axis=-2).reshape(lead + (2 * kvw,)), w[..., 2 * kvw:]], axis=-1)


def _kv_deinterleave(w, kv_heads):
    kvw = kv_heads * HEAD_DIM
    lead = w.shape[:-1]
    kv = w[..., 0:2 * kvw].reshape(lead + (kv_heads, 2, HEAD_DIM))
    return jnp.concatenate([kv[..., 0, :].reshape(lead + (kvw,)), kv[..., 1, :].reshape(lead + (kvw,)), w[..., 2 * kvw:]],
                           axis=-1)


def _step(x, target, w, m, v):
    B, S, D = x.shape
    L = S + N_META
    T = B * L
    H = D // HEAD_DIM
    KV = H // Q_PER_KV
    SW = D // 2
    tm = _pick_tile(L, ROW_TILE_CAP, ROW_ALIGN)
    rc = _pick_tile(L, ROW_TILE_CAP // B, 4) * B
    tw = _pick_tile(T, 3 * ROW_TILE_CAP, ROW_ALIGN)
    my_c = lax.axis_index("c")
    my_slot = 2 * lax.axis_index("x") + lax.axis_index("y")

    groups = {"ffn1": ["ffn1_w1", "ffn1_w3", "ffn1_w2"], "mix": ["w_in", "ssm_glu_a", "ssm_glu_b", "w_out"],
              "ffn2": ["ffn2_w1", "ffn2_w3", "ffn2_w2"]}
    waves = {"first": ["ffn1_w1", "ffn1_w3"], "early": ["ffn1_w2"] + groups["mix"], "late": groups["ffn2"]}
    shards = {n: w[n][0].astype(BF16) for n in _BIG}
    gathered = _gather_weights([shards[n] for n in waves["first"]] + [w["meta_tokens"]])
    ws = dict(zip(waves["first"], gathered[:-1]))
    meta = jnp.transpose(gathered[-1], (1, 0, 2)).reshape(N_META, D)

    def arrive(wave, landed):
        mine = [shards[n] for n in waves[wave]]
        ws.update(zip(waves[wave], _forward_halves(landed, mine, wave)))

    g_ffn1, g_mix, g_ffn2 = w["ffn1_norm"], w["mix_norm"], w["ffn2_norm"]
    g_final = w["final_norm"].reshape(1, D)

    h0 = jnp.concatenate([jnp.broadcast_to(meta[None], (B, N_META, D)), x], axis=1).reshape(T, D)

    def ffn_fwd(h, g, tag, carry=None):
        n = _rmsnorm_fwd(h, g, tm, f"{tag}_norm")
        riders = None if carry is None else _gather_riders([shards[k] for k in waves[carry]])
        out = _ffn_up(n, ws[f"{tag}_w1"], ws[f"{tag}_w3"], tm, f"{tag}_up", riders)
        if carry is not None:
            out, landed = out
            arrive(carry, landed)
        a, c, s = out
        return _ffn_down(s, ws[f"{tag}_w2"], h, tm, f"{tag}_down"), (n, a, c, s)

    h1, saved1 = ffn_fwd(h0, g_ffn1, "ffn1", carry="early")
    w_kvu = _kv_interleave(ws["w_in"][1], KV)
    hn = _rmsnorm_fwd(h1, g_mix, tm, "mix_norm")
    q = _mm_colslots(hn, ws["w_in"], BF16, "w_in_q", tm, first=0, count=1, scale=HEAD_DIM ** -0.5)
    kvu = _mm_plain(hn, w_kvu, "nn", F32, "w_in_kvu", tm)
    gates = _mm_colslots(hn, ws["w_in"], F32, "w_in_gates", tm, first=2, count=2)

    sinks = w["attn_sinks"].reshape(KV, Q_PER_KV, 1, 1)
    sink_col = jnp.broadcast_to(sinks, (KV, Q_PER_KV, BLOCK, 1)).reshape(KV, Q_PER_KV * BLOCK, 1)
    sink_meta = jnp.broadcast_to(sinks, (KV, Q_PER_KV, N_META, 1)).reshape(KV, Q_PER_KV * N_META, 1)
    (attn,), landed = _attn_fwd(q, kvu, sink_col, sink_meta, B, "attn_fwd",
                                _gather_riders([shards[k] for k in waves["late"]]))
    arrive("late", landed)

    def to_time_major(a2d):
        return jnp.transpose(a2d.reshape(B, L, a2d.shape[-1]), (1, 0, 2)).reshape(T, a2d.shape[-1])

    def to_batch_major(a2d):
        return jnp.transpose(a2d.reshape(L, B, a2d.shape[-1]), (1, 0, 2)).reshape(T, a2d.shape[-1])

    ssm_args = (w["ssm_a_re"][0], w["ssm_a_im"][0], w["ssm_log_step"][0], w["ssm_b_re"][0], w["ssm_b_im"][0],
                w["ssm_c_re"][0], w["ssm_c_im"][0])
    (lam, bmat, cmat), ssm_vjp = jax.vjp(_ssm_matrices, *ssm_args)
    bmat16, cmat16 = bmat.astype(BF16), cmat.astype(BF16)
    u_t = to_time_major(kvu[:, SW:])
    y_t, xs = _ssm_fwd(u_t, bmat16, cmat16, w["ssm_d"], _scan_tables(lam, B, False), B, rc, "ssm_fwd")
    y0 = to_batch_major(y_t)
    yg = _gelu_fwd(y0, tm, "gelu_fwd")
    ga = _mm_colslots(yg, ws["ssm_glu_a"], F32, "glu_a", tm)
    gb = _mm_colslots(yg, ws["ssm_glu_b"], F32, "glu_b", tm)
    merged = _merge_fwd(gates, attn, ga, gb, tm, "merge_fwd")
    h2 = _mm_rowslots(merged, ws["w_out"], h1, tm, "w_out")
    h3, saved2 = ffn_fwd(h2, g_ffn2, "ffn2")
    dh3, dh3b, dg_final, loss_row = _loss_head(h3, g_final, target, tm, "loss_head")

    grads, swapped, received = {}, {}, {}
    c_idx = my_c.reshape(1).astype(jnp.int32)
    idx = jnp.stack([my_slot, my_c]).astype(jnp.int32)

    def swap_riders(group):
        return _swap_riders([grads[n] for n in groups[group]])

    def exchange_riders(group):
        names = groups[group]
        if names[0] not in swapped:
            swapped.update(zip(names, _swap_halves([grads[n] for n in names], group)))
        return _exchange_riders([_sum_halves(grads[n], swapped[n], c_idx, f"grad_sum_halves_{n}") for n in names])

    def ffn_bwd(h, g, saved, dh, dhb, tag, dhidden_carries=None, dn_carries=None):
        n, a, c, s = saved
        w1, w3, w2 = ws[f"{tag}_w1"], ws[f"{tag}_w3"], ws[f"{tag}_w2"]
        grads[f"{tag}_w2"] = _wgrad_hidden_rows(s, dhb, tw, f"{tag}_dw2", 0.5)
        if dhidden_carries is None:
            da, dc = _ffn_dhidden(dhb, w2, a, c, tm, f"{tag}_dhidden")
        else:
            (da, dc), got = _ffn_dhidden(dhb, w2, a, c, tm, f"{tag}_dhidden", exchange_riders(dhidden_carries[1]))
            received.update(zip(groups[dhidden_carries[1]], got))
        grads[f"{tag}_w1"] = _wgrad_hidden_cols(n, da, tw, f"{tag}_dw1")
        grads[f"{tag}_w3"] = _wgrad_hidden_cols(n, dc, tw, f"{tag}_dw3")
        kind, group = dn_carries
        riders = swap_riders(group) if kind == "swap" else exchange_riders(group)
        (dh_in, dhb_in, grads[f"{tag}_norm"]), got = _ffn_dn(da, w1, dc, w3, h, g, dh, tm, f"{tag}_dn", riders)
        return dh_in, dhb_in, got

    dh2, dh2b, got = ffn_bwd(h2, g_ffn2, saved2, dh3, dh3b, "ffn2", dn_carries=("swap", "ffn2"))
    swapped.update(zip(groups["ffn2"], got))

    grads["w_out"] = _wgrad_rowslots(merged, dh2b, tw, "dw_out")
    dattn, dgat, dgss, dga, dgb = _merge_bwd(dh2b, ws["w_out"], gates, attn, ga, gb, tm, "merge_bwd")
    grads["ssm_glu_a"] = _wgrad_colslots(yg, dga, tw, "dglu_a")
    grads["ssm_glu_b"] = _wgrad_colslots(yg, dgb, tw, "dglu_b")
    dy0 = _gelu_bwd([(dga, ws["ssm_glu_a"]), (dgb, ws["ssm_glu_b"])], y0, tm, "gelu_bwd")
    du_t, dbmat, dcmat, dlam, dd = _ssm_bwd(to_time_major(dy0), u_t, xs, bmat16, cmat16, w["ssm_d"],
                                            _scan_tables(lam, B, True), B, rc, "ssm_bwd")
    d_ssm = ssm_vjp((dlam[:, 0, :], dbmat, dcmat))
    for n, gval in zip(["ssm_a_re", "ssm_a_im", "ssm_log_step", "ssm_b_re", "ssm_b_im", "ssm_c_re", "ssm_c_im"], d_ssm):
        grads[n] = gval[None]
    grads["ssm_d"] = dd

    (dq, dkv, dsink), got = _attn_bwd(q, kvu, attn, dattn, sink_col, sink_meta, B, "attn_bwd",
                                      exchange_riders("ffn2"))
    received.update(zip(groups["ffn2"], got))
    grads["attn_sinks"] = dsink[:, 0:Q_PER_KV, 0].reshape(1, H)
    dkvu = jnp.concatenate([dkv, to_batch_major(du_t).astype(BF16)], axis=1)
    pieces = [dq, dkvu, dgat, dgss]
    dw_in = [_wgrad_plain(hn, p, f"dw_in_{k}", tw) for k, p in enumerate(pieces)]
    dw_in[1] = _kv_deinterleave(dw_in[1], KV)
    grads["w_in"] = jnp.stack(dw_in)
    w_in_parts = [ws["w_in"][0], w_kvu, ws["w_in"][2], ws["w_in"][3]]
    whole = _once((D, D), lambda i: (0, 0))
    (dh1, dh1b, grads["mix_norm"]), swap_mix = _mm_norm_bwd(
        "dhn", "nt", [(p, _spec((tm, D), lambda i: (i, 0)), wp, whole) for p, wp in zip(pieces, w_in_parts)],
        h1, g_mix, dh2, tm, swap_riders("mix"))
    swapped.update(zip(groups["mix"], swap_mix))
    dh0, _, got = ffn_bwd(h0, g_ffn1, saved1, dh1, dh1b, "ffn1", dhidden_carries=("exchange", "mix"),
                          dn_carries=("exchange", "ffn1"))
    received.update(zip(groups["ffn1"], got))
    dh0 = dh0.reshape(B, L, D)
    grad_x = dh0[:, N_META:, :]

    grads["final_norm"] = dg_final
    slay = _Layout(D, 8)
    for n in _SMALL:
        slay.add(n, w[n].shape)
    slay.add("loss", (1, D))
    meta_at = slay.align(8)
    slay.add("meta", (B * N_META, D))
    small = slay.pack({**{n: grads[n] for n in _SMALL}, "loss": loss_row, "meta": dh0[:, :N_META, :]})
    tot_small, dmeta = _all_reduce_small(small, B, N_META, meta_at)
    loss = slay.unpack(tot_small, "loss")[0, 0]
    for n in _SMALL:
        grads[n] = slay.unpack(tot_small, n)
    cw = D // N_CHIPS
    grads["meta_tokens"] = lax.dynamic_slice_in_dim(dmeta, my_slot * cw, cw, axis=1)

    fulls = [_sum_chips(grads[n], swapped[n], received[n], idx, f"grad_sum_chips_{n}") for n in _BIG]
    for n, f in zip(_BIG, _join_halves(fulls)):
        grads[n] = f

    delta, new_m, new_v = {}, {}, {}
    for n in _BIG + ["meta_tokens"]:
        delta[n], new_m[n], new_v[n] = _adamw(w[n], grads[n], m[n], v[n], f"adamw_{n}")
        grads[n] = grads[n].reshape(w[n].shape)

    def flat2d(a):
        return a.reshape(-1, a.shape[-1])

    d_, m_, v_ = _adamw_small([flat2d(w[n]) for n in _SMALL], [flat2d(grads[n]) for n in _SMALL],
                              [flat2d(m[n]) for n in _SMALL], [flat2d(v[n]) for n in _SMALL], "adamw_small")
    for i, n in enumerate(_SMALL):
        shp = w[n].shape
        delta[n], new_m[n], new_v[n] = d_[i].reshape(shp), m_[i].reshape(shp), v_[i].reshape(shp)
        grads[n] = grads[n].reshape(shp)

    return (loss, grad_x, *[grads[n] for n in _WEIGHTS], *[delta[n] for n in _WEIGHTS],
            *[new_m[n] for n in _WEIGHTS], *[new_v[n] for n in _WEIGHTS])


def kernel(x, meta_tokens, ffn1_norm, ffn1_w1, ffn1_w3, ffn1_w2, mix_norm, w_in, attn_sinks, ssm_a_re, ssm_a_im, ssm_log_step, ssm_b_re, ssm_b_im, ssm_c_re, ssm_c_im, ssm_d, ssm_glu_a, ssm_glu_b, w_out, ffn2_norm, ffn2_w1, ffn2_w3, ffn2_w2, final_norm, loss_target, m_meta_tokens, m_ffn1_norm, m_ffn1_w1, m_ffn1_w3, m_ffn1_w2, m_mix_norm, m_w_in, m_attn_sinks, m_ssm_a_re, m_ssm_a_im, m_ssm_log_step, m_ssm_b_re, m_ssm_b_im, m_ssm_c_re, m_ssm_c_im, m_ssm_d, m_ssm_glu_a, m_ssm_glu_b, m_w_out, m_ffn2_norm, m_ffn2_w1, m_ffn2_w3, m_ffn2_w2, m_final_norm, v_meta_tokens, v_ffn1_norm, v_ffn1_w1, v_ffn1_w3, v_ffn1_w2, v_mix_norm, v_w_in, v_attn_sinks, v_ssm_a_re, v_ssm_a_im, v_ssm_log_step, v_ssm_b_re, v_ssm_b_im, v_ssm_c_re, v_ssm_c_im, v_ssm_d, v_ssm_glu_a, v_ssm_glu_b, v_w_out, v_ffn2_norm, v_ffn2_w1, v_ffn2_w3, v_ffn2_w2, v_final_norm):
    args = locals()
    w = {n: args[n] for n in _WEIGHTS}
    m = {n: args["m_" + n] for n in _WEIGHTS}
    v = {n: args["v_" + n] for n in _WEIGHTS}
    return _step(x, loss_target, w, m, v)
```

```python
import functools
import math

import jax
import jax.numpy as jnp
from jax import lax
from jax.experimental import pallas as pl
from jax.experimental.pallas import tpu as pltpu

F32 = jnp.float32
BF16 = jnp.bfloat16
MESH_IDS = pl.DeviceIdType.MESH

N_CHIPS = 4
N_META = 16
HEAD_DIM = 64
Q_PER_KV = 4
QW = Q_PER_KV * HEAD_DIM
BLOCK = 128
SSM_GROUP = 16
SSM_STATE = 64
SSM_LANES = 128
GROUPS_PER_COL = SSM_LANES // SSM_GROUP
STATE_LANES = GROUPS_PER_COL * SSM_STATE
NORM_EPS = 1e-6
NEG_INF = -1e30
ADAM_LR, ADAM_B1, ADAM_B2, ADAM_EPS, ADAM_WD, ADAM_STEP = 0.001, 0.9, 0.999, 1e-08, 0.01, 10
GELU_C = math.sqrt(2.0 / math.pi)
ROW_ALIGN = 16
VMEM_LIMIT = 56 * 1024 * 1024
ROW_TILE_CAP = 688

_NN = (((1,), (0,)), ((), ()))
_NT = (((1,), (1,)), ((), ()))
_TN = (((0,), (0,)), ((), ()))
_DIMS = {"nn": _NN, "nt": _NT, "tn": _TN}


def _params(sem, **kw):
    return pltpu.CompilerParams(dimension_semantics=sem, vmem_limit_bytes=VMEM_LIMIT, **kw)


def _pick_tile(n, cap, mult):
    best = None
    for t in range(mult, min(n, cap) + 1, mult):
        if n % t == 0:
            best = t
    if best is None:
        raise ValueError(f"no tile for {n} (cap {cap}, multiple of {mult})")
    return best


def _sigmoid(x):
    return 1.0 / (1.0 + jnp.exp(-x))


def _spec(block, index_map):
    return pl.BlockSpec(block, index_map)


def _sum_dots(ins, mode):
    tot = None
    for p in range(len(ins) // 2):
        a_ref, b_ref = ins[2 * p], ins[2 * p + 1]
        for sl in ([None] if len(b_ref.shape) == 2 else range(b_ref.shape[0])):
            if sl is None:
                a, b = a_ref[...], b_ref[...]
            elif len(a_ref.shape) == 3:
                a, b = a_ref[sl], b_ref[sl]
            else:
                width = a_ref.shape[1] // b_ref.shape[0]
                a, b = a_ref[:, sl * width:(sl + 1) * width], b_ref[sl]
            d = lax.dot_general(a.astype(BF16), b.astype(BF16), _DIMS[mode], preferred_element_type=F32)
            tot = d if tot is None else tot + d
    return tot


def _mm(name, grid, kaxis, mode, pairs, out_shape, out_spec, scale=1.0, res=None):
    npairs = len(pairs)
    has_res = res is not None
    gk = 1 if kaxis is None else grid[kaxis]
    acc_shape = tuple(d for d in out_spec.block_shape if d is not None)

    def body(*refs):
        res_ref = refs[2 * npairs] if has_res else None
        o_ref = refs[2 * npairs + has_res]
        tot = _sum_dots(refs[:2 * npairs], mode)

        def finish(acc):
            r = acc * scale if scale != 1.0 else acc
            if has_res:
                r = res_ref[...] + r
            o_ref[...] = r.astype(o_ref.dtype)

        if gk == 1:
            finish(tot)
        else:
            acc_ref = refs[-1]
            k = pl.program_id(kaxis)

            @pl.when(k == 0)
            def _():
                acc_ref[...] = tot

            @pl.when(k > 0)
            def _():
                acc_ref[...] += tot

            @pl.when(k == gk - 1)
            def _():
                finish(acc_ref[...])

    in_specs, args = [], []
    for a, a_spec, b, b_spec in pairs:
        in_specs += [a_spec, b_spec]
        args += [a, b]
    if has_res:
        in_specs.append(res[1])
        args.append(res[0])
    sem = tuple("arbitrary" if ax == kaxis else "parallel" for ax in range(len(grid)))
    return pl.pallas_call(
        body, name=name, grid=grid, in_specs=in_specs, out_specs=out_spec, out_shape=out_shape,
        scratch_shapes=[pltpu.VMEM(acc_shape, F32)] if gk > 1 else [],
        compiler_params=_params(sem),
    )(*args)


def _mm_plain(a, b, mode, out_dtype, name, tm, scale=1.0):
    M, K = a.shape
    N = b.shape[1] if mode == "nn" else b.shape[0]
    return _mm(name, (M // tm,), None, mode,
               [(a, _spec((tm, K), lambda i: (i, 0)), b, _spec(b.shape, lambda i: (0, 0)))],
               jax.ShapeDtypeStruct((M, N), out_dtype), _spec((tm, N), lambda i: (i, 0)), scale=scale)


def _wgrad_plain(a, b, name, tr):
    R, M = a.shape
    N = b.shape[1]
    return _mm(name, (R // tr,), 0, "tn",
               [(a, _spec((tr, M), lambda r: (r, 0)), b, _spec((tr, N), lambda r: (r, 0)))],
               jax.ShapeDtypeStruct((M, N), F32), _spec((M, N), lambda r: (0, 0)))


def _rmsnorm_fwd(h, g, tm, name):
    T, D = h.shape

    def body(h_ref, g_ref, o_ref):
        x = h_ref[...]
        r = lax.rsqrt(jnp.mean(x * x, axis=-1, keepdims=True) + NORM_EPS)
        o_ref[...] = ((x * r) * g_ref[...]).astype(BF16)

    return pl.pallas_call(
        body, name=name, grid=(T // tm,),
        in_specs=[pl.BlockSpec((tm, D), lambda i: (i, 0)), pl.BlockSpec((1, D), lambda i: (0, 0))],
        out_specs=pl.BlockSpec((tm, D), lambda i: (i, 0)),
        out_shape=jax.ShapeDtypeStruct((T, D), BF16),
        compiler_params=_params(("parallel",)),
    )(h, g)


def _fold8(x):
    return jnp.sum(x.reshape(x.shape[0] // 8, 8, x.shape[1]), axis=0)


def _mm_norm_bwd(name, mode, pairs, h, g, dres, tm, riders=None):
    T, D = h.shape
    nt = T // tm
    npairs = len(pairs)

    def body(*refs):
        h_ref, g_ref, dres_ref, dh_ref, dhb_ref, dg_ref, acc_ref = refs[2 * npairs:]
        i = pl.program_id(0)
        x = h_ref[...]
        r = lax.rsqrt(jnp.mean(x * x, axis=-1, keepdims=True) + NORM_EPS)
        xhat = x * r
        dy = _sum_dots(refs[:2 * npairs], mode)
        dxhat = dy * g_ref[...]
        dx = r * (dxhat - xhat * jnp.mean(dxhat * xhat, axis=-1, keepdims=True))
        dh = dres_ref[...] + dx
        dh_ref[...] = dh
        dhb_ref[...] = dh.astype(BF16)
        part = _fold8(dy * xhat)

        @pl.when(i == 0)
        def _():
            acc_ref[...] = part

        @pl.when(i > 0)
        def _():
            acc_ref[...] += part

        @pl.when(i == nt - 1)
        def _():
            dg_ref[...] = jnp.sum(acc_ref[...], axis=0, keepdims=True)

    row = pl.BlockSpec((tm, D), lambda i: (i, 0))
    vec = pl.BlockSpec((1, D), lambda i: (0, 0))
    in_specs, args = [], []
    for a, a_spec, b, b_spec in pairs:
        in_specs += [a_spec, b_spec]
        args += [a, b]
    return _call(body, name, (nt,), in_specs + [row, vec, row], [row, row, vec],
                 [jax.ShapeDtypeStruct((T, D), F32), jax.ShapeDtypeStruct((T, D), BF16), jax.ShapeDtypeStruct((1, D), F32)],
                 [pltpu.VMEM((8, D), F32)], ("arbitrary",), (*args, h, g, dres), riders)


def _ffn_up(n, w1, w3, tm, name, riders=None):
    T, D = n.shape
    Fs = w1.shape[2]

    def body(n_ref, w1_ref, w3_ref, a_ref, c_ref, s_ref):
        x = n_ref[...]
        a = jnp.dot(x, w1_ref[...], preferred_element_type=F32)
        c = jnp.dot(x, w3_ref[...], preferred_element_type=F32)
        a_ref[...] = a.astype(BF16)
        c_ref[...] = c.astype(BF16)
        s_ref[...] = (a * _sigmoid(a) * c).astype(BF16)

    w_spec = _spec((None, D, Fs), lambda s, i: (s, 0, 0))
    o_spec = _spec((None, tm, Fs), lambda s, i: (s, i, 0))
    o_shape = jax.ShapeDtypeStruct((N_CHIPS, T, Fs), BF16)
    return _call(body, name, (N_CHIPS, T // tm), [_spec((tm, D), lambda s, i: (i, 0)), w_spec, w_spec],
                 [o_spec, o_spec, o_spec], [o_shape, o_shape, o_shape], [], ("parallel", "parallel"), (n, w1, w3), riders)


def _ffn_down(s, w2, h, tm, name):
    _, T, Fs = s.shape
    D = w2.shape[2]
    row = _spec((tm, D), lambda i: (i, 0))
    return _mm(name, (T // tm,), None, "nn",
               [(s, _spec((N_CHIPS, tm, Fs), lambda i: (0, i, 0)), w2, _spec((N_CHIPS, Fs, D), lambda i: (0, 0, 0)))],
               jax.ShapeDtypeStruct((T, D), F32), row, scale=0.5, res=(h, row))


def _ffn_dhidden(dhb, w2, a, c, tm, name, riders=None):
    T, D = dhb.shape
    Fs = w2.shape[1]

    def body(dh_ref, w2_ref, a_ref, c_ref, da_ref, dc_ref):
        d = 0.5 * lax.dot_general(dh_ref[...], w2_ref[...], _NT, preferred_element_type=F32)
        av = a_ref[...].astype(F32)
        cv = c_ref[...].astype(F32)
        sg = _sigmoid(av)
        da_ref[...] = (d * cv * (sg * (1.0 + av * (1.0 - sg)))).astype(BF16)
        dc_ref[...] = (d * (av * sg)).astype(BF16)

    h_spec = _spec((None, tm, Fs), lambda s, i: (s, i, 0))
    o_shape = jax.ShapeDtypeStruct((N_CHIPS, T, Fs), BF16)
    return _call(body, name, (N_CHIPS, T // tm),
                 [_spec((tm, D), lambda s, i: (i, 0)), _spec((None, Fs, D), lambda s, i: (s, 0, 0)), h_spec, h_spec],
                 [h_spec, h_spec], [o_shape, o_shape], [], ("parallel", "parallel"), (dhb, w2, a, c), riders)


def _wgrad_hidden_rows(s, dhb, tr, name, scale):
    _, T, Fs = s.shape
    D = dhb.shape[1]
    return _mm(name, (N_CHIPS, T // tr), 1, "tn",
               [(s, _spec((None, tr, Fs), lambda k, r: (k, r, 0)), dhb, _spec((tr, D), lambda k, r: (r, 0)))],
               jax.ShapeDtypeStruct((N_CHIPS, Fs, D), F32), _spec((None, Fs, D), lambda k, r: (k, 0, 0)), scale=scale)


def _wgrad_hidden_cols(n, da, tr, name):
    T, D = n.shape
    Fs = da.shape[2]
    return _mm(name, (N_CHIPS, T // tr), 1, "tn",
               [(n, _spec((tr, D), lambda k, r: (r, 0)), da, _spec((None, tr, Fs), lambda k, r: (k, r, 0)))],
               jax.ShapeDtypeStruct((N_CHIPS, D, Fs), F32), _spec((None, D, Fs), lambda k, r: (k, 0, 0)))


def _once(block, index_map):
    return pl.BlockSpec(block, index_map, pipeline_mode=pl.Buffered(1))


def _ffn_dn(da, w1, dc, w3, h, g, dres, tm, name, riders=None):
    _, T, Fs = da.shape
    D = w1.shape[1]
    h_spec = _spec((N_CHIPS, tm, Fs), lambda i: (0, i, 0))
    w_spec = _once((N_CHIPS, D, Fs), lambda i: (0, 0, 0))
    return _mm_norm_bwd(name, "nt", [(da, h_spec, w1, w_spec), (dc, h_spec, w3, w_spec)], h, g, dres, tm, riders)


def _mm_side_by_side(a, w, mode, out_dtype, name, tm, first=0, count=N_CHIPS, scale=1.0):
    T, K = a.shape
    assert first % count == 0
    n = w.shape[2] if mode == "nn" else w.shape[1]

    def body(a_ref, w_ref, o_ref):
        av = a_ref[...].astype(BF16)
        for j in range(count):
            r = lax.dot_general(av, w_ref[j].astype(BF16), _DIMS[mode], preferred_element_type=F32)
            o_ref[:, j * n:(j + 1) * n] = (r * scale if scale != 1.0 else r).astype(o_ref.dtype)

    return pl.pallas_call(
        body, name=name, grid=(T // tm,),
        in_specs=[_spec((tm, K), lambda i: (i, 0)), _once((count,) + w.shape[1:], lambda i: (first // count, 0, 0))],
        out_specs=_spec((tm, count * n), lambda i: (i, 0)),
        out_shape=jax.ShapeDtypeStruct((T, count * n), out_dtype),
        compiler_params=_params(("parallel",)),
    )(a, w)


def _mm_colslots(a, w, out_dtype, name, tm, first=0, count=N_CHIPS, scale=1.0):
    return _mm_side_by_side(a, w, "nn", out_dtype, name, tm, first, count, scale)


def _wgrad_colslots(a, d, tr, name):
    T, K = a.shape
    Ns = d.shape[1] // N_CHIPS
    return _mm(name, (N_CHIPS, T // tr), 1, "tn",
               [(a, _spec((tr, K), lambda k, r: (r, 0)), d, _spec((tr, Ns), lambda k, r: (r, k)))],
               jax.ShapeDtypeStruct((N_CHIPS, K, Ns), F32), _spec((None, K, Ns), lambda k, r: (k, 0, 0)))


def _mm_rowslots(a, w, h, tm, name):
    T = a.shape[0]
    N = w.shape[2]
    row = _spec((tm, N), lambda i: (i, 0))
    return _mm(name, (T // tm,), None, "nn",
               [(a, _spec((tm, a.shape[1]), lambda i: (i, 0)), w, _once(w.shape, lambda i: (0, 0, 0)))],
               jax.ShapeDtypeStruct((T, N), F32), row, res=(h, row))


def _wgrad_rowslots(a, d, tr, name):
    T = a.shape[0]
    Ks = a.shape[1] // N_CHIPS
    N = d.shape[1]
    return _mm(name, (N_CHIPS, T // tr), 1, "tn",
               [(a, _spec((tr, Ks), lambda k, r: (r, k)), d, _spec((tr, N), lambda k, r: (r, 0)))],
               jax.ShapeDtypeStruct((N_CHIPS, Ks, N), F32), _spec((None, Ks, N), lambda k, r: (k, 0, 0)))


def _gelu_parts(x):
    inner = GELU_C * (x + 0.044715 * (x * x * x))
    t = jnp.tanh(inner)
    return t, GELU_C * (1.0 + 3.0 * 0.044715 * (x * x))


def _gelu_fwd(y, tm, name):
    T, W = y.shape

    def body(y_ref, o_ref):
        x = y_ref[...]
        t, _ = _gelu_parts(x)
        o_ref[...] = (0.5 * x * (1.0 + t)).astype(BF16)

    spec = pl.BlockSpec((tm, W), lambda i: (i, 0))
    return pl.pallas_call(body, name=name, grid=(T // tm,), in_specs=[spec], out_specs=spec,
                          out_shape=jax.ShapeDtypeStruct((T, W), BF16),
                          compiler_params=_params(("parallel",)))(y)


def _gelu_bwd(pairs, y, tm, name):
    T, W = y.shape
    npairs = len(pairs)

    def body(*refs):
        y_ref, o_ref = refs[2 * npairs], refs[2 * npairs + 1]
        x = y_ref[...]
        t, dinner = _gelu_parts(x)
        o_ref[...] = _sum_dots(refs[:2 * npairs], "nt") * (0.5 * (1.0 + t) + 0.5 * x * (1.0 - t * t) * dinner)

    spec = pl.BlockSpec((tm, W), lambda i: (i, 0))
    in_specs, args = [], []
    for d, w in pairs:
        in_specs += [_spec((tm, d.shape[1]), lambda i: (i, 0)), _once(w.shape, lambda i: (0, 0, 0))]
        args += [d, w]
    return pl.pallas_call(body, name=name, grid=(T // tm,), in_specs=in_specs + [spec], out_specs=spec,
                          out_shape=jax.ShapeDtypeStruct((T, W), F32),
                          compiler_params=_params(("parallel",)))(*args, y)


def _merge_cols(D):
    cb = 512 if D % 512 == 0 else D
    return cb, D // cb


def _merge_fwd(gates, attn, ga, gb, tm, name):
    T, D = attn.shape
    cb, nc = _merge_cols(D)

    def body(gat_ref, gss_ref, attn_ref, ga_ref, gb_ref, o_ref):
        ssm = ga_ref[...] * _sigmoid(gb_ref[...])
        o_ref[...] = (_sigmoid(gat_ref[...]) * attn_ref[...] + _sigmoid(gss_ref[...]) * ssm).astype(BF16)

    def col(block):
        return pl.BlockSpec((tm, cb), lambda i, j: (i, block * nc + j))

    return pl.pallas_call(
        body, name=name, grid=(T // tm, nc),
        in_specs=[col(0), col(1), col(0), col(0), col(0)],
        out_specs=col(0), out_shape=jax.ShapeDtypeStruct((T, D), BF16),
        compiler_params=_params(("parallel", "parallel")),
    )(gates, gates, attn, ga, gb)


def _merge_bwd(dhb, w_out, gates, attn, ga, gb, tm, name):
    T, D = attn.shape
    cb, nc = _merge_cols(D)
    Ks = w_out.shape[1]
    spb = cb // Ks

    def body(dh_ref, w_ref, gat_ref, gss_ref, attn_ref, ga_ref, gb_ref, dattn_ref, dgat_ref, dgss_ref, dga_ref, dgb_ref):
        dh = dh_ref[...]
        d = jnp.concatenate([lax.dot_general(dh, w_ref[s], _NT, preferred_element_type=F32) for s in range(spb)], axis=1)
        sa = _sigmoid(gat_ref[...])
        ss = _sigmoid(gss_ref[...])
        sb = _sigmoid(gb_ref[...])
        gav = ga_ref[...]
        dattn_ref[...] = d * sa
        dgat_ref[...] = (d * attn_ref[...] * (sa * (1.0 - sa))).astype(BF16)
        dgss_ref[...] = (d * (gav * sb) * (ss * (1.0 - ss))).astype(BF16)
        dssm = d * ss
        dga_ref[...] = (dssm * sb).astype(BF16)
        dgb_ref[...] = (dssm * gav * (sb * (1.0 - sb))).astype(BF16)

    def col(block):
        return pl.BlockSpec((tm, cb), lambda i, j: (i, block * nc + j))

    b16 = jax.ShapeDtypeStruct((T, D), BF16)
    return pl.pallas_call(
        body, name=name, grid=(T // tm, nc),
        in_specs=[pl.BlockSpec((tm, D), lambda i, j: (i, 0)), pl.BlockSpec((spb, Ks, D), lambda i, j: (j, 0, 0)),
                  col(0), col(1), col(0), col(0), col(0)],
        out_specs=[col(0)] * 5,
        out_shape=[jax.ShapeDtypeStruct((T, D), F32), b16, b16, b16, b16],
        compiler_params=_params(("parallel", "parallel")),
    )(dhb, w_out, gates, gates, attn, ga, gb)


def _loss_head(h, g, target, tm, name):
    T, D = h.shape
    B, S, _ = target.shape
    L = S + N_META
    nt = T // tm
    tpe = L // tm

    def body(h_ref, g_ref, t_hbm, dh_ref, dhb_ref, dg_ref, loss_ref, tbuf, acc_g, acc_l, sem):
        i = pl.program_id(0)
        b, j = i // tpe, i % tpe

        @pl.when(j == 0)
        def _():
            tbuf[0:N_META, :] = jnp.zeros((N_META, D), F32)
            cp = pltpu.make_async_copy(t_hbm.at[b, pl.ds(0, tm - N_META), :], tbuf.at[pl.ds(N_META, tm - N_META), :], sem)
            cp.start()
            cp.wait()

        @pl.when(j > 0)
        def _():
            cp = pltpu.make_async_copy(t_hbm.at[b, pl.ds(j * tm - N_META, tm), :], tbuf, sem)
            cp.start()
            cp.wait()

        x = h_ref[...]
        gv = g_ref[...]
        r = lax.rsqrt(jnp.mean(x * x, axis=-1, keepdims=True) + NORM_EPS)
        xhat = x * r
        pos = j * tm + lax.broadcasted_iota(jnp.int32, (tm, 1), 0)
        err = jnp.where(pos >= N_META, xhat * gv - tbuf[...], 0.0)
        dy = err * (1.0 / D)
        dxhat = dy * gv
        dh = r * (dxhat - xhat * jnp.mean(dxhat * xhat, axis=-1, keepdims=True))
        dh_ref[...] = dh
        dhb_ref[...] = dh.astype(BF16)
        pg = _fold8(dy * xhat)
        pe = _fold8(err * err)

        @pl.when(i == 0)
        def _():
            acc_g[...] = pg
            acc_l[...] = pe

        @pl.when(i > 0)
        def _():
            acc_g[...] += pg
            acc_l[...] += pe

        @pl.when(i == nt - 1)
        def _():
            dg_ref[...] = jnp.sum(acc_g[...], axis=0, keepdims=True)
            loss_ref[...] = jnp.full((1, D), (0.5 / D) * jnp.sum(acc_l[...]), F32)

    row = pl.BlockSpec((tm, D), lambda i: (i, 0))
    vec = pl.BlockSpec((1, D), lambda i: (0, 0))
    return pl.pallas_call(
        body, name=name, grid=(nt,),
        in_specs=[row, vec, pl.BlockSpec(memory_space=pl.ANY)], out_specs=[row, row, vec, vec],
        out_shape=[jax.ShapeDtypeStruct((T, D), F32), jax.ShapeDtypeStruct((T, D), BF16),
                   jax.ShapeDtypeStruct((1, D), F32), jax.ShapeDtypeStruct((1, D), F32)],
        scratch_shapes=[pltpu.VMEM((tm, D), F32), pltpu.VMEM((8, D), F32), pltpu.VMEM((8, D), F32),
                        pltpu.SemaphoreType.DMA],
        compiler_params=_params(("arbitrary",)),
    )(h, g, target)


def _heads_to_rows(blk):
    return jnp.concatenate([blk[:, g * HEAD_DIM:(g + 1) * HEAD_DIM] for g in range(Q_PER_KV)], axis=0)


def _rows_to_heads(x):
    rows = x.shape[0] // Q_PER_KV
    return jnp.concatenate([x[g * rows:(g + 1) * rows] for g in range(Q_PER_KV)], axis=1)


def _causal(R):
    kj = lax.broadcasted_iota(jnp.int32, (BLOCK, R), 0)
    qi = lax.broadcasted_iota(jnp.int32, (BLOCK, R), 1) & (BLOCK - 1)
    return kj <= qi


def _band_probs(s_band, s_m, sink):
    m = jnp.maximum(jnp.maximum(jnp.max(s_band, axis=0, keepdims=True), jnp.max(s_m, axis=0, keepdims=True)), sink)
    e_b, e_m, e_s = jnp.exp(s_band - m), jnp.exp(s_m - m), jnp.exp(sink - m)
    inv = 1.0 / (jnp.sum(e_b, axis=0, keepdims=True) + jnp.sum(e_m, axis=0, keepdims=True) + e_s)
    return e_b * inv, e_m * inv, e_s * inv


def _fold_band(tri, two):
    return jnp.where(tri, two[BLOCK:2 * BLOCK], two[0:BLOCK])


def _unfold_band(tri, band):
    return jnp.concatenate([jnp.where(tri, 0.0, band), jnp.where(tri, band, 0.0)], axis=0)


def _meta_probs(qm, k_m, sink_m):
    R = qm.shape[0]
    s = lax.dot_general(qm, k_m, _NT, preferred_element_type=F32)
    qi = lax.broadcasted_iota(jnp.int32, (R, N_META), 0) & (N_META - 1)
    kj = lax.broadcasted_iota(jnp.int32, (R, N_META), 1)
    s = jnp.where(kj <= qi, s, NEG_INF)
    m = jnp.maximum(jnp.max(s, axis=-1, keepdims=True), sink_m)
    e, e_s = jnp.exp(s - m), jnp.exp(sink_m - m)
    inv = 1.0 / (jnp.sum(e, axis=-1, keepdims=True) + e_s)
    return e * inv, e_s * inv


def _block_start(n):
    return pl.multiple_of(N_META + n * BLOCK, ROW_ALIGN)


def _kv(blk):
    return blk[:, 0:HEAD_DIM], blk[:, HEAD_DIM:2 * HEAD_DIM]


def _attn_fwd(q, kv, sink_row, sink_meta, B, name, riders=None):
    T, D = q.shape
    L = T // B
    KV = D // QW
    nb = (L - N_META) // BLOCK

    def body(q_ref, kv_ref, sk_ref, skm_ref, o_ref, kvs):
        kvs[...] = kv_ref[...].astype(BF16)
        k_m, v_m = _kv(kvs[0:N_META, :])
        p, _ = _meta_probs(_heads_to_rows(q_ref[0:N_META, :]), k_m, skm_ref[0])
        o_ref[0:N_META, :] = _rows_to_heads(jnp.dot(p.astype(BF16), v_m, preferred_element_type=F32))
        tri = _causal(Q_PER_KV * BLOCK)

        def block(cur, first, keys):
            k2, v2 = _kv(kvs[keys, :])
            qb = _heads_to_rows(q_ref[pl.ds(cur, BLOCK), :])
            st = lax.dot_general(k2, qb, _NT, preferred_element_type=F32)
            smt = lax.dot_general(k_m, qb, _NT, preferred_element_type=F32)
            s_band = jnp.where(tri, st, NEG_INF) if first else _fold_band(tri, st)
            p_b, p_m, _ = _band_probs(s_band, smt, sk_ref[0])
            p2 = (p_b if first else _unfold_band(tri, p_b)).astype(BF16)
            o = (lax.dot_general(p2, v2, _TN, preferred_element_type=F32)
                 + lax.dot_general(p_m.astype(BF16), v_m, _TN, preferred_element_type=F32))
            o_ref[pl.ds(cur, BLOCK), :] = _rows_to_heads(o)

        block(N_META, True, pl.ds(N_META, BLOCK))

        def step(n, carry):
            block(_block_start(n), False, pl.ds(_block_start(n - 1), 2 * BLOCK))
            return carry

        lax.fori_loop(1, nb, step, 0)

    q_spec = pl.BlockSpec((L, QW), lambda b, h: (b, h))
    return _call(body, name, (B, KV),
                 [q_spec, pl.BlockSpec((L, 2 * HEAD_DIM), lambda b, h: (b, h)),
                  pl.BlockSpec((1, 1, Q_PER_KV * BLOCK), lambda b, h: (h, 0, 0)),
                  pl.BlockSpec((1, Q_PER_KV * N_META, 1), lambda b, h: (h, 0, 0))],
                 [q_spec], [jax.ShapeDtypeStruct((T, D), F32)], [pltpu.VMEM((L, 2 * HEAD_DIM), BF16)],
                 ("parallel", "parallel"), (q, kv, sink_row, sink_meta), riders)


def _attn_bwd(q, kv, o, do, sink_row, sink_meta, B, name, riders=None):
    T, D = q.shape
    L = T // B
    KV = D // QW
    nb = (L - N_META) // BLOCK
    R = Q_PER_KV * BLOCK
    scale = HEAD_DIM ** -0.5

    def head_totals(col, rows_per_head):
        rid = lax.broadcasted_iota(jnp.int32, (8, 128), 0)
        out = jnp.zeros((8, 128), F32)
        for g in range(Q_PER_KV):
            out = out + jnp.where(rid == g, jnp.sum(col[g * rows_per_head:(g + 1) * rows_per_head, :]), 0.0)
        return out

    def body(q_ref, kv_ref, o_ref, do_ref, sk_ref, skm_ref, dq_ref, dkv_ref, dsk_ref, kvs, acc, acc_sink):
        b = pl.program_id(1)
        kvs[...] = kv_ref[...].astype(BF16)
        acc[...] = jnp.zeros_like(acc)
        k_m, v_m = _kv(kvs[0:N_META, :])

        qm = _heads_to_rows(q_ref[0:N_META, :])
        dom = _heads_to_rows(do_ref[0:N_META, :])
        delta = jnp.sum(dom * _heads_to_rows(o_ref[0:N_META, :]), axis=-1, keepdims=True)
        p, p_s = _meta_probs(qm, k_m, skm_ref[0])
        domb = dom.astype(BF16)
        ds = (p * (lax.dot_general(domb, v_m, _NT, preferred_element_type=F32) - delta)).astype(BF16)
        dq_ref[0:N_META, :] = _rows_to_heads(jnp.dot(ds, k_m, preferred_element_type=F32) * scale).astype(BF16)
        acc[0:N_META, :] += jnp.concatenate([lax.dot_general(ds, qm, _TN, preferred_element_type=F32),
                                             lax.dot_general(p.astype(BF16), domb, _TN, preferred_element_type=F32)], axis=1)
        sink_tot = head_totals(-p_s * delta, N_META)
        tri = _causal(R)
        acc_sink[...] = jnp.zeros_like(acc_sink)
        ones = jnp.ones((8, HEAD_DIM), BF16)

        def block(cur, first, keys):
            k2, v2 = _kv(kvs[keys, :])
            rows = pl.ds(cur, BLOCK)
            qb = _heads_to_rows(q_ref[rows, :])
            dob = _heads_to_rows(do_ref[rows, :])
            prod = dob * _heads_to_rows(o_ref[rows, :])
            hi = prod.astype(BF16)
            lo = (prod - hi.astype(F32)).astype(BF16)
            delta = (lax.dot_general(ones, hi, _NT, preferred_element_type=F32)
                     + lax.dot_general(ones, lo, _NT, preferred_element_type=F32))[0:1]
            dobb = dob.astype(BF16)
            st = lax.dot_general(k2, qb, _NT, preferred_element_type=F32)
            smt = lax.dot_general(k_m, qb, _NT, preferred_element_type=F32)
            s_band = jnp.where(tri, st, NEG_INF) if first else _fold_band(tri, st)
            p_b, p_m, p_s = _band_probs(s_band, smt, sk_ref[0])
            dpt = lax.dot_general(v2, dobb, _NT, preferred_element_type=F32)
            dpm = lax.dot_general(v_m, dobb, _NT, preferred_element_type=F32)
            ds_b = p_b * ((dpt if first else _fold_band(tri, dpt)) - delta)
            ds2 = (ds_b if first else _unfold_band(tri, ds_b)).astype(BF16)
            p2 = (p_b if first else _unfold_band(tri, p_b)).astype(BF16)
            dsm = (p_m * (dpm - delta)).astype(BF16)
            pm = p_m.astype(BF16)
            dq = (lax.dot_general(ds2, k2, _TN, preferred_element_type=F32)
                  + lax.dot_general(dsm, k_m, _TN, preferred_element_type=F32))
            dq_ref[rows, :] = _rows_to_heads(dq * scale).astype(BF16)
            acc[keys, :] += jnp.concatenate([jnp.dot(ds2, qb, preferred_element_type=F32),
                                             jnp.dot(p2, dobb, preferred_element_type=F32)], axis=1)
            acc[0:N_META, :] += jnp.concatenate([jnp.dot(dsm, qb, preferred_element_type=F32),
                                                 jnp.dot(pm, dobb, preferred_element_type=F32)], axis=1)
            acc_sink[0:1, :] += -p_s * delta

        block(N_META, True, pl.ds(N_META, BLOCK))

        def step(n, carry):
            block(_block_start(n), False, pl.ds(_block_start(n - 1), 2 * BLOCK))
            return carry

        lax.fori_loop(1, nb, step, 0)
        dkv_ref[...] = acc[...].astype(BF16)
        rid = lax.broadcasted_iota(jnp.int32, (8, 128), 0)
        tot = sink_tot
        for g in range(Q_PER_KV):
            tot = tot + jnp.where(rid == g, jnp.sum(acc_sink[:, g * BLOCK:(g + 1) * BLOCK]), 0.0)

        @pl.when(b == 0)
        def _():
            dsk_ref[0] = tot

        @pl.when(b > 0)
        def _():
            dsk_ref[0] += tot

    q_spec = pl.BlockSpec((L, QW), lambda h, b: (b, h))
    kv_spec = pl.BlockSpec((L, 2 * HEAD_DIM), lambda h, b: (b, h))
    return _call(body, name, (KV, B),
                 [q_spec, kv_spec, q_spec, q_spec,
                  pl.BlockSpec((1, 1, R), lambda h, b: (h, 0, 0)),
                  pl.BlockSpec((1, Q_PER_KV * N_META, 1), lambda h, b: (h, 0, 0))],
                 [q_spec, kv_spec, pl.BlockSpec((1, 8, 128), lambda h, b: (h, 0, 0))],
                 [jax.ShapeDtypeStruct((T, D), BF16), jax.ShapeDtypeStruct((T, KV * 2 * HEAD_DIM), BF16),
                  jax.ShapeDtypeStruct((KV, 8, 128), F32)],
                 [pltpu.VMEM((L, 2 * HEAD_DIM), BF16), pltpu.VMEM((L, 2 * HEAD_DIM), F32), pltpu.VMEM((8, R), F32)],
                 ("parallel", "arbitrary"), (q, kv, o, do, sink_row, sink_meta), riders)


def _cmul_add(acc_r, acc_i, lr, li, xr, xi):
    return acc_r + (lr * xr - li * xi), acc_i + (lr * xi + li * xr)


def _cols_per_step(ncol):
    for cps in (4, 2):
        if ncol % cps == 0:
            return cps
    return 1


def _ssm_fwd(u, bmat, cmat, dskip, tables, nbatch, rc, name):
    T, W = u.shape
    ncol = W // SSM_LANES
    nch = T // rc
    S = STATE_LANES
    cps = _cols_per_step(ncol)
    assert nbatch == 4

    def body(u_ref, b_ref, c_ref, d_ref, tab_ref, y_ref, xs_ref, st_ref, carry_ref):
        ch = pl.program_id(1)

        @pl.when(ch == 0)
        def _():
            carry_ref[...] = jnp.zeros_like(carry_ref)

        uv = u_ref[...]
        for k in range(cps):
            st_ref[:, 2 * S * k:2 * S * (k + 1)] = jnp.dot(uv[:, SSM_LANES * k:SSM_LANES * (k + 1)].astype(BF16), b_ref[k],
                                                           preferred_element_type=F32)
        low = lax.broadcasted_iota(jnp.int32, (8, S), 0) < nbatch

        def tile(k, r0, c_r, c_i):
            re, im = slice(2 * S * k, 2 * S * k + S), slice(2 * S * k + S, 2 * S * (k + 1))
            la_r, la_i = tab_ref[k, :, 0:S], tab_ref[k, :, S:2 * S]
            lb_r, lb_i = tab_ref[k, :, 2 * S:3 * S], tab_ref[k, :, 3 * S:4 * S]
            v_r = st_ref[pl.ds(r0, 8), re]
            v_i = st_ref[pl.ds(r0, 8), im]
            v_r, v_i = _cmul_add(v_r, v_i, la_r, la_i, pltpu.roll(v_r, nbatch, 0), pltpu.roll(v_i, nbatch, 0))
            rc_r, rc_i = pltpu.roll(c_r, nbatch, 0), pltpu.roll(c_i, nbatch, 0)
            cb_r, cb_i = jnp.where(low, rc_r, c_r), jnp.where(low, rc_i, c_i)
            v_r, v_i = _cmul_add(v_r, v_i, lb_r, lb_i, cb_r, cb_i)
            st_ref[pl.ds(r0, 8), re] = v_r
            st_ref[pl.ds(r0, 8), im] = v_i
            return v_r, v_i

        def step(i, carry):
            r0 = pl.multiple_of(i * 8, 8)
            out = []
            for k in range(cps):
                out += list(tile(k, r0, carry[2 * k], carry[2 * k + 1]))
            return tuple(out)

        halves = tuple(carry_ref[:, S * j:S * (j + 1)] for j in range(2 * cps))
        halves = lax.fori_loop(0, rc // 8, step, halves)
        for j in range(2 * cps):
            carry_ref[:, S * j:S * (j + 1)] = halves[j]
        xb = st_ref[...].astype(BF16)
        xs_ref[...] = xb
        for k in range(cps):
            cols = slice(SSM_LANES * k, SSM_LANES * (k + 1))
            y_ref[:, cols] = (jnp.dot(xb[:, 2 * S * k:2 * S * (k + 1)], c_ref[k], preferred_element_type=F32)
                              + d_ref[:, cols] * uv[:, cols])

    return pl.pallas_call(
        body, name=name, grid=(ncol // cps, nch),
        in_specs=[pl.BlockSpec((rc, cps * SSM_LANES), lambda g, c: (c, g)),
                  pl.BlockSpec((cps, SSM_LANES, 2 * S), lambda g, c: (g, 0, 0)),
                  pl.BlockSpec((cps, 2 * S, SSM_LANES), lambda g, c: (g, 0, 0)),
                  pl.BlockSpec((1, cps * SSM_LANES), lambda g, c: (0, g)),
                  pl.BlockSpec((cps, 8, 4 * S), lambda g, c: (g, 0, 0))],
        out_specs=[pl.BlockSpec((rc, cps * SSM_LANES), lambda g, c: (c, g)),
                   pl.BlockSpec((rc, cps * 2 * S), lambda g, c: (c, g))],
        out_shape=[jax.ShapeDtypeStruct((T, W), F32), jax.ShapeDtypeStruct((T, ncol * 2 * S), BF16)],
        scratch_shapes=[pltpu.VMEM((rc, cps * 2 * S), F32), pltpu.VMEM((8, cps * 2 * S), F32)],
        compiler_params=_params(("parallel", "arbitrary")),
    )(u, bmat, cmat, dskip, tables)


def _ssm_bwd(dy, u, xs, bmat, cmat, dskip, tables, nbatch, rc, name):
    T, W = u.shape
    ncol = W // SSM_LANES
    nch = T // rc
    S = STATE_LANES
    ntile = rc // 16
    cps = _cols_per_step(ncol)

    def body(dy_ref, u_ref, xs_ref, b_ref, c_ref, d_ref, tab_ref,
             du_ref, db_ref, dc_ref, dl_ref, dd_ref, st_ref, carry_ref, accl_ref, accd_ref):
        ch = pl.program_id(1)

        @pl.when(ch == 0)
        def _():
            carry_ref[...] = jnp.zeros_like(carry_ref)
            accl_ref[...] = jnp.zeros_like(accl_ref)
            accd_ref[...] = jnp.zeros_like(accd_ref)
            db_ref[...] = jnp.zeros_like(db_ref)
            dc_ref[...] = jnp.zeros_like(dc_ref)

        dyv = dy_ref[...]
        uv = u_ref[...]
        dyb = dyv.astype(BF16)
        for k in range(cps):
            st_ref[:, 2 * S * k:2 * S * (k + 1)] = lax.dot_general(dyb[:, SSM_LANES * k:SSM_LANES * (k + 1)], c_ref[k], _NT,
                                                                   preferred_element_type=F32)
        low = lax.broadcasted_iota(jnp.int32, (8, S), 0) < nbatch

        def tile(k, r0, x_r, x_i, c_r, c_i, al_r, al_i):
            re, im = slice(2 * S * k, 2 * S * k + S), slice(2 * S * k + S, 2 * S * (k + 1))
            la_r, la_i = tab_ref[k, :, 0:S], tab_ref[k, :, S:2 * S]
            lb_r, lb_i = tab_ref[k, :, 2 * S:3 * S], tab_ref[k, :, 3 * S:4 * S]
            v_r = st_ref[pl.ds(r0, 8), re]
            v_i = st_ref[pl.ds(r0, 8), im]
            v_r, v_i = _cmul_add(v_r, v_i, la_r, la_i, pltpu.roll(v_r, nbatch, 0), pltpu.roll(v_i, nbatch, 0))
            cb_r = jnp.where(low, c_r, pltpu.roll(c_r, nbatch, 0))
            cb_i = jnp.where(low, c_i, pltpu.roll(c_i, nbatch, 0))
            v_r, v_i = _cmul_add(v_r, v_i, lb_r, lb_i, cb_r, cb_i)
            st_ref[pl.ds(r0, 8), re] = v_r
            st_ref[pl.ds(r0, 8), im] = v_i
            n_r = jnp.where(low, pltpu.roll(v_r, nbatch, 0), cb_r)
            n_i = jnp.where(low, pltpu.roll(v_i, nbatch, 0), cb_i)
            al_r = al_r + (n_r * x_r + n_i * x_i)
            al_i = al_i + (n_i * x_r - n_r * x_i)
            return v_r, v_i, al_r, al_i

        def step(j, carry):
            r0 = pl.multiple_of((ntile - 1 - j) * 16, 16)
            out = []
            for k in range(cps):
                re, im = slice(2 * S * k, 2 * S * k + S), slice(2 * S * k + S, 2 * S * (k + 1))
                x_r = xs_ref[pl.ds(r0, 16), re].astype(F32)
                x_i = xs_ref[pl.ds(r0, 16), im].astype(F32)
                mid = tile(k, r0 + 8, x_r[8:16], x_i[8:16], *carry[4 * k:4 * k + 4])
                out += list(tile(k, r0, x_r[0:8], x_i[0:8], *mid))
            return tuple(out)

        init = []
        for k in range(cps):
            init += [carry_ref[:, 2 * S * k:2 * S * k + S], carry_ref[:, 2 * S * k + S:2 * S * (k + 1)],
                     accl_ref[:, 2 * S * k:2 * S * k + S], accl_ref[:, 2 * S * k + S:2 * S * (k + 1)]]
        fin = lax.fori_loop(0, ntile, step, tuple(init))
        for k in range(cps):
            carry_ref[:, 2 * S * k:2 * S * k + S] = fin[4 * k]
            carry_ref[:, 2 * S * k + S:2 * S * (k + 1)] = fin[4 * k + 1]
            accl_ref[:, 2 * S * k:2 * S * k + S] = fin[4 * k + 2]
            accl_ref[:, 2 * S * k + S:2 * S * (k + 1)] = fin[4 * k + 3]
        dsb = st_ref[...].astype(BF16)
        ub = uv.astype(BF16)
        for k in range(cps):
            cols, lanes = slice(SSM_LANES * k, SSM_LANES * (k + 1)), slice(2 * S * k, 2 * S * (k + 1))
            du_ref[:, cols] = (lax.dot_general(dsb[:, lanes], b_ref[k], _NT, preferred_element_type=F32)
                               + d_ref[:, cols] * dyv[:, cols])
            db_ref[k] += lax.dot_general(ub[:, cols], dsb[:, lanes], _TN, preferred_element_type=F32)
            dc_ref[k] += lax.dot_general(xs_ref[:, lanes], dyb[:, cols], _TN, preferred_element_type=F32)
        accd_ref[...] += _fold8(dyv * uv)

        @pl.when(ch == nch - 1)
        def _():
            for k in range(cps):
                dl_ref[k] = jnp.sum(accl_ref[:, 2 * S * k:2 * S * (k + 1)], axis=0, keepdims=True)
            dd_ref[...] = jnp.sum(accd_ref[...], axis=0, keepdims=True)

    rev = lambda g, c: (nch - 1 - c, g)
    return pl.pallas_call(
        body, name=name, grid=(ncol // cps, nch),
        in_specs=[pl.BlockSpec((rc, cps * SSM_LANES), rev), pl.BlockSpec((rc, cps * SSM_LANES), rev),
                  pl.BlockSpec((rc, cps * 2 * S), rev),
                  pl.BlockSpec((cps, SSM_LANES, 2 * S), lambda g, c: (g, 0, 0)),
                  pl.BlockSpec((cps, 2 * S, SSM_LANES), lambda g, c: (g, 0, 0)),
                  pl.BlockSpec((1, cps * SSM_LANES), lambda g, c: (0, g)),
                  pl.BlockSpec((cps, 8, 4 * S), lambda g, c: (g, 0, 0))],
        out_specs=[pl.BlockSpec((rc, cps * SSM_LANES), rev),
                   pl.BlockSpec((cps, SSM_LANES, 2 * S), lambda g, c: (g, 0, 0)),
                   pl.BlockSpec((cps, 2 * S, SSM_LANES), lambda g, c: (g, 0, 0)),
                   pl.BlockSpec((cps, 1, 2 * S), lambda g, c: (g, 0, 0)),
                   pl.BlockSpec((1, cps * SSM_LANES), lambda g, c: (0, g))],
        out_shape=[jax.ShapeDtypeStruct((T, W), F32),
                   jax.ShapeDtypeStruct((ncol, SSM_LANES, 2 * S), F32),
                   jax.ShapeDtypeStruct((ncol, 2 * S, SSM_LANES), F32),
                   jax.ShapeDtypeStruct((ncol, 1, 2 * S), F32),
                   jax.ShapeDtypeStruct((1, W), F32)],
        scratch_shapes=[pltpu.VMEM((rc, cps * 2 * S), F32), pltpu.VMEM((8, cps * 2 * S), F32),
                        pltpu.VMEM((8, cps * 2 * S), F32), pltpu.VMEM((8, cps * SSM_LANES), F32)],
        compiler_params=_params(("parallel", "arbitrary")),
    )(dy, u, xs, bmat, cmat, dskip, tables)


def _ssm_matrices(a_re, a_im, log_step, b_re, b_im, c_re, c_im):
    G, N = a_re.shape
    ncol = G // GROUPS_PER_COL
    step = jnp.exp(log_step)[:, None]
    mag = jnp.exp(a_re * step)
    ang = a_im * step
    lam_re, lam_im = mag * jnp.cos(ang), mag * jnp.sin(ang)
    den = a_re * a_re + a_im * a_im
    nr, ni = lam_re - 1.0, lam_im
    coef_re = (nr * a_re + ni * a_im) / den
    coef_im = (ni * a_re - nr * a_im) / den
    bb_re = coef_re[..., None] * b_re - coef_im[..., None] * b_im
    bb_im = coef_re[..., None] * b_im + coef_im[..., None] * b_re
    eye = jnp.eye(GROUPS_PER_COL, dtype=F32)
    bb = jnp.stack([bb_re, bb_im]).reshape(2, ncol, GROUPS_PER_COL, N, SSM_GROUP)
    bmat = jnp.einsum("pbgnc,gh->bgcphn", bb, eye).reshape(ncol, SSM_LANES, 2 * STATE_LANES)
    cc = jnp.stack([c_re, -c_im]).reshape(2, ncol, GROUPS_PER_COL, SSM_GROUP, N)
    cmat = jnp.einsum("pbgcn,gh->bpgnhc", cc, eye).reshape(ncol, 2 * STATE_LANES, SSM_LANES)
    lam = jnp.concatenate([lam_re.reshape(ncol, STATE_LANES), lam_im.reshape(ncol, STATE_LANES)], axis=-1)
    return lam, bmat, cmat


def _scan_tables(lam, nbatch, conj):
    S = STATE_LANES
    lr, li = lam[:, None, 0:S], lam[:, None, S:2 * S]
    if conj:
        li = -li
    l2r, l2i = lr * lr - li * li, 2.0 * lr * li
    first = (jnp.arange(8) < nbatch)[None, :, None]
    zero = jnp.zeros_like(lr)
    if conj:
        parts = [jnp.where(first, lr, zero), jnp.where(first, li, zero), jnp.where(first, l2r, lr), jnp.where(first, l2i, li)]
    else:
        parts = [jnp.where(first, zero, lr), jnp.where(first, zero, li), jnp.where(first, lr, l2r), jnp.where(first, li, l2i)]
    return jnp.concatenate([jnp.broadcast_to(p, (lam.shape[0], 8, S)) for p in parts], axis=-1)


def _adamw_update(w_ref, g_ref, m_ref, v_ref, d_ref, nm_ref, nv_ref):
    gv = g_ref[...]
    mn = ADAM_B1 * m_ref[...] + (1.0 - ADAM_B1) * gv
    vn = ADAM_B2 * v_ref[...] + (1.0 - ADAM_B2) * (gv * gv)
    m_hat = mn / (1.0 - ADAM_B1 ** ADAM_STEP)
    v_hat = vn / (1.0 - ADAM_B2 ** ADAM_STEP)
    d_ref[...] = -ADAM_LR * (m_hat / (jnp.sqrt(v_hat) + ADAM_EPS) + ADAM_WD * w_ref[...])
    nm_ref[...] = mn
    nv_ref[...] = vn


def _adamw_small(ws, gs, ms, vs, name):
    n = len(ws)

    def body(*refs):
        for i in range(n):
            _adamw_update(refs[i], refs[n + i], refs[2 * n + i], refs[3 * n + i],
                          refs[4 * n + i], refs[5 * n + i], refs[6 * n + i])

    vm = pl.BlockSpec(memory_space=pltpu.VMEM)
    shapes = [jax.ShapeDtypeStruct(a.shape, F32) for a in ws]
    outs = pl.pallas_call(body, name=name, in_specs=[vm] * (4 * n), out_specs=[vm] * (3 * n), out_shape=shapes * 3,
                          compiler_params=pltpu.CompilerParams(vmem_limit_bytes=VMEM_LIMIT))(*ws, *gs, *ms, *vs)
    return outs[:n], outs[n:2 * n], outs[2 * n:]


def _adamw(w, g, m, v, name):
    R, C = w.shape[-2], w.shape[-1]
    tr = R if R <= 512 else _pick_tile(R, 512, 8)
    body = functools.partial(_adamw_update)

    def spec_for(a):
        if len(a.shape) == 2:
            return pl.BlockSpec((tr, C), lambda i: (i, 0))
        return pl.BlockSpec((None, tr, C), lambda i: (0, i, 0))

    spec = spec_for(w)
    shp = jax.ShapeDtypeStruct(w.shape, F32)
    return pl.pallas_call(body, name=name, grid=(R // tr,), in_specs=[spec, spec_for(g), spec, spec], out_specs=[spec] * 3,
                          out_shape=[shp, shp, shp], compiler_params=_params(("parallel",)))(w, g, m, v)


_ANY = pl.BlockSpec(memory_space=pl.ANY)


def _place():
    x, y, c = lax.axis_index("x"), lax.axis_index("y"), lax.axis_index("c")
    chips = [(1 - x, y), (x, 1 - y), (1 - x, 1 - y)]
    return x, y, c, chips


def _remote(src, dst, send_sems, recv_sems, k, to):
    return pltpu.make_async_remote_copy(src_ref=src, dst_ref=dst, send_sem=send_sems.at[k], recv_sem=recv_sems.at[k],
                                        device_id=to, device_id_type=MESH_IDS)


class _Riders:
    def __init__(self, srcs, out_shapes, n_sems, copies):
        self.srcs, self.out_shapes, self.n_sems, self.copies = list(srcs), list(out_shapes), n_sems, copies


def _call(body, name, grid, in_specs, out_specs, out_shape, scratch_shapes, sem, args, riders=None):
    if riders is None:
        return pl.pallas_call(body, name=name, grid=grid, in_specs=in_specs, out_specs=out_specs, out_shape=out_shape,
                              scratch_shapes=scratch_shapes, compiler_params=_params(sem))(*args)
    n_in, n_out, n_scr = len(in_specs), len(out_specs), len(scratch_shapes)
    r_in, r_out = len(riders.srcs), len(riders.out_shapes)

    def carrying(*refs):
        a, b = n_in, n_in + r_in
        c, d = b + n_out, b + n_out + r_out
        e = d + n_scr
        sends, arrivals = riders.copies(refs[a:b], refs[c:d], refs[e], refs[e + 1])
        first, last = None, None
        for ax, size in enumerate(grid):
            at0, at1 = pl.program_id(ax) == 0, pl.program_id(ax) == size - 1
            first = at0 if first is None else first & at0
            last = at1 if last is None else last & at1

        @pl.when(first)
        def _():
            for cp in sends:
                cp.start()

        body(*refs[:a], *refs[b:c], *refs[d:e])

        @pl.when(last)
        def _():
            for cp in arrivals:
                cp.wait_recv()
            for cp in sends:
                cp.wait_send()

    outs = pl.pallas_call(
        carrying, name=name, grid=grid, in_specs=list(in_specs) + [_ANY] * r_in,
        out_specs=list(out_specs) + [_ANY] * r_out, out_shape=list(out_shape) + riders.out_shapes,
        scratch_shapes=list(scratch_shapes) + [pltpu.SemaphoreType.DMA((riders.n_sems,)),
                                               pltpu.SemaphoreType.DMA((riders.n_sems,))],
        compiler_params=pltpu.CompilerParams(dimension_semantics=("arbitrary",) * len(grid),
                                             vmem_limit_bytes=VMEM_LIMIT, has_side_effects=True),
    )(*args, *riders.srcs)
    return outs[:n_out], outs[n_out:]


def _gather_riders(shards):
    def copies(srcs, outs, send_sems, recv_sems):
        x, y, c, chips = _place()
        sends, arrivals = [], []
        for i, s in enumerate(shards):
            half = s.shape[0] // 2
            rows = pl.ds(c * half, half)
            for j, chip in enumerate(chips):
                sends.append(_remote(srcs[i].at[rows, :], outs[i].at[2 * x + y, rows, :], send_sems, recv_sems,
                                     3 * i + j, (*chip, c)))
                landed = outs[i].at[2 * chip[0] + chip[1], rows, :]
                arrivals.append(_remote(landed, landed, send_sems, recv_sems, 3 * i + j, (*chip, c)))
        return sends, arrivals

    return _Riders(shards, [jax.ShapeDtypeStruct((N_CHIPS,) + s.shape, s.dtype) for s in shards], 3 * len(shards), copies)


def _exchange_riders(parts):
    def copies(srcs, outs, send_sems, recv_sems):
        x, y, c, chips = _place()
        sends = [_remote(srcs[i].at[2 * chip[0] + chip[1]], outs[i].at[j], send_sems, recv_sems, 3 * i + j, (*chip, c))
                 for i in range(len(parts)) for j, chip in enumerate(chips)]
        return sends, sends

    return _Riders(parts, [jax.ShapeDtypeStruct((3,) + p.shape[1:], p.dtype) for p in parts], 3 * len(parts), copies)


def _swap_riders(grads):
    def copies(srcs, outs, send_sems, recv_sems):
        x, y, c, _ = _place()
        sends = []
        for i, g in enumerate(grads):
            half = g.shape[1] // 2
            sends.append(_remote(srcs[i].at[:, pl.ds((1 - c) * half, half), :], outs[i], send_sems, recv_sems, i,
                                 (x, y, 1 - c)))
        return sends, sends

    return _Riders(grads, [jax.ShapeDtypeStruct((N_CHIPS, g.shape[1] // 2, g.shape[2]), g.dtype) for g in grads],
                   len(grads), copies)


def _forward_halves(gathered, shards, tag):
    n = len(gathered)

    def body(*refs):
        srcs, outs = refs[:n], refs[n:2 * n]
        send_sems, recv_sems = refs[2 * n:]
        x, y, c, chips = _place()
        sibling = (x, y, 1 - c)
        cps = []
        for i in range(n):
            half = gathered[i].shape[1] // 2
            for j, chip in enumerate(chips):
                slot = 2 * chip[0] + chip[1]
                cps.append(_remote(srcs[i].at[slot, pl.ds(c * half, half), :], outs[i].at[slot, pl.ds(c * half, half), :],
                                   send_sems, recv_sems, 3 * i + j, sibling))
        for cp in cps:
            cp.start()
        for i in range(n):
            half = gathered[i].shape[1] // 2
            for j, chip in enumerate(chips):
                theirs = outs[i].at[2 * chip[0] + chip[1], pl.ds((1 - c) * half, half), :]
                _remote(theirs, theirs, send_sems, recv_sems, 3 * i + j, sibling).wait_recv()
        for cp in cps:
            cp.wait_send()

    outs = pl.pallas_call(
        body, name=f"gather_forward_{tag}", in_specs=[_ANY] * n, out_specs=[_ANY] * n,
        out_shape=[jax.ShapeDtypeStruct(g.shape, g.dtype) for g in gathered],
        input_output_aliases={i: i for i in range(n)},
        scratch_shapes=[pltpu.SemaphoreType.DMA((3 * n,)), pltpu.SemaphoreType.DMA((3 * n,))],
        compiler_params=pltpu.CompilerParams(has_side_effects=True),
    )(*gathered)
    slot = 2 * lax.axis_index("x") + lax.axis_index("y")
    return [lax.dynamic_update_slice(o, s[None], (slot, 0, 0)) for o, s in zip(outs, shards)]


def _gather_weights(shards):
    n = len(shards)

    def body(*refs):
        srcs, outs = refs[:n], refs[n:2 * n]
        send_sems, recv_sems = refs[2 * n:]
        x, y, c, chips = _place()
        sibling = (x, y, 1 - c)

        def piece(i, px, py, pc):
            half = shards[i].shape[0] // 2
            return outs[i].at[2 * px + py, pl.ds(pc * half, half), :]

        first = []
        for i in range(n):
            half = shards[i].shape[0] // 2
            for j, chip in enumerate(chips):
                first.append(_remote(srcs[i].at[pl.ds(c * half, half), :], piece(i, x, y, c), send_sems, recv_sems,
                                     6 * i + j, (*chip, c)))
        for cp in first:
            cp.start()
        passed = []
        for i in range(n):
            for j, chip in enumerate(chips):
                _remote(piece(i, *chip, c), piece(i, *chip, c), send_sems, recv_sems, 6 * i + j, (*chip, c)).wait_recv()
                cp = _remote(piece(i, *chip, c), piece(i, *chip, c), send_sems, recv_sems, 6 * i + 3 + j, sibling)
                cp.start()
                passed.append(cp)
        for i in range(n):
            for j, chip in enumerate(chips):
                _remote(piece(i, *chip, 1 - c), piece(i, *chip, 1 - c), send_sems, recv_sems, 6 * i + 3 + j,
                        sibling).wait_recv()
        for cp in first + passed:
            cp.wait_send()

    outs = pl.pallas_call(
        body, name="gather_weights", in_specs=[_ANY] * n, out_specs=[_ANY] * n,
        out_shape=[jax.ShapeDtypeStruct((N_CHIPS,) + s.shape, s.dtype) for s in shards],
        scratch_shapes=[pltpu.SemaphoreType.DMA((6 * n,)), pltpu.SemaphoreType.DMA((6 * n,))],
        compiler_params=pltpu.CompilerParams(has_side_effects=True),
    )(*shards)
    slot = 2 * lax.axis_index("x") + lax.axis_index("y")
    return [lax.dynamic_update_slice(o, s[None], (slot, 0, 0)) for o, s in zip(outs, shards)]


def _swap_halves(grads, tag):
    n = len(grads)

    def body(*refs):
        srcs, outs = refs[:n], refs[n:2 * n]
        send_sems, recv_sems = refs[2 * n:]
        x, y, c, _ = _place()
        cps = []
        for i in range(n):
            half = grads[i].shape[1] // 2
            cps.append(_remote(srcs[i].at[:, pl.ds((1 - c) * half, half), :], outs[i], send_sems, recv_sems, i, (x, y, 1 - c)))
        for cp in cps:
            cp.start()
        for cp in cps:
            cp.wait()

    return pl.pallas_call(
        body, name=f"grad_swap_halves_{tag}", in_specs=[_ANY] * n, out_specs=[_ANY] * n,
        out_shape=[jax.ShapeDtypeStruct((N_CHIPS, g.shape[1] // 2, g.shape[2]), g.dtype) for g in grads],
        scratch_shapes=[pltpu.SemaphoreType.DMA((n,)), pltpu.SemaphoreType.DMA((n,))],
        compiler_params=pltpu.CompilerParams(has_side_effects=True),
    )(*grads)


def _join_halves(fulls):
    n = len(fulls)

    def body(*refs):
        srcs, outs = refs[:n], refs[n:2 * n]
        send_sems, recv_sems = refs[2 * n:]
        x, y, c, _ = _place()
        sibling = (x, y, 1 - c)
        cps = []
        for i in range(n):
            h = fulls[i].shape[0] // 2
            cps.append(_remote(srcs[i].at[pl.ds(c * h, h), :], outs[i].at[pl.ds(c * h, h), :], send_sems, recv_sems, i,
                               sibling))
        for cp in cps:
            cp.start()
        for i in range(n):
            h = fulls[i].shape[0] // 2
            theirs = outs[i].at[pl.ds((1 - c) * h, h), :]
            _remote(theirs, theirs, send_sems, recv_sems, i, sibling).wait_recv()
        for cp in cps:
            cp.wait_send()

    return pl.pallas_call(
        body, name="grad_join_halves", in_specs=[_ANY] * n, out_specs=[_ANY] * n,
        out_shape=[jax.ShapeDtypeStruct(f.shape, f.dtype) for f in fulls],
        input_output_aliases={i: i for i in range(n)},
        scratch_shapes=[pltpu.SemaphoreType.DMA((n,)), pltpu.SemaphoreType.DMA((n,))],
        compiler_params=pltpu.CompilerParams(has_side_effects=True),
    )(*fulls)


def _half_tile(h):
    return h if h <= 512 else _pick_tile(h, 512, ROW_ALIGN)


def _sum_halves(g, r1, c_idx, name):
    _, R, C = g.shape
    H = R // 2
    tr = _half_tile(H)
    nblk = H // tr

    def body(c_ref, g_ref, r_ref, p_ref):
        p_ref[...] = (g_ref[...] + r_ref[...]).astype(BF16)

    half = pl.BlockSpec((None, tr, C), lambda s, i, c_ref: (s, c_ref[0] * nblk + i, 0))
    plain = pl.BlockSpec((None, tr, C), lambda s, i, c_ref: (s, i, 0))
    return pl.pallas_call(
        body, name=name,
        grid_spec=pltpu.PrefetchScalarGridSpec(num_scalar_prefetch=1, grid=(N_CHIPS, nblk), in_specs=[half, plain],
                                               out_specs=plain),
        out_shape=jax.ShapeDtypeStruct((N_CHIPS, H, C), BF16),
        compiler_params=_params(("parallel", "parallel")),
    )(c_idx, g, r1)


def _sum_chips(g, r1, r2, idx, name):
    _, R, C = g.shape
    H = R // 2
    tr = _half_tile(H)
    nblk = H // tr

    def body(idx_ref, g_ref, r1_ref, r2_ref, o_ref):
        o_ref[...] = (((g_ref[...] + r1_ref[...]) + r2_ref[0].astype(F32)) + r2_ref[1].astype(F32)) + r2_ref[2].astype(F32)

    return pl.pallas_call(
        body, name=name,
        grid_spec=pltpu.PrefetchScalarGridSpec(
            num_scalar_prefetch=1, grid=(nblk,),
            in_specs=[pl.BlockSpec((None, tr, C), lambda i, idx_ref: (idx_ref[0], idx_ref[1] * nblk + i, 0)),
                      pl.BlockSpec((None, tr, C), lambda i, idx_ref: (idx_ref[0], i, 0)),
                      pl.BlockSpec((3, tr, C), lambda i, idx_ref: (0, i, 0))],
            out_specs=pl.BlockSpec((tr, C), lambda i, idx_ref: (idx_ref[1] * nblk + i, 0))),
        out_shape=jax.ShapeDtypeStruct((R, C), F32),
        compiler_params=_params(("parallel",)),
    )(idx, g, r1, r2)


def _all_reduce_small(v, n_fold, fold_rows, fold_at):
    M, N = v.shape

    def body(x_ref, tot_ref, fold_ref, all_ref, send_sems, recv_sems, local_sem):
        x, y, c, chips = _place()
        me, sibling = (x, y, c), (x, y, 1 - c)

        def rows(px, py, pc):
            return all_ref.at[pl.ds((4 * px + 2 * py + pc) * M, M), :]

        def copy(k, block, to, src=None):
            return _remote(rows(*block) if src is None else src, rows(*block), send_sems, recv_sems, k, to)

        mine = pltpu.make_async_copy(x_ref, rows(*me), local_sem)
        mine.start()
        first = [copy(0, me, sibling, src=x_ref)]
        first += [copy(1 + j, me, (*chip, c), src=x_ref) for j, chip in enumerate(chips)]
        for cp in first:
            cp.start()
        passed = [copy(4 + j, (*chip, c), sibling) for j, chip in enumerate(chips)]
        for j, chip in enumerate(chips):
            copy(1 + j, (*chip, c), me).wait_recv()
            passed[j].start()
        copy(0, sibling, me).wait_recv()
        for j, chip in enumerate(chips):
            copy(4 + j, (*chip, 1 - c), me).wait_recv()
        for cp in first + passed:
            cp.wait_send()
        mine.wait()
        tot = all_ref[0:M, :]
        for d in range(1, 8):
            tot = tot + all_ref[d * M:(d + 1) * M, :]
        tot_ref[...] = tot
        f = tot[fold_at:fold_at + fold_rows, :]
        for e in range(1, n_fold):
            f = f + tot[fold_at + e * fold_rows:fold_at + (e + 1) * fold_rows, :]
        fold_ref[...] = f

    vm = pl.BlockSpec(memory_space=pltpu.VMEM)
    return pl.pallas_call(
        body, name="all_reduce_small", in_specs=[vm], out_specs=[vm, vm],
        out_shape=[jax.ShapeDtypeStruct((M, N), F32), jax.ShapeDtypeStruct((fold_rows, N), F32)],
        scratch_shapes=[pltpu.VMEM((8 * M, N), F32), pltpu.SemaphoreType.DMA((7,)), pltpu.SemaphoreType.DMA((7,)),
                        pltpu.SemaphoreType.DMA],
        compiler_params=pltpu.CompilerParams(has_side_effects=True, vmem_limit_bytes=VMEM_LIMIT),
    )(v)


def _as_rows(a, width):
    flat = a.reshape(-1)
    pad = (-flat.shape[0]) % width
    if pad:
        flat = jnp.concatenate([flat, jnp.zeros((pad,), flat.dtype)])
    return flat.reshape(-1, width)


class _Layout:
    def __init__(self, width, total_mult):
        self.width, self.total_mult = width, total_mult
        self.offsets, self.shapes, self.rows = {}, {}, 0

    def add(self, name, shape):
        r = -(-math.prod(shape) // self.width)
        self.offsets[name], self.shapes[name] = (self.rows, r), tuple(shape)
        self.rows += r

    def align(self, mult):
        gap = (-self.rows) % mult
        if gap:
            self.offsets[f"_gap{self.rows}"], self.shapes[f"_gap{self.rows}"] = (self.rows, gap), (gap, self.width)
            self.rows += gap
        return self.rows

    def pack(self, pieces):
        self.align(self.total_mult)
        parts = [_as_rows(pieces[n].astype(F32), self.width) if n in pieces else jnp.zeros(self.shapes[n], F32)
                 for n in self.offsets]
        return jnp.concatenate(parts, axis=0)

    def unpack(self, buf, name):
        off, r = self.offsets[name]
        shape = self.shapes[name]
        return buf[off:off + r].reshape(-1)[:math.prod(shape)].reshape(shape)


_BIG = ["ffn1_w1", "ffn1_w3", "ffn1_w2", "w_in", "ssm_glu_a", "ssm_glu_b", "w_out", "ffn2_w1", "ffn2_w3", "ffn2_w2"]
_SMALL = ["ffn1_norm", "mix_norm", "ffn2_norm", "final_norm", "attn_sinks", "ssm_a_re", "ssm_a_im", "ssm_log_step",
          "ssm_b_re", "ssm_b_im", "ssm_c_re", "ssm_c_im", "ssm_d"]
_WEIGHTS = ["meta_tokens", "ffn1_norm", "ffn1_w1", "ffn1_w3", "ffn1_w2", "mix_norm", "w_in", "attn_sinks", "ssm_a_re",
            "ssm_a_im", "ssm_log_step", "ssm_b_re", "ssm_b_im", "ssm_c_re", "ssm_c_im", "ssm_d", "ssm_glu_a",
            "ssm_glu_b", "w_out", "ffn2_norm", "ffn2_w1", "ffn2_w3", "ffn2_w2", "final_norm"]


def _kv_interleave(w, kv_heads):
    kvw = kv_heads * HEAD_DIM
    lead = w.shape[:-1]
    k = w[..., 0:kvw].reshape(lead + (kv_heads, 1, HEAD_DIM))
    v = w[..., kvw:2 * kvw].reshape(lead + (kv_heads, 1, HEAD_DIM))
    return jnp.concatenate([jnp.concatenate([k, v], axis=-2).reshape(lead + (2 * kvw,)), w[..., 2 * kvw:]], axis=-1)


def _kv_deinterleave(w, kv_heads):
    kvw = kv_heads * HEAD_DIM
    lead = w.shape[:-1]
    kv = w[..., 0:2 * kvw].reshape(lead + (kv_heads, 2, HEAD_DIM))
    return jnp.concatenate([kv[..., 0, :].reshape(lead + (kvw,)), kv[..., 1, :].reshape(lead + (kvw,)), w[..., 2 * kvw:]],
                           axis=-1)


def _step(x, target, w, m, v):
    B, S, D = x.shape
    L = S + N_META
    T = B * L
    H = D // HEAD_DIM
    KV = H // Q_PER_KV
    SW = D // 2
    tm = _pick_tile(L, ROW_TILE_CAP, ROW_ALIGN)
    rc = _pick_tile(L, ROW_TILE_CAP // B, 4) * B
    tw = _pick_tile(T, 3 * ROW_TILE_CAP, ROW_ALIGN)
    my_c = lax.axis_index("c")
    my_slot = 2 * lax.axis_index("x") + lax.axis_index("y")

    groups = {"ffn1": ["ffn1_w1", "ffn1_w3", "ffn1_w2"], "mix": ["w_in", "ssm_glu_a", "ssm_glu_b", "w_out"],
              "ffn2": ["ffn2_w1", "ffn2_w3", "ffn2_w2"]}
    waves = {"first": ["ffn1_w1", "ffn1_w3"], "early": ["ffn1_w2"] + groups["mix"], "late": groups["ffn2"]}
    shards = {n: w[n][0].astype(BF16) for n in _BIG}
    gathered = _gather_weights([shards[n] for n in waves["first"]] + [w["meta_tokens"]])
    ws = dict(zip(waves["first"], gathered[:-1]))
    meta = jnp.transpose(gathered[-1], (1, 0, 2)).reshape(N_META, D)

    def arrive(wave, landed):
        mine = [shards[n] for n in waves[wave]]
        ws.update(zip(waves[wave], _forward_halves(landed, mine, wave)))

    g_ffn1, g_mix, g_ffn2 = w["ffn1_norm"], w["mix_norm"], w["ffn2_norm"]
    g_final = w["final_norm"].reshape(1, D)

    h0 = jnp.concatenate([jnp.broadcast_to(meta[None], (B, N_META, D)), x], axis=1).reshape(T, D)

    def ffn_fwd(h, g, tag, carry=None):
        n = _rmsnorm_fwd(h, g, tm, f"{tag}_norm")
        riders = None if carry is None else _gather_riders([shards[k] for k in waves[carry]])
        out = _ffn_up(n, ws[f"{tag}_w1"], ws[f"{tag}_w3"], tm, f"{tag}_up", riders)
        if carry is not None:
            out, landed = out
            arrive(carry, landed)
        a, c, s = out
        return _ffn_down(s, ws[f"{tag}_w2"], h, tm, f"{tag}_down"), (n, a, c, s)

    h1, saved1 = ffn_fwd(h0, g_ffn1, "ffn1", carry="early")
    w_kvu = _kv_interleave(ws["w_in"][1], KV)
    hn = _rmsnorm_fwd(h1, g_mix, tm, "mix_norm")
    q = _mm_colslots(hn, ws["w_in"], BF16, "w_in_q", tm, first=0, count=1, scale=HEAD_DIM ** -0.5)
    kvu = _mm_plain(hn, w_kvu, "nn", F32, "w_in_kvu", tm)
    gates = _mm_colslots(hn, ws["w_in"], F32, "w_in_gates", tm, first=2, count=2)

    sinks = w["attn_sinks"].reshape(KV, Q_PER_KV, 1, 1)
    sink_row = jnp.broadcast_to(sinks.reshape(KV, 1, Q_PER_KV, 1), (KV, 1, Q_PER_KV, BLOCK)).reshape(KV, 1, Q_PER_KV * BLOCK)
    sink_meta = jnp.broadcast_to(sinks, (KV, Q_PER_KV, N_META, 1)).reshape(KV, Q_PER_KV * N_META, 1)
    (attn,), landed = _attn_fwd(q, kvu, sink_row, sink_meta, B, "attn_fwd",
                                _gather_riders([shards[k] for k in waves["late"]]))
    arrive("late", landed)

    def to_time_major(a2d):
        return jnp.transpose(a2d.reshape(B, L, a2d.shape[-1]), (1, 0, 2)).reshape(T, a2d.shape[-1])

    def to_batch_major(a2d):
        return jnp.transpose(a2d.reshape(L, B, a2d.shape[-1]), (1, 0, 2)).reshape(T, a2d.shape[-1])

    ssm_args = (w["ssm_a_re"][0], w["ssm_a_im"][0], w["ssm_log_step"][0], w["ssm_b_re"][0], w["ssm_b_im"][0],
                w["ssm_c_re"][0], w["ssm_c_im"][0])
    (lam, bmat, cmat), ssm_vjp = jax.vjp(_ssm_matrices, *ssm_args)
    bmat16, cmat16 = bmat.astype(BF16), cmat.astype(BF16)
    u_t = to_time_major(kvu[:, SW:])
    y_t, xs = _ssm_fwd(u_t, bmat16, cmat16, w["ssm_d"], _scan_tables(lam, B, False), B, rc, "ssm_fwd")
    y0 = to_batch_major(y_t)
    yg = _gelu_fwd(y0, tm, "gelu_fwd")
    ga = _mm_colslots(yg, ws["ssm_glu_a"], F32, "glu_a", tm)
    gb = _mm_colslots(yg, ws["ssm_glu_b"], F32, "glu_b", tm)
    merged = _merge_fwd(gates, attn, ga, gb, tm, "merge_fwd")
    h2 = _mm_rowslots(merged, ws["w_out"], h1, tm, "w_out")
    h3, saved2 = ffn_fwd(h2, g_ffn2, "ffn2")
    dh3, dh3b, dg_final, loss_row = _loss_head(h3, g_final, target, tm, "loss_head")

    grads, swapped, received = {}, {}, {}
    c_idx = my_c.reshape(1).astype(jnp.int32)
    idx = jnp.stack([my_slot, my_c]).astype(jnp.int32)

    def swap_riders(group):
        return _swap_riders([grads[n] for n in groups[group]])

    def exchange_riders(group):
        names = groups[group]
        if names[0] not in swapped:
            swapped.update(zip(names, _swap_halves([grads[n] for n in names], group)))
        return _exchange_riders([_sum_halves(grads[n], swapped[n], c_idx, f"grad_sum_halves_{n}") for n in names])

    def ffn_bwd(h, g, saved, dh, dhb, tag, dhidden_carries=None, dn_carries=None):
        n, a, c, s = saved
        w1, w3, w2 = ws[f"{tag}_w1"], ws[f"{tag}_w3"], ws[f"{tag}_w2"]
        grads[f"{tag}_w2"] = _wgrad_hidden_rows(s, dhb, tw, f"{tag}_dw2", 0.5)
        if dhidden_carries is None:
            da, dc = _ffn_dhidden(dhb, w2, a, c, tm, f"{tag}_dhidden")
        else:
            (da, dc), got = _ffn_dhidden(dhb, w2, a, c, tm, f"{tag}_dhidden", exchange_riders(dhidden_carries[1]))
            received.update(zip(groups[dhidden_carries[1]], got))
        grads[f"{tag}_w1"] = _wgrad_hidden_cols(n, da, tw, f"{tag}_dw1")
        grads[f"{tag}_w3"] = _wgrad_hidden_cols(n, dc, tw, f"{tag}_dw3")
        kind, group = dn_carries
        riders = swap_riders(group) if kind == "swap" else exchange_riders(group)
        (dh_in, dhb_in, grads[f"{tag}_norm"]), got = _ffn_dn(da, w1, dc, w3, h, g, dh, tm, f"{tag}_dn", riders)
        return dh_in, dhb_in, got

    dh2, dh2b, got = ffn_bwd(h2, g_ffn2, saved2, dh3, dh3b, "ffn2", dn_carries=("swap", "ffn2"))
    swapped.update(zip(groups["ffn2"], got))

    grads["w_out"] = _wgrad_rowslots(merged, dh2b, tw, "dw_out")
    dattn, dgat, dgss, dga, dgb = _merge_bwd(dh2b, ws["w_out"], gates, attn, ga, gb, tm, "merge_bwd")
    grads["ssm_glu_a"] = _wgrad_colslots(yg, dga, tw, "dglu_a")
    grads["ssm_glu_b"] = _wgrad_colslots(yg, dgb, tw, "dglu_b")
    dy0 = _gelu_bwd([(dga, ws["ssm_glu_a"]), (dgb, ws["ssm_glu_b"])], y0, tm, "gelu_bwd")
    du_t, dbmat, dcmat, dlam, dd = _ssm_bwd(to_time_major(dy0), u_t, xs, bmat16, cmat16, w["ssm_d"],
                                            _scan_tables(lam, B, True), B, rc, "ssm_bwd")
    d_ssm = ssm_vjp((dlam[:, 0, :], dbmat, dcmat))
    for n, gval in zip(["ssm_a_re", "ssm_a_im", "ssm_log_step", "ssm_b_re", "ssm_b_im", "ssm_c_re", "ssm_c_im"], d_ssm):
        grads[n] = gval[None]
    grads["ssm_d"] = dd

    (dq, dkv, dsink), got = _attn_bwd(q, kvu, attn, dattn, sink_row, sink_meta, B, "attn_bwd",
                                      exchange_riders("ffn2"))
    received.update(zip(groups["ffn2"], got))
    grads["attn_sinks"] = dsink[:, 0:Q_PER_KV, 0].reshape(1, H)
    dkvu = jnp.concatenate([dkv, to_batch_major(du_t).astype(BF16)], axis=1)
    pieces = [dq, dkvu, dgat, dgss]
    dw_in = [_wgrad_plain(hn, p, f"dw_in_{k}", tw) for k, p in enumerate(pieces)]
    dw_in[1] = _kv_deinterleave(dw_in[1], KV)
    grads["w_in"] = jnp.stack(dw_in)
    w_in_parts = [ws["w_in"][0], w_kvu, ws["w_in"][2], ws["w_in"][3]]
    whole = _once((D, D), lambda i: (0, 0))
    (dh1, dh1b, grads["mix_norm"]), swap_mix = _mm_norm_bwd(
        "dhn", "nt", [(p, _spec((tm, D), lambda i: (i, 0)), wp, whole) for p, wp in zip(pieces, w_in_parts)],
        h1, g_mix, dh2, tm, swap_riders("mix"))
    swapped.update(zip(groups["mix"], swap_mix))
    dh0, _, got = ffn_bwd(h0, g_ffn1, saved1, dh1, dh1b, "ffn1", dhidden_carries=("exchange", "mix"),
                          dn_carries=("exchange", "ffn1"))
    received.update(zip(groups["ffn1"], got))
    dh0 = dh0.reshape(B, L, D)
    grad_x = dh0[:, N_META:, :]

    grads["final_norm"] = dg_final
    slay = _Layout(D, 8)
    for n in _SMALL:
        slay.add(n, w[n].shape)
    slay.add("loss", (1, D))
    meta_at = slay.align(8)
    slay.add("meta", (B * N_META, D))
    small = slay.pack({**{n: grads[n] for n in _SMALL}, "loss": loss_row, "meta": dh0[:, :N_META, :]})
    tot_small, dmeta = _all_reduce_small(small, B, N_META, meta_at)
    loss = slay.unpack(tot_small, "loss")[0, 0]
    for n in _SMALL:
        grads[n] = slay.unpack(tot_small, n)
    cw = D // N_CHIPS
    grads["meta_tokens"] = lax.dynamic_slice_in_dim(dmeta, my_slot * cw, cw, axis=1)

    fulls = [_sum_chips(grads[n], swapped[n], received[n], idx, f"grad_sum_chips_{n}") for n in _BIG]
    for n, f in zip(_BIG, _join_halves(fulls)):
        grads[n] = f

    delta, new_m, new_v = {}, {}, {}
    for n in _BIG + ["meta_tokens"]:
        delta[n], new_m[n], new_v[n] = _adamw(w[n], grads[n], m[n], v[n], f"adamw_{n}")
        grads[n] = grads[n].reshape(w[n].shape)

    def flat2d(a):
        return a.reshape(-1, a.shape[-1])

    d_, m_, v_ = _adamw_small([flat2d(w[n]) for n in _SMALL], [flat2d(grads[n]) for n in _SMALL],
                              [flat2d(m[n]) for n in _SMALL], [flat2d(v[n]) for n in _SMALL], "adamw_small")
    for i, n in enumerate(_SMALL):
        shp = w[n].shape
        delta[n], new_m[n], new_v[n] = d_[i].reshape(shp), m_[i].reshape(shp), v_[i].reshape(shp)
        grads[n] = grads[n].reshape(shp)

    return (loss, grad_x, *[grads[n] for n in _WEIGHTS], *[delta[n] for n in _WEIGHTS],
            *[new_m[n] for n in _WEIGHTS], *[new_v[n] for n in _WEIGHTS])


def kernel(x, meta_tokens, ffn1_norm, ffn1_w1, ffn1_w3, ffn1_w2, mix_norm, w_in, attn_sinks, ssm_a_re, ssm_a_im, ssm_log_step, ssm_b_re, ssm_b_im, ssm_c_re, ssm_c_im, ssm_d, ssm_glu_a, ssm_glu_b, w_out, ffn2_norm, ffn2_w1, ffn2_w3, ffn2_w2, final_norm, loss_target, m_meta_tokens, m_ffn1_norm, m_ffn1_w1, m_ffn1_w3, m_ffn1_w2, m_mix_norm, m_w_in, m_attn_sinks, m_ssm_a_re, m_ssm_a_im, m_ssm_log_step, m_ssm_b_re, m_ssm_b_im, m_ssm_c_re, m_ssm_c_im, m_ssm_d, m_ssm_glu_a, m_ssm_glu_b, m_w_out, m_ffn2_norm, m_ffn2_w1, m_ffn2_w3, m_ffn2_w2, m_final_norm, v_meta_tokens, v_ffn1_norm, v_ffn1_w1, v_ffn1_w3, v_ffn1_w2, v_mix_norm, v_w_in, v_attn_sinks, v_ssm_a_re, v_ssm_a_im, v_ssm_log_step, v_ssm_b_re, v_ssm_b_im, v_ssm_c_re, v_ssm_c_im, v_ssm_d, v_ssm_glu_a, v_ssm_glu_b, v_w_out, v_ffn2_norm, v_ffn2_w1, v_ffn2_w3, v_ffn2_w2, v_final_norm):
    args = locals()
    w = {n: args[n] for n in _WEIGHTS}
    m = {n: args["m_" + n] for n in _WEIGHTS}
    v = {n: args["v_" + n] for n in _WEIGHTS}
    return _step(x, loss_target, w, m, v)
```

```python
import functools
import math

import jax
import jax.numpy as jnp
from jax import lax
from jax.experimental import pallas as pl
from jax.experimental.pallas import tpu as pltpu

F32 = jnp.float32
BF16 = jnp.bfloat16
MESH_IDS = pl.DeviceIdType.MESH

N_CHIPS = 4
N_META = 16
HEAD_DIM = 64
Q_PER_KV = 4
QW = Q_PER_KV * HEAD_DIM
BLOCK = 128
SSM_GROUP = 16
SSM_STATE = 64
SSM_LANES = 128
GROUPS_PER_COL = SSM_LANES // SSM_GROUP
STATE_LANES = GROUPS_PER_COL * SSM_STATE
NORM_EPS = 1e-6
NEG_INF = -1e30
ADAM_LR, ADAM_B1, ADAM_B2, ADAM_EPS, ADAM_WD, ADAM_STEP = 0.001, 0.9, 0.999, 1e-08, 0.01, 10
GELU_C = math.sqrt(2.0 / math.pi)
ROW_ALIGN = 16
VMEM_LIMIT = 56 * 1024 * 1024
ROW_TILE_CAP = 688

_NN = (((1,), (0,)), ((), ()))
_NT = (((1,), (1,)), ((), ()))
_TN = (((0,), (0,)), ((), ()))
_DIMS = {"nn": _NN, "nt": _NT, "tn": _TN}


def _params(sem, **kw):
    return pltpu.CompilerParams(dimension_semantics=sem, vmem_limit_bytes=VMEM_LIMIT, **kw)


def _pick_tile(n, cap, mult):
    best = None
    for t in range(mult, min(n, cap) + 1, mult):
        if n % t == 0:
            best = t
    if best is None:
        raise ValueError(f"no tile for {n} (cap {cap}, multiple of {mult})")
    return best


def _sigmoid(x):
    return 0.5 * jnp.tanh(0.5 * x) + 0.5


def _spec(block, index_map):
    return pl.BlockSpec(block, index_map)


def _sum_dots(ins, mode):
    tot = None
    for p in range(len(ins) // 2):
        a_ref, b_ref = ins[2 * p], ins[2 * p + 1]
        for sl in ([None] if len(b_ref.shape) == 2 else range(b_ref.shape[0])):
            if sl is None:
                a, b = a_ref[...], b_ref[...]
            elif len(a_ref.shape) == 3:
                a, b = a_ref[sl], b_ref[sl]
            else:
                width = a_ref.shape[1] // b_ref.shape[0]
                a, b = a_ref[:, sl * width:(sl + 1) * width], b_ref[sl]
            d = lax.dot_general(a.astype(BF16), b.astype(BF16), _DIMS[mode], preferred_element_type=F32)
            tot = d if tot is None else tot + d
    return tot


def _mm(name, grid, kaxis, mode, pairs, out_shape, out_spec, scale=1.0, res=None):
    npairs = len(pairs)
    has_res = res is not None
    gk = 1 if kaxis is None else grid[kaxis]
    acc_shape = tuple(d for d in out_spec.block_shape if d is not None)

    def body(*refs):
        res_ref = refs[2 * npairs] if has_res else None
        o_ref = refs[2 * npairs + has_res]
        tot = _sum_dots(refs[:2 * npairs], mode)

        def finish(acc):
            r = acc * scale if scale != 1.0 else acc
            if has_res:
                r = res_ref[...] + r
            o_ref[...] = r.astype(o_ref.dtype)

        if gk == 1:
            finish(tot)
        else:
            acc_ref = refs[-1]
            k = pl.program_id(kaxis)

            @pl.when(k == 0)
            def _():
                acc_ref[...] = tot

            @pl.when(k > 0)
            def _():
                acc_ref[...] += tot

            @pl.when(k == gk - 1)
            def _():
                finish(acc_ref[...])

    in_specs, args = [], []
    for a, a_spec, b, b_spec in pairs:
        in_specs += [a_spec, b_spec]
        args += [a, b]
    if has_res:
        in_specs.append(res[1])
        args.append(res[0])
    sem = tuple("arbitrary" if ax == kaxis else "parallel" for ax in range(len(grid)))
    return pl.pallas_call(
        body, name=name, grid=grid, in_specs=in_specs, out_specs=out_spec, out_shape=out_shape,
        scratch_shapes=[pltpu.VMEM(acc_shape, F32)] if gk > 1 else [],
        compiler_params=_params(sem),
    )(*args)


def _mm_plain(a, b, mode, out_dtype, name, tm, scale=1.0):
    M, K = a.shape
    N = b.shape[1] if mode == "nn" else b.shape[0]
    return _mm(name, (M // tm,), None, mode,
               [(a, _spec((tm, K), lambda i: (i, 0)), b, _spec(b.shape, lambda i: (0, 0)))],
               jax.ShapeDtypeStruct((M, N), out_dtype), _spec((tm, N), lambda i: (i, 0)), scale=scale)


def _wgrad_plain(a, b, name, tr):
    R, M = a.shape
    N = b.shape[1]
    return _mm(name, (R // tr,), 0, "tn",
               [(a, _spec((tr, M), lambda r: (r, 0)), b, _spec((tr, N), lambda r: (r, 0)))],
               jax.ShapeDtypeStruct((M, N), F32), _spec((M, N), lambda r: (0, 0)))


def _rmsnorm_fwd(h, g, tm, name):
    T, D = h.shape

    def body(h_ref, g_ref, o_ref):
        x = h_ref[...]
        r = lax.rsqrt(jnp.mean(x * x, axis=-1, keepdims=True) + NORM_EPS)
        o_ref[...] = ((x * r) * g_ref[...]).astype(BF16)

    return pl.pallas_call(
        body, name=name, grid=(T // tm,),
        in_specs=[pl.BlockSpec((tm, D), lambda i: (i, 0)), pl.BlockSpec((1, D), lambda i: (0, 0))],
        out_specs=pl.BlockSpec((tm, D), lambda i: (i, 0)),
        out_shape=jax.ShapeDtypeStruct((T, D), BF16),
        compiler_params=_params(("parallel",)),
    )(h, g)


def _fold8(x):
    return jnp.sum(x.reshape(x.shape[0] // 8, 8, x.shape[1]), axis=0)


def _mm_norm_bwd(name, mode, pairs, h, g, dres, tm, riders=None):
    T, D = h.shape
    nt = T // tm
    npairs = len(pairs)

    def body(*refs):
        h_ref, g_ref, dres_ref, dh_ref, dhb_ref, dg_ref, acc_ref = refs[2 * npairs:]
        i = pl.program_id(0)
        x = h_ref[...]
        r = lax.rsqrt(jnp.mean(x * x, axis=-1, keepdims=True) + NORM_EPS)
        xhat = x * r
        dy = _sum_dots(refs[:2 * npairs], mode)
        dxhat = dy * g_ref[...]
        dx = r * (dxhat - xhat * jnp.mean(dxhat * xhat, axis=-1, keepdims=True))
        dh = dres_ref[...] + dx
        dh_ref[...] = dh
        dhb_ref[...] = dh.astype(BF16)
        part = _fold8(dy * xhat)

        @pl.when(i == 0)
        def _():
            acc_ref[...] = part

        @pl.when(i > 0)
        def _():
            acc_ref[...] += part

        @pl.when(i == nt - 1)
        def _():
            dg_ref[...] = jnp.sum(acc_ref[...], axis=0, keepdims=True)

    row = pl.BlockSpec((tm, D), lambda i: (i, 0))
    vec = pl.BlockSpec((1, D), lambda i: (0, 0))
    in_specs, args = [], []
    for a, a_spec, b, b_spec in pairs:
        in_specs += [a_spec, b_spec]
        args += [a, b]
    return _call(body, name, (nt,), in_specs + [row, vec, row], [row, row, vec],
                 [jax.ShapeDtypeStruct((T, D), F32), jax.ShapeDtypeStruct((T, D), BF16), jax.ShapeDtypeStruct((1, D), F32)],
                 [pltpu.VMEM((8, D), F32)], ("arbitrary",), (*args, h, g, dres), riders)


def _ffn_up(n, w1, w3, tm, name, riders=None):
    T, D = n.shape
    Fs = w1.shape[2]

    def body(n_ref, w1_ref, w3_ref, a_ref, c_ref, s_ref):
        x = n_ref[...]
        a = jnp.dot(x, w1_ref[...], preferred_element_type=F32)
        c = jnp.dot(x, w3_ref[...], preferred_element_type=F32)
        a_ref[...] = a.astype(BF16)
        c_ref[...] = c.astype(BF16)
        s_ref[...] = (a * _sigmoid(a) * c).astype(BF16)

    w_spec = _spec((None, D, Fs), lambda s, i: (s, 0, 0))
    o_spec = _spec((None, tm, Fs), lambda s, i: (s, i, 0))
    o_shape = jax.ShapeDtypeStruct((N_CHIPS, T, Fs), BF16)
    return _call(body, name, (N_CHIPS, T // tm), [_spec((tm, D), lambda s, i: (i, 0)), w_spec, w_spec],
                 [o_spec, o_spec, o_spec], [o_shape, o_shape, o_shape], [], ("parallel", "parallel"), (n, w1, w3), riders)


def _ffn_down(s, w2, h, tm, name):
    _, T, Fs = s.shape
    D = w2.shape[2]
    row = _spec((tm, D), lambda i: (i, 0))
    return _mm(name, (T // tm,), None, "nn",
               [(s, _spec((N_CHIPS, tm, Fs), lambda i: (0, i, 0)), w2, _spec((N_CHIPS, Fs, D), lambda i: (0, 0, 0)))],
               jax.ShapeDtypeStruct((T, D), F32), row, scale=0.5, res=(h, row))


def _ffn_dhidden(dhb, w2, a, c, tm, name, riders=None):
    T, D = dhb.shape
    Fs = w2.shape[1]

    def body(dh_ref, w2_ref, a_ref, c_ref, da_ref, dc_ref):
        d = 0.5 * lax.dot_general(dh_ref[...], w2_ref[...], _NT, preferred_element_type=F32)
        av = a_ref[...].astype(F32)
        cv = c_ref[...].astype(F32)
        sg = _sigmoid(av)
        da_ref[...] = (d * cv * (sg * (1.0 + av * (1.0 - sg)))).astype(BF16)
        dc_ref[...] = (d * (av * sg)).astype(BF16)

    h_spec = _spec((None, tm, Fs), lambda s, i: (s, i, 0))
    o_shape = jax.ShapeDtypeStruct((N_CHIPS, T, Fs), BF16)
    return _call(body, name, (N_CHIPS, T // tm),
                 [_spec((tm, D), lambda s, i: (i, 0)), _spec((None, Fs, D), lambda s, i: (s, 0, 0)), h_spec, h_spec],
                 [h_spec, h_spec], [o_shape, o_shape], [], ("parallel", "parallel"), (dhb, w2, a, c), riders)


def _wgrad_hidden_rows(s, dhb, tr, name, scale):
    _, T, Fs = s.shape
    D = dhb.shape[1]
    return _mm(name, (N_CHIPS, T // tr), 1, "tn",
               [(s, _spec((None, tr, Fs), lambda k, r: (k, r, 0)), dhb, _spec((tr, D), lambda k, r: (r, 0)))],
               jax.ShapeDtypeStruct((N_CHIPS, Fs, D), F32), _spec((None, Fs, D), lambda k, r: (k, 0, 0)), scale=scale)


def _wgrad_hidden_cols(n, da, tr, name):
    T, D = n.shape
    Fs = da.shape[2]
    return _mm(name, (N_CHIPS, T // tr), 1, "tn",
               [(n, _spec((tr, D), lambda k, r: (r, 0)), da, _spec((None, tr, Fs), lambda k, r: (k, r, 0)))],
               jax.ShapeDtypeStruct((N_CHIPS, D, Fs), F32), _spec((None, D, Fs), lambda k, r: (k, 0, 0)))


def _once(block, index_map):
    return pl.BlockSpec(block, index_map, pipeline_mode=pl.Buffered(1))


def _ffn_dn(da, w1, dc, w3, h, g, dres, tm, name, riders=None):
    _, T, Fs = da.shape
    D = w1.shape[1]
    h_spec = _spec((N_CHIPS, tm, Fs), lambda i: (0, i, 0))
    w_spec = _once((N_CHIPS, D, Fs), lambda i: (0, 0, 0))
    return _mm_norm_bwd(name, "nt", [(da, h_spec, w1, w_spec), (dc, h_spec, w3, w_spec)], h, g, dres, tm, riders)


def _mm_side_by_side(a, w, mode, out_dtype, name, tm, first=0, count=N_CHIPS, scale=1.0):
    T, K = a.shape
    assert first % count == 0
    n = w.shape[2] if mode == "nn" else w.shape[1]

    def body(a_ref, w_ref, o_ref):
        av = a_ref[...].astype(BF16)
        for j in range(count):
            r = lax.dot_general(av, w_ref[j].astype(BF16), _DIMS[mode], preferred_element_type=F32)
            o_ref[:, j * n:(j + 1) * n] = (r * scale if scale != 1.0 else r).astype(o_ref.dtype)

    return pl.pallas_call(
        body, name=name, grid=(T // tm,),
        in_specs=[_spec((tm, K), lambda i: (i, 0)), _once((count,) + w.shape[1:], lambda i: (first // count, 0, 0))],
        out_specs=_spec((tm, count * n), lambda i: (i, 0)),
        out_shape=jax.ShapeDtypeStruct((T, count * n), out_dtype),
        compiler_params=_params(("parallel",)),
    )(a, w)


def _mm_colslots(a, w, out_dtype, name, tm, first=0, count=N_CHIPS, scale=1.0):
    return _mm_side_by_side(a, w, "nn", out_dtype, name, tm, first, count, scale)


def _wgrad_colslots(a, d, tr, name):
    T, K = a.shape
    Ns = d.shape[1] // N_CHIPS
    return _mm(name, (N_CHIPS, T // tr), 1, "tn",
               [(a, _spec((tr, K), lambda k, r: (r, 0)), d, _spec((tr, Ns), lambda k, r: (r, k)))],
               jax.ShapeDtypeStruct((N_CHIPS, K, Ns), F32), _spec((None, K, Ns), lambda k, r: (k, 0, 0)))


def _mm_rowslots(a, w, h, tm, name):
    T = a.shape[0]
    N = w.shape[2]
    row = _spec((tm, N), lambda i: (i, 0))
    return _mm(name, (T // tm,), None, "nn",
               [(a, _spec((tm, a.shape[1]), lambda i: (i, 0)), w, _once(w.shape, lambda i: (0, 0, 0)))],
               jax.ShapeDtypeStruct((T, N), F32), row, res=(h, row))


def _wgrad_rowslots(a, d, tr, name):
    T = a.shape[0]
    Ks = a.shape[1] // N_CHIPS
    N = d.shape[1]
    return _mm(name, (N_CHIPS, T // tr), 1, "tn",
               [(a, _spec((tr, Ks), lambda k, r: (r, k)), d, _spec((tr, N), lambda k, r: (r, 0)))],
               jax.ShapeDtypeStruct((N_CHIPS, Ks, N), F32), _spec((None, Ks, N), lambda k, r: (k, 0, 0)))


def _gelu_parts(x):
    inner = GELU_C * (x + 0.044715 * (x * x * x))
    t = jnp.tanh(inner)
    return t, GELU_C * (1.0 + 3.0 * 0.044715 * (x * x))


def _gelu_fwd(y, tm, name):
    T, W = y.shape

    def body(y_ref, o_ref):
        x = y_ref[...]
        t, _ = _gelu_parts(x)
        o_ref[...] = (0.5 * x * (1.0 + t)).astype(BF16)

    spec = pl.BlockSpec((tm, W), lambda i: (i, 0))
    return pl.pallas_call(body, name=name, grid=(T // tm,), in_specs=[spec], out_specs=spec,
                          out_shape=jax.ShapeDtypeStruct((T, W), BF16),
                          compiler_params=_params(("parallel",)))(y)


def _gelu_bwd(pairs, y, tm, name):
    T, W = y.shape
    npairs = len(pairs)

    def body(*refs):
        y_ref, o_ref = refs[2 * npairs], refs[2 * npairs + 1]
        x = y_ref[...]
        t, dinner = _gelu_parts(x)
        o_ref[...] = _sum_dots(refs[:2 * npairs], "nt") * (0.5 * (1.0 + t) + 0.5 * x * (1.0 - t * t) * dinner)

    spec = pl.BlockSpec((tm, W), lambda i: (i, 0))
    in_specs, args = [], []
    for d, w in pairs:
        in_specs += [_spec((tm, d.shape[1]), lambda i: (i, 0)), _once(w.shape, lambda i: (0, 0, 0))]
        args += [d, w]
    return pl.pallas_call(body, name=name, grid=(T // tm,), in_specs=in_specs + [spec], out_specs=spec,
                          out_shape=jax.ShapeDtypeStruct((T, W), F32),
                          compiler_params=_params(("parallel",)))(*args, y)


def _merge_cols(D):
    cb = 512 if D % 512 == 0 else D
    return cb, D // cb


def _merge_fwd(gates, attn, ga, gb, tm, name):
    T, D = attn.shape
    cb, nc = _merge_cols(D)

    def body(gat_ref, gss_ref, attn_ref, ga_ref, gb_ref, o_ref):
        ssm = ga_ref[...] * _sigmoid(gb_ref[...])
        o_ref[...] = (_sigmoid(gat_ref[...]) * attn_ref[...] + _sigmoid(gss_ref[...]) * ssm).astype(BF16)

    def col(block):
        return pl.BlockSpec((tm, cb), lambda i, j: (i, block * nc + j))

    return pl.pallas_call(
        body, name=name, grid=(T // tm, nc),
        in_specs=[col(0), col(1), col(0), col(0), col(0)],
        out_specs=col(0), out_shape=jax.ShapeDtypeStruct((T, D), BF16),
        compiler_params=_params(("parallel", "parallel")),
    )(gates, gates, attn, ga, gb)


def _merge_bwd(dhb, w_out, gates, attn, ga, gb, tm, name):
    T, D = attn.shape
    cb, nc = _merge_cols(D)
    Ks = w_out.shape[1]
    spb = cb // Ks

    def body(dh_ref, w_ref, gat_ref, gss_ref, attn_ref, ga_ref, gb_ref, dattn_ref, dgat_ref, dgss_ref, dga_ref, dgb_ref):
        dh = dh_ref[...]
        d = jnp.concatenate([lax.dot_general(dh, w_ref[s], _NT, preferred_element_type=F32) for s in range(spb)], axis=1)
        sa = _sigmoid(gat_ref[...])
        ss = _sigmoid(gss_ref[...])
        sb = _sigmoid(gb_ref[...])
        gav = ga_ref[...]
        dattn_ref[...] = d * sa
        dgat_ref[...] = (d * attn_ref[...] * (sa * (1.0 - sa))).astype(BF16)
        dgss_ref[...] = (d * (gav * sb) * (ss * (1.0 - ss))).astype(BF16)
        dssm = d * ss
        dga_ref[...] = (dssm * sb).astype(BF16)
        dgb_ref[...] = (dssm * gav * (sb * (1.0 - sb))).astype(BF16)

    def col(block):
        return pl.BlockSpec((tm, cb), lambda i, j: (i, block * nc + j))

    b16 = jax.ShapeDtypeStruct((T, D), BF16)
    return pl.pallas_call(
        body, name=name, grid=(T // tm, nc),
        in_specs=[pl.BlockSpec((tm, D), lambda i, j: (i, 0)), pl.BlockSpec((spb, Ks, D), lambda i, j: (j, 0, 0)),
                  col(0), col(1), col(0), col(0), col(0)],
        out_specs=[col(0)] * 5,
        out_shape=[jax.ShapeDtypeStruct((T, D), F32), b16, b16, b16, b16],
        compiler_params=_params(("parallel", "parallel")),
    )(dhb, w_out, gates, gates, attn, ga, gb)


def _loss_head(h, g, target, tm, name):
    T, D = h.shape
    B, S, _ = target.shape
    L = S + N_META
    nt = T // tm
    tpe = L // tm

    def body(h_ref, g_ref, t_hbm, dh_ref, dhb_ref, dg_ref, loss_ref, tbuf, acc_g, acc_l, sems):
        i = pl.program_id(0)
        j = i % tpe
        slot = i % 2

        def fetch(tile, sl, act):
            tb, tj = tile // tpe, tile % tpe

            @pl.when(tj == 0)
            def _():
                act(pltpu.make_async_copy(t_hbm.at[tb, pl.ds(0, tm - N_META), :],
                                          tbuf.at[sl, pl.ds(N_META, tm - N_META), :], sems.at[sl]))

            @pl.when(tj > 0)
            def _():
                act(pltpu.make_async_copy(t_hbm.at[tb, pl.ds(tj * tm - N_META, tm), :], tbuf.at[sl], sems.at[sl]))

        @pl.when(i == 0)
        def _():
            tbuf[:, 0:N_META, :] = jnp.zeros((2, N_META, D), F32)
            fetch(i, slot, lambda cp: cp.start())

        @pl.when(i + 1 < nt)
        def _():
            fetch(i + 1, 1 - slot, lambda cp: cp.start())

        fetch(i, slot, lambda cp: cp.wait())

        x = h_ref[...]
        gv = g_ref[...]
        r = lax.rsqrt(jnp.mean(x * x, axis=-1, keepdims=True) + NORM_EPS)
        xhat = x * r
        pos = j * tm + lax.broadcasted_iota(jnp.int32, (tm, 1), 0)
        err = jnp.where(pos >= N_META, xhat * gv - tbuf[slot], 0.0)
        dy = err * (1.0 / D)
        dxhat = dy * gv
        dh = r * (dxhat - xhat * jnp.mean(dxhat * xhat, axis=-1, keepdims=True))
        dh_ref[...] = dh
        dhb_ref[...] = dh.astype(BF16)
        pg = _fold8(dy * xhat)
        pe = _fold8(err * err)

        @pl.when(i == 0)
        def _():
            acc_g[...] = pg
            acc_l[...] = pe

        @pl.when(i > 0)
        def _():
            acc_g[...] += pg
            acc_l[...] += pe

        @pl.when(i == nt - 1)
        def _():
            dg_ref[...] = jnp.sum(acc_g[...], axis=0, keepdims=True)
            loss_ref[...] = jnp.full((1, D), (0.5 / D) * jnp.sum(acc_l[...]), F32)

    row = pl.BlockSpec((tm, D), lambda i: (i, 0))
    vec = pl.BlockSpec((1, D), lambda i: (0, 0))
    return pl.pallas_call(
        body, name=name, grid=(nt,),
        in_specs=[row, vec, pl.BlockSpec(memory_space=pl.ANY)], out_specs=[row, row, vec, vec],
        out_shape=[jax.ShapeDtypeStruct((T, D), F32), jax.ShapeDtypeStruct((T, D), BF16),
                   jax.ShapeDtypeStruct((1, D), F32), jax.ShapeDtypeStruct((1, D), F32)],
        scratch_shapes=[pltpu.VMEM((2, tm, D), F32), pltpu.VMEM((8, D), F32), pltpu.VMEM((8, D), F32),
                        pltpu.SemaphoreType.DMA((2,))],
        compiler_params=_params(("arbitrary",)),
    )(h, g, target)


def _heads_to_rows(blk):
    return jnp.concatenate([blk[:, g * HEAD_DIM:(g + 1) * HEAD_DIM] for g in range(Q_PER_KV)], axis=0)


def _rows_to_heads(x):
    rows = x.shape[0] // Q_PER_KV
    return jnp.concatenate([x[g * rows:(g + 1) * rows] for g in range(Q_PER_KV)], axis=1)


def _causal(R):
    kj = lax.broadcasted_iota(jnp.int32, (BLOCK, R), 0)
    qi = lax.broadcasted_iota(jnp.int32, (BLOCK, R), 1) & (BLOCK - 1)
    return kj <= qi


def _band_probs(s_band, s_m, sink):
    m = jnp.maximum(jnp.maximum(jnp.max(s_band, axis=0, keepdims=True), jnp.max(s_m, axis=0, keepdims=True)), sink)
    e_b, e_m, e_s = jnp.exp(s_band - m), jnp.exp(s_m - m), jnp.exp(sink - m)
    inv = 1.0 / (jnp.sum(e_b, axis=0, keepdims=True) + jnp.sum(e_m, axis=0, keepdims=True) + e_s)
    return e_b * inv, e_m * inv, e_s * inv


def _fold_band(tri, two):
    return jnp.where(tri, two[BLOCK:2 * BLOCK], two[0:BLOCK])


def _unfold_band(tri, band):
    return jnp.concatenate([jnp.where(tri, 0.0, band), jnp.where(tri, band, 0.0)], axis=0)


def _meta_probs(qm, k_m, sink_m):
    R = qm.shape[0]
    s = lax.dot_general(qm, k_m, _NT, preferred_element_type=F32)
    qi = lax.broadcasted_iota(jnp.int32, (R, N_META), 0) & (N_META - 1)
    kj = lax.broadcasted_iota(jnp.int32, (R, N_META), 1)
    s = jnp.where(kj <= qi, s, NEG_INF)
    m = jnp.maximum(jnp.max(s, axis=-1, keepdims=True), sink_m)
    e, e_s = jnp.exp(s - m), jnp.exp(sink_m - m)
    inv = 1.0 / (jnp.sum(e, axis=-1, keepdims=True) + e_s)
    return e * inv, e_s * inv


def _block_start(n):
    return pl.multiple_of(N_META + n * BLOCK, ROW_ALIGN)


def _kv(blk):
    return blk[:, 0:HEAD_DIM], blk[:, HEAD_DIM:2 * HEAD_DIM]


def _attn_fwd(q, kv, sink_row, sink_meta, B, name, riders=None):
    T, D = q.shape
    L = T // B
    KV = D // QW
    nb = (L - N_META) // BLOCK

    def body(q_ref, kv_ref, sk_ref, skm_ref, o_ref, kvs):
        kvs[...] = kv_ref[...].astype(BF16)
        k_m, v_m = _kv(kvs[0:N_META, :])
        p, _ = _meta_probs(_heads_to_rows(q_ref[0:N_META, :]), k_m, skm_ref[0])
        o_ref[0:N_META, :] = _rows_to_heads(jnp.dot(p.astype(BF16), v_m, preferred_element_type=F32))
        tri = _causal(Q_PER_KV * BLOCK)

        def block(cur, first, keys):
            k2, v2 = _kv(kvs[keys, :])
            qb = _heads_to_rows(q_ref[pl.ds(cur, BLOCK), :])
            st = lax.dot_general(k2, qb, _NT, preferred_element_type=F32)
            smt = lax.dot_general(k_m, qb, _NT, preferred_element_type=F32)
            s_band = jnp.where(tri, st, NEG_INF) if first else _fold_band(tri, st)
            p_b, p_m, _ = _band_probs(s_band, smt, sk_ref[0])
            p2 = (p_b if first else _unfold_band(tri, p_b)).astype(BF16)
            o = (lax.dot_general(p2, v2, _TN, preferred_element_type=F32)
                 + lax.dot_general(p_m.astype(BF16), v_m, _TN, preferred_element_type=F32))
            o_ref[pl.ds(cur, BLOCK), :] = _rows_to_heads(o)

        block(N_META, True, pl.ds(N_META, BLOCK))

        def step(n, carry):
            block(_block_start(n), False, pl.ds(_block_start(n - 1), 2 * BLOCK))
            return carry

        lax.fori_loop(1, nb, step, 0)

    q_spec = pl.BlockSpec((L, QW), lambda b, h: (b, h))
    return _call(body, name, (B, KV),
                 [q_spec, pl.BlockSpec((L, 2 * HEAD_DIM), lambda b, h: (b, h)),
                  pl.BlockSpec((1, 1, Q_PER_KV * BLOCK), lambda b, h: (h, 0, 0)),
                  pl.BlockSpec((1, Q_PER_KV * N_META, 1), lambda b, h: (h, 0, 0))],
                 [q_spec], [jax.ShapeDtypeStruct((T, D), F32)], [pltpu.VMEM((L, 2 * HEAD_DIM), BF16)],
                 ("parallel", "parallel"), (q, kv, sink_row, sink_meta), riders)


def _attn_bwd(q, kv, o, do, sink_row, sink_meta, B, name, riders=None):
    T, D = q.shape
    L = T // B
    KV = D // QW
    nb = (L - N_META) // BLOCK
    R = Q_PER_KV * BLOCK
    scale = HEAD_DIM ** -0.5

    def head_totals(col, rows_per_head):
        rid = lax.broadcasted_iota(jnp.int32, (8, 128), 0)
        out = jnp.zeros((8, 128), F32)
        for g in range(Q_PER_KV):
            out = out + jnp.where(rid == g, jnp.sum(col[g * rows_per_head:(g + 1) * rows_per_head, :]), 0.0)
        return out

    def body(q_ref, kv_ref, o_ref, do_ref, sk_ref, skm_ref, dq_ref, dkv_ref, dsk_ref, kvs, acc, acc_sink):
        b = pl.program_id(1)
        kvs[...] = kv_ref[...].astype(BF16)
        acc[...] = jnp.zeros_like(acc)
        k_m, v_m = _kv(kvs[0:N_META, :])

        qm = _heads_to_rows(q_ref[0:N_META, :])
        dom = _heads_to_rows(do_ref[0:N_META, :])
        delta = jnp.sum(dom * _heads_to_rows(o_ref[0:N_META, :]), axis=-1, keepdims=True)
        p, p_s = _meta_probs(qm, k_m, skm_ref[0])
        domb = dom.astype(BF16)
        ds = (p * (lax.dot_general(domb, v_m, _NT, preferred_element_type=F32) - delta)).astype(BF16)
        dq_ref[0:N_META, :] = _rows_to_heads(jnp.dot(ds, k_m, preferred_element_type=F32) * scale).astype(BF16)
        acc[0:N_META, :] += jnp.concatenate([lax.dot_general(ds, qm, _TN, preferred_element_type=F32),
                                             lax.dot_general(p.astype(BF16), domb, _TN, preferred_element_type=F32)], axis=1)
        sink_tot = head_totals(-p_s * delta, N_META)
        tri = _causal(R)
        acc_sink[...] = jnp.zeros_like(acc_sink)
        ones = jnp.ones((8, HEAD_DIM), BF16)

        def block(cur, first, keys):
            k2, v2 = _kv(kvs[keys, :])
            rows = pl.ds(cur, BLOCK)
            qb = _heads_to_rows(q_ref[rows, :])
            dob = _heads_to_rows(do_ref[rows, :])
            prod = dob * _heads_to_rows(o_ref[rows, :])
            hi = prod.astype(BF16)
            lo = (prod - hi.astype(F32)).astype(BF16)
            delta = (lax.dot_general(ones, hi, _NT, preferred_element_type=F32)
                     + lax.dot_general(ones, lo, _NT, preferred_element_type=F32))[0:1]
            dobb = dob.astype(BF16)
            st = lax.dot_general(k2, qb, _NT, preferred_element_type=F32)
            smt = lax.dot_general(k_m, qb, _NT, preferred_element_type=F32)
            s_band = jnp.where(tri, st, NEG_INF) if first else _fold_band(tri, st)
            p_b, p_m, p_s = _band_probs(s_band, smt, sk_ref[0])
            dpt = lax.dot_general(v2, dobb, _NT, preferred_element_type=F32)
            dpm = lax.dot_general(v_m, dobb, _NT, preferred_element_type=F32)
            ds_b = p_b * ((dpt if first else _fold_band(tri, dpt)) - delta)
            ds2 = (ds_b if first else _unfold_band(tri, ds_b)).astype(BF16)
            p2 = (p_b if first else _unfold_band(tri, p_b)).astype(BF16)
            dsm = (p_m * (dpm - delta)).astype(BF16)
            pm = p_m.astype(BF16)
            dq = (lax.dot_general(ds2, k2, _TN, preferred_element_type=F32)
                  + lax.dot_general(dsm, k_m, _TN, preferred_element_type=F32))
            dq_ref[rows, :] = _rows_to_heads(dq * scale).astype(BF16)
            acc[keys, :] += jnp.concatenate([jnp.dot(ds2, qb, preferred_element_type=F32),
                                             jnp.dot(p2, dobb, preferred_element_type=F32)], axis=1)
            acc[0:N_META, :] += jnp.concatenate([jnp.dot(dsm, qb, preferred_element_type=F32),
                                                 jnp.dot(pm, dobb, preferred_element_type=F32)], axis=1)
            acc_sink[0:1, :] += -p_s * delta

        block(N_META, True, pl.ds(N_META, BLOCK))

        def step(n, carry):
            block(_block_start(n), False, pl.ds(_block_start(n - 1), 2 * BLOCK))
            return carry

        lax.fori_loop(1, nb, step, 0)
        dkv_ref[...] = acc[...].astype(BF16)
        rid = lax.broadcasted_iota(jnp.int32, (8, 128), 0)
        tot = sink_tot
        for g in range(Q_PER_KV):
            tot = tot + jnp.where(rid == g, jnp.sum(acc_sink[:, g * BLOCK:(g + 1) * BLOCK]), 0.0)

        @pl.when(b == 0)
        def _():
            dsk_ref[0] = tot

        @pl.when(b > 0)
        def _():
            dsk_ref[0] += tot

    q_spec = pl.BlockSpec((L, QW), lambda h, b: (b, h))
    kv_spec = pl.BlockSpec((L, 2 * HEAD_DIM), lambda h, b: (b, h))
    return _call(body, name, (KV, B),
                 [q_spec, kv_spec, q_spec, q_spec,
                  pl.BlockSpec((1, 1, R), lambda h, b: (h, 0, 0)),
                  pl.BlockSpec((1, Q_PER_KV * N_META, 1), lambda h, b: (h, 0, 0))],
                 [q_spec, kv_spec, pl.BlockSpec((1, 8, 128), lambda h, b: (h, 0, 0))],
                 [jax.ShapeDtypeStruct((T, D), BF16), jax.ShapeDtypeStruct((T, KV * 2 * HEAD_DIM), BF16),
                  jax.ShapeDtypeStruct((KV, 8, 128), F32)],
                 [pltpu.VMEM((L, 2 * HEAD_DIM), BF16), pltpu.VMEM((L, 2 * HEAD_DIM), F32), pltpu.VMEM((8, R), F32)],
                 ("parallel", "arbitrary"), (q, kv, o, do, sink_row, sink_meta), riders)


def _cmul_add(acc_r, acc_i, lr, li, xr, xi):
    return acc_r + (lr * xr - li * xi), acc_i + (lr * xi + li * xr)


def _cols_per_step(ncol):
    for cps in (4, 2):
        if ncol % cps == 0:
            return cps
    return 1


def _ssm_fwd(u, bmat, cmat, dskip, tables, nbatch, rc, name):
    T, W = u.shape
    ncol = W // SSM_LANES
    nch = T // rc
    S = STATE_LANES
    cps = _cols_per_step(ncol)
    assert nbatch == 4

    def body(u_ref, b_ref, c_ref, d_ref, tab_ref, y_ref, xs_ref, st_ref, carry_ref):
        ch = pl.program_id(1)

        @pl.when(ch == 0)
        def _():
            carry_ref[...] = jnp.zeros_like(carry_ref)

        uv = u_ref[...]
        for k in range(cps):
            st_ref[:, 2 * S * k:2 * S * (k + 1)] = jnp.dot(uv[:, SSM_LANES * k:SSM_LANES * (k + 1)].astype(BF16), b_ref[k],
                                                           preferred_element_type=F32)
        low = lax.broadcasted_iota(jnp.int32, (8, S), 0) < nbatch

        def tile(k, r0, c_r, c_i):
            re, im = slice(2 * S * k, 2 * S * k + S), slice(2 * S * k + S, 2 * S * (k + 1))
            la_r, la_i = tab_ref[k, :, 0:S], tab_ref[k, :, S:2 * S]
            lb_r, lb_i = tab_ref[k, :, 2 * S:3 * S], tab_ref[k, :, 3 * S:4 * S]
            v_r = st_ref[pl.ds(r0, 8), re]
            v_i = st_ref[pl.ds(r0, 8), im]
            v_r, v_i = _cmul_add(v_r, v_i, la_r, la_i, pltpu.roll(v_r, nbatch, 0), pltpu.roll(v_i, nbatch, 0))
            rc_r, rc_i = pltpu.roll(c_r, nbatch, 0), pltpu.roll(c_i, nbatch, 0)
            cb_r, cb_i = jnp.where(low, rc_r, c_r), jnp.where(low, rc_i, c_i)
            v_r, v_i = _cmul_add(v_r, v_i, lb_r, lb_i, cb_r, cb_i)
            st_ref[pl.ds(r0, 8), re] = v_r
            st_ref[pl.ds(r0, 8), im] = v_i
            return v_r, v_i

        def step(i, carry):
            r0 = pl.multiple_of(i * 8, 8)
            out = []
            for k in range(cps):
                out += list(tile(k, r0, carry[2 * k], carry[2 * k + 1]))
            return tuple(out)

        halves = tuple(carry_ref[:, S * j:S * (j + 1)] for j in range(2 * cps))
        halves = lax.fori_loop(0, rc // 8, step, halves)
        for j in range(2 * cps):
            carry_ref[:, S * j:S * (j + 1)] = halves[j]
        xb = st_ref[...].astype(BF16)
        xs_ref[...] = xb
        for k in range(cps):
            cols = slice(SSM_LANES * k, SSM_LANES * (k + 1))
            y_ref[:, cols] = (jnp.dot(xb[:, 2 * S * k:2 * S * (k + 1)], c_ref[k], preferred_element_type=F32)
                              + d_ref[:, cols] * uv[:, cols])

    return pl.pallas_call(
        body, name=name, grid=(ncol // cps, nch),
        in_specs=[pl.BlockSpec((rc, cps * SSM_LANES), lambda g, c: (c, g)),
                  pl.BlockSpec((cps, SSM_LANES, 2 * S), lambda g, c: (g, 0, 0)),
                  pl.BlockSpec((cps, 2 * S, SSM_LANES), lambda g, c: (g, 0, 0)),
                  pl.BlockSpec((1, cps * SSM_LANES), lambda g, c: (0, g)),
                  pl.BlockSpec((cps, 8, 4 * S), lambda g, c: (g, 0, 0))],
        out_specs=[pl.BlockSpec((rc, cps * SSM_LANES), lambda g, c: (c, g)),
                   pl.BlockSpec((rc, cps * 2 * S), lambda g, c: (c, g))],
        out_shape=[jax.ShapeDtypeStruct((T, W), F32), jax.ShapeDtypeStruct((T, ncol * 2 * S), BF16)],
        scratch_shapes=[pltpu.VMEM((rc, cps * 2 * S), F32), pltpu.VMEM((8, cps * 2 * S), F32)],
        compiler_params=_params(("parallel", "arbitrary")),
    )(u, bmat, cmat, dskip, tables)


def _ssm_bwd(dy, u, xs, bmat, cmat, dskip, tables, nbatch, rc, name):
    T, W = u.shape
    ncol = W // SSM_LANES
    nch = T // rc
    S = STATE_LANES
    ntile = rc // 16
    cps = _cols_per_step(ncol)

    def body(dy_ref, u_ref, xs_ref, b_ref, c_ref, d_ref, tab_ref,
             du_ref, db_ref, dc_ref, dl_ref, dd_ref, st_ref, carry_ref, accl_ref, accd_ref):
        ch = pl.program_id(1)

        @pl.when(ch == 0)
        def _():
            carry_ref[...] = jnp.zeros_like(carry_ref)
            accl_ref[...] = jnp.zeros_like(accl_ref)
            accd_ref[...] = jnp.zeros_like(accd_ref)
            db_ref[...] = jnp.zeros_like(db_ref)
            dc_ref[...] = jnp.zeros_like(dc_ref)

        dyv = dy_ref[...]
        uv = u_ref[...]
        dyb = dyv.astype(BF16)
        for k in range(cps):
            st_ref[:, 2 * S * k:2 * S * (k + 1)] = lax.dot_general(dyb[:, SSM_LANES * k:SSM_LANES * (k + 1)], c_ref[k], _NT,
                                                                   preferred_element_type=F32)
        low = lax.broadcasted_iota(jnp.int32, (8, S), 0) < nbatch

        def tile(k, r0, x_r, x_i, c_r, c_i, al_r, al_i):
            re, im = slice(2 * S * k, 2 * S * k + S), slice(2 * S * k + S, 2 * S * (k + 1))
            la_r, la_i = tab_ref[k, :, 0:S], tab_ref[k, :, S:2 * S]
            lb_r, lb_i = tab_ref[k, :, 2 * S:3 * S], tab_ref[k, :, 3 * S:4 * S]
            v_r = st_ref[pl.ds(r0, 8), re]
            v_i = st_ref[pl.ds(r0, 8), im]
            v_r, v_i = _cmul_add(v_r, v_i, la_r, la_i, pltpu.roll(v_r, nbatch, 0), pltpu.roll(v_i, nbatch, 0))
            cb_r = jnp.where(low, c_r, pltpu.roll(c_r, nbatch, 0))
            cb_i = jnp.where(low, c_i, pltpu.roll(c_i, nbatch, 0))
            v_r, v_i = _cmul_add(v_r, v_i, lb_r, lb_i, cb_r, cb_i)
            st_ref[pl.ds(r0, 8), re] = v_r
            st_ref[pl.ds(r0, 8), im] = v_i
            n_r = jnp.where(low, pltpu.roll(v_r, nbatch, 0), cb_r)
            n_i = jnp.where(low, pltpu.roll(v_i, nbatch, 0), cb_i)
            al_r = al_r + (n_r * x_r + n_i * x_i)
            al_i = al_i + (n_i * x_r - n_r * x_i)
            return v_r, v_i, al_r, al_i

        def step(j, carry):
            r0 = pl.multiple_of((ntile - 1 - j) * 16, 16)
            out = []
            for k in range(cps):
                re, im = slice(2 * S * k, 2 * S * k + S), slice(2 * S * k + S, 2 * S * (k + 1))
                x_r = xs_ref[pl.ds(r0, 16), re].astype(F32)
                x_i = xs_ref[pl.ds(r0, 16), im].astype(F32)
                mid = tile(k, r0 + 8, x_r[8:16], x_i[8:16], *carry[4 * k:4 * k + 4])
                out += list(tile(k, r0, x_r[0:8], x_i[0:8], *mid))
            return tuple(out)

        init = []
        for k in range(cps):
            init += [carry_ref[:, 2 * S * k:2 * S * k + S], carry_ref[:, 2 * S * k + S:2 * S * (k + 1)],
                     accl_ref[:, 2 * S * k:2 * S * k + S], accl_ref[:, 2 * S * k + S:2 * S * (k + 1)]]
        fin = lax.fori_loop(0, ntile, step, tuple(init))
        for k in range(cps):
            carry_ref[:, 2 * S * k:2 * S * k + S] = fin[4 * k]
            carry_ref[:, 2 * S * k + S:2 * S * (k + 1)] = fin[4 * k + 1]
            accl_ref[:, 2 * S * k:2 * S * k + S] = fin[4 * k + 2]
            accl_ref[:, 2 * S * k + S:2 * S * (k + 1)] = fin[4 * k + 3]
        dsb = st_ref[...].astype(BF16)
        ub = uv.astype(BF16)
        for k in range(cps):
            cols, lanes = slice(SSM_LANES * k, SSM_LANES * (k + 1)), slice(2 * S * k, 2 * S * (k + 1))
            du_ref[:, cols] = (lax.dot_general(dsb[:, lanes], b_ref[k], _NT, preferred_element_type=F32)
                               + d_ref[:, cols] * dyv[:, cols])
            db_ref[k] += lax.dot_general(ub[:, cols], dsb[:, lanes], _TN, preferred_element_type=F32)
            dc_ref[k] += lax.dot_general(xs_ref[:, lanes], dyb[:, cols], _TN, preferred_element_type=F32)
        accd_ref[...] += _fold8(dyv * uv)

        @pl.when(ch == nch - 1)
        def _():
            for k in range(cps):
                dl_ref[k] = jnp.sum(accl_ref[:, 2 * S * k:2 * S * (k + 1)], axis=0, keepdims=True)
            dd_ref[...] = jnp.sum(accd_ref[...], axis=0, keepdims=True)

    rev = lambda g, c: (nch - 1 - c, g)
    return pl.pallas_call(
        body, name=name, grid=(ncol // cps, nch),
        in_specs=[pl.BlockSpec((rc, cps * SSM_LANES), rev), pl.BlockSpec((rc, cps * SSM_LANES), rev),
                  pl.BlockSpec((rc, cps * 2 * S), rev),
                  pl.BlockSpec((cps, SSM_LANES, 2 * S), lambda g, c: (g, 0, 0)),
                  pl.BlockSpec((cps, 2 * S, SSM_LANES), lambda g, c: (g, 0, 0)),
                  pl.BlockSpec((1, cps * SSM_LANES), lambda g, c: (0, g)),
                  pl.BlockSpec((cps, 8, 4 * S), lambda g, c: (g, 0, 0))],
        out_specs=[pl.BlockSpec((rc, cps * SSM_LANES), rev),
                   pl.BlockSpec((cps, SSM_LANES, 2 * S), lambda g, c: (g, 0, 0)),
                   pl.BlockSpec((cps, 2 * S, SSM_LANES), lambda g, c: (g, 0, 0)),
                   pl.BlockSpec((cps, 1, 2 * S), lambda g, c: (g, 0, 0)),
                   pl.BlockSpec((1, cps * SSM_LANES), lambda g, c: (0, g))],
        out_shape=[jax.ShapeDtypeStruct((T, W), F32),
                   jax.ShapeDtypeStruct((ncol, SSM_LANES, 2 * S), F32),
                   jax.ShapeDtypeStruct((ncol, 2 * S, SSM_LANES), F32),
                   jax.ShapeDtypeStruct((ncol, 1, 2 * S), F32),
                   jax.ShapeDtypeStruct((1, W), F32)],
        scratch_shapes=[pltpu.VMEM((rc, cps * 2 * S), F32), pltpu.VMEM((8, cps * 2 * S), F32),
                        pltpu.VMEM((8, cps * 2 * S), F32), pltpu.VMEM((8, cps * SSM_LANES), F32)],
        compiler_params=_params(("parallel", "arbitrary")),
    )(dy, u, xs, bmat, cmat, dskip, tables)


def _ssm_matrices(a_re, a_im, log_step, b_re, b_im, c_re, c_im):
    G, N = a_re.shape
    ncol = G // GROUPS_PER_COL
    step = jnp.exp(log_step)[:, None]
    mag = jnp.exp(a_re * step)
    ang = a_im * step
    lam_re, lam_im = mag * jnp.cos(ang), mag * jnp.sin(ang)
    den = a_re * a_re + a_im * a_im
    nr, ni = lam_re - 1.0, lam_im
    coef_re = (nr * a_re + ni * a_im) / den
    coef_im = (ni * a_re - nr * a_im) / den
    bb_re = coef_re[..., None] * b_re - coef_im[..., None] * b_im
    bb_im = coef_re[..., None] * b_im + coef_im[..., None] * b_re
    eye = jnp.eye(GROUPS_PER_COL, dtype=F32)
    bb = jnp.stack([bb_re, bb_im]).reshape(2, ncol, GROUPS_PER_COL, N, SSM_GROUP)
    bmat = jnp.einsum("pbgnc,gh->bgcphn", bb, eye).reshape(ncol, SSM_LANES, 2 * STATE_LANES)
    cc = jnp.stack([c_re, -c_im]).reshape(2, ncol, GROUPS_PER_COL, SSM_GROUP, N)
    cmat = jnp.einsum("pbgcn,gh->bpgnhc", cc, eye).reshape(ncol, 2 * STATE_LANES, SSM_LANES)
    lam = jnp.concatenate([lam_re.reshape(ncol, STATE_LANES), lam_im.reshape(ncol, STATE_LANES)], axis=-1)
    return lam, bmat, cmat


def _scan_tables(lam, nbatch, conj):
    S = STATE_LANES
    lr, li = lam[:, None, 0:S], lam[:, None, S:2 * S]
    if conj:
        li = -li
    l2r, l2i = lr * lr - li * li, 2.0 * lr * li
    first = (jnp.arange(8) < nbatch)[None, :, None]
    zero = jnp.zeros_like(lr)
    if conj:
        parts = [jnp.where(first, lr, zero), jnp.where(first, li, zero), jnp.where(first, l2r, lr), jnp.where(first, l2i, li)]
    else:
        parts = [jnp.where(first, zero, lr), jnp.where(first, zero, li), jnp.where(first, lr, l2r), jnp.where(first, li, l2i)]
    return jnp.concatenate([jnp.broadcast_to(p, (lam.shape[0], 8, S)) for p in parts], axis=-1)


def _adamw_update(w_ref, g_ref, m_ref, v_ref, d_ref, nm_ref, nv_ref):
    gv = g_ref[...]
    mn = ADAM_B1 * m_ref[...] + (1.0 - ADAM_B1) * gv
    vn = ADAM_B2 * v_ref[...] + (1.0 - ADAM_B2) * (gv * gv)
    m_hat = mn / (1.0 - ADAM_B1 ** ADAM_STEP)
    v_hat = vn / (1.0 - ADAM_B2 ** ADAM_STEP)
    d_ref[...] = -ADAM_LR * (m_hat / (jnp.sqrt(v_hat) + ADAM_EPS) + ADAM_WD * w_ref[...])
    nm_ref[...] = mn
    nv_ref[...] = vn


def _adamw_small(ws, gs, ms, vs, name):
    n = len(ws)

    def body(*refs):
        for i in range(n):
            _adamw_update(refs[i], refs[n + i], refs[2 * n + i], refs[3 * n + i],
                          refs[4 * n + i], refs[5 * n + i], refs[6 * n + i])

    vm = pl.BlockSpec(memory_space=pltpu.VMEM)
    shapes = [jax.ShapeDtypeStruct(a.shape, F32) for a in ws]
    outs = pl.pallas_call(body, name=name, in_specs=[vm] * (4 * n), out_specs=[vm] * (3 * n), out_shape=shapes * 3,
                          compiler_params=pltpu.CompilerParams(vmem_limit_bytes=VMEM_LIMIT))(*ws, *gs, *ms, *vs)
    return outs[:n], outs[n:2 * n], outs[2 * n:]


def _adamw(w, g, m, v, name):
    R, C = w.shape[-2], w.shape[-1]
    tr = R if R <= 512 else _pick_tile(R, 512, 8)
    body = functools.partial(_adamw_update)

    def spec_for(a):
        if len(a.shape) == 2:
            return pl.BlockSpec((tr, C), lambda i: (i, 0))
        return pl.BlockSpec((None, tr, C), lambda i: (0, i, 0))

    spec = spec_for(w)
    shp = jax.ShapeDtypeStruct(w.shape, F32)
    return pl.pallas_call(body, name=name, grid=(R // tr,), in_specs=[spec, spec_for(g), spec, spec], out_specs=[spec] * 3,
                          out_shape=[shp, shp, shp], compiler_params=_params(("parallel",)))(w, g, m, v)


_ANY = pl.BlockSpec(memory_space=pl.ANY)


def _place():
    x, y, c = lax.axis_index("x"), lax.axis_index("y"), lax.axis_index("c")
    chips = [(1 - x, y), (x, 1 - y), (1 - x, 1 - y)]
    return x, y, c, chips


def _remote(src, dst, send_sems, recv_sems, k, to):
    return pltpu.make_async_remote_copy(src_ref=src, dst_ref=dst, send_sem=send_sems.at[k], recv_sem=recv_sems.at[k],
                                        device_id=to, device_id_type=MESH_IDS)


class _Riders:
    def __init__(self, srcs, out_shapes, n_sems, copies):
        self.srcs, self.out_shapes, self.n_sems, self.copies = list(srcs), list(out_shapes), n_sems, copies


def _call(body, name, grid, in_specs, out_specs, out_shape, scratch_shapes, sem, args, riders=None):
    if riders is None:
        return pl.pallas_call(body, name=name, grid=grid, in_specs=in_specs, out_specs=out_specs, out_shape=out_shape,
                              scratch_shapes=scratch_shapes, compiler_params=_params(sem))(*args)
    n_in, n_out, n_scr = len(in_specs), len(out_specs), len(scratch_shapes)
    r_in, r_out = len(riders.srcs), len(riders.out_shapes)

    def carrying(*refs):
        a, b = n_in, n_in + r_in
        c, d = b + n_out, b + n_out + r_out
        e = d + n_scr
        sends, arrivals = riders.copies(refs[a:b], refs[c:d], refs[e], refs[e + 1])
        first, last = None, None
        for ax, size in enumerate(grid):
            at0, at1 = pl.program_id(ax) == 0, pl.program_id(ax) == size - 1
            first = at0 if first is None else first & at0
            last = at1 if last is None else last & at1

        @pl.when(first)
        def _():
            for cp in sends:
                cp.start()

        body(*refs[:a], *refs[b:c], *refs[d:e])

        @pl.when(last)
        def _():
            for cp in arrivals:
                cp.wait_recv()
            for cp in sends:
                cp.wait_send()

    outs = pl.pallas_call(
        carrying, name=name, grid=grid, in_specs=list(in_specs) + [_ANY] * r_in,
        out_specs=list(out_specs) + [_ANY] * r_out, out_shape=list(out_shape) + riders.out_shapes,
        scratch_shapes=list(scratch_shapes) + [pltpu.SemaphoreType.DMA((riders.n_sems,)),
                                               pltpu.SemaphoreType.DMA((riders.n_sems,))],
        compiler_params=pltpu.CompilerParams(dimension_semantics=("arbitrary",) * len(grid),
                                             vmem_limit_bytes=VMEM_LIMIT, has_side_effects=True),
    )(*args, *riders.srcs)
    return outs[:n_out], outs[n_out:]


def _gather_riders(shards):
    def copies(srcs, outs, send_sems, recv_sems):
        x, y, c, chips = _place()
        sends, arrivals = [], []
        for i, s in enumerate(shards):
            half = s.shape[0] // 2
            rows = pl.ds(c * half, half)
            for j, chip in enumerate(chips):
                sends.append(_remote(srcs[i].at[rows, :], outs[i].at[2 * x + y, rows, :], send_sems, recv_sems,
                                     3 * i + j, (*chip, c)))
                landed = outs[i].at[2 * chip[0] + chip[1], rows, :]
                arrivals.append(_remote(landed, landed, send_sems, recv_sems, 3 * i + j, (*chip, c)))
        return sends, arrivals

    return _Riders(shards, [jax.ShapeDtypeStruct((N_CHIPS,) + s.shape, s.dtype) for s in shards], 3 * len(shards), copies)


def _exchange_riders(parts):
    def copies(srcs, outs, send_sems, recv_sems):
        x, y, c, chips = _place()
        sends = [_remote(srcs[i].at[2 * chip[0] + chip[1]], outs[i].at[j], send_sems, recv_sems, 3 * i + j, (*chip, c))
                 for i in range(len(parts)) for j, chip in enumerate(chips)]
        return sends, sends

    return _Riders(parts, [jax.ShapeDtypeStruct((3,) + p.shape[1:], p.dtype) for p in parts], 3 * len(parts), copies)


def _swap_riders(grads):
    def copies(srcs, outs, send_sems, recv_sems):
        x, y, c, _ = _place()
        sends = []
        for i, g in enumerate(grads):
            half = g.shape[1] // 2
            sends.append(_remote(srcs[i].at[:, pl.ds((1 - c) * half, half), :], outs[i], send_sems, recv_sems, i,
                                 (x, y, 1 - c)))
        return sends, sends

    return _Riders(grads, [jax.ShapeDtypeStruct((N_CHIPS, g.shape[1] // 2, g.shape[2]), g.dtype) for g in grads],
                   len(grads), copies)


def _forward_halves(gathered, shards, tag):
    n = len(gathered)

    def body(*refs):
        srcs, outs = refs[:n], refs[n:2 * n]
        send_sems, recv_sems = refs[2 * n:]
        x, y, c, chips = _place()
        sibling = (x, y, 1 - c)
        cps = []
        for i in range(n):
            half = gathered[i].shape[1] // 2
            for j, chip in enumerate(chips):
                slot = 2 * chip[0] + chip[1]
                cps.append(_remote(srcs[i].at[slot, pl.ds(c * half, half), :], outs[i].at[slot, pl.ds(c * half, half), :],
                                   send_sems, recv_sems, 3 * i + j, sibling))
        for cp in cps:
            cp.start()
        for i in range(n):
            half = gathered[i].shape[1] // 2
            for j, chip in enumerate(chips):
                theirs = outs[i].at[2 * chip[0] + chip[1], pl.ds((1 - c) * half, half), :]
                _remote(theirs, theirs, send_sems, recv_sems, 3 * i + j, sibling).wait_recv()
        for cp in cps:
            cp.wait_send()

    outs = pl.pallas_call(
        body, name=f"gather_forward_{tag}", in_specs=[_ANY] * n, out_specs=[_ANY] * n,
        out_shape=[jax.ShapeDtypeStruct(g.shape, g.dtype) for g in gathered],
        input_output_aliases={i: i for i in range(n)},
        scratch_shapes=[pltpu.SemaphoreType.DMA((3 * n,)), pltpu.SemaphoreType.DMA((3 * n,))],
        compiler_params=pltpu.CompilerParams(has_side_effects=True),
    )(*gathered)
    slot = 2 * lax.axis_index("x") + lax.axis_index("y")
    return [lax.dynamic_update_slice(o, s[None], (slot, 0, 0)) for o, s in zip(outs, shards)]


def _gather_weights(shards):
    n = len(shards)

    def body(*refs):
        srcs, outs = refs[:n], refs[n:2 * n]
        send_sems, recv_sems = refs[2 * n:]
        x, y, c, chips = _place()
        sibling = (x, y, 1 - c)

        def piece(i, px, py, pc):
            half = shards[i].shape[0] // 2
            return outs[i].at[2 * px + py, pl.ds(pc * half, half), :]

        first = []
        for i in range(n):
            half = shards[i].shape[0] // 2
            for j, chip in enumerate(chips):
                first.append(_remote(srcs[i].at[pl.ds(c * half, half), :], piece(i, x, y, c), send_sems, recv_sems,
                                     6 * i + j, (*chip, c)))
        for cp in first:
            cp.start()
        passed = []
        for i in range(n):
            for j, chip in enumerate(chips):
                _remote(piece(i, *chip, c), piece(i, *chip, c), send_sems, recv_sems, 6 * i + j, (*chip, c)).wait_recv()
                cp = _remote(piece(i, *chip, c), piece(i, *chip, c), send_sems, recv_sems, 6 * i + 3 + j, sibling)
                cp.start()
                passed.append(cp)
        for i in range(n):
            for j, chip in enumerate(chips):
                _remote(piece(i, *chip, 1 - c), piece(i, *chip, 1 - c), send_sems, recv_sems, 6 * i + 3 + j,
                        sibling).wait_recv()
        for cp in first + passed:
            cp.wait_send()

    outs = pl.pallas_call(
        body, name="gather_weights", in_specs=[_ANY] * n, out_specs=[_ANY] * n,
        out_shape=[jax.ShapeDtypeStruct((N_CHIPS,) + s.shape, s.dtype) for s in shards],
        scratch_shapes=[pltpu.SemaphoreType.DMA((6 * n,)), pltpu.SemaphoreType.DMA((6 * n,))],
        compiler_params=pltpu.CompilerParams(has_side_effects=True),
    )(*shards)
    slot = 2 * lax.axis_index("x") + lax.axis_index("y")
    return [lax.dynamic_update_slice(o, s[None], (slot, 0, 0)) for o, s in zip(outs, shards)]


def _swap_halves(grads, tag):
    n = len(grads)

    def body(*refs):
        srcs, outs = refs[:n], refs[n:2 * n]
        send_sems, recv_sems = refs[2 * n:]
        x, y, c, _ = _place()
        cps = []
        for i in range(n):
            half = grads[i].shape[1] // 2
            cps.append(_remote(srcs[i].at[:, pl.ds((1 - c) * half, half), :], outs[i], send_sems, recv_sems, i, (x, y, 1 - c)))
        for cp in cps:
            cp.start()
        for cp in cps:
            cp.wait()

    return pl.pallas_call(
        body, name=f"grad_swap_halves_{tag}", in_specs=[_ANY] * n, out_specs=[_ANY] * n,
        out_shape=[jax.ShapeDtypeStruct((N_CHIPS, g.shape[1] // 2, g.shape[2]), g.dtype) for g in grads],
        scratch_shapes=[pltpu.SemaphoreType.DMA((n,)), pltpu.SemaphoreType.DMA((n,))],
        compiler_params=pltpu.CompilerParams(has_side_effects=True),
    )(*grads)


def _join_halves(fulls):
    n = len(fulls)

    def body(*refs):
        srcs, outs = refs[:n], refs[n:2 * n]
        send_sems, recv_sems = refs[2 * n:]
        x, y, c, _ = _place()
        sibling = (x, y, 1 - c)
        cps = []
        for i in range(n):
            h = fulls[i].shape[0] // 2
            cps.append(_remote(srcs[i].at[pl.ds(c * h, h), :], outs[i].at[pl.ds(c * h, h), :], send_sems, recv_sems, i,
                               sibling))
        for cp in cps:
            cp.start()
        for i in range(n):
            h = fulls[i].shape[0] // 2
            theirs = outs[i].at[pl.ds((1 - c) * h, h), :]
            _remote(theirs, theirs, send_sems, recv_sems, i, sibling).wait_recv()
        for cp in cps:
            cp.wait_send()

    return pl.pallas_call(
        body, name="grad_join_halves", in_specs=[_ANY] * n, out_specs=[_ANY] * n,
        out_shape=[jax.ShapeDtypeStruct(f.shape, f.dtype) for f in fulls],
        input_output_aliases={i: i for i in range(n)},
        scratch_shapes=[pltpu.SemaphoreType.DMA((n,)), pltpu.SemaphoreType.DMA((n,))],
        compiler_params=pltpu.CompilerParams(has_side_effects=True),
    )(*fulls)


def _half_tile(h):
    return h if h <= 512 else _pick_tile(h, 512, ROW_ALIGN)


def _sum_halves(g, r1, c_idx, name):
    _, R, C = g.shape
    H = R // 2
    tr = _half_tile(H)
    nblk = H // tr

    def body(c_ref, g_ref, r_ref, p_ref):
        p_ref[...] = (g_ref[...] + r_ref[...]).astype(BF16)

    half = pl.BlockSpec((None, tr, C), lambda s, i, c_ref: (s, c_ref[0] * nblk + i, 0))
    plain = pl.BlockSpec((None, tr, C), lambda s, i, c_ref: (s, i, 0))
    return pl.pallas_call(
        body, name=name,
        grid_spec=pltpu.PrefetchScalarGridSpec(num_scalar_prefetch=1, grid=(N_CHIPS, nblk), in_specs=[half, plain],
                                               out_specs=plain),
        out_shape=jax.ShapeDtypeStruct((N_CHIPS, H, C), BF16),
        compiler_params=_params(("parallel", "parallel")),
    )(c_idx, g, r1)


def _sum_chips(g, r1, r2, idx, name):
    _, R, C = g.shape
    H = R // 2
    tr = _half_tile(H)
    nblk = H // tr

    def body(idx_ref, g_ref, r1_ref, r2_ref, o_ref):
        o_ref[...] = (((g_ref[...] + r1_ref[...]) + r2_ref[0].astype(F32)) + r2_ref[1].astype(F32)) + r2_ref[2].astype(F32)

    return pl.pallas_call(
        body, name=name,
        grid_spec=pltpu.PrefetchScalarGridSpec(
            num_scalar_prefetch=1, grid=(nblk,),
            in_specs=[pl.BlockSpec((None, tr, C), lambda i, idx_ref: (idx_ref[0], idx_ref[1] * nblk + i, 0)),
                      pl.BlockSpec((None, tr, C), lambda i, idx_ref: (idx_ref[0], i, 0)),
                      pl.BlockSpec((3, tr, C), lambda i, idx_ref: (0, i, 0))],
            out_specs=pl.BlockSpec((tr, C), lambda i, idx_ref: (idx_ref[1] * nblk + i, 0))),
        out_shape=jax.ShapeDtypeStruct((R, C), F32),
        compiler_params=_params(("parallel",)),
    )(idx, g, r1, r2)


def _all_reduce_small(v, n_fold, fold_rows, fold_at):
    M, N = v.shape

    def body(x_ref, tot_ref, fold_ref, all_ref, send_sems, recv_sems, local_sem):
        x, y, c, chips = _place()
        me, sibling = (x, y, c), (x, y, 1 - c)

        def rows(px, py, pc):
            return all_ref.at[pl.ds((4 * px + 2 * py + pc) * M, M), :]

        def copy(k, block, to, src=None):
            return _remote(rows(*block) if src is None else src, rows(*block), send_sems, recv_sems, k, to)

        mine = pltpu.make_async_copy(x_ref, rows(*me), local_sem)
        mine.start()
        first = [copy(0, me, sibling, src=x_ref)]
        first += [copy(1 + j, me, (*chip, c), src=x_ref) for j, chip in enumerate(chips)]
        for cp in first:
            cp.start()
        passed = [copy(4 + j, (*chip, c), sibling) for j, chip in enumerate(chips)]
        for j, chip in enumerate(chips):
            copy(1 + j, (*chip, c), me).wait_recv()
            passed[j].start()
        copy(0, sibling, me).wait_recv()
        for j, chip in enumerate(chips):
            copy(4 + j, (*chip, 1 - c), me).wait_recv()
        for cp in first + passed:
            cp.wait_send()
        mine.wait()
        tot = all_ref[0:M, :]
        for d in range(1, 8):
            tot = tot + all_ref[d * M:(d + 1) * M, :]
        tot_ref[...] = tot
        f = tot[fold_at:fold_at + fold_rows, :]
        for e in range(1, n_fold):
            f = f + tot[fold_at + e * fold_rows:fold_at + (e + 1) * fold_rows, :]
        fold_ref[...] = f

    vm = pl.BlockSpec(memory_space=pltpu.VMEM)
    return pl.pallas_call(
        body, name="all_reduce_small", in_specs=[vm], out_specs=[vm, vm],
        out_shape=[jax.ShapeDtypeStruct((M, N), F32), jax.ShapeDtypeStruct((fold_rows, N), F32)],
        scratch_shapes=[pltpu.VMEM((8 * M, N), F32), pltpu.SemaphoreType.DMA((7,)), pltpu.SemaphoreType.DMA((7,)),
                        pltpu.SemaphoreType.DMA],
        compiler_params=pltpu.CompilerParams(has_side_effects=True, vmem_limit_bytes=VMEM_LIMIT),
    )(v)


def _as_rows(a, width):
    flat = a.reshape(-1)
    pad = (-flat.shape[0]) % width
    if pad:
        flat = jnp.concatenate([flat, jnp.zeros((pad,), flat.dtype)])
    return flat.reshape(-1, width)


class _Layout:
    def __init__(self, width, total_mult):
        self.width, self.total_mult = width, total_mult
        self.offsets, self.shapes, self.rows = {}, {}, 0

    def add(self, name, shape):
        r = -(-math.prod(shape) // self.width)
        self.offsets[name], self.shapes[name] = (self.rows, r), tuple(shape)
        self.rows += r

    def align(self, mult):
        gap = (-self.rows) % mult
        if gap:
            self.offsets[f"_gap{self.rows}"], self.shapes[f"_gap{self.rows}"] = (self.rows, gap), (gap, self.width)
            self.rows += gap
        return self.rows

    def pack(self, pieces):
        self.align(self.total_mult)
        parts = [_as_rows(pieces[n].astype(F32), self.width) if n in pieces else jnp.zeros(self.shapes[n], F32)
                 for n in self.offsets]
        return jnp.concatenate(parts, axis=0)

    def unpack(self, buf, name):
        off, r = self.offsets[name]
        shape = self.shapes[name]
        return buf[off:off + r].reshape(-1)[:math.prod(shape)].reshape(shape)


_BIG = ["ffn1_w1", "ffn1_w3", "ffn1_w2", "w_in", "ssm_glu_a", "ssm_glu_b", "w_out", "ffn2_w1", "ffn2_w3", "ffn2_w2"]
_SMALL = ["ffn1_norm", "mix_norm", "ffn2_norm", "final_norm", "attn_sinks", "ssm_a_re", "ssm_a_im", "ssm_log_step",
          "ssm_b_re", "ssm_b_im", "ssm_c_re", "ssm_c_im", "ssm_d"]
_WEIGHTS = ["meta_tokens", "ffn1_norm", "ffn1_w1", "ffn1_w3", "ffn1_w2", "mix_norm", "w_in", "attn_sinks", "ssm_a_re",
            "ssm_a_im", "ssm_log_step", "ssm_b_re", "ssm_b_im", "ssm_c_re", "ssm_c_im", "ssm_d", "ssm_glu_a",
            "ssm_glu_b", "w_out", "ffn2_norm", "ffn2_w1", "ffn2_w3", "ffn2_w2", "final_norm"]


def _kv_interleave(w, kv_heads):
    kvw = kv_heads * HEAD_DIM
    lead = w.shape[:-1]
    k = w[..., 0:kvw].reshape(lead + (kv_heads, 1, HEAD_DIM))
    v = w[..., kvw:2 * kvw].reshape(lead + (kv_heads, 1, HEAD_DIM))
    return jnp.concatenate([jnp.concatenate([k, v], axis=-2).reshape(lead + (2 * kvw,)), w[..., 2 * kvw:]], axis=-1)


def _kv_deinterleave(w, kv_heads):
    kvw = kv_heads * HEAD_DIM
    lead = w.shape[:-1]
    kv = w[..., 0:2 * kvw].reshape(lead + (kv_heads, 2, HEAD_DIM))
    return jnp.concatenate([kv[..., 0, :].reshape(lead + (kvw,)), kv[..., 1, :].reshape(lead + (kvw,)), w[..., 2 * kvw:]],
                           axis=-1)


def _step(x, target, w, m, v):
    B, S, D = x.shape
    L = S + N_META
    T = B * L
    H = D // HEAD_DIM
    KV = H // Q_PER_KV
    SW = D // 2
    tm = _pick_tile(L, ROW_TILE_CAP, ROW_ALIGN)
    rc = _pick_tile(L, ROW_TILE_CAP // B, 4) * B
    tw = _pick_tile(T, 3 * ROW_TILE_CAP, ROW_ALIGN)
    my_c = lax.axis_index("c")
    my_slot = 2 * lax.axis_index("x") + lax.axis_index("y")

    groups = {"ffn1": ["ffn1_w1", "ffn1_w3", "ffn1_w2"], "mix": ["w_in", "ssm_glu_a", "ssm_glu_b", "w_out"],
              "ffn2": ["ffn2_w1", "ffn2_w3", "ffn2_w2"]}
    waves = {"first": ["ffn1_w1", "ffn1_w3"], "early": ["ffn1_w2"] + groups["mix"], "late": groups["ffn2"]}
    shards = {n: w[n][0].astype(BF16) for n in _BIG}
    gathered = _gather_weights([shards[n] for n in waves["first"]] + [w["meta_tokens"]])
    ws = dict(zip(waves["first"], gathered[:-1]))
    meta = jnp.transpose(gathered[-1], (1, 0, 2)).reshape(N_META, D)

    def arrive(wave, landed):
        mine = [shards[n] for n in waves[wave]]
        ws.update(zip(waves[wave], _forward_halves(landed, mine, wave)))

    g_ffn1, g_mix, g_ffn2 = w["ffn1_norm"], w["mix_norm"], w["ffn2_norm"]
    g_final = w["final_norm"].reshape(1, D)

    h0 = jnp.concatenate([jnp.broadcast_to(meta[None], (B, N_META, D)), x], axis=1).reshape(T, D)

    def ffn_fwd(h, g, tag, carry=None):
        n = _rmsnorm_fwd(h, g, tm, f"{tag}_norm")
        riders = None if carry is None else _gather_riders([shards[k] for k in waves[carry]])
        out = _ffn_up(n, ws[f"{tag}_w1"], ws[f"{tag}_w3"], tm, f"{tag}_up", riders)
        if carry is not None:
            out, landed = out
            arrive(carry, landed)
        a, c, s = out
        return _ffn_down(s, ws[f"{tag}_w2"], h, tm, f"{tag}_down"), (n, a, c, s)

    h1, saved1 = ffn_fwd(h0, g_ffn1, "ffn1", carry="early")
    w_kvu = _kv_interleave(ws["w_in"][1], KV)
    hn = _rmsnorm_fwd(h1, g_mix, tm, "mix_norm")
    q = _mm_colslots(hn, ws["w_in"], BF16, "w_in_q", tm, first=0, count=1, scale=HEAD_DIM ** -0.5)
    kvu = _mm_plain(hn, w_kvu, "nn", F32, "w_in_kvu", tm)
    gates = _mm_colslots(hn, ws["w_in"], F32, "w_in_gates", tm, first=2, count=2)

    sinks = w["attn_sinks"].reshape(KV, Q_PER_KV, 1, 1)
    sink_row = jnp.broadcast_to(sinks.reshape(KV, 1, Q_PER_KV, 1), (KV, 1, Q_PER_KV, BLOCK)).reshape(KV, 1, Q_PER_KV * BLOCK)
    sink_meta = jnp.broadcast_to(sinks, (KV, Q_PER_KV, N_META, 1)).reshape(KV, Q_PER_KV * N_META, 1)
    (attn,), landed = _attn_fwd(q, kvu, sink_row, sink_meta, B, "attn_fwd",
                                _gather_riders([shards[k] for k in waves["late"]]))
    arrive("late", landed)

    def to_time_major(a2d):
        return jnp.transpose(a2d.reshape(B, L, a2d.shape[-1]), (1, 0, 2)).reshape(T, a2d.shape[-1])

    def to_batch_major(a2d):
        return jnp.transpose(a2d.reshape(L, B, a2d.shape[-1]), (1, 0, 2)).reshape(T, a2d.shape[-1])

    ssm_args = (w["ssm_a_re"][0], w["ssm_a_im"][0], w["ssm_log_step"][0], w["ssm_b_re"][0], w["ssm_b_im"][0],
                w["ssm_c_re"][0], w["ssm_c_im"][0])
    (lam, bmat, cmat), ssm_vjp = jax.vjp(_ssm_matrices, *ssm_args)
    bmat16, cmat16 = bmat.astype(BF16), cmat.astype(BF16)
    u_t = to_time_major(kvu[:, SW:])
    y_t, xs = _ssm_fwd(u_t, bmat16, cmat16, w["ssm_d"], _scan_tables(lam, B, False), B, rc, "ssm_fwd")
    y0 = to_batch_major(y_t)
    yg = _gelu_fwd(y0, tm, "gelu_fwd")
    ga = _mm_colslots(yg, ws["ssm_glu_a"], F32, "glu_a", tm)
    gb = _mm_colslots(yg, ws["ssm_glu_b"], F32, "glu_b", tm)
    merged = _merge_fwd(gates, attn, ga, gb, tm, "merge_fwd")
    h2 = _mm_rowslots(merged, ws["w_out"], h1, tm, "w_out")
    h3, saved2 = ffn_fwd(h2, g_ffn2, "ffn2")
    dh3, dh3b, dg_final, loss_row = _loss_head(h3, g_final, target, tm, "loss_head")

    grads, swapped, received = {}, {}, {}
    c_idx = my_c.reshape(1).astype(jnp.int32)
    idx = jnp.stack([my_slot, my_c]).astype(jnp.int32)

    def swap_riders(group):
        return _swap_riders([grads[n] for n in groups[group]])

    def exchange_riders(group):
        names = groups[group]
        if names[0] not in swapped:
            swapped.update(zip(names, _swap_halves([grads[n] for n in names], group)))
        return _exchange_riders([_sum_halves(grads[n], swapped[n], c_idx, f"grad_sum_halves_{n}") for n in names])

    def ffn_bwd(h, g, saved, dh, dhb, tag, dhidden_carries=None, dn_carries=None):
        n, a, c, s = saved
        w1, w3, w2 = ws[f"{tag}_w1"], ws[f"{tag}_w3"], ws[f"{tag}_w2"]
        grads[f"{tag}_w2"] = _wgrad_hidden_rows(s, dhb, tw, f"{tag}_dw2", 0.5)
        if dhidden_carries is None:
            da, dc = _ffn_dhidden(dhb, w2, a, c, tm, f"{tag}_dhidden")
        else:
            (da, dc), got = _ffn_dhidden(dhb, w2, a, c, tm, f"{tag}_dhidden", exchange_riders(dhidden_carries[1]))
            received.update(zip(groups[dhidden_carries[1]], got))
        grads[f"{tag}_w1"] = _wgrad_hidden_cols(n, da, tw, f"{tag}_dw1")
        grads[f"{tag}_w3"] = _wgrad_hidden_cols(n, dc, tw, f"{tag}_dw3")
        kind, group = dn_carries
        riders = swap_riders(group) if kind == "swap" else exchange_riders(group)
        (dh_in, dhb_in, grads[f"{tag}_norm"]), got = _ffn_dn(da, w1, dc, w3, h, g, dh, tm, f"{tag}_dn", riders)
        return dh_in, dhb_in, got

    dh2, dh2b, got = ffn_bwd(h2, g_ffn2, saved2, dh3, dh3b, "ffn2", dn_carries=("swap", "ffn2"))
    swapped.update(zip(groups["ffn2"], got))

    grads["w_out"] = _wgrad_rowslots(merged, dh2b, tw, "dw_out")
    dattn, dgat, dgss, dga, dgb = _merge_bwd(dh2b, ws["w_out"], gates, attn, ga, gb, tm, "merge_bwd")
    grads["ssm_glu_a"] = _wgrad_colslots(yg, dga, tw, "dglu_a")
    grads["ssm_glu_b"] = _wgrad_colslots(yg, dgb, tw, "dglu_b")
    dy0 = _gelu_bwd([(dga, ws["ssm_glu_a"]), (dgb, ws["ssm_glu_b"])], y0, tm, "gelu_bwd")
    du_t, dbmat, dcmat, dlam, dd = _ssm_bwd(to_time_major(dy0), u_t, xs, bmat16, cmat16, w["ssm_d"],
                                            _scan_tables(lam, B, True), B, rc, "ssm_bwd")
    d_ssm = ssm_vjp((dlam[:, 0, :], dbmat, dcmat))
    for n, gval in zip(["ssm_a_re", "ssm_a_im", "ssm_log_step", "ssm_b_re", "ssm_b_im", "ssm_c_re", "ssm_c_im"], d_ssm):
        grads[n] = gval[None]
    grads["ssm_d"] = dd

    (dq, dkv, dsink), got = _attn_bwd(q, kvu, attn, dattn, sink_row, sink_meta, B, "attn_bwd",
                                      exchange_riders("ffn2"))
    received.update(zip(groups["ffn2"], got))
    grads["attn_sinks"] = dsink[:, 0:Q_PER_KV, 0].reshape(1, H)
    dkvu = jnp.concatenate([dkv, to_batch_major(du_t).astype(BF16)], axis=1)
    pieces = [dq, dkvu, dgat, dgss]
    dw_in = [_wgrad_plain(hn, p, f"dw_in_{k}", tw) for k, p in enumerate(pieces)]
    dw_in[1] = _kv_deinterleave(dw_in[1], KV)
    grads["w_in"] = jnp.stack(dw_in)
    w_in_parts = [ws["w_in"][0], w_kvu, ws["w_in"][2], ws["w_in"][3]]
    whole = _once((D, D), lambda i: (0, 0))
    (dh1, dh1b, grads["mix_norm"]), swap_mix = _mm_norm_bwd(
        "dhn", "nt", [(p, _spec((tm, D), lambda i: (i, 0)), wp, whole) for p, wp in zip(pieces, w_in_parts)],
        h1, g_mix, dh2, tm, swap_riders("mix"))
    swapped.update(zip(groups["mix"], swap_mix))
    dh0, _, got = ffn_bwd(h0, g_ffn1, saved1, dh1, dh1b, "ffn1", dhidden_carries=("exchange", "mix"),
                          dn_carries=("exchange", "ffn1"))
    received.update(zip(groups["ffn1"], got))
    dh0 = dh0.reshape(B, L, D)
    grad_x = dh0[:, N_META:, :]

    grads["final_norm"] = dg_final
    slay = _Layout(D, 8)
    for n in _SMALL:
        slay.add(n, w[n].shape)
    slay.add("loss", (1, D))
    meta_at = slay.align(8)
    slay.add("meta", (B * N_META, D))
    small = slay.pack({**{n: grads[n] for n in _SMALL}, "loss": loss_row, "meta": dh0[:, :N_META, :]})
    tot_small, dmeta = _all_reduce_small(small, B, N_META, meta_at)
    loss = slay.unpack(tot_small, "loss")[0, 0]
    for n in _SMALL:
        grads[n] = slay.unpack(tot_small, n)
    cw = D // N_CHIPS
    grads["meta_tokens"] = lax.dynamic_slice_in_dim(dmeta, my_slot * cw, cw, axis=1)

    fulls = [_sum_chips(grads[n], swapped[n], received[n], idx, f"grad_sum_chips_{n}") for n in _BIG]
    for n, f in zip(_BIG, _join_halves(fulls)):
        grads[n] = f

    delta, new_m, new_v = {}, {}, {}
    for n in _BIG + ["meta_tokens"]:
        delta[n], new_m[n], new_v[n] = _adamw(w[n], grads[n], m[n], v[n], f"adamw_{n}")
        grads[n] = grads[n].reshape(w[n].shape)

    def flat2d(a):
        return a.reshape(-1, a.shape[-1])

    d_, m_, v_ = _adamw_small([flat2d(w[n]) for n in _SMALL], [flat2d(grads[n]) for n in _SMALL],
                              [flat2d(m[n]) for n in _SMALL], [flat2d(v[n]) for n in _SMALL], "adamw_small")
    for i, n in enumerate(_SMALL):
        shp = w[n].shape
        delta[n], new_m[n], new_v[n] = d_[i].reshape(shp), m_[i].reshape(shp), v_[i].reshape(shp)
        grads[n] = grads[n].reshape(shp)

    return (loss, grad_x, *[grads[n] for n in _WEIGHTS], *[delta[n] for n in _WEIGHTS],
            *[new_m[n] for n in _WEIGHTS], *[new_v[n] for n in _WEIGHTS])


def kernel(x, meta_tokens, ffn1_norm, ffn1_w1, ffn1_w3, ffn1_w2, mix_norm, w_in, attn_sinks, ssm_a_re, ssm_a_im, ssm_log_step, ssm_b_re, ssm_b_im, ssm_c_re, ssm_c_im, ssm_d, ssm_glu_a, ssm_glu_b, w_out, ffn2_norm, ffn2_w1, ffn2_w3, ffn2_w2, final_norm, loss_target, m_meta_tokens, m_ffn1_norm, m_ffn1_w1, m_ffn1_w3, m_ffn1_w2, m_mix_norm, m_w_in, m_attn_sinks, m_ssm_a_re, m_ssm_a_im, m_ssm_log_step, m_ssm_b_re, m_ssm_b_im, m_ssm_c_re, m_ssm_c_im, m_ssm_d, m_ssm_glu_a, m_ssm_glu_b, m_w_out, m_ffn2_norm, m_ffn2_w1, m_ffn2_w3, m_ffn2_w2, m_final_norm, v_meta_tokens, v_ffn1_norm, v_ffn1_w1, v_ffn1_w3, v_ffn1_w2, v_mix_norm, v_w_in, v_attn_sinks, v_ssm_a_re, v_ssm_a_im, v_ssm_log_step, v_ssm_b_re, v_ssm_b_im, v_ssm_c_re, v_ssm_c_im, v_ssm_d, v_ssm_glu_a, v_ssm_glu_b, v_w_out, v_ffn2_norm, v_ffn2_w1, v_ffn2_w3, v_ffn2_w2, v_final_norm):
    args = locals()
    w = {n: args[n] for n in _WEIGHTS}
    m = {n: args["m_" + n] for n in _WEIGHTS}
    v = {n: args["v_" + n] for n in _WEIGHTS}
    return _step(x, loss_target, w, m, v)
```

```python
import functools
import math

import jax
import jax.numpy as jnp
from jax import lax
from jax.experimental import pallas as pl
from jax.experimental.pallas import tpu as pltpu

F32 = jnp.float32
BF16 = jnp.bfloat16
MESH_IDS = pl.DeviceIdType.MESH

N_CHIPS = 4
N_META = 16
HEAD_DIM = 64
Q_PER_KV = 4
QW = Q_PER_KV * HEAD_DIM
BLOCK = 128
SSM_GROUP = 16
SSM_STATE = 64
SSM_LANES = 128
GROUPS_PER_COL = SSM_LANES // SSM_GROUP
STATE_LANES = GROUPS_PER_COL * SSM_STATE
NORM_EPS = 1e-6
NEG_INF = -1e30
ADAM_LR, ADAM_B1, ADAM_B2, ADAM_EPS, ADAM_WD, ADAM_STEP = 0.001, 0.9, 0.999, 1e-08, 0.01, 10
GELU_C = math.sqrt(2.0 / math.pi)
ROW_ALIGN = 16
VMEM_LIMIT = 56 * 1024 * 1024
ROW_TILE_CAP = 688

_NN = (((1,), (0,)), ((), ()))
_NT = (((1,), (1,)), ((), ()))
_TN = (((0,), (0,)), ((), ()))
_DIMS = {"nn": _NN, "nt": _NT, "tn": _TN}


def _params(sem, **kw):
    return pltpu.CompilerParams(dimension_semantics=sem, vmem_limit_bytes=VMEM_LIMIT, **kw)


def _pick_tile(n, cap, mult):
    best = None
    for t in range(mult, min(n, cap) + 1, mult):
        if n % t == 0:
            best = t
    if best is None:
        raise ValueError(f"no tile for {n} (cap {cap}, multiple of {mult})")
    return best


def _sigmoid(x):
    return 0.5 * jnp.tanh(0.5 * x) + 0.5


def _spec(block, index_map):
    return pl.BlockSpec(block, index_map)


def _sum_dots(ins, mode):
    tot = None
    for p in range(len(ins) // 2):
        a_ref, b_ref = ins[2 * p], ins[2 * p + 1]
        for sl in ([None] if len(b_ref.shape) == 2 else range(b_ref.shape[0])):
            if sl is None:
                a, b = a_ref[...], b_ref[...]
            elif len(a_ref.shape) == 3:
                a, b = a_ref[sl], b_ref[sl]
            else:
                width = a_ref.shape[1] // b_ref.shape[0]
                a, b = a_ref[:, sl * width:(sl + 1) * width], b_ref[sl]
            d = lax.dot_general(a.astype(BF16), b.astype(BF16), _DIMS[mode], preferred_element_type=F32)
            tot = d if tot is None else tot + d
    return tot


def _mm(name, grid, kaxis, mode, pairs, out_shape, out_spec, scale=1.0, res=None):
    npairs = len(pairs)
    has_res = res is not None
    gk = 1 if kaxis is None else grid[kaxis]
    acc_shape = tuple(d for d in out_spec.block_shape if d is not None)

    def body(*refs):
        res_ref = refs[2 * npairs] if has_res else None
        o_ref = refs[2 * npairs + has_res]
        tot = _sum_dots(refs[:2 * npairs], mode)

        def finish(acc):
            r = acc * scale if scale != 1.0 else acc
            if has_res:
                r = res_ref[...] + r
            o_ref[...] = r.astype(o_ref.dtype)

        if gk == 1:
            finish(tot)
        else:
            acc_ref = refs[-1]
            k = pl.program_id(kaxis)

            @pl.when(k == 0)
            def _():
                acc_ref[...] = tot

            @pl.when(k > 0)
            def _():
                acc_ref[...] += tot

            @pl.when(k == gk - 1)
            def _():
                finish(acc_ref[...])

    in_specs, args = [], []
    for a, a_spec, b, b_spec in pairs:
        in_specs += [a_spec, b_spec]
        args += [a, b]
    if has_res:
        in_specs.append(res[1])
        args.append(res[0])
    sem = tuple("arbitrary" if ax == kaxis else "parallel" for ax in range(len(grid)))
    return pl.pallas_call(
        body, name=name, grid=grid, in_specs=in_specs, out_specs=out_spec, out_shape=out_shape,
        scratch_shapes=[pltpu.VMEM(acc_shape, F32)] if gk > 1 else [],
        compiler_params=_params(sem),
    )(*args)


def _mm_plain(a, b, mode, out_dtype, name, tm, scale=1.0):
    M, K = a.shape
    N = b.shape[1] if mode == "nn" else b.shape[0]
    return _mm(name, (M // tm,), None, mode,
               [(a, _spec((tm, K), lambda i: (i, 0)), b, _spec(b.shape, lambda i: (0, 0)))],
               jax.ShapeDtypeStruct((M, N), out_dtype), _spec((tm, N), lambda i: (i, 0)), scale=scale)


def _wgrad_plain(a, b, name, tr):
    R, M = a.shape
    N = b.shape[1]
    return _mm(name, (R // tr,), 0, "tn",
               [(a, _spec((tr, M), lambda r: (r, 0)), b, _spec((tr, N), lambda r: (r, 0)))],
               jax.ShapeDtypeStruct((M, N), F32), _spec((M, N), lambda r: (0, 0)))


def _rmsnorm_fwd(h, g, tm, name):
    T, D = h.shape

    def body(h_ref, g_ref, o_ref):
        x = h_ref[...]
        r = lax.rsqrt(jnp.mean(x * x, axis=-1, keepdims=True) + NORM_EPS)
        o_ref[...] = ((x * r) * g_ref[...]).astype(BF16)

    return pl.pallas_call(
        body, name=name, grid=(T // tm,),
        in_specs=[pl.BlockSpec((tm, D), lambda i: (i, 0)), pl.BlockSpec((1, D), lambda i: (0, 0))],
        out_specs=pl.BlockSpec((tm, D), lambda i: (i, 0)),
        out_shape=jax.ShapeDtypeStruct((T, D), BF16),
        compiler_params=_params(("parallel",)),
    )(h, g)


def _fold8(x):
    return jnp.sum(x.reshape(x.shape[0] // 8, 8, x.shape[1]), axis=0)


def _mm_norm_bwd(name, mode, pairs, h, g, dres, tm, riders=None):
    T, D = h.shape
    nt = T // tm
    npairs = len(pairs)

    def body(*refs):
        h_ref, g_ref, dres_ref, dh_ref, dhb_ref, dg_ref, acc_ref = refs[2 * npairs:]
        i = pl.program_id(0)
        x = h_ref[...]
        r = lax.rsqrt(jnp.mean(x * x, axis=-1, keepdims=True) + NORM_EPS)
        xhat = x * r
        dy = _sum_dots(refs[:2 * npairs], mode)
        dxhat = dy * g_ref[...]
        dx = r * (dxhat - xhat * jnp.mean(dxhat * xhat, axis=-1, keepdims=True))
        dh = dres_ref[...] + dx
        dh_ref[...] = dh
        dhb_ref[...] = dh.astype(BF16)
        part = _fold8(dy * xhat)

        @pl.when(i == 0)
        def _():
            acc_ref[...] = part

        @pl.when(i > 0)
        def _():
            acc_ref[...] += part

        @pl.when(i == nt - 1)
        def _():
            dg_ref[...] = jnp.sum(acc_ref[...], axis=0, keepdims=True)

    row = pl.BlockSpec((tm, D), lambda i: (i, 0))
    vec = pl.BlockSpec((1, D), lambda i: (0, 0))
    in_specs, args = [], []
    for a, a_spec, b, b_spec in pairs:
        in_specs += [a_spec, b_spec]
        args += [a, b]
    return _call(body, name, (nt,), in_specs + [row, vec, row], [row, row, vec],
                 [jax.ShapeDtypeStruct((T, D), F32), jax.ShapeDtypeStruct((T, D), BF16), jax.ShapeDtypeStruct((1, D), F32)],
                 [pltpu.VMEM((8, D), F32)], ("arbitrary",), (*args, h, g, dres), riders)


def _ffn_up(n, w1t, w3t, tm, name, riders=None):
    T, D = n.shape
    Fs = w1t.shape[1]

    def body(n_ref, w1_ref, w3_ref, a_ref, c_ref, s_ref):
        x = n_ref[...]
        a = lax.dot_general(x, w1_ref[...], _NT, preferred_element_type=F32)
        c = lax.dot_general(x, w3_ref[...], _NT, preferred_element_type=F32)
        a_ref[...] = a.astype(BF16)
        c_ref[...] = c.astype(BF16)
        s_ref[...] = (a * _sigmoid(a) * c).astype(BF16)

    w_spec = _spec((None, Fs, D), lambda s, i: (s, 0, 0))
    o_spec = _spec((None, tm, Fs), lambda s, i: (s, i, 0))
    o_shape = jax.ShapeDtypeStruct((N_CHIPS, T, Fs), BF16)
    return _call(body, name, (N_CHIPS, T // tm), [_spec((tm, D), lambda s, i: (i, 0)), w_spec, w_spec],
                 [o_spec, o_spec, o_spec], [o_shape, o_shape, o_shape], [], ("parallel", "parallel"), (n, w1t, w3t), riders)


def _ffn_down(s, w2, h, tm, name):
    _, T, Fs = s.shape
    D = w2.shape[2]
    row = _spec((tm, D), lambda i: (i, 0))
    return _mm(name, (T // tm,), None, "nn",
               [(s, _spec((N_CHIPS, tm, Fs), lambda i: (0, i, 0)), w2, _spec((N_CHIPS, Fs, D), lambda i: (0, 0, 0)))],
               jax.ShapeDtypeStruct((T, D), F32), row, scale=0.5, res=(h, row))


def _ffn_dhidden(dhb, w2, a, c, tm, name, riders=None):
    T, D = dhb.shape
    Fs = w2.shape[1]

    def body(dh_ref, w2_ref, a_ref, c_ref, da_ref, dc_ref):
        d = 0.5 * lax.dot_general(dh_ref[...], w2_ref[...], _NT, preferred_element_type=F32)
        av = a_ref[...].astype(F32)
        cv = c_ref[...].astype(F32)
        sg = _sigmoid(av)
        da_ref[...] = (d * cv * (sg * (1.0 + av * (1.0 - sg)))).astype(BF16)
        dc_ref[...] = (d * (av * sg)).astype(BF16)

    h_spec = _spec((None, tm, Fs), lambda s, i: (s, i, 0))
    o_shape = jax.ShapeDtypeStruct((N_CHIPS, T, Fs), BF16)
    return _call(body, name, (N_CHIPS, T // tm),
                 [_spec((tm, D), lambda s, i: (i, 0)), _spec((None, Fs, D), lambda s, i: (s, 0, 0)), h_spec, h_spec],
                 [h_spec, h_spec], [o_shape, o_shape], [], ("parallel", "parallel"), (dhb, w2, a, c), riders)


def _wgrad_hidden_rows(s, dhb, tr, name, scale):
    _, T, Fs = s.shape
    D = dhb.shape[1]
    return _mm(name, (N_CHIPS, T // tr), 1, "tn",
               [(s, _spec((None, tr, Fs), lambda k, r: (k, r, 0)), dhb, _spec((tr, D), lambda k, r: (r, 0)))],
               jax.ShapeDtypeStruct((N_CHIPS, Fs, D), F32), _spec((None, Fs, D), lambda k, r: (k, 0, 0)), scale=scale)


def _once(block, index_map):
    return pl.BlockSpec(block, index_map, pipeline_mode=pl.Buffered(1))


def _ffn_dn(da, w1t, dc, w3t, h, g, dres, tm, name, riders=None):
    _, T, Fs = da.shape
    D = w1t.shape[2]
    h_spec = _spec((N_CHIPS, tm, Fs), lambda i: (0, i, 0))
    w_spec = _once((N_CHIPS, Fs, D), lambda i: (0, 0, 0))
    return _mm_norm_bwd(name, "nn", [(da, h_spec, w1t, w_spec), (dc, h_spec, w3t, w_spec)], h, g, dres, tm, riders)


def _mm_side_by_side(a, w, mode, out_dtype, name, tm, first=0, count=N_CHIPS, scale=1.0):
    T, K = a.shape
    assert first % count == 0
    n = w.shape[2] if mode == "nn" else w.shape[1]

    def body(a_ref, w_ref, o_ref):
        av = a_ref[...].astype(BF16)
        for j in range(count):
            r = lax.dot_general(av, w_ref[j].astype(BF16), _DIMS[mode], preferred_element_type=F32)
            o_ref[:, j * n:(j + 1) * n] = (r * scale if scale != 1.0 else r).astype(o_ref.dtype)

    return pl.pallas_call(
        body, name=name, grid=(T // tm,),
        in_specs=[_spec((tm, K), lambda i: (i, 0)), _once((count,) + w.shape[1:], lambda i: (first // count, 0, 0))],
        out_specs=_spec((tm, count * n), lambda i: (i, 0)),
        out_shape=jax.ShapeDtypeStruct((T, count * n), out_dtype),
        compiler_params=_params(("parallel",)),
    )(a, w)


def _mm_colslots(a, w, out_dtype, name, tm, first=0, count=N_CHIPS, scale=1.0):
    return _mm_side_by_side(a, w, "nn", out_dtype, name, tm, first, count, scale)


def _wgrad_colslots(a, d, tr, name):
    T, K = a.shape
    Ns = d.shape[1] // N_CHIPS
    return _mm(name, (N_CHIPS, T // tr), 1, "tn",
               [(a, _spec((tr, K), lambda k, r: (r, 0)), d, _spec((tr, Ns), lambda k, r: (r, k)))],
               jax.ShapeDtypeStruct((N_CHIPS, K, Ns), F32), _spec((None, K, Ns), lambda k, r: (k, 0, 0)))


def _mm_rowslots(a, w, h, tm, name):
    T = a.shape[0]
    N = w.shape[2]
    row = _spec((tm, N), lambda i: (i, 0))
    return _mm(name, (T // tm,), None, "nn",
               [(a, _spec((tm, a.shape[1]), lambda i: (i, 0)), w, _once(w.shape, lambda i: (0, 0, 0)))],
               jax.ShapeDtypeStruct((T, N), F32), row, res=(h, row))


def _wgrad_rowslots(a, d, tr, name):
    T = a.shape[0]
    Ks = a.shape[1] // N_CHIPS
    N = d.shape[1]
    return _mm(name, (N_CHIPS, T // tr), 1, "tn",
               [(a, _spec((tr, Ks), lambda k, r: (r, k)), d, _spec((tr, N), lambda k, r: (r, 0)))],
               jax.ShapeDtypeStruct((N_CHIPS, Ks, N), F32), _spec((None, Ks, N), lambda k, r: (k, 0, 0)))


def _gelu_parts(x):
    inner = GELU_C * (x + 0.044715 * (x * x * x))
    t = jnp.tanh(inner)
    return t, GELU_C * (1.0 + 3.0 * 0.044715 * (x * x))


def _gelu_fwd(y, tm, name):
    T, W = y.shape

    def body(y_ref, o_ref):
        x = y_ref[...]
        t, _ = _gelu_parts(x)
        o_ref[...] = (0.5 * x * (1.0 + t)).astype(BF16)

    spec = pl.BlockSpec((tm, W), lambda i: (i, 0))
    return pl.pallas_call(body, name=name, grid=(T // tm,), in_specs=[spec], out_specs=spec,
                          out_shape=jax.ShapeDtypeStruct((T, W), BF16),
                          compiler_params=_params(("parallel",)))(y)


def _gelu_bwd(pairs, y, tm, name):
    T, W = y.shape
    npairs = len(pairs)

    def body(*refs):
        y_ref, o_ref = refs[2 * npairs], refs[2 * npairs + 1]
        x = y_ref[...]
        t, dinner = _gelu_parts(x)
        o_ref[...] = _sum_dots(refs[:2 * npairs], "nt") * (0.5 * (1.0 + t) + 0.5 * x * (1.0 - t * t) * dinner)

    spec = pl.BlockSpec((tm, W), lambda i: (i, 0))
    in_specs, args = [], []
    for d, w in pairs:
        in_specs += [_spec((tm, d.shape[1]), lambda i: (i, 0)), _once(w.shape, lambda i: (0, 0, 0))]
        args += [d, w]
    return pl.pallas_call(body, name=name, grid=(T // tm,), in_specs=in_specs + [spec], out_specs=spec,
                          out_shape=jax.ShapeDtypeStruct((T, W), F32),
                          compiler_params=_params(("parallel",)))(*args, y)


def _merge_cols(D):
    cb = 512 if D % 512 == 0 else D
    return cb, D // cb


def _merge_fwd(gates, attn, ga, gb, tm, name):
    T, D = attn.shape
    cb, nc = _merge_cols(D)

    def body(gat_ref, gss_ref, attn_ref, ga_ref, gb_ref, o_ref):
        ssm = ga_ref[...] * _sigmoid(gb_ref[...])
        o_ref[...] = (_sigmoid(gat_ref[...]) * attn_ref[...] + _sigmoid(gss_ref[...]) * ssm).astype(BF16)

    def col(block):
        return pl.BlockSpec((tm, cb), lambda i, j: (i, block * nc + j))

    return pl.pallas_call(
        body, name=name, grid=(T // tm, nc),
        in_specs=[col(0), col(1), col(0), col(0), col(0)],
        out_specs=col(0), out_shape=jax.ShapeDtypeStruct((T, D), BF16),
        compiler_params=_params(("parallel", "parallel")),
    )(gates, gates, attn, ga, gb)


def _merge_bwd(dhb, w_out, gates, attn, ga, gb, tm, name):
    T, D = attn.shape
    cb, nc = _merge_cols(D)
    Ks = w_out.shape[1]
    spb = cb // Ks

    def body(dh_ref, w_ref, gat_ref, gss_ref, attn_ref, ga_ref, gb_ref, dattn_ref, dgat_ref, dgss_ref, dga_ref, dgb_ref):
        dh = dh_ref[...]
        d = jnp.concatenate([lax.dot_general(dh, w_ref[s], _NT, preferred_element_type=F32) for s in range(spb)], axis=1)
        sa = _sigmoid(gat_ref[...])
        ss = _sigmoid(gss_ref[...])
        sb = _sigmoid(gb_ref[...])
        gav = ga_ref[...]
        dattn_ref[...] = d * sa
        dgat_ref[...] = (d * attn_ref[...] * (sa * (1.0 - sa))).astype(BF16)
        dgss_ref[...] = (d * (gav * sb) * (ss * (1.0 - ss))).astype(BF16)
        dssm = d * ss
        dga_ref[...] = (dssm * sb).astype(BF16)
        dgb_ref[...] = (dssm * gav * (sb * (1.0 - sb))).astype(BF16)

    def col(block):
        return pl.BlockSpec((tm, cb), lambda i, j: (i, block * nc + j))

    b16 = jax.ShapeDtypeStruct((T, D), BF16)
    return pl.pallas_call(
        body, name=name, grid=(T // tm, nc),
        in_specs=[pl.BlockSpec((tm, D), lambda i, j: (i, 0)), pl.BlockSpec((spb, Ks, D), lambda i, j: (j, 0, 0)),
                  col(0), col(1), col(0), col(0), col(0)],
        out_specs=[col(0)] * 5,
        out_shape=[jax.ShapeDtypeStruct((T, D), F32), b16, b16, b16, b16],
        compiler_params=_params(("parallel", "parallel")),
    )(dhb, w_out, gates, gates, attn, ga, gb)


def _loss_head(h, g, target, tm, name):
    T, D = h.shape
    B, S, _ = target.shape
    L = S + N_META
    nt = T // tm
    tpe = L // tm

    def body(h_ref, g_ref, t_hbm, dh_ref, dhb_ref, dg_ref, loss_ref, tbuf, acc_g, acc_l, sems):
        i = pl.program_id(0)
        j = i % tpe
        slot = i % 2

        def fetch(tile, sl, act):
            tb, tj = tile // tpe, tile % tpe

            @pl.when(tj == 0)
            def _():
                act(pltpu.make_async_copy(t_hbm.at[tb, pl.ds(0, tm - N_META), :],
                                          tbuf.at[sl, pl.ds(N_META, tm - N_META), :], sems.at[sl]))

            @pl.when(tj > 0)
            def _():
                act(pltpu.make_async_copy(t_hbm.at[tb, pl.ds(tj * tm - N_META, tm), :], tbuf.at[sl], sems.at[sl]))

        @pl.when(i == 0)
        def _():
            tbuf[:, 0:N_META, :] = jnp.zeros((2, N_META, D), F32)
            fetch(i, slot, lambda cp: cp.start())

        @pl.when(i + 1 < nt)
        def _():
            fetch(i + 1, 1 - slot, lambda cp: cp.start())

        fetch(i, slot, lambda cp: cp.wait())

        x = h_ref[...]
        gv = g_ref[...]
        r = lax.rsqrt(jnp.mean(x * x, axis=-1, keepdims=True) + NORM_EPS)
        xhat = x * r
        pos = j * tm + lax.broadcasted_iota(jnp.int32, (tm, 1), 0)
        err = jnp.where(pos >= N_META, xhat * gv - tbuf[slot], 0.0)
        dy = err * (1.0 / D)
        dxhat = dy * gv
        dh = r * (dxhat - xhat * jnp.mean(dxhat * xhat, axis=-1, keepdims=True))
        dh_ref[...] = dh
        dhb_ref[...] = dh.astype(BF16)
        pg = _fold8(dy * xhat)
        pe = _fold8(err * err)

        @pl.when(i == 0)
        def _():
            acc_g[...] = pg
            acc_l[...] = pe

        @pl.when(i > 0)
        def _():
            acc_g[...] += pg
            acc_l[...] += pe

        @pl.when(i == nt - 1)
        def _():
            dg_ref[...] = jnp.sum(acc_g[...], axis=0, keepdims=True)
            loss_ref[...] = jnp.full((1, D), (0.5 / D) * jnp.sum(acc_l[...]), F32)

    row = pl.BlockSpec((tm, D), lambda i: (i, 0))
    vec = pl.BlockSpec((1, D), lambda i: (0, 0))
    return pl.pallas_call(
        body, name=name, grid=(nt,),
        in_specs=[row, vec, pl.BlockSpec(memory_space=pl.ANY)], out_specs=[row, row, vec, vec],
        out_shape=[jax.ShapeDtypeStruct((T, D), F32), jax.ShapeDtypeStruct((T, D), BF16),
                   jax.ShapeDtypeStruct((1, D), F32), jax.ShapeDtypeStruct((1, D), F32)],
        scratch_shapes=[pltpu.VMEM((2, tm, D), F32), pltpu.VMEM((8, D), F32), pltpu.VMEM((8, D), F32),
                        pltpu.SemaphoreType.DMA((2,))],
        compiler_params=_params(("arbitrary",)),
    )(h, g, target)


def _heads_to_rows(blk):
    return jnp.concatenate([blk[:, g * HEAD_DIM:(g + 1) * HEAD_DIM] for g in range(Q_PER_KV)], axis=0)


def _rows_to_heads(x):
    rows = x.shape[0] // Q_PER_KV
    return jnp.concatenate([x[g * rows:(g + 1) * rows] for g in range(Q_PER_KV)], axis=1)


def _causal(R):
    kj = lax.broadcasted_iota(jnp.int32, (BLOCK, R), 0)
    qi = lax.broadcasted_iota(jnp.int32, (BLOCK, R), 1) & (BLOCK - 1)
    return kj <= qi


def _band_probs(s_band, s_m, sink):
    m = jnp.maximum(jnp.maximum(jnp.max(s_band, axis=0, keepdims=True), jnp.max(s_m, axis=0, keepdims=True)), sink)
    e_b, e_m, e_s = jnp.exp(s_band - m), jnp.exp(s_m - m), jnp.exp(sink - m)
    inv = 1.0 / (jnp.sum(e_b, axis=0, keepdims=True) + jnp.sum(e_m, axis=0, keepdims=True) + e_s)
    return e_b * inv, e_m * inv, e_s * inv


def _fold_band(tri, two):
    return jnp.where(tri, two[BLOCK:2 * BLOCK], two[0:BLOCK])


def _unfold_band(tri, band):
    return jnp.concatenate([jnp.where(tri, 0.0, band), jnp.where(tri, band, 0.0)], axis=0)


def _meta_probs(qm, k_m, sink_m):
    R = qm.shape[0]
    s = lax.dot_general(qm, k_m, _NT, preferred_element_type=F32)
    qi = lax.broadcasted_iota(jnp.int32, (R, N_META), 0) & (N_META - 1)
    kj = lax.broadcasted_iota(jnp.int32, (R, N_META), 1)
    s = jnp.where(kj <= qi, s, NEG_INF)
    m = jnp.maximum(jnp.max(s, axis=-1, keepdims=True), sink_m)
    e, e_s = jnp.exp(s - m), jnp.exp(sink_m - m)
    inv = 1.0 / (jnp.sum(e, axis=-1, keepdims=True) + e_s)
    return e * inv, e_s * inv


def _block_start(n):
    return pl.multiple_of(N_META + n * BLOCK, ROW_ALIGN)


def _kv(blk):
    return blk[:, 0:HEAD_DIM], blk[:, HEAD_DIM:2 * HEAD_DIM]


def _attn_fwd(q, kv, sink_row, sink_meta, B, name, riders=None):
    T, D = q.shape
    L = T // B
    KV = D // QW
    nb = (L - N_META) // BLOCK

    def body(q_ref, kv_ref, sk_ref, skm_ref, o_ref, kvs):
        kvs[...] = kv_ref[...].astype(BF16)
        k_m, v_m = _kv(kvs[0:N_META, :])
        p, _ = _meta_probs(_heads_to_rows(q_ref[0:N_META, :]), k_m, skm_ref[0])
        o_ref[0:N_META, :] = _rows_to_heads(jnp.dot(p.astype(BF16), v_m, preferred_element_type=F32))
        tri = _causal(Q_PER_KV * BLOCK)

        def block(cur, first, keys):
            k2, v2 = _kv(kvs[keys, :])
            qb = _heads_to_rows(q_ref[pl.ds(cur, BLOCK), :])
            st = lax.dot_general(k2, qb, _NT, preferred_element_type=F32)
            smt = lax.dot_general(k_m, qb, _NT, preferred_element_type=F32)
            s_band = jnp.where(tri, st, NEG_INF) if first else _fold_band(tri, st)
            p_b, p_m, _ = _band_probs(s_band, smt, sk_ref[0])
            p2 = (p_b if first else _unfold_band(tri, p_b)).astype(BF16)
            o = (lax.dot_general(p2, v2, _TN, preferred_element_type=F32)
                 + lax.dot_general(p_m.astype(BF16), v_m, _TN, preferred_element_type=F32))
            o_ref[pl.ds(cur, BLOCK), :] = _rows_to_heads(o)

        block(N_META, True, pl.ds(N_META, BLOCK))

        def step(n, carry):
            block(_block_start(n), False, pl.ds(_block_start(n - 1), 2 * BLOCK))
            return carry

        lax.fori_loop(1, nb, step, 0)

    q_spec = pl.BlockSpec((L, QW), lambda b, h: (b, h))
    return _call(body, name, (B, KV),
                 [q_spec, pl.BlockSpec((L, 2 * HEAD_DIM), lambda b, h: (b, h)),
                  pl.BlockSpec((1, 1, Q_PER_KV * BLOCK), lambda b, h: (h, 0, 0)),
                  pl.BlockSpec((1, Q_PER_KV * N_META, 1), lambda b, h: (h, 0, 0))],
                 [q_spec], [jax.ShapeDtypeStruct((T, D), F32)], [pltpu.VMEM((L, 2 * HEAD_DIM), BF16)],
                 ("parallel", "parallel"), (q, kv, sink_row, sink_meta), riders)


def _attn_bwd(q, kv, o, do, sink_row, sink_meta, B, name, riders=None):
    T, D = q.shape
    L = T // B
    KV = D // QW
    nb = (L - N_META) // BLOCK
    R = Q_PER_KV * BLOCK
    scale = HEAD_DIM ** -0.5

    def head_totals(col, rows_per_head):
        rid = lax.broadcasted_iota(jnp.int32, (8, 128), 0)
        out = jnp.zeros((8, 128), F32)
        for g in range(Q_PER_KV):
            out = out + jnp.where(rid == g, jnp.sum(col[g * rows_per_head:(g + 1) * rows_per_head, :]), 0.0)
        return out

    def body(q_ref, kv_ref, o_ref, do_ref, sk_ref, skm_ref, dq_ref, dkv_ref, dsk_ref, kvs, acc, acc_sink):
        b = pl.program_id(1)
        kvs[...] = kv_ref[...].astype(BF16)
        acc[...] = jnp.zeros_like(acc)
        k_m, v_m = _kv(kvs[0:N_META, :])

        qm = _heads_to_rows(q_ref[0:N_META, :])
        dom = _heads_to_rows(do_ref[0:N_META, :])
        delta = jnp.sum(dom * _heads_to_rows(o_ref[0:N_META, :]), axis=-1, keepdims=True)
        p, p_s = _meta_probs(qm, k_m, skm_ref[0])
        domb = dom.astype(BF16)
        ds = (p * (lax.dot_general(domb, v_m, _NT, preferred_element_type=F32) - delta)).astype(BF16)
        dq_ref[0:N_META, :] = _rows_to_heads(jnp.dot(ds, k_m, preferred_element_type=F32) * scale).astype(BF16)
        acc[0:N_META, :] += jnp.concatenate([lax.dot_general(ds, qm, _TN, preferred_element_type=F32),
                                             lax.dot_general(p.astype(BF16), domb, _TN, preferred_element_type=F32)], axis=1)
        sink_tot = head_totals(-p_s * delta, N_META)
        tri = _causal(R)
        acc_sink[...] = jnp.zeros_like(acc_sink)
        ones = jnp.ones((8, HEAD_DIM), BF16)

        def block(cur, first, keys):
            k2, v2 = _kv(kvs[keys, :])
            rows = pl.ds(cur, BLOCK)
            qb = _heads_to_rows(q_ref[rows, :])
            dob = _heads_to_rows(do_ref[rows, :])
            prod = dob * _heads_to_rows(o_ref[rows, :])
            hi = prod.astype(BF16)
            lo = (prod - hi.astype(F32)).astype(BF16)
            delta = (lax.dot_general(ones, hi, _NT, preferred_element_type=F32)
                     + lax.dot_general(ones, lo, _NT, preferred_element_type=F32))[0:1]
            dobb = dob.astype(BF16)
            st = lax.dot_general(k2, qb, _NT, preferred_element_type=F32)
            smt = lax.dot_general(k_m, qb, _NT, preferred_element_type=F32)
            s_band = jnp.where(tri, st, NEG_INF) if first else _fold_band(tri, st)
            p_b, p_m, p_s = _band_probs(s_band, smt, sk_ref[0])
            dpt = lax.dot_general(v2, dobb, _NT, preferred_element_type=F32)
            dpm = lax.dot_general(v_m, dobb, _NT, preferred_element_type=F32)
            ds_b = p_b * ((dpt if first else _fold_band(tri, dpt)) - delta)
            ds2 = (ds_b if first else _unfold_band(tri, ds_b)).astype(BF16)
            p2 = (p_b if first else _unfold_band(tri, p_b)).astype(BF16)
            dsm = (p_m * (dpm - delta)).astype(BF16)
            pm = p_m.astype(BF16)
            dq = (lax.dot_general(ds2, k2, _TN, preferred_element_type=F32)
                  + lax.dot_general(dsm, k_m, _TN, preferred_element_type=F32))
            dq_ref[rows, :] = _rows_to_heads(dq * scale).astype(BF16)
            acc[keys, :] += jnp.concatenate([jnp.dot(ds2, qb, preferred_element_type=F32),
                                             jnp.dot(p2, dobb, preferred_element_type=F32)], axis=1)
            acc[0:N_META, :] += jnp.concatenate([jnp.dot(dsm, qb, preferred_element_type=F32),
                                                 jnp.dot(pm, dobb, preferred_element_type=F32)], axis=1)
            acc_sink[0:1, :] += -p_s * delta

        block(N_META, True, pl.ds(N_META, BLOCK))

        def step(n, carry):
            block(_block_start(n), False, pl.ds(_block_start(n - 1), 2 * BLOCK))
            return carry

        lax.fori_loop(1, nb, step, 0)
        dkv_ref[...] = acc[...].astype(BF16)
        rid = lax.broadcasted_iota(jnp.int32, (8, 128), 0)
        tot = sink_tot
        for g in range(Q_PER_KV):
            tot = tot + jnp.where(rid == g, jnp.sum(acc_sink[:, g * BLOCK:(g + 1) * BLOCK]), 0.0)

        @pl.when(b == 0)
        def _():
            dsk_ref[0] = tot

        @pl.when(b > 0)
        def _():
            dsk_ref[0] += tot

    q_spec = pl.BlockSpec((L, QW), lambda h, b: (b, h))
    kv_spec = pl.BlockSpec((L, 2 * HEAD_DIM), lambda h, b: (b, h))
    return _call(body, name, (KV, B),
                 [q_spec, kv_spec, q_spec, q_spec,
                  pl.BlockSpec((1, 1, R), lambda h, b: (h, 0, 0)),
                  pl.BlockSpec((1, Q_PER_KV * N_META, 1), lambda h, b: (h, 0, 0))],
                 [q_spec, kv_spec, pl.BlockSpec((1, 8, 128), lambda h, b: (h, 0, 0))],
                 [jax.ShapeDtypeStruct((T, D), BF16), jax.ShapeDtypeStruct((T, KV * 2 * HEAD_DIM), BF16),
                  jax.ShapeDtypeStruct((KV, 8, 128), F32)],
                 [pltpu.VMEM((L, 2 * HEAD_DIM), BF16), pltpu.VMEM((L, 2 * HEAD_DIM), F32), pltpu.VMEM((8, R), F32)],
                 ("parallel", "arbitrary"), (q, kv, o, do, sink_row, sink_meta), riders)


def _cmul_add(acc_r, acc_i, lr, li, xr, xi):
    return acc_r + (lr * xr - li * xi), acc_i + (lr * xi + li * xr)


def _cols_per_step(ncol):
    for cps in (4, 2):
        if ncol % cps == 0:
            return cps
    return 1


def _ssm_fwd(u, bmat, cmat, dskip, tables, nbatch, rc, name):
    T, W = u.shape
    ncol = W // SSM_LANES
    nch = T // rc
    S = STATE_LANES
    cps = _cols_per_step(ncol)
    assert nbatch == 4

    def body(u_ref, b_ref, c_ref, d_ref, tab_ref, y_ref, xs_ref, st_ref, carry_ref):
        ch = pl.program_id(1)

        @pl.when(ch == 0)
        def _():
            carry_ref[...] = jnp.zeros_like(carry_ref)

        uv = u_ref[...]
        for k in range(cps):
            st_ref[:, 2 * S * k:2 * S * (k + 1)] = jnp.dot(uv[:, SSM_LANES * k:SSM_LANES * (k + 1)].astype(BF16), b_ref[k],
                                                           preferred_element_type=F32)
        low = lax.broadcasted_iota(jnp.int32, (8, S), 0) < nbatch

        def tile(k, r0, c_r, c_i):
            re, im = slice(2 * S * k, 2 * S * k + S), slice(2 * S * k + S, 2 * S * (k + 1))
            la_r, la_i = tab_ref[k, :, 0:S], tab_ref[k, :, S:2 * S]
            lb_r, lb_i = tab_ref[k, :, 2 * S:3 * S], tab_ref[k, :, 3 * S:4 * S]
            v_r = st_ref[pl.ds(r0, 8), re]
            v_i = st_ref[pl.ds(r0, 8), im]
            v_r, v_i = _cmul_add(v_r, v_i, la_r, la_i, pltpu.roll(v_r, nbatch, 0), pltpu.roll(v_i, nbatch, 0))
            rc_r, rc_i = pltpu.roll(c_r, nbatch, 0), pltpu.roll(c_i, nbatch, 0)
            cb_r, cb_i = jnp.where(low, rc_r, c_r), jnp.where(low, rc_i, c_i)
            v_r, v_i = _cmul_add(v_r, v_i, lb_r, lb_i, cb_r, cb_i)
            st_ref[pl.ds(r0, 8), re] = v_r
            st_ref[pl.ds(r0, 8), im] = v_i
            return v_r, v_i

        def step(i, carry):
            r0 = pl.multiple_of(i * 8, 8)
            out = []
            for k in range(cps):
                out += list(tile(k, r0, carry[2 * k], carry[2 * k + 1]))
            return tuple(out)

        halves = tuple(carry_ref[:, S * j:S * (j + 1)] for j in range(2 * cps))
        halves = lax.fori_loop(0, rc // 8, step, halves)
        for j in range(2 * cps):
            carry_ref[:, S * j:S * (j + 1)] = halves[j]
        xb = st_ref[...].astype(BF16)
        xs_ref[...] = xb
        for k in range(cps):
            cols = slice(SSM_LANES * k, SSM_LANES * (k + 1))
            y_ref[:, cols] = (jnp.dot(xb[:, 2 * S * k:2 * S * (k + 1)], c_ref[k], preferred_element_type=F32)
                              + d_ref[:, cols] * uv[:, cols])

    return pl.pallas_call(
        body, name=name, grid=(ncol // cps, nch),
        in_specs=[pl.BlockSpec((rc, cps * SSM_LANES), lambda g, c: (c, g)),
                  pl.BlockSpec((cps, SSM_LANES, 2 * S), lambda g, c: (g, 0, 0)),
                  pl.BlockSpec((cps, 2 * S, SSM_LANES), lambda g, c: (g, 0, 0)),
                  pl.BlockSpec((1, cps * SSM_LANES), lambda g, c: (0, g)),
                  pl.BlockSpec((cps, 8, 4 * S), lambda g, c: (g, 0, 0))],
        out_specs=[pl.BlockSpec((rc, cps * SSM_LANES), lambda g, c: (c, g)),
                   pl.BlockSpec((rc, cps * 2 * S), lambda g, c: (c, g))],
        out_shape=[jax.ShapeDtypeStruct((T, W), F32), jax.ShapeDtypeStruct((T, ncol * 2 * S), BF16)],
        scratch_shapes=[pltpu.VMEM((rc, cps * 2 * S), F32), pltpu.VMEM((8, cps * 2 * S), F32)],
        compiler_params=_params(("parallel", "arbitrary")),
    )(u, bmat, cmat, dskip, tables)


def _ssm_bwd(dy, u, xs, bmat, cmat, dskip, tables, nbatch, rc, name):
    T, W = u.shape
    ncol = W // SSM_LANES
    nch = T // rc
    S = STATE_LANES
    ntile = rc // 16
    cps = _cols_per_step(ncol)

    def body(dy_ref, u_ref, xs_ref, b_ref, c_ref, d_ref, tab_ref,
             du_ref, db_ref, dc_ref, dl_ref, dd_ref, st_ref, carry_ref, accl_ref, accd_ref):
        ch = pl.program_id(1)

        @pl.when(ch == 0)
        def _():
            carry_ref[...] = jnp.zeros_like(carry_ref)
            accl_ref[...] = jnp.zeros_like(accl_ref)
            accd_ref[...] = jnp.zeros_like(accd_ref)
            db_ref[...] = jnp.zeros_like(db_ref)
            dc_ref[...] = jnp.zeros_like(dc_ref)

        dyv = dy_ref[...]
        uv = u_ref[...]
        dyb = dyv.astype(BF16)
        for k in range(cps):
            st_ref[:, 2 * S * k:2 * S * (k + 1)] = lax.dot_general(dyb[:, SSM_LANES * k:SSM_LANES * (k + 1)], c_ref[k], _NT,
                                                                   preferred_element_type=F32)
        low = lax.broadcasted_iota(jnp.int32, (8, S), 0) < nbatch

        def tile(k, r0, x_r, x_i, c_r, c_i, al_r, al_i):
            re, im = slice(2 * S * k, 2 * S * k + S), slice(2 * S * k + S, 2 * S * (k + 1))
            la_r, la_i = tab_ref[k, :, 0:S], tab_ref[k, :, S:2 * S]
            lb_r, lb_i = tab_ref[k, :, 2 * S:3 * S], tab_ref[k, :, 3 * S:4 * S]
            v_r = st_ref[pl.ds(r0, 8), re]
            v_i = st_ref[pl.ds(r0, 8), im]
            v_r, v_i = _cmul_add(v_r, v_i, la_r, la_i, pltpu.roll(v_r, nbatch, 0), pltpu.roll(v_i, nbatch, 0))
            cb_r = jnp.where(low, c_r, pltpu.roll(c_r, nbatch, 0))
            cb_i = jnp.where(low, c_i, pltpu.roll(c_i, nbatch, 0))
            v_r, v_i = _cmul_add(v_r, v_i, lb_r, lb_i, cb_r, cb_i)
            st_ref[pl.ds(r0, 8), re] = v_r
            st_ref[pl.ds(r0, 8), im] = v_i
            n_r = jnp.where(low, pltpu.roll(v_r, nbatch, 0), cb_r)
            n_i = jnp.where(low, pltpu.roll(v_i, nbatch, 0), cb_i)
            al_r = al_r + (n_r * x_r + n_i * x_i)
            al_i = al_i + (n_i * x_r - n_r * x_i)
            return v_r, v_i, al_r, al_i

        def step(j, carry):
            r0 = pl.multiple_of((ntile - 1 - j) * 16, 16)
            out = []
            for k in range(cps):
                re, im = slice(2 * S * k, 2 * S * k + S), slice(2 * S * k + S, 2 * S * (k + 1))
                x_r = xs_ref[pl.ds(r0, 16), re].astype(F32)
                x_i = xs_ref[pl.ds(r0, 16), im].astype(F32)
                mid = tile(k, r0 + 8, x_r[8:16], x_i[8:16], *carry[4 * k:4 * k + 4])
                out += list(tile(k, r0, x_r[0:8], x_i[0:8], *mid))
            return tuple(out)

        init = []
        for k in range(cps):
            init += [carry_ref[:, 2 * S * k:2 * S * k + S], carry_ref[:, 2 * S * k + S:2 * S * (k + 1)],
                     accl_ref[:, 2 * S * k:2 * S * k + S], accl_ref[:, 2 * S * k + S:2 * S * (k + 1)]]
        fin = lax.fori_loop(0, ntile, step, tuple(init))
        for k in range(cps):
            carry_ref[:, 2 * S * k:2 * S * k + S] = fin[4 * k]
            carry_ref[:, 2 * S * k + S:2 * S * (k + 1)] = fin[4 * k + 1]
            accl_ref[:, 2 * S * k:2 * S * k + S] = fin[4 * k + 2]
            accl_ref[:, 2 * S * k + S:2 * S * (k + 1)] = fin[4 * k + 3]
        dsb = st_ref[...].astype(BF16)
        ub = uv.astype(BF16)
        for k in range(cps):
            cols, lanes = slice(SSM_LANES * k, SSM_LANES * (k + 1)), slice(2 * S * k, 2 * S * (k + 1))
            du_ref[:, cols] = (lax.dot_general(dsb[:, lanes], b_ref[k], _NT, preferred_element_type=F32)
                               + d_ref[:, cols] * dyv[:, cols])
            db_ref[k] += lax.dot_general(ub[:, cols], dsb[:, lanes], _TN, preferred_element_type=F32)
            dc_ref[k] += lax.dot_general(xs_ref[:, lanes], dyb[:, cols], _TN, preferred_element_type=F32)
        accd_ref[...] += _fold8(dyv * uv)

        @pl.when(ch == nch - 1)
        def _():
            for k in range(cps):
                dl_ref[k] = jnp.sum(accl_ref[:, 2 * S * k:2 * S * (k + 1)], axis=0, keepdims=True)
            dd_ref[...] = jnp.sum(accd_ref[...], axis=0, keepdims=True)

    rev = lambda g, c: (nch - 1 - c, g)
    return pl.pallas_call(
        body, name=name, grid=(ncol // cps, nch),
        in_specs=[pl.BlockSpec((rc, cps * SSM_LANES), rev), pl.BlockSpec((rc, cps * SSM_LANES), rev),
                  pl.BlockSpec((rc, cps * 2 * S), rev),
                  pl.BlockSpec((cps, SSM_LANES, 2 * S), lambda g, c: (g, 0, 0)),
                  pl.BlockSpec((cps, 2 * S, SSM_LANES), lambda g, c: (g, 0, 0)),
                  pl.BlockSpec((1, cps * SSM_LANES), lambda g, c: (0, g)),
                  pl.BlockSpec((cps, 8, 4 * S), lambda g, c: (g, 0, 0))],
        out_specs=[pl.BlockSpec((rc, cps * SSM_LANES), rev),
                   pl.BlockSpec((cps, SSM_LANES, 2 * S), lambda g, c: (g, 0, 0)),
                   pl.BlockSpec((cps, 2 * S, SSM_LANES), lambda g, c: (g, 0, 0)),
                   pl.BlockSpec((cps, 1, 2 * S), lambda g, c: (g, 0, 0)),
                   pl.BlockSpec((1, cps * SSM_LANES), lambda g, c: (0, g))],
        out_shape=[jax.ShapeDtypeStruct((T, W), F32),
                   jax.ShapeDtypeStruct((ncol, SSM_LANES, 2 * S), F32),
                   jax.ShapeDtypeStruct((ncol, 2 * S, SSM_LANES), F32),
                   jax.ShapeDtypeStruct((ncol, 1, 2 * S), F32),
                   jax.ShapeDtypeStruct((1, W), F32)],
        scratch_shapes=[pltpu.VMEM((rc, cps * 2 * S), F32), pltpu.VMEM((8, cps * 2 * S), F32),
                        pltpu.VMEM((8, cps * 2 * S), F32), pltpu.VMEM((8, cps * SSM_LANES), F32)],
        compiler_params=_params(("parallel", "arbitrary")),
    )(dy, u, xs, bmat, cmat, dskip, tables)


def _ssm_matrices(a_re, a_im, log_step, b_re, b_im, c_re, c_im):
    G, N = a_re.shape
    ncol = G // GROUPS_PER_COL
    step = jnp.exp(log_step)[:, None]
    mag = jnp.exp(a_re * step)
    ang = a_im * step
    lam_re, lam_im = mag * jnp.cos(ang), mag * jnp.sin(ang)
    den = a_re * a_re + a_im * a_im
    nr, ni = lam_re - 1.0, lam_im
    coef_re = (nr * a_re + ni * a_im) / den
    coef_im = (ni * a_re - nr * a_im) / den
    bb_re = coef_re[..., None] * b_re - coef_im[..., None] * b_im
    bb_im = coef_re[..., None] * b_im + coef_im[..., None] * b_re
    eye = jnp.eye(GROUPS_PER_COL, dtype=F32)
    bb = jnp.stack([bb_re, bb_im]).reshape(2, ncol, GROUPS_PER_COL, N, SSM_GROUP)
    bmat = jnp.einsum("pbgnc,gh->bgcphn", bb, eye).reshape(ncol, SSM_LANES, 2 * STATE_LANES)
    cc = jnp.stack([c_re, -c_im]).reshape(2, ncol, GROUPS_PER_COL, SSM_GROUP, N)
    cmat = jnp.einsum("pbgcn,gh->bpgnhc", cc, eye).reshape(ncol, 2 * STATE_LANES, SSM_LANES)
    lam = jnp.concatenate([lam_re.reshape(ncol, STATE_LANES), lam_im.reshape(ncol, STATE_LANES)], axis=-1)
    return lam, bmat, cmat


def _scan_tables(lam, nbatch, conj):
    S = STATE_LANES
    lr, li = lam[:, None, 0:S], lam[:, None, S:2 * S]
    if conj:
        li = -li
    l2r, l2i = lr * lr - li * li, 2.0 * lr * li
    first = (jnp.arange(8) < nbatch)[None, :, None]
    zero = jnp.zeros_like(lr)
    if conj:
        parts = [jnp.where(first, lr, zero), jnp.where(first, li, zero), jnp.where(first, l2r, lr), jnp.where(first, l2i, li)]
    else:
        parts = [jnp.where(first, zero, lr), jnp.where(first, zero, li), jnp.where(first, lr, l2r), jnp.where(first, li, l2i)]
    return jnp.concatenate([jnp.broadcast_to(p, (lam.shape[0], 8, S)) for p in parts], axis=-1)


def _adamw_update(w_ref, g_ref, m_ref, v_ref, d_ref, nm_ref, nv_ref):
    gv = g_ref[...]
    mn = ADAM_B1 * m_ref[...] + (1.0 - ADAM_B1) * gv
    vn = ADAM_B2 * v_ref[...] + (1.0 - ADAM_B2) * (gv * gv)
    m_hat = mn / (1.0 - ADAM_B1 ** ADAM_STEP)
    v_hat = vn / (1.0 - ADAM_B2 ** ADAM_STEP)
    d_ref[...] = -ADAM_LR * (m_hat / (jnp.sqrt(v_hat) + ADAM_EPS) + ADAM_WD * w_ref[...])
    nm_ref[...] = mn
    nv_ref[...] = vn


def _adamw_small(ws, gs, ms, vs, name):
    n = len(ws)

    def body(*refs):
        for i in range(n):
            _adamw_update(refs[i], refs[n + i], refs[2 * n + i], refs[3 * n + i],
                          refs[4 * n + i], refs[5 * n + i], refs[6 * n + i])

    vm = pl.BlockSpec(memory_space=pltpu.VMEM)
    shapes = [jax.ShapeDtypeStruct(a.shape, F32) for a in ws]
    outs = pl.pallas_call(body, name=name, in_specs=[vm] * (4 * n), out_specs=[vm] * (3 * n), out_shape=shapes * 3,
                          compiler_params=pltpu.CompilerParams(vmem_limit_bytes=VMEM_LIMIT))(*ws, *gs, *ms, *vs)
    return outs[:n], outs[n:2 * n], outs[2 * n:]


def _adamw(w, g, m, v, name):
    R, C = w.shape[-2], w.shape[-1]
    tr = R if R <= 512 else _pick_tile(R, 512, 8)
    body = functools.partial(_adamw_update)

    def spec_for(a):
        if len(a.shape) == 2:
            return pl.BlockSpec((tr, C), lambda i: (i, 0))
        return pl.BlockSpec((None, tr, C), lambda i: (0, i, 0))

    spec = spec_for(w)
    shp = jax.ShapeDtypeStruct(w.shape, F32)
    return pl.pallas_call(body, name=name, grid=(R // tr,), in_specs=[spec, spec_for(g), spec, spec], out_specs=[spec] * 3,
                          out_shape=[shp, shp, shp], compiler_params=_params(("parallel",)))(w, g, m, v)


_ANY = pl.BlockSpec(memory_space=pl.ANY)


def _place():
    x, y, c = lax.axis_index("x"), lax.axis_index("y"), lax.axis_index("c")
    chips = [(1 - x, y), (x, 1 - y), (1 - x, 1 - y)]
    return x, y, c, chips


def _remote(src, dst, send_sems, recv_sems, k, to):
    return pltpu.make_async_remote_copy(src_ref=src, dst_ref=dst, send_sem=send_sems.at[k], recv_sem=recv_sems.at[k],
                                        device_id=to, device_id_type=MESH_IDS)


class _Riders:
    def __init__(self, srcs, out_shapes, n_sems, copies):
        self.srcs, self.out_shapes, self.n_sems, self.copies = list(srcs), list(out_shapes), n_sems, copies


def _call(body, name, grid, in_specs, out_specs, out_shape, scratch_shapes, sem, args, riders=None):
    if riders is None:
        return pl.pallas_call(body, name=name, grid=grid, in_specs=in_specs, out_specs=out_specs, out_shape=out_shape,
                              scratch_shapes=scratch_shapes, compiler_params=_params(sem))(*args)
    n_in, n_out, n_scr = len(in_specs), len(out_specs), len(scratch_shapes)
    r_in, r_out = len(riders.srcs), len(riders.out_shapes)

    def carrying(*refs):
        a, b = n_in, n_in + r_in
        c, d = b + n_out, b + n_out + r_out
        e = d + n_scr
        sends, arrivals = riders.copies(refs[a:b], refs[c:d], refs[e], refs[e + 1])
        first, last = None, None
        for ax, size in enumerate(grid):
            at0, at1 = pl.program_id(ax) == 0, pl.program_id(ax) == size - 1
            first = at0 if first is None else first & at0
            last = at1 if last is None else last & at1

        @pl.when(first)
        def _():
            for cp in sends:
                cp.start()

        body(*refs[:a], *refs[b:c], *refs[d:e])

        @pl.when(last)
        def _():
            for cp in arrivals:
                cp.wait_recv()
            for cp in sends:
                cp.wait_send()

    outs = pl.pallas_call(
        carrying, name=name, grid=grid, in_specs=list(in_specs) + [_ANY] * r_in,
        out_specs=list(out_specs) + [_ANY] * r_out, out_shape=list(out_shape) + riders.out_shapes,
        scratch_shapes=list(scratch_shapes) + [pltpu.SemaphoreType.DMA((riders.n_sems,)),
                                               pltpu.SemaphoreType.DMA((riders.n_sems,))],
        compiler_params=pltpu.CompilerParams(dimension_semantics=("arbitrary",) * len(grid),
                                             vmem_limit_bytes=VMEM_LIMIT, has_side_effects=True),
    )(*args, *riders.srcs)
    return outs[:n_out], outs[n_out:]


def _gather_riders(shards):
    def copies(srcs, outs, send_sems, recv_sems):
        x, y, c, chips = _place()
        sends, arrivals = [], []
        for i, s in enumerate(shards):
            half = s.shape[0] // 2
            rows = pl.ds(c * half, half)
            for j, chip in enumerate(chips):
                sends.append(_remote(srcs[i].at[rows, :], outs[i].at[2 * x + y, rows, :], send_sems, recv_sems,
                                     3 * i + j, (*chip, c)))
                landed = outs[i].at[2 * chip[0] + chip[1], rows, :]
                arrivals.append(_remote(landed, landed, send_sems, recv_sems, 3 * i + j, (*chip, c)))
        return sends, arrivals

    return _Riders(shards, [jax.ShapeDtypeStruct((N_CHIPS,) + s.shape, s.dtype) for s in shards], 3 * len(shards), copies)


def _exchange_riders(parts):
    def copies(srcs, outs, send_sems, recv_sems):
        x, y, c, chips = _place()
        sends = [_remote(srcs[i].at[2 * chip[0] + chip[1]], outs[i].at[j], send_sems, recv_sems, 3 * i + j, (*chip, c))
                 for i in range(len(parts)) for j, chip in enumerate(chips)]
        return sends, sends

    return _Riders(parts, [jax.ShapeDtypeStruct((3,) + p.shape[1:], p.dtype) for p in parts], 3 * len(parts), copies)


def _swap_riders(grads):
    def copies(srcs, outs, send_sems, recv_sems):
        x, y, c, _ = _place()
        sends = []
        for i, g in enumerate(grads):
            half = g.shape[1] // 2
            sends.append(_remote(srcs[i].at[:, pl.ds((1 - c) * half, half), :], outs[i], send_sems, recv_sems, i,
                                 (x, y, 1 - c)))
        return sends, sends

    return _Riders(grads, [jax.ShapeDtypeStruct((N_CHIPS, g.shape[1] // 2, g.shape[2]), g.dtype) for g in grads],
                   len(grads), copies)


def _forward_halves(gathered, shards, tag):
    n = len(gathered)

    def body(*refs):
        srcs, outs = refs[:n], refs[n:2 * n]
        send_sems, recv_sems = refs[2 * n:]
        x, y, c, chips = _place()
        sibling = (x, y, 1 - c)
        cps = []
        for i in range(n):
            half = gathered[i].shape[1] // 2
            for j, chip in enumerate(chips):
                slot = 2 * chip[0] + chip[1]
                cps.append(_remote(srcs[i].at[slot, pl.ds(c * half, half), :], outs[i].at[slot, pl.ds(c * half, half), :],
                                   send_sems, recv_sems, 3 * i + j, sibling))
        for cp in cps:
            cp.start()
        for i in range(n):
            half = gathered[i].shape[1] // 2
            for j, chip in enumerate(chips):
                theirs = outs[i].at[2 * chip[0] + chip[1], pl.ds((1 - c) * half, half), :]
                _remote(theirs, theirs, send_sems, recv_sems, 3 * i + j, sibling).wait_recv()
        for cp in cps:
            cp.wait_send()

    outs = pl.pallas_call(
        body, name=f"gather_forward_{tag}", in_specs=[_ANY] * n, out_specs=[_ANY] * n,
        out_shape=[jax.ShapeDtypeStruct(g.shape, g.dtype) for g in gathered],
        input_output_aliases={i: i for i in range(n)},
        scratch_shapes=[pltpu.SemaphoreType.DMA((3 * n,)), pltpu.SemaphoreType.DMA((3 * n,))],
        compiler_params=pltpu.CompilerParams(has_side_effects=True),
    )(*gathered)
    slot = 2 * lax.axis_index("x") + lax.axis_index("y")
    return [lax.dynamic_update_slice(o, s[None], (slot, 0, 0)) for o, s in zip(outs, shards)]


def _gather_weights(shards):
    n = len(shards)

    def body(*refs):
        srcs, outs = refs[:n], refs[n:2 * n]
        send_sems, recv_sems = refs[2 * n:]
        x, y, c, chips = _place()
        sibling = (x, y, 1 - c)

        def piece(i, px, py, pc):
            half = shards[i].shape[0] // 2
            return outs[i].at[2 * px + py, pl.ds(pc * half, half), :]

        first = []
        for i in range(n):
            half = shards[i].shape[0] // 2
            for j, chip in enumerate(chips):
                first.append(_remote(srcs[i].at[pl.ds(c * half, half), :], piece(i, x, y, c), send_sems, recv_sems,
                                     6 * i + j, (*chip, c)))
        for cp in first:
            cp.start()
        passed = []
        for i in range(n):
            for j, chip in enumerate(chips):
                _remote(piece(i, *chip, c), piece(i, *chip, c), send_sems, recv_sems, 6 * i + j, (*chip, c)).wait_recv()
                cp = _remote(piece(i, *chip, c), piece(i, *chip, c), send_sems, recv_sems, 6 * i + 3 + j, sibling)
                cp.start()
                passed.append(cp)
        for i in range(n):
            for j, chip in enumerate(chips):
                _remote(piece(i, *chip, 1 - c), piece(i, *chip, 1 - c), send_sems, recv_sems, 6 * i + 3 + j,
                        sibling).wait_recv()
        for cp in first + passed:
            cp.wait_send()

    outs = pl.pallas_call(
        body, name="gather_weights", in_specs=[_ANY] * n, out_specs=[_ANY] * n,
        out_shape=[jax.ShapeDtypeStruct((N_CHIPS,) + s.shape, s.dtype) for s in shards],
        scratch_shapes=[pltpu.SemaphoreType.DMA((6 * n,)), pltpu.SemaphoreType.DMA((6 * n,))],
        compiler_params=pltpu.CompilerParams(has_side_effects=True),
    )(*shards)
    slot = 2 * lax.axis_index("x") + lax.axis_index("y")
    return [lax.dynamic_update_slice(o, s[None], (slot, 0, 0)) for o, s in zip(outs, shards)]


def _swap_halves(grads, tag):
    n = len(grads)

    def body(*refs):
        srcs, outs = refs[:n], refs[n:2 * n]
        send_sems, recv_sems = refs[2 * n:]
        x, y, c, _ = _place()
        cps = []
        for i in range(n):
            half = grads[i].shape[1] // 2
            cps.append(_remote(srcs[i].at[:, pl.ds((1 - c) * half, half), :], outs[i], send_sems, recv_sems, i, (x, y, 1 - c)))
        for cp in cps:
            cp.start()
        for cp in cps:
            cp.wait()

    return pl.pallas_call(
        body, name=f"grad_swap_halves_{tag}", in_specs=[_ANY] * n, out_specs=[_ANY] * n,
        out_shape=[jax.ShapeDtypeStruct((N_CHIPS, g.shape[1] // 2, g.shape[2]), g.dtype) for g in grads],
        scratch_shapes=[pltpu.SemaphoreType.DMA((n,)), pltpu.SemaphoreType.DMA((n,))],
        compiler_params=pltpu.CompilerParams(has_side_effects=True),
    )(*grads)


def _join_halves(fulls):
    n = len(fulls)

    def body(*refs):
        srcs, outs = refs[:n], refs[n:2 * n]
        send_sems, recv_sems = refs[2 * n:]
        x, y, c, _ = _place()
        sibling = (x, y, 1 - c)
        cps = []
        for i in range(n):
            h = fulls[i].shape[0] // 2
            cps.append(_remote(srcs[i].at[pl.ds(c * h, h), :], outs[i].at[pl.ds(c * h, h), :], send_sems, recv_sems, i,
                               sibling))
        for cp in cps:
            cp.start()
        for i in range(n):
            h = fulls[i].shape[0] // 2
            theirs = outs[i].at[pl.ds((1 - c) * h, h), :]
            _remote(theirs, theirs, send_sems, recv_sems, i, sibling).wait_recv()
        for cp in cps:
            cp.wait_send()

    return pl.pallas_call(
        body, name="grad_join_halves", in_specs=[_ANY] * n, out_specs=[_ANY] * n,
        out_shape=[jax.ShapeDtypeStruct(f.shape, f.dtype) for f in fulls],
        input_output_aliases={i: i for i in range(n)},
        scratch_shapes=[pltpu.SemaphoreType.DMA((n,)), pltpu.SemaphoreType.DMA((n,))],
        compiler_params=pltpu.CompilerParams(has_side_effects=True),
    )(*fulls)


def _half_tile(h):
    return h if h <= 512 else _pick_tile(h, 512, ROW_ALIGN)


def _sum_halves(g, r1, c_idx, name):
    _, R, C = g.shape
    H = R // 2
    tr = _half_tile(H)
    nblk = H // tr

    def body(c_ref, g_ref, r_ref, p_ref):
        p_ref[...] = (g_ref[...] + r_ref[...]).astype(BF16)

    half = pl.BlockSpec((None, tr, C), lambda s, i, c_ref: (s, c_ref[0] * nblk + i, 0))
    plain = pl.BlockSpec((None, tr, C), lambda s, i, c_ref: (s, i, 0))
    return pl.pallas_call(
        body, name=name,
        grid_spec=pltpu.PrefetchScalarGridSpec(num_scalar_prefetch=1, grid=(N_CHIPS, nblk), in_specs=[half, plain],
                                               out_specs=plain),
        out_shape=jax.ShapeDtypeStruct((N_CHIPS, H, C), BF16),
        compiler_params=_params(("parallel", "parallel")),
    )(c_idx, g, r1)


def _sum_chips(g, r1, r2, idx, name):
    _, R, C = g.shape
    H = R // 2
    tr = _half_tile(H)
    nblk = H // tr

    def body(idx_ref, g_ref, r1_ref, r2_ref, o_ref):
        o_ref[...] = (((g_ref[...] + r1_ref[...]) + r2_ref[0].astype(F32)) + r2_ref[1].astype(F32)) + r2_ref[2].astype(F32)

    return pl.pallas_call(
        body, name=name,
        grid_spec=pltpu.PrefetchScalarGridSpec(
            num_scalar_prefetch=1, grid=(nblk,),
            in_specs=[pl.BlockSpec((None, tr, C), lambda i, idx_ref: (idx_ref[0], idx_ref[1] * nblk + i, 0)),
                      pl.BlockSpec((None, tr, C), lambda i, idx_ref: (idx_ref[0], i, 0)),
                      pl.BlockSpec((3, tr, C), lambda i, idx_ref: (0, i, 0))],
            out_specs=pl.BlockSpec((tr, C), lambda i, idx_ref: (idx_ref[1] * nblk + i, 0))),
        out_shape=jax.ShapeDtypeStruct((R, C), F32),
        compiler_params=_params(("parallel",)),
    )(idx, g, r1, r2)


def _all_reduce_small(v, n_fold, fold_rows, fold_at):
    M, N = v.shape

    def body(x_ref, tot_ref, fold_ref, all_ref, send_sems, recv_sems, local_sem):
        x, y, c, chips = _place()
        me, sibling = (x, y, c), (x, y, 1 - c)

        def rows(px, py, pc):
            return all_ref.at[pl.ds((4 * px + 2 * py + pc) * M, M), :]

        def copy(k, block, to, src=None):
            return _remote(rows(*block) if src is None else src, rows(*block), send_sems, recv_sems, k, to)

        mine = pltpu.make_async_copy(x_ref, rows(*me), local_sem)
        mine.start()
        first = [copy(0, me, sibling, src=x_ref)]
        first += [copy(1 + j, me, (*chip, c), src=x_ref) for j, chip in enumerate(chips)]
        for cp in first:
            cp.start()
        passed = [copy(4 + j, (*chip, c), sibling) for j, chip in enumerate(chips)]
        for j, chip in enumerate(chips):
            copy(1 + j, (*chip, c), me).wait_recv()
            passed[j].start()
        copy(0, sibling, me).wait_recv()
        for j, chip in enumerate(chips):
            copy(4 + j, (*chip, 1 - c), me).wait_recv()
        for cp in first + passed:
            cp.wait_send()
        mine.wait()
        tot = all_ref[0:M, :]
        for d in range(1, 8):
            tot = tot + all_ref[d * M:(d + 1) * M, :]
        tot_ref[...] = tot
        f = tot[fold_at:fold_at + fold_rows, :]
        for e in range(1, n_fold):
            f = f + tot[fold_at + e * fold_rows:fold_at + (e + 1) * fold_rows, :]
        fold_ref[...] = f

    vm = pl.BlockSpec(memory_space=pltpu.VMEM)
    return pl.pallas_call(
        body, name="all_reduce_small", in_specs=[vm], out_specs=[vm, vm],
        out_shape=[jax.ShapeDtypeStruct((M, N), F32), jax.ShapeDtypeStruct((fold_rows, N), F32)],
        scratch_shapes=[pltpu.VMEM((8 * M, N), F32), pltpu.SemaphoreType.DMA((7,)), pltpu.SemaphoreType.DMA((7,)),
                        pltpu.SemaphoreType.DMA],
        compiler_params=pltpu.CompilerParams(has_side_effects=True, vmem_limit_bytes=VMEM_LIMIT),
    )(v)


def _as_rows(a, width):
    flat = a.reshape(-1)
    pad = (-flat.shape[0]) % width
    if pad:
        flat = jnp.concatenate([flat, jnp.zeros((pad,), flat.dtype)])
    return flat.reshape(-1, width)


class _Layout:
    def __init__(self, width, total_mult):
        self.width, self.total_mult = width, total_mult
        self.offsets, self.shapes, self.rows = {}, {}, 0

    def add(self, name, shape):
        r = -(-math.prod(shape) // self.width)
        self.offsets[name], self.shapes[name] = (self.rows, r), tuple(shape)
        self.rows += r

    def align(self, mult):
        gap = (-self.rows) % mult
        if gap:
            self.offsets[f"_gap{self.rows}"], self.shapes[f"_gap{self.rows}"] = (self.rows, gap), (gap, self.width)
            self.rows += gap
        return self.rows

    def pack(self, pieces):
        self.align(self.total_mult)
        parts = [_as_rows(pieces[n].astype(F32), self.width) if n in pieces else jnp.zeros(self.shapes[n], F32)
                 for n in self.offsets]
        return jnp.concatenate(parts, axis=0)

    def unpack(self, buf, name):
        off, r = self.offsets[name]
        shape = self.shapes[name]
        return buf[off:off + r].reshape(-1)[:math.prod(shape)].reshape(shape)


_BIG = ["ffn1_w1", "ffn1_w3", "ffn1_w2", "w_in", "ssm_glu_a", "ssm_glu_b", "w_out", "ffn2_w1", "ffn2_w3", "ffn2_w2"]
_TRANSPOSED = {"ffn1_w1", "ffn1_w3", "ffn2_w1", "ffn2_w3"}
_SMALL = ["ffn1_norm", "mix_norm", "ffn2_norm", "final_norm", "attn_sinks", "ssm_a_re", "ssm_a_im", "ssm_log_step",
          "ssm_b_re", "ssm_b_im", "ssm_c_re", "ssm_c_im", "ssm_d"]
_WEIGHTS = ["meta_tokens", "ffn1_norm", "ffn1_w1", "ffn1_w3", "ffn1_w2", "mix_norm", "w_in", "attn_sinks", "ssm_a_re",
            "ssm_a_im", "ssm_log_step", "ssm_b_re", "ssm_b_im", "ssm_c_re", "ssm_c_im", "ssm_d", "ssm_glu_a",
            "ssm_glu_b", "w_out", "ffn2_norm", "ffn2_w1", "ffn2_w3", "ffn2_w2", "final_norm"]


def _kv_interleave(w, kv_heads):
    kvw = kv_heads * HEAD_DIM
    lead = w.shape[:-1]
    k = w[..., 0:kvw].reshape(lead + (kv_heads, 1, HEAD_DIM))
    v = w[..., kvw:2 * kvw].reshape(lead + (kv_heads, 1, HEAD_DIM))
    return jnp.concatenate([jnp.concatenate([k, v], axis=-2).reshape(lead + (2 * kvw,)), w[..., 2 * kvw:]], axis=-1)


def _kv_deinterleave(w, kv_heads):
    kvw = kv_heads * HEAD_DIM
    lead = w.shape[:-1]
    kv = w[..., 0:2 * kvw].reshape(lead + (kv_heads, 2, HEAD_DIM))
    return jnp.concatenate([kv[..., 0, :].reshape(lead + (kvw,)), kv[..., 1, :].reshape(lead + (kvw,)), w[..., 2 * kvw:]],
                           axis=-1)


def _step(x, target, w, m, v):
    B, S, D = x.shape
    L = S + N_META
    T = B * L
    H = D // HEAD_DIM
    KV = H // Q_PER_KV
    SW = D // 2
    tm = _pick_tile(L, ROW_TILE_CAP, ROW_ALIGN)
    rc = _pick_tile(L, ROW_TILE_CAP // B, 4) * B
    tw = _pick_tile(T, 3 * ROW_TILE_CAP, ROW_ALIGN)
    my_c = lax.axis_index("c")
    my_slot = 2 * lax.axis_index("x") + lax.axis_index("y")

    groups = {"ffn1": ["ffn1_w1", "ffn1_w3", "ffn1_w2"], "mix": ["w_in", "ssm_glu_a", "ssm_glu_b", "w_out"],
              "ffn2": ["ffn2_w1", "ffn2_w3", "ffn2_w2"]}
    waves = {"first": ["ffn1_w1", "ffn1_w3"], "early": ["ffn1_w2"] + groups["mix"], "late": groups["ffn2"]}
    def own_layout(a, n):
        return jnp.swapaxes(a[0], 0, 1) if n in _TRANSPOSED else a[0]

    shards = {n: own_layout(w[n], n).astype(BF16) for n in _BIG}
    gathered = _gather_weights([shards[n] for n in waves["first"]] + [w["meta_tokens"]])
    ws = dict(zip(waves["first"], gathered[:-1]))
    meta = jnp.transpose(gathered[-1], (1, 0, 2)).reshape(N_META, D)

    def arrive(wave, landed):
        mine = [shards[n] for n in waves[wave]]
        ws.update(zip(waves[wave], _forward_halves(landed, mine, wave)))

    g_ffn1, g_mix, g_ffn2 = w["ffn1_norm"], w["mix_norm"], w["ffn2_norm"]
    g_final = w["final_norm"].reshape(1, D)

    h0 = jnp.concatenate([jnp.broadcast_to(meta[None], (B, N_META, D)), x], axis=1).reshape(T, D)

    def ffn_fwd(h, g, tag, carry=None):
        n = _rmsnorm_fwd(h, g, tm, f"{tag}_norm")
        riders = None if carry is None else _gather_riders([shards[k] for k in waves[carry]])
        out = _ffn_up(n, ws[f"{tag}_w1"], ws[f"{tag}_w3"], tm, f"{tag}_up", riders)
        if carry is not None:
            out, landed = out
            arrive(carry, landed)
        a, c, s = out
        return _ffn_down(s, ws[f"{tag}_w2"], h, tm, f"{tag}_down"), (n, a, c, s)

    h1, saved1 = ffn_fwd(h0, g_ffn1, "ffn1", carry="early")
    w_kvu = _kv_interleave(ws["w_in"][1], KV)
    hn = _rmsnorm_fwd(h1, g_mix, tm, "mix_norm")
    q = _mm_colslots(hn, ws["w_in"], BF16, "w_in_q", tm, first=0, count=1, scale=HEAD_DIM ** -0.5)
    kvu = _mm_plain(hn, w_kvu, "nn", F32, "w_in_kvu", tm)
    gates = _mm_colslots(hn, ws["w_in"], F32, "w_in_gates", tm, first=2, count=2)

    sinks = w["attn_sinks"].reshape(KV, Q_PER_KV, 1, 1)
    sink_row = jnp.broadcast_to(sinks.reshape(KV, 1, Q_PER_KV, 1), (KV, 1, Q_PER_KV, BLOCK)).reshape(KV, 1, Q_PER_KV * BLOCK)
    sink_meta = jnp.broadcast_to(sinks, (KV, Q_PER_KV, N_META, 1)).reshape(KV, Q_PER_KV * N_META, 1)
    (attn,), landed = _attn_fwd(q, kvu, sink_row, sink_meta, B, "attn_fwd",
                                _gather_riders([shards[k] for k in waves["late"]]))
    arrive("late", landed)

    def to_time_major(a2d):
        return jnp.transpose(a2d.reshape(B, L, a2d.shape[-1]), (1, 0, 2)).reshape(T, a2d.shape[-1])

    def to_batch_major(a2d):
        return jnp.transpose(a2d.reshape(L, B, a2d.shape[-1]), (1, 0, 2)).reshape(T, a2d.shape[-1])

    ssm_args = (w["ssm_a_re"][0], w["ssm_a_im"][0], w["ssm_log_step"][0], w["ssm_b_re"][0], w["ssm_b_im"][0],
                w["ssm_c_re"][0], w["ssm_c_im"][0])
    (lam, bmat, cmat), ssm_vjp = jax.vjp(_ssm_matrices, *ssm_args)
    bmat16, cmat16 = bmat.astype(BF16), cmat.astype(BF16)
    u_t = to_time_major(kvu[:, SW:])
    y_t, xs = _ssm_fwd(u_t, bmat16, cmat16, w["ssm_d"], _scan_tables(lam, B, False), B, rc, "ssm_fwd")
    y0 = to_batch_major(y_t)
    yg = _gelu_fwd(y0, tm, "gelu_fwd")
    ga = _mm_colslots(yg, ws["ssm_glu_a"], F32, "glu_a", tm)
    gb = _mm_colslots(yg, ws["ssm_glu_b"], F32, "glu_b", tm)
    merged = _merge_fwd(gates, attn, ga, gb, tm, "merge_fwd")
    h2 = _mm_rowslots(merged, ws["w_out"], h1, tm, "w_out")
    h3, saved2 = ffn_fwd(h2, g_ffn2, "ffn2")
    dh3, dh3b, dg_final, loss_row = _loss_head(h3, g_final, target, tm, "loss_head")

    grads, swapped, received = {}, {}, {}
    c_idx = my_c.reshape(1).astype(jnp.int32)
    idx = jnp.stack([my_slot, my_c]).astype(jnp.int32)

    def swap_riders(group):
        return _swap_riders([grads[n] for n in groups[group]])

    def exchange_riders(group):
        names = groups[group]
        if names[0] not in swapped:
            swapped.update(zip(names, _swap_halves([grads[n] for n in names], group)))
        return _exchange_riders([_sum_halves(grads[n], swapped[n], c_idx, f"grad_sum_halves_{n}") for n in names])

    def ffn_bwd(h, g, saved, dh, dhb, tag, dhidden_carries=None, dn_carries=None):
        n, a, c, s = saved
        w1, w3, w2 = ws[f"{tag}_w1"], ws[f"{tag}_w3"], ws[f"{tag}_w2"]
        grads[f"{tag}_w2"] = _wgrad_hidden_rows(s, dhb, tw, f"{tag}_dw2", 0.5)
        if dhidden_carries is None:
            da, dc = _ffn_dhidden(dhb, w2, a, c, tm, f"{tag}_dhidden")
        else:
            (da, dc), got = _ffn_dhidden(dhb, w2, a, c, tm, f"{tag}_dhidden", exchange_riders(dhidden_carries[1]))
            received.update(zip(groups[dhidden_carries[1]], got))
        grads[f"{tag}_w1"] = _wgrad_hidden_rows(da, n, tw, f"{tag}_dw1", 1.0)
        grads[f"{tag}_w3"] = _wgrad_hidden_rows(dc, n, tw, f"{tag}_dw3", 1.0)
        kind, group = dn_carries
        riders = swap_riders(group) if kind == "swap" else exchange_riders(group)
        (dh_in, dhb_in, grads[f"{tag}_norm"]), got = _ffn_dn(da, w1, dc, w3, h, g, dh, tm, f"{tag}_dn", riders)
        return dh_in, dhb_in, got

    dh2, dh2b, got = ffn_bwd(h2, g_ffn2, saved2, dh3, dh3b, "ffn2", dn_carries=("swap", "ffn2"))
    swapped.update(zip(groups["ffn2"], got))

    grads["w_out"] = _wgrad_rowslots(merged, dh2b, tw, "dw_out")
    dattn, dgat, dgss, dga, dgb = _merge_bwd(dh2b, ws["w_out"], gates, attn, ga, gb, tm, "merge_bwd")
    grads["ssm_glu_a"] = _wgrad_colslots(yg, dga, tw, "dglu_a")
    grads["ssm_glu_b"] = _wgrad_colslots(yg, dgb, tw, "dglu_b")
    dy0 = _gelu_bwd([(dga, ws["ssm_glu_a"]), (dgb, ws["ssm_glu_b"])], y0, tm, "gelu_bwd")
    du_t, dbmat, dcmat, dlam, dd = _ssm_bwd(to_time_major(dy0), u_t, xs, bmat16, cmat16, w["ssm_d"],
                                            _scan_tables(lam, B, True), B, rc, "ssm_bwd")
    d_ssm = ssm_vjp((dlam[:, 0, :], dbmat, dcmat))
    for n, gval in zip(["ssm_a_re", "ssm_a_im", "ssm_log_step", "ssm_b_re", "ssm_b_im", "ssm_c_re", "ssm_c_im"], d_ssm):
        grads[n] = gval[None]
    grads["ssm_d"] = dd

    (dq, dkv, dsink), got = _attn_bwd(q, kvu, attn, dattn, sink_row, sink_meta, B, "attn_bwd",
                                      exchange_riders("ffn2"))
    received.update(zip(groups["ffn2"], got))
    grads["attn_sinks"] = dsink[:, 0:Q_PER_KV, 0].reshape(1, H)
    dkvu = jnp.concatenate([dkv, to_batch_major(du_t).astype(BF16)], axis=1)
    pieces = [dq, dkvu, dgat, dgss]
    dw_in = [_wgrad_plain(hn, p, f"dw_in_{k}", tw) for k, p in enumerate(pieces)]
    dw_in[1] = _kv_deinterleave(dw_in[1], KV)
    grads["w_in"] = jnp.stack(dw_in)
    w_in_parts = [ws["w_in"][0], w_kvu, ws["w_in"][2], ws["w_in"][3]]
    whole = _once((D, D), lambda i: (0, 0))
    (dh1, dh1b, grads["mix_norm"]), swap_mix = _mm_norm_bwd(
        "dhn", "nt", [(p, _spec((tm, D), lambda i: (i, 0)), wp, whole) for p, wp in zip(pieces, w_in_parts)],
        h1, g_mix, dh2, tm, swap_riders("mix"))
    swapped.update(zip(groups["mix"], swap_mix))
    dh0, _, got = ffn_bwd(h0, g_ffn1, saved1, dh1, dh1b, "ffn1", dhidden_carries=("exchange", "mix"),
                          dn_carries=("exchange", "ffn1"))
    received.update(zip(groups["ffn1"], got))
    dh0 = dh0.reshape(B, L, D)
    grad_x = dh0[:, N_META:, :]

    grads["final_norm"] = dg_final
    slay = _Layout(D, 8)
    for n in _SMALL:
        slay.add(n, w[n].shape)
    slay.add("loss", (1, D))
    meta_at = slay.align(8)
    slay.add("meta", (B * N_META, D))
    small = slay.pack({**{n: grads[n] for n in _SMALL}, "loss": loss_row, "meta": dh0[:, :N_META, :]})
    tot_small, dmeta = _all_reduce_small(small, B, N_META, meta_at)
    loss = slay.unpack(tot_small, "loss")[0, 0]
    for n in _SMALL:
        grads[n] = slay.unpack(tot_small, n)
    cw = D // N_CHIPS
    grads["meta_tokens"] = lax.dynamic_slice_in_dim(dmeta, my_slot * cw, cw, axis=1)

    fulls = [_sum_chips(grads[n], swapped[n], received[n], idx, f"grad_sum_chips_{n}") for n in _BIG]
    for n, f in zip(_BIG, _join_halves(fulls)):
        grads[n] = f

    delta, new_m, new_v = {}, {}, {}
    for n in _BIG + ["meta_tokens"]:
        if n in _TRANSPOSED:
            flip = lambda a: jnp.swapaxes(a, -1, -2)
            outs = _adamw(flip(w[n]), grads[n], flip(m[n]), flip(v[n]), f"adamw_{n}")
            delta[n], new_m[n], new_v[n] = (flip(o) for o in outs)
            grads[n] = flip(grads[n])[None]
        else:
            delta[n], new_m[n], new_v[n] = _adamw(w[n], grads[n], m[n], v[n], f"adamw_{n}")
            grads[n] = grads[n].reshape(w[n].shape)

    def flat2d(a):
        return a.reshape(-1, a.shape[-1])

    d_, m_, v_ = _adamw_small([flat2d(w[n]) for n in _SMALL], [flat2d(grads[n]) for n in _SMALL],
                              [flat2d(m[n]) for n in _SMALL], [flat2d(v[n]) for n in _SMALL], "adamw_small")
    for i, n in enumerate(_SMALL):
        shp = w[n].shape
        delta[n], new_m[n], new_v[n] = d_[i].reshape(shp), m_[i].reshape(shp), v_[i].reshape(shp)
        grads[n] = grads[n].reshape(shp)

    return (loss, grad_x, *[grads[n] for n in _WEIGHTS], *[delta[n] for n in _WEIGHTS],
            *[new_m[n] for n in _WEIGHTS], *[new_v[n] for n in _WEIGHTS])


def kernel(x, meta_tokens, ffn1_norm, ffn1_w1, ffn1_w3, ffn1_w2, mix_norm, w_in, attn_sinks, ssm_a_re, ssm_a_im, ssm_log_step, ssm_b_re, ssm_b_im, ssm_c_re, ssm_c_im, ssm_d, ssm_glu_a, ssm_glu_b, w_out, ffn2_norm, ffn2_w1, ffn2_w3, ffn2_w2, final_norm, loss_target, m_meta_tokens, m_ffn1_norm, m_ffn1_w1, m_ffn1_w3, m_ffn1_w2, m_mix_norm, m_w_in, m_attn_sinks, m_ssm_a_re, m_ssm_a_im, m_ssm_log_step, m_ssm_b_re, m_ssm_b_im, m_ssm_c_re, m_ssm_c_im, m_ssm_d, m_ssm_glu_a, m_ssm_glu_b, m_w_out, m_ffn2_norm, m_ffn2_w1, m_ffn2_w3, m_ffn2_w2, m_final_norm, v_meta_tokens, v_ffn1_norm, v_ffn1_w1, v_ffn1_w3, v_ffn1_w2, v_mix_norm, v_w_in, v_attn_sinks, v_ssm_a_re, v_ssm_a_im, v_ssm_log_step, v_ssm_b_re, v_ssm_b_im, v_ssm_c_re, v_ssm_c_im, v_ssm_d, v_ssm_glu_a, v_ssm_glu_b, v_w_out, v_ffn2_norm, v_ffn2_w1, v_ffn2_w3, v_ffn2_w2, v_final_norm):
    args = locals()
    w = {n: args[n] for n in _WEIGHTS}
    m = {n: args["m_" + n] for n in _WEIGHTS}
    v = {n: args["v_" + n] for n in _WEIGHTS}
    return _step(x, loss_target, w, m, v)
```

```python
import functools
import math

import jax
import jax.numpy as jnp
from jax import lax
from jax.experimental import pallas as pl
from jax.experimental.pallas import tpu as pltpu

F32 = jnp.float32
BF16 = jnp.bfloat16
MESH_IDS = pl.DeviceIdType.MESH

N_CHIPS = 4
N_META = 16
HEAD_DIM = 64
Q_PER_KV = 4
QW = Q_PER_KV * HEAD_DIM
BLOCK = 128
SSM_GROUP = 16
SSM_STATE = 64
SSM_LANES = 128
GROUPS_PER_COL = SSM_LANES // SSM_GROUP
STATE_LANES = GROUPS_PER_COL * SSM_STATE
NORM_EPS = 1e-6
NEG_INF = -1e30
ADAM_LR, ADAM_B1, ADAM_B2, ADAM_EPS, ADAM_WD, ADAM_STEP = 0.001, 0.9, 0.999, 1e-08, 0.01, 10
GELU_C = math.sqrt(2.0 / math.pi)
ROW_ALIGN = 16
VMEM_LIMIT = 56 * 1024 * 1024
ROW_TILE_CAP = 688

_NN = (((1,), (0,)), ((), ()))
_NT = (((1,), (1,)), ((), ()))
_TN = (((0,), (0,)), ((), ()))
_DIMS = {"nn": _NN, "nt": _NT, "tn": _TN}


def _params(sem, **kw):
    return pltpu.CompilerParams(dimension_semantics=sem, vmem_limit_bytes=VMEM_LIMIT, **kw)


def _pick_tile(n, cap, mult):
    best = None
    for t in range(mult, min(n, cap) + 1, mult):
        if n % t == 0:
            best = t
    if best is None:
        raise ValueError(f"no tile for {n} (cap {cap}, multiple of {mult})")
    return best


def _sigmoid(x):
    return 0.5 * jnp.tanh(0.5 * x) + 0.5


def _spec(block, index_map):
    return pl.BlockSpec(block, index_map)


def _sum_dots(ins, mode):
    tot = None
    for p in range(len(ins) // 2):
        a_ref, b_ref = ins[2 * p], ins[2 * p + 1]
        for sl in ([None] if len(b_ref.shape) == 2 else range(b_ref.shape[0])):
            if sl is None:
                a, b = a_ref[...], b_ref[...]
            elif len(a_ref.shape) == 3:
                a, b = a_ref[sl], b_ref[sl]
            else:
                width = a_ref.shape[1] // b_ref.shape[0]
                a, b = a_ref[:, sl * width:(sl + 1) * width], b_ref[sl]
            d = lax.dot_general(a.astype(BF16), b.astype(BF16), _DIMS[mode], preferred_element_type=F32)
            tot = d if tot is None else tot + d
    return tot


def _mm(name, grid, kaxis, mode, pairs, out_shape, out_spec, scale=1.0, res=None):
    npairs = len(pairs)
    has_res = res is not None
    gk = 1 if kaxis is None else grid[kaxis]
    acc_shape = tuple(d for d in out_spec.block_shape if d is not None)

    def body(*refs):
        res_ref = refs[2 * npairs] if has_res else None
        o_ref = refs[2 * npairs + has_res]
        tot = _sum_dots(refs[:2 * npairs], mode)

        def finish(acc):
            r = acc * scale if scale != 1.0 else acc
            if has_res:
                r = res_ref[...] + r
            o_ref[...] = r.astype(o_ref.dtype)

        if gk == 1:
            finish(tot)
        else:
            acc_ref = refs[-1]
            k = pl.program_id(kaxis)

            @pl.when(k == 0)
            def _():
                acc_ref[...] = tot

            @pl.when(k > 0)
            def _():
                acc_ref[...] += tot

            @pl.when(k == gk - 1)
            def _():
                finish(acc_ref[...])

    in_specs, args = [], []
    for a, a_spec, b, b_spec in pairs:
        in_specs += [a_spec, b_spec]
        args += [a, b]
    if has_res:
        in_specs.append(res[1])
        args.append(res[0])
    sem = tuple("arbitrary" if ax == kaxis else "parallel" for ax in range(len(grid)))
    return pl.pallas_call(
        body, name=name, grid=grid, in_specs=in_specs, out_specs=out_spec, out_shape=out_shape,
        scratch_shapes=[pltpu.VMEM(acc_shape, F32)] if gk > 1 else [],
        compiler_params=_params(sem),
    )(*args)


def _mm_plain(a, b, mode, out_dtype, name, tm, scale=1.0):
    M, K = a.shape
    N = b.shape[1] if mode == "nn" else b.shape[0]
    return _mm(name, (M // tm,), None, mode,
               [(a, _spec((tm, K), lambda i: (i, 0)), b, _spec(b.shape, lambda i: (0, 0)))],
               jax.ShapeDtypeStruct((M, N), out_dtype), _spec((tm, N), lambda i: (i, 0)), scale=scale)


def _wgrad_plain(a, b, name, tr):
    R, M = a.shape
    N = b.shape[1]
    return _mm(name, (R // tr,), 0, "tn",
               [(a, _spec((tr, M), lambda r: (r, 0)), b, _spec((tr, N), lambda r: (r, 0)))],
               jax.ShapeDtypeStruct((M, N), F32), _spec((M, N), lambda r: (0, 0)))


def _rmsnorm_fwd(h, g, tm, name):
    T, D = h.shape

    def body(h_ref, g_ref, o_ref):
        x = h_ref[...]
        r = lax.rsqrt(jnp.mean(x * x, axis=-1, keepdims=True) + NORM_EPS)
        o_ref[...] = ((x * r) * g_ref[...]).astype(BF16)

    return pl.pallas_call(
        body, name=name, grid=(T // tm,),
        in_specs=[pl.BlockSpec((tm, D), lambda i: (i, 0)), pl.BlockSpec((1, D), lambda i: (0, 0))],
        out_specs=pl.BlockSpec((tm, D), lambda i: (i, 0)),
        out_shape=jax.ShapeDtypeStruct((T, D), BF16),
        compiler_params=_params(("parallel",)),
    )(h, g)


def _fold8(x):
    return jnp.sum(x.reshape(x.shape[0] // 8, 8, x.shape[1]), axis=0)


def _mm_norm_bwd(name, mode, pairs, h, g, dres, tm, riders=None):
    T, D = h.shape
    nt = T // tm
    npairs = len(pairs)

    def body(*refs):
        h_ref, g_ref, dres_ref, dh_ref, dhb_ref, dg_ref, acc_ref = refs[2 * npairs:]
        i = pl.program_id(0)
        x = h_ref[...]
        r = lax.rsqrt(jnp.mean(x * x, axis=-1, keepdims=True) + NORM_EPS)
        xhat = x * r
        dy = _sum_dots(refs[:2 * npairs], mode)
        dxhat = dy * g_ref[...]
        dx = r * (dxhat - xhat * jnp.mean(dxhat * xhat, axis=-1, keepdims=True))
        dh = dres_ref[...] + dx
        dh_ref[...] = dh
        dhb_ref[...] = dh.astype(BF16)
        part = _fold8(dy * xhat)

        @pl.when(i == 0)
        def _():
            acc_ref[...] = part

        @pl.when(i > 0)
        def _():
            acc_ref[...] += part

        @pl.when(i == nt - 1)
        def _():
            dg_ref[...] = jnp.sum(acc_ref[...], axis=0, keepdims=True)

    row = pl.BlockSpec((tm, D), lambda i: (i, 0))
    vec = pl.BlockSpec((1, D), lambda i: (0, 0))
    in_specs, args = [], []
    for a, a_spec, b, b_spec in pairs:
        in_specs += [a_spec, b_spec]
        args += [a, b]
    return _call(body, name, (nt,), in_specs + [row, vec, row], [row, row, vec],
                 [jax.ShapeDtypeStruct((T, D), F32), jax.ShapeDtypeStruct((T, D), BF16), jax.ShapeDtypeStruct((1, D), F32)],
                 [pltpu.VMEM((8, D), F32)], ("arbitrary",), (*args, h, g, dres), riders)


def _ffn_up(n, w1t, w3t, tm, name, riders=None):
    T, D = n.shape
    Fs = w1t.shape[1]

    def body(n_ref, w1_ref, w3_ref, a_ref, c_ref, s_ref):
        x = n_ref[...]
        a = lax.dot_general(x, w1_ref[...], _NT, preferred_element_type=F32)
        c = lax.dot_general(x, w3_ref[...], _NT, preferred_element_type=F32)
        a_ref[...] = a.astype(BF16)
        c_ref[...] = c.astype(BF16)
        s_ref[...] = (a * _sigmoid(a) * c).astype(BF16)

    w_spec = _spec((None, Fs, D), lambda s, i: (s, 0, 0))
    o_spec = _spec((None, tm, Fs), lambda s, i: (s, i, 0))
    o_shape = jax.ShapeDtypeStruct((N_CHIPS, T, Fs), BF16)
    return _call(body, name, (N_CHIPS, T // tm), [_spec((tm, D), lambda s, i: (i, 0)), w_spec, w_spec],
                 [o_spec, o_spec, o_spec], [o_shape, o_shape, o_shape], [], ("parallel", "parallel"), (n, w1t, w3t), riders)


def _ffn_down(s, w2, h, tm, name):
    _, T, Fs = s.shape
    D = w2.shape[2]
    row = _spec((tm, D), lambda i: (i, 0))
    return _mm(name, (T // tm,), None, "nn",
               [(s, _spec((N_CHIPS, tm, Fs), lambda i: (0, i, 0)), w2, _spec((N_CHIPS, Fs, D), lambda i: (0, 0, 0)))],
               jax.ShapeDtypeStruct((T, D), F32), row, scale=0.5, res=(h, row))


def _ffn_dhidden(dhb, w2, a, c, tm, name, riders=None):
    T, D = dhb.shape
    Fs = w2.shape[1]

    def body(dh_ref, w2_ref, a_ref, c_ref, da_ref, dc_ref):
        d = 0.5 * lax.dot_general(dh_ref[...], w2_ref[...], _NT, preferred_element_type=F32)
        av = a_ref[...].astype(F32)
        cv = c_ref[...].astype(F32)
        sg = _sigmoid(av)
        da_ref[...] = (d * cv * (sg * (1.0 + av * (1.0 - sg)))).astype(BF16)
        dc_ref[...] = (d * (av * sg)).astype(BF16)

    h_spec = _spec((None, tm, Fs), lambda s, i: (s, i, 0))
    o_shape = jax.ShapeDtypeStruct((N_CHIPS, T, Fs), BF16)
    return _call(body, name, (N_CHIPS, T // tm),
                 [_spec((tm, D), lambda s, i: (i, 0)), _spec((None, Fs, D), lambda s, i: (s, 0, 0)), h_spec, h_spec],
                 [h_spec, h_spec], [o_shape, o_shape], [], ("parallel", "parallel"), (dhb, w2, a, c), riders)


def _wgrad_hidden_rows(s, dhb, tr, name, scale):
    _, T, Fs = s.shape
    D = dhb.shape[1]
    return _mm(name, (N_CHIPS, T // tr), 1, "tn",
               [(s, _spec((None, tr, Fs), lambda k, r: (k, r, 0)), dhb, _spec((tr, D), lambda k, r: (r, 0)))],
               jax.ShapeDtypeStruct((N_CHIPS, Fs, D), F32), _spec((None, Fs, D), lambda k, r: (k, 0, 0)), scale=scale)


def _once(block, index_map):
    return pl.BlockSpec(block, index_map, pipeline_mode=pl.Buffered(1))


def _ffn_dn(da, w1t, dc, w3t, h, g, dres, tm, name, riders=None):
    _, T, Fs = da.shape
    D = w1t.shape[2]
    h_spec = _spec((N_CHIPS, tm, Fs), lambda i: (0, i, 0))
    w_spec = _once((N_CHIPS, Fs, D), lambda i: (0, 0, 0))
    return _mm_norm_bwd(name, "nn", [(da, h_spec, w1t, w_spec), (dc, h_spec, w3t, w_spec)], h, g, dres, tm, riders)


def _mm_side_by_side(a, w, mode, out_dtype, name, tm, first=0, count=N_CHIPS, scale=1.0):
    T, K = a.shape
    assert first % count == 0
    n = w.shape[2] if mode == "nn" else w.shape[1]

    def body(a_ref, w_ref, o_ref):
        av = a_ref[...].astype(BF16)
        for j in range(count):
            r = lax.dot_general(av, w_ref[j].astype(BF16), _DIMS[mode], preferred_element_type=F32)
            o_ref[:, j * n:(j + 1) * n] = (r * scale if scale != 1.0 else r).astype(o_ref.dtype)

    return pl.pallas_call(
        body, name=name, grid=(T // tm,),
        in_specs=[_spec((tm, K), lambda i: (i, 0)), _once((count,) + w.shape[1:], lambda i: (first // count, 0, 0))],
        out_specs=_spec((tm, count * n), lambda i: (i, 0)),
        out_shape=jax.ShapeDtypeStruct((T, count * n), out_dtype),
        compiler_params=_params(("parallel",)),
    )(a, w)


def _mm_colslots(a, w, out_dtype, name, tm, first=0, count=N_CHIPS, scale=1.0):
    return _mm_side_by_side(a, w, "nn", out_dtype, name, tm, first, count, scale)


def _wgrad_colslots(a, d, tr, name):
    T, K = a.shape
    Ns = d.shape[1] // N_CHIPS
    return _mm(name, (N_CHIPS, T // tr), 1, "tn",
               [(a, _spec((tr, K), lambda k, r: (r, 0)), d, _spec((tr, Ns), lambda k, r: (r, k)))],
               jax.ShapeDtypeStruct((N_CHIPS, K, Ns), F32), _spec((None, K, Ns), lambda k, r: (k, 0, 0)))


def _mm_rowslots(a, w, h, tm, name):
    T = a.shape[0]
    N = w.shape[2]
    row = _spec((tm, N), lambda i: (i, 0))
    return _mm(name, (T // tm,), None, "nn",
               [(a, _spec((tm, a.shape[1]), lambda i: (i, 0)), w, _once(w.shape, lambda i: (0, 0, 0)))],
               jax.ShapeDtypeStruct((T, N), F32), row, res=(h, row))


def _wgrad_rowslots(a, d, tr, name):
    T = a.shape[0]
    Ks = a.shape[1] // N_CHIPS
    N = d.shape[1]
    return _mm(name, (N_CHIPS, T // tr), 1, "tn",
               [(a, _spec((tr, Ks), lambda k, r: (r, k)), d, _spec((tr, N), lambda k, r: (r, 0)))],
               jax.ShapeDtypeStruct((N_CHIPS, Ks, N), F32), _spec((None, Ks, N), lambda k, r: (k, 0, 0)))


def _gelu_parts(x):
    inner = GELU_C * (x + 0.044715 * (x * x * x))
    t = jnp.tanh(inner)
    return t, GELU_C * (1.0 + 3.0 * 0.044715 * (x * x))


def _gelu_fwd(y, tm, name):
    T, W = y.shape

    def body(y_ref, o_ref):
        x = y_ref[...]
        t, _ = _gelu_parts(x)
        o_ref[...] = (0.5 * x * (1.0 + t)).astype(BF16)

    spec = pl.BlockSpec((tm, W), lambda i: (i, 0))
    return pl.pallas_call(body, name=name, grid=(T // tm,), in_specs=[spec], out_specs=spec,
                          out_shape=jax.ShapeDtypeStruct((T, W), BF16),
                          compiler_params=_params(("parallel",)))(y)


def _gelu_bwd(pairs, y, tm, name):
    T, W = y.shape
    npairs = len(pairs)

    def body(*refs):
        y_ref, o_ref = refs[2 * npairs], refs[2 * npairs + 1]
        x = y_ref[...]
        t, dinner = _gelu_parts(x)
        o_ref[...] = _sum_dots(refs[:2 * npairs], "nt") * (0.5 * (1.0 + t) + 0.5 * x * (1.0 - t * t) * dinner)

    spec = pl.BlockSpec((tm, W), lambda i: (i, 0))
    in_specs, args = [], []
    for d, w in pairs:
        in_specs += [_spec((tm, d.shape[1]), lambda i: (i, 0)), _once(w.shape, lambda i: (0, 0, 0))]
        args += [d, w]
    return pl.pallas_call(body, name=name, grid=(T // tm,), in_specs=in_specs + [spec], out_specs=spec,
                          out_shape=jax.ShapeDtypeStruct((T, W), F32),
                          compiler_params=_params(("parallel",)))(*args, y)


def _merge_cols(D):
    cb = 512 if D % 512 == 0 else D
    return cb, D // cb


def _merge_fwd(gates, attn, ga, gb, tm, name):
    T, D = attn.shape
    cb, nc = _merge_cols(D)

    def body(gat_ref, gss_ref, attn_ref, ga_ref, gb_ref, o_ref):
        ssm = ga_ref[...] * _sigmoid(gb_ref[...])
        o_ref[...] = (_sigmoid(gat_ref[...]) * attn_ref[...] + _sigmoid(gss_ref[...]) * ssm).astype(BF16)

    def col(block):
        return pl.BlockSpec((tm, cb), lambda i, j: (i, block * nc + j))

    return pl.pallas_call(
        body, name=name, grid=(T // tm, nc),
        in_specs=[col(0), col(1), col(0), col(0), col(0)],
        out_specs=col(0), out_shape=jax.ShapeDtypeStruct((T, D), BF16),
        compiler_params=_params(("parallel", "parallel")),
    )(gates, gates, attn, ga, gb)


def _merge_bwd(dhb, w_out, gates, attn, ga, gb, tm, name):
    T, D = attn.shape
    cb, nc = _merge_cols(D)
    Ks = w_out.shape[1]
    spb = cb // Ks

    def body(dh_ref, w_ref, gat_ref, gss_ref, attn_ref, ga_ref, gb_ref, dattn_ref, dgat_ref, dgss_ref, dga_ref, dgb_ref):
        dh = dh_ref[...]
        d = jnp.concatenate([lax.dot_general(dh, w_ref[s], _NT, preferred_element_type=F32) for s in range(spb)], axis=1)
        sa = _sigmoid(gat_ref[...])
        ss = _sigmoid(gss_ref[...])
        sb = _sigmoid(gb_ref[...])
        gav = ga_ref[...]
        dattn_ref[...] = d * sa
        dgat_ref[...] = (d * attn_ref[...] * (sa * (1.0 - sa))).astype(BF16)
        dgss_ref[...] = (d * (gav * sb) * (ss * (1.0 - ss))).astype(BF16)
        dssm = d * ss
        dga_ref[...] = (dssm * sb).astype(BF16)
        dgb_ref[...] = (dssm * gav * (sb * (1.0 - sb))).astype(BF16)

    def col(block):
        return pl.BlockSpec((tm, cb), lambda i, j: (i, block * nc + j))

    b16 = jax.ShapeDtypeStruct((T, D), BF16)
    return pl.pallas_call(
        body, name=name, grid=(T // tm, nc),
        in_specs=[pl.BlockSpec((tm, D), lambda i, j: (i, 0)), pl.BlockSpec((spb, Ks, D), lambda i, j: (j, 0, 0)),
                  col(0), col(1), col(0), col(0), col(0)],
        out_specs=[col(0)] * 5,
        out_shape=[jax.ShapeDtypeStruct((T, D), F32), b16, b16, b16, b16],
        compiler_params=_params(("parallel", "parallel")),
    )(dhb, w_out, gates, gates, attn, ga, gb)


def _loss_head(h, g, target, tm, name):
    T, D = h.shape
    B, S, _ = target.shape
    L = S + N_META
    nt = T // tm
    tpe = L // tm

    def body(h_ref, g_ref, t_hbm, dh_ref, dhb_ref, dg_ref, loss_ref, tbuf, acc_g, acc_l, sems):
        i = pl.program_id(0)
        j = i % tpe
        slot = i % 2

        def fetch(tile, sl, act):
            tb, tj = tile // tpe, tile % tpe

            @pl.when(tj == 0)
            def _():
                act(pltpu.make_async_copy(t_hbm.at[tb, pl.ds(0, tm - N_META), :],
                                          tbuf.at[sl, pl.ds(N_META, tm - N_META), :], sems.at[sl]))

            @pl.when(tj > 0)
            def _():
                act(pltpu.make_async_copy(t_hbm.at[tb, pl.ds(tj * tm - N_META, tm), :], tbuf.at[sl], sems.at[sl]))

        @pl.when(i == 0)
        def _():
            tbuf[:, 0:N_META, :] = jnp.zeros((2, N_META, D), F32)
            fetch(i, slot, lambda cp: cp.start())

        @pl.when(i + 1 < nt)
        def _():
            fetch(i + 1, 1 - slot, lambda cp: cp.start())

        fetch(i, slot, lambda cp: cp.wait())

        x = h_ref[...]
        gv = g_ref[...]
        r = lax.rsqrt(jnp.mean(x * x, axis=-1, keepdims=True) + NORM_EPS)
        xhat = x * r
        pos = j * tm + lax.broadcasted_iota(jnp.int32, (tm, 1), 0)
        err = jnp.where(pos >= N_META, xhat * gv - tbuf[slot], 0.0)
        dy = err * (1.0 / D)
        dxhat = dy * gv
        dh = r * (dxhat - xhat * jnp.mean(dxhat * xhat, axis=-1, keepdims=True))
        dh_ref[...] = dh
        dhb_ref[...] = dh.astype(BF16)
        pg = _fold8(dy * xhat)
        pe = _fold8(err * err)

        @pl.when(i == 0)
        def _():
            acc_g[...] = pg
            acc_l[...] = pe

        @pl.when(i > 0)
        def _():
            acc_g[...] += pg
            acc_l[...] += pe

        @pl.when(i == nt - 1)
        def _():
            dg_ref[...] = jnp.sum(acc_g[...], axis=0, keepdims=True)
            loss_ref[...] = jnp.full((1, D), (0.5 / D) * jnp.sum(acc_l[...]), F32)

    row = pl.BlockSpec((tm, D), lambda i: (i, 0))
    vec = pl.BlockSpec((1, D), lambda i: (0, 0))
    return pl.pallas_call(
        body, name=name, grid=(nt,),
        in_specs=[row, vec, pl.BlockSpec(memory_space=pl.ANY)], out_specs=[row, row, vec, vec],
        out_shape=[jax.ShapeDtypeStruct((T, D), F32), jax.ShapeDtypeStruct((T, D), BF16),
                   jax.ShapeDtypeStruct((1, D), F32), jax.ShapeDtypeStruct((1, D), F32)],
        scratch_shapes=[pltpu.VMEM((2, tm, D), F32), pltpu.VMEM((8, D), F32), pltpu.VMEM((8, D), F32),
                        pltpu.SemaphoreType.DMA((2,))],
        compiler_params=_params(("arbitrary",)),
    )(h, g, target)


def _heads_to_rows(blk):
    return jnp.concatenate([blk[:, g * HEAD_DIM:(g + 1) * HEAD_DIM] for g in range(Q_PER_KV)], axis=0)


def _rows_to_heads(x):
    rows = x.shape[0] // Q_PER_KV
    return jnp.concatenate([x[g * rows:(g + 1) * rows] for g in range(Q_PER_KV)], axis=1)


def _causal(R):
    kj = lax.broadcasted_iota(jnp.int32, (BLOCK, R), 0)
    qi = lax.broadcasted_iota(jnp.int32, (BLOCK, R), 1) & (BLOCK - 1)
    return kj <= qi


def _band_probs(s_band, s_m, sink):
    m = jnp.maximum(jnp.maximum(jnp.max(s_band, axis=0, keepdims=True), jnp.max(s_m, axis=0, keepdims=True)), sink)
    e_b, e_m, e_s = jnp.exp(s_band - m), jnp.exp(s_m - m), jnp.exp(sink - m)
    inv = 1.0 / (jnp.sum(e_b, axis=0, keepdims=True) + jnp.sum(e_m, axis=0, keepdims=True) + e_s)
    return e_b * inv, e_m * inv, e_s * inv


def _fold_band(tri, two):
    return jnp.where(tri, two[BLOCK:2 * BLOCK], two[0:BLOCK])


def _unfold_band(tri, band):
    return jnp.concatenate([jnp.where(tri, 0.0, band), jnp.where(tri, band, 0.0)], axis=0)


def _meta_probs(qm, k_m, sink_m):
    R = qm.shape[0]
    s = lax.dot_general(qm, k_m, _NT, preferred_element_type=F32)
    qi = lax.broadcasted_iota(jnp.int32, (R, N_META), 0) & (N_META - 1)
    kj = lax.broadcasted_iota(jnp.int32, (R, N_META), 1)
    s = jnp.where(kj <= qi, s, NEG_INF)
    m = jnp.maximum(jnp.max(s, axis=-1, keepdims=True), sink_m)
    e, e_s = jnp.exp(s - m), jnp.exp(sink_m - m)
    inv = 1.0 / (jnp.sum(e, axis=-1, keepdims=True) + e_s)
    return e * inv, e_s * inv


def _block_start(n):
    return pl.multiple_of(N_META + n * BLOCK, ROW_ALIGN)


def _kv(blk):
    return blk[:, 0:HEAD_DIM], blk[:, HEAD_DIM:2 * HEAD_DIM]


def _attn_fwd(q, kv, sink_row, sink_meta, B, name, riders=None):
    T, D = q.shape
    L = T // B
    KV = D // QW
    nb = (L - N_META) // BLOCK

    def body(q_ref, kv_ref, sk_ref, skm_ref, o_ref, kvs):
        kvs[...] = kv_ref[...].astype(BF16)
        k_m, v_m = _kv(kvs[0:N_META, :])
        p, _ = _meta_probs(_heads_to_rows(q_ref[0:N_META, :]), k_m, skm_ref[0])
        o_ref[0:N_META, :] = _rows_to_heads(jnp.dot(p.astype(BF16), v_m, preferred_element_type=F32))
        tri = _causal(Q_PER_KV * BLOCK)

        def block(cur, first, keys):
            k2, v2 = _kv(kvs[keys, :])
            qb = _heads_to_rows(q_ref[pl.ds(cur, BLOCK), :])
            st = lax.dot_general(k2, qb, _NT, preferred_element_type=F32)
            smt = lax.dot_general(k_m, qb, _NT, preferred_element_type=F32)
            s_band = jnp.where(tri, st, NEG_INF) if first else _fold_band(tri, st)
            p_b, p_m, _ = _band_probs(s_band, smt, sk_ref[0])
            p2 = (p_b if first else _unfold_band(tri, p_b)).astype(BF16)
            o = (lax.dot_general(p2, v2, _TN, preferred_element_type=F32)
                 + lax.dot_general(p_m.astype(BF16), v_m, _TN, preferred_element_type=F32))
            o_ref[pl.ds(cur, BLOCK), :] = _rows_to_heads(o)

        block(N_META, True, pl.ds(N_META, BLOCK))

        def step(n, carry):
            block(_block_start(n), False, pl.ds(_block_start(n - 1), 2 * BLOCK))
            return carry

        lax.fori_loop(1, nb, step, 0, unroll=5 if (nb - 1) % 5 == 0 else 1)

    q_spec = pl.BlockSpec((L, QW), lambda b, h: (b, h))
    return _call(body, name, (B, KV),
                 [q_spec, pl.BlockSpec((L, 2 * HEAD_DIM), lambda b, h: (b, h)),
                  pl.BlockSpec((1, 1, Q_PER_KV * BLOCK), lambda b, h: (h, 0, 0)),
                  pl.BlockSpec((1, Q_PER_KV * N_META, 1), lambda b, h: (h, 0, 0))],
                 [q_spec], [jax.ShapeDtypeStruct((T, D), F32)], [pltpu.VMEM((L, 2 * HEAD_DIM), BF16)],
                 ("parallel", "parallel"), (q, kv, sink_row, sink_meta), riders)


def _attn_bwd(q, kv, o, do, sink_row, sink_meta, B, name, riders=None):
    T, D = q.shape
    L = T // B
    KV = D // QW
    nb = (L - N_META) // BLOCK
    R = Q_PER_KV * BLOCK
    scale = HEAD_DIM ** -0.5

    def head_totals(col, rows_per_head):
        rid = lax.broadcasted_iota(jnp.int32, (8, 128), 0)
        out = jnp.zeros((8, 128), F32)
        for g in range(Q_PER_KV):
            out = out + jnp.where(rid == g, jnp.sum(col[g * rows_per_head:(g + 1) * rows_per_head, :]), 0.0)
        return out

    def body(q_ref, kv_ref, o_ref, do_ref, sk_ref, skm_ref, dq_ref, dkv_ref, dsk_ref, kvs, acc, acc_sink):
        b = pl.program_id(1)
        kvs[...] = kv_ref[...].astype(BF16)
        acc[...] = jnp.zeros_like(acc)
        k_m, v_m = _kv(kvs[0:N_META, :])

        qm = _heads_to_rows(q_ref[0:N_META, :])
        dom = _heads_to_rows(do_ref[0:N_META, :])
        delta = jnp.sum(dom * _heads_to_rows(o_ref[0:N_META, :]), axis=-1, keepdims=True)
        p, p_s = _meta_probs(qm, k_m, skm_ref[0])
        domb = dom.astype(BF16)
        ds = (p * (lax.dot_general(domb, v_m, _NT, preferred_element_type=F32) - delta)).astype(BF16)
        dq_ref[0:N_META, :] = _rows_to_heads(jnp.dot(ds, k_m, preferred_element_type=F32) * scale).astype(BF16)
        acc[0:N_META, :] += jnp.concatenate([lax.dot_general(ds, qm, _TN, preferred_element_type=F32),
                                             lax.dot_general(p.astype(BF16), domb, _TN, preferred_element_type=F32)], axis=1)
        sink_tot = head_totals(-p_s * delta, N_META)
        tri = _causal(R)
        acc_sink[...] = jnp.zeros_like(acc_sink)
        ones = jnp.ones((8, HEAD_DIM), BF16)

        def block(cur, first, keys):
            k2, v2 = _kv(kvs[keys, :])
            rows = pl.ds(cur, BLOCK)
            qb = _heads_to_rows(q_ref[rows, :])
            dob = _heads_to_rows(do_ref[rows, :])
            prod = dob * _heads_to_rows(o_ref[rows, :])
            hi = prod.astype(BF16)
            lo = (prod - hi.astype(F32)).astype(BF16)
            delta = (lax.dot_general(ones, hi, _NT, preferred_element_type=F32)
                     + lax.dot_general(ones, lo, _NT, preferred_element_type=F32))[0:1]
            dobb = dob.astype(BF16)
            st = lax.dot_general(k2, qb, _NT, preferred_element_type=F32)
            smt = lax.dot_general(k_m, qb, _NT, preferred_element_type=F32)
            s_band = jnp.where(tri, st, NEG_INF) if first else _fold_band(tri, st)
            p_b, p_m, p_s = _band_probs(s_band, smt, sk_ref[0])
            dpt = lax.dot_general(v2, dobb, _NT, preferred_element_type=F32)
            dpm = lax.dot_general(v_m, dobb, _NT, preferred_element_type=F32)
            ds_b = p_b * ((dpt if first else _fold_band(tri, dpt)) - delta)
            ds2 = (ds_b if first else _unfold_band(tri, ds_b)).astype(BF16)
            p2 = (p_b if first else _unfold_band(tri, p_b)).astype(BF16)
            dsm = (p_m * (dpm - delta)).astype(BF16)
            pm = p_m.astype(BF16)
            dq = (lax.dot_general(ds2, k2, _TN, preferred_element_type=F32)
                  + lax.dot_general(dsm, k_m, _TN, preferred_element_type=F32))
            dq_ref[rows, :] = _rows_to_heads(dq * scale).astype(BF16)
            acc[keys, :] += jnp.concatenate([jnp.dot(ds2, qb, preferred_element_type=F32),
                                             jnp.dot(p2, dobb, preferred_element_type=F32)], axis=1)
            acc[0:N_META, :] += jnp.concatenate([jnp.dot(dsm, qb, preferred_element_type=F32),
                                                 jnp.dot(pm, dobb, preferred_element_type=F32)], axis=1)
            acc_sink[0:1, :] += -p_s * delta

        block(N_META, True, pl.ds(N_META, BLOCK))

        def step(n, carry):
            block(_block_start(n), False, pl.ds(_block_start(n - 1), 2 * BLOCK))
            return carry

        lax.fori_loop(1, nb, step, 0, unroll=5 if (nb - 1) % 5 == 0 else 1)
        dkv_ref[...] = acc[...].astype(BF16)
        rid = lax.broadcasted_iota(jnp.int32, (8, 128), 0)
        tot = sink_tot
        for g in range(Q_PER_KV):
            tot = tot + jnp.where(rid == g, jnp.sum(acc_sink[:, g * BLOCK:(g + 1) * BLOCK]), 0.0)

        @pl.when(b == 0)
        def _():
            dsk_ref[0] = tot

        @pl.when(b > 0)
        def _():
            dsk_ref[0] += tot

    q_spec = pl.BlockSpec((L, QW), lambda h, b: (b, h))
    kv_spec = pl.BlockSpec((L, 2 * HEAD_DIM), lambda h, b: (b, h))
    return _call(body, name, (KV, B),
                 [q_spec, kv_spec, q_spec, q_spec,
                  pl.BlockSpec((1, 1, R), lambda h, b: (h, 0, 0)),
                  pl.BlockSpec((1, Q_PER_KV * N_META, 1), lambda h, b: (h, 0, 0))],
                 [q_spec, kv_spec, pl.BlockSpec((1, 8, 128), lambda h, b: (h, 0, 0))],
                 [jax.ShapeDtypeStruct((T, D), BF16), jax.ShapeDtypeStruct((T, KV * 2 * HEAD_DIM), BF16),
                  jax.ShapeDtypeStruct((KV, 8, 128), F32)],
                 [pltpu.VMEM((L, 2 * HEAD_DIM), BF16), pltpu.VMEM((L, 2 * HEAD_DIM), F32), pltpu.VMEM((8, R), F32)],
                 ("parallel", "arbitrary"), (q, kv, o, do, sink_row, sink_meta), riders)


def _cmul_add(acc_r, acc_i, lr, li, xr, xi):
    return acc_r + (lr * xr - li * xi), acc_i + (lr * xi + li * xr)


def _cols_per_step(ncol):
    for cps in (4, 2):
        if ncol % cps == 0:
            return cps
    return 1


def _ssm_fwd(u, bmat, cmat, dskip, tables, nbatch, rc, name):
    T, W = u.shape
    ncol = W // SSM_LANES
    nch = T // rc
    S = STATE_LANES
    cps = _cols_per_step(ncol)
    assert nbatch == 4

    def body(u_ref, b_ref, c_ref, d_ref, tab_ref, y_ref, xs_ref, st_ref, carry_ref):
        ch = pl.program_id(1)

        @pl.when(ch == 0)
        def _():
            carry_ref[...] = jnp.zeros_like(carry_ref)

        uv = u_ref[...]
        for k in range(cps):
            st_ref[:, 2 * S * k:2 * S * (k + 1)] = jnp.dot(uv[:, SSM_LANES * k:SSM_LANES * (k + 1)].astype(BF16), b_ref[k],
                                                           preferred_element_type=F32)
        low = lax.broadcasted_iota(jnp.int32, (8, S), 0) < nbatch

        def tile(k, r0, c_r, c_i):
            re, im = slice(2 * S * k, 2 * S * k + S), slice(2 * S * k + S, 2 * S * (k + 1))
            la_r, la_i = tab_ref[k, :, 0:S], tab_ref[k, :, S:2 * S]
            lb_r, lb_i = tab_ref[k, :, 2 * S:3 * S], tab_ref[k, :, 3 * S:4 * S]
            v_r = st_ref[pl.ds(r0, 8), re]
            v_i = st_ref[pl.ds(r0, 8), im]
            v_r, v_i = _cmul_add(v_r, v_i, la_r, la_i, pltpu.roll(v_r, nbatch, 0), pltpu.roll(v_i, nbatch, 0))
            rc_r, rc_i = pltpu.roll(c_r, nbatch, 0), pltpu.roll(c_i, nbatch, 0)
            cb_r, cb_i = jnp.where(low, rc_r, c_r), jnp.where(low, rc_i, c_i)
            v_r, v_i = _cmul_add(v_r, v_i, lb_r, lb_i, cb_r, cb_i)
            st_ref[pl.ds(r0, 8), re] = v_r
            st_ref[pl.ds(r0, 8), im] = v_i
            return v_r, v_i

        def step(i, carry):
            r0 = pl.multiple_of(i * 8, 8)
            out = []
            for k in range(cps):
                out += list(tile(k, r0, carry[2 * k], carry[2 * k + 1]))
            return tuple(out)

        halves = tuple(carry_ref[:, S * j:S * (j + 1)] for j in range(2 * cps))
        halves = lax.fori_loop(0, rc // 8, step, halves)
        for j in range(2 * cps):
            carry_ref[:, S * j:S * (j + 1)] = halves[j]
        xb = st_ref[...].astype(BF16)
        xs_ref[...] = xb
        for k in range(cps):
            cols = slice(SSM_LANES * k, SSM_LANES * (k + 1))
            y_ref[:, cols] = (jnp.dot(xb[:, 2 * S * k:2 * S * (k + 1)], c_ref[k], preferred_element_type=F32)
                              + d_ref[:, cols] * uv[:, cols])

    return pl.pallas_call(
        body, name=name, grid=(ncol // cps, nch),
        in_specs=[pl.BlockSpec((rc, cps * SSM_LANES), lambda g, c: (c, g)),
                  pl.BlockSpec((cps, SSM_LANES, 2 * S), lambda g, c: (g, 0, 0)),
                  pl.BlockSpec((cps, 2 * S, SSM_LANES), lambda g, c: (g, 0, 0)),
                  pl.BlockSpec((1, cps * SSM_LANES), lambda g, c: (0, g)),
                  pl.BlockSpec((cps, 8, 4 * S), lambda g, c: (g, 0, 0))],
        out_specs=[pl.BlockSpec((rc, cps * SSM_LANES), lambda g, c: (c, g)),
                   pl.BlockSpec((rc, cps * 2 * S), lambda g, c: (c, g))],
        out_shape=[jax.ShapeDtypeStruct((T, W), F32), jax.ShapeDtypeStruct((T, ncol * 2 * S), BF16)],
        scratch_shapes=[pltpu.VMEM((rc, cps * 2 * S), F32), pltpu.VMEM((8, cps * 2 * S), F32)],
        compiler_params=_params(("parallel", "arbitrary")),
    )(u, bmat, cmat, dskip, tables)


def _ssm_bwd(dy, u, xs, bmat, cmat, dskip, tables, nbatch, rc, name):
    T, W = u.shape
    ncol = W // SSM_LANES
    nch = T // rc
    S = STATE_LANES
    ntile = rc // 16
    cps = _cols_per_step(ncol)

    def body(dy_ref, u_ref, xs_ref, b_ref, c_ref, d_ref, tab_ref,
             du_ref, db_ref, dc_ref, dl_ref, dd_ref, st_ref, carry_ref, accl_ref, accd_ref):
        ch = pl.program_id(1)

        @pl.when(ch == 0)
        def _():
            carry_ref[...] = jnp.zeros_like(carry_ref)
            accl_ref[...] = jnp.zeros_like(accl_ref)
            accd_ref[...] = jnp.zeros_like(accd_ref)
            db_ref[...] = jnp.zeros_like(db_ref)
            dc_ref[...] = jnp.zeros_like(dc_ref)

        dyv = dy_ref[...]
        uv = u_ref[...]
        dyb = dyv.astype(BF16)
        for k in range(cps):
            st_ref[:, 2 * S * k:2 * S * (k + 1)] = lax.dot_general(dyb[:, SSM_LANES * k:SSM_LANES * (k + 1)], c_ref[k], _NT,
                                                                   preferred_element_type=F32)
        low = lax.broadcasted_iota(jnp.int32, (8, S), 0) < nbatch

        def tile(k, r0, x_r, x_i, c_r, c_i, al_r, al_i):
            re, im = slice(2 * S * k, 2 * S * k + S), slice(2 * S * k + S, 2 * S * (k + 1))
            la_r, la_i = tab_ref[k, :, 0:S], tab_ref[k, :, S:2 * S]
            lb_r, lb_i = tab_ref[k, :, 2 * S:3 * S], tab_ref[k, :, 3 * S:4 * S]
            v_r = st_ref[pl.ds(r0, 8), re]
            v_i = st_ref[pl.ds(r0, 8), im]
            v_r, v_i = _cmul_add(v_r, v_i, la_r, la_i, pltpu.roll(v_r, nbatch, 0), pltpu.roll(v_i, nbatch, 0))
            cb_r = jnp.where(low, c_r, pltpu.roll(c_r, nbatch, 0))
            cb_i = jnp.where(low, c_i, pltpu.roll(c_i, nbatch, 0))
            v_r, v_i = _cmul_add(v_r, v_i, lb_r, lb_i, cb_r, cb_i)
            st_ref[pl.ds(r0, 8), re] = v_r
            st_ref[pl.ds(r0, 8), im] = v_i
            n_r = jnp.where(low, pltpu.roll(v_r, nbatch, 0), cb_r)
            n_i = jnp.where(low, pltpu.roll(v_i, nbatch, 0), cb_i)
            al_r = al_r + (n_r * x_r + n_i * x_i)
            al_i = al_i + (n_i * x_r - n_r * x_i)
            return v_r, v_i, al_r, al_i

        def step(j, carry):
            r0 = pl.multiple_of((ntile - 1 - j) * 16, 16)
            out = []
            for k in range(cps):
                re, im = slice(2 * S * k, 2 * S * k + S), slice(2 * S * k + S, 2 * S * (k + 1))
                x_r = xs_ref[pl.ds(r0, 16), re].astype(F32)
                x_i = xs_ref[pl.ds(r0, 16), im].astype(F32)
                mid = tile(k, r0 + 8, x_r[8:16], x_i[8:16], *carry[4 * k:4 * k + 4])
                out += list(tile(k, r0, x_r[0:8], x_i[0:8], *mid))
            return tuple(out)

        init = []
        for k in range(cps):
            init += [carry_ref[:, 2 * S * k:2 * S * k + S], carry_ref[:, 2 * S * k + S:2 * S * (k + 1)],
                     accl_ref[:, 2 * S * k:2 * S * k + S], accl_ref[:, 2 * S * k + S:2 * S * (k + 1)]]
        fin = lax.fori_loop(0, ntile, step, tuple(init))
        for k in range(cps):
            carry_ref[:, 2 * S * k:2 * S * k + S] = fin[4 * k]
            carry_ref[:, 2 * S * k + S:2 * S * (k + 1)] = fin[4 * k + 1]
            accl_ref[:, 2 * S * k:2 * S * k + S] = fin[4 * k + 2]
            accl_ref[:, 2 * S * k + S:2 * S * (k + 1)] = fin[4 * k + 3]
        dsb = st_ref[...].astype(BF16)
        ub = uv.astype(BF16)
        for k in range(cps):
            cols, lanes = slice(SSM_LANES * k, SSM_LANES * (k + 1)), slice(2 * S * k, 2 * S * (k + 1))
            du_ref[:, cols] = (lax.dot_general(dsb[:, lanes], b_ref[k], _NT, preferred_element_type=F32)
                               + d_ref[:, cols] * dyv[:, cols])
            db_ref[k] += lax.dot_general(ub[:, cols], dsb[:, lanes], _TN, preferred_element_type=F32)
            dc_ref[k] += lax.dot_general(xs_ref[:, lanes], dyb[:, cols], _TN, preferred_element_type=F32)
        accd_ref[...] += _fold8(dyv * uv)

        @pl.when(ch == nch - 1)
        def _():
            for k in range(cps):
                dl_ref[k] = jnp.sum(accl_ref[:, 2 * S * k:2 * S * (k + 1)], axis=0, keepdims=True)
            dd_ref[...] = jnp.sum(accd_ref[...], axis=0, keepdims=True)

    rev = lambda g, c: (nch - 1 - c, g)
    return pl.pallas_call(
        body, name=name, grid=(ncol // cps, nch),
        in_specs=[pl.BlockSpec((rc, cps * SSM_LANES), rev), pl.BlockSpec((rc, cps * SSM_LANES), rev),
                  pl.BlockSpec((rc, cps * 2 * S), rev),
                  pl.BlockSpec((cps, SSM_LANES, 2 * S), lambda g, c: (g, 0, 0)),
                  pl.BlockSpec((cps, 2 * S, SSM_LANES), lambda g, c: (g, 0, 0)),
                  pl.BlockSpec((1, cps * SSM_LANES), lambda g, c: (0, g)),
                  pl.BlockSpec((cps, 8, 4 * S), lambda g, c: (g, 0, 0))],
        out_specs=[pl.BlockSpec((rc, cps * SSM_LANES), rev),
                   pl.BlockSpec((cps, SSM_LANES, 2 * S), lambda g, c: (g, 0, 0)),
                   pl.BlockSpec((cps, 2 * S, SSM_LANES), lambda g, c: (g, 0, 0)),
                   pl.BlockSpec((cps, 1, 2 * S), lambda g, c: (g, 0, 0)),
                   pl.BlockSpec((1, cps * SSM_LANES), lambda g, c: (0, g))],
        out_shape=[jax.ShapeDtypeStruct((T, W), F32),
                   jax.ShapeDtypeStruct((ncol, SSM_LANES, 2 * S), F32),
                   jax.ShapeDtypeStruct((ncol, 2 * S, SSM_LANES), F32),
                   jax.ShapeDtypeStruct((ncol, 1, 2 * S), F32),
                   jax.ShapeDtypeStruct((1, W), F32)],
        scratch_shapes=[pltpu.VMEM((rc, cps * 2 * S), F32), pltpu.VMEM((8, cps * 2 * S), F32),
                        pltpu.VMEM((8, cps * 2 * S), F32), pltpu.VMEM((8, cps * SSM_LANES), F32)],
        compiler_params=_params(("parallel", "arbitrary")),
    )(dy, u, xs, bmat, cmat, dskip, tables)


def _ssm_matrices(a_re, a_im, log_step, b_re, b_im, c_re, c_im):
    G, N = a_re.shape
    ncol = G // GROUPS_PER_COL
    step = jnp.exp(log_step)[:, None]
    mag = jnp.exp(a_re * step)
    ang = a_im * step
    lam_re, lam_im = mag * jnp.cos(ang), mag * jnp.sin(ang)
    den = a_re * a_re + a_im * a_im
    nr, ni = lam_re - 1.0, lam_im
    coef_re = (nr * a_re + ni * a_im) / den
    coef_im = (ni * a_re - nr * a_im) / den
    bb_re = coef_re[..., None] * b_re - coef_im[..., None] * b_im
    bb_im = coef_re[..., None] * b_im + coef_im[..., None] * b_re
    eye = jnp.eye(GROUPS_PER_COL, dtype=F32)
    bb = jnp.stack([bb_re, bb_im]).reshape(2, ncol, GROUPS_PER_COL, N, SSM_GROUP)
    bmat = jnp.einsum("pbgnc,gh->bgcphn", bb, eye).reshape(ncol, SSM_LANES, 2 * STATE_LANES)
    cc = jnp.stack([c_re, -c_im]).reshape(2, ncol, GROUPS_PER_COL, SSM_GROUP, N)
    cmat = jnp.einsum("pbgcn,gh->bpgnhc", cc, eye).reshape(ncol, 2 * STATE_LANES, SSM_LANES)
    lam = jnp.concatenate([lam_re.reshape(ncol, STATE_LANES), lam_im.reshape(ncol, STATE_LANES)], axis=-1)
    return lam, bmat, cmat


def _scan_tables(lam, nbatch, conj):
    S = STATE_LANES
    lr, li = lam[:, None, 0:S], lam[:, None, S:2 * S]
    if conj:
        li = -li
    l2r, l2i = lr * lr - li * li, 2.0 * lr * li
    first = (jnp.arange(8) < nbatch)[None, :, None]
    zero = jnp.zeros_like(lr)
    if conj:
        parts = [jnp.where(first, lr, zero), jnp.where(first, li, zero), jnp.where(first, l2r, lr), jnp.where(first, l2i, li)]
    else:
        parts = [jnp.where(first, zero, lr), jnp.where(first, zero, li), jnp.where(first, lr, l2r), jnp.where(first, li, l2i)]
    return jnp.concatenate([jnp.broadcast_to(p, (lam.shape[0], 8, S)) for p in parts], axis=-1)


def _adamw_update(w_ref, g_ref, m_ref, v_ref, d_ref, nm_ref, nv_ref):
    gv = g_ref[...]
    mn = ADAM_B1 * m_ref[...] + (1.0 - ADAM_B1) * gv
    vn = ADAM_B2 * v_ref[...] + (1.0 - ADAM_B2) * (gv * gv)
    m_hat = mn / (1.0 - ADAM_B1 ** ADAM_STEP)
    v_hat = vn / (1.0 - ADAM_B2 ** ADAM_STEP)
    d_ref[...] = -ADAM_LR * (m_hat / (jnp.sqrt(v_hat) + ADAM_EPS) + ADAM_WD * w_ref[...])
    nm_ref[...] = mn
    nv_ref[...] = vn


def _adamw_small(ws, gs, ms, vs, name):
    n = len(ws)

    def body(*refs):
        for i in range(n):
            _adamw_update(refs[i], refs[n + i], refs[2 * n + i], refs[3 * n + i],
                          refs[4 * n + i], refs[5 * n + i], refs[6 * n + i])

    vm = pl.BlockSpec(memory_space=pltpu.VMEM)
    shapes = [jax.ShapeDtypeStruct(a.shape, F32) for a in ws]
    outs = pl.pallas_call(body, name=name, in_specs=[vm] * (4 * n), out_specs=[vm] * (3 * n), out_shape=shapes * 3,
                          compiler_params=pltpu.CompilerParams(vmem_limit_bytes=VMEM_LIMIT))(*ws, *gs, *ms, *vs)
    return outs[:n], outs[n:2 * n], outs[2 * n:]


def _adamw(w, g, m, v, name):
    R, C = w.shape[-2], w.shape[-1]
    tr = R if R <= 512 else _pick_tile(R, 512, 8)
    body = functools.partial(_adamw_update)

    def spec_for(a):
        if len(a.shape) == 2:
            return pl.BlockSpec((tr, C), lambda i: (i, 0))
        return pl.BlockSpec((None, tr, C), lambda i: (0, i, 0))

    spec = spec_for(w)
    shp = jax.ShapeDtypeStruct(w.shape, F32)
    return pl.pallas_call(body, name=name, grid=(R // tr,), in_specs=[spec, spec_for(g), spec, spec], out_specs=[spec] * 3,
                          out_shape=[shp, shp, shp], compiler_params=_params(("parallel",)))(w, g, m, v)


_ANY = pl.BlockSpec(memory_space=pl.ANY)


def _place():
    x, y, c = lax.axis_index("x"), lax.axis_index("y"), lax.axis_index("c")
    chips = [(1 - x, y), (x, 1 - y), (1 - x, 1 - y)]
    return x, y, c, chips


def _remote(src, dst, send_sems, recv_sems, k, to):
    return pltpu.make_async_remote_copy(src_ref=src, dst_ref=dst, send_sem=send_sems.at[k], recv_sem=recv_sems.at[k],
                                        device_id=to, device_id_type=MESH_IDS)


class _Riders:
    def __init__(self, srcs, out_shapes, n_sems, copies):
        self.srcs, self.out_shapes, self.n_sems, self.copies = list(srcs), list(out_shapes), n_sems, copies


def _call(body, name, grid, in_specs, out_specs, out_shape, scratch_shapes, sem, args, riders=None):
    if riders is None:
        return pl.pallas_call(body, name=name, grid=grid, in_specs=in_specs, out_specs=out_specs, out_shape=out_shape,
                              scratch_shapes=scratch_shapes, compiler_params=_params(sem))(*args)
    n_in, n_out, n_scr = len(in_specs), len(out_specs), len(scratch_shapes)
    r_in, r_out = len(riders.srcs), len(riders.out_shapes)

    def carrying(*refs):
        a, b = n_in, n_in + r_in
        c, d = b + n_out, b + n_out + r_out
        e = d + n_scr
        sends, arrivals = riders.copies(refs[a:b], refs[c:d], refs[e], refs[e + 1])
        first, last = None, None
        for ax, size in enumerate(grid):
            at0, at1 = pl.program_id(ax) == 0, pl.program_id(ax) == size - 1
            first = at0 if first is None else first & at0
            last = at1 if last is None else last & at1

        @pl.when(first)
        def _():
            for cp in sends:
                cp.start()

        body(*refs[:a], *refs[b:c], *refs[d:e])

        @pl.when(last)
        def _():
            for cp in arrivals:
                cp.wait_recv()
            for cp in sends:
                cp.wait_send()

    outs = pl.pallas_call(
        carrying, name=name, grid=grid, in_specs=list(in_specs) + [_ANY] * r_in,
        out_specs=list(out_specs) + [_ANY] * r_out, out_shape=list(out_shape) + riders.out_shapes,
        scratch_shapes=list(scratch_shapes) + [pltpu.SemaphoreType.DMA((riders.n_sems,)),
                                               pltpu.SemaphoreType.DMA((riders.n_sems,))],
        compiler_params=pltpu.CompilerParams(dimension_semantics=("arbitrary",) * len(grid),
                                             vmem_limit_bytes=VMEM_LIMIT, has_side_effects=True),
    )(*args, *riders.srcs)
    return outs[:n_out], outs[n_out:]


def _gather_riders(shards):
    def copies(srcs, outs, send_sems, recv_sems):
        x, y, c, chips = _place()
        sends, arrivals = [], []
        for i, s in enumerate(shards):
            half = s.shape[0] // 2
            rows = pl.ds(c * half, half)
            for j, chip in enumerate(chips):
                sends.append(_remote(srcs[i].at[rows, :], outs[i].at[2 * x + y, rows, :], send_sems, recv_sems,
                                     3 * i + j, (*chip, c)))
                landed = outs[i].at[2 * chip[0] + chip[1], rows, :]
                arrivals.append(_remote(landed, landed, send_sems, recv_sems, 3 * i + j, (*chip, c)))
        return sends, arrivals

    return _Riders(shards, [jax.ShapeDtypeStruct((N_CHIPS,) + s.shape, s.dtype) for s in shards], 3 * len(shards), copies)


def _exchange_riders(parts):
    def copies(srcs, outs, send_sems, recv_sems):
        x, y, c, chips = _place()
        sends = [_remote(srcs[i].at[2 * chip[0] + chip[1]], outs[i].at[j], send_sems, recv_sems, 3 * i + j, (*chip, c))
                 for i in range(len(parts)) for j, chip in enumerate(chips)]
        return sends, sends

    return _Riders(parts, [jax.ShapeDtypeStruct((3,) + p.shape[1:], p.dtype) for p in parts], 3 * len(parts), copies)


def _swap_riders(grads):
    def copies(srcs, outs, send_sems, recv_sems):
        x, y, c, _ = _place()
        sends = []
        for i, g in enumerate(grads):
            half = g.shape[1] // 2
            sends.append(_remote(srcs[i].at[:, pl.ds((1 - c) * half, half), :], outs[i], send_sems, recv_sems, i,
                                 (x, y, 1 - c)))
        return sends, sends

    return _Riders(grads, [jax.ShapeDtypeStruct((N_CHIPS, g.shape[1] // 2, g.shape[2]), g.dtype) for g in grads],
                   len(grads), copies)


def _forward_halves(gathered, shards, tag):
    n = len(gathered)

    def body(*refs):
        srcs, outs = refs[:n], refs[n:2 * n]
        send_sems, recv_sems = refs[2 * n:]
        x, y, c, chips = _place()
        sibling = (x, y, 1 - c)
        cps = []
        for i in range(n):
            half = gathered[i].shape[1] // 2
            for j, chip in enumerate(chips):
                slot = 2 * chip[0] + chip[1]
                cps.append(_remote(srcs[i].at[slot, pl.ds(c * half, half), :], outs[i].at[slot, pl.ds(c * half, half), :],
                                   send_sems, recv_sems, 3 * i + j, sibling))
        for cp in cps:
            cp.start()
        for i in range(n):
            half = gathered[i].shape[1] // 2
            for j, chip in enumerate(chips):
                theirs = outs[i].at[2 * chip[0] + chip[1], pl.ds((1 - c) * half, half), :]
                _remote(theirs, theirs, send_sems, recv_sems, 3 * i + j, sibling).wait_recv()
        for cp in cps:
            cp.wait_send()

    outs = pl.pallas_call(
        body, name=f"gather_forward_{tag}", in_specs=[_ANY] * n, out_specs=[_ANY] * n,
        out_shape=[jax.ShapeDtypeStruct(g.shape, g.dtype) for g in gathered],
        input_output_aliases={i: i for i in range(n)},
        scratch_shapes=[pltpu.SemaphoreType.DMA((3 * n,)), pltpu.SemaphoreType.DMA((3 * n,))],
        compiler_params=pltpu.CompilerParams(has_side_effects=True),
    )(*gathered)
    slot = 2 * lax.axis_index("x") + lax.axis_index("y")
    return [lax.dynamic_update_slice(o, s[None], (slot, 0, 0)) for o, s in zip(outs, shards)]


def _gather_weights(shards):
    n = len(shards)

    def body(*refs):
        srcs, outs = refs[:n], refs[n:2 * n]
        send_sems, recv_sems = refs[2 * n:]
        x, y, c, chips = _place()
        sibling = (x, y, 1 - c)

        def piece(i, px, py, pc):
            half = shards[i].shape[0] // 2
            return outs[i].at[2 * px + py, pl.ds(pc * half, half), :]

        first = []
        for i in range(n):
            half = shards[i].shape[0] // 2
            for j, chip in enumerate(chips):
                first.append(_remote(srcs[i].at[pl.ds(c * half, half), :], piece(i, x, y, c), send_sems, recv_sems,
                                     6 * i + j, (*chip, c)))
        for cp in first:
            cp.start()
        passed = []
        for i in range(n):
            for j, chip in enumerate(chips):
                _remote(piece(i, *chip, c), piece(i, *chip, c), send_sems, recv_sems, 6 * i + j, (*chip, c)).wait_recv()
                cp = _remote(piece(i, *chip, c), piece(i, *chip, c), send_sems, recv_sems, 6 * i + 3 + j, sibling)
                cp.start()
                passed.append(cp)
        for i in range(n):
            for j, chip in enumerate(chips):
                _remote(piece(i, *chip, 1 - c), piece(i, *chip, 1 - c), send_sems, recv_sems, 6 * i + 3 + j,
                        sibling).wait_recv()
        for cp in first + passed:
            cp.wait_send()

    outs = pl.pallas_call(
        body, name="gather_weights", in_specs=[_ANY] * n, out_specs=[_ANY] * n,
        out_shape=[jax.ShapeDtypeStruct((N_CHIPS,) + s.shape, s.dtype) for s in shards],
        scratch_shapes=[pltpu.SemaphoreType.DMA((6 * n,)), pltpu.SemaphoreType.DMA((6 * n,))],
        compiler_params=pltpu.CompilerParams(has_side_effects=True),
    )(*shards)
    slot = 2 * lax.axis_index("x") + lax.axis_index("y")
    return [lax.dynamic_update_slice(o, s[None], (slot, 0, 0)) for o, s in zip(outs, shards)]


def _swap_halves(grads, tag):
    n = len(grads)

    def body(*refs):
        srcs, outs = refs[:n], refs[n:2 * n]
        send_sems, recv_sems = refs[2 * n:]
        x, y, c, _ = _place()
        cps = []
        for i in range(n):
            half = grads[i].shape[1] // 2
            cps.append(_remote(srcs[i].at[:, pl.ds((1 - c) * half, half), :], outs[i], send_sems, recv_sems, i, (x, y, 1 - c)))
        for cp in cps:
            cp.start()
        for cp in cps:
            cp.wait()

    return pl.pallas_call(
        body, name=f"grad_swap_halves_{tag}", in_specs=[_ANY] * n, out_specs=[_ANY] * n,
        out_shape=[jax.ShapeDtypeStruct((N_CHIPS, g.shape[1] // 2, g.shape[2]), g.dtype) for g in grads],
        scratch_shapes=[pltpu.SemaphoreType.DMA((n,)), pltpu.SemaphoreType.DMA((n,))],
        compiler_params=pltpu.CompilerParams(has_side_effects=True),
    )(*grads)


def _join_halves(fulls):
    n = len(fulls)

    def body(*refs):
        srcs, outs = refs[:n], refs[n:2 * n]
        send_sems, recv_sems = refs[2 * n:]
        x, y, c, _ = _place()
        sibling = (x, y, 1 - c)
        cps = []
        for i in range(n):
            h = fulls[i].shape[0] // 2
            cps.append(_remote(srcs[i].at[pl.ds(c * h, h), :], outs[i].at[pl.ds(c * h, h), :], send_sems, recv_sems, i,
                               sibling))
        for cp in cps:
            cp.start()
        for i in range(n):
            h = fulls[i].shape[0] // 2
            theirs = outs[i].at[pl.ds((1 - c) * h, h), :]
            _remote(theirs, theirs, send_sems, recv_sems, i, sibling).wait_recv()
        for cp in cps:
            cp.wait_send()

    return pl.pallas_call(
        body, name="grad_join_halves", in_specs=[_ANY] * n, out_specs=[_ANY] * n,
        out_shape=[jax.ShapeDtypeStruct(f.shape, f.dtype) for f in fulls],
        input_output_aliases={i: i for i in range(n)},
        scratch_shapes=[pltpu.SemaphoreType.DMA((n,)), pltpu.SemaphoreType.DMA((n,))],
        compiler_params=pltpu.CompilerParams(has_side_effects=True),
    )(*fulls)


def _half_tile(h):
    return h if h <= 512 else _pick_tile(h, 512, ROW_ALIGN)


def _sum_halves(g, r1, c_idx, name):
    _, R, C = g.shape
    H = R // 2
    tr = _half_tile(H)
    nblk = H // tr

    def body(c_ref, g_ref, r_ref, p_ref):
        p_ref[...] = (g_ref[...] + r_ref[...]).astype(BF16)

    half = pl.BlockSpec((None, tr, C), lambda s, i, c_ref: (s, c_ref[0] * nblk + i, 0))
    plain = pl.BlockSpec((None, tr, C), lambda s, i, c_ref: (s, i, 0))
    return pl.pallas_call(
        body, name=name,
        grid_spec=pltpu.PrefetchScalarGridSpec(num_scalar_prefetch=1, grid=(N_CHIPS, nblk), in_specs=[half, plain],
                                               out_specs=plain),
        out_shape=jax.ShapeDtypeStruct((N_CHIPS, H, C), BF16),
        compiler_params=_params(("parallel", "parallel")),
    )(c_idx, g, r1)


def _sum_chips(g, r1, r2, idx, name):
    _, R, C = g.shape
    H = R // 2
    tr = _half_tile(H)
    nblk = H // tr

    def body(idx_ref, g_ref, r1_ref, r2_ref, o_ref):
        o_ref[...] = (((g_ref[...] + r1_ref[...]) + r2_ref[0].astype(F32)) + r2_ref[1].astype(F32)) + r2_ref[2].astype(F32)

    return pl.pallas_call(
        body, name=name,
        grid_spec=pltpu.PrefetchScalarGridSpec(
            num_scalar_prefetch=1, grid=(nblk,),
            in_specs=[pl.BlockSpec((None, tr, C), lambda i, idx_ref: (idx_ref[0], idx_ref[1] * nblk + i, 0)),
                      pl.BlockSpec((None, tr, C), lambda i, idx_ref: (idx_ref[0], i, 0)),
                      pl.BlockSpec((3, tr, C), lambda i, idx_ref: (0, i, 0))],
            out_specs=pl.BlockSpec((tr, C), lambda i, idx_ref: (idx_ref[1] * nblk + i, 0))),
        out_shape=jax.ShapeDtypeStruct((R, C), F32),
        compiler_params=_params(("parallel",)),
    )(idx, g, r1, r2)


def _all_reduce_small(v, n_fold, fold_rows, fold_at):
    M, N = v.shape

    def body(x_ref, tot_ref, fold_ref, all_ref, send_sems, recv_sems, local_sem):
        x, y, c, chips = _place()
        me, sibling = (x, y, c), (x, y, 1 - c)

        def rows(px, py, pc):
            return all_ref.at[pl.ds((4 * px + 2 * py + pc) * M, M), :]

        def copy(k, block, to, src=None):
            return _remote(rows(*block) if src is None else src, rows(*block), send_sems, recv_sems, k, to)

        mine = pltpu.make_async_copy(x_ref, rows(*me), local_sem)
        mine.start()
        first = [copy(0, me, sibling, src=x_ref)]
        first += [copy(1 + j, me, (*chip, c), src=x_ref) for j, chip in enumerate(chips)]
        for cp in first:
            cp.start()
        passed = [copy(4 + j, (*chip, c), sibling) for j, chip in enumerate(chips)]
        for j, chip in enumerate(chips):
            copy(1 + j, (*chip, c), me).wait_recv()
            passed[j].start()
        copy(0, sibling, me).wait_recv()
        for j, chip in enumerate(chips):
            copy(4 + j, (*chip, 1 - c), me).wait_recv()
        for cp in first + passed:
            cp.wait_send()
        mine.wait()
        tot = all_ref[0:M, :]
        for d in range(1, 8):
            tot = tot + all_ref[d * M:(d + 1) * M, :]
        tot_ref[...] = tot
        f = tot[fold_at:fold_at + fold_rows, :]
        for e in range(1, n_fold):
            f = f + tot[fold_at + e * fold_rows:fold_at + (e + 1) * fold_rows, :]
        fold_ref[...] = f

    vm = pl.BlockSpec(memory_space=pltpu.VMEM)
    return pl.pallas_call(
        body, name="all_reduce_small", in_specs=[vm], out_specs=[vm, vm],
        out_shape=[jax.ShapeDtypeStruct((M, N), F32), jax.ShapeDtypeStruct((fold_rows, N), F32)],
        scratch_shapes=[pltpu.VMEM((8 * M, N), F32), pltpu.SemaphoreType.DMA((7,)), pltpu.SemaphoreType.DMA((7,)),
                        pltpu.SemaphoreType.DMA],
        compiler_params=pltpu.CompilerParams(has_side_effects=True, vmem_limit_bytes=VMEM_LIMIT),
    )(v)


def _as_rows(a, width):
    flat = a.reshape(-1)
    pad = (-flat.shape[0]) % width
    if pad:
        flat = jnp.concatenate([flat, jnp.zeros((pad,), flat.dtype)])
    return flat.reshape(-1, width)


class _Layout:
    def __init__(self, width, total_mult):
        self.width, self.total_mult = width, total_mult
        self.offsets, self.shapes, self.rows = {}, {}, 0

    def add(self, name, shape):
        r = -(-math.prod(shape) // self.width)
        self.offsets[name], self.shapes[name] = (self.rows, r), tuple(shape)
        self.rows += r

    def align(self, mult):
        gap = (-self.rows) % mult
        if gap:
            self.offsets[f"_gap{self.rows}"], self.shapes[f"_gap{self.rows}"] = (self.rows, gap), (gap, self.width)
            self.rows += gap
        return self.rows

    def pack(self, pieces):
        self.align(self.total_mult)
        parts = [_as_rows(pieces[n].astype(F32), self.width) if n in pieces else jnp.zeros(self.shapes[n], F32)
                 for n in self.offsets]
        return jnp.concatenate(parts, axis=0)

    def unpack(self, buf, name):
        off, r = self.offsets[name]
        shape = self.shapes[name]
        return buf[off:off + r].reshape(-1)[:math.prod(shape)].reshape(shape)


_BIG = ["ffn1_w1", "ffn1_w3", "ffn1_w2", "w_in", "ssm_glu_a", "ssm_glu_b", "w_out", "ffn2_w1", "ffn2_w3", "ffn2_w2"]
_TRANSPOSED = {"ffn1_w1", "ffn1_w3", "ffn2_w1", "ffn2_w3"}
_SMALL = ["ffn1_norm", "mix_norm", "ffn2_norm", "final_norm", "attn_sinks", "ssm_a_re", "ssm_a_im", "ssm_log_step",
          "ssm_b_re", "ssm_b_im", "ssm_c_re", "ssm_c_im", "ssm_d"]
_WEIGHTS = ["meta_tokens", "ffn1_norm", "ffn1_w1", "ffn1_w3", "ffn1_w2", "mix_norm", "w_in", "attn_sinks", "ssm_a_re",
            "ssm_a_im", "ssm_log_step", "ssm_b_re", "ssm_b_im", "ssm_c_re", "ssm_c_im", "ssm_d", "ssm_glu_a",
            "ssm_glu_b", "w_out", "ffn2_norm", "ffn2_w1", "ffn2_w3", "ffn2_w2", "final_norm"]


def _kv_interleave(w, kv_heads):
    kvw = kv_heads * HEAD_DIM
    lead = w.shape[:-1]
    k = w[..., 0:kvw].reshape(lead + (kv_heads, 1, HEAD_DIM))
    v = w[..., kvw:2 * kvw].reshape(lead + (kv_heads, 1, HEAD_DIM))
    return jnp.concatenate([jnp.concatenate([k, v], axis=-2).reshape(lead + (2 * kvw,)), w[..., 2 * kvw:]], axis=-1)


def _kv_deinterleave(w, kv_heads):
    kvw = kv_heads * HEAD_DIM
    lead = w.shape[:-1]
    kv = w[..., 0:2 * kvw].reshape(lead + (kv_heads, 2, HEAD_DIM))
    return jnp.concatenate([kv[..., 0, :].reshape(lead + (kvw,)), kv[..., 1, :].reshape(lead + (kvw,)), w[..., 2 * kvw:]],
                           axis=-1)


def _step(x, target, w, m, v):
    B, S, D = x.shape
    L = S + N_META
    T = B * L
    H = D // HEAD_DIM
    KV = H // Q_PER_KV
    SW = D // 2
    tm = _pick_tile(L, ROW_TILE_CAP, ROW_ALIGN)
    rc = _pick_tile(L, ROW_TILE_CAP // B, 4) * B
    tw = _pick_tile(T, 3 * ROW_TILE_CAP, ROW_ALIGN)
    my_c = lax.axis_index("c")
    my_slot = 2 * lax.axis_index("x") + lax.axis_index("y")

    groups = {"ffn1": ["ffn1_w1", "ffn1_w3", "ffn1_w2"], "mix": ["w_in", "ssm_glu_a", "ssm_glu_b", "w_out"],
              "ffn2": ["ffn2_w1", "ffn2_w3", "ffn2_w2"]}
    waves = {"first": ["ffn1_w1", "ffn1_w3"], "early": ["ffn1_w2"] + groups["mix"], "late": groups["ffn2"]}
    def own_layout(a, n):
        return jnp.swapaxes(a[0], 0, 1) if n in _TRANSPOSED else a[0]

    shards = {n: own_layout(w[n], n).astype(BF16) for n in _BIG}
    gathered = _gather_weights([shards[n] for n in waves["first"]] + [w["meta_tokens"]])
    ws = dict(zip(waves["first"], gathered[:-1]))
    meta = jnp.transpose(gathered[-1], (1, 0, 2)).reshape(N_META, D)

    def arrive(wave, landed):
        mine = [shards[n] for n in waves[wave]]
        ws.update(zip(waves[wave], _forward_halves(landed, mine, wave)))

    g_ffn1, g_mix, g_ffn2 = w["ffn1_norm"], w["mix_norm"], w["ffn2_norm"]
    g_final = w["final_norm"].reshape(1, D)

    h0 = jnp.concatenate([jnp.broadcast_to(meta[None], (B, N_META, D)), x], axis=1).reshape(T, D)

    def ffn_fwd(h, g, tag, carry=None):
        n = _rmsnorm_fwd(h, g, tm, f"{tag}_norm")
        riders = None if carry is None else _gather_riders([shards[k] for k in waves[carry]])
        out = _ffn_up(n, ws[f"{tag}_w1"], ws[f"{tag}_w3"], tm, f"{tag}_up", riders)
        if carry is not None:
            out, landed = out
            arrive(carry, landed)
        a, c, s = out
        return _ffn_down(s, ws[f"{tag}_w2"], h, tm, f"{tag}_down"), (n, a, c, s)

    h1, saved1 = ffn_fwd(h0, g_ffn1, "ffn1", carry="early")
    w_kvu = _kv_interleave(ws["w_in"][1], KV)
    hn = _rmsnorm_fwd(h1, g_mix, tm, "mix_norm")
    q = _mm_colslots(hn, ws["w_in"], BF16, "w_in_q", tm, first=0, count=1, scale=HEAD_DIM ** -0.5)
    kvu = _mm_plain(hn, w_kvu, "nn", F32, "w_in_kvu", tm)
    gates = _mm_colslots(hn, ws["w_in"], F32, "w_in_gates", tm, first=2, count=2)

    sinks = w["attn_sinks"].reshape(KV, Q_PER_KV, 1, 1)
    sink_row = jnp.broadcast_to(sinks.reshape(KV, 1, Q_PER_KV, 1), (KV, 1, Q_PER_KV, BLOCK)).reshape(KV, 1, Q_PER_KV * BLOCK)
    sink_meta = jnp.broadcast_to(sinks, (KV, Q_PER_KV, N_META, 1)).reshape(KV, Q_PER_KV * N_META, 1)
    (attn,), landed = _attn_fwd(q, kvu, sink_row, sink_meta, B, "attn_fwd",
                                _gather_riders([shards[k] for k in waves["late"]]))
    arrive("late", landed)

    def to_time_major(a2d):
        return jnp.transpose(a2d.reshape(B, L, a2d.shape[-1]), (1, 0, 2)).reshape(T, a2d.shape[-1])

    def to_batch_major(a2d):
        return jnp.transpose(a2d.reshape(L, B, a2d.shape[-1]), (1, 0, 2)).reshape(T, a2d.shape[-1])

    ssm_args = (w["ssm_a_re"][0], w["ssm_a_im"][0], w["ssm_log_step"][0], w["ssm_b_re"][0], w["ssm_b_im"][0],
                w["ssm_c_re"][0], w["ssm_c_im"][0])
    (lam, bmat, cmat), ssm_vjp = jax.vjp(_ssm_matrices, *ssm_args)
    bmat16, cmat16 = bmat.astype(BF16), cmat.astype(BF16)
    u_t = to_time_major(kvu[:, SW:])
    y_t, xs = _ssm_fwd(u_t, bmat16, cmat16, w["ssm_d"], _scan_tables(lam, B, False), B, rc, "ssm_fwd")
    y0 = to_batch_major(y_t)
    yg = _gelu_fwd(y0, tm, "gelu_fwd")
    ga = _mm_colslots(yg, ws["ssm_glu_a"], F32, "glu_a", tm)
    gb = _mm_colslots(yg, ws["ssm_glu_b"], F32, "glu_b", tm)
    merged = _merge_fwd(gates, attn, ga, gb, tm, "merge_fwd")
    h2 = _mm_rowslots(merged, ws["w_out"], h1, tm, "w_out")
    h3, saved2 = ffn_fwd(h2, g_ffn2, "ffn2")
    dh3, dh3b, dg_final, loss_row = _loss_head(h3, g_final, target, tm, "loss_head")

    grads, swapped, received = {}, {}, {}
    c_idx = my_c.reshape(1).astype(jnp.int32)
    idx = jnp.stack([my_slot, my_c]).astype(jnp.int32)

    def swap_riders(group):
        return _swap_riders([grads[n] for n in groups[group]])

    def exchange_riders(group):
        names = groups[group]
        if names[0] not in swapped:
            swapped.update(zip(names, _swap_halves([grads[n] for n in names], group)))
        return _exchange_riders([_sum_halves(grads[n], swapped[n], c_idx, f"grad_sum_halves_{n}") for n in names])

    def ffn_bwd(h, g, saved, dh, dhb, tag, dhidden_carries=None, dn_carries=None):
        n, a, c, s = saved
        w1, w3, w2 = ws[f"{tag}_w1"], ws[f"{tag}_w3"], ws[f"{tag}_w2"]
        grads[f"{tag}_w2"] = _wgrad_hidden_rows(s, dhb, tw, f"{tag}_dw2", 0.5)
        if dhidden_carries is None:
            da, dc = _ffn_dhidden(dhb, w2, a, c, tm, f"{tag}_dhidden")
        else:
            (da, dc), got = _ffn_dhidden(dhb, w2, a, c, tm, f"{tag}_dhidden", exchange_riders(dhidden_carries[1]))
            received.update(zip(groups[dhidden_carries[1]], got))
        grads[f"{tag}_w1"] = _wgrad_hidden_rows(da, n, tw, f"{tag}_dw1", 1.0)
        grads[f"{tag}_w3"] = _wgrad_hidden_rows(dc, n, tw, f"{tag}_dw3", 1.0)
        kind, group = dn_carries
        riders = swap_riders(group) if kind == "swap" else exchange_riders(group)
        (dh_in, dhb_in, grads[f"{tag}_norm"]), got = _ffn_dn(da, w1, dc, w3, h, g, dh, tm, f"{tag}_dn", riders)
        return dh_in, dhb_in, got

    dh2, dh2b, got = ffn_bwd(h2, g_ffn2, saved2, dh3, dh3b, "ffn2", dn_carries=("swap", "ffn2"))
    swapped.update(zip(groups["ffn2"], got))

    grads["w_out"] = _wgrad_rowslots(merged, dh2b, tw, "dw_out")
    dattn, dgat, dgss, dga, dgb = _merge_bwd(dh2b, ws["w_out"], gates, attn, ga, gb, tm, "merge_bwd")
    grads["ssm_glu_a"] = _wgrad_colslots(yg, dga, tw, "dglu_a")
    grads["ssm_glu_b"] = _wgrad_colslots(yg, dgb, tw, "dglu_b")
    dy0 = _gelu_bwd([(dga, ws["ssm_glu_a"]), (dgb, ws["ssm_glu_b"])], y0, tm, "gelu_bwd")
    du_t, dbmat, dcmat, dlam, dd = _ssm_bwd(to_time_major(dy0), u_t, xs, bmat16, cmat16, w["ssm_d"],
                                            _scan_tables(lam, B, True), B, rc, "ssm_bwd")
    d_ssm = ssm_vjp((dlam[:, 0, :], dbmat, dcmat))
    for n, gval in zip(["ssm_a_re", "ssm_a_im", "ssm_log_step", "ssm_b_re", "ssm_b_im", "ssm_c_re", "ssm_c_im"], d_ssm):
        grads[n] = gval[None]
    grads["ssm_d"] = dd

    (dq, dkv, dsink), got = _attn_bwd(q, kvu, attn, dattn, sink_row, sink_meta, B, "attn_bwd",
                                      exchange_riders("ffn2"))
    received.update(zip(groups["ffn2"], got))
    grads["attn_sinks"] = dsink[:, 0:Q_PER_KV, 0].reshape(1, H)
    dkvu = jnp.concatenate([dkv, to_batch_major(du_t).astype(BF16)], axis=1)
    pieces = [dq, dkvu, dgat, dgss]
    dw_in = [_wgrad_plain(hn, p, f"dw_in_{k}", tw) for k, p in enumerate(pieces)]
    dw_in[1] = _kv_deinterleave(dw_in[1], KV)
    grads["w_in"] = jnp.stack(dw_in)
    w_in_parts = [ws["w_in"][0], w_kvu, ws["w_in"][2], ws["w_in"][3]]
    whole = _once((D, D), lambda i: (0, 0))
    (dh1, dh1b, grads["mix_norm"]), swap_mix = _mm_norm_bwd(
        "dhn", "nt", [(p, _spec((tm, D), lambda i: (i, 0)), wp, whole) for p, wp in zip(pieces, w_in_parts)],
        h1, g_mix, dh2, tm, swap_riders("mix"))
    swapped.update(zip(groups["mix"], swap_mix))
    dh0, _, got = ffn_bwd(h0, g_ffn1, saved1, dh1, dh1b, "ffn1", dhidden_carries=("exchange", "mix"),
                          dn_carries=("exchange", "ffn1"))
    received.update(zip(groups["ffn1"], got))
    dh0 = dh0.reshape(B, L, D)
    grad_x = dh0[:, N_META:, :]

    grads["final_norm"] = dg_final
    slay = _Layout(D, 8)
    for n in _SMALL:
        slay.add(n, w[n].shape)
    slay.add("loss", (1, D))
    meta_at = slay.align(8)
    slay.add("meta", (B * N_META, D))
    small = slay.pack({**{n: grads[n] for n in _SMALL}, "loss": loss_row, "meta": dh0[:, :N_META, :]})
    tot_small, dmeta = _all_reduce_small(small, B, N_META, meta_at)
    loss = slay.unpack(tot_small, "loss")[0, 0]
    for n in _SMALL:
        grads[n] = slay.unpack(tot_small, n)
    cw = D // N_CHIPS
    grads["meta_tokens"] = lax.dynamic_slice_in_dim(dmeta, my_slot * cw, cw, axis=1)

    fulls = [_sum_chips(grads[n], swapped[n], received[n], idx, f"grad_sum_chips_{n}") for n in _BIG]
    for n, f in zip(_BIG, _join_halves(fulls)):
        grads[n] = f

    delta, new_m, new_v = {}, {}, {}
    for n in _BIG + ["meta_tokens"]:
        if n in _TRANSPOSED:
            flip = lambda a: jnp.swapaxes(a, -1, -2)
            outs = _adamw(flip(w[n]), grads[n], flip(m[n]), flip(v[n]), f"adamw_{n}")
            delta[n], new_m[n], new_v[n] = (flip(o) for o in outs)
            grads[n] = flip(grads[n])[None]
        else:
            delta[n], new_m[n], new_v[n] = _adamw(w[n], grads[n], m[n], v[n], f"adamw_{n}")
            grads[n] = grads[n].reshape(w[n].shape)

    def flat2d(a):
        return a.reshape(-1, a.shape[-1])

    d_, m_, v_ = _adamw_small([flat2d(w[n]) for n in _SMALL], [flat2d(grads[n]) for n in _SMALL],
                              [flat2d(m[n]) for n in _SMALL], [flat2d(v[n]) for n in _SMALL], "adamw_small")
    for i, n in enumerate(_SMALL):
        shp = w[n].shape
        delta[n], new_m[n], new_v[n] = d_[i].reshape(shp), m_[i].reshape(shp), v_[i].reshape(shp)
        grads[n] = grads[n].reshape(shp)

    return (loss, grad_x, *[grads[n] for n in _WEIGHTS], *[delta[n] for n in _WEIGHTS],
            *[new_m[n] for n in _WEIGHTS], *[new_v[n] for n in _WEIGHTS])


def kernel(x, meta_tokens, ffn1_norm, ffn1_w1, ffn1_w3, ffn1_w2, mix_norm, w_in, attn_sinks, ssm_a_re, ssm_a_im, ssm_log_step, ssm_b_re, ssm_b_im, ssm_c_re, ssm_c_im, ssm_d, ssm_glu_a, ssm_glu_b, w_out, ffn2_norm, ffn2_w1, ffn2_w3, ffn2_w2, final_norm, loss_target, m_meta_tokens, m_ffn1_norm, m_ffn1_w1, m_ffn1_w3, m_ffn1_w2, m_mix_norm, m_w_in, m_attn_sinks, m_ssm_a_re, m_ssm_a_im, m_ssm_log_step, m_ssm_b_re, m_ssm_b_im, m_ssm_c_re, m_ssm_c_im, m_ssm_d, m_ssm_glu_a, m_ssm_glu_b, m_w_out, m_ffn2_norm, m_ffn2_w1, m_ffn2_w3, m_ffn2_w2, m_final_norm, v_meta_tokens, v_ffn1_norm, v_ffn1_w1, v_ffn1_w3, v_ffn1_w2, v_mix_norm, v_w_in, v_attn_sinks, v_ssm_a_re, v_ssm_a_im, v_ssm_log_step, v_ssm_b_re, v_ssm_b_im, v_ssm_c_re, v_ssm_c_im, v_ssm_d, v_ssm_glu_a, v_ssm_glu_b, v_w_out, v_ffn2_norm, v_ffn2_w1, v_ffn2_w3, v_ffn2_w2, v_final_norm):
    args = locals()
    w = {n: args[n] for n in _WEIGHTS}
    m = {n: args["m_" + n] for n in _WEIGHTS}
    v = {n: args["v_" + n] for n in _WEIGHTS}
    return _step(x, loss_target, w, m, v)
```

```python
import functools
import math

import jax
import jax.numpy as jnp
from jax import lax
from jax.experimental import pallas as pl
from jax.experimental.pallas import tpu as pltpu

F32 = jnp.float32
BF16 = jnp.bfloat16
MESH_IDS = pl.DeviceIdType.MESH

N_CHIPS = 4
N_META = 16
HEAD_DIM = 64
Q_PER_KV = 4
QW = Q_PER_KV * HEAD_DIM
BLOCK = 128
SSM_GROUP = 16
SSM_STATE = 64
SSM_LANES = 128
GROUPS_PER_COL = SSM_LANES // SSM_GROUP
STATE_LANES = GROUPS_PER_COL * SSM_STATE
NORM_EPS = 1e-6
NEG_INF = -1e30
ADAM_LR, ADAM_B1, ADAM_B2, ADAM_EPS, ADAM_WD, ADAM_STEP = 0.001, 0.9, 0.999, 1e-08, 0.01, 10
GELU_C = math.sqrt(2.0 / math.pi)
ROW_ALIGN = 16
VMEM_LIMIT = 56 * 1024 * 1024
ROW_TILE_CAP = 688

_NN = (((1,), (0,)), ((), ()))
_NT = (((1,), (1,)), ((), ()))
_TN = (((0,), (0,)), ((), ()))
_DIMS = {"nn": _NN, "nt": _NT, "tn": _TN}


def _params(sem, **kw):
    return pltpu.CompilerParams(dimension_semantics=sem, vmem_limit_bytes=VMEM_LIMIT, **kw)


def _pick_tile(n, cap, mult):
    best = None
    for t in range(mult, min(n, cap) + 1, mult):
        if n % t == 0:
            best = t
    if best is None:
        raise ValueError(f"no tile for {n} (cap {cap}, multiple of {mult})")
    return best


def _sigmoid(x):
    return 0.5 * jnp.tanh(0.5 * x) + 0.5


def _spec(block, index_map):
    return pl.BlockSpec(block, index_map)


def _sum_dots(ins, mode):
    tot = None
    for p in range(len(ins) // 2):
        a_ref, b_ref = ins[2 * p], ins[2 * p + 1]
        for sl in ([None] if len(b_ref.shape) == 2 else range(b_ref.shape[0])):
            if sl is None:
                a, b = a_ref[...], b_ref[...]
            elif len(a_ref.shape) == 3:
                a, b = a_ref[sl], b_ref[sl]
            else:
                width = a_ref.shape[1] // b_ref.shape[0]
                a, b = a_ref[:, sl * width:(sl + 1) * width], b_ref[sl]
            d = lax.dot_general(a.astype(BF16), b.astype(BF16), _DIMS[mode], preferred_element_type=F32)
            tot = d if tot is None else tot + d
    return tot


def _mm(name, grid, kaxis, mode, pairs, out_shape, out_spec, scale=1.0, res=None):
    npairs = len(pairs)
    has_res = res is not None
    gk = 1 if kaxis is None else grid[kaxis]
    acc_shape = tuple(d for d in out_spec.block_shape if d is not None)

    def body(*refs):
        res_ref = refs[2 * npairs] if has_res else None
        o_ref = refs[2 * npairs + has_res]
        tot = _sum_dots(refs[:2 * npairs], mode)

        def finish(acc):
            r = acc * scale if scale != 1.0 else acc
            if has_res:
                r = res_ref[...] + r
            o_ref[...] = r.astype(o_ref.dtype)

        if gk == 1:
            finish(tot)
        else:
            acc_ref = refs[-1]
            k = pl.program_id(kaxis)

            @pl.when(k == 0)
            def _():
                acc_ref[...] = tot

            @pl.when(k > 0)
            def _():
                acc_ref[...] += tot

            @pl.when(k == gk - 1)
            def _():
                finish(acc_ref[...])

    in_specs, args = [], []
    for a, a_spec, b, b_spec in pairs:
        in_specs += [a_spec, b_spec]
        args += [a, b]
    if has_res:
        in_specs.append(res[1])
        args.append(res[0])
    sem = tuple("arbitrary" if ax == kaxis else "parallel" for ax in range(len(grid)))
    return pl.pallas_call(
        body, name=name, grid=grid, in_specs=in_specs, out_specs=out_spec, out_shape=out_shape,
        scratch_shapes=[pltpu.VMEM(acc_shape, F32)] if gk > 1 else [],
        compiler_params=_params(sem),
    )(*args)


def _mm_plain(a, b, mode, out_dtype, name, tm, scale=1.0):
    M, K = a.shape
    N = b.shape[1] if mode == "nn" else b.shape[0]
    return _mm(name, (M // tm,), None, mode,
               [(a, _spec((tm, K), lambda i: (i, 0)), b, _spec(b.shape, lambda i: (0, 0)))],
               jax.ShapeDtypeStruct((M, N), out_dtype), _spec((tm, N), lambda i: (i, 0)), scale=scale)


def _wgrad_plain(a, b, name, tr):
    R, M = a.shape
    N = b.shape[1]
    return _mm(name, (R // tr,), 0, "tn",
               [(a, _spec((tr, M), lambda r: (r, 0)), b, _spec((tr, N), lambda r: (r, 0)))],
               jax.ShapeDtypeStruct((M, N), F32), _spec((M, N), lambda r: (0, 0)))


def _for_real_rows(tile, tpe, tm, hbm, buf, fn):
    tb, tj = tile // tpe, tile % tpe

    @pl.when(tj == 0)
    def _():
        fn(hbm.at[tb, pl.ds(0, tm - N_META), :], buf.at[pl.ds(N_META, tm - N_META), :])

    @pl.when(tj > 0)
    def _():
        fn(hbm.at[tb, pl.ds(tj * tm - N_META, tm), :], buf)


def _embed_norm(x, meta, g, tm, name):
    B, S, D = x.shape
    L = S + N_META
    T = B * L
    nt = T // tm
    tpe = L // tm

    def body(x_hbm, meta_ref, g_ref, h_ref, n_ref, xbuf, sems):
        i = pl.program_id(0)
        slot = i % 2

        def fetch(tile, sl, act):
            _for_real_rows(tile, tpe, tm, x_hbm, xbuf.at[sl], lambda src, dst: act(pltpu.make_async_copy(src, dst, sems.at[sl])))

        @pl.when(i == 0)
        def _():
            fetch(i, slot, lambda cp: cp.start())

        @pl.when(i + 1 < nt)
        def _():
            fetch(i + 1, 1 - slot, lambda cp: cp.start())

        fetch(i, slot, lambda cp: cp.wait())

        @pl.when(i % tpe == 0)
        def _():
            xbuf[slot, 0:N_META, :] = meta_ref[...]

        hv = xbuf[slot]
        h_ref[...] = hv
        r = lax.rsqrt(jnp.mean(hv * hv, axis=-1, keepdims=True) + NORM_EPS)
        n_ref[...] = ((hv * r) * g_ref[...]).astype(BF16)

    row = pl.BlockSpec((tm, D), lambda i: (i, 0))
    return pl.pallas_call(
        body, name=name, grid=(nt,),
        in_specs=[pl.BlockSpec(memory_space=pl.ANY), pl.BlockSpec((N_META, D), lambda i: (0, 0)),
                  pl.BlockSpec((1, D), lambda i: (0, 0))],
        out_specs=[row, row],
        out_shape=[jax.ShapeDtypeStruct((T, D), F32), jax.ShapeDtypeStruct((T, D), BF16)],
        scratch_shapes=[pltpu.VMEM((2, tm, D), F32), pltpu.SemaphoreType.DMA((2,))],
        compiler_params=_params(("arbitrary",)),
    )(x, meta, g)


def _rmsnorm_fwd(h, g, tm, name):
    T, D = h.shape

    def body(h_ref, g_ref, o_ref):
        x = h_ref[...]
        r = lax.rsqrt(jnp.mean(x * x, axis=-1, keepdims=True) + NORM_EPS)
        o_ref[...] = ((x * r) * g_ref[...]).astype(BF16)

    return pl.pallas_call(
        body, name=name, grid=(T // tm,),
        in_specs=[pl.BlockSpec((tm, D), lambda i: (i, 0)), pl.BlockSpec((1, D), lambda i: (0, 0))],
        out_specs=pl.BlockSpec((tm, D), lambda i: (i, 0)),
        out_shape=jax.ShapeDtypeStruct((T, D), BF16),
        compiler_params=_params(("parallel",)),
    )(h, g)


def _fold8(x):
    return jnp.sum(x.reshape(x.shape[0] // 8, 8, x.shape[1]), axis=0)


def _mm_norm_bwd(name, mode, pairs, h, g, dres, tm, riders=None, examples=None):
    T, D = h.shape
    nt = T // tm
    npairs = len(pairs)
    tpe = None if examples is None else T // examples // tm

    def body(*refs):
        if examples is None:
            h_ref, g_ref, dres_ref, dh_ref, dhb_ref, dg_ref, acc_ref = refs[2 * npairs:]
        else:
            h_ref, g_ref, dres_ref, dx_hbm, dmeta_ref, dg_ref, acc_ref, dbuf, sems = refs[2 * npairs:]
        i = pl.program_id(0)
        x = h_ref[...]
        r = lax.rsqrt(jnp.mean(x * x, axis=-1, keepdims=True) + NORM_EPS)
        xhat = x * r
        dy = _sum_dots(refs[:2 * npairs], mode)
        dxhat = dy * g_ref[...]
        dx = r * (dxhat - xhat * jnp.mean(dxhat * xhat, axis=-1, keepdims=True))
        dh = dres_ref[...] + dx
        if examples is None:
            dh_ref[...] = dh
            dhb_ref[...] = dh.astype(BF16)
        else:
            slot = i % 2

            def push(tile, sl, act):
                _for_real_rows(tile, tpe, tm, dx_hbm, dbuf.at[sl],
                               lambda dst, src: act(pltpu.make_async_copy(src, dst, sems.at[sl])))

            dbuf[slot] = dh
            push(i, slot, lambda cp: cp.start())

            @pl.when(i > 0)
            def _():
                push(i - 1, 1 - slot, lambda cp: cp.wait())

            @pl.when(i == nt - 1)
            def _():
                push(i, slot, lambda cp: cp.wait())

            @pl.when(i % tpe == 0)
            def _():
                dmeta_ref[...] = dh[0:N_META]
        part = _fold8(dy * xhat)

        @pl.when(i == 0)
        def _():
            acc_ref[...] = part

        @pl.when(i > 0)
        def _():
            acc_ref[...] += part

        @pl.when(i == nt - 1)
        def _():
            dg_ref[...] = jnp.sum(acc_ref[...], axis=0, keepdims=True)

    row = pl.BlockSpec((tm, D), lambda i: (i, 0))
    vec = pl.BlockSpec((1, D), lambda i: (0, 0))
    in_specs, args = [], []
    for a, a_spec, b, b_spec in pairs:
        in_specs += [a_spec, b_spec]
        args += [a, b]
    if examples is None:
        return _call(body, name, (nt,), in_specs + [row, vec, row], [row, row, vec],
                     [jax.ShapeDtypeStruct((T, D), F32), jax.ShapeDtypeStruct((T, D), BF16), jax.ShapeDtypeStruct((1, D), F32)],
                     [pltpu.VMEM((8, D), F32)], ("arbitrary",), (*args, h, g, dres), riders)
    S = T // examples - N_META
    return _call(body, name, (nt,), in_specs + [row, vec, row],
                 [_ANY, pl.BlockSpec((N_META, D), lambda i: (i // tpe, 0)), vec],
                 [jax.ShapeDtypeStruct((examples, S, D), F32), jax.ShapeDtypeStruct((examples * N_META, D), F32),
                  jax.ShapeDtypeStruct((1, D), F32)],
                 [pltpu.VMEM((8, D), F32), pltpu.VMEM((2, tm, D), F32), pltpu.SemaphoreType.DMA((2,))],
                 ("arbitrary",), (*args, h, g, dres), riders)


def _ffn_up(n, w1t, w3t, tm, name, riders=None):
    T, D = n.shape
    Fs = w1t.shape[1]

    def body(n_ref, w1_ref, w3_ref, a_ref, c_ref, s_ref):
        x = n_ref[...]
        a = lax.dot_general(x, w1_ref[...], _NT, preferred_element_type=F32)
        c = lax.dot_general(x, w3_ref[...], _NT, preferred_element_type=F32)
        a_ref[...] = a.astype(BF16)
        c_ref[...] = c.astype(BF16)
        s_ref[...] = (a * _sigmoid(a) * c).astype(BF16)

    w_spec = _spec((None, Fs, D), lambda s, i: (s, 0, 0))
    o_spec = _spec((None, tm, Fs), lambda s, i: (s, i, 0))
    o_shape = jax.ShapeDtypeStruct((N_CHIPS, T, Fs), BF16)
    return _call(body, name, (N_CHIPS, T // tm), [_spec((tm, D), lambda s, i: (i, 0)), w_spec, w_spec],
                 [o_spec, o_spec, o_spec], [o_shape, o_shape, o_shape], [], ("parallel", "parallel"), (n, w1t, w3t), riders)


def _ffn_down(s, w2, h, tm, name):
    _, T, Fs = s.shape
    D = w2.shape[2]
    row = _spec((tm, D), lambda i: (i, 0))
    return _mm(name, (T // tm,), None, "nn",
               [(s, _spec((N_CHIPS, tm, Fs), lambda i: (0, i, 0)), w2, _spec((N_CHIPS, Fs, D), lambda i: (0, 0, 0)))],
               jax.ShapeDtypeStruct((T, D), F32), row, scale=0.5, res=(h, row))


def _ffn_dhidden(dhb, w2, a, c, tm, name, riders=None):
    T, D = dhb.shape
    Fs = w2.shape[1]

    def body(dh_ref, w2_ref, a_ref, c_ref, da_ref, dc_ref):
        d = 0.5 * lax.dot_general(dh_ref[...], w2_ref[...], _NT, preferred_element_type=F32)
        av = a_ref[...].astype(F32)
        cv = c_ref[...].astype(F32)
        sg = _sigmoid(av)
        da_ref[...] = (d * cv * (sg * (1.0 + av * (1.0 - sg)))).astype(BF16)
        dc_ref[...] = (d * (av * sg)).astype(BF16)

    h_spec = _spec((None, tm, Fs), lambda s, i: (s, i, 0))
    o_shape = jax.ShapeDtypeStruct((N_CHIPS, T, Fs), BF16)
    return _call(body, name, (N_CHIPS, T // tm),
                 [_spec((tm, D), lambda s, i: (i, 0)), _spec((None, Fs, D), lambda s, i: (s, 0, 0)), h_spec, h_spec],
                 [h_spec, h_spec], [o_shape, o_shape], [], ("parallel", "parallel"), (dhb, w2, a, c), riders)


def _wgrad_hidden_rows(s, dhb, tr, name, scale):
    _, T, Fs = s.shape
    D = dhb.shape[1]
    return _mm(name, (N_CHIPS, T // tr), 1, "tn",
               [(s, _spec((None, tr, Fs), lambda k, r: (k, r, 0)), dhb, _spec((tr, D), lambda k, r: (r, 0)))],
               jax.ShapeDtypeStruct((N_CHIPS, Fs, D), F32), _spec((None, Fs, D), lambda k, r: (k, 0, 0)), scale=scale)


def _once(block, index_map):
    return pl.BlockSpec(block, index_map, pipeline_mode=pl.Buffered(1))


def _ffn_dn(da, w1t, dc, w3t, h, g, dres, tm, name, riders=None, examples=None):
    _, T, Fs = da.shape
    D = w1t.shape[2]
    h_spec = _spec((N_CHIPS, tm, Fs), lambda i: (0, i, 0))
    w_spec = _once((N_CHIPS, Fs, D), lambda i: (0, 0, 0))
    return _mm_norm_bwd(name, "nn", [(da, h_spec, w1t, w_spec), (dc, h_spec, w3t, w_spec)], h, g, dres, tm, riders,
                        examples)


def _mm_side_by_side(a, w, mode, out_dtype, name, tm, first=0, count=N_CHIPS, scale=1.0):
    T, K = a.shape
    assert first % count == 0
    n = w.shape[2] if mode == "nn" else w.shape[1]

    def body(a_ref, w_ref, o_ref):
        av = a_ref[...].astype(BF16)
        for j in range(count):
            r = lax.dot_general(av, w_ref[j].astype(BF16), _DIMS[mode], preferred_element_type=F32)
            o_ref[:, j * n:(j + 1) * n] = (r * scale if scale != 1.0 else r).astype(o_ref.dtype)

    return pl.pallas_call(
        body, name=name, grid=(T // tm,),
        in_specs=[_spec((tm, K), lambda i: (i, 0)), _once((count,) + w.shape[1:], lambda i: (first // count, 0, 0))],
        out_specs=_spec((tm, count * n), lambda i: (i, 0)),
        out_shape=jax.ShapeDtypeStruct((T, count * n), out_dtype),
        compiler_params=_params(("parallel",)),
    )(a, w)


def _mm_colslots(a, w, out_dtype, name, tm, first=0, count=N_CHIPS, scale=1.0):
    return _mm_side_by_side(a, w, "nn", out_dtype, name, tm, first, count, scale)


def _wgrad_colslots(a, d, tr, name):
    T, K = a.shape
    Ns = d.shape[1] // N_CHIPS
    return _mm(name, (N_CHIPS, T // tr), 1, "tn",
               [(a, _spec((tr, K), lambda k, r: (r, 0)), d, _spec((tr, Ns), lambda k, r: (r, k)))],
               jax.ShapeDtypeStruct((N_CHIPS, K, Ns), F32), _spec((None, K, Ns), lambda k, r: (k, 0, 0)))


def _mm_rowslots(a, w, h, tm, name):
    T = a.shape[0]
    N = w.shape[2]
    row = _spec((tm, N), lambda i: (i, 0))
    return _mm(name, (T // tm,), None, "nn",
               [(a, _spec((tm, a.shape[1]), lambda i: (i, 0)), w, _once(w.shape, lambda i: (0, 0, 0)))],
               jax.ShapeDtypeStruct((T, N), F32), row, res=(h, row))


def _wgrad_rowslots(a, d, tr, name):
    T = a.shape[0]
    Ks = a.shape[1] // N_CHIPS
    N = d.shape[1]
    return _mm(name, (N_CHIPS, T // tr), 1, "tn",
               [(a, _spec((tr, Ks), lambda k, r: (r, k)), d, _spec((tr, N), lambda k, r: (r, 0)))],
               jax.ShapeDtypeStruct((N_CHIPS, Ks, N), F32), _spec((None, Ks, N), lambda k, r: (k, 0, 0)))


def _gelu_parts(x):
    inner = GELU_C * (x + 0.044715 * (x * x * x))
    t = jnp.tanh(inner)
    return t, GELU_C * (1.0 + 3.0 * 0.044715 * (x * x))


def _gelu_fwd(y, tm, name):
    T, W = y.shape

    def body(y_ref, o_ref):
        x = y_ref[...]
        t, _ = _gelu_parts(x)
        o_ref[...] = (0.5 * x * (1.0 + t)).astype(BF16)

    spec = pl.BlockSpec((tm, W), lambda i: (i, 0))
    return pl.pallas_call(body, name=name, grid=(T // tm,), in_specs=[spec], out_specs=spec,
                          out_shape=jax.ShapeDtypeStruct((T, W), BF16),
                          compiler_params=_params(("parallel",)))(y)


def _gelu_bwd(pairs, y, tm, name):
    T, W = y.shape
    npairs = len(pairs)

    def body(*refs):
        y_ref, o_ref = refs[2 * npairs], refs[2 * npairs + 1]
        x = y_ref[...]
        t, dinner = _gelu_parts(x)
        o_ref[...] = _sum_dots(refs[:2 * npairs], "nt") * (0.5 * (1.0 + t) + 0.5 * x * (1.0 - t * t) * dinner)

    spec = pl.BlockSpec((tm, W), lambda i: (i, 0))
    in_specs, args = [], []
    for d, w in pairs:
        in_specs += [_spec((tm, d.shape[1]), lambda i: (i, 0)), _once(w.shape, lambda i: (0, 0, 0))]
        args += [d, w]
    return pl.pallas_call(body, name=name, grid=(T // tm,), in_specs=in_specs + [spec], out_specs=spec,
                          out_shape=jax.ShapeDtypeStruct((T, W), F32),
                          compiler_params=_params(("parallel",)))(*args, y)


def _merge_cols(D):
    cb = 512 if D % 512 == 0 else D
    return cb, D // cb


def _merge_fwd(gates, attn, ga, gb, tm, name):
    T, D = attn.shape
    cb, nc = _merge_cols(D)

    def body(gat_ref, gss_ref, attn_ref, ga_ref, gb_ref, o_ref):
        ssm = ga_ref[...] * _sigmoid(gb_ref[...])
        o_ref[...] = (_sigmoid(gat_ref[...]) * attn_ref[...] + _sigmoid(gss_ref[...]) * ssm).astype(BF16)

    def col(block):
        return pl.BlockSpec((tm, cb), lambda i, j: (i, block * nc + j))

    return pl.pallas_call(
        body, name=name, grid=(T // tm, nc),
        in_specs=[col(0), col(1), col(0), col(0), col(0)],
        out_specs=col(0), out_shape=jax.ShapeDtypeStruct((T, D), BF16),
        compiler_params=_params(("parallel", "parallel")),
    )(gates, gates, attn, ga, gb)


def _merge_bwd(dhb, w_out, gates, attn, ga, gb, tm, name):
    T, D = attn.shape
    cb, nc = _merge_cols(D)
    Ks = w_out.shape[1]
    spb = cb // Ks

    def body(dh_ref, w_ref, gat_ref, gss_ref, attn_ref, ga_ref, gb_ref, dattn_ref, dgat_ref, dgss_ref, dga_ref, dgb_ref):
        dh = dh_ref[...]
        d = jnp.concatenate([lax.dot_general(dh, w_ref[s], _NT, preferred_element_type=F32) for s in range(spb)], axis=1)
        sa = _sigmoid(gat_ref[...])
        ss = _sigmoid(gss_ref[...])
        sb = _sigmoid(gb_ref[...])
        gav = ga_ref[...]
        dattn_ref[...] = d * sa
        dgat_ref[...] = (d * attn_ref[...] * (sa * (1.0 - sa))).astype(BF16)
        dgss_ref[...] = (d * (gav * sb) * (ss * (1.0 - ss))).astype(BF16)
        dssm = d * ss
        dga_ref[...] = (dssm * sb).astype(BF16)
        dgb_ref[...] = (dssm * gav * (sb * (1.0 - sb))).astype(BF16)

    def col(block):
        return pl.BlockSpec((tm, cb), lambda i, j: (i, block * nc + j))

    b16 = jax.ShapeDtypeStruct((T, D), BF16)
    return pl.pallas_call(
        body, name=name, grid=(T // tm, nc),
        in_specs=[pl.BlockSpec((tm, D), lambda i, j: (i, 0)), pl.BlockSpec((spb, Ks, D), lambda i, j: (j, 0, 0)),
                  col(0), col(1), col(0), col(0), col(0)],
        out_specs=[col(0)] * 5,
        out_shape=[jax.ShapeDtypeStruct((T, D), F32), b16, b16, b16, b16],
        compiler_params=_params(("parallel", "parallel")),
    )(dhb, w_out, gates, gates, attn, ga, gb)


def _loss_head(h, g, target, tm, name):
    T, D = h.shape
    B, S, _ = target.shape
    L = S + N_META
    nt = T // tm
    tpe = L // tm

    def body(h_ref, g_ref, t_hbm, dh_ref, dhb_ref, dg_ref, loss_ref, tbuf, acc_g, acc_l, sems):
        i = pl.program_id(0)
        j = i % tpe
        slot = i % 2

        def fetch(tile, sl, act):
            tb, tj = tile // tpe, tile % tpe

            @pl.when(tj == 0)
            def _():
                act(pltpu.make_async_copy(t_hbm.at[tb, pl.ds(0, tm - N_META), :],
                                          tbuf.at[sl, pl.ds(N_META, tm - N_META), :], sems.at[sl]))

            @pl.when(tj > 0)
            def _():
                act(pltpu.make_async_copy(t_hbm.at[tb, pl.ds(tj * tm - N_META, tm), :], tbuf.at[sl], sems.at[sl]))

        @pl.when(i == 0)
        def _():
            tbuf[:, 0:N_META, :] = jnp.zeros((2, N_META, D), F32)
            fetch(i, slot, lambda cp: cp.start())

        @pl.when(i + 1 < nt)
        def _():
            fetch(i + 1, 1 - slot, lambda cp: cp.start())

        fetch(i, slot, lambda cp: cp.wait())

        x = h_ref[...]
        gv = g_ref[...]
        r = lax.rsqrt(jnp.mean(x * x, axis=-1, keepdims=True) + NORM_EPS)
        xhat = x * r
        pos = j * tm + lax.broadcasted_iota(jnp.int32, (tm, 1), 0)
        err = jnp.where(pos >= N_META, xhat * gv - tbuf[slot], 0.0)
        dy = err * (1.0 / D)
        dxhat = dy * gv
        dh = r * (dxhat - xhat * jnp.mean(dxhat * xhat, axis=-1, keepdims=True))
        dh_ref[...] = dh
        dhb_ref[...] = dh.astype(BF16)
        pg = _fold8(dy * xhat)
        pe = _fold8(err * err)

        @pl.when(i == 0)
        def _():
            acc_g[...] = pg
            acc_l[...] = pe

        @pl.when(i > 0)
        def _():
            acc_g[...] += pg
            acc_l[...] += pe

        @pl.when(i == nt - 1)
        def _():
            dg_ref[...] = jnp.sum(acc_g[...], axis=0, keepdims=True)
            loss_ref[...] = jnp.full((1, D), (0.5 / D) * jnp.sum(acc_l[...]), F32)

    row = pl.BlockSpec((tm, D), lambda i: (i, 0))
    vec = pl.BlockSpec((1, D), lambda i: (0, 0))
    return pl.pallas_call(
        body, name=name, grid=(nt,),
        in_specs=[row, vec, pl.BlockSpec(memory_space=pl.ANY)], out_specs=[row, row, vec, vec],
        out_shape=[jax.ShapeDtypeStruct((T, D), F32), jax.ShapeDtypeStruct((T, D), BF16),
                   jax.ShapeDtypeStruct((1, D), F32), jax.ShapeDtypeStruct((1, D), F32)],
        scratch_shapes=[pltpu.VMEM((2, tm, D), F32), pltpu.VMEM((8, D), F32), pltpu.VMEM((8, D), F32),
                        pltpu.SemaphoreType.DMA((2,))],
        compiler_params=_params(("arbitrary",)),
    )(h, g, target)


def _heads_to_rows(blk):
    return jnp.concatenate([blk[:, g * HEAD_DIM:(g + 1) * HEAD_DIM] for g in range(Q_PER_KV)], axis=0)


def _rows_to_heads(x):
    rows = x.shape[0] // Q_PER_KV
    return jnp.concatenate([x[g * rows:(g + 1) * rows] for g in range(Q_PER_KV)], axis=1)


def _causal(R):
    kj = lax.broadcasted_iota(jnp.int32, (BLOCK, R), 0)
    qi = lax.broadcasted_iota(jnp.int32, (BLOCK, R), 1) & (BLOCK - 1)
    return kj <= qi


def _band_probs(s_band, s_m, sink):
    m = jnp.maximum(jnp.maximum(jnp.max(s_band, axis=0, keepdims=True), jnp.max(s_m, axis=0, keepdims=True)), sink)
    e_b, e_m, e_s = jnp.exp(s_band - m), jnp.exp(s_m - m), jnp.exp(sink - m)
    inv = 1.0 / (jnp.sum(e_b, axis=0, keepdims=True) + jnp.sum(e_m, axis=0, keepdims=True) + e_s)
    return e_b * inv, e_m * inv, e_s * inv


def _fold_band(tri, two):
    return jnp.where(tri, two[BLOCK:2 * BLOCK], two[0:BLOCK])


def _unfold_band(tri, band):
    return jnp.concatenate([jnp.where(tri, 0.0, band), jnp.where(tri, band, 0.0)], axis=0)


def _meta_probs(qm, k_m, sink_m):
    R = qm.shape[0]
    s = lax.dot_general(qm, k_m, _NT, preferred_element_type=F32)
    qi = lax.broadcasted_iota(jnp.int32, (R, N_META), 0) & (N_META - 1)
    kj = lax.broadcasted_iota(jnp.int32, (R, N_META), 1)
    s = jnp.where(kj <= qi, s, NEG_INF)
    m = jnp.maximum(jnp.max(s, axis=-1, keepdims=True), sink_m)
    e, e_s = jnp.exp(s - m), jnp.exp(sink_m - m)
    inv = 1.0 / (jnp.sum(e, axis=-1, keepdims=True) + e_s)
    return e * inv, e_s * inv


def _block_start(n):
    return pl.multiple_of(N_META + n * BLOCK, ROW_ALIGN)


def _kv(blk):
    return blk[:, 0:HEAD_DIM], blk[:, HEAD_DIM:2 * HEAD_DIM]


def _attn_fwd(q, kv, sink_row, sink_meta, B, name, riders=None):
    T, D = q.shape
    L = T // B
    KV = D // QW
    nb = (L - N_META) // BLOCK

    def body(q_ref, kv_ref, sk_ref, skm_ref, o_ref, kvs):
        kvs[...] = kv_ref[...].astype(BF16)
        k_m, v_m = _kv(kvs[0:N_META, :])
        p, _ = _meta_probs(_heads_to_rows(q_ref[0:N_META, :]), k_m, skm_ref[0])
        o_ref[0:N_META, :] = _rows_to_heads(jnp.dot(p.astype(BF16), v_m, preferred_element_type=F32))
        tri = _causal(Q_PER_KV * BLOCK)

        def block(cur, first, keys):
            k2, v2 = _kv(kvs[keys, :])
            qb = _heads_to_rows(q_ref[pl.ds(cur, BLOCK), :])
            st = lax.dot_general(k2, qb, _NT, preferred_element_type=F32)
            smt = lax.dot_general(k_m, qb, _NT, preferred_element_type=F32)
            s_band = jnp.where(tri, st, NEG_INF) if first else _fold_band(tri, st)
            p_b, p_m, _ = _band_probs(s_band, smt, sk_ref[0])
            p2 = (p_b if first else _unfold_band(tri, p_b)).astype(BF16)
            o = (lax.dot_general(p2, v2, _TN, preferred_element_type=F32)
                 + lax.dot_general(p_m.astype(BF16), v_m, _TN, preferred_element_type=F32))
            o_ref[pl.ds(cur, BLOCK), :] = _rows_to_heads(o)

        block(N_META, True, pl.ds(N_META, BLOCK))

        def step(n, carry):
            block(_block_start(n), False, pl.ds(_block_start(n - 1), 2 * BLOCK))
            return carry

        lax.fori_loop(1, nb, step, 0, unroll=5 if (nb - 1) % 5 == 0 else 1)

    q_spec = pl.BlockSpec((L, QW), lambda b, h: (b, h))
    return _call(body, name, (B, KV),
                 [q_spec, pl.BlockSpec((L, 2 * HEAD_DIM), lambda b, h: (b, h)),
                  pl.BlockSpec((1, 1, Q_PER_KV * BLOCK), lambda b, h: (h, 0, 0)),
                  pl.BlockSpec((1, Q_PER_KV * N_META, 1), lambda b, h: (h, 0, 0))],
                 [q_spec], [jax.ShapeDtypeStruct((T, D), F32)], [pltpu.VMEM((L, 2 * HEAD_DIM), BF16)],
                 ("parallel", "parallel"), (q, kv, sink_row, sink_meta), riders)


def _attn_bwd(q, kv, o, do, sink_row, sink_meta, B, name, riders=None):
    T, D = q.shape
    L = T // B
    KV = D // QW
    nb = (L - N_META) // BLOCK
    R = Q_PER_KV * BLOCK
    scale = HEAD_DIM ** -0.5

    def head_totals(col, rows_per_head):
        rid = lax.broadcasted_iota(jnp.int32, (8, 128), 0)
        out = jnp.zeros((8, 128), F32)
        for g in range(Q_PER_KV):
            out = out + jnp.where(rid == g, jnp.sum(col[g * rows_per_head:(g + 1) * rows_per_head, :]), 0.0)
        return out

    def body(q_ref, kv_ref, o_ref, do_ref, sk_ref, skm_ref, dq_ref, dkv_ref, dsk_ref, kvs, acc, acc_sink):
        b = pl.program_id(1)
        kvs[...] = kv_ref[...].astype(BF16)
        acc[...] = jnp.zeros_like(acc)
        k_m, v_m = _kv(kvs[0:N_META, :])

        qm = _heads_to_rows(q_ref[0:N_META, :])
        dom = _heads_to_rows(do_ref[0:N_META, :])
        delta = jnp.sum(dom * _heads_to_rows(o_ref[0:N_META, :]), axis=-1, keepdims=True)
        p, p_s = _meta_probs(qm, k_m, skm_ref[0])
        domb = dom.astype(BF16)
        ds = (p * (lax.dot_general(domb, v_m, _NT, preferred_element_type=F32) - delta)).astype(BF16)
        dq_ref[0:N_META, :] = _rows_to_heads(jnp.dot(ds, k_m, preferred_element_type=F32) * scale).astype(BF16)
        acc[0:N_META, :] += jnp.concatenate([lax.dot_general(ds, qm, _TN, preferred_element_type=F32),
                                             lax.dot_general(p.astype(BF16), domb, _TN, preferred_element_type=F32)], axis=1)
        sink_tot = head_totals(-p_s * delta, N_META)
        tri = _causal(R)
        acc_sink[...] = jnp.zeros_like(acc_sink)
        ones = jnp.ones((8, HEAD_DIM), BF16)

        def block(cur, first, keys):
            k2, v2 = _kv(kvs[keys, :])
            rows = pl.ds(cur, BLOCK)
            qb = _heads_to_rows(q_ref[rows, :])
            dob = _heads_to_rows(do_ref[rows, :])
            prod = dob * _heads_to_rows(o_ref[rows, :])
            hi = prod.astype(BF16)
            lo = (prod - hi.astype(F32)).astype(BF16)
            delta = (lax.dot_general(ones, hi, _NT, preferred_element_type=F32)
                     + lax.dot_general(ones, lo, _NT, preferred_element_type=F32))[0:1]
            dobb = dob.astype(BF16)
            st = lax.dot_general(k2, qb, _NT, preferred_element_type=F32)
            smt = lax.dot_general(k_m, qb, _NT, preferred_element_type=F32)
            s_band = jnp.where(tri, st, NEG_INF) if first else _fold_band(tri, st)
            p_b, p_m, p_s = _band_probs(s_band, smt, sk_ref[0])
            dpt = lax.dot_general(v2, dobb, _NT, preferred_element_type=F32)
            dpm = lax.dot_general(v_m, dobb, _NT, preferred_element_type=F32)
            ds_b = p_b * ((dpt if first else _fold_band(tri, dpt)) - delta)
            ds2 = (ds_b if first else _unfold_band(tri, ds_b)).astype(BF16)
            p2 = (p_b if first else _unfold_band(tri, p_b)).astype(BF16)
            dsm = (p_m * (dpm - delta)).astype(BF16)
            pm = p_m.astype(BF16)
            dq = (lax.dot_general(ds2, k2, _TN, preferred_element_type=F32)
                  + lax.dot_general(dsm, k_m, _TN, preferred_element_type=F32))
            dq_ref[rows, :] = _rows_to_heads(dq * scale).astype(BF16)
            acc[keys, :] += jnp.concatenate([jnp.dot(ds2, qb, preferred_element_type=F32),
                                             jnp.dot(p2, dobb, preferred_element_type=F32)], axis=1)
            acc[0:N_META, :] += jnp.concatenate([jnp.dot(dsm, qb, preferred_element_type=F32),
                                                 jnp.dot(pm, dobb, preferred_element_type=F32)], axis=1)
            acc_sink[0:1, :] += -p_s * delta

        block(N_META, True, pl.ds(N_META, BLOCK))

        def step(n, carry):
            block(_block_start(n), False, pl.ds(_block_start(n - 1), 2 * BLOCK))
            return carry

        lax.fori_loop(1, nb, step, 0, unroll=5 if (nb - 1) % 5 == 0 else 1)
        dkv_ref[...] = acc[...].astype(BF16)
        rid = lax.broadcasted_iota(jnp.int32, (8, 128), 0)
        tot = sink_tot
        for g in range(Q_PER_KV):
            tot = tot + jnp.where(rid == g, jnp.sum(acc_sink[:, g * BLOCK:(g + 1) * BLOCK]), 0.0)

        @pl.when(b == 0)
        def _():
            dsk_ref[0] = tot

        @pl.when(b > 0)
        def _():
            dsk_ref[0] += tot

    q_spec = pl.BlockSpec((L, QW), lambda h, b: (b, h))
    kv_spec = pl.BlockSpec((L, 2 * HEAD_DIM), lambda h, b: (b, h))
    return _call(body, name, (KV, B),
                 [q_spec, kv_spec, q_spec, q_spec,
                  pl.BlockSpec((1, 1, R), lambda h, b: (h, 0, 0)),
                  pl.BlockSpec((1, Q_PER_KV * N_META, 1), lambda h, b: (h, 0, 0))],
                 [q_spec, kv_spec, pl.BlockSpec((1, 8, 128), lambda h, b: (h, 0, 0))],
                 [jax.ShapeDtypeStruct((T, D), BF16), jax.ShapeDtypeStruct((T, KV * 2 * HEAD_DIM), BF16),
                  jax.ShapeDtypeStruct((KV, 8, 128), F32)],
                 [pltpu.VMEM((L, 2 * HEAD_DIM), BF16), pltpu.VMEM((L, 2 * HEAD_DIM), F32), pltpu.VMEM((8, R), F32)],
                 ("parallel", "arbitrary"), (q, kv, o, do, sink_row, sink_meta), riders)


def _cmul_add(acc_r, acc_i, lr, li, xr, xi):
    return acc_r + (lr * xr - li * xi), acc_i + (lr * xi + li * xr)


def _cols_per_step(ncol):
    for cps in (4, 2):
        if ncol % cps == 0:
            return cps
    return 1


def _ssm_fwd(u, bmat, cmat, dskip, tables, nbatch, rc, name):
    T, W = u.shape
    ncol = W // SSM_LANES
    nch = T // rc
    S = STATE_LANES
    cps = _cols_per_step(ncol)
    assert nbatch == 4

    def body(u_ref, b_ref, c_ref, d_ref, tab_ref, y_ref, xs_ref, st_ref, carry_ref):
        ch = pl.program_id(1)

        @pl.when(ch == 0)
        def _():
            carry_ref[...] = jnp.zeros_like(carry_ref)

        uv = u_ref[...]
        for k in range(cps):
            st_ref[:, 2 * S * k:2 * S * (k + 1)] = jnp.dot(uv[:, SSM_LANES * k:SSM_LANES * (k + 1)].astype(BF16), b_ref[k],
                                                           preferred_element_type=F32)
        low = lax.broadcasted_iota(jnp.int32, (8, S), 0) < nbatch

        def tile(k, r0, c_r, c_i):
            re, im = slice(2 * S * k, 2 * S * k + S), slice(2 * S * k + S, 2 * S * (k + 1))
            la_r, la_i = tab_ref[k, :, 0:S], tab_ref[k, :, S:2 * S]
            lb_r, lb_i = tab_ref[k, :, 2 * S:3 * S], tab_ref[k, :, 3 * S:4 * S]
            v_r = st_ref[pl.ds(r0, 8), re]
            v_i = st_ref[pl.ds(r0, 8), im]
            v_r, v_i = _cmul_add(v_r, v_i, la_r, la_i, pltpu.roll(v_r, nbatch, 0), pltpu.roll(v_i, nbatch, 0))
            rc_r, rc_i = pltpu.roll(c_r, nbatch, 0), pltpu.roll(c_i, nbatch, 0)
            cb_r, cb_i = jnp.where(low, rc_r, c_r), jnp.where(low, rc_i, c_i)
            v_r, v_i = _cmul_add(v_r, v_i, lb_r, lb_i, cb_r, cb_i)
            st_ref[pl.ds(r0, 8), re] = v_r
            st_ref[pl.ds(r0, 8), im] = v_i
            return v_r, v_i

        def step(i, carry):
            r0 = pl.multiple_of(i * 8, 8)
            out = []
            for k in range(cps):
                out += list(tile(k, r0, carry[2 * k], carry[2 * k + 1]))
            return tuple(out)

        halves = tuple(carry_ref[:, S * j:S * (j + 1)] for j in range(2 * cps))
        halves = lax.fori_loop(0, rc // 8, step, halves)
        for j in range(2 * cps):
            carry_ref[:, S * j:S * (j + 1)] = halves[j]
        xb = st_ref[...].astype(BF16)
        xs_ref[...] = xb
        for k in range(cps):
            cols = slice(SSM_LANES * k, SSM_LANES * (k + 1))
            y_ref[:, cols] = (jnp.dot(xb[:, 2 * S * k:2 * S * (k + 1)], c_ref[k], preferred_element_type=F32)
                              + d_ref[:, cols] * uv[:, cols])

    return pl.pallas_call(
        body, name=name, grid=(ncol // cps, nch),
        in_specs=[pl.BlockSpec((rc, cps * SSM_LANES), lambda g, c: (c, g)),
                  pl.BlockSpec((cps, SSM_LANES, 2 * S), lambda g, c: (g, 0, 0)),
                  pl.BlockSpec((cps, 2 * S, SSM_LANES), lambda g, c: (g, 0, 0)),
                  pl.BlockSpec((1, cps * SSM_LANES), lambda g, c: (0, g)),
                  pl.BlockSpec((cps, 8, 4 * S), lambda g, c: (g, 0, 0))],
        out_specs=[pl.BlockSpec((rc, cps * SSM_LANES), lambda g, c: (c, g)),
                   pl.BlockSpec((rc, cps * 2 * S), lambda g, c: (c, g))],
        out_shape=[jax.ShapeDtypeStruct((T, W), F32), jax.ShapeDtypeStruct((T, ncol * 2 * S), BF16)],
        scratch_shapes=[pltpu.VMEM((rc, cps * 2 * S), F32), pltpu.VMEM((8, cps * 2 * S), F32)],
        compiler_params=_params(("parallel", "arbitrary")),
    )(u, bmat, cmat, dskip, tables)


def _ssm_bwd(dy, u, xs, bmat, cmat, dskip, tables, nbatch, rc, name):
    T, W = u.shape
    ncol = W // SSM_LANES
    nch = T // rc
    S = STATE_LANES
    ntile = rc // 16
    cps = _cols_per_step(ncol)

    def body(dy_ref, u_ref, xs_ref, b_ref, c_ref, d_ref, tab_ref,
             du_ref, db_ref, dc_ref, dl_ref, dd_ref, st_ref, carry_ref, accl_ref, accd_ref):
        ch = pl.program_id(1)

        @pl.when(ch == 0)
        def _():
            carry_ref[...] = jnp.zeros_like(carry_ref)
            accl_ref[...] = jnp.zeros_like(accl_ref)
            accd_ref[...] = jnp.zeros_like(accd_ref)
            db_ref[...] = jnp.zeros_like(db_ref)
            dc_ref[...] = jnp.zeros_like(dc_ref)

        dyv = dy_ref[...]
        uv = u_ref[...]
        dyb = dyv.astype(BF16)
        for k in range(cps):
            st_ref[:, 2 * S * k:2 * S * (k + 1)] = lax.dot_general(dyb[:, SSM_LANES * k:SSM_LANES * (k + 1)], c_ref[k], _NT,
                                                                   preferred_element_type=F32)
        low = lax.broadcasted_iota(jnp.int32, (8, S), 0) < nbatch

        def tile(k, r0, x_r, x_i, c_r, c_i, al_r, al_i):
            re, im = slice(2 * S * k, 2 * S * k + S), slice(2 * S * k + S, 2 * S * (k + 1))
            la_r, la_i = tab_ref[k, :, 0:S], tab_ref[k, :, S:2 * S]
            lb_r, lb_i = tab_ref[k, :, 2 * S:3 * S], tab_ref[k, :, 3 * S:4 * S]
            v_r = st_ref[pl.ds(r0, 8), re]
            v_i = st_ref[pl.ds(r0, 8), im]
            v_r, v_i = _cmul_add(v_r, v_i, la_r, la_i, pltpu.roll(v_r, nbatch, 0), pltpu.roll(v_i, nbatch, 0))
            cb_r = jnp.where(low, c_r, pltpu.roll(c_r, nbatch, 0))
            cb_i = jnp.where(low, c_i, pltpu.roll(c_i, nbatch, 0))
            v_r, v_i = _cmul_add(v_r, v_i, lb_r, lb_i, cb_r, cb_i)
            st_ref[pl.ds(r0, 8), re] = v_r
            st_ref[pl.ds(r0, 8), im] = v_i
            n_r = jnp.where(low, pltpu.roll(v_r, nbatch, 0), cb_r)
            n_i = jnp.where(low, pltpu.roll(v_i, nbatch, 0), cb_i)
            al_r = al_r + (n_r * x_r + n_i * x_i)
            al_i = al_i + (n_i * x_r - n_r * x_i)
            return v_r, v_i, al_r, al_i

        def step(j, carry):
            r0 = pl.multiple_of((ntile - 1 - j) * 16, 16)
            out = []
            for k in range(cps):
                re, im = slice(2 * S * k, 2 * S * k + S), slice(2 * S * k + S, 2 * S * (k + 1))
                x_r = xs_ref[pl.ds(r0, 16), re].astype(F32)
                x_i = xs_ref[pl.ds(r0, 16), im].astype(F32)
                mid = tile(k, r0 + 8, x_r[8:16], x_i[8:16], *carry[4 * k:4 * k + 4])
                out += list(tile(k, r0, x_r[0:8], x_i[0:8], *mid))
            return tuple(out)

        init = []
        for k in range(cps):
            init += [carry_ref[:, 2 * S * k:2 * S * k + S], carry_ref[:, 2 * S * k + S:2 * S * (k + 1)],
                     accl_ref[:, 2 * S * k:2 * S * k + S], accl_ref[:, 2 * S * k + S:2 * S * (k + 1)]]
        fin = lax.fori_loop(0, ntile, step, tuple(init))
        for k in range(cps):
            carry_ref[:, 2 * S * k:2 * S * k + S] = fin[4 * k]
            carry_ref[:, 2 * S * k + S:2 * S * (k + 1)] = fin[4 * k + 1]
            accl_ref[:, 2 * S * k:2 * S * k + S] = fin[4 * k + 2]
            accl_ref[:, 2 * S * k + S:2 * S * (k + 1)] = fin[4 * k + 3]
        dsb = st_ref[...].astype(BF16)
        ub = uv.astype(BF16)
        for k in range(cps):
            cols, lanes = slice(SSM_LANES * k, SSM_LANES * (k + 1)), slice(2 * S * k, 2 * S * (k + 1))
            du_ref[:, cols] = (lax.dot_general(dsb[:, lanes], b_ref[k], _NT, preferred_element_type=F32)
                               + d_ref[:, cols] * dyv[:, cols])
            db_ref[k] += lax.dot_general(ub[:, cols], dsb[:, lanes], _TN, preferred_element_type=F32)
            dc_ref[k] += lax.dot_general(xs_ref[:, lanes], dyb[:, cols], _TN, preferred_element_type=F32)
        accd_ref[...] += _fold8(dyv * uv)

        @pl.when(ch == nch - 1)
        def _():
            for k in range(cps):
                dl_ref[k] = jnp.sum(accl_ref[:, 2 * S * k:2 * S * (k + 1)], axis=0, keepdims=True)
            dd_ref[...] = jnp.sum(accd_ref[...], axis=0, keepdims=True)

    rev = lambda g, c: (nch - 1 - c, g)
    return pl.pallas_call(
        body, name=name, grid=(ncol // cps, nch),
        in_specs=[pl.BlockSpec((rc, cps * SSM_LANES), rev), pl.BlockSpec((rc, cps * SSM_LANES), rev),
                  pl.BlockSpec((rc, cps * 2 * S), rev),
                  pl.BlockSpec((cps, SSM_LANES, 2 * S), lambda g, c: (g, 0, 0)),
                  pl.BlockSpec((cps, 2 * S, SSM_LANES), lambda g, c: (g, 0, 0)),
                  pl.BlockSpec((1, cps * SSM_LANES), lambda g, c: (0, g)),
                  pl.BlockSpec((cps, 8, 4 * S), lambda g, c: (g, 0, 0))],
        out_specs=[pl.BlockSpec((rc, cps * SSM_LANES), rev),
                   pl.BlockSpec((cps, SSM_LANES, 2 * S), lambda g, c: (g, 0, 0)),
                   pl.BlockSpec((cps, 2 * S, SSM_LANES), lambda g, c: (g, 0, 0)),
                   pl.BlockSpec((cps, 1, 2 * S), lambda g, c: (g, 0, 0)),
                   pl.BlockSpec((1, cps * SSM_LANES), lambda g, c: (0, g))],
        out_shape=[jax.ShapeDtypeStruct((T, W), F32),
                   jax.ShapeDtypeStruct((ncol, SSM_LANES, 2 * S), F32),
                   jax.ShapeDtypeStruct((ncol, 2 * S, SSM_LANES), F32),
                   jax.ShapeDtypeStruct((ncol, 1, 2 * S), F32),
                   jax.ShapeDtypeStruct((1, W), F32)],
        scratch_shapes=[pltpu.VMEM((rc, cps * 2 * S), F32), pltpu.VMEM((8, cps * 2 * S), F32),
                        pltpu.VMEM((8, cps * 2 * S), F32), pltpu.VMEM((8, cps * SSM_LANES), F32)],
        compiler_params=_params(("parallel", "arbitrary")),
    )(dy, u, xs, bmat, cmat, dskip, tables)


def _ssm_matrices(a_re, a_im, log_step, b_re, b_im, c_re, c_im):
    G, N = a_re.shape
    ncol = G // GROUPS_PER_COL
    step = jnp.exp(log_step)[:, None]
    mag = jnp.exp(a_re * step)
    ang = a_im * step
    lam_re, lam_im = mag * jnp.cos(ang), mag * jnp.sin(ang)
    den = a_re * a_re + a_im * a_im
    nr, ni = lam_re - 1.0, lam_im
    coef_re = (nr * a_re + ni * a_im) / den
    coef_im = (ni * a_re - nr * a_im) / den
    bb_re = coef_re[..., None] * b_re - coef_im[..., None] * b_im
    bb_im = coef_re[..., None] * b_im + coef_im[..., None] * b_re
    eye = jnp.eye(GROUPS_PER_COL, dtype=F32)
    bb = jnp.stack([bb_re, bb_im]).reshape(2, ncol, GROUPS_PER_COL, N, SSM_GROUP)
    bmat = jnp.einsum("pbgnc,gh->bgcphn", bb, eye).reshape(ncol, SSM_LANES, 2 * STATE_LANES)
    cc = jnp.stack([c_re, -c_im]).reshape(2, ncol, GROUPS_PER_COL, SSM_GROUP, N)
    cmat = jnp.einsum("pbgcn,gh->bpgnhc", cc, eye).reshape(ncol, 2 * STATE_LANES, SSM_LANES)
    lam = jnp.concatenate([lam_re.reshape(ncol, STATE_LANES), lam_im.reshape(ncol, STATE_LANES)], axis=-1)
    return lam, bmat, cmat


def _scan_tables(lam, nbatch, conj):
    S = STATE_LANES
    lr, li = lam[:, None, 0:S], lam[:, None, S:2 * S]
    if conj:
        li = -li
    l2r, l2i = lr * lr - li * li, 2.0 * lr * li
    first = (jnp.arange(8) < nbatch)[None, :, None]
    zero = jnp.zeros_like(lr)
    if conj:
        parts = [jnp.where(first, lr, zero), jnp.where(first, li, zero), jnp.where(first, l2r, lr), jnp.where(first, l2i, li)]
    else:
        parts = [jnp.where(first, zero, lr), jnp.where(first, zero, li), jnp.where(first, lr, l2r), jnp.where(first, li, l2i)]
    return jnp.concatenate([jnp.broadcast_to(p, (lam.shape[0], 8, S)) for p in parts], axis=-1)


def _adamw_update(w_ref, g_ref, m_ref, v_ref, d_ref, nm_ref, nv_ref):
    gv = g_ref[...]
    mn = ADAM_B1 * m_ref[...] + (1.0 - ADAM_B1) * gv
    vn = ADAM_B2 * v_ref[...] + (1.0 - ADAM_B2) * (gv * gv)
    m_hat = mn / (1.0 - ADAM_B1 ** ADAM_STEP)
    v_hat = vn / (1.0 - ADAM_B2 ** ADAM_STEP)
    d_ref[...] = -ADAM_LR * (m_hat / (jnp.sqrt(v_hat) + ADAM_EPS) + ADAM_WD * w_ref[...])
    nm_ref[...] = mn
    nv_ref[...] = vn


def _adamw_small(ws, gs, ms, vs, name):
    n = len(ws)

    def body(*refs):
        for i in range(n):
            _adamw_update(refs[i], refs[n + i], refs[2 * n + i], refs[3 * n + i],
                          refs[4 * n + i], refs[5 * n + i], refs[6 * n + i])

    vm = pl.BlockSpec(memory_space=pltpu.VMEM)
    shapes = [jax.ShapeDtypeStruct(a.shape, F32) for a in ws]
    outs = pl.pallas_call(body, name=name, in_specs=[vm] * (4 * n), out_specs=[vm] * (3 * n), out_shape=shapes * 3,
                          compiler_params=pltpu.CompilerParams(vmem_limit_bytes=VMEM_LIMIT))(*ws, *gs, *ms, *vs)
    return outs[:n], outs[n:2 * n], outs[2 * n:]


def _adamw(w, g, m, v, name):
    R, C = w.shape[-2], w.shape[-1]
    tr = R if R <= 512 else _pick_tile(R, 512, 8)
    body = functools.partial(_adamw_update)

    def spec_for(a):
        if len(a.shape) == 2:
            return pl.BlockSpec((tr, C), lambda i: (i, 0))
        return pl.BlockSpec((None, tr, C), lambda i: (0, i, 0))

    spec = spec_for(w)
    shp = jax.ShapeDtypeStruct(w.shape, F32)
    return pl.pallas_call(body, name=name, grid=(R // tr,), in_specs=[spec, spec_for(g), spec, spec], out_specs=[spec] * 3,
                          out_shape=[shp, shp, shp], compiler_params=_params(("parallel",)))(w, g, m, v)


_ANY = pl.BlockSpec(memory_space=pl.ANY)


def _place():
    x, y, c = lax.axis_index("x"), lax.axis_index("y"), lax.axis_index("c")
    chips = [(1 - x, y), (x, 1 - y), (1 - x, 1 - y)]
    return x, y, c, chips


def _remote(src, dst, send_sems, recv_sems, k, to):
    return pltpu.make_async_remote_copy(src_ref=src, dst_ref=dst, send_sem=send_sems.at[k], recv_sem=recv_sems.at[k],
                                        device_id=to, device_id_type=MESH_IDS)


class _Riders:
    def __init__(self, srcs, out_shapes, n_sems, copies):
        self.srcs, self.out_shapes, self.n_sems, self.copies = list(srcs), list(out_shapes), n_sems, copies


def _call(body, name, grid, in_specs, out_specs, out_shape, scratch_shapes, sem, args, riders=None):
    if riders is None:
        return pl.pallas_call(body, name=name, grid=grid, in_specs=in_specs, out_specs=out_specs, out_shape=out_shape,
                              scratch_shapes=scratch_shapes, compiler_params=_params(sem))(*args)
    n_in, n_out, n_scr = len(in_specs), len(out_specs), len(scratch_shapes)
    r_in, r_out = len(riders.srcs), len(riders.out_shapes)

    def carrying(*refs):
        a, b = n_in, n_in + r_in
        c, d = b + n_out, b + n_out + r_out
        e = d + n_scr
        sends, arrivals = riders.copies(refs[a:b], refs[c:d], refs[e], refs[e + 1])
        first, last = None, None
        for ax, size in enumerate(grid):
            at0, at1 = pl.program_id(ax) == 0, pl.program_id(ax) == size - 1
            first = at0 if first is None else first & at0
            last = at1 if last is None else last & at1

        @pl.when(first)
        def _():
            for cp in sends:
                cp.start()

        body(*refs[:a], *refs[b:c], *refs[d:e])

        @pl.when(last)
        def _():
            for cp in arrivals:
                cp.wait_recv()
            for cp in sends:
                cp.wait_send()

    outs = pl.pallas_call(
        carrying, name=name, grid=grid, in_specs=list(in_specs) + [_ANY] * r_in,
        out_specs=list(out_specs) + [_ANY] * r_out, out_shape=list(out_shape) + riders.out_shapes,
        scratch_shapes=list(scratch_shapes) + [pltpu.SemaphoreType.DMA((riders.n_sems,)),
                                               pltpu.SemaphoreType.DMA((riders.n_sems,))],
        compiler_params=pltpu.CompilerParams(dimension_semantics=("arbitrary",) * len(grid),
                                             vmem_limit_bytes=VMEM_LIMIT, has_side_effects=True),
    )(*args, *riders.srcs)
    return outs[:n_out], outs[n_out:]


def _gather_riders(shards):
    def copies(srcs, outs, send_sems, recv_sems):
        x, y, c, chips = _place()
        sends, arrivals = [], []
        for i, s in enumerate(shards):
            half = s.shape[0] // 2
            rows = pl.ds(c * half, half)
            for j, chip in enumerate(chips):
                sends.append(_remote(srcs[i].at[rows, :], outs[i].at[2 * x + y, rows, :], send_sems, recv_sems,
                                     3 * i + j, (*chip, c)))
                landed = outs[i].at[2 * chip[0] + chip[1], rows, :]
                arrivals.append(_remote(landed, landed, send_sems, recv_sems, 3 * i + j, (*chip, c)))
        return sends, arrivals

    return _Riders(shards, [jax.ShapeDtypeStruct((N_CHIPS,) + s.shape, s.dtype) for s in shards], 3 * len(shards), copies)


def _exchange_riders(parts):
    def copies(srcs, outs, send_sems, recv_sems):
        x, y, c, chips = _place()
        sends = [_remote(srcs[i].at[2 * chip[0] + chip[1]], outs[i].at[j], send_sems, recv_sems, 3 * i + j, (*chip, c))
                 for i in range(len(parts)) for j, chip in enumerate(chips)]
        return sends, sends

    return _Riders(parts, [jax.ShapeDtypeStruct((3,) + p.shape[1:], p.dtype) for p in parts], 3 * len(parts), copies)


def _swap_riders(grads):
    def copies(srcs, outs, send_sems, recv_sems):
        x, y, c, _ = _place()
        sends = []
        for i, g in enumerate(grads):
            half = g.shape[1] // 2
            sends.append(_remote(srcs[i].at[:, pl.ds((1 - c) * half, half), :], outs[i], send_sems, recv_sems, i,
                                 (x, y, 1 - c)))
        return sends, sends

    return _Riders(grads, [jax.ShapeDtypeStruct((N_CHIPS, g.shape[1] // 2, g.shape[2]), g.dtype) for g in grads],
                   len(grads), copies)


def _forward_halves(gathered, shards, tag):
    n = len(gathered)

    def body(*refs):
        srcs, outs = refs[:n], refs[n:2 * n]
        send_sems, recv_sems = refs[2 * n:]
        x, y, c, chips = _place()
        sibling = (x, y, 1 - c)
        cps = []
        for i in range(n):
            half = gathered[i].shape[1] // 2
            for j, chip in enumerate(chips):
                slot = 2 * chip[0] + chip[1]
                cps.append(_remote(srcs[i].at[slot, pl.ds(c * half, half), :], outs[i].at[slot, pl.ds(c * half, half), :],
                                   send_sems, recv_sems, 3 * i + j, sibling))
        for cp in cps:
            cp.start()
        for i in range(n):
            half = gathered[i].shape[1] // 2
            for j, chip in enumerate(chips):
                theirs = outs[i].at[2 * chip[0] + chip[1], pl.ds((1 - c) * half, half), :]
                _remote(theirs, theirs, send_sems, recv_sems, 3 * i + j, sibling).wait_recv()
        for cp in cps:
            cp.wait_send()

    outs = pl.pallas_call(
        body, name=f"gather_forward_{tag}", in_specs=[_ANY] * n, out_specs=[_ANY] * n,
        out_shape=[jax.ShapeDtypeStruct(g.shape, g.dtype) for g in gathered],
        input_output_aliases={i: i for i in range(n)},
        scratch_shapes=[pltpu.SemaphoreType.DMA((3 * n,)), pltpu.SemaphoreType.DMA((3 * n,))],
        compiler_params=pltpu.CompilerParams(has_side_effects=True),
    )(*gathered)
    slot = 2 * lax.axis_index("x") + lax.axis_index("y")
    return [lax.dynamic_update_slice(o, s[None], (slot, 0, 0)) for o, s in zip(outs, shards)]


def _gather_weights(shards):
    n = len(shards)

    def body(*refs):
        srcs, outs = refs[:n], refs[n:2 * n]
        send_sems, recv_sems = refs[2 * n:]
        x, y, c, chips = _place()
        sibling = (x, y, 1 - c)

        def piece(i, px, py, pc):
            half = shards[i].shape[0] // 2
            return outs[i].at[2 * px + py, pl.ds(pc * half, half), :]

        first = []
        for i in range(n):
            half = shards[i].shape[0] // 2
            for j, chip in enumerate(chips):
                first.append(_remote(srcs[i].at[pl.ds(c * half, half), :], piece(i, x, y, c), send_sems, recv_sems,
                                     6 * i + j, (*chip, c)))
        for cp in first:
            cp.start()
        passed = []
        for i in range(n):
            for j, chip in enumerate(chips):
                _remote(piece(i, *chip, c), piece(i, *chip, c), send_sems, recv_sems, 6 * i + j, (*chip, c)).wait_recv()
                cp = _remote(piece(i, *chip, c), piece(i, *chip, c), send_sems, recv_sems, 6 * i + 3 + j, sibling)
                cp.start()
                passed.append(cp)
        for i in range(n):
            for j, chip in enumerate(chips):
                _remote(piece(i, *chip, 1 - c), piece(i, *chip, 1 - c), send_sems, recv_sems, 6 * i + 3 + j,
                        sibling).wait_recv()
        for cp in first + passed:
            cp.wait_send()

    outs = pl.pallas_call(
        body, name="gather_weights", in_specs=[_ANY] * n, out_specs=[_ANY] * n,
        out_shape=[jax.ShapeDtypeStruct((N_CHIPS,) + s.shape, s.dtype) for s in shards],
        scratch_shapes=[pltpu.SemaphoreType.DMA((6 * n,)), pltpu.SemaphoreType.DMA((6 * n,))],
        compiler_params=pltpu.CompilerParams(has_side_effects=True),
    )(*shards)
    slot = 2 * lax.axis_index("x") + lax.axis_index("y")
    return [lax.dynamic_update_slice(o, s[None], (slot, 0, 0)) for o, s in zip(outs, shards)]


def _swap_halves(grads, tag):
    n = len(grads)

    def body(*refs):
        srcs, outs = refs[:n], refs[n:2 * n]
        send_sems, recv_sems = refs[2 * n:]
        x, y, c, _ = _place()
        cps = []
        for i in range(n):
            half = grads[i].shape[1] // 2
            cps.append(_remote(srcs[i].at[:, pl.ds((1 - c) * half, half), :], outs[i], send_sems, recv_sems, i, (x, y, 1 - c)))
        for cp in cps:
            cp.start()
        for cp in cps:
            cp.wait()

    return pl.pallas_call(
        body, name=f"grad_swap_halves_{tag}", in_specs=[_ANY] * n, out_specs=[_ANY] * n,
        out_shape=[jax.ShapeDtypeStruct((N_CHIPS, g.shape[1] // 2, g.shape[2]), g.dtype) for g in grads],
        scratch_shapes=[pltpu.SemaphoreType.DMA((n,)), pltpu.SemaphoreType.DMA((n,))],
        compiler_params=pltpu.CompilerParams(has_side_effects=True),
    )(*grads)


def _join_halves(fulls):
    n = len(fulls)

    def body(*refs):
        srcs, outs = refs[:n], refs[n:2 * n]
        send_sems, recv_sems = refs[2 * n:]
        x, y, c, _ = _place()
        sibling = (x, y, 1 - c)
        cps = []
        for i in range(n):
            h = fulls[i].shape[0] // 2
            cps.append(_remote(srcs[i].at[pl.ds(c * h, h), :], outs[i].at[pl.ds(c * h, h), :], send_sems, recv_sems, i,
                               sibling))
        for cp in cps:
            cp.start()
        for i in range(n):
            h = fulls[i].shape[0] // 2
            theirs = outs[i].at[pl.ds((1 - c) * h, h), :]
            _remote(theirs, theirs, send_sems, recv_sems, i, sibling).wait_recv()
        for cp in cps:
            cp.wait_send()

    return pl.pallas_call(
        body, name="grad_join_halves", in_specs=[_ANY] * n, out_specs=[_ANY] * n,
        out_shape=[jax.ShapeDtypeStruct(f.shape, f.dtype) for f in fulls],
        input_output_aliases={i: i for i in range(n)},
        scratch_shapes=[pltpu.SemaphoreType.DMA((n,)), pltpu.SemaphoreType.DMA((n,))],
        compiler_params=pltpu.CompilerParams(has_side_effects=True),
    )(*fulls)


def _half_tile(h):
    return h if h <= 512 else _pick_tile(h, 512, ROW_ALIGN)


def _sum_halves(g, r1, c_idx, name):
    _, R, C = g.shape
    H = R // 2
    tr = _half_tile(H)
    nblk = H // tr

    def body(c_ref, g_ref, r_ref, p_ref):
        p_ref[...] = (g_ref[...] + r_ref[...]).astype(BF16)

    half = pl.BlockSpec((None, tr, C), lambda s, i, c_ref: (s, c_ref[0] * nblk + i, 0))
    plain = pl.BlockSpec((None, tr, C), lambda s, i, c_ref: (s, i, 0))
    return pl.pallas_call(
        body, name=name,
        grid_spec=pltpu.PrefetchScalarGridSpec(num_scalar_prefetch=1, grid=(N_CHIPS, nblk), in_specs=[half, plain],
                                               out_specs=plain),
        out_shape=jax.ShapeDtypeStruct((N_CHIPS, H, C), BF16),
        compiler_params=_params(("parallel", "parallel")),
    )(c_idx, g, r1)


def _sum_chips(g, r1, r2, idx, name):
    _, R, C = g.shape
    H = R // 2
    tr = _half_tile(H)
    nblk = H // tr

    def body(idx_ref, g_ref, r1_ref, r2_ref, o_ref):
        o_ref[...] = (((g_ref[...] + r1_ref[...]) + r2_ref[0].astype(F32)) + r2_ref[1].astype(F32)) + r2_ref[2].astype(F32)

    return pl.pallas_call(
        body, name=name,
        grid_spec=pltpu.PrefetchScalarGridSpec(
            num_scalar_prefetch=1, grid=(nblk,),
            in_specs=[pl.BlockSpec((None, tr, C), lambda i, idx_ref: (idx_ref[0], idx_ref[1] * nblk + i, 0)),
                      pl.BlockSpec((None, tr, C), lambda i, idx_ref: (idx_ref[0], i, 0)),
                      pl.BlockSpec((3, tr, C), lambda i, idx_ref: (0, i, 0))],
            out_specs=pl.BlockSpec((tr, C), lambda i, idx_ref: (idx_ref[1] * nblk + i, 0))),
        out_shape=jax.ShapeDtypeStruct((R, C), F32),
        compiler_params=_params(("parallel",)),
    )(idx, g, r1, r2)


def _all_reduce_small(v, n_fold, fold_rows, fold_at):
    M, N = v.shape

    def body(x_ref, tot_ref, fold_ref, all_ref, send_sems, recv_sems, local_sem):
        x, y, c, chips = _place()
        me, sibling = (x, y, c), (x, y, 1 - c)

        def rows(px, py, pc):
            return all_ref.at[pl.ds((4 * px + 2 * py + pc) * M, M), :]

        def copy(k, block, to, src=None):
            return _remote(rows(*block) if src is None else src, rows(*block), send_sems, recv_sems, k, to)

        mine = pltpu.make_async_copy(x_ref, rows(*me), local_sem)
        mine.start()
        first = [copy(0, me, sibling, src=x_ref)]
        first += [copy(1 + j, me, (*chip, c), src=x_ref) for j, chip in enumerate(chips)]
        for cp in first:
            cp.start()
        passed = [copy(4 + j, (*chip, c), sibling) for j, chip in enumerate(chips)]
        for j, chip in enumerate(chips):
            copy(1 + j, (*chip, c), me).wait_recv()
            passed[j].start()
        copy(0, sibling, me).wait_recv()
        for j, chip in enumerate(chips):
            copy(4 + j, (*chip, 1 - c), me).wait_recv()
        for cp in first + passed:
            cp.wait_send()
        mine.wait()
        tot = all_ref[0:M, :]
        for d in range(1, 8):
            tot = tot + all_ref[d * M:(d + 1) * M, :]
        tot_ref[...] = tot
        f = tot[fold_at:fold_at + fold_rows, :]
        for e in range(1, n_fold):
            f = f + tot[fold_at + e * fold_rows:fold_at + (e + 1) * fold_rows, :]
        fold_ref[...] = f

    vm = pl.BlockSpec(memory_space=pltpu.VMEM)
    return pl.pallas_call(
        body, name="all_reduce_small", in_specs=[vm], out_specs=[vm, vm],
        out_shape=[jax.ShapeDtypeStruct((M, N), F32), jax.ShapeDtypeStruct((fold_rows, N), F32)],
        scratch_shapes=[pltpu.VMEM((8 * M, N), F32), pltpu.SemaphoreType.DMA((7,)), pltpu.SemaphoreType.DMA((7,)),
                        pltpu.SemaphoreType.DMA],
        compiler_params=pltpu.CompilerParams(has_side_effects=True, vmem_limit_bytes=VMEM_LIMIT),
    )(v)


def _as_rows(a, width):
    flat = a.reshape(-1)
    pad = (-flat.shape[0]) % width
    if pad:
        flat = jnp.concatenate([flat, jnp.zeros((pad,), flat.dtype)])
    return flat.reshape(-1, width)


class _Layout:
    def __init__(self, width, total_mult):
        self.width, self.total_mult = width, total_mult
        self.offsets, self.shapes, self.rows = {}, {}, 0

    def add(self, name, shape):
        r = -(-math.prod(shape) // self.width)
        self.offsets[name], self.shapes[name] = (self.rows, r), tuple(shape)
        self.rows += r

    def align(self, mult):
        gap = (-self.rows) % mult
        if gap:
            self.offsets[f"_gap{self.rows}"], self.shapes[f"_gap{self.rows}"] = (self.rows, gap), (gap, self.width)
            self.rows += gap
        return self.rows

    def pack(self, pieces):
        self.align(self.total_mult)
        parts = [_as_rows(pieces[n].astype(F32), self.width) if n in pieces else jnp.zeros(self.shapes[n], F32)
                 for n in self.offsets]
        return jnp.concatenate(parts, axis=0)

    def unpack(self, buf, name):
        off, r = self.offsets[name]
        shape = self.shapes[name]
        return buf[off:off + r].reshape(-1)[:math.prod(shape)].reshape(shape)


_BIG = ["ffn1_w1", "ffn1_w3", "ffn1_w2", "w_in", "ssm_glu_a", "ssm_glu_b", "w_out", "ffn2_w1", "ffn2_w3", "ffn2_w2"]
_TRANSPOSED = {"ffn1_w1", "ffn1_w3", "ffn2_w1", "ffn2_w3"}
_SMALL = ["ffn1_norm", "mix_norm", "ffn2_norm", "final_norm", "attn_sinks", "ssm_a_re", "ssm_a_im", "ssm_log_step",
          "ssm_b_re", "ssm_b_im", "ssm_c_re", "ssm_c_im", "ssm_d"]
_WEIGHTS = ["meta_tokens", "ffn1_norm", "ffn1_w1", "ffn1_w3", "ffn1_w2", "mix_norm", "w_in", "attn_sinks", "ssm_a_re",
            "ssm_a_im", "ssm_log_step", "ssm_b_re", "ssm_b_im", "ssm_c_re", "ssm_c_im", "ssm_d", "ssm_glu_a",
            "ssm_glu_b", "w_out", "ffn2_norm", "ffn2_w1", "ffn2_w3", "ffn2_w2", "final_norm"]


def _kv_interleave(w, kv_heads):
    kvw = kv_heads * HEAD_DIM
    lead = w.shape[:-1]
    k = w[..., 0:kvw].reshape(lead + (kv_heads, 1, HEAD_DIM))
    v = w[..., kvw:2 * kvw].reshape(lead + (kv_heads, 1, HEAD_DIM))
    return jnp.concatenate([jnp.concatenate([k, v], axis=-2).reshape(lead + (2 * kvw,)), w[..., 2 * kvw:]], axis=-1)


def _kv_deinterleave(w, kv_heads):
    kvw = kv_heads * HEAD_DIM
    lead = w.shape[:-1]
    kv = w[..., 0:2 * kvw].reshape(lead + (kv_heads, 2, HEAD_DIM))
    return jnp.concatenate([kv[..., 0, :].reshape(lead + (kvw,)), kv[..., 1, :].reshape(lead + (kvw,)), w[..., 2 * kvw:]],
                           axis=-1)


def _step(x, target, w, m, v):
    B, S, D = x.shape
    L = S + N_META
    T = B * L
    H = D // HEAD_DIM
    KV = H // Q_PER_KV
    SW = D // 2
    tm = _pick_tile(L, ROW_TILE_CAP, ROW_ALIGN)
    rc = _pick_tile(L, ROW_TILE_CAP // B, 4) * B
    tw = _pick_tile(T, 3 * ROW_TILE_CAP, ROW_ALIGN)
    my_c = lax.axis_index("c")
    my_slot = 2 * lax.axis_index("x") + lax.axis_index("y")

    groups = {"ffn1": ["ffn1_w1", "ffn1_w3", "ffn1_w2"], "mix": ["w_in", "ssm_glu_a", "ssm_glu_b", "w_out"],
              "ffn2": ["ffn2_w1", "ffn2_w3", "ffn2_w2"]}
    waves = {"first": ["ffn1_w1", "ffn1_w3"], "early": ["ffn1_w2"] + groups["mix"], "late": groups["ffn2"]}
    def own_layout(a, n):
        return jnp.swapaxes(a[0], 0, 1) if n in _TRANSPOSED else a[0]

    shards = {n: own_layout(w[n], n).astype(BF16) for n in _BIG}
    gathered = _gather_weights([shards[n] for n in waves["first"]] + [w["meta_tokens"]])
    ws = dict(zip(waves["first"], gathered[:-1]))
    meta = jnp.transpose(gathered[-1], (1, 0, 2)).reshape(N_META, D)

    def arrive(wave, landed):
        mine = [shards[n] for n in waves[wave]]
        ws.update(zip(waves[wave], _forward_halves(landed, mine, wave)))

    g_ffn1, g_mix, g_ffn2 = w["ffn1_norm"], w["mix_norm"], w["ffn2_norm"]
    g_final = w["final_norm"].reshape(1, D)

    h0, n_ffn1 = _embed_norm(x, meta, g_ffn1, tm, "ffn1_norm")

    def ffn_fwd(h, g, tag, carry=None, n=None):
        if n is None:
            n = _rmsnorm_fwd(h, g, tm, f"{tag}_norm")
        riders = None if carry is None else _gather_riders([shards[k] for k in waves[carry]])
        out = _ffn_up(n, ws[f"{tag}_w1"], ws[f"{tag}_w3"], tm, f"{tag}_up", riders)
        if carry is not None:
            out, landed = out
            arrive(carry, landed)
        a, c, s = out
        return _ffn_down(s, ws[f"{tag}_w2"], h, tm, f"{tag}_down"), (n, a, c, s)

    h1, saved1 = ffn_fwd(h0, g_ffn1, "ffn1", carry="early", n=n_ffn1)
    w_kvu = _kv_interleave(ws["w_in"][1], KV)
    hn = _rmsnorm_fwd(h1, g_mix, tm, "mix_norm")
    q = _mm_colslots(hn, ws["w_in"], BF16, "w_in_q", tm, first=0, count=1, scale=HEAD_DIM ** -0.5)
    kvu = _mm_plain(hn, w_kvu, "nn", F32, "w_in_kvu", tm)
    gates = _mm_colslots(hn, ws["w_in"], F32, "w_in_gates", tm, first=2, count=2)

    sinks = w["attn_sinks"].reshape(KV, Q_PER_KV, 1, 1)
    sink_row = jnp.broadcast_to(sinks.reshape(KV, 1, Q_PER_KV, 1), (KV, 1, Q_PER_KV, BLOCK)).reshape(KV, 1, Q_PER_KV * BLOCK)
    sink_meta = jnp.broadcast_to(sinks, (KV, Q_PER_KV, N_META, 1)).reshape(KV, Q_PER_KV * N_META, 1)
    (attn,), landed = _attn_fwd(q, kvu, sink_row, sink_meta, B, "attn_fwd",
                                _gather_riders([shards[k] for k in waves["late"]]))
    arrive("late", landed)

    def to_time_major(a2d):
        return jnp.transpose(a2d.reshape(B, L, a2d.shape[-1]), (1, 0, 2)).reshape(T, a2d.shape[-1])

    def to_batch_major(a2d):
        return jnp.transpose(a2d.reshape(L, B, a2d.shape[-1]), (1, 0, 2)).reshape(T, a2d.shape[-1])

    ssm_args = (w["ssm_a_re"][0], w["ssm_a_im"][0], w["ssm_log_step"][0], w["ssm_b_re"][0], w["ssm_b_im"][0],
                w["ssm_c_re"][0], w["ssm_c_im"][0])
    (lam, bmat, cmat), ssm_vjp = jax.vjp(_ssm_matrices, *ssm_args)
    bmat16, cmat16 = bmat.astype(BF16), cmat.astype(BF16)
    u_t = to_time_major(kvu[:, SW:])
    y_t, xs = _ssm_fwd(u_t, bmat16, cmat16, w["ssm_d"], _scan_tables(lam, B, False), B, rc, "ssm_fwd")
    y0 = to_batch_major(y_t)
    yg = _gelu_fwd(y0, tm, "gelu_fwd")
    ga = _mm_colslots(yg, ws["ssm_glu_a"], F32, "glu_a", tm)
    gb = _mm_colslots(yg, ws["ssm_glu_b"], F32, "glu_b", tm)
    merged = _merge_fwd(gates, attn, ga, gb, tm, "merge_fwd")
    h2 = _mm_rowslots(merged, ws["w_out"], h1, tm, "w_out")
    h3, saved2 = ffn_fwd(h2, g_ffn2, "ffn2")
    dh3, dh3b, dg_final, loss_row = _loss_head(h3, g_final, target, tm, "loss_head")

    grads, swapped, received = {}, {}, {}
    c_idx = my_c.reshape(1).astype(jnp.int32)
    idx = jnp.stack([my_slot, my_c]).astype(jnp.int32)

    def swap_riders(group):
        return _swap_riders([grads[n] for n in groups[group]])

    def exchange_riders(group):
        names = groups[group]
        if names[0] not in swapped:
            swapped.update(zip(names, _swap_halves([grads[n] for n in names], group)))
        return _exchange_riders([_sum_halves(grads[n], swapped[n], c_idx, f"grad_sum_halves_{n}") for n in names])

    def ffn_bwd(h, g, saved, dh, dhb, tag, dhidden_carries=None, dn_carries=None, last=False):
        n, a, c, s = saved
        w1, w3, w2 = ws[f"{tag}_w1"], ws[f"{tag}_w3"], ws[f"{tag}_w2"]
        grads[f"{tag}_w2"] = _wgrad_hidden_rows(s, dhb, tw, f"{tag}_dw2", 0.5)
        if dhidden_carries is None:
            da, dc = _ffn_dhidden(dhb, w2, a, c, tm, f"{tag}_dhidden")
        else:
            (da, dc), got = _ffn_dhidden(dhb, w2, a, c, tm, f"{tag}_dhidden", exchange_riders(dhidden_carries[1]))
            received.update(zip(groups[dhidden_carries[1]], got))
        grads[f"{tag}_w1"] = _wgrad_hidden_rows(da, n, tw, f"{tag}_dw1", 1.0)
        grads[f"{tag}_w3"] = _wgrad_hidden_rows(dc, n, tw, f"{tag}_dw3", 1.0)
        kind, group = dn_carries
        riders = swap_riders(group) if kind == "swap" else exchange_riders(group)
        (dh_in, dhb_in, grads[f"{tag}_norm"]), got = _ffn_dn(da, w1, dc, w3, h, g, dh, tm, f"{tag}_dn", riders,
                                                               B if last else None)
        return dh_in, dhb_in, got

    dh2, dh2b, got = ffn_bwd(h2, g_ffn2, saved2, dh3, dh3b, "ffn2", dn_carries=("swap", "ffn2"))
    swapped.update(zip(groups["ffn2"], got))

    grads["w_out"] = _wgrad_rowslots(merged, dh2b, tw, "dw_out")
    dattn, dgat, dgss, dga, dgb = _merge_bwd(dh2b, ws["w_out"], gates, attn, ga, gb, tm, "merge_bwd")
    grads["ssm_glu_a"] = _wgrad_colslots(yg, dga, tw, "dglu_a")
    grads["ssm_glu_b"] = _wgrad_colslots(yg, dgb, tw, "dglu_b")
    dy0 = _gelu_bwd([(dga, ws["ssm_glu_a"]), (dgb, ws["ssm_glu_b"])], y0, tm, "gelu_bwd")
    du_t, dbmat, dcmat, dlam, dd = _ssm_bwd(to_time_major(dy0), u_t, xs, bmat16, cmat16, w["ssm_d"],
                                            _scan_tables(lam, B, True), B, rc, "ssm_bwd")
    d_ssm = ssm_vjp((dlam[:, 0, :], dbmat, dcmat))
    for n, gval in zip(["ssm_a_re", "ssm_a_im", "ssm_log_step", "ssm_b_re", "ssm_b_im", "ssm_c_re", "ssm_c_im"], d_ssm):
        grads[n] = gval[None]
    grads["ssm_d"] = dd

    (dq, dkv, dsink), got = _attn_bwd(q, kvu, attn, dattn, sink_row, sink_meta, B, "attn_bwd",
                                      exchange_riders("ffn2"))
    received.update(zip(groups["ffn2"], got))
    grads["attn_sinks"] = dsink[:, 0:Q_PER_KV, 0].reshape(1, H)
    dkvu = jnp.concatenate([dkv, to_batch_major(du_t).astype(BF16)], axis=1)
    pieces = [dq, dkvu, dgat, dgss]
    dw_in = [_wgrad_plain(hn, p, f"dw_in_{k}", tw) for k, p in enumerate(pieces)]
    dw_in[1] = _kv_deinterleave(dw_in[1], KV)
    grads["w_in"] = jnp.stack(dw_in)
    w_in_parts = [ws["w_in"][0], w_kvu, ws["w_in"][2], ws["w_in"][3]]
    whole = _once((D, D), lambda i: (0, 0))
    (dh1, dh1b, grads["mix_norm"]), swap_mix = _mm_norm_bwd(
        "dhn", "nt", [(p, _spec((tm, D), lambda i: (i, 0)), wp, whole) for p, wp in zip(pieces, w_in_parts)],
        h1, g_mix, dh2, tm, swap_riders("mix"))
    swapped.update(zip(groups["mix"], swap_mix))
    grad_x, dmeta_rows, got = ffn_bwd(h0, g_ffn1, saved1, dh1, dh1b, "ffn1", dhidden_carries=("exchange", "mix"),
                                      dn_carries=("exchange", "ffn1"), last=True)
    received.update(zip(groups["ffn1"], got))

    grads["final_norm"] = dg_final
    slay = _Layout(D, 8)
    for n in _SMALL:
        slay.add(n, w[n].shape)
    slay.add("loss", (1, D))
    meta_at = slay.align(8)
    slay.add("meta", (B * N_META, D))
    small = slay.pack({**{n: grads[n] for n in _SMALL}, "loss": loss_row, "meta": dmeta_rows})
    tot_small, dmeta = _all_reduce_small(small, B, N_META, meta_at)
    loss = slay.unpack(tot_small, "loss")[0, 0]
    for n in _SMALL:
        grads[n] = slay.unpack(tot_small, n)
    cw = D // N_CHIPS
    grads["meta_tokens"] = lax.dynamic_slice_in_dim(dmeta, my_slot * cw, cw, axis=1)

    fulls = [_sum_chips(grads[n], swapped[n], received[n], idx, f"grad_sum_chips_{n}") for n in _BIG]
    for n, f in zip(_BIG, _join_halves(fulls)):
        grads[n] = f

    delta, new_m, new_v = {}, {}, {}
    for n in _BIG + ["meta_tokens"]:
        if n in _TRANSPOSED:
            flip = lambda a: jnp.swapaxes(a, -1, -2)
            outs = _adamw(flip(w[n]), grads[n], flip(m[n]), flip(v[n]), f"adamw_{n}")
            delta[n], new_m[n], new_v[n] = (flip(o) for o in outs)
            grads[n] = flip(grads[n])[None]
        else:
            delta[n], new_m[n], new_v[n] = _adamw(w[n], grads[n], m[n], v[n], f"adamw_{n}")
            grads[n] = grads[n].reshape(w[n].shape)

    def flat2d(a):
        return a.reshape(-1, a.shape[-1])

    d_, m_, v_ = _adamw_small([flat2d(w[n]) for n in _SMALL], [flat2d(grads[n]) for n in _SMALL],
                              [flat2d(m[n]) for n in _SMALL], [flat2d(v[n]) for n in _SMALL], "adamw_small")
    for i, n in enumerate(_SMALL):
        shp = w[n].shape
        delta[n], new_m[n], new_v[n] = d_[i].reshape(shp), m_[i].reshape(shp), v_[i].reshape(shp)
        grads[n] = grads[n].reshape(shp)

    return (loss, grad_x, *[grads[n] for n in _WEIGHTS], *[delta[n] for n in _WEIGHTS],
            *[new_m[n] for n in _WEIGHTS], *[new_v[n] for n in _WEIGHTS])


def kernel(x, meta_tokens, ffn1_norm, ffn1_w1, ffn1_w3, ffn1_w2, mix_norm, w_in, attn_sinks, ssm_a_re, ssm_a_im, ssm_log_step, ssm_b_re, ssm_b_im, ssm_c_re, ssm_c_im, ssm_d, ssm_glu_a, ssm_glu_b, w_out, ffn2_norm, ffn2_w1, ffn2_w3, ffn2_w2, final_norm, loss_target, m_meta_tokens, m_ffn1_norm, m_ffn1_w1, m_ffn1_w3, m_ffn1_w2, m_mix_norm, m_w_in, m_attn_sinks, m_ssm_a_re, m_ssm_a_im, m_ssm_log_step, m_ssm_b_re, m_ssm_b_im, m_ssm_c_re, m_ssm_c_im, m_ssm_d, m_ssm_glu_a, m_ssm_glu_b, m_w_out, m_ffn2_norm, m_ffn2_w1, m_ffn2_w3, m_ffn2_w2, m_final_norm, v_meta_tokens, v_ffn1_norm, v_ffn1_w1, v_ffn1_w3, v_ffn1_w2, v_mix_norm, v_w_in, v_attn_sinks, v_ssm_a_re, v_ssm_a_im, v_ssm_log_step, v_ssm_b_re, v_ssm_b_im, v_ssm_c_re, v_ssm_c_im, v_ssm_d, v_ssm_glu_a, v_ssm_glu_b, v_w_out, v_ffn2_norm, v_ffn2_w1, v_ffn2_w3, v_ffn2_w2, v_final_norm):
    args = locals()
    w = {n: args[n] for n in _WEIGHTS}
    m = {n: args["m_" + n] for n in _WEIGHTS}
    v = {n: args["v_" + n] for n in _WEIGHTS}
    return _step(x, loss_target, w, m, v)
```

```python
import functools
import math

import jax
import jax.numpy as jnp
from jax import lax
from jax.experimental import pallas as pl
from jax.experimental.pallas import tpu as pltpu

F32 = jnp.float32
BF16 = jnp.bfloat16
MESH_IDS = pl.DeviceIdType.MESH

N_CHIPS = 4
N_META = 16
HEAD_DIM = 64
Q_PER_KV = 4
QW = Q_PER_KV * HEAD_DIM
BLOCK = 128
SSM_GROUP = 16
SSM_STATE = 64
SSM_LANES = 128
GROUPS_PER_COL = SSM_LANES // SSM_GROUP
STATE_LANES = GROUPS_PER_COL * SSM_STATE
NORM_EPS = 1e-6
NEG_INF = -1e30
ADAM_LR, ADAM_B1, ADAM_B2, ADAM_EPS, ADAM_WD, ADAM_STEP = 0.001, 0.9, 0.999, 1e-08, 0.01, 10
GELU_C = math.sqrt(2.0 / math.pi)
ROW_ALIGN = 16
VMEM_LIMIT = 56 * 1024 * 1024
ROW_TILE_CAP = 688

_NN = (((1,), (0,)), ((), ()))
_NT = (((1,), (1,)), ((), ()))
_TN = (((0,), (0,)), ((), ()))
_DIMS = {"nn": _NN, "nt": _NT, "tn": _TN}


def _params(sem, **kw):
    return pltpu.CompilerParams(dimension_semantics=sem, vmem_limit_bytes=VMEM_LIMIT, **kw)


def _pick_tile(n, cap, mult):
    best = None
    for t in range(mult, min(n, cap) + 1, mult):
        if n % t == 0:
            best = t
    if best is None:
        raise ValueError(f"no tile for {n} (cap {cap}, multiple of {mult})")
    return best


def _sigmoid(x):
    return 0.5 * jnp.tanh(0.5 * x) + 0.5


def _spec(block, index_map):
    return pl.BlockSpec(block, index_map)


def _sum_dots(ins, mode):
    tot = None
    for p in range(len(ins) // 2):
        a_ref, b_ref = ins[2 * p], ins[2 * p + 1]
        for sl in ([None] if len(b_ref.shape) == 2 else range(b_ref.shape[0])):
            if sl is None:
                a, b = a_ref[...], b_ref[...]
            elif len(a_ref.shape) == 3:
                a, b = a_ref[sl], b_ref[sl]
            else:
                width = a_ref.shape[1] // b_ref.shape[0]
                a, b = a_ref[:, sl * width:(sl + 1) * width], b_ref[sl]
            d = lax.dot_general(a.astype(BF16), b.astype(BF16), _DIMS[mode], preferred_element_type=F32)
            tot = d if tot is None else tot + d
    return tot


def _mm(name, grid, kaxis, mode, pairs, out_shape, out_spec, scale=1.0, res=None):
    npairs = len(pairs)
    has_res = res is not None
    gk = 1 if kaxis is None else grid[kaxis]
    acc_shape = tuple(d for d in out_spec.block_shape if d is not None)

    def body(*refs):
        res_ref = refs[2 * npairs] if has_res else None
        o_ref = refs[2 * npairs + has_res]
        tot = _sum_dots(refs[:2 * npairs], mode)

        def finish(acc):
            r = acc * scale if scale != 1.0 else acc
            if has_res:
                r = res_ref[...] + r
            o_ref[...] = r.astype(o_ref.dtype)

        if gk == 1:
            finish(tot)
        else:
            acc_ref = refs[-1]
            k = pl.program_id(kaxis)

            @pl.when(k == 0)
            def _():
                acc_ref[...] = tot

            @pl.when(k > 0)
            def _():
                acc_ref[...] += tot

            @pl.when(k == gk - 1)
            def _():
                finish(acc_ref[...])

    in_specs, args = [], []
    for a, a_spec, b, b_spec in pairs:
        in_specs += [a_spec, b_spec]
        args += [a, b]
    if has_res:
        in_specs.append(res[1])
        args.append(res[0])
    sem = tuple("arbitrary" if ax == kaxis else "parallel" for ax in range(len(grid)))
    return pl.pallas_call(
        body, name=name, grid=grid, in_specs=in_specs, out_specs=out_spec, out_shape=out_shape,
        scratch_shapes=[pltpu.VMEM(acc_shape, F32)] if gk > 1 else [],
        compiler_params=_params(sem),
    )(*args)


def _mm_plain(a, b, mode, out_dtype, name, tm, scale=1.0):
    M, K = a.shape
    N = b.shape[1] if mode == "nn" else b.shape[0]
    return _mm(name, (M // tm,), None, mode,
               [(a, _spec((tm, K), lambda i: (i, 0)), b, _spec(b.shape, lambda i: (0, 0)))],
               jax.ShapeDtypeStruct((M, N), out_dtype), _spec((tm, N), lambda i: (i, 0)), scale=scale)


def _wgrad_plain(a, b, name, tr):
    R, M = a.shape
    N = b.shape[1]
    return _mm(name, (R // tr,), 0, "tn",
               [(a, _spec((tr, M), lambda r: (r, 0)), b, _spec((tr, N), lambda r: (r, 0)))],
               jax.ShapeDtypeStruct((M, N), F32), _spec((M, N), lambda r: (0, 0)))


def _for_real_rows(tile, tpe, tm, hbm, buf, fn):
    tb, tj = tile // tpe, tile % tpe

    @pl.when(tj == 0)
    def _():
        fn(hbm.at[tb, pl.ds(0, tm - N_META), :], buf.at[pl.ds(N_META, tm - N_META), :])

    @pl.when(tj > 0)
    def _():
        fn(hbm.at[tb, pl.ds(tj * tm - N_META, tm), :], buf)


def _embed_norm(x, meta, g, tm, name):
    B, S, D = x.shape
    L = S + N_META
    T = B * L
    nt = T // tm
    tpe = L // tm

    def body(x_hbm, meta_ref, g_ref, h_ref, n_ref, xbuf, sems):
        i = pl.program_id(0)
        slot = i % 2

        def fetch(tile, sl, act):
            _for_real_rows(tile, tpe, tm, x_hbm, xbuf.at[sl], lambda src, dst: act(pltpu.make_async_copy(src, dst, sems.at[sl])))

        @pl.when(i == 0)
        def _():
            fetch(i, slot, lambda cp: cp.start())

        @pl.when(i + 1 < nt)
        def _():
            fetch(i + 1, 1 - slot, lambda cp: cp.start())

        fetch(i, slot, lambda cp: cp.wait())

        @pl.when(i % tpe == 0)
        def _():
            xbuf[slot, 0:N_META, :] = meta_ref[...]

        hv = xbuf[slot]
        h_ref[...] = hv
        r = lax.rsqrt(jnp.mean(hv * hv, axis=-1, keepdims=True) + NORM_EPS)
        n_ref[...] = ((hv * r) * g_ref[...]).astype(BF16)

    row = pl.BlockSpec((tm, D), lambda i: (i, 0))
    return pl.pallas_call(
        body, name=name, grid=(nt,),
        in_specs=[pl.BlockSpec(memory_space=pl.ANY), pl.BlockSpec((N_META, D), lambda i: (0, 0)),
                  pl.BlockSpec((1, D), lambda i: (0, 0))],
        out_specs=[row, row],
        out_shape=[jax.ShapeDtypeStruct((T, D), F32), jax.ShapeDtypeStruct((T, D), BF16)],
        scratch_shapes=[pltpu.VMEM((2, tm, D), F32), pltpu.SemaphoreType.DMA((2,))],
        compiler_params=_params(("arbitrary",)),
    )(x, meta, g)


def _rmsnorm_fwd(h, g, tm, name):
    T, D = h.shape

    def body(h_ref, g_ref, o_ref):
        x = h_ref[...]
        r = lax.rsqrt(jnp.mean(x * x, axis=-1, keepdims=True) + NORM_EPS)
        o_ref[...] = ((x * r) * g_ref[...]).astype(BF16)

    return pl.pallas_call(
        body, name=name, grid=(T // tm,),
        in_specs=[pl.BlockSpec((tm, D), lambda i: (i, 0)), pl.BlockSpec((1, D), lambda i: (0, 0))],
        out_specs=pl.BlockSpec((tm, D), lambda i: (i, 0)),
        out_shape=jax.ShapeDtypeStruct((T, D), BF16),
        compiler_params=_params(("parallel",)),
    )(h, g)


def _fold8(x):
    return jnp.sum(x.reshape(x.shape[0] // 8, 8, x.shape[1]), axis=0)


def _mm_norm_bwd(name, mode, pairs, h, g, dres, tm, riders=None, examples=None):
    T, D = h.shape
    nt = T // tm
    npairs = len(pairs)
    tpe = None if examples is None else T // examples // tm

    def body(*refs):
        if examples is None:
            h_ref, g_ref, dres_ref, dh_ref, dhb_ref, dg_ref, acc_ref = refs[2 * npairs:]
        else:
            h_ref, g_ref, dres_ref, dx_hbm, dmeta_ref, dg_ref, acc_ref, dbuf, sems = refs[2 * npairs:]
        i = pl.program_id(0)
        x = h_ref[...]
        r = lax.rsqrt(jnp.mean(x * x, axis=-1, keepdims=True) + NORM_EPS)
        xhat = x * r
        dy = _sum_dots(refs[:2 * npairs], mode)
        dxhat = dy * g_ref[...]
        dx = r * (dxhat - xhat * jnp.mean(dxhat * xhat, axis=-1, keepdims=True))
        dh = dres_ref[...] + dx
        if examples is None:
            dh_ref[...] = dh
            dhb_ref[...] = dh.astype(BF16)
        else:
            slot = i % 2

            def push(tile, sl, act):
                _for_real_rows(tile, tpe, tm, dx_hbm, dbuf.at[sl],
                               lambda dst, src: act(pltpu.make_async_copy(src, dst, sems.at[sl])))

            dbuf[slot] = dh
            push(i, slot, lambda cp: cp.start())

            @pl.when(i > 0)
            def _():
                push(i - 1, 1 - slot, lambda cp: cp.wait())

            @pl.when(i == nt - 1)
            def _():
                push(i, slot, lambda cp: cp.wait())

            @pl.when(i == 0)
            def _():
                dmeta_ref[...] = dh[0:N_META]

            @pl.when((i > 0) & (i % tpe == 0))
            def _():
                dmeta_ref[...] += dh[0:N_META]
        part = _fold8(dy * xhat)

        @pl.when(i == 0)
        def _():
            acc_ref[...] = part

        @pl.when(i > 0)
        def _():
            acc_ref[...] += part

        @pl.when(i == nt - 1)
        def _():
            dg_ref[...] = jnp.sum(acc_ref[...], axis=0, keepdims=True)

    row = pl.BlockSpec((tm, D), lambda i: (i, 0))
    vec = pl.BlockSpec((1, D), lambda i: (0, 0))
    in_specs, args = [], []
    for a, a_spec, b, b_spec in pairs:
        in_specs += [a_spec, b_spec]
        args += [a, b]
    if examples is None:
        return _call(body, name, (nt,), in_specs + [row, vec, row], [row, row, vec],
                     [jax.ShapeDtypeStruct((T, D), F32), jax.ShapeDtypeStruct((T, D), BF16), jax.ShapeDtypeStruct((1, D), F32)],
                     [pltpu.VMEM((8, D), F32)], ("arbitrary",), (*args, h, g, dres), riders)
    S = T // examples - N_META
    return _call(body, name, (nt,), in_specs + [row, vec, row],
                 [_ANY, pl.BlockSpec((N_META, D), lambda i: (0, 0)), vec],
                 [jax.ShapeDtypeStruct((examples, S, D), F32), jax.ShapeDtypeStruct((N_META, D), F32),
                  jax.ShapeDtypeStruct((1, D), F32)],
                 [pltpu.VMEM((8, D), F32), pltpu.VMEM((2, tm, D), F32), pltpu.SemaphoreType.DMA((2,))],
                 ("arbitrary",), (*args, h, g, dres), riders)


def _ffn_up(n, w1t, w3t, tm, name, riders=None):
    T, D = n.shape
    Fs = w1t.shape[1]

    def body(n_ref, w1_ref, w3_ref, a_ref, c_ref, s_ref):
        x = n_ref[...]
        a = lax.dot_general(x, w1_ref[...], _NT, preferred_element_type=F32)
        c = lax.dot_general(x, w3_ref[...], _NT, preferred_element_type=F32)
        a_ref[...] = a.astype(BF16)
        c_ref[...] = c.astype(BF16)
        s_ref[...] = (a * _sigmoid(a) * c).astype(BF16)

    w_spec = _spec((None, Fs, D), lambda s, i: (s, 0, 0))
    o_spec = _spec((None, tm, Fs), lambda s, i: (s, i, 0))
    o_shape = jax.ShapeDtypeStruct((N_CHIPS, T, Fs), BF16)
    return _call(body, name, (N_CHIPS, T // tm), [_spec((tm, D), lambda s, i: (i, 0)), w_spec, w_spec],
                 [o_spec, o_spec, o_spec], [o_shape, o_shape, o_shape], [], ("parallel", "parallel"), (n, w1t, w3t), riders)


def _ffn_down(s, w2, h, tm, name):
    _, T, Fs = s.shape
    D = w2.shape[2]
    row = _spec((tm, D), lambda i: (i, 0))
    return _mm(name, (T // tm,), None, "nn",
               [(s, _spec((N_CHIPS, tm, Fs), lambda i: (0, i, 0)), w2, _spec((N_CHIPS, Fs, D), lambda i: (0, 0, 0)))],
               jax.ShapeDtypeStruct((T, D), F32), row, scale=0.5, res=(h, row))


def _ffn_dhidden(dhb, w2, a, c, tm, name, riders=None):
    T, D = dhb.shape
    Fs = w2.shape[1]

    def body(dh_ref, w2_ref, a_ref, c_ref, da_ref, dc_ref):
        d = 0.5 * lax.dot_general(dh_ref[...], w2_ref[pl.program_id(1)], _NT, preferred_element_type=F32)
        av = a_ref[...].astype(F32)
        cv = c_ref[...].astype(F32)
        sg = _sigmoid(av)
        da_ref[...] = (d * cv * (sg * (1.0 + av * (1.0 - sg)))).astype(BF16)
        dc_ref[...] = (d * (av * sg)).astype(BF16)

    h_spec = _spec((None, tm, Fs), lambda i, s: (s, i, 0))
    o_shape = jax.ShapeDtypeStruct((N_CHIPS, T, Fs), BF16)
    return _call(body, name, (T // tm, N_CHIPS),
                 [_spec((tm, D), lambda i, s: (i, 0)), _once((N_CHIPS, Fs, D), lambda i, s: (0, 0, 0)), h_spec, h_spec],
                 [h_spec, h_spec], [o_shape, o_shape], [], ("parallel", "parallel"), (dhb, w2, a, c), riders)


def _wgrad_hidden_rows(s, dhb, tr, name, scale):
    _, T, Fs = s.shape
    D = dhb.shape[1]
    return _mm(name, (N_CHIPS, T // tr), 1, "tn",
               [(s, _spec((None, tr, Fs), lambda k, r: (k, r, 0)), dhb, _spec((tr, D), lambda k, r: (r, 0)))],
               jax.ShapeDtypeStruct((N_CHIPS, Fs, D), F32), _spec((None, Fs, D), lambda k, r: (k, 0, 0)), scale=scale)


def _once(block, index_map):
    return pl.BlockSpec(block, index_map, pipeline_mode=pl.Buffered(1))


def _ffn_dn(da, w1t, dc, w3t, h, g, dres, tm, name, riders=None, examples=None):
    _, T, Fs = da.shape
    D = w1t.shape[2]
    h_spec = _spec((N_CHIPS, tm, Fs), lambda i: (0, i, 0))
    w_spec = _once((N_CHIPS, Fs, D), lambda i: (0, 0, 0))
    return _mm_norm_bwd(name, "nn", [(da, h_spec, w1t, w_spec), (dc, h_spec, w3t, w_spec)], h, g, dres, tm, riders,
                        examples)


def _mm_side_by_side(a, w, mode, out_dtype, name, tm, first=0, count=N_CHIPS, scale=1.0):
    T, K = a.shape
    assert first % count == 0
    n = w.shape[2] if mode == "nn" else w.shape[1]

    def body(a_ref, w_ref, o_ref):
        av = a_ref[...].astype(BF16)
        for j in range(count):
            r = lax.dot_general(av, w_ref[j].astype(BF16), _DIMS[mode], preferred_element_type=F32)
            o_ref[:, j * n:(j + 1) * n] = (r * scale if scale != 1.0 else r).astype(o_ref.dtype)

    return pl.pallas_call(
        body, name=name, grid=(T // tm,),
        in_specs=[_spec((tm, K), lambda i: (i, 0)), _once((count,) + w.shape[1:], lambda i: (first // count, 0, 0))],
        out_specs=_spec((tm, count * n), lambda i: (i, 0)),
        out_shape=jax.ShapeDtypeStruct((T, count * n), out_dtype),
        compiler_params=_params(("parallel",)),
    )(a, w)


def _mm_colslots(a, w, out_dtype, name, tm, first=0, count=N_CHIPS, scale=1.0):
    return _mm_side_by_side(a, w, "nn", out_dtype, name, tm, first, count, scale)


def _wgrad_colslots(a, d, tr, name):
    T, K = a.shape
    Ns = d.shape[1] // N_CHIPS
    return _mm(name, (N_CHIPS, T // tr), 1, "tn",
               [(a, _spec((tr, K), lambda k, r: (r, 0)), d, _spec((tr, Ns), lambda k, r: (r, k)))],
               jax.ShapeDtypeStruct((N_CHIPS, K, Ns), F32), _spec((None, K, Ns), lambda k, r: (k, 0, 0)))


def _mm_rowslots(a, w, h, tm, name):
    T = a.shape[0]
    N = w.shape[2]
    row = _spec((tm, N), lambda i: (i, 0))
    return _mm(name, (T // tm,), None, "nn",
               [(a, _spec((tm, a.shape[1]), lambda i: (i, 0)), w, _once(w.shape, lambda i: (0, 0, 0)))],
               jax.ShapeDtypeStruct((T, N), F32), row, res=(h, row))


def _wgrad_rowslots(a, d, tr, name):
    T = a.shape[0]
    Ks = a.shape[1] // N_CHIPS
    N = d.shape[1]
    return _mm(name, (N_CHIPS, T // tr), 1, "tn",
               [(a, _spec((tr, Ks), lambda k, r: (r, k)), d, _spec((tr, N), lambda k, r: (r, 0)))],
               jax.ShapeDtypeStruct((N_CHIPS, Ks, N), F32), _spec((None, Ks, N), lambda k, r: (k, 0, 0)))


def _gelu_parts(x):
    inner = GELU_C * (x + 0.044715 * (x * x * x))
    t = jnp.tanh(inner)
    return t, GELU_C * (1.0 + 3.0 * 0.044715 * (x * x))


def _gelu_fwd(y, tm, name):
    T, W = y.shape

    def body(y_ref, o_ref):
        x = y_ref[...]
        t, _ = _gelu_parts(x)
        o_ref[...] = (0.5 * x * (1.0 + t)).astype(BF16)

    spec = pl.BlockSpec((tm, W), lambda i: (i, 0))
    return pl.pallas_call(body, name=name, grid=(T // tm,), in_specs=[spec], out_specs=spec,
                          out_shape=jax.ShapeDtypeStruct((T, W), BF16),
                          compiler_params=_params(("parallel",)))(y)


def _gelu_bwd(pairs, y, tm, name):
    T, W = y.shape
    npairs = len(pairs)

    def body(*refs):
        y_ref, o_ref = refs[2 * npairs], refs[2 * npairs + 1]
        x = y_ref[...]
        t, dinner = _gelu_parts(x)
        o_ref[...] = _sum_dots(refs[:2 * npairs], "nt") * (0.5 * (1.0 + t) + 0.5 * x * (1.0 - t * t) * dinner)

    spec = pl.BlockSpec((tm, W), lambda i: (i, 0))
    in_specs, args = [], []
    for d, w in pairs:
        in_specs += [_spec((tm, d.shape[1]), lambda i: (i, 0)), _once(w.shape, lambda i: (0, 0, 0))]
        args += [d, w]
    return pl.pallas_call(body, name=name, grid=(T // tm,), in_specs=in_specs + [spec], out_specs=spec,
                          out_shape=jax.ShapeDtypeStruct((T, W), F32),
                          compiler_params=_params(("parallel",)))(*args, y)


def _merge_cols(D):
    cb = 512 if D % 512 == 0 else D
    return cb, D // cb


def _merge_fwd(gates, attn, ga, gb, tm, name):
    T, D = attn.shape
    cb, nc = _merge_cols(D)

    def body(gat_ref, gss_ref, attn_ref, ga_ref, gb_ref, o_ref):
        ssm = ga_ref[...] * _sigmoid(gb_ref[...])
        o_ref[...] = (_sigmoid(gat_ref[...]) * attn_ref[...] + _sigmoid(gss_ref[...]) * ssm).astype(BF16)

    def col(block):
        return pl.BlockSpec((tm, cb), lambda i, j: (i, block * nc + j))

    return pl.pallas_call(
        body, name=name, grid=(T // tm, nc),
        in_specs=[col(0), col(1), col(0), col(0), col(0)],
        out_specs=col(0), out_shape=jax.ShapeDtypeStruct((T, D), BF16),
        compiler_params=_params(("parallel", "parallel")),
    )(gates, gates, attn, ga, gb)


def _merge_bwd(dhb, w_out, gates, attn, ga, gb, tm, name):
    T, D = attn.shape
    cb, nc = _merge_cols(D)
    Ks = w_out.shape[1]
    spb = cb // Ks

    def body(dh_ref, w_ref, gat_ref, gss_ref, attn_ref, ga_ref, gb_ref, dattn_ref, dgat_ref, dgss_ref, dga_ref, dgb_ref):
        dh = dh_ref[...]
        d = jnp.concatenate([lax.dot_general(dh, w_ref[s], _NT, preferred_element_type=F32) for s in range(spb)], axis=1)
        sa = _sigmoid(gat_ref[...])
        ss = _sigmoid(gss_ref[...])
        sb = _sigmoid(gb_ref[...])
        gav = ga_ref[...]
        dattn_ref[...] = d * sa
        dgat_ref[...] = (d * attn_ref[...] * (sa * (1.0 - sa))).astype(BF16)
        dgss_ref[...] = (d * (gav * sb) * (ss * (1.0 - ss))).astype(BF16)
        dssm = d * ss
        dga_ref[...] = (dssm * sb).astype(BF16)
        dgb_ref[...] = (dssm * gav * (sb * (1.0 - sb))).astype(BF16)

    def col(block):
        return pl.BlockSpec((tm, cb), lambda i, j: (i, block * nc + j))

    b16 = jax.ShapeDtypeStruct((T, D), BF16)
    return pl.pallas_call(
        body, name=name, grid=(T // tm, nc),
        in_specs=[pl.BlockSpec((tm, D), lambda i, j: (i, 0)), pl.BlockSpec((spb, Ks, D), lambda i, j: (j, 0, 0)),
                  col(0), col(1), col(0), col(0), col(0)],
        out_specs=[col(0)] * 5,
        out_shape=[jax.ShapeDtypeStruct((T, D), F32), b16, b16, b16, b16],
        compiler_params=_params(("parallel", "parallel")),
    )(dhb, w_out, gates, gates, attn, ga, gb)


def _loss_head(h, g, target, tm, name):
    T, D = h.shape
    B, S, _ = target.shape
    L = S + N_META
    nt = T // tm
    tpe = L // tm

    def body(h_ref, g_ref, t_hbm, dh_ref, dhb_ref, dg_ref, loss_ref, tbuf, acc_g, acc_l, sems):
        i = pl.program_id(0)
        j = i % tpe
        slot = i % 2

        def fetch(tile, sl, act):
            tb, tj = tile // tpe, tile % tpe

            @pl.when(tj == 0)
            def _():
                act(pltpu.make_async_copy(t_hbm.at[tb, pl.ds(0, tm - N_META), :],
                                          tbuf.at[sl, pl.ds(N_META, tm - N_META), :], sems.at[sl]))

            @pl.when(tj > 0)
            def _():
                act(pltpu.make_async_copy(t_hbm.at[tb, pl.ds(tj * tm - N_META, tm), :], tbuf.at[sl], sems.at[sl]))

        @pl.when(i == 0)
        def _():
            tbuf[:, 0:N_META, :] = jnp.zeros((2, N_META, D), F32)
            fetch(i, slot, lambda cp: cp.start())

        @pl.when(i + 1 < nt)
        def _():
            fetch(i + 1, 1 - slot, lambda cp: cp.start())

        fetch(i, slot, lambda cp: cp.wait())

        x = h_ref[...]
        gv = g_ref[...]
        r = lax.rsqrt(jnp.mean(x * x, axis=-1, keepdims=True) + NORM_EPS)
        xhat = x * r
        pos = j * tm + lax.broadcasted_iota(jnp.int32, (tm, 1), 0)
        err = jnp.where(pos >= N_META, xhat * gv - tbuf[slot], 0.0)
        dy = err * (1.0 / D)
        dxhat = dy * gv
        dh = r * (dxhat - xhat * jnp.mean(dxhat * xhat, axis=-1, keepdims=True))
        dh_ref[...] = dh
        dhb_ref[...] = dh.astype(BF16)
        pg = _fold8(dy * xhat)
        pe = _fold8(err * err)

        @pl.when(i == 0)
        def _():
            acc_g[...] = pg
            acc_l[...] = pe

        @pl.when(i > 0)
        def _():
            acc_g[...] += pg
            acc_l[...] += pe

        @pl.when(i == nt - 1)
        def _():
            dg_ref[...] = jnp.sum(acc_g[...], axis=0, keepdims=True)
            loss_ref[...] = jnp.full((1, D), (0.5 / D) * jnp.sum(acc_l[...]), F32)

    row = pl.BlockSpec((tm, D), lambda i: (i, 0))
    vec = pl.BlockSpec((1, D), lambda i: (0, 0))
    return pl.pallas_call(
        body, name=name, grid=(nt,),
        in_specs=[row, vec, pl.BlockSpec(memory_space=pl.ANY)], out_specs=[row, row, vec, vec],
        out_shape=[jax.ShapeDtypeStruct((T, D), F32), jax.ShapeDtypeStruct((T, D), BF16),
                   jax.ShapeDtypeStruct((1, D), F32), jax.ShapeDtypeStruct((1, D), F32)],
        scratch_shapes=[pltpu.VMEM((2, tm, D), F32), pltpu.VMEM((8, D), F32), pltpu.VMEM((8, D), F32),
                        pltpu.SemaphoreType.DMA((2,))],
        compiler_params=_params(("arbitrary",)),
    )(h, g, target)


def _heads_to_rows(blk):
    return jnp.concatenate([blk[:, g * HEAD_DIM:(g + 1) * HEAD_DIM] for g in range(Q_PER_KV)], axis=0)


def _rows_to_heads(x):
    rows = x.shape[0] // Q_PER_KV
    return jnp.concatenate([x[g * rows:(g + 1) * rows] for g in range(Q_PER_KV)], axis=1)


def _causal(R):
    kj = lax.broadcasted_iota(jnp.int32, (BLOCK, R), 0)
    qi = lax.broadcasted_iota(jnp.int32, (BLOCK, R), 1) & (BLOCK - 1)
    return kj <= qi


def _band_probs(s_band, s_m, sink):
    m = jnp.maximum(jnp.maximum(jnp.max(s_band, axis=0, keepdims=True), jnp.max(s_m, axis=0, keepdims=True)), sink)
    e_b, e_m, e_s = jnp.exp(s_band - m), jnp.exp(s_m - m), jnp.exp(sink - m)
    inv = 1.0 / (jnp.sum(e_b, axis=0, keepdims=True) + jnp.sum(e_m, axis=0, keepdims=True) + e_s)
    return e_b * inv, e_m * inv, e_s * inv


def _fold_band(tri, two):
    return jnp.where(tri, two[BLOCK:2 * BLOCK], two[0:BLOCK])


def _unfold_band(tri, band):
    return jnp.concatenate([jnp.where(tri, 0.0, band), jnp.where(tri, band, 0.0)], axis=0)


def _meta_probs(qm, k_m, sink_m):
    R = qm.shape[0]
    s = lax.dot_general(qm, k_m, _NT, preferred_element_type=F32)
    qi = lax.broadcasted_iota(jnp.int32, (R, N_META), 0) & (N_META - 1)
    kj = lax.broadcasted_iota(jnp.int32, (R, N_META), 1)
    s = jnp.where(kj <= qi, s, NEG_INF)
    m = jnp.maximum(jnp.max(s, axis=-1, keepdims=True), sink_m)
    e, e_s = jnp.exp(s - m), jnp.exp(sink_m - m)
    inv = 1.0 / (jnp.sum(e, axis=-1, keepdims=True) + e_s)
    return e * inv, e_s * inv


def _block_start(n):
    return pl.multiple_of(N_META + n * BLOCK, ROW_ALIGN)


def _kv(blk):
    return blk[:, 0:HEAD_DIM], blk[:, HEAD_DIM:2 * HEAD_DIM]


def _attn_fwd(q, kv, sink_row, sink_meta, B, name, riders=None):
    T, D = q.shape
    L = T // B
    KV = D // QW
    nb = (L - N_META) // BLOCK

    def body(q_ref, kv_ref, sk_ref, skm_ref, o_ref, kvs):
        kvs[...] = kv_ref[...].astype(BF16)
        k_m, v_m = _kv(kvs[0:N_META, :])
        p, _ = _meta_probs(_heads_to_rows(q_ref[0:N_META, :]), k_m, skm_ref[0])
        o_ref[0:N_META, :] = _rows_to_heads(jnp.dot(p.astype(BF16), v_m, preferred_element_type=F32))
        tri = _causal(Q_PER_KV * BLOCK)

        def block(cur, first, keys):
            k2, v2 = _kv(kvs[keys, :])
            qb = _heads_to_rows(q_ref[pl.ds(cur, BLOCK), :])
            st = lax.dot_general(k2, qb, _NT, preferred_element_type=F32)
            smt = lax.dot_general(k_m, qb, _NT, preferred_element_type=F32)
            s_band = jnp.where(tri, st, NEG_INF) if first else _fold_band(tri, st)
            p_b, p_m, _ = _band_probs(s_band, smt, sk_ref[0])
            p2 = (p_b if first else _unfold_band(tri, p_b)).astype(BF16)
            o = (lax.dot_general(p2, v2, _TN, preferred_element_type=F32)
                 + lax.dot_general(p_m.astype(BF16), v_m, _TN, preferred_element_type=F32))
            o_ref[pl.ds(cur, BLOCK), :] = _rows_to_heads(o)

        block(N_META, True, pl.ds(N_META, BLOCK))

        def step(n, carry):
            block(_block_start(n), False, pl.ds(_block_start(n - 1), 2 * BLOCK))
            return carry

        lax.fori_loop(1, nb, step, 0, unroll=5 if (nb - 1) % 5 == 0 else 1)

    q_spec = pl.BlockSpec((L, QW), lambda b, h: (b, h))
    return _call(body, name, (B, KV),
                 [q_spec, pl.BlockSpec((L, 2 * HEAD_DIM), lambda b, h: (b, h)),
                  pl.BlockSpec((1, 1, Q_PER_KV * BLOCK), lambda b, h: (h, 0, 0)),
                  pl.BlockSpec((1, Q_PER_KV * N_META, 1), lambda b, h: (h, 0, 0))],
                 [q_spec], [jax.ShapeDtypeStruct((T, D), F32)], [pltpu.VMEM((L, 2 * HEAD_DIM), BF16)],
                 ("parallel", "parallel"), (q, kv, sink_row, sink_meta), riders)


def _attn_bwd(q, kv, o, do, sink_row, sink_meta, B, name, riders=None):
    T, D = q.shape
    L = T // B
    KV = D // QW
    nb = (L - N_META) // BLOCK
    R = Q_PER_KV * BLOCK
    scale = HEAD_DIM ** -0.5

    def head_totals(col, rows_per_head):
        rid = lax.broadcasted_iota(jnp.int32, (8, 128), 0)
        out = jnp.zeros((8, 128), F32)
        for g in range(Q_PER_KV):
            out = out + jnp.where(rid == g, jnp.sum(col[g * rows_per_head:(g + 1) * rows_per_head, :]), 0.0)
        return out

    def body(q_ref, kv_ref, o_ref, do_ref, sk_ref, skm_ref, dq_ref, dkv_ref, dsk_ref, kvs, acc, acc_sink):
        b = pl.program_id(1)
        kvs[...] = kv_ref[...].astype(BF16)
        acc[...] = jnp.zeros_like(acc)
        k_m, v_m = _kv(kvs[0:N_META, :])

        qm = _heads_to_rows(q_ref[0:N_META, :])
        dom = _heads_to_rows(do_ref[0:N_META, :])
        delta = jnp.sum(dom * _heads_to_rows(o_ref[0:N_META, :]), axis=-1, keepdims=True)
        p, p_s = _meta_probs(qm, k_m, skm_ref[0])
        domb = dom.astype(BF16)
        ds = (p * (lax.dot_general(domb, v_m, _NT, preferred_element_type=F32) - delta)).astype(BF16)
        dq_ref[0:N_META, :] = _rows_to_heads(jnp.dot(ds, k_m, preferred_element_type=F32) * scale).astype(BF16)
        acc[0:N_META, :] += jnp.concatenate([lax.dot_general(ds, qm, _TN, preferred_element_type=F32),
                                             lax.dot_general(p.astype(BF16), domb, _TN, preferred_element_type=F32)], axis=1)
        sink_tot = head_totals(-p_s * delta, N_META)
        tri = _causal(R)
        acc_sink[...] = jnp.zeros_like(acc_sink)
        ones = jnp.ones((8, HEAD_DIM), BF16)

        def block(cur, first, keys):
            k2, v2 = _kv(kvs[keys, :])
            rows = pl.ds(cur, BLOCK)
            qb = _heads_to_rows(q_ref[rows, :])
            dob = _heads_to_rows(do_ref[rows, :])
            prod = dob * _heads_to_rows(o_ref[rows, :])
            hi = prod.astype(BF16)
            lo = (prod - hi.astype(F32)).astype(BF16)
            delta = (lax.dot_general(ones, hi, _NT, preferred_element_type=F32)
                     + lax.dot_general(ones, lo, _NT, preferred_element_type=F32))[0:1]
            dobb = dob.astype(BF16)
            st = lax.dot_general(k2, qb, _NT, preferred_element_type=F32)
            smt = lax.dot_general(k_m, qb, _NT, preferred_element_type=F32)
            s_band = jnp.where(tri, st, NEG_INF) if first else _fold_band(tri, st)
            p_b, p_m, p_s = _band_probs(s_band, smt, sk_ref[0])
            dpt = lax.dot_general(v2, dobb, _NT, preferred_element_type=F32)
            dpm = lax.dot_general(v_m, dobb, _NT, preferred_element_type=F32)
            ds_b = p_b * ((dpt if first else _fold_band(tri, dpt)) - delta)
            ds2 = (ds_b if first else _unfold_band(tri, ds_b)).astype(BF16)
            p2 = (p_b if first else _unfold_band(tri, p_b)).astype(BF16)
            dsm = (p_m * (dpm - delta)).astype(BF16)
            pm = p_m.astype(BF16)
            dq = (lax.dot_general(ds2, k2, _TN, preferred_element_type=F32)
                  + lax.dot_general(dsm, k_m, _TN, preferred_element_type=F32))
            dq_ref[rows, :] = _rows_to_heads(dq * scale).astype(BF16)
            acc[keys, :] += jnp.concatenate([jnp.dot(ds2, qb, preferred_element_type=F32),
                                             jnp.dot(p2, dobb, preferred_element_type=F32)], axis=1)
            acc[0:N_META, :] += jnp.concatenate([jnp.dot(dsm, qb, preferred_element_type=F32),
                                                 jnp.dot(pm, dobb, preferred_element_type=F32)], axis=1)
            acc_sink[0:1, :] += -p_s * delta

        block(N_META, True, pl.ds(N_META, BLOCK))

        def step(n, carry):
            block(_block_start(n), False, pl.ds(_block_start(n - 1), 2 * BLOCK))
            return carry

        lax.fori_loop(1, nb, step, 0, unroll=5 if (nb - 1) % 5 == 0 else 1)
        dkv_ref[...] = acc[...].astype(BF16)
        rid = lax.broadcasted_iota(jnp.int32, (8, 128), 0)
        tot = sink_tot
        for g in range(Q_PER_KV):
            tot = tot + jnp.where(rid == g, jnp.sum(acc_sink[:, g * BLOCK:(g + 1) * BLOCK]), 0.0)

        @pl.when(b == 0)
        def _():
            dsk_ref[0] = tot

        @pl.when(b > 0)
        def _():
            dsk_ref[0] += tot

    q_spec = pl.BlockSpec((L, QW), lambda h, b: (b, h))
    kv_spec = pl.BlockSpec((L, 2 * HEAD_DIM), lambda h, b: (b, h))
    return _call(body, name, (KV, B),
                 [q_spec, kv_spec, q_spec, q_spec,
                  pl.BlockSpec((1, 1, R), lambda h, b: (h, 0, 0)),
                  pl.BlockSpec((1, Q_PER_KV * N_META, 1), lambda h, b: (h, 0, 0))],
                 [q_spec, kv_spec, pl.BlockSpec((1, 8, 128), lambda h, b: (h, 0, 0))],
                 [jax.ShapeDtypeStruct((T, D), BF16), jax.ShapeDtypeStruct((T, KV * 2 * HEAD_DIM), BF16),
                  jax.ShapeDtypeStruct((KV, 8, 128), F32)],
                 [pltpu.VMEM((L, 2 * HEAD_DIM), BF16), pltpu.VMEM((L, 2 * HEAD_DIM), F32), pltpu.VMEM((8, R), F32)],
                 ("parallel", "arbitrary"), (q, kv, o, do, sink_row, sink_meta), riders)


def _cmul_add(acc_r, acc_i, lr, li, xr, xi):
    return acc_r + (lr * xr - li * xi), acc_i + (lr * xi + li * xr)


def _cols_per_step(ncol):
    for cps in (4, 2):
        if ncol % cps == 0:
            return cps
    return 1


def _ssm_fwd(u, bmat, cmat, dskip, tables, nbatch, rc, name):
    T, W = u.shape
    ncol = W // SSM_LANES
    nch = T // rc
    S = STATE_LANES
    cps = _cols_per_step(ncol)
    assert nbatch == 4

    def body(u_ref, b_ref, c_ref, d_ref, tab_ref, y_ref, xs_ref, st_ref, carry_ref):
        ch = pl.program_id(1)

        @pl.when(ch == 0)
        def _():
            carry_ref[...] = jnp.zeros_like(carry_ref)

        uv = u_ref[...]
        for k in range(cps):
            st_ref[:, 2 * S * k:2 * S * (k + 1)] = jnp.dot(uv[:, SSM_LANES * k:SSM_LANES * (k + 1)].astype(BF16), b_ref[k],
                                                           preferred_element_type=F32)
        low = lax.broadcasted_iota(jnp.int32, (8, S), 0) < nbatch

        def tile(k, r0, c_r, c_i):
            re, im = slice(2 * S * k, 2 * S * k + S), slice(2 * S * k + S, 2 * S * (k + 1))
            la_r, la_i = tab_ref[k, :, 0:S], tab_ref[k, :, S:2 * S]
            lb_r, lb_i = tab_ref[k, :, 2 * S:3 * S], tab_ref[k, :, 3 * S:4 * S]
            v_r = st_ref[pl.ds(r0, 8), re]
            v_i = st_ref[pl.ds(r0, 8), im]
            v_r, v_i = _cmul_add(v_r, v_i, la_r, la_i, pltpu.roll(v_r, nbatch, 0), pltpu.roll(v_i, nbatch, 0))
            rc_r, rc_i = pltpu.roll(c_r, nbatch, 0), pltpu.roll(c_i, nbatch, 0)
            cb_r, cb_i = jnp.where(low, rc_r, c_r), jnp.where(low, rc_i, c_i)
            v_r, v_i = _cmul_add(v_r, v_i, lb_r, lb_i, cb_r, cb_i)
            st_ref[pl.ds(r0, 8), re] = v_r
            st_ref[pl.ds(r0, 8), im] = v_i
            return v_r, v_i

        def step(i, carry):
            r0 = pl.multiple_of(i * 8, 8)
            out = []
            for k in range(cps):
                out += list(tile(k, r0, carry[2 * k], carry[2 * k + 1]))
            return tuple(out)

        halves = tuple(carry_ref[:, S * j:S * (j + 1)] for j in range(2 * cps))
        halves = lax.fori_loop(0, rc // 8, step, halves)
        for j in range(2 * cps):
            carry_ref[:, S * j:S * (j + 1)] = halves[j]
        xb = st_ref[...].astype(BF16)
        xs_ref[...] = xb
        for k in range(cps):
            cols = slice(SSM_LANES * k, SSM_LANES * (k + 1))
            y_ref[:, cols] = (jnp.dot(xb[:, 2 * S * k:2 * S * (k + 1)], c_ref[k], preferred_element_type=F32)
                              + d_ref[:, cols] * uv[:, cols])

    return pl.pallas_call(
        body, name=name, grid=(ncol // cps, nch),
        in_specs=[pl.BlockSpec((rc, cps * SSM_LANES), lambda g, c: (c, g)),
                  pl.BlockSpec((cps, SSM_LANES, 2 * S), lambda g, c: (g, 0, 0)),
                  pl.BlockSpec((cps, 2 * S, SSM_LANES), lambda g, c: (g, 0, 0)),
                  pl.BlockSpec((1, cps * SSM_LANES), lambda g, c: (0, g)),
                  pl.BlockSpec((cps, 8, 4 * S), lambda g, c: (g, 0, 0))],
        out_specs=[pl.BlockSpec((rc, cps * SSM_LANES), lambda g, c: (c, g)),
                   pl.BlockSpec((rc, cps * 2 * S), lambda g, c: (c, g))],
        out_shape=[jax.ShapeDtypeStruct((T, W), F32), jax.ShapeDtypeStruct((T, ncol * 2 * S), BF16)],
        scratch_shapes=[pltpu.VMEM((rc, cps * 2 * S), F32), pltpu.VMEM((8, cps * 2 * S), F32)],
        compiler_params=_params(("parallel", "arbitrary")),
    )(u, bmat, cmat, dskip, tables)


def _ssm_bwd(dy, u, xs, bmat, cmat, dskip, tables, nbatch, rc, name):
    T, W = u.shape
    ncol = W // SSM_LANES
    nch = T // rc
    S = STATE_LANES
    ntile = rc // 16
    cps = _cols_per_step(ncol)

    def body(dy_ref, u_ref, xs_ref, b_ref, c_ref, d_ref, tab_ref,
             du_ref, db_ref, dc_ref, dl_ref, dd_ref, st_ref, carry_ref, accl_ref, accd_ref):
        ch = pl.program_id(1)

        @pl.when(ch == 0)
        def _():
            carry_ref[...] = jnp.zeros_like(carry_ref)
            accl_ref[...] = jnp.zeros_like(accl_ref)
            accd_ref[...] = jnp.zeros_like(accd_ref)
            db_ref[...] = jnp.zeros_like(db_ref)
            dc_ref[...] = jnp.zeros_like(dc_ref)

        dyv = dy_ref[...]
        uv = u_ref[...]
        dyb = dyv.astype(BF16)
        for k in range(cps):
            st_ref[:, 2 * S * k:2 * S * (k + 1)] = lax.dot_general(dyb[:, SSM_LANES * k:SSM_LANES * (k + 1)], c_ref[k], _NT,
                                                                   preferred_element_type=F32)
        low = lax.broadcasted_iota(jnp.int32, (8, S), 0) < nbatch

        def tile(k, r0, x_r, x_i, c_r, c_i, al_r, al_i):
            re, im = slice(2 * S * k, 2 * S * k + S), slice(2 * S * k + S, 2 * S * (k + 1))
            la_r, la_i = tab_ref[k, :, 0:S], tab_ref[k, :, S:2 * S]
            lb_r, lb_i = tab_ref[k, :, 2 * S:3 * S], tab_ref[k, :, 3 * S:4 * S]
            v_r = st_ref[pl.ds(r0, 8), re]
            v_i = st_ref[pl.ds(r0, 8), im]
            v_r, v_i = _cmul_add(v_r, v_i, la_r, la_i, pltpu.roll(v_r, nbatch, 0), pltpu.roll(v_i, nbatch, 0))
            cb_r = jnp.where(low, c_r, pltpu.roll(c_r, nbatch, 0))
            cb_i = jnp.where(low, c_i, pltpu.roll(c_i, nbatch, 0))
            v_r, v_i = _cmul_add(v_r, v_i, lb_r, lb_i, cb_r, cb_i)
            st_ref[pl.ds(r0, 8), re] = v_r
            st_ref[pl.ds(r0, 8), im] = v_i
            n_r = jnp.where(low, pltpu.roll(v_r, nbatch, 0), cb_r)
            n_i = jnp.where(low, pltpu.roll(v_i, nbatch, 0), cb_i)
            al_r = al_r + (n_r * x_r + n_i * x_i)
            al_i = al_i + (n_i * x_r - n_r * x_i)
            return v_r, v_i, al_r, al_i

        def step(j, carry):
            r0 = pl.multiple_of((ntile - 1 - j) * 16, 16)
            out = []
            for k in range(cps):
                re, im = slice(2 * S * k, 2 * S * k + S), slice(2 * S * k + S, 2 * S * (k + 1))
                x_r = xs_ref[pl.ds(r0, 16), re].astype(F32)
                x_i = xs_ref[pl.ds(r0, 16), im].astype(F32)
                mid = tile(k, r0 + 8, x_r[8:16], x_i[8:16], *carry[4 * k:4 * k + 4])
                out += list(tile(k, r0, x_r[0:8], x_i[0:8], *mid))
            return tuple(out)

        init = []
        for k in range(cps):
            init += [carry_ref[:, 2 * S * k:2 * S * k + S], carry_ref[:, 2 * S * k + S:2 * S * (k + 1)],
                     accl_ref[:, 2 * S * k:2 * S * k + S], accl_ref[:, 2 * S * k + S:2 * S * (k + 1)]]
        fin = lax.fori_loop(0, ntile, step, tuple(init))
        for k in range(cps):
            carry_ref[:, 2 * S * k:2 * S * k + S] = fin[4 * k]
            carry_ref[:, 2 * S * k + S:2 * S * (k + 1)] = fin[4 * k + 1]
            accl_ref[:, 2 * S * k:2 * S * k + S] = fin[4 * k + 2]
            accl_ref[:, 2 * S * k + S:2 * S * (k + 1)] = fin[4 * k + 3]
        dsb = st_ref[...].astype(BF16)
        ub = uv.astype(BF16)
        for k in range(cps):
            cols, lanes = slice(SSM_LANES * k, SSM_LANES * (k + 1)), slice(2 * S * k, 2 * S * (k + 1))
            du_ref[:, cols] = (lax.dot_general(dsb[:, lanes], b_ref[k], _NT, preferred_element_type=F32)
                               + d_ref[:, cols] * dyv[:, cols])
            db_ref[k] += lax.dot_general(ub[:, cols], dsb[:, lanes], _TN, preferred_element_type=F32)
            dc_ref[k] += lax.dot_general(xs_ref[:, lanes], dyb[:, cols], _TN, preferred_element_type=F32)
        accd_ref[...] += _fold8(dyv * uv)

        @pl.when(ch == nch - 1)
        def _():
            for k in range(cps):
                dl_ref[k] = jnp.sum(accl_ref[:, 2 * S * k:2 * S * (k + 1)], axis=0, keepdims=True)
            dd_ref[...] = jnp.sum(accd_ref[...], axis=0, keepdims=True)

    rev = lambda g, c: (nch - 1 - c, g)
    return pl.pallas_call(
        body, name=name, grid=(ncol // cps, nch),
        in_specs=[pl.BlockSpec((rc, cps * SSM_LANES), rev), pl.BlockSpec((rc, cps * SSM_LANES), rev),
                  pl.BlockSpec((rc, cps * 2 * S), rev),
                  pl.BlockSpec((cps, SSM_LANES, 2 * S), lambda g, c: (g, 0, 0)),
                  pl.BlockSpec((cps, 2 * S, SSM_LANES), lambda g, c: (g, 0, 0)),
                  pl.BlockSpec((1, cps * SSM_LANES), lambda g, c: (0, g)),
                  pl.BlockSpec((cps, 8, 4 * S), lambda g, c: (g, 0, 0))],
        out_specs=[pl.BlockSpec((rc, cps * SSM_LANES), rev),
                   pl.BlockSpec((cps, SSM_LANES, 2 * S), lambda g, c: (g, 0, 0)),
                   pl.BlockSpec((cps, 2 * S, SSM_LANES), lambda g, c: (g, 0, 0)),
                   pl.BlockSpec((cps, 1, 2 * S), lambda g, c: (g, 0, 0)),
                   pl.BlockSpec((1, cps * SSM_LANES), lambda g, c: (0, g))],
        out_shape=[jax.ShapeDtypeStruct((T, W), F32),
                   jax.ShapeDtypeStruct((ncol, SSM_LANES, 2 * S), F32),
                   jax.ShapeDtypeStruct((ncol, 2 * S, SSM_LANES), F32),
                   jax.ShapeDtypeStruct((ncol, 1, 2 * S), F32),
                   jax.ShapeDtypeStruct((1, W), F32)],
        scratch_shapes=[pltpu.VMEM((rc, cps * 2 * S), F32), pltpu.VMEM((8, cps * 2 * S), F32),
                        pltpu.VMEM((8, cps * 2 * S), F32), pltpu.VMEM((8, cps * SSM_LANES), F32)],
        compiler_params=_params(("parallel", "arbitrary")),
    )(dy, u, xs, bmat, cmat, dskip, tables)


def _ssm_matrices(a_re, a_im, log_step, b_re, b_im, c_re, c_im):
    G, N = a_re.shape
    ncol = G // GROUPS_PER_COL
    step = jnp.exp(log_step)[:, None]
    mag = jnp.exp(a_re * step)
    ang = a_im * step
    lam_re, lam_im = mag * jnp.cos(ang), mag * jnp.sin(ang)
    den = a_re * a_re + a_im * a_im
    nr, ni = lam_re - 1.0, lam_im
    coef_re = (nr * a_re + ni * a_im) / den
    coef_im = (ni * a_re - nr * a_im) / den
    bb_re = coef_re[..., None] * b_re - coef_im[..., None] * b_im
    bb_im = coef_re[..., None] * b_im + coef_im[..., None] * b_re
    eye = jnp.eye(GROUPS_PER_COL, dtype=F32)
    bb = jnp.stack([bb_re, bb_im]).reshape(2, ncol, GROUPS_PER_COL, N, SSM_GROUP)
    bmat = jnp.einsum("pbgnc,gh->bgcphn", bb, eye).reshape(ncol, SSM_LANES, 2 * STATE_LANES)
    cc = jnp.stack([c_re, -c_im]).reshape(2, ncol, GROUPS_PER_COL, SSM_GROUP, N)
    cmat = jnp.einsum("pbgcn,gh->bpgnhc", cc, eye).reshape(ncol, 2 * STATE_LANES, SSM_LANES)
    lam = jnp.concatenate([lam_re.reshape(ncol, STATE_LANES), lam_im.reshape(ncol, STATE_LANES)], axis=-1)
    return lam, bmat, cmat


def _scan_tables(lam, nbatch, conj):
    S = STATE_LANES
    lr, li = lam[:, None, 0:S], lam[:, None, S:2 * S]
    if conj:
        li = -li
    l2r, l2i = lr * lr - li * li, 2.0 * lr * li
    first = (jnp.arange(8) < nbatch)[None, :, None]
    zero = jnp.zeros_like(lr)
    if conj:
        parts = [jnp.where(first, lr, zero), jnp.where(first, li, zero), jnp.where(first, l2r, lr), jnp.where(first, l2i, li)]
    else:
        parts = [jnp.where(first, zero, lr), jnp.where(first, zero, li), jnp.where(first, lr, l2r), jnp.where(first, li, l2i)]
    return jnp.concatenate([jnp.broadcast_to(p, (lam.shape[0], 8, S)) for p in parts], axis=-1)


def _adamw_update(w_ref, g_ref, m_ref, v_ref, d_ref, nm_ref, nv_ref):
    gv = g_ref[...]
    mn = ADAM_B1 * m_ref[...] + (1.0 - ADAM_B1) * gv
    vn = ADAM_B2 * v_ref[...] + (1.0 - ADAM_B2) * (gv * gv)
    m_hat = mn / (1.0 - ADAM_B1 ** ADAM_STEP)
    v_hat = vn / (1.0 - ADAM_B2 ** ADAM_STEP)
    d_ref[...] = -ADAM_LR * (m_hat / (jnp.sqrt(v_hat) + ADAM_EPS) + ADAM_WD * w_ref[...])
    nm_ref[...] = mn
    nv_ref[...] = vn


def _adamw_small(ws, gs, ms, vs, name):
    n = len(ws)

    def body(*refs):
        for i in range(n):
            _adamw_update(refs[i], refs[n + i], refs[2 * n + i], refs[3 * n + i],
                          refs[4 * n + i], refs[5 * n + i], refs[6 * n + i])

    vm = pl.BlockSpec(memory_space=pltpu.VMEM)
    shapes = [jax.ShapeDtypeStruct(a.shape, F32) for a in ws]
    outs = pl.pallas_call(body, name=name, in_specs=[vm] * (4 * n), out_specs=[vm] * (3 * n), out_shape=shapes * 3,
                          compiler_params=pltpu.CompilerParams(vmem_limit_bytes=VMEM_LIMIT))(*ws, *gs, *ms, *vs)
    return outs[:n], outs[n:2 * n], outs[2 * n:]


def _adamw(w, g, m, v, name):
    R, C = w.shape[-2], w.shape[-1]
    tr = R if R <= 512 else _pick_tile(R, 512, 8)
    body = functools.partial(_adamw_update)

    def spec_for(a):
        if len(a.shape) == 2:
            return pl.BlockSpec((tr, C), lambda i: (i, 0))
        return pl.BlockSpec((None, tr, C), lambda i: (0, i, 0))

    spec = spec_for(w)
    shp = jax.ShapeDtypeStruct(w.shape, F32)
    return pl.pallas_call(body, name=name, grid=(R // tr,), in_specs=[spec, spec_for(g), spec, spec], out_specs=[spec] * 3,
                          out_shape=[shp, shp, shp], compiler_params=_params(("parallel",)))(w, g, m, v)


_ANY = pl.BlockSpec(memory_space=pl.ANY)


def _place():
    x, y, c = lax.axis_index("x"), lax.axis_index("y"), lax.axis_index("c")
    chips = [(1 - x, y), (x, 1 - y), (1 - x, 1 - y)]
    return x, y, c, chips


def _remote(src, dst, send_sems, recv_sems, k, to):
    return pltpu.make_async_remote_copy(src_ref=src, dst_ref=dst, send_sem=send_sems.at[k], recv_sem=recv_sems.at[k],
                                        device_id=to, device_id_type=MESH_IDS)


class _Riders:
    def __init__(self, srcs, out_shapes, n_sems, copies):
        self.srcs, self.out_shapes, self.n_sems, self.copies = list(srcs), list(out_shapes), n_sems, copies


def _call(body, name, grid, in_specs, out_specs, out_shape, scratch_shapes, sem, args, riders=None):
    if riders is None:
        return pl.pallas_call(body, name=name, grid=grid, in_specs=in_specs, out_specs=out_specs, out_shape=out_shape,
                              scratch_shapes=scratch_shapes, compiler_params=_params(sem))(*args)
    n_in, n_out, n_scr = len(in_specs), len(out_specs), len(scratch_shapes)
    r_in, r_out = len(riders.srcs), len(riders.out_shapes)

    def carrying(*refs):
        a, b = n_in, n_in + r_in
        c, d = b + n_out, b + n_out + r_out
        e = d + n_scr
        sends, arrivals = riders.copies(refs[a:b], refs[c:d], refs[e], refs[e + 1])
        first, last = None, None
        for ax, size in enumerate(grid):
            at0, at1 = pl.program_id(ax) == 0, pl.program_id(ax) == size - 1
            first = at0 if first is None else first & at0
            last = at1 if last is None else last & at1

        @pl.when(first)
        def _():
            for cp in sends:
                cp.start()

        body(*refs[:a], *refs[b:c], *refs[d:e])

        @pl.when(last)
        def _():
            for cp in arrivals:
                cp.wait_recv()
            for cp in sends:
                cp.wait_send()

    outs = pl.pallas_call(
        carrying, name=name, grid=grid, in_specs=list(in_specs) + [_ANY] * r_in,
        out_specs=list(out_specs) + [_ANY] * r_out, out_shape=list(out_shape) + riders.out_shapes,
        scratch_shapes=list(scratch_shapes) + [pltpu.SemaphoreType.DMA((riders.n_sems,)),
                                               pltpu.SemaphoreType.DMA((riders.n_sems,))],
        compiler_params=pltpu.CompilerParams(dimension_semantics=("arbitrary",) * len(grid),
                                             vmem_limit_bytes=VMEM_LIMIT, has_side_effects=True),
    )(*args, *riders.srcs)
    return outs[:n_out], outs[n_out:]


def _gather_riders(shards):
    def copies(srcs, outs, send_sems, recv_sems):
        x, y, c, chips = _place()
        sends, arrivals = [], []
        for i, s in enumerate(shards):
            half = s.shape[0] // 2
            rows = pl.ds(c * half, half)
            for j, chip in enumerate(chips):
                sends.append(_remote(srcs[i].at[rows, :], outs[i].at[2 * x + y, rows, :], send_sems, recv_sems,
                                     3 * i + j, (*chip, c)))
                landed = outs[i].at[2 * chip[0] + chip[1], rows, :]
                arrivals.append(_remote(landed, landed, send_sems, recv_sems, 3 * i + j, (*chip, c)))
        return sends, arrivals

    return _Riders(shards, [jax.ShapeDtypeStruct((N_CHIPS,) + s.shape, s.dtype) for s in shards], 3 * len(shards), copies)


def _exchange_riders(parts):
    def copies(srcs, outs, send_sems, recv_sems):
        x, y, c, chips = _place()
        sends = [_remote(srcs[i].at[2 * chip[0] + chip[1]], outs[i].at[j], send_sems, recv_sems, 3 * i + j, (*chip, c))
                 for i in range(len(parts)) for j, chip in enumerate(chips)]
        return sends, sends

    return _Riders(parts, [jax.ShapeDtypeStruct((3,) + p.shape[1:], p.dtype) for p in parts], 3 * len(parts), copies)


def _swap_riders(grads):
    def copies(srcs, outs, send_sems, recv_sems):
        x, y, c, _ = _place()
        sends = []
        for i, g in enumerate(grads):
            half = g.shape[1] // 2
            sends.append(_remote(srcs[i].at[:, pl.ds((1 - c) * half, half), :], outs[i], send_sems, recv_sems, i,
                                 (x, y, 1 - c)))
        return sends, sends

    return _Riders(grads, [jax.ShapeDtypeStruct((N_CHIPS, g.shape[1] // 2, g.shape[2]), g.dtype) for g in grads],
                   len(grads), copies)


def _forward_halves(gathered, shards, tag):
    n = len(gathered)

    def body(*refs):
        srcs, outs = refs[:n], refs[n:2 * n]
        send_sems, recv_sems = refs[2 * n:]
        x, y, c, chips = _place()
        sibling = (x, y, 1 - c)
        cps = []
        for i in range(n):
            half = gathered[i].shape[1] // 2
            for j, chip in enumerate(chips):
                slot = 2 * chip[0] + chip[1]
                cps.append(_remote(srcs[i].at[slot, pl.ds(c * half, half), :], outs[i].at[slot, pl.ds(c * half, half), :],
                                   send_sems, recv_sems, 3 * i + j, sibling))
        for cp in cps:
            cp.start()
        for i in range(n):
            half = gathered[i].shape[1] // 2
            for j, chip in enumerate(chips):
                theirs = outs[i].at[2 * chip[0] + chip[1], pl.ds((1 - c) * half, half), :]
                _remote(theirs, theirs, send_sems, recv_sems, 3 * i + j, sibling).wait_recv()
        for cp in cps:
            cp.wait_send()

    outs = pl.pallas_call(
        body, name=f"gather_forward_{tag}", in_specs=[_ANY] * n, out_specs=[_ANY] * n,
        out_shape=[jax.ShapeDtypeStruct(g.shape, g.dtype) for g in gathered],
        input_output_aliases={i: i for i in range(n)},
        scratch_shapes=[pltpu.SemaphoreType.DMA((3 * n,)), pltpu.SemaphoreType.DMA((3 * n,))],
        compiler_params=pltpu.CompilerParams(has_side_effects=True),
    )(*gathered)
    slot = 2 * lax.axis_index("x") + lax.axis_index("y")
    return [lax.dynamic_update_slice(o, s[None], (slot, 0, 0)) for o, s in zip(outs, shards)]


def _gather_weights(shards):
    n = len(shards)

    def body(*refs):
        srcs, outs = refs[:n], refs[n:2 * n]
        send_sems, recv_sems = refs[2 * n:]
        x, y, c, chips = _place()
        sibling = (x, y, 1 - c)

        def piece(i, px, py, pc):
            half = shards[i].shape[0] // 2
            return outs[i].at[2 * px + py, pl.ds(pc * half, half), :]

        first = []
        for i in range(n):
            half = shards[i].shape[0] // 2
            for j, chip in enumerate(chips):
                first.append(_remote(srcs[i].at[pl.ds(c * half, half), :], piece(i, x, y, c), send_sems, recv_sems,
                                     6 * i + j, (*chip, c)))
        for cp in first:
            cp.start()
        passed = []
        for i in range(n):
            for j, chip in enumerate(chips):
                _remote(piece(i, *chip, c), piece(i, *chip, c), send_sems, recv_sems, 6 * i + j, (*chip, c)).wait_recv()
                cp = _remote(piece(i, *chip, c), piece(i, *chip, c), send_sems, recv_sems, 6 * i + 3 + j, sibling)
                cp.start()
                passed.append(cp)
        for i in range(n):
            for j, chip in enumerate(chips):
                _remote(piece(i, *chip, 1 - c), piece(i, *chip, 1 - c), send_sems, recv_sems, 6 * i + 3 + j,
                        sibling).wait_recv()
        for cp in first + passed:
            cp.wait_send()

    outs = pl.pallas_call(
        body, name="gather_weights", in_specs=[_ANY] * n, out_specs=[_ANY] * n,
        out_shape=[jax.ShapeDtypeStruct((N_CHIPS,) + s.shape, s.dtype) for s in shards],
        scratch_shapes=[pltpu.SemaphoreType.DMA((6 * n,)), pltpu.SemaphoreType.DMA((6 * n,))],
        compiler_params=pltpu.CompilerParams(has_side_effects=True),
    )(*shards)
    slot = 2 * lax.axis_index("x") + lax.axis_index("y")
    return [lax.dynamic_update_slice(o, s[None], (slot, 0, 0)) for o, s in zip(outs, shards)]


def _swap_halves(grads, tag):
    n = len(grads)

    def body(*refs):
        srcs, outs = refs[:n], refs[n:2 * n]
        send_sems, recv_sems = refs[2 * n:]
        x, y, c, _ = _place()
        cps = []
        for i in range(n):
            half = grads[i].shape[1] // 2
            cps.append(_remote(srcs[i].at[:, pl.ds((1 - c) * half, half), :], outs[i], send_sems, recv_sems, i, (x, y, 1 - c)))
        for cp in cps:
            cp.start()
        for cp in cps:
            cp.wait()

    return pl.pallas_call(
        body, name=f"grad_swap_halves_{tag}", in_specs=[_ANY] * n, out_specs=[_ANY] * n,
        out_shape=[jax.ShapeDtypeStruct((N_CHIPS, g.shape[1] // 2, g.shape[2]), g.dtype) for g in grads],
        scratch_shapes=[pltpu.SemaphoreType.DMA((n,)), pltpu.SemaphoreType.DMA((n,))],
        compiler_params=pltpu.CompilerParams(has_side_effects=True),
    )(*grads)


def _join_halves(fulls):
    n = len(fulls)

    def body(*refs):
        srcs, outs = refs[:n], refs[n:2 * n]
        send_sems, recv_sems = refs[2 * n:]
        x, y, c, _ = _place()
        sibling = (x, y, 1 - c)
        cps = []
        for i in range(n):
            h = fulls[i].shape[0] // 2
            cps.append(_remote(srcs[i].at[pl.ds(c * h, h), :], outs[i].at[pl.ds(c * h, h), :], send_sems, recv_sems, i,
                               sibling))
        for cp in cps:
            cp.start()
        for i in range(n):
            h = fulls[i].shape[0] // 2
            theirs = outs[i].at[pl.ds((1 - c) * h, h), :]
            _remote(theirs, theirs, send_sems, recv_sems, i, sibling).wait_recv()
        for cp in cps:
            cp.wait_send()

    return pl.pallas_call(
        body, name="grad_join_halves", in_specs=[_ANY] * n, out_specs=[_ANY] * n,
        out_shape=[jax.ShapeDtypeStruct(f.shape, f.dtype) for f in fulls],
        input_output_aliases={i: i for i in range(n)},
        scratch_shapes=[pltpu.SemaphoreType.DMA((n,)), pltpu.SemaphoreType.DMA((n,))],
        compiler_params=pltpu.CompilerParams(has_side_effects=True),
    )(*fulls)


def _half_tile(h):
    return h if h <= 512 else _pick_tile(h, 512, ROW_ALIGN)


def _sum_halves(g, r1, c_idx, name):
    _, R, C = g.shape
    H = R // 2
    tr = _half_tile(H)
    nblk = H // tr

    def body(c_ref, g_ref, r_ref, p_ref):
        p_ref[...] = (g_ref[...] + r_ref[...]).astype(BF16)

    half = pl.BlockSpec((None, tr, C), lambda s, i, c_ref: (s, c_ref[0] * nblk + i, 0))
    plain = pl.BlockSpec((None, tr, C), lambda s, i, c_ref: (s, i, 0))
    return pl.pallas_call(
        body, name=name,
        grid_spec=pltpu.PrefetchScalarGridSpec(num_scalar_prefetch=1, grid=(N_CHIPS, nblk), in_specs=[half, plain],
                                               out_specs=plain),
        out_shape=jax.ShapeDtypeStruct((N_CHIPS, H, C), BF16),
        compiler_params=_params(("parallel", "parallel")),
    )(c_idx, g, r1)


def _sum_chips(g, r1, r2, idx, name):
    _, R, C = g.shape
    H = R // 2
    tr = _half_tile(H)
    nblk = H // tr

    def body(idx_ref, g_ref, r1_ref, r2_ref, o_ref):
        o_ref[...] = (((g_ref[...] + r1_ref[...]) + r2_ref[0].astype(F32)) + r2_ref[1].astype(F32)) + r2_ref[2].astype(F32)

    return pl.pallas_call(
        body, name=name,
        grid_spec=pltpu.PrefetchScalarGridSpec(
            num_scalar_prefetch=1, grid=(nblk,),
            in_specs=[pl.BlockSpec((None, tr, C), lambda i, idx_ref: (idx_ref[0], idx_ref[1] * nblk + i, 0)),
                      pl.BlockSpec((None, tr, C), lambda i, idx_ref: (idx_ref[0], i, 0)),
                      pl.BlockSpec((3, tr, C), lambda i, idx_ref: (0, i, 0))],
            out_specs=pl.BlockSpec((tr, C), lambda i, idx_ref: (idx_ref[1] * nblk + i, 0))),
        out_shape=jax.ShapeDtypeStruct((R, C), F32),
        compiler_params=_params(("parallel",)),
    )(idx, g, r1, r2)


def _all_reduce_small(v, n_fold, fold_rows, fold_at):
    M, N = v.shape

    def body(x_ref, tot_ref, fold_ref, all_ref, send_sems, recv_sems, local_sem):
        x, y, c, chips = _place()
        me, sibling = (x, y, c), (x, y, 1 - c)

        def rows(px, py, pc):
            return all_ref.at[pl.ds((4 * px + 2 * py + pc) * M, M), :]

        def copy(k, block, to, src=None):
            return _remote(rows(*block) if src is None else src, rows(*block), send_sems, recv_sems, k, to)

        mine = pltpu.make_async_copy(x_ref, rows(*me), local_sem)
        mine.start()
        first = [copy(0, me, sibling, src=x_ref)]
        first += [copy(1 + j, me, (*chip, c), src=x_ref) for j, chip in enumerate(chips)]
        for cp in first:
            cp.start()
        passed = [copy(4 + j, (*chip, c), sibling) for j, chip in enumerate(chips)]
        for j, chip in enumerate(chips):
            copy(1 + j, (*chip, c), me).wait_recv()
            passed[j].start()
        copy(0, sibling, me).wait_recv()
        for j, chip in enumerate(chips):
            copy(4 + j, (*chip, 1 - c), me).wait_recv()
        for cp in first + passed:
            cp.wait_send()
        mine.wait()
        tot = all_ref[0:M, :]
        for d in range(1, 8):
            tot = tot + all_ref[d * M:(d + 1) * M, :]
        tot_ref[...] = tot
        f = tot[fold_at:fold_at + fold_rows, :]
        for e in range(1, n_fold):
            f = f + tot[fold_at + e * fold_rows:fold_at + (e + 1) * fold_rows, :]
        fold_ref[...] = f

    vm = pl.BlockSpec(memory_space=pltpu.VMEM)
    return pl.pallas_call(
        body, name="all_reduce_small", in_specs=[vm], out_specs=[vm, vm],
        out_shape=[jax.ShapeDtypeStruct((M, N), F32), jax.ShapeDtypeStruct((fold_rows, N), F32)],
        scratch_shapes=[pltpu.VMEM((8 * M, N), F32), pltpu.SemaphoreType.DMA((7,)), pltpu.SemaphoreType.DMA((7,)),
                        pltpu.SemaphoreType.DMA],
        compiler_params=pltpu.CompilerParams(has_side_effects=True, vmem_limit_bytes=VMEM_LIMIT),
    )(v)


def _as_rows(a, width):
    flat = a.reshape(-1)
    pad = (-flat.shape[0]) % width
    if pad:
        flat = jnp.concatenate([flat, jnp.zeros((pad,), flat.dtype)])
    return flat.reshape(-1, width)


class _Layout:
    def __init__(self, width, total_mult):
        self.width, self.total_mult = width, total_mult
        self.offsets, self.shapes, self.rows = {}, {}, 0

    def add(self, name, shape):
        r = -(-math.prod(shape) // self.width)
        self.offsets[name], self.shapes[name] = (self.rows, r), tuple(shape)
        self.rows += r

    def align(self, mult):
        gap = (-self.rows) % mult
        if gap:
            self.offsets[f"_gap{self.rows}"], self.shapes[f"_gap{self.rows}"] = (self.rows, gap), (gap, self.width)
            self.rows += gap
        return self.rows

    def pack(self, pieces):
        self.align(self.total_mult)
        parts = [_as_rows(pieces[n].astype(F32), self.width) if n in pieces else jnp.zeros(self.shapes[n], F32)
                 for n in self.offsets]
        return jnp.concatenate(parts, axis=0)

    def unpack(self, buf, name):
        off, r = self.offsets[name]
        shape = self.shapes[name]
        return buf[off:off + r].reshape(-1)[:math.prod(shape)].reshape(shape)


_BIG = ["ffn1_w1", "ffn1_w3", "ffn1_w2", "w_in", "ssm_glu_a", "ssm_glu_b", "w_out", "ffn2_w1", "ffn2_w3", "ffn2_w2"]
_TRANSPOSED = {"ffn1_w1", "ffn1_w3", "ffn2_w1", "ffn2_w3"}
_SMALL = ["ffn1_norm", "mix_norm", "ffn2_norm", "final_norm", "attn_sinks", "ssm_a_re", "ssm_a_im", "ssm_log_step",
          "ssm_b_re", "ssm_b_im", "ssm_c_re", "ssm_c_im", "ssm_d"]
_WEIGHTS = ["meta_tokens", "ffn1_norm", "ffn1_w1", "ffn1_w3", "ffn1_w2", "mix_norm", "w_in", "attn_sinks", "ssm_a_re",
            "ssm_a_im", "ssm_log_step", "ssm_b_re", "ssm_b_im", "ssm_c_re", "ssm_c_im", "ssm_d", "ssm_glu_a",
            "ssm_glu_b", "w_out", "ffn2_norm", "ffn2_w1", "ffn2_w3", "ffn2_w2", "final_norm"]


def _kv_interleave(w, kv_heads):
    kvw = kv_heads * HEAD_DIM
    lead = w.shape[:-1]
    k = w[..., 0:kvw].reshape(lead + (kv_heads, 1, HEAD_DIM))
    v = w[..., kvw:2 * kvw].reshape(lead + (kv_heads, 1, HEAD_DIM))
    return jnp.concatenate([jnp.concatenate([k, v], axis=-2).reshape(lead + (2 * kvw,)), w[..., 2 * kvw:]], axis=-1)


def _kv_deinterleave(w, kv_heads):
    kvw = kv_heads * HEAD_DIM
    lead = w.shape[:-1]
    kv = w[..., 0:2 * kvw].reshape(lead + (kv_heads, 2, HEAD_DIM))
    return jnp.concatenate([kv[..., 0, :].reshape(lead + (kvw,)), kv[..., 1, :].reshape(lead + (kvw,)), w[..., 2 * kvw:]],
                           axis=-1)


def _step(x, target, w, m, v):
    B, S, D = x.shape
    L = S + N_META
    T = B * L
    H = D // HEAD_DIM
    KV = H // Q_PER_KV
    SW = D // 2
    tm = _pick_tile(L, ROW_TILE_CAP, ROW_ALIGN)
    rc = _pick_tile(L, ROW_TILE_CAP // B, 4) * B
    tw = _pick_tile(T, 3 * ROW_TILE_CAP, ROW_ALIGN)
    my_c = lax.axis_index("c")
    my_slot = 2 * lax.axis_index("x") + lax.axis_index("y")

    groups = {"ffn1": ["ffn1_w1", "ffn1_w3", "ffn1_w2"], "mix": ["w_in", "ssm_glu_a", "ssm_glu_b", "w_out"],
              "ffn2": ["ffn2_w1", "ffn2_w3", "ffn2_w2"]}
    waves = {"first": ["ffn1_w1", "ffn1_w3"], "early": ["ffn1_w2"] + groups["mix"], "late": groups["ffn2"]}
    def own_layout(a, n):
        return jnp.swapaxes(a[0], 0, 1) if n in _TRANSPOSED else a[0]

    shards = {n: own_layout(w[n], n).astype(BF16) for n in _BIG}
    gathered = _gather_weights([shards[n] for n in waves["first"]] + [w["meta_tokens"]])
    ws = dict(zip(waves["first"], gathered[:-1]))
    meta = jnp.transpose(gathered[-1], (1, 0, 2)).reshape(N_META, D)

    def arrive(wave, landed):
        mine = [shards[n] for n in waves[wave]]
        ws.update(zip(waves[wave], _forward_halves(landed, mine, wave)))

    g_ffn1, g_mix, g_ffn2 = w["ffn1_norm"], w["mix_norm"], w["ffn2_norm"]
    g_final = w["final_norm"].reshape(1, D)

    h0, n_ffn1 = _embed_norm(x, meta, g_ffn1, tm, "ffn1_norm")

    def ffn_fwd(h, g, tag, carry=None, n=None):
        if n is None:
            n = _rmsnorm_fwd(h, g, tm, f"{tag}_norm")
        riders = None if carry is None else _gather_riders([shards[k] for k in waves[carry]])
        out = _ffn_up(n, ws[f"{tag}_w1"], ws[f"{tag}_w3"], tm, f"{tag}_up", riders)
        if carry is not None:
            out, landed = out
            arrive(carry, landed)
        a, c, s = out
        return _ffn_down(s, ws[f"{tag}_w2"], h, tm, f"{tag}_down"), (n, a, c, s)

    h1, saved1 = ffn_fwd(h0, g_ffn1, "ffn1", carry="early", n=n_ffn1)
    w_kvu = _kv_interleave(ws["w_in"][1], KV)
    hn = _rmsnorm_fwd(h1, g_mix, tm, "mix_norm")
    q = _mm_colslots(hn, ws["w_in"], BF16, "w_in_q", tm, first=0, count=1, scale=HEAD_DIM ** -0.5)
    kvu = _mm_plain(hn, w_kvu, "nn", F32, "w_in_kvu", tm)
    gates = _mm_colslots(hn, ws["w_in"], F32, "w_in_gates", tm, first=2, count=2)

    sinks = w["attn_sinks"].reshape(KV, Q_PER_KV, 1, 1)
    sink_row = jnp.broadcast_to(sinks.reshape(KV, 1, Q_PER_KV, 1), (KV, 1, Q_PER_KV, BLOCK)).reshape(KV, 1, Q_PER_KV * BLOCK)
    sink_meta = jnp.broadcast_to(sinks, (KV, Q_PER_KV, N_META, 1)).reshape(KV, Q_PER_KV * N_META, 1)
    (attn,), landed = _attn_fwd(q, kvu, sink_row, sink_meta, B, "attn_fwd",
                                _gather_riders([shards[k] for k in waves["late"]]))
    arrive("late", landed)

    def to_time_major(a2d):
        return jnp.transpose(a2d.reshape(B, L, a2d.shape[-1]), (1, 0, 2)).reshape(T, a2d.shape[-1])

    def to_batch_major(a2d):
        return jnp.transpose(a2d.reshape(L, B, a2d.shape[-1]), (1, 0, 2)).reshape(T, a2d.shape[-1])

    ssm_args = (w["ssm_a_re"][0], w["ssm_a_im"][0], w["ssm_log_step"][0], w["ssm_b_re"][0], w["ssm_b_im"][0],
                w["ssm_c_re"][0], w["ssm_c_im"][0])
    (lam, bmat, cmat), ssm_vjp = jax.vjp(_ssm_matrices, *ssm_args)
    bmat16, cmat16 = bmat.astype(BF16), cmat.astype(BF16)
    u_t = to_time_major(kvu[:, SW:])
    y_t, xs = _ssm_fwd(u_t, bmat16, cmat16, w["ssm_d"], _scan_tables(lam, B, False), B, rc, "ssm_fwd")
    y0 = to_batch_major(y_t)
    yg = _gelu_fwd(y0, tm, "gelu_fwd")
    ga = _mm_colslots(yg, ws["ssm_glu_a"], F32, "glu_a", tm)
    gb = _mm_colslots(yg, ws["ssm_glu_b"], F32, "glu_b", tm)
    merged = _merge_fwd(gates, attn, ga, gb, tm, "merge_fwd")
    h2 = _mm_rowslots(merged, ws["w_out"], h1, tm, "w_out")
    h3, saved2 = ffn_fwd(h2, g_ffn2, "ffn2")
    dh3, dh3b, dg_final, loss_row = _loss_head(h3, g_final, target, tm, "loss_head")

    grads, swapped, received = {}, {}, {}
    c_idx = my_c.reshape(1).astype(jnp.int32)
    idx = jnp.stack([my_slot, my_c]).astype(jnp.int32)

    def swap_riders(group):
        return _swap_riders([grads[n] for n in groups[group]])

    def exchange_riders(group):
        names = groups[group]
        if names[0] not in swapped:
            swapped.update(zip(names, _swap_halves([grads[n] for n in names], group)))
        return _exchange_riders([_sum_halves(grads[n], swapped[n], c_idx, f"grad_sum_halves_{n}") for n in names])

    def ffn_bwd(h, g, saved, dh, dhb, tag, dhidden_carries=None, dn_carries=None, last=False):
        n, a, c, s = saved
        w1, w3, w2 = ws[f"{tag}_w1"], ws[f"{tag}_w3"], ws[f"{tag}_w2"]
        grads[f"{tag}_w2"] = _wgrad_hidden_rows(s, dhb, tw, f"{tag}_dw2", 0.5)
        if dhidden_carries is None:
            da, dc = _ffn_dhidden(dhb, w2, a, c, tm, f"{tag}_dhidden")
        else:
            (da, dc), got = _ffn_dhidden(dhb, w2, a, c, tm, f"{tag}_dhidden", exchange_riders(dhidden_carries[1]))
            received.update(zip(groups[dhidden_carries[1]], got))
        grads[f"{tag}_w1"] = _wgrad_hidden_rows(da, n, tw, f"{tag}_dw1", 1.0)
        grads[f"{tag}_w3"] = _wgrad_hidden_rows(dc, n, tw, f"{tag}_dw3", 1.0)
        kind, group = dn_carries
        riders = swap_riders(group) if kind == "swap" else exchange_riders(group)
        (dh_in, dhb_in, grads[f"{tag}_norm"]), got = _ffn_dn(da, w1, dc, w3, h, g, dh, tm, f"{tag}_dn", riders,
                                                               B if last else None)
        return dh_in, dhb_in, got

    dh2, dh2b, got = ffn_bwd(h2, g_ffn2, saved2, dh3, dh3b, "ffn2", dn_carries=("swap", "ffn2"))
    swapped.update(zip(groups["ffn2"], got))

    grads["w_out"] = _wgrad_rowslots(merged, dh2b, tw, "dw_out")
    dattn, dgat, dgss, dga, dgb = _merge_bwd(dh2b, ws["w_out"], gates, attn, ga, gb, tm, "merge_bwd")
    grads["ssm_glu_a"] = _wgrad_colslots(yg, dga, tw, "dglu_a")
    grads["ssm_glu_b"] = _wgrad_colslots(yg, dgb, tw, "dglu_b")
    dy0 = _gelu_bwd([(dga, ws["ssm_glu_a"]), (dgb, ws["ssm_glu_b"])], y0, tm, "gelu_bwd")
    du_t, dbmat, dcmat, dlam, dd = _ssm_bwd(to_time_major(dy0), u_t, xs, bmat16, cmat16, w["ssm_d"],
                                            _scan_tables(lam, B, True), B, rc, "ssm_bwd")
    d_ssm = ssm_vjp((dlam[:, 0, :], dbmat, dcmat))
    for n, gval in zip(["ssm_a_re", "ssm_a_im", "ssm_log_step", "ssm_b_re", "ssm_b_im", "ssm_c_re", "ssm_c_im"], d_ssm):
        grads[n] = gval[None]
    grads["ssm_d"] = dd

    (dq, dkv, dsink), got = _attn_bwd(q, kvu, attn, dattn, sink_row, sink_meta, B, "attn_bwd",
                                      exchange_riders("ffn2"))
    received.update(zip(groups["ffn2"], got))
    grads["attn_sinks"] = dsink[:, 0:Q_PER_KV, 0].reshape(1, H)
    dkvu = jnp.concatenate([dkv, to_batch_major(du_t).astype(BF16)], axis=1)
    pieces = [dq, dkvu, dgat, dgss]
    dw_in = [_wgrad_plain(hn, p, f"dw_in_{k}", tw) for k, p in enumerate(pieces)]
    dw_in[1] = _kv_deinterleave(dw_in[1], KV)
    grads["w_in"] = jnp.stack(dw_in)
    w_in_parts = [ws["w_in"][0], w_kvu, ws["w_in"][2], ws["w_in"][3]]
    whole = _once((D, D), lambda i: (0, 0))
    (dh1, dh1b, grads["mix_norm"]), swap_mix = _mm_norm_bwd(
        "dhn", "nt", [(p, _spec((tm, D), lambda i: (i, 0)), wp, whole) for p, wp in zip(pieces, w_in_parts)],
        h1, g_mix, dh2, tm, swap_riders("mix"))
    swapped.update(zip(groups["mix"], swap_mix))
    grad_x, dmeta_rows, got = ffn_bwd(h0, g_ffn1, saved1, dh1, dh1b, "ffn1", dhidden_carries=("exchange", "mix"),
                                      dn_carries=("exchange", "ffn1"), last=True)
    received.update(zip(groups["ffn1"], got))

    grads["final_norm"] = dg_final
    slay = _Layout(D, 8)
    for n in _SMALL:
        slay.add(n, w[n].shape)
    slay.add("loss", (1, D))
    meta_at = slay.align(8)
    slay.add("meta", (N_META, D))
    small = slay.pack({**{n: grads[n] for n in _SMALL}, "loss": loss_row, "meta": dmeta_rows})
    tot_small, dmeta = _all_reduce_small(small, 1, N_META, meta_at)
    loss = slay.unpack(tot_small, "loss")[0, 0]
    for n in _SMALL:
        grads[n] = slay.unpack(tot_small, n)
    cw = D // N_CHIPS
    grads["meta_tokens"] = lax.dynamic_slice_in_dim(dmeta, my_slot * cw, cw, axis=1)

    fulls = [_sum_chips(grads[n], swapped[n], received[n], idx, f"grad_sum_chips_{n}") for n in _BIG]
    for n, f in zip(_BIG, _join_halves(fulls)):
        grads[n] = f

    delta, new_m, new_v = {}, {}, {}
    for n in _BIG + ["meta_tokens"]:
        if n in _TRANSPOSED:
            flip = lambda a: jnp.swapaxes(a, -1, -2)
            outs = _adamw(flip(w[n]), grads[n], flip(m[n]), flip(v[n]), f"adamw_{n}")
            delta[n], new_m[n], new_v[n] = (flip(o) for o in outs)
            grads[n] = flip(grads[n])[None]
        else:
            delta[n], new_m[n], new_v[n] = _adamw(w[n], grads[n], m[n], v[n], f"adamw_{n}")
            grads[n] = grads[n].reshape(w[n].shape)

    def flat2d(a):
        return a.reshape(-1, a.shape[-1])

    d_, m_, v_ = _adamw_small([flat2d(w[n]) for n in _SMALL], [flat2d(grads[n]) for n in _SMALL],
                              [flat2d(m[n]) for n in _SMALL], [flat2d(v[n]) for n in _SMALL], "adamw_small")
    for i, n in enumerate(_SMALL):
        shp = w[n].shape
        delta[n], new_m[n], new_v[n] = d_[i].reshape(shp), m_[i].reshape(shp), v_[i].reshape(shp)
        grads[n] = grads[n].reshape(shp)

    return (loss, grad_x, *[grads[n] for n in _WEIGHTS], *[delta[n] for n in _WEIGHTS],
            *[new_m[n] for n in _WEIGHTS], *[new_v[n] for n in _WEIGHTS])


def kernel(x, meta_tokens, ffn1_norm, ffn1_w1, ffn1_w3, ffn1_w2, mix_norm, w_in, attn_sinks, ssm_a_re, ssm_a_im, ssm_log_step, ssm_b_re, ssm_b_im, ssm_c_re, ssm_c_im, ssm_d, ssm_glu_a, ssm_glu_b, w_out, ffn2_norm, ffn2_w1, ffn2_w3, ffn2_w2, final_norm, loss_target, m_meta_tokens, m_ffn1_norm, m_ffn1_w1, m_ffn1_w3, m_ffn1_w2, m_mix_norm, m_w_in, m_attn_sinks, m_ssm_a_re, m_ssm_a_im, m_ssm_log_step, m_ssm_b_re, m_ssm_b_im, m_ssm_c_re, m_ssm_c_im, m_ssm_d, m_ssm_glu_a, m_ssm_glu_b, m_w_out, m_ffn2_norm, m_ffn2_w1, m_ffn2_w3, m_ffn2_w2, m_final_norm, v_meta_tokens, v_ffn1_norm, v_ffn1_w1, v_ffn1_w3, v_ffn1_w2, v_mix_norm, v_w_in, v_attn_sinks, v_ssm_a_re, v_ssm_a_im, v_ssm_log_step, v_ssm_b_re, v_ssm_b_im, v_ssm_c_re, v_ssm_c_im, v_ssm_d, v_ssm_glu_a, v_ssm_glu_b, v_w_out, v_ffn2_norm, v_ffn2_w1, v_ffn2_w3, v_ffn2_w2, v_final_norm):
    args = locals()
    w = {n: args[n] for n in _WEIGHTS}
    m = {n: args["m_" + n] for n in _WEIGHTS}
    v = {n: args["v_" + n] for n in _WEIGHTS}
    return _step(x, loss_target, w, m, v)
```

```python
import functools
import math

import jax
import jax.numpy as jnp
from jax import lax
from jax.experimental import pallas as pl
from jax.experimental.pallas import tpu as pltpu

F32 = jnp.float32
BF16 = jnp.bfloat16
MESH_IDS = pl.DeviceIdType.MESH

N_CHIPS = 4
N_META = 16
HEAD_DIM = 64
Q_PER_KV = 4
QW = Q_PER_KV * HEAD_DIM
BLOCK = 128
SSM_GROUP = 16
SSM_STATE = 64
SSM_LANES = 128
GROUPS_PER_COL = SSM_LANES // SSM_GROUP
STATE_LANES = GROUPS_PER_COL * SSM_STATE
NORM_EPS = 1e-6
NEG_INF = -1e30
ADAM_LR, ADAM_B1, ADAM_B2, ADAM_EPS, ADAM_WD, ADAM_STEP = 0.001, 0.9, 0.999, 1e-08, 0.01, 10
GELU_C = math.sqrt(2.0 / math.pi)
ROW_ALIGN = 16
VMEM_LIMIT = 56 * 1024 * 1024
ROW_TILE_CAP = 688

_NN = (((1,), (0,)), ((), ()))
_NT = (((1,), (1,)), ((), ()))
_TN = (((0,), (0,)), ((), ()))
_DIMS = {"nn": _NN, "nt": _NT, "tn": _TN}


def _params(sem, **kw):
    return pltpu.CompilerParams(dimension_semantics=sem, vmem_limit_bytes=VMEM_LIMIT, **kw)


def _pick_tile(n, cap, mult):
    best = None
    for t in range(mult, min(n, cap) + 1, mult):
        if n % t == 0:
            best = t
    if best is None:
        raise ValueError(f"no tile for {n} (cap {cap}, multiple of {mult})")
    return best


def _sigmoid(x):
    return 0.5 * jnp.tanh(0.5 * x) + 0.5


def _spec(block, index_map):
    return pl.BlockSpec(block, index_map)


def _sum_dots(ins, mode):
    tot = None
    for p in range(len(ins) // 2):
        a_ref, b_ref = ins[2 * p], ins[2 * p + 1]
        for sl in ([None] if len(b_ref.shape) == 2 else range(b_ref.shape[0])):
            if sl is None:
                a, b = a_ref[...], b_ref[...]
            elif len(a_ref.shape) == 3:
                a, b = a_ref[sl], b_ref[sl]
            else:
                width = a_ref.shape[1] // b_ref.shape[0]
                a, b = a_ref[:, sl * width:(sl + 1) * width], b_ref[sl]
            d = lax.dot_general(a.astype(BF16), b.astype(BF16), _DIMS[mode], preferred_element_type=F32)
            tot = d if tot is None else tot + d
    return tot


def _mm(name, grid, kaxis, mode, pairs, out_shape, out_spec, scale=1.0, res=None):
    npairs = len(pairs)
    has_res = res is not None
    gk = 1 if kaxis is None else grid[kaxis]
    acc_shape = tuple(d for d in out_spec.block_shape if d is not None)

    def body(*refs):
        res_ref = refs[2 * npairs] if has_res else None
        o_ref = refs[2 * npairs + has_res]
        tot = _sum_dots(refs[:2 * npairs], mode)

        def finish(acc):
            r = acc * scale if scale != 1.0 else acc
            if has_res:
                r = res_ref[...] + r
            o_ref[...] = r.astype(o_ref.dtype)

        if gk == 1:
            finish(tot)
        else:
            acc_ref = refs[-1]
            k = pl.program_id(kaxis)

            @pl.when(k == 0)
            def _():
                acc_ref[...] = tot

            @pl.when(k > 0)
            def _():
                acc_ref[...] += tot

            @pl.when(k == gk - 1)
            def _():
                finish(acc_ref[...])

    in_specs, args = [], []
    for a, a_spec, b, b_spec in pairs:
        in_specs += [a_spec, b_spec]
        args += [a, b]
    if has_res:
        in_specs.append(res[1])
        args.append(res[0])
    sem = tuple("arbitrary" if ax == kaxis else "parallel" for ax in range(len(grid)))
    return pl.pallas_call(
        body, name=name, grid=grid, in_specs=in_specs, out_specs=out_spec, out_shape=out_shape,
        scratch_shapes=[pltpu.VMEM(acc_shape, F32)] if gk > 1 else [],
        compiler_params=_params(sem),
    )(*args)


def _mm_plain(a, b, mode, out_dtype, name, tm, scale=1.0):
    M, K = a.shape
    N = b.shape[1] if mode == "nn" else b.shape[0]
    return _mm(name, (M // tm,), None, mode,
               [(a, _spec((tm, K), lambda i: (i, 0)), b, _spec(b.shape, lambda i: (0, 0)))],
               jax.ShapeDtypeStruct((M, N), out_dtype), _spec((tm, N), lambda i: (i, 0)), scale=scale)


def _wgrad_plain(a, b, name, tr):
    R, M = a.shape
    N = b.shape[1]
    return _mm(name, (R // tr,), 0, "tn",
               [(a, _spec((tr, M), lambda r: (r, 0)), b, _spec((tr, N), lambda r: (r, 0)))],
               jax.ShapeDtypeStruct((M, N), F32), _spec((M, N), lambda r: (0, 0)))


def _for_real_rows(tile, tpe, tm, hbm, buf, fn):
    tb, tj = tile // tpe, tile % tpe

    @pl.when(tj == 0)
    def _():
        fn(hbm.at[tb, pl.ds(0, tm - N_META), :], buf.at[pl.ds(N_META, tm - N_META), :])

    @pl.when(tj > 0)
    def _():
        fn(hbm.at[tb, pl.ds(tj * tm - N_META, tm), :], buf)


def _embed_norm(x, meta, g, tm, name):
    B, S, D = x.shape
    L = S + N_META
    T = B * L
    nt = T // tm
    tpe = L // tm

    def body(x_hbm, meta_ref, g_ref, h_ref, n_ref, xbuf, sems):
        i = pl.program_id(0)
        slot = i % 2

        def fetch(tile, sl, act):
            _for_real_rows(tile, tpe, tm, x_hbm, xbuf.at[sl], lambda src, dst: act(pltpu.make_async_copy(src, dst, sems.at[sl])))

        @pl.when(i == 0)
        def _():
            fetch(i, slot, lambda cp: cp.start())

        @pl.when(i + 1 < nt)
        def _():
            fetch(i + 1, 1 - slot, lambda cp: cp.start())

        fetch(i, slot, lambda cp: cp.wait())

        @pl.when(i % tpe == 0)
        def _():
            xbuf[slot, 0:N_META, :] = meta_ref[...]

        hv = xbuf[slot]
        h_ref[...] = hv
        r = lax.rsqrt(jnp.mean(hv * hv, axis=-1, keepdims=True) + NORM_EPS)
        n_ref[...] = ((hv * r) * g_ref[...]).astype(BF16)

    row = pl.BlockSpec((tm, D), lambda i: (i, 0))
    return pl.pallas_call(
        body, name=name, grid=(nt,),
        in_specs=[pl.BlockSpec(memory_space=pl.ANY), pl.BlockSpec((N_META, D), lambda i: (0, 0)),
                  pl.BlockSpec((1, D), lambda i: (0, 0))],
        out_specs=[row, row],
        out_shape=[jax.ShapeDtypeStruct((T, D), F32), jax.ShapeDtypeStruct((T, D), BF16)],
        scratch_shapes=[pltpu.VMEM((2, tm, D), F32), pltpu.SemaphoreType.DMA((2,))],
        compiler_params=_params(("arbitrary",)),
    )(x, meta, g)


def _rmsnorm_fwd(h, g, tm, name):
    T, D = h.shape

    def body(h_ref, g_ref, o_ref):
        x = h_ref[...]
        r = lax.rsqrt(jnp.mean(x * x, axis=-1, keepdims=True) + NORM_EPS)
        o_ref[...] = ((x * r) * g_ref[...]).astype(BF16)

    return pl.pallas_call(
        body, name=name, grid=(T // tm,),
        in_specs=[pl.BlockSpec((tm, D), lambda i: (i, 0)), pl.BlockSpec((1, D), lambda i: (0, 0))],
        out_specs=pl.BlockSpec((tm, D), lambda i: (i, 0)),
        out_shape=jax.ShapeDtypeStruct((T, D), BF16),
        compiler_params=_params(("parallel",)),
    )(h, g)


def _fold8(x):
    return jnp.sum(x.reshape(x.shape[0] // 8, 8, x.shape[1]), axis=0)


def _mm_norm_bwd(name, mode, pairs, h, g, dres, tm, riders=None, examples=None):
    T, D = h.shape
    nt = T // tm
    npairs = len(pairs)
    tpe = None if examples is None else T // examples // tm

    def body(*refs):
        if examples is None:
            h_ref, g_ref, dres_ref, dh_ref, dhb_ref, dg_ref, acc_ref = refs[2 * npairs:]
        else:
            h_ref, g_ref, dres_ref, dx_hbm, dmeta_ref, dg_ref, acc_ref, dbuf, sems = refs[2 * npairs:]
        i = pl.program_id(0)
        x = h_ref[...]
        r = lax.rsqrt(jnp.mean(x * x, axis=-1, keepdims=True) + NORM_EPS)
        xhat = x * r
        dy = _sum_dots(refs[:2 * npairs], mode)
        dxhat = dy * g_ref[...]
        dx = r * (dxhat - xhat * jnp.mean(dxhat * xhat, axis=-1, keepdims=True))
        dh = dres_ref[...] + dx
        if examples is None:
            dh_ref[...] = dh
            dhb_ref[...] = dh.astype(BF16)
        else:
            slot = i % 2

            def push(tile, sl, act):
                _for_real_rows(tile, tpe, tm, dx_hbm, dbuf.at[sl],
                               lambda dst, src: act(pltpu.make_async_copy(src, dst, sems.at[sl])))

            dbuf[slot] = dh
            push(i, slot, lambda cp: cp.start())

            @pl.when(i > 0)
            def _():
                push(i - 1, 1 - slot, lambda cp: cp.wait())

            @pl.when(i == nt - 1)
            def _():
                push(i, slot, lambda cp: cp.wait())

            @pl.when(i == 0)
            def _():
                dmeta_ref[...] = dh[0:N_META]

            @pl.when((i > 0) & (i % tpe == 0))
            def _():
                dmeta_ref[...] += dh[0:N_META]
        part = _fold8(dy * xhat)

        @pl.when(i == 0)
        def _():
            acc_ref[...] = part

        @pl.when(i > 0)
        def _():
            acc_ref[...] += part

        @pl.when(i == nt - 1)
        def _():
            dg_ref[...] = jnp.sum(acc_ref[...], axis=0, keepdims=True)

    row = pl.BlockSpec((tm, D), lambda i: (i, 0))
    vec = pl.BlockSpec((1, D), lambda i: (0, 0))
    in_specs, args = [], []
    for a, a_spec, b, b_spec in pairs:
        in_specs += [a_spec, b_spec]
        args += [a, b]
    if examples is None:
        return _call(body, name, (nt,), in_specs + [row, vec, row], [row, row, vec],
                     [jax.ShapeDtypeStruct((T, D), F32), jax.ShapeDtypeStruct((T, D), BF16), jax.ShapeDtypeStruct((1, D), F32)],
                     [pltpu.VMEM((8, D), F32)], ("arbitrary",), (*args, h, g, dres), riders)
    S = T // examples - N_META
    return _call(body, name, (nt,), in_specs + [row, vec, row],
                 [_ANY, pl.BlockSpec((N_META, D), lambda i: (0, 0)), vec],
                 [jax.ShapeDtypeStruct((examples, S, D), F32), jax.ShapeDtypeStruct((N_META, D), F32),
                  jax.ShapeDtypeStruct((1, D), F32)],
                 [pltpu.VMEM((8, D), F32), pltpu.VMEM((2, tm, D), F32), pltpu.SemaphoreType.DMA((2,))],
                 ("arbitrary",), (*args, h, g, dres), riders)


def _ffn_up(n, w1t, w3t, tm, name, riders=None):
    T, D = n.shape
    Fs = w1t.shape[1]

    def body(n_ref, w1_ref, w3_ref, a_ref, c_ref, s_ref):
        x = n_ref[...]
        a = lax.dot_general(x, w1_ref[...], _NT, preferred_element_type=F32)
        c = lax.dot_general(x, w3_ref[...], _NT, preferred_element_type=F32)
        a_ref[...] = a.astype(BF16)
        c_ref[...] = c.astype(BF16)
        s_ref[...] = (a * _sigmoid(a) * c).astype(BF16)

    w_spec = _spec((None, Fs, D), lambda s, i: (s, 0, 0))
    o_spec = _spec((None, tm, Fs), lambda s, i: (s, i, 0))
    o_shape = jax.ShapeDtypeStruct((N_CHIPS, T, Fs), BF16)
    return _call(body, name, (N_CHIPS, T // tm), [_spec((tm, D), lambda s, i: (i, 0)), w_spec, w_spec],
                 [o_spec, o_spec, o_spec], [o_shape, o_shape, o_shape], [], ("parallel", "parallel"), (n, w1t, w3t), riders)


def _ffn_down(s, w2, h, tm, name):
    _, T, Fs = s.shape
    D = w2.shape[2]
    row = _spec((tm, D), lambda i: (i, 0))
    return _mm(name, (T // tm,), None, "nn",
               [(s, _spec((N_CHIPS, tm, Fs), lambda i: (0, i, 0)), w2, _spec((N_CHIPS, Fs, D), lambda i: (0, 0, 0)))],
               jax.ShapeDtypeStruct((T, D), F32), row, scale=0.5, res=(h, row))


def _ffn_dhidden(dhb, w2, a, c, tm, name, riders=None):
    T, D = dhb.shape
    Fs = w2.shape[1]

    def body(dh_ref, w2_ref, a_ref, c_ref, da_ref, dc_ref):
        d = 0.5 * lax.dot_general(dh_ref[...], w2_ref[pl.program_id(1)], _NT, preferred_element_type=F32)
        av = a_ref[...].astype(F32)
        cv = c_ref[...].astype(F32)
        sg = _sigmoid(av)
        da_ref[...] = (d * cv * (sg * (1.0 + av * (1.0 - sg)))).astype(BF16)
        dc_ref[...] = (d * (av * sg)).astype(BF16)

    h_spec = _spec((None, tm, Fs), lambda i, s: (s, i, 0))
    o_shape = jax.ShapeDtypeStruct((N_CHIPS, T, Fs), BF16)
    return _call(body, name, (T // tm, N_CHIPS),
                 [_spec((tm, D), lambda i, s: (i, 0)), _once((N_CHIPS, Fs, D), lambda i, s: (0, 0, 0)), h_spec, h_spec],
                 [h_spec, h_spec], [o_shape, o_shape], [], ("parallel", "parallel"), (dhb, w2, a, c), riders)


def _wgrad_hidden_rows(s, dhb, tr, name, scale):
    _, T, Fs = s.shape
    D = dhb.shape[1]
    return _mm(name, (N_CHIPS, T // tr), 1, "tn",
               [(s, _spec((None, tr, Fs), lambda k, r: (k, r, 0)), dhb, _spec((tr, D), lambda k, r: (r, 0)))],
               jax.ShapeDtypeStruct((N_CHIPS, Fs, D), F32), _spec((None, Fs, D), lambda k, r: (k, 0, 0)), scale=scale)


def _once(block, index_map):
    return pl.BlockSpec(block, index_map, pipeline_mode=pl.Buffered(1))


def _ffn_dn(da, w1t, dc, w3t, h, g, dres, tm, name, riders=None, examples=None):
    _, T, Fs = da.shape
    D = w1t.shape[2]
    h_spec = _spec((N_CHIPS, tm, Fs), lambda i: (0, i, 0))
    w_spec = _once((N_CHIPS, Fs, D), lambda i: (0, 0, 0))
    return _mm_norm_bwd(name, "nn", [(da, h_spec, w1t, w_spec), (dc, h_spec, w3t, w_spec)], h, g, dres, tm, riders,
                        examples)


def _mm_side_by_side(a, w, mode, out_dtype, name, tm, first=0, count=N_CHIPS, scale=1.0):
    T, K = a.shape
    assert first % count == 0
    n = w.shape[2] if mode == "nn" else w.shape[1]

    def body(a_ref, w_ref, o_ref):
        av = a_ref[...].astype(BF16)
        for j in range(count):
            r = lax.dot_general(av, w_ref[j].astype(BF16), _DIMS[mode], preferred_element_type=F32)
            o_ref[:, j * n:(j + 1) * n] = (r * scale if scale != 1.0 else r).astype(o_ref.dtype)

    return pl.pallas_call(
        body, name=name, grid=(T // tm,),
        in_specs=[_spec((tm, K), lambda i: (i, 0)), _once((count,) + w.shape[1:], lambda i: (first // count, 0, 0))],
        out_specs=_spec((tm, count * n), lambda i: (i, 0)),
        out_shape=jax.ShapeDtypeStruct((T, count * n), out_dtype),
        compiler_params=_params(("parallel",)),
    )(a, w)


def _mm_colslots(a, w, out_dtype, name, tm, first=0, count=N_CHIPS, scale=1.0):
    return _mm_side_by_side(a, w, "nn", out_dtype, name, tm, first, count, scale)


def _wgrad_colslots(a, d, tr, name):
    T, K = a.shape
    Ns = d.shape[1] // N_CHIPS
    return _mm(name, (N_CHIPS, T // tr), 1, "tn",
               [(a, _spec((tr, K), lambda k, r: (r, 0)), d, _spec((tr, Ns), lambda k, r: (r, k)))],
               jax.ShapeDtypeStruct((N_CHIPS, K, Ns), F32), _spec((None, K, Ns), lambda k, r: (k, 0, 0)))


def _mm_rowslots(a, w, h, tm, name):
    T = a.shape[0]
    N = w.shape[2]
    row = _spec((tm, N), lambda i: (i, 0))
    return _mm(name, (T // tm,), None, "nn",
               [(a, _spec((tm, a.shape[1]), lambda i: (i, 0)), w, _once(w.shape, lambda i: (0, 0, 0)))],
               jax.ShapeDtypeStruct((T, N), F32), row, res=(h, row))


def _wgrad_rowslots(a, d, tr, name):
    T = a.shape[0]
    Ks = a.shape[1] // N_CHIPS
    N = d.shape[1]
    return _mm(name, (N_CHIPS, T // tr), 1, "tn",
               [(a, _spec((tr, Ks), lambda k, r: (r, k)), d, _spec((tr, N), lambda k, r: (r, 0)))],
               jax.ShapeDtypeStruct((N_CHIPS, Ks, N), F32), _spec((None, Ks, N), lambda k, r: (k, 0, 0)))


def _gelu_parts(x):
    inner = GELU_C * (x + 0.044715 * (x * x * x))
    t = jnp.tanh(inner)
    return t, GELU_C * (1.0 + 3.0 * 0.044715 * (x * x))


def _gelu_fwd(y, tm, name):
    T, W = y.shape

    def body(y_ref, o_ref):
        x = y_ref[...]
        t, _ = _gelu_parts(x)
        o_ref[...] = (0.5 * x * (1.0 + t)).astype(BF16)

    spec = pl.BlockSpec((tm, W), lambda i: (i, 0))
    return pl.pallas_call(body, name=name, grid=(T // tm,), in_specs=[spec], out_specs=spec,
                          out_shape=jax.ShapeDtypeStruct((T, W), BF16),
                          compiler_params=_params(("parallel",)))(y)


def _gelu_bwd(pairs, y, tm, name):
    T, W = y.shape
    npairs = len(pairs)

    def body(*refs):
        y_ref, o_ref = refs[2 * npairs], refs[2 * npairs + 1]
        x = y_ref[...]
        t, dinner = _gelu_parts(x)
        o_ref[...] = _sum_dots(refs[:2 * npairs], "nt") * (0.5 * (1.0 + t) + 0.5 * x * (1.0 - t * t) * dinner)

    spec = pl.BlockSpec((tm, W), lambda i: (i, 0))
    in_specs, args = [], []
    for d, w in pairs:
        in_specs += [_spec((tm, d.shape[1]), lambda i: (i, 0)), _once(w.shape, lambda i: (0, 0, 0))]
        args += [d, w]
    return pl.pallas_call(body, name=name, grid=(T // tm,), in_specs=in_specs + [spec], out_specs=spec,
                          out_shape=jax.ShapeDtypeStruct((T, W), F32),
                          compiler_params=_params(("parallel",)))(*args, y)


def _merge_cols(D):
    cb = 512 if D % 512 == 0 else D
    return cb, D // cb


def _merge_fwd(gates, attn, ga, gb, tm, name):
    T, D = attn.shape
    cb, nc = _merge_cols(D)

    def body(gat_ref, gss_ref, attn_ref, ga_ref, gb_ref, o_ref):
        ssm = ga_ref[...] * _sigmoid(gb_ref[...])
        o_ref[...] = (_sigmoid(gat_ref[...]) * attn_ref[...] + _sigmoid(gss_ref[...]) * ssm).astype(BF16)

    def col(block):
        return pl.BlockSpec((tm, cb), lambda i, j: (i, block * nc + j))

    return pl.pallas_call(
        body, name=name, grid=(T // tm, nc),
        in_specs=[col(0), col(1), col(0), col(0), col(0)],
        out_specs=col(0), out_shape=jax.ShapeDtypeStruct((T, D), BF16),
        compiler_params=_params(("parallel", "parallel")),
    )(gates, gates, attn, ga, gb)


def _merge_bwd(dhb, w_out, gates, attn, ga, gb, tm, name):
    T, D = attn.shape
    cb, nc = _merge_cols(D)
    Ks = w_out.shape[1]
    spb = cb // Ks

    def body(dh_ref, w_ref, gat_ref, gss_ref, attn_ref, ga_ref, gb_ref, dattn_ref, dgat_ref, dgss_ref, dga_ref, dgb_ref):
        dh = dh_ref[...]
        d = jnp.concatenate([lax.dot_general(dh, w_ref[s], _NT, preferred_element_type=F32) for s in range(spb)], axis=1)
        sa = _sigmoid(gat_ref[...])
        ss = _sigmoid(gss_ref[...])
        sb = _sigmoid(gb_ref[...])
        gav = ga_ref[...]
        dattn_ref[...] = d * sa
        dgat_ref[...] = (d * attn_ref[...] * (sa * (1.0 - sa))).astype(BF16)
        dgss_ref[...] = (d * (gav * sb) * (ss * (1.0 - ss))).astype(BF16)
        dssm = d * ss
        dga_ref[...] = (dssm * sb).astype(BF16)
        dgb_ref[...] = (dssm * gav * (sb * (1.0 - sb))).astype(BF16)

    def col(block):
        return pl.BlockSpec((tm, cb), lambda i, j: (i, block * nc + j))

    b16 = jax.ShapeDtypeStruct((T, D), BF16)
    return pl.pallas_call(
        body, name=name, grid=(T // tm, nc),
        in_specs=[pl.BlockSpec((tm, D), lambda i, j: (i, 0)), pl.BlockSpec((spb, Ks, D), lambda i, j: (j, 0, 0)),
                  col(0), col(1), col(0), col(0), col(0)],
        out_specs=[col(0)] * 5,
        out_shape=[jax.ShapeDtypeStruct((T, D), F32), b16, b16, b16, b16],
        compiler_params=_params(("parallel", "parallel")),
    )(dhb, w_out, gates, gates, attn, ga, gb)


def _loss_head(h, g, target, tm, name):
    T, D = h.shape
    B, S, _ = target.shape
    L = S + N_META
    nt = T // tm
    tpe = L // tm

    def body(h_ref, g_ref, t_hbm, dh_ref, dhb_ref, dg_ref, loss_ref, tbuf, acc_g, acc_l, sems):
        i = pl.program_id(0)
        j = i % tpe
        slot = i % 2

        def fetch(tile, sl, act):
            tb, tj = tile // tpe, tile % tpe

            @pl.when(tj == 0)
            def _():
                act(pltpu.make_async_copy(t_hbm.at[tb, pl.ds(0, tm - N_META), :],
                                          tbuf.at[sl, pl.ds(N_META, tm - N_META), :], sems.at[sl]))

            @pl.when(tj > 0)
            def _():
                act(pltpu.make_async_copy(t_hbm.at[tb, pl.ds(tj * tm - N_META, tm), :], tbuf.at[sl], sems.at[sl]))

        @pl.when(i == 0)
        def _():
            tbuf[:, 0:N_META, :] = jnp.zeros((2, N_META, D), F32)
            fetch(i, slot, lambda cp: cp.start())

        @pl.when(i + 1 < nt)
        def _():
            fetch(i + 1, 1 - slot, lambda cp: cp.start())

        fetch(i, slot, lambda cp: cp.wait())

        x = h_ref[...]
        gv = g_ref[...]
        r = lax.rsqrt(jnp.mean(x * x, axis=-1, keepdims=True) + NORM_EPS)
        xhat = x * r
        pos = j * tm + lax.broadcasted_iota(jnp.int32, (tm, 1), 0)
        err = jnp.where(pos >= N_META, xhat * gv - tbuf[slot], 0.0)
        dy = err * (1.0 / D)
        dxhat = dy * gv
        dh = r * (dxhat - xhat * jnp.mean(dxhat * xhat, axis=-1, keepdims=True))
        dh_ref[...] = dh
        dhb_ref[...] = dh.astype(BF16)
        pg = _fold8(dy * xhat)
        pe = _fold8(err * err)

        @pl.when(i == 0)
        def _():
            acc_g[...] = pg
            acc_l[...] = pe

        @pl.when(i > 0)
        def _():
            acc_g[...] += pg
            acc_l[...] += pe

        @pl.when(i == nt - 1)
        def _():
            dg_ref[...] = jnp.sum(acc_g[...], axis=0, keepdims=True)
            loss_ref[...] = jnp.full((1, D), (0.5 / D) * jnp.sum(acc_l[...]), F32)

    row = pl.BlockSpec((tm, D), lambda i: (i, 0))
    vec = pl.BlockSpec((1, D), lambda i: (0, 0))
    return pl.pallas_call(
        body, name=name, grid=(nt,),
        in_specs=[row, vec, pl.BlockSpec(memory_space=pl.ANY)], out_specs=[row, row, vec, vec],
        out_shape=[jax.ShapeDtypeStruct((T, D), F32), jax.ShapeDtypeStruct((T, D), BF16),
                   jax.ShapeDtypeStruct((1, D), F32), jax.ShapeDtypeStruct((1, D), F32)],
        scratch_shapes=[pltpu.VMEM((2, tm, D), F32), pltpu.VMEM((8, D), F32), pltpu.VMEM((8, D), F32),
                        pltpu.SemaphoreType.DMA((2,))],
        compiler_params=_params(("arbitrary",)),
    )(h, g, target)


def _heads_to_rows(blk):
    return jnp.concatenate([blk[:, g * HEAD_DIM:(g + 1) * HEAD_DIM] for g in range(Q_PER_KV)], axis=0)


def _rows_to_heads(x):
    rows = x.shape[0] // Q_PER_KV
    return jnp.concatenate([x[g * rows:(g + 1) * rows] for g in range(Q_PER_KV)], axis=1)


def _causal(R):
    kj = lax.broadcasted_iota(jnp.int32, (BLOCK, R), 0)
    qi = lax.broadcasted_iota(jnp.int32, (BLOCK, R), 1) & (BLOCK - 1)
    return kj <= qi


def _band_probs(s_band, s_m, sink):
    m = jnp.maximum(jnp.maximum(jnp.max(s_band, axis=0, keepdims=True), jnp.max(s_m, axis=0, keepdims=True)), sink)
    e_b, e_m, e_s = jnp.exp(s_band - m), jnp.exp(s_m - m), jnp.exp(sink - m)
    inv = 1.0 / (jnp.sum(e_b, axis=0, keepdims=True) + jnp.sum(e_m, axis=0, keepdims=True) + e_s)
    return e_b * inv, e_m * inv, e_s * inv


def _fold_band(tri, two):
    return jnp.where(tri, two[BLOCK:2 * BLOCK], two[0:BLOCK])


def _unfold_band(tri, band):
    return jnp.concatenate([jnp.where(tri, 0.0, band), jnp.where(tri, band, 0.0)], axis=0)


def _meta_probs(qm, k_m, sink_m):
    R = qm.shape[0]
    s = lax.dot_general(qm, k_m, _NT, preferred_element_type=F32)
    qi = lax.broadcasted_iota(jnp.int32, (R, N_META), 0) & (N_META - 1)
    kj = lax.broadcasted_iota(jnp.int32, (R, N_META), 1)
    s = jnp.where(kj <= qi, s, NEG_INF)
    m = jnp.maximum(jnp.max(s, axis=-1, keepdims=True), sink_m)
    e, e_s = jnp.exp(s - m), jnp.exp(sink_m - m)
    inv = 1.0 / (jnp.sum(e, axis=-1, keepdims=True) + e_s)
    return e * inv, e_s * inv


def _block_start(n):
    return pl.multiple_of(N_META + n * BLOCK, ROW_ALIGN)


def _kv(blk):
    return blk[:, 0:HEAD_DIM], blk[:, HEAD_DIM:2 * HEAD_DIM]


def _attn_fwd(q, kv, sink_row, sink_meta, B, name, riders=None):
    T, D = q.shape
    L = T // B
    KV = D // QW
    nb = (L - N_META) // BLOCK

    def body(q_ref, kv_ref, sk_ref, skm_ref, o_ref, kvs):
        kvs[...] = kv_ref[...].astype(BF16)
        k_m, v_m = _kv(kvs[0:N_META, :])
        p, _ = _meta_probs(_heads_to_rows(q_ref[0:N_META, :]), k_m, skm_ref[0])
        o_ref[0:N_META, :] = _rows_to_heads(jnp.dot(p.astype(BF16), v_m, preferred_element_type=F32))
        tri = _causal(Q_PER_KV * BLOCK)

        def block(cur, first, keys):
            k2, v2 = _kv(kvs[keys, :])
            qb = _heads_to_rows(q_ref[pl.ds(cur, BLOCK), :])
            st = lax.dot_general(k2, qb, _NT, preferred_element_type=F32)
            smt = lax.dot_general(k_m, qb, _NT, preferred_element_type=F32)
            s_band = jnp.where(tri, st, NEG_INF) if first else _fold_band(tri, st)
            p_b, p_m, _ = _band_probs(s_band, smt, sk_ref[0])
            p2 = (p_b if first else _unfold_band(tri, p_b)).astype(BF16)
            o = (lax.dot_general(p2, v2, _TN, preferred_element_type=F32)
                 + lax.dot_general(p_m.astype(BF16), v_m, _TN, preferred_element_type=F32))
            o_ref[pl.ds(cur, BLOCK), :] = _rows_to_heads(o)

        block(N_META, True, pl.ds(N_META, BLOCK))

        def step(n, carry):
            block(_block_start(n), False, pl.ds(_block_start(n - 1), 2 * BLOCK))
            return carry

        lax.fori_loop(1, nb, step, 0, unroll=5 if (nb - 1) % 5 == 0 else 1)

    q_spec = pl.BlockSpec((L, QW), lambda b, h: (b, h))
    return _call(body, name, (B, KV),
                 [q_spec, pl.BlockSpec((L, 2 * HEAD_DIM), lambda b, h: (b, h)),
                  pl.BlockSpec((1, 1, Q_PER_KV * BLOCK), lambda b, h: (h, 0, 0)),
                  pl.BlockSpec((1, Q_PER_KV * N_META, 1), lambda b, h: (h, 0, 0))],
                 [q_spec], [jax.ShapeDtypeStruct((T, D), F32)], [pltpu.VMEM((L, 2 * HEAD_DIM), BF16)],
                 ("parallel", "parallel"), (q, kv, sink_row, sink_meta), riders)


def _attn_bwd(q, kv, o, do, sink_row, sink_meta, B, name, riders=None):
    T, D = q.shape
    L = T // B
    KV = D // QW
    nb = (L - N_META) // BLOCK
    R = Q_PER_KV * BLOCK
    scale = HEAD_DIM ** -0.5

    def head_totals(col, rows_per_head):
        rid = lax.broadcasted_iota(jnp.int32, (8, 128), 0)
        out = jnp.zeros((8, 128), F32)
        for g in range(Q_PER_KV):
            out = out + jnp.where(rid == g, jnp.sum(col[g * rows_per_head:(g + 1) * rows_per_head, :]), 0.0)
        return out

    def body(q_ref, kv_ref, o_ref, do_ref, sk_ref, skm_ref, dq_ref, dkv_ref, dsk_ref, kvs, acc, acc_sink):
        b = pl.program_id(1)
        kvs[...] = kv_ref[...].astype(BF16)
        acc[...] = jnp.zeros_like(acc)
        k_m, v_m = _kv(kvs[0:N_META, :])

        qm = _heads_to_rows(q_ref[0:N_META, :])
        dom = _heads_to_rows(do_ref[0:N_META, :])
        delta = jnp.sum(dom * _heads_to_rows(o_ref[0:N_META, :]), axis=-1, keepdims=True)
        p, p_s = _meta_probs(qm, k_m, skm_ref[0])
        domb = dom.astype(BF16)
        ds = (p * (lax.dot_general(domb, v_m, _NT, preferred_element_type=F32) - delta)).astype(BF16)
        dq_ref[0:N_META, :] = _rows_to_heads(jnp.dot(ds, k_m, preferred_element_type=F32) * scale).astype(BF16)
        acc[0:N_META, :] += jnp.concatenate([lax.dot_general(ds, qm, _TN, preferred_element_type=F32),
                                             lax.dot_general(p.astype(BF16), domb, _TN, preferred_element_type=F32)], axis=1)
        sink_tot = head_totals(-p_s * delta, N_META)
        tri = _causal(R)
        acc_sink[...] = jnp.zeros_like(acc_sink)
        ones = jnp.ones((8, HEAD_DIM), BF16)

        def block(cur, first, keys):
            k2, v2 = _kv(kvs[keys, :])
            rows = pl.ds(cur, BLOCK)
            qb = _heads_to_rows(q_ref[rows, :])
            dob = _heads_to_rows(do_ref[rows, :])
            prod = dob * _heads_to_rows(o_ref[rows, :])
            hi = prod.astype(BF16)
            lo = (prod - hi.astype(F32)).astype(BF16)
            delta = (lax.dot_general(ones, hi, _NT, preferred_element_type=F32)
                     + lax.dot_general(ones, lo, _NT, preferred_element_type=F32))[0:1]
            dobb = dob.astype(BF16)
            st = lax.dot_general(k2, qb, _NT, preferred_element_type=F32)
            smt = lax.dot_general(k_m, qb, _NT, preferred_element_type=F32)
            s_band = jnp.where(tri, st, NEG_INF) if first else _fold_band(tri, st)
            p_b, p_m, p_s = _band_probs(s_band, smt, sk_ref[0])
            dpt = lax.dot_general(v2, dobb, _NT, preferred_element_type=F32)
            dpm = lax.dot_general(v_m, dobb, _NT, preferred_element_type=F32)
            ds_b = p_b * ((dpt if first else _fold_band(tri, dpt)) - delta)
            ds2 = (ds_b if first else _unfold_band(tri, ds_b)).astype(BF16)
            p2 = (p_b if first else _unfold_band(tri, p_b)).astype(BF16)
            dsm = (p_m * (dpm - delta)).astype(BF16)
            pm = p_m.astype(BF16)
            dq = (lax.dot_general(ds2, k2, _TN, preferred_element_type=F32)
                  + lax.dot_general(dsm, k_m, _TN, preferred_element_type=F32))
            dq_ref[rows, :] = _rows_to_heads(dq * scale).astype(BF16)
            acc[keys, :] += jnp.concatenate([jnp.dot(ds2, qb, preferred_element_type=F32),
                                             jnp.dot(p2, dobb, preferred_element_type=F32)], axis=1)
            acc[0:N_META, :] += jnp.concatenate([jnp.dot(dsm, qb, preferred_element_type=F32),
                                                 jnp.dot(pm, dobb, preferred_element_type=F32)], axis=1)
            acc_sink[0:1, :] += -p_s * delta

        block(N_META, True, pl.ds(N_META, BLOCK))

        def step(n, carry):
            block(_block_start(n), False, pl.ds(_block_start(n - 1), 2 * BLOCK))
            return carry

        lax.fori_loop(1, nb, step, 0, unroll=5 if (nb - 1) % 5 == 0 else 1)
        dkv_ref[...] = acc[...].astype(BF16)
        rid = lax.broadcasted_iota(jnp.int32, (8, 128), 0)
        tot = sink_tot
        for g in range(Q_PER_KV):
            tot = tot + jnp.where(rid == g, jnp.sum(acc_sink[:, g * BLOCK:(g + 1) * BLOCK]), 0.0)

        @pl.when(b == 0)
        def _():
            dsk_ref[0] = tot

        @pl.when(b > 0)
        def _():
            dsk_ref[0] += tot

    q_spec = pl.BlockSpec((L, QW), lambda h, b: (b, h))
    kv_spec = pl.BlockSpec((L, 2 * HEAD_DIM), lambda h, b: (b, h))
    return _call(body, name, (KV, B),
                 [q_spec, kv_spec, q_spec, q_spec,
                  pl.BlockSpec((1, 1, R), lambda h, b: (h, 0, 0)),
                  pl.BlockSpec((1, Q_PER_KV * N_META, 1), lambda h, b: (h, 0, 0))],
                 [q_spec, kv_spec, pl.BlockSpec((1, 8, 128), lambda h, b: (h, 0, 0))],
                 [jax.ShapeDtypeStruct((T, D), BF16), jax.ShapeDtypeStruct((T, KV * 2 * HEAD_DIM), BF16),
                  jax.ShapeDtypeStruct((KV, 8, 128), F32)],
                 [pltpu.VMEM((L, 2 * HEAD_DIM), BF16), pltpu.VMEM((L, 2 * HEAD_DIM), F32), pltpu.VMEM((8, R), F32)],
                 ("parallel", "arbitrary"), (q, kv, o, do, sink_row, sink_meta), riders)


def _cmul_add(acc_r, acc_i, lr, li, xr, xi):
    return acc_r + (lr * xr - li * xi), acc_i + (lr * xi + li * xr)


def _cols_per_step(ncol):
    for cps in (4, 2):
        if ncol % cps == 0:
            return cps
    return 1


def _ssm_fwd(u, bmat, cmat, dskip, tables, nbatch, rc, name):
    T, W = u.shape
    ncol = W // SSM_LANES
    nch = T // rc
    S = STATE_LANES
    cps = _cols_per_step(ncol)
    assert nbatch == 4

    def body(u_ref, b_ref, c_ref, d_ref, tab_ref, y_ref, xs_ref, st_ref, carry_ref):
        ch = pl.program_id(1)

        @pl.when(ch == 0)
        def _():
            carry_ref[...] = jnp.zeros_like(carry_ref)

        uv = u_ref[...]
        for k in range(cps):
            st_ref[:, 2 * S * k:2 * S * (k + 1)] = jnp.dot(uv[:, SSM_LANES * k:SSM_LANES * (k + 1)].astype(BF16), b_ref[k],
                                                           preferred_element_type=F32)
        low = lax.broadcasted_iota(jnp.int32, (8, S), 0) < nbatch

        def tile(k, r0, c_r, c_i):
            re, im = slice(2 * S * k, 2 * S * k + S), slice(2 * S * k + S, 2 * S * (k + 1))
            la_r, la_i = tab_ref[k, :, 0:S], tab_ref[k, :, S:2 * S]
            lb_r, lb_i = tab_ref[k, :, 2 * S:3 * S], tab_ref[k, :, 3 * S:4 * S]
            v_r = st_ref[pl.ds(r0, 8), re]
            v_i = st_ref[pl.ds(r0, 8), im]
            v_r, v_i = _cmul_add(v_r, v_i, la_r, la_i, pltpu.roll(v_r, nbatch, 0), pltpu.roll(v_i, nbatch, 0))
            rc_r, rc_i = pltpu.roll(c_r, nbatch, 0), pltpu.roll(c_i, nbatch, 0)
            cb_r, cb_i = jnp.where(low, rc_r, c_r), jnp.where(low, rc_i, c_i)
            v_r, v_i = _cmul_add(v_r, v_i, lb_r, lb_i, cb_r, cb_i)
            st_ref[pl.ds(r0, 8), re] = v_r
            st_ref[pl.ds(r0, 8), im] = v_i
            return v_r, v_i

        def step(i, carry):
            r0 = pl.multiple_of(i * 8, 8)
            out = []
            for k in range(cps):
                out += list(tile(k, r0, carry[2 * k], carry[2 * k + 1]))
            return tuple(out)

        halves = tuple(carry_ref[:, S * j:S * (j + 1)] for j in range(2 * cps))
        halves = lax.fori_loop(0, rc // 8, step, halves)
        for j in range(2 * cps):
            carry_ref[:, S * j:S * (j + 1)] = halves[j]
        xb = st_ref[...].astype(BF16)
        xs_ref[...] = xb
        for k in range(cps):
            cols = slice(SSM_LANES * k, SSM_LANES * (k + 1))
            y_ref[:, cols] = (jnp.dot(xb[:, 2 * S * k:2 * S * (k + 1)], c_ref[k], preferred_element_type=F32)
                              + d_ref[:, cols] * uv[:, cols])

    return pl.pallas_call(
        body, name=name, grid=(ncol // cps, nch),
        in_specs=[pl.BlockSpec((rc, cps * SSM_LANES), lambda g, c: (c, g)),
                  pl.BlockSpec((cps, SSM_LANES, 2 * S), lambda g, c: (g, 0, 0)),
                  pl.BlockSpec((cps, 2 * S, SSM_LANES), lambda g, c: (g, 0, 0)),
                  pl.BlockSpec((1, cps * SSM_LANES), lambda g, c: (0, g)),
                  pl.BlockSpec((cps, 8, 4 * S), lambda g, c: (g, 0, 0))],
        out_specs=[pl.BlockSpec((rc, cps * SSM_LANES), lambda g, c: (c, g)),
                   pl.BlockSpec((rc, cps * 2 * S), lambda g, c: (c, g))],
        out_shape=[jax.ShapeDtypeStruct((T, W), F32), jax.ShapeDtypeStruct((T, ncol * 2 * S), BF16)],
        scratch_shapes=[pltpu.VMEM((rc, cps * 2 * S), F32), pltpu.VMEM((8, cps * 2 * S), F32)],
        compiler_params=_params(("parallel", "arbitrary")),
    )(u, bmat, cmat, dskip, tables)


def _ssm_bwd(dy, u, xs, bmat, cmat, dskip, tables, nbatch, rc, name):
    T, W = u.shape
    ncol = W // SSM_LANES
    nch = T // rc
    S = STATE_LANES
    ntile = rc // 16
    cps = _cols_per_step(ncol)

    def body(dy_ref, u_ref, xs_ref, b_ref, c_ref, d_ref, tab_ref,
             du_ref, db_ref, dc_ref, dl_ref, dd_ref, st_ref, carry_ref, accl_ref, accd_ref):
        ch = pl.program_id(1)

        @pl.when(ch == 0)
        def _():
            carry_ref[...] = jnp.zeros_like(carry_ref)
            accl_ref[...] = jnp.zeros_like(accl_ref)
            accd_ref[...] = jnp.zeros_like(accd_ref)
            db_ref[...] = jnp.zeros_like(db_ref)
            dc_ref[...] = jnp.zeros_like(dc_ref)

        dyv = dy_ref[...]
        uv = u_ref[...]
        dyb = dyv.astype(BF16)
        for k in range(cps):
            st_ref[:, 2 * S * k:2 * S * (k + 1)] = lax.dot_general(dyb[:, SSM_LANES * k:SSM_LANES * (k + 1)], c_ref[k], _NT,
                                                                   preferred_element_type=F32)
        low = lax.broadcasted_iota(jnp.int32, (8, S), 0) < nbatch

        def tile(k, r0, x_r, x_i, c_r, c_i, al_r, al_i):
            re, im = slice(2 * S * k, 2 * S * k + S), slice(2 * S * k + S, 2 * S * (k + 1))
            la_r, la_i = tab_ref[k, :, 0:S], tab_ref[k, :, S:2 * S]
            lb_r, lb_i = tab_ref[k, :, 2 * S:3 * S], tab_ref[k, :, 3 * S:4 * S]
            v_r = st_ref[pl.ds(r0, 8), re]
            v_i = st_ref[pl.ds(r0, 8), im]
            v_r, v_i = _cmul_add(v_r, v_i, la_r, la_i, pltpu.roll(v_r, nbatch, 0), pltpu.roll(v_i, nbatch, 0))
            cb_r = jnp.where(low, c_r, pltpu.roll(c_r, nbatch, 0))
            cb_i = jnp.where(low, c_i, pltpu.roll(c_i, nbatch, 0))
            v_r, v_i = _cmul_add(v_r, v_i, lb_r, lb_i, cb_r, cb_i)
            st_ref[pl.ds(r0, 8), re] = v_r
            st_ref[pl.ds(r0, 8), im] = v_i
            n_r = jnp.where(low, pltpu.roll(v_r, nbatch, 0), cb_r)
            n_i = jnp.where(low, pltpu.roll(v_i, nbatch, 0), cb_i)
            al_r = al_r + (n_r * x_r + n_i * x_i)
            al_i = al_i + (n_i * x_r - n_r * x_i)
            return v_r, v_i, al_r, al_i

        def step(j, carry):
            r0 = pl.multiple_of((ntile - 1 - j) * 16, 16)
            out = []
            for k in range(cps):
                re, im = slice(2 * S * k, 2 * S * k + S), slice(2 * S * k + S, 2 * S * (k + 1))
                x_r = xs_ref[pl.ds(r0, 16), re].astype(F32)
                x_i = xs_ref[pl.ds(r0, 16), im].astype(F32)
                mid = tile(k, r0 + 8, x_r[8:16], x_i[8:16], *carry[4 * k:4 * k + 4])
                out += list(tile(k, r0, x_r[0:8], x_i[0:8], *mid))
            return tuple(out)

        init = []
        for k in range(cps):
            init += [carry_ref[:, 2 * S * k:2 * S * k + S], carry_ref[:, 2 * S * k + S:2 * S * (k + 1)],
                     accl_ref[:, 2 * S * k:2 * S * k + S], accl_ref[:, 2 * S * k + S:2 * S * (k + 1)]]
        fin = lax.fori_loop(0, ntile, step, tuple(init))
        for k in range(cps):
            carry_ref[:, 2 * S * k:2 * S * k + S] = fin[4 * k]
            carry_ref[:, 2 * S * k + S:2 * S * (k + 1)] = fin[4 * k + 1]
            accl_ref[:, 2 * S * k:2 * S * k + S] = fin[4 * k + 2]
            accl_ref[:, 2 * S * k + S:2 * S * (k + 1)] = fin[4 * k + 3]
        dsb = st_ref[...].astype(BF16)
        ub = uv.astype(BF16)
        for k in range(cps):
            cols, lanes = slice(SSM_LANES * k, SSM_LANES * (k + 1)), slice(2 * S * k, 2 * S * (k + 1))
            du_ref[:, cols] = (lax.dot_general(dsb[:, lanes], b_ref[k], _NT, preferred_element_type=F32)
                               + d_ref[:, cols] * dyv[:, cols])
            db_ref[k] += lax.dot_general(ub[:, cols], dsb[:, lanes], _TN, preferred_element_type=F32)
            dc_ref[k] += lax.dot_general(xs_ref[:, lanes], dyb[:, cols], _TN, preferred_element_type=F32)
        accd_ref[...] += _fold8(dyv * uv)

        @pl.when(ch == nch - 1)
        def _():
            for k in range(cps):
                dl_ref[k] = jnp.sum(accl_ref[:, 2 * S * k:2 * S * (k + 1)], axis=0, keepdims=True)
            dd_ref[...] = jnp.sum(accd_ref[...], axis=0, keepdims=True)

    rev = lambda g, c: (nch - 1 - c, g)
    return pl.pallas_call(
        body, name=name, grid=(ncol // cps, nch),
        in_specs=[pl.BlockSpec((rc, cps * SSM_LANES), rev), pl.BlockSpec((rc, cps * SSM_LANES), rev),
                  pl.BlockSpec((rc, cps * 2 * S), rev),
                  pl.BlockSpec((cps, SSM_LANES, 2 * S), lambda g, c: (g, 0, 0)),
                  pl.BlockSpec((cps, 2 * S, SSM_LANES), lambda g, c: (g, 0, 0)),
                  pl.BlockSpec((1, cps * SSM_LANES), lambda g, c: (0, g)),
                  pl.BlockSpec((cps, 8, 4 * S), lambda g, c: (g, 0, 0))],
        out_specs=[pl.BlockSpec((rc, cps * SSM_LANES), rev),
                   pl.BlockSpec((cps, SSM_LANES, 2 * S), lambda g, c: (g, 0, 0)),
                   pl.BlockSpec((cps, 2 * S, SSM_LANES), lambda g, c: (g, 0, 0)),
                   pl.BlockSpec((cps, 1, 2 * S), lambda g, c: (g, 0, 0)),
                   pl.BlockSpec((1, cps * SSM_LANES), lambda g, c: (0, g))],
        out_shape=[jax.ShapeDtypeStruct((T, W), F32),
                   jax.ShapeDtypeStruct((ncol, SSM_LANES, 2 * S), F32),
                   jax.ShapeDtypeStruct((ncol, 2 * S, SSM_LANES), F32),
                   jax.ShapeDtypeStruct((ncol, 1, 2 * S), F32),
                   jax.ShapeDtypeStruct((1, W), F32)],
        scratch_shapes=[pltpu.VMEM((rc, cps * 2 * S), F32), pltpu.VMEM((8, cps * 2 * S), F32),
                        pltpu.VMEM((8, cps * 2 * S), F32), pltpu.VMEM((8, cps * SSM_LANES), F32)],
        compiler_params=_params(("parallel", "arbitrary")),
    )(dy, u, xs, bmat, cmat, dskip, tables)


def _ssm_matrices(a_re, a_im, log_step, b_re, b_im, c_re, c_im):
    G, N = a_re.shape
    ncol = G // GROUPS_PER_COL
    step = jnp.exp(log_step)[:, None]
    mag = jnp.exp(a_re * step)
    ang = a_im * step
    lam_re, lam_im = mag * jnp.cos(ang), mag * jnp.sin(ang)
    den = a_re * a_re + a_im * a_im
    nr, ni = lam_re - 1.0, lam_im
    coef_re = (nr * a_re + ni * a_im) / den
    coef_im = (ni * a_re - nr * a_im) / den
    bb_re = coef_re[..., None] * b_re - coef_im[..., None] * b_im
    bb_im = coef_re[..., None] * b_im + coef_im[..., None] * b_re
    eye = jnp.eye(GROUPS_PER_COL, dtype=F32)
    bb = jnp.stack([bb_re, bb_im]).reshape(2, ncol, GROUPS_PER_COL, N, SSM_GROUP)
    bmat = jnp.einsum("pbgnc,gh->bgcphn", bb, eye).reshape(ncol, SSM_LANES, 2 * STATE_LANES)
    cc = jnp.stack([c_re, -c_im]).reshape(2, ncol, GROUPS_PER_COL, SSM_GROUP, N)
    cmat = jnp.einsum("pbgcn,gh->bpgnhc", cc, eye).reshape(ncol, 2 * STATE_LANES, SSM_LANES)
    lam = jnp.concatenate([lam_re.reshape(ncol, STATE_LANES), lam_im.reshape(ncol, STATE_LANES)], axis=-1)
    return lam, bmat, cmat


def _scan_tables(lam, nbatch, conj):
    S = STATE_LANES
    lr, li = lam[:, None, 0:S], lam[:, None, S:2 * S]
    if conj:
        li = -li
    l2r, l2i = lr * lr - li * li, 2.0 * lr * li
    first = (jnp.arange(8) < nbatch)[None, :, None]
    zero = jnp.zeros_like(lr)
    if conj:
        parts = [jnp.where(first, lr, zero), jnp.where(first, li, zero), jnp.where(first, l2r, lr), jnp.where(first, l2i, li)]
    else:
        parts = [jnp.where(first, zero, lr), jnp.where(first, zero, li), jnp.where(first, lr, l2r), jnp.where(first, li, l2i)]
    return jnp.concatenate([jnp.broadcast_to(p, (lam.shape[0], 8, S)) for p in parts], axis=-1)


def _adamw_update(w_ref, g_ref, m_ref, v_ref, d_ref, nm_ref, nv_ref):
    gv = g_ref[...]
    mn = ADAM_B1 * m_ref[...] + (1.0 - ADAM_B1) * gv
    vn = ADAM_B2 * v_ref[...] + (1.0 - ADAM_B2) * (gv * gv)
    m_hat = mn / (1.0 - ADAM_B1 ** ADAM_STEP)
    v_hat = vn / (1.0 - ADAM_B2 ** ADAM_STEP)
    d_ref[...] = -ADAM_LR * (m_hat / (jnp.sqrt(v_hat) + ADAM_EPS) + ADAM_WD * w_ref[...])
    nm_ref[...] = mn
    nv_ref[...] = vn


def _adamw_small(ws, gs, ms, vs, name):
    n = len(ws)

    def body(*refs):
        for i in range(n):
            _adamw_update(refs[i], refs[n + i], refs[2 * n + i], refs[3 * n + i],
                          refs[4 * n + i], refs[5 * n + i], refs[6 * n + i])

    vm = pl.BlockSpec(memory_space=pltpu.VMEM)
    shapes = [jax.ShapeDtypeStruct(a.shape, F32) for a in ws]
    outs = pl.pallas_call(body, name=name, in_specs=[vm] * (4 * n), out_specs=[vm] * (3 * n), out_shape=shapes * 3,
                          compiler_params=pltpu.CompilerParams(vmem_limit_bytes=VMEM_LIMIT))(*ws, *gs, *ms, *vs)
    return outs[:n], outs[n:2 * n], outs[2 * n:]


def _adamw(w, g, m, v, name):
    R, C = w.shape[-2], w.shape[-1]
    tr = R if R <= 512 else _pick_tile(R, 512, 8)
    body = functools.partial(_adamw_update)

    def spec_for(a):
        if len(a.shape) == 2:
            return pl.BlockSpec((tr, C), lambda i: (i, 0))
        return pl.BlockSpec((None, tr, C), lambda i: (0, i, 0))

    spec = spec_for(w)
    shp = jax.ShapeDtypeStruct(w.shape, F32)
    return pl.pallas_call(body, name=name, grid=(R // tr,), in_specs=[spec, spec_for(g), spec, spec], out_specs=[spec] * 3,
                          out_shape=[shp, shp, shp], compiler_params=_params(("parallel",)))(w, g, m, v)


_ANY = pl.BlockSpec(memory_space=pl.ANY)


def _place():
    x, y, c = lax.axis_index("x"), lax.axis_index("y"), lax.axis_index("c")
    chips = [(1 - x, y), (x, 1 - y), (1 - x, 1 - y)]
    return x, y, c, chips


def _remote(src, dst, send_sems, recv_sems, k, to):
    return pltpu.make_async_remote_copy(src_ref=src, dst_ref=dst, send_sem=send_sems.at[k], recv_sem=recv_sems.at[k],
                                        device_id=to, device_id_type=MESH_IDS)


class _Riders:
    def __init__(self, srcs, out_shapes, n_sems, copies):
        self.srcs, self.out_shapes, self.n_sems, self.copies = list(srcs), list(out_shapes), n_sems, copies


def _call(body, name, grid, in_specs, out_specs, out_shape, scratch_shapes, sem, args, riders=None):
    if riders is None:
        return pl.pallas_call(body, name=name, grid=grid, in_specs=in_specs, out_specs=out_specs, out_shape=out_shape,
                              scratch_shapes=scratch_shapes, compiler_params=_params(sem))(*args)
    n_in, n_out, n_scr = len(in_specs), len(out_specs), len(scratch_shapes)
    r_in, r_out = len(riders.srcs), len(riders.out_shapes)

    def carrying(*refs):
        a, b = n_in, n_in + r_in
        c, d = b + n_out, b + n_out + r_out
        e = d + n_scr
        sends, arrivals = riders.copies(refs[a:b], refs[c:d], refs[e], refs[e + 1])
        first, last = None, None
        for ax, size in enumerate(grid):
            at0, at1 = pl.program_id(ax) == 0, pl.program_id(ax) == size - 1
            first = at0 if first is None else first & at0
            last = at1 if last is None else last & at1

        @pl.when(first)
        def _():
            for cp in sends:
                cp.start()

        body(*refs[:a], *refs[b:c], *refs[d:e])

        @pl.when(last)
        def _():
            for cp in arrivals:
                cp.wait_recv()
            for cp in sends:
                cp.wait_send()

    outs = pl.pallas_call(
        carrying, name=name, grid=grid, in_specs=list(in_specs) + [_ANY] * r_in,
        out_specs=list(out_specs) + [_ANY] * r_out, out_shape=list(out_shape) + riders.out_shapes,
        scratch_shapes=list(scratch_shapes) + [pltpu.SemaphoreType.DMA((riders.n_sems,)),
                                               pltpu.SemaphoreType.DMA((riders.n_sems,))],
        compiler_params=pltpu.CompilerParams(dimension_semantics=("arbitrary",) * len(grid),
                                             vmem_limit_bytes=VMEM_LIMIT, has_side_effects=True),
    )(*args, *riders.srcs)
    return outs[:n_out], outs[n_out:]


def _gather_riders(shards):
    def copies(srcs, outs, send_sems, recv_sems):
        x, y, c, chips = _place()
        sends, arrivals = [], []
        for i, s in enumerate(shards):
            half = s.shape[0] // 2
            rows = pl.ds(c * half, half)
            for j, chip in enumerate(chips):
                sends.append(_remote(srcs[i].at[rows, :], outs[i].at[2 * x + y, rows, :], send_sems, recv_sems,
                                     3 * i + j, (*chip, c)))
                landed = outs[i].at[2 * chip[0] + chip[1], rows, :]
                arrivals.append(_remote(landed, landed, send_sems, recv_sems, 3 * i + j, (*chip, c)))
        return sends, arrivals

    return _Riders(shards, [jax.ShapeDtypeStruct((N_CHIPS,) + s.shape, s.dtype) for s in shards], 3 * len(shards), copies)


def _exchange_riders(parts):
    def copies(srcs, outs, send_sems, recv_sems):
        x, y, c, chips = _place()
        sends = [_remote(srcs[i].at[2 * chip[0] + chip[1]], outs[i].at[j], send_sems, recv_sems, 3 * i + j, (*chip, c))
                 for i in range(len(parts)) for j, chip in enumerate(chips)]
        return sends, sends

    return _Riders(parts, [jax.ShapeDtypeStruct((3,) + p.shape[1:], p.dtype) for p in parts], 3 * len(parts), copies)


def _swap_riders(grads):
    def copies(srcs, outs, send_sems, recv_sems):
        x, y, c, _ = _place()
        sends = []
        for i, g in enumerate(grads):
            half = g.shape[1] // 2
            sends.append(_remote(srcs[i].at[:, pl.ds((1 - c) * half, half), :], outs[i], send_sems, recv_sems, i,
                                 (x, y, 1 - c)))
        return sends, sends

    return _Riders(grads, [jax.ShapeDtypeStruct((N_CHIPS, g.shape[1] // 2, g.shape[2]), g.dtype) for g in grads],
                   len(grads), copies)


def _forward_halves(gathered, shards, tag):
    n = len(gathered)

    def body(*refs):
        srcs, outs = refs[:n], refs[n:2 * n]
        send_sems, recv_sems = refs[2 * n:]
        x, y, c, chips = _place()
        sibling = (x, y, 1 - c)
        cps = []
        for i in range(n):
            half = gathered[i].shape[1] // 2
            for j, chip in enumerate(chips):
                slot = 2 * chip[0] + chip[1]
                cps.append(_remote(srcs[i].at[slot, pl.ds(c * half, half), :], outs[i].at[slot, pl.ds(c * half, half), :],
                                   send_sems, recv_sems, 3 * i + j, sibling))
        for cp in cps:
            cp.start()
        for i in range(n):
            half = gathered[i].shape[1] // 2
            for j, chip in enumerate(chips):
                theirs = outs[i].at[2 * chip[0] + chip[1], pl.ds((1 - c) * half, half), :]
                _remote(theirs, theirs, send_sems, recv_sems, 3 * i + j, sibling).wait_recv()
        for cp in cps:
            cp.wait_send()

    outs = pl.pallas_call(
        body, name=f"gather_forward_{tag}", in_specs=[_ANY] * n, out_specs=[_ANY] * n,
        out_shape=[jax.ShapeDtypeStruct(g.shape, g.dtype) for g in gathered],
        input_output_aliases={i: i for i in range(n)},
        scratch_shapes=[pltpu.SemaphoreType.DMA((3 * n,)), pltpu.SemaphoreType.DMA((3 * n,))],
        compiler_params=pltpu.CompilerParams(has_side_effects=True),
    )(*gathered)
    slot = 2 * lax.axis_index("x") + lax.axis_index("y")
    return [lax.dynamic_update_slice(o, s[None], (slot, 0, 0)) for o, s in zip(outs, shards)]


def _gather_weights(shards):
    n = len(shards)

    def body(*refs):
        srcs, outs = refs[:n], refs[n:2 * n]
        send_sems, recv_sems = refs[2 * n:]
        x, y, c, chips = _place()
        sibling = (x, y, 1 - c)

        def piece(i, px, py, pc):
            half = shards[i].shape[0] // 2
            return outs[i].at[2 * px + py, pl.ds(pc * half, half), :]

        first = []
        for i in range(n):
            half = shards[i].shape[0] // 2
            for j, chip in enumerate(chips):
                first.append(_remote(srcs[i].at[pl.ds(c * half, half), :], piece(i, x, y, c), send_sems, recv_sems,
                                     6 * i + j, (*chip, c)))
        for cp in first:
            cp.start()
        passed = []
        for i in range(n):
            for j, chip in enumerate(chips):
                _remote(piece(i, *chip, c), piece(i, *chip, c), send_sems, recv_sems, 6 * i + j, (*chip, c)).wait_recv()
                cp = _remote(piece(i, *chip, c), piece(i, *chip, c), send_sems, recv_sems, 6 * i + 3 + j, sibling)
                cp.start()
                passed.append(cp)
        for i in range(n):
            for j, chip in enumerate(chips):
                _remote(piece(i, *chip, 1 - c), piece(i, *chip, 1 - c), send_sems, recv_sems, 6 * i + 3 + j,
                        sibling).wait_recv()
        for cp in first + passed:
            cp.wait_send()

    outs = pl.pallas_call(
        body, name="gather_weights", in_specs=[_ANY] * n, out_specs=[_ANY] * n,
        out_shape=[jax.ShapeDtypeStruct((N_CHIPS,) + s.shape, s.dtype) for s in shards],
        scratch_shapes=[pltpu.SemaphoreType.DMA((6 * n,)), pltpu.SemaphoreType.DMA((6 * n,))],
        compiler_params=pltpu.CompilerParams(has_side_effects=True),
    )(*shards)
    slot = 2 * lax.axis_index("x") + lax.axis_index("y")
    return [lax.dynamic_update_slice(o, s[None], (slot, 0, 0)) for o, s in zip(outs, shards)]


def _swap_halves(grads, tag):
    n = len(grads)

    def body(*refs):
        srcs, outs = refs[:n], refs[n:2 * n]
        send_sems, recv_sems = refs[2 * n:]
        x, y, c, _ = _place()
        cps = []
        for i in range(n):
            half = grads[i].shape[1] // 2
            cps.append(_remote(srcs[i].at[:, pl.ds((1 - c) * half, half), :], outs[i], send_sems, recv_sems, i, (x, y, 1 - c)))
        for cp in cps:
            cp.start()
        for cp in cps:
            cp.wait()

    return pl.pallas_call(
        body, name=f"grad_swap_halves_{tag}", in_specs=[_ANY] * n, out_specs=[_ANY] * n,
        out_shape=[jax.ShapeDtypeStruct((N_CHIPS, g.shape[1] // 2, g.shape[2]), g.dtype) for g in grads],
        scratch_shapes=[pltpu.SemaphoreType.DMA((n,)), pltpu.SemaphoreType.DMA((n,))],
        compiler_params=pltpu.CompilerParams(has_side_effects=True),
    )(*grads)


def _join_halves(fulls):
    n = len(fulls)

    def body(*refs):
        srcs, outs = refs[:n], refs[n:2 * n]
        send_sems, recv_sems = refs[2 * n:]
        x, y, c, _ = _place()
        sibling = (x, y, 1 - c)
        cps = []
        for i in range(n):
            h = fulls[i].shape[0] // 2
            cps.append(_remote(srcs[i].at[pl.ds(c * h, h), :], outs[i].at[pl.ds(c * h, h), :], send_sems, recv_sems, i,
                               sibling))
        for cp in cps:
            cp.start()
        for i in range(n):
            h = fulls[i].shape[0] // 2
            theirs = outs[i].at[pl.ds((1 - c) * h, h), :]
            _remote(theirs, theirs, send_sems, recv_sems, i, sibling).wait_recv()
        for cp in cps:
            cp.wait_send()

    return pl.pallas_call(
        body, name="grad_join_halves", in_specs=[_ANY] * n, out_specs=[_ANY] * n,
        out_shape=[jax.ShapeDtypeStruct(f.shape, f.dtype) for f in fulls],
        input_output_aliases={i: i for i in range(n)},
        scratch_shapes=[pltpu.SemaphoreType.DMA((n,)), pltpu.SemaphoreType.DMA((n,))],
        compiler_params=pltpu.CompilerParams(has_side_effects=True),
    )(*fulls)


def _half_tile(h):
    return h if h <= 512 else _pick_tile(h, 512, ROW_ALIGN)


def _sum_halves(g, r1, c_idx, name):
    _, R, C = g.shape
    H = R // 2
    tr = _half_tile(H)
    nblk = H // tr

    def body(c_ref, g_ref, r_ref, p_ref):
        p_ref[...] = (g_ref[...] + r_ref[...]).astype(BF16)

    half = pl.BlockSpec((None, tr, C), lambda s, i, c_ref: (s, c_ref[0] * nblk + i, 0))
    plain = pl.BlockSpec((None, tr, C), lambda s, i, c_ref: (s, i, 0))
    return pl.pallas_call(
        body, name=name,
        grid_spec=pltpu.PrefetchScalarGridSpec(num_scalar_prefetch=1, grid=(N_CHIPS, nblk), in_specs=[half, plain],
                                               out_specs=plain),
        out_shape=jax.ShapeDtypeStruct((N_CHIPS, H, C), BF16),
        compiler_params=_params(("parallel", "parallel")),
    )(c_idx, g, r1)


def _sum_chips(g, r1, r2, idx, name):
    _, R, C = g.shape
    H = R // 2
    tr = _half_tile(H)
    nblk = H // tr

    def body(idx_ref, g_ref, r1_ref, r2_ref, o_ref):
        o_ref[...] = (((g_ref[...] + r1_ref[...]) + r2_ref[0].astype(F32)) + r2_ref[1].astype(F32)) + r2_ref[2].astype(F32)

    return pl.pallas_call(
        body, name=name,
        grid_spec=pltpu.PrefetchScalarGridSpec(
            num_scalar_prefetch=1, grid=(nblk,),
            in_specs=[pl.BlockSpec((None, tr, C), lambda i, idx_ref: (idx_ref[0], idx_ref[1] * nblk + i, 0)),
                      pl.BlockSpec((None, tr, C), lambda i, idx_ref: (idx_ref[0], i, 0)),
                      pl.BlockSpec((3, tr, C), lambda i, idx_ref: (0, i, 0))],
            out_specs=pl.BlockSpec((tr, C), lambda i, idx_ref: (idx_ref[1] * nblk + i, 0))),
        out_shape=jax.ShapeDtypeStruct((R, C), F32),
        compiler_params=_params(("parallel",)),
    )(idx, g, r1, r2)


def _all_reduce_small(v, n_fold, fold_rows, fold_at):
    M, N = v.shape

    def body(x_ref, tot_ref, fold_ref, all_ref, send_sems, recv_sems, local_sem):
        x, y, c, chips = _place()
        me, sibling = (x, y, c), (x, y, 1 - c)

        def rows(px, py, pc):
            return all_ref.at[pl.ds((4 * px + 2 * py + pc) * M, M), :]

        def copy(k, block, to, src=None):
            return _remote(rows(*block) if src is None else src, rows(*block), send_sems, recv_sems, k, to)

        mine = pltpu.make_async_copy(x_ref, rows(*me), local_sem)
        mine.start()
        first = [copy(0, me, sibling, src=x_ref)]
        first += [copy(1 + j, me, (*chip, c), src=x_ref) for j, chip in enumerate(chips)]
        for cp in first:
            cp.start()
        passed = [copy(4 + j, (*chip, c), sibling) for j, chip in enumerate(chips)]
        for j, chip in enumerate(chips):
            copy(1 + j, (*chip, c), me).wait_recv()
            passed[j].start()
        copy(0, sibling, me).wait_recv()
        for j, chip in enumerate(chips):
            copy(4 + j, (*chip, 1 - c), me).wait_recv()
        for cp in first + passed:
            cp.wait_send()
        mine.wait()
        tot = all_ref[0:M, :]
        for d in range(1, 8):
            tot = tot + all_ref[d * M:(d + 1) * M, :]
        tot_ref[...] = tot
        f = tot[fold_at:fold_at + fold_rows, :]
        for e in range(1, n_fold):
            f = f + tot[fold_at + e * fold_rows:fold_at + (e + 1) * fold_rows, :]
        fold_ref[...] = f

    vm = pl.BlockSpec(memory_space=pltpu.VMEM)
    return pl.pallas_call(
        body, name="all_reduce_small", in_specs=[vm], out_specs=[vm, vm],
        out_shape=[jax.ShapeDtypeStruct((M, N), F32), jax.ShapeDtypeStruct((fold_rows, N), F32)],
        scratch_shapes=[pltpu.VMEM((8 * M, N), F32), pltpu.SemaphoreType.DMA((7,)), pltpu.SemaphoreType.DMA((7,)),
                        pltpu.SemaphoreType.DMA],
        compiler_params=pltpu.CompilerParams(has_side_effects=True, vmem_limit_bytes=VMEM_LIMIT),
    )(v)


def _as_rows(a, width):
    flat = a.reshape(-1)
    pad = (-flat.shape[0]) % width
    if pad:
        flat = jnp.concatenate([flat, jnp.zeros((pad,), flat.dtype)])
    return flat.reshape(-1, width)


class _Layout:
    def __init__(self, width, total_mult):
        self.width, self.total_mult = width, total_mult
        self.offsets, self.shapes, self.rows = {}, {}, 0

    def add(self, name, shape):
        r = -(-math.prod(shape) // self.width)
        self.offsets[name], self.shapes[name] = (self.rows, r), tuple(shape)
        self.rows += r

    def align(self, mult):
        gap = (-self.rows) % mult
        if gap:
            self.offsets[f"_gap{self.rows}"], self.shapes[f"_gap{self.rows}"] = (self.rows, gap), (gap, self.width)
            self.rows += gap
        return self.rows

    def pack(self, pieces):
        self.align(self.total_mult)
        parts = [_as_rows(pieces[n].astype(F32), self.width) if n in pieces else jnp.zeros(self.shapes[n], F32)
                 for n in self.offsets]
        return jnp.concatenate(parts, axis=0)

    def unpack(self, buf, name):
        off, r = self.offsets[name]
        shape = self.shapes[name]
        return buf[off:off + r].reshape(-1)[:math.prod(shape)].reshape(shape)


_BIG = ["ffn1_w1", "ffn1_w3", "ffn1_w2", "w_in", "ssm_glu_a", "ssm_glu_b", "w_out", "ffn2_w1", "ffn2_w3", "ffn2_w2"]
_TRANSPOSED = {"ffn1_w1", "ffn1_w3", "ffn2_w1", "ffn2_w3"}
_SMALL = ["ffn1_norm", "mix_norm", "ffn2_norm", "final_norm", "attn_sinks", "ssm_a_re", "ssm_a_im", "ssm_log_step",
          "ssm_b_re", "ssm_b_im", "ssm_c_re", "ssm_c_im", "ssm_d"]
_WEIGHTS = ["meta_tokens", "ffn1_norm", "ffn1_w1", "ffn1_w3", "ffn1_w2", "mix_norm", "w_in", "attn_sinks", "ssm_a_re",
            "ssm_a_im", "ssm_log_step", "ssm_b_re", "ssm_b_im", "ssm_c_re", "ssm_c_im", "ssm_d", "ssm_glu_a",
            "ssm_glu_b", "w_out", "ffn2_norm", "ffn2_w1", "ffn2_w3", "ffn2_w2", "final_norm"]


def _kv_interleave(w, kv_heads):
    kvw = kv_heads * HEAD_DIM
    lead = w.shape[:-1]
    k = w[..., 0:kvw].reshape(lead + (kv_heads, 1, HEAD_DIM))
    v = w[..., kvw:2 * kvw].reshape(lead + (kv_heads, 1, HEAD_DIM))
    return jnp.concatenate([jnp.concatenate([k, v], axis=-2).reshape(lead + (2 * kvw,)), w[..., 2 * kvw:]], axis=-1)


def _kv_deinterleave(w, kv_heads):
    kvw = kv_heads * HEAD_DIM
    lead = w.shape[:-1]
    kv = w[..., 0:2 * kvw].reshape(lead + (kv_heads, 2, HEAD_DIM))
    return jnp.concatenate([kv[..., 0, :].reshape(lead + (kvw,)), kv[..., 1, :].reshape(lead + (kvw,)), w[..., 2 * kvw:]],
                           axis=-1)


def _step(x, target, w, m, v):
    B, S, D = x.shape
    L = S + N_META
    T = B * L
    H = D // HEAD_DIM
    KV = H // Q_PER_KV
    SW = D // 2
    tm = _pick_tile(L, ROW_TILE_CAP, ROW_ALIGN)
    rc = _pick_tile(L, ROW_TILE_CAP // B, 4) * B
    tw = _pick_tile(T, 6 * ROW_TILE_CAP, ROW_ALIGN)
    tm2 = _pick_tile(T, 2 * ROW_TILE_CAP, ROW_ALIGN)
    my_c = lax.axis_index("c")
    my_slot = 2 * lax.axis_index("x") + lax.axis_index("y")

    groups = {"ffn1": ["ffn1_w1", "ffn1_w3", "ffn1_w2"], "mix": ["w_in", "ssm_glu_a", "ssm_glu_b", "w_out"],
              "ffn2": ["ffn2_w1", "ffn2_w3", "ffn2_w2"]}
    waves = {"first": ["ffn1_w1", "ffn1_w3"], "early": ["ffn1_w2"] + groups["mix"], "late": groups["ffn2"]}
    def own_layout(a, n):
        return jnp.swapaxes(a[0], 0, 1) if n in _TRANSPOSED else a[0]

    shards = {n: own_layout(w[n], n).astype(BF16) for n in _BIG}
    gathered = _gather_weights([shards[n] for n in waves["first"]] + [w["meta_tokens"]])
    ws = dict(zip(waves["first"], gathered[:-1]))
    meta = jnp.transpose(gathered[-1], (1, 0, 2)).reshape(N_META, D)

    def arrive(wave, landed):
        mine = [shards[n] for n in waves[wave]]
        ws.update(zip(waves[wave], _forward_halves(landed, mine, wave)))

    g_ffn1, g_mix, g_ffn2 = w["ffn1_norm"], w["mix_norm"], w["ffn2_norm"]
    g_final = w["final_norm"].reshape(1, D)

    h0, n_ffn1 = _embed_norm(x, meta, g_ffn1, tm, "ffn1_norm")

    def ffn_fwd(h, g, tag, carry=None, n=None):
        if n is None:
            n = _rmsnorm_fwd(h, g, tm, f"{tag}_norm")
        riders = None if carry is None else _gather_riders([shards[k] for k in waves[carry]])
        out = _ffn_up(n, ws[f"{tag}_w1"], ws[f"{tag}_w3"], tm2, f"{tag}_up", riders)
        if carry is not None:
            out, landed = out
            arrive(carry, landed)
        a, c, s = out
        return _ffn_down(s, ws[f"{tag}_w2"], h, tm, f"{tag}_down"), (n, a, c, s)

    h1, saved1 = ffn_fwd(h0, g_ffn1, "ffn1", carry="early", n=n_ffn1)
    w_kvu = _kv_interleave(ws["w_in"][1], KV)
    hn = _rmsnorm_fwd(h1, g_mix, tm, "mix_norm")
    q = _mm_colslots(hn, ws["w_in"], BF16, "w_in_q", tm, first=0, count=1, scale=HEAD_DIM ** -0.5)
    kvu = _mm_plain(hn, w_kvu, "nn", F32, "w_in_kvu", tm)
    gates = _mm_colslots(hn, ws["w_in"], F32, "w_in_gates", tm, first=2, count=2)

    sinks = w["attn_sinks"].reshape(KV, Q_PER_KV, 1, 1)
    sink_row = jnp.broadcast_to(sinks.reshape(KV, 1, Q_PER_KV, 1), (KV, 1, Q_PER_KV, BLOCK)).reshape(KV, 1, Q_PER_KV * BLOCK)
    sink_meta = jnp.broadcast_to(sinks, (KV, Q_PER_KV, N_META, 1)).reshape(KV, Q_PER_KV * N_META, 1)
    (attn,), landed = _attn_fwd(q, kvu, sink_row, sink_meta, B, "attn_fwd",
                                _gather_riders([shards[k] for k in waves["late"]]))
    arrive("late", landed)

    def to_time_major(a2d):
        return jnp.transpose(a2d.reshape(B, L, a2d.shape[-1]), (1, 0, 2)).reshape(T, a2d.shape[-1])

    def to_batch_major(a2d):
        return jnp.transpose(a2d.reshape(L, B, a2d.shape[-1]), (1, 0, 2)).reshape(T, a2d.shape[-1])

    ssm_args = (w["ssm_a_re"][0], w["ssm_a_im"][0], w["ssm_log_step"][0], w["ssm_b_re"][0], w["ssm_b_im"][0],
                w["ssm_c_re"][0], w["ssm_c_im"][0])
    (lam, bmat, cmat), ssm_vjp = jax.vjp(_ssm_matrices, *ssm_args)
    bmat16, cmat16 = bmat.astype(BF16), cmat.astype(BF16)
    u_t = to_time_major(kvu[:, SW:])
    y_t, xs = _ssm_fwd(u_t, bmat16, cmat16, w["ssm_d"], _scan_tables(lam, B, False), B, rc, "ssm_fwd")
    y0 = to_batch_major(y_t)
    yg = _gelu_fwd(y0, tm, "gelu_fwd")
    ga = _mm_colslots(yg, ws["ssm_glu_a"], F32, "glu_a", tm)
    gb = _mm_colslots(yg, ws["ssm_glu_b"], F32, "glu_b", tm)
    merged = _merge_fwd(gates, attn, ga, gb, tm, "merge_fwd")
    h2 = _mm_rowslots(merged, ws["w_out"], h1, tm, "w_out")
    h3, saved2 = ffn_fwd(h2, g_ffn2, "ffn2")
    dh3, dh3b, dg_final, loss_row = _loss_head(h3, g_final, target, tm, "loss_head")

    grads, swapped, received = {}, {}, {}
    c_idx = my_c.reshape(1).astype(jnp.int32)
    idx = jnp.stack([my_slot, my_c]).astype(jnp.int32)

    def swap_riders(group):
        return _swap_riders([grads[n] for n in groups[group]])

    def exchange_riders(group):
        names = groups[group]
        if names[0] not in swapped:
            swapped.update(zip(names, _swap_halves([grads[n] for n in names], group)))
        return _exchange_riders([_sum_halves(grads[n], swapped[n], c_idx, f"grad_sum_halves_{n}") for n in names])

    def ffn_bwd(h, g, saved, dh, dhb, tag, dhidden_carries=None, dn_carries=None, last=False):
        n, a, c, s = saved
        w1, w3, w2 = ws[f"{tag}_w1"], ws[f"{tag}_w3"], ws[f"{tag}_w2"]
        grads[f"{tag}_w2"] = _wgrad_hidden_rows(s, dhb, tw, f"{tag}_dw2", 0.5)
        if dhidden_carries is None:
            da, dc = _ffn_dhidden(dhb, w2, a, c, tm2, f"{tag}_dhidden")
        else:
            (da, dc), got = _ffn_dhidden(dhb, w2, a, c, tm2, f"{tag}_dhidden", exchange_riders(dhidden_carries[1]))
            received.update(zip(groups[dhidden_carries[1]], got))
        grads[f"{tag}_w1"] = _wgrad_hidden_rows(da, n, tw, f"{tag}_dw1", 1.0)
        grads[f"{tag}_w3"] = _wgrad_hidden_rows(dc, n, tw, f"{tag}_dw3", 1.0)
        kind, group = dn_carries
        riders = swap_riders(group) if kind == "swap" else exchange_riders(group)
        (dh_in, dhb_in, grads[f"{tag}_norm"]), got = _ffn_dn(da, w1, dc, w3, h, g, dh, tm, f"{tag}_dn", riders,
                                                               B if last else None)
        return dh_in, dhb_in, got

    dh2, dh2b, got = ffn_bwd(h2, g_ffn2, saved2, dh3, dh3b, "ffn2", dn_carries=("swap", "ffn2"))
    swapped.update(zip(groups["ffn2"], got))

    grads["w_out"] = _wgrad_rowslots(merged, dh2b, tw, "dw_out")
    dattn, dgat, dgss, dga, dgb = _merge_bwd(dh2b, ws["w_out"], gates, attn, ga, gb, tm, "merge_bwd")
    grads["ssm_glu_a"] = _wgrad_colslots(yg, dga, tw, "dglu_a")
    grads["ssm_glu_b"] = _wgrad_colslots(yg, dgb, tw, "dglu_b")
    dy0 = _gelu_bwd([(dga, ws["ssm_glu_a"]), (dgb, ws["ssm_glu_b"])], y0, tm, "gelu_bwd")
    du_t, dbmat, dcmat, dlam, dd = _ssm_bwd(to_time_major(dy0), u_t, xs, bmat16, cmat16, w["ssm_d"],
                                            _scan_tables(lam, B, True), B, rc, "ssm_bwd")
    d_ssm = ssm_vjp((dlam[:, 0, :], dbmat, dcmat))
    for n, gval in zip(["ssm_a_re", "ssm_a_im", "ssm_log_step", "ssm_b_re", "ssm_b_im", "ssm_c_re", "ssm_c_im"], d_ssm):
        grads[n] = gval[None]
    grads["ssm_d"] = dd

    (dq, dkv, dsink), got = _attn_bwd(q, kvu, attn, dattn, sink_row, sink_meta, B, "attn_bwd",
                                      exchange_riders("ffn2"))
    received.update(zip(groups["ffn2"], got))
    grads["attn_sinks"] = dsink[:, 0:Q_PER_KV, 0].reshape(1, H)
    dkvu = jnp.concatenate([dkv, to_batch_major(du_t).astype(BF16)], axis=1)
    pieces = [dq, dkvu, dgat, dgss]
    dw_in = [_wgrad_plain(hn, p, f"dw_in_{k}", tw) for k, p in enumerate(pieces)]
    dw_in[1] = _kv_deinterleave(dw_in[1], KV)
    grads["w_in"] = jnp.stack(dw_in)
    w_in_parts = [ws["w_in"][0], w_kvu, ws["w_in"][2], ws["w_in"][3]]
    whole = _once((D, D), lambda i: (0, 0))
    (dh1, dh1b, grads["mix_norm"]), swap_mix = _mm_norm_bwd(
        "dhn", "nt", [(p, _spec((tm, D), lambda i: (i, 0)), wp, whole) for p, wp in zip(pieces, w_in_parts)],
        h1, g_mix, dh2, tm, swap_riders("mix"))
    swapped.update(zip(groups["mix"], swap_mix))
    grad_x, dmeta_rows, got = ffn_bwd(h0, g_ffn1, saved1, dh1, dh1b, "ffn1", dhidden_carries=("exchange", "mix"),
                                      dn_carries=("exchange", "ffn1"), last=True)
    received.update(zip(groups["ffn1"], got))

    grads["final_norm"] = dg_final
    slay = _Layout(D, 8)
    for n in _SMALL:
        slay.add(n, w[n].shape)
    slay.add("loss", (1, D))
    meta_at = slay.align(8)
    slay.add("meta", (N_META, D))
    small = slay.pack({**{n: grads[n] for n in _SMALL}, "loss": loss_row, "meta": dmeta_rows})
    tot_small, dmeta = _all_reduce_small(small, 1, N_META, meta_at)
    loss = slay.unpack(tot_small, "loss")[0, 0]
    for n in _SMALL:
        grads[n] = slay.unpack(tot_small, n)
    cw = D // N_CHIPS
    grads["meta_tokens"] = lax.dynamic_slice_in_dim(dmeta, my_slot * cw, cw, axis=1)

    fulls = [_sum_chips(grads[n], swapped[n], received[n], idx, f"grad_sum_chips_{n}") for n in _BIG]
    for n, f in zip(_BIG, _join_halves(fulls)):
        grads[n] = f

    delta, new_m, new_v = {}, {}, {}
    for n in _BIG + ["meta_tokens"]:
        if n in _TRANSPOSED:
            flip = lambda a: jnp.swapaxes(a, -1, -2)
            outs = _adamw(flip(w[n]), grads[n], flip(m[n]), flip(v[n]), f"adamw_{n}")
            delta[n], new_m[n], new_v[n] = (flip(o) for o in outs)
            grads[n] = flip(grads[n])[None]
        else:
            delta[n], new_m[n], new_v[n] = _adamw(w[n], grads[n], m[n], v[n], f"adamw_{n}")
            grads[n] = grads[n].reshape(w[n].shape)

    def flat2d(a):
        return a.reshape(-1, a.shape[-1])

    d_, m_, v_ = _adamw_small([flat2d(w[n]) for n in _SMALL], [flat2d(grads[n]) for n in _SMALL],
                              [flat2d(m[n]) for n in _SMALL], [flat2d(v[n]) for n in _SMALL], "adamw_small")
    for i, n in enumerate(_SMALL):
        shp = w[n].shape
        delta[n], new_m[n], new_v[n] = d_[i].reshape(shp), m_[i].reshape(shp), v_[i].reshape(shp)
        grads[n] = grads[n].reshape(shp)

    return (loss, grad_x, *[grads[n] for n in _WEIGHTS], *[delta[n] for n in _WEIGHTS],
            *[new_m[n] for n in _WEIGHTS], *[new_v[n] for n in _WEIGHTS])


def kernel(x, meta_tokens, ffn1_norm, ffn1_w1, ffn1_w3, ffn1_w2, mix_norm, w_in, attn_sinks, ssm_a_re, ssm_a_im, ssm_log_step, ssm_b_re, ssm_b_im, ssm_c_re, ssm_c_im, ssm_d, ssm_glu_a, ssm_glu_b, w_out, ffn2_norm, ffn2_w1, ffn2_w3, ffn2_w2, final_norm, loss_target, m_meta_tokens, m_ffn1_norm, m_ffn1_w1, m_ffn1_w3, m_ffn1_w2, m_mix_norm, m_w_in, m_attn_sinks, m_ssm_a_re, m_ssm_a_im, m_ssm_log_step, m_ssm_b_re, m_ssm_b_im, m_ssm_c_re, m_ssm_c_im, m_ssm_d, m_ssm_glu_a, m_ssm_glu_b, m_w_out, m_ffn2_norm, m_ffn2_w1, m_ffn2_w3, m_ffn2_w2, m_final_norm, v_meta_tokens, v_ffn1_norm, v_ffn1_w1, v_ffn1_w3, v_ffn1_w2, v_mix_norm, v_w_in, v_attn_sinks, v_ssm_a_re, v_ssm_a_im, v_ssm_log_step, v_ssm_b_re, v_ssm_b_im, v_ssm_c_re, v_ssm_c_im, v_ssm_d, v_ssm_glu_a, v_ssm_glu_b, v_w_out, v_ffn2_norm, v_ffn2_w1, v_ffn2_w3, v_ffn2_w2, v_final_norm):
    args = locals()
    w = {n: args[n] for n in _WEIGHTS}
    m = {n: args["m_" + n] for n in _WEIGHTS}
    v = {n: args["v_" + n] for n in _WEIGHTS}
    return _step(x, loss_target, w, m, v)
```

```python
import functools
import math

import jax
import jax.numpy as jnp
from jax import lax
from jax.experimental import pallas as pl
from jax.experimental.pallas import tpu as pltpu

F32 = jnp.float32
BF16 = jnp.bfloat16
MESH_IDS = pl.DeviceIdType.MESH

N_CHIPS = 4
N_META = 16
HEAD_DIM = 64
Q_PER_KV = 4
QW = Q_PER_KV * HEAD_DIM
BLOCK = 128
SSM_GROUP = 16
SSM_STATE = 64
SSM_LANES = 128
GROUPS_PER_COL = SSM_LANES // SSM_GROUP
STATE_LANES = GROUPS_PER_COL * SSM_STATE
NORM_EPS = 1e-6
NEG_INF = -1e30
ADAM_LR, ADAM_B1, ADAM_B2, ADAM_EPS, ADAM_WD, ADAM_STEP = 0.001, 0.9, 0.999, 1e-08, 0.01, 10
GELU_C = math.sqrt(2.0 / math.pi)
ROW_ALIGN = 16
VMEM_LIMIT = 56 * 1024 * 1024
ROW_TILE_CAP = 688

_NN = (((1,), (0,)), ((), ()))
_NT = (((1,), (1,)), ((), ()))
_TN = (((0,), (0,)), ((), ()))
_DIMS = {"nn": _NN, "nt": _NT, "tn": _TN}


def _params(sem, **kw):
    return pltpu.CompilerParams(dimension_semantics=sem, vmem_limit_bytes=VMEM_LIMIT, **kw)


def _pick_tile(n, cap, mult):
    best = None
    for t in range(mult, min(n, cap) + 1, mult):
        if n % t == 0:
            best = t
    if best is None:
        raise ValueError(f"no tile for {n} (cap {cap}, multiple of {mult})")
    return best


def _sigmoid(x):
    return 0.5 * jnp.tanh(0.5 * x) + 0.5


def _spec(block, index_map):
    return pl.BlockSpec(block, index_map)


def _sum_dots(ins, mode):
    tot = None
    for p in range(len(ins) // 2):
        a_ref, b_ref = ins[2 * p], ins[2 * p + 1]
        for sl in ([None] if len(b_ref.shape) == 2 else range(b_ref.shape[0])):
            if sl is None:
                a, b = a_ref[...], b_ref[...]
            elif len(a_ref.shape) == 3:
                a, b = a_ref[sl], b_ref[sl]
            else:
                width = a_ref.shape[1] // b_ref.shape[0]
                a, b = a_ref[:, sl * width:(sl + 1) * width], b_ref[sl]
            d = lax.dot_general(a.astype(BF16), b.astype(BF16), _DIMS[mode], preferred_element_type=F32)
            tot = d if tot is None else tot + d
    return tot


def _mm(name, grid, kaxis, mode, pairs, out_shape, out_spec, scale=1.0, res=None):
    npairs = len(pairs)
    has_res = res is not None
    gk = 1 if kaxis is None else grid[kaxis]
    acc_shape = tuple(d for d in out_spec.block_shape if d is not None)

    def body(*refs):
        res_ref = refs[2 * npairs] if has_res else None
        o_ref = refs[2 * npairs + has_res]
        tot = _sum_dots(refs[:2 * npairs], mode)

        def finish(acc):
            r = acc * scale if scale != 1.0 else acc
            if has_res:
                r = res_ref[...] + r
            o_ref[...] = r.astype(o_ref.dtype)

        if gk == 1:
            finish(tot)
        else:
            acc_ref = refs[-1]
            k = pl.program_id(kaxis)

            @pl.when(k == 0)
            def _():
                acc_ref[...] = tot

            @pl.when(k > 0)
            def _():
                acc_ref[...] += tot

            @pl.when(k == gk - 1)
            def _():
                finish(acc_ref[...])

    in_specs, args = [], []
    for a, a_spec, b, b_spec in pairs:
        in_specs += [a_spec, b_spec]
        args += [a, b]
    if has_res:
        in_specs.append(res[1])
        args.append(res[0])
    sem = tuple("arbitrary" if ax == kaxis else "parallel" for ax in range(len(grid)))
    return pl.pallas_call(
        body, name=name, grid=grid, in_specs=in_specs, out_specs=out_spec, out_shape=out_shape,
        scratch_shapes=[pltpu.VMEM(acc_shape, F32)] if gk > 1 else [],
        compiler_params=_params(sem),
    )(*args)


def _mm_plain(a, b, mode, out_dtype, name, tm, scale=1.0):
    M, K = a.shape
    N = b.shape[1] if mode == "nn" else b.shape[0]
    return _mm(name, (M // tm,), None, mode,
               [(a, _spec((tm, K), lambda i: (i, 0)), b, _spec(b.shape, lambda i: (0, 0)))],
               jax.ShapeDtypeStruct((M, N), out_dtype), _spec((tm, N), lambda i: (i, 0)), scale=scale)


def _wgrad_plain(a, b, name, tr):
    R, M = a.shape
    N = b.shape[1]
    return _mm(name, (R // tr,), 0, "tn",
               [(a, _spec((tr, M), lambda r: (r, 0)), b, _spec((tr, N), lambda r: (r, 0)))],
               jax.ShapeDtypeStruct((M, N), F32), _spec((M, N), lambda r: (0, 0)))


def _for_real_rows(tile, tpe, tm, hbm, buf, fn):
    tb, tj = tile // tpe, tile % tpe

    @pl.when(tj == 0)
    def _():
        fn(hbm.at[tb, pl.ds(0, tm - N_META), :], buf.at[pl.ds(N_META, tm - N_META), :])

    @pl.when(tj > 0)
    def _():
        fn(hbm.at[tb, pl.ds(tj * tm - N_META, tm), :], buf)


def _embed_norm(x, meta, g, tm, name):
    B, S, D = x.shape
    L = S + N_META
    T = B * L
    nt = T // tm
    tpe = L // tm

    def body(x_hbm, meta_ref, g_ref, h_ref, n_ref, xbuf, sems):
        i = pl.program_id(0)
        slot = i % 2

        def fetch(tile, sl, act):
            _for_real_rows(tile, tpe, tm, x_hbm, xbuf.at[sl], lambda src, dst: act(pltpu.make_async_copy(src, dst, sems.at[sl])))

        @pl.when(i == 0)
        def _():
            fetch(i, slot, lambda cp: cp.start())

        @pl.when(i + 1 < nt)
        def _():
            fetch(i + 1, 1 - slot, lambda cp: cp.start())

        fetch(i, slot, lambda cp: cp.wait())

        @pl.when(i % tpe == 0)
        def _():
            xbuf[slot, 0:N_META, :] = meta_ref[...]

        hv = xbuf[slot]
        h_ref[...] = hv
        r = lax.rsqrt(jnp.mean(hv * hv, axis=-1, keepdims=True) + NORM_EPS)
        n_ref[...] = ((hv * r) * g_ref[...]).astype(BF16)

    row = pl.BlockSpec((tm, D), lambda i: (i, 0))
    return pl.pallas_call(
        body, name=name, grid=(nt,),
        in_specs=[pl.BlockSpec(memory_space=pl.ANY), pl.BlockSpec((N_META, D), lambda i: (0, 0)),
                  pl.BlockSpec((1, D), lambda i: (0, 0))],
        out_specs=[row, row],
        out_shape=[jax.ShapeDtypeStruct((T, D), F32), jax.ShapeDtypeStruct((T, D), BF16)],
        scratch_shapes=[pltpu.VMEM((2, tm, D), F32), pltpu.SemaphoreType.DMA((2,))],
        compiler_params=_params(("arbitrary",)),
    )(x, meta, g)


def _rmsnorm_fwd(h, g, tm, name):
    T, D = h.shape

    def body(h_ref, g_ref, o_ref):
        x = h_ref[...]
        r = lax.rsqrt(jnp.mean(x * x, axis=-1, keepdims=True) + NORM_EPS)
        o_ref[...] = ((x * r) * g_ref[...]).astype(BF16)

    return pl.pallas_call(
        body, name=name, grid=(T // tm,),
        in_specs=[pl.BlockSpec((tm, D), lambda i: (i, 0)), pl.BlockSpec((1, D), lambda i: (0, 0))],
        out_specs=pl.BlockSpec((tm, D), lambda i: (i, 0)),
        out_shape=jax.ShapeDtypeStruct((T, D), BF16),
        compiler_params=_params(("parallel",)),
    )(h, g)


def _fold8(x):
    return jnp.sum(x.reshape(x.shape[0] // 8, 8, x.shape[1]), axis=0)


def _mm_norm_bwd(name, mode, pairs, h, g, dres, tm, riders=None, examples=None):
    T, D = h.shape
    nt = T // tm
    npairs = len(pairs)
    tpe = None if examples is None else T // examples // tm

    def body(*refs):
        if examples is None:
            h_ref, g_ref, dres_ref, dh_ref, dhb_ref, dg_ref, acc_ref = refs[2 * npairs:]
        else:
            h_ref, g_ref, dres_ref, dx_hbm, dmeta_ref, dg_ref, acc_ref, dbuf, sems = refs[2 * npairs:]
        i = pl.program_id(0)
        x = h_ref[...]
        r = lax.rsqrt(jnp.mean(x * x, axis=-1, keepdims=True) + NORM_EPS)
        xhat = x * r
        dy = _sum_dots(refs[:2 * npairs], mode)
        dxhat = dy * g_ref[...]
        dx = r * (dxhat - xhat * jnp.mean(dxhat * xhat, axis=-1, keepdims=True))
        dh = dres_ref[...] + dx
        if examples is None:
            dh_ref[...] = dh
            dhb_ref[...] = dh.astype(BF16)
        else:
            slot = i % 2

            def push(tile, sl, act):
                _for_real_rows(tile, tpe, tm, dx_hbm, dbuf.at[sl],
                               lambda dst, src: act(pltpu.make_async_copy(src, dst, sems.at[sl])))

            dbuf[slot] = dh
            push(i, slot, lambda cp: cp.start())

            @pl.when(i > 0)
            def _():
                push(i - 1, 1 - slot, lambda cp: cp.wait())

            @pl.when(i == nt - 1)
            def _():
                push(i, slot, lambda cp: cp.wait())

            @pl.when(i == 0)
            def _():
                dmeta_ref[...] = dh[0:N_META]

            @pl.when((i > 0) & (i % tpe == 0))
            def _():
                dmeta_ref[...] += dh[0:N_META]
        part = _fold8(dy * xhat)

        @pl.when(i == 0)
        def _():
            acc_ref[...] = part

        @pl.when(i > 0)
        def _():
            acc_ref[...] += part

        @pl.when(i == nt - 1)
        def _():
            dg_ref[...] = jnp.sum(acc_ref[...], axis=0, keepdims=True)

    row = pl.BlockSpec((tm, D), lambda i: (i, 0))
    vec = pl.BlockSpec((1, D), lambda i: (0, 0))
    in_specs, args = [], []
    for a, a_spec, b, b_spec in pairs:
        in_specs += [a_spec, b_spec]
        args += [a, b]
    if examples is None:
        return _call(body, name, (nt,), in_specs + [row, vec, row], [row, row, vec],
                     [jax.ShapeDtypeStruct((T, D), F32), jax.ShapeDtypeStruct((T, D), BF16), jax.ShapeDtypeStruct((1, D), F32)],
                     [pltpu.VMEM((8, D), F32)], ("arbitrary",), (*args, h, g, dres), riders)
    S = T // examples - N_META
    return _call(body, name, (nt,), in_specs + [row, vec, row],
                 [_ANY, pl.BlockSpec((N_META, D), lambda i: (0, 0)), vec],
                 [jax.ShapeDtypeStruct((examples, S, D), F32), jax.ShapeDtypeStruct((N_META, D), F32),
                  jax.ShapeDtypeStruct((1, D), F32)],
                 [pltpu.VMEM((8, D), F32), pltpu.VMEM((2, tm, D), F32), pltpu.SemaphoreType.DMA((2,))],
                 ("arbitrary",), (*args, h, g, dres), riders)


def _ffn_up(n, w1t, w3t, tm, name, riders=None):
    T, D = n.shape
    Fs = w1t.shape[1]

    def body(n_ref, w1_ref, w3_ref, a_ref, c_ref, s_ref):
        x = n_ref[...]
        a = lax.dot_general(x, w1_ref[...], _NT, preferred_element_type=F32)
        c = lax.dot_general(x, w3_ref[...], _NT, preferred_element_type=F32)
        a_ref[...] = a.astype(BF16)
        c_ref[...] = c.astype(BF16)
        s_ref[...] = (a * _sigmoid(a) * c).astype(BF16)

    w_spec = _spec((None, Fs, D), lambda s, i: (s, 0, 0))
    o_spec = _spec((None, tm, Fs), lambda s, i: (s, i, 0))
    o_shape = jax.ShapeDtypeStruct((N_CHIPS, T, Fs), BF16)
    return _call(body, name, (N_CHIPS, T // tm), [_spec((tm, D), lambda s, i: (i, 0)), w_spec, w_spec],
                 [o_spec, o_spec, o_spec], [o_shape, o_shape, o_shape], [], ("parallel", "parallel"), (n, w1t, w3t), riders)


def _ffn_down(s, w2, h, tm, name):
    _, T, Fs = s.shape
    D = w2.shape[2]
    row = _spec((tm, D), lambda i: (i, 0))
    return _mm(name, (T // tm,), None, "nn",
               [(s, _spec((N_CHIPS, tm, Fs), lambda i: (0, i, 0)), w2, _spec((N_CHIPS, Fs, D), lambda i: (0, 0, 0)))],
               jax.ShapeDtypeStruct((T, D), F32), row, scale=0.5, res=(h, row))


def _ffn_dhidden(dhb, w2, a, c, tm, name, riders=None):
    T, D = dhb.shape
    Fs = w2.shape[1]

    def body(dh_ref, w2_ref, a_ref, c_ref, da_ref, dc_ref):
        d = 0.5 * lax.dot_general(dh_ref[...], w2_ref[pl.program_id(1)], _NT, preferred_element_type=F32)
        av = a_ref[...].astype(F32)
        cv = c_ref[...].astype(F32)
        sg = _sigmoid(av)
        da_ref[...] = (d * cv * (sg * (1.0 + av * (1.0 - sg)))).astype(BF16)
        dc_ref[...] = (d * (av * sg)).astype(BF16)

    h_spec = _spec((None, tm, Fs), lambda i, s: (s, i, 0))
    o_shape = jax.ShapeDtypeStruct((N_CHIPS, T, Fs), BF16)
    return _call(body, name, (T // tm, N_CHIPS),
                 [_spec((tm, D), lambda i, s: (i, 0)), _once((N_CHIPS, Fs, D), lambda i, s: (0, 0, 0)), h_spec, h_spec],
                 [h_spec, h_spec], [o_shape, o_shape], [], ("parallel", "parallel"), (dhb, w2, a, c), riders)


def _wgrad_hidden_rows(s, dhb, tr, name, scale):
    _, T, Fs = s.shape
    D = dhb.shape[1]
    return _mm(name, (N_CHIPS, T // tr), 1, "tn",
               [(s, _spec((None, tr, Fs), lambda k, r: (k, r, 0)), dhb, _spec((tr, D), lambda k, r: (r, 0)))],
               jax.ShapeDtypeStruct((N_CHIPS, Fs, D), F32), _spec((None, Fs, D), lambda k, r: (k, 0, 0)), scale=scale)


def _once(block, index_map):
    return pl.BlockSpec(block, index_map, pipeline_mode=pl.Buffered(1))


def _ffn_dn(da, w1t, dc, w3t, h, g, dres, tm, name, riders=None, examples=None):
    _, T, Fs = da.shape
    D = w1t.shape[2]
    h_spec = _spec((N_CHIPS, tm, Fs), lambda i: (0, i, 0))
    w_spec = _once((N_CHIPS, Fs, D), lambda i: (0, 0, 0))
    return _mm_norm_bwd(name, "nn", [(da, h_spec, w1t, w_spec), (dc, h_spec, w3t, w_spec)], h, g, dres, tm, riders,
                        examples)


def _mm_side_by_side(a, w, mode, out_dtype, name, tm, first=0, count=N_CHIPS, scale=1.0):
    T, K = a.shape
    assert first % count == 0
    n = w.shape[2] if mode == "nn" else w.shape[1]

    def body(a_ref, w_ref, o_ref):
        av = a_ref[...].astype(BF16)
        for j in range(count):
            r = lax.dot_general(av, w_ref[j].astype(BF16), _DIMS[mode], preferred_element_type=F32)
            o_ref[:, j * n:(j + 1) * n] = (r * scale if scale != 1.0 else r).astype(o_ref.dtype)

    return pl.pallas_call(
        body, name=name, grid=(T // tm,),
        in_specs=[_spec((tm, K), lambda i: (i, 0)), _once((count,) + w.shape[1:], lambda i: (first // count, 0, 0))],
        out_specs=_spec((tm, count * n), lambda i: (i, 0)),
        out_shape=jax.ShapeDtypeStruct((T, count * n), out_dtype),
        compiler_params=_params(("parallel",)),
    )(a, w)


def _mm_colslots(a, w, out_dtype, name, tm, first=0, count=N_CHIPS, scale=1.0):
    return _mm_side_by_side(a, w, "nn", out_dtype, name, tm, first, count, scale)


def _wgrad_colslots(a, d, tr, name):
    T, K = a.shape
    Ns = d.shape[1] // N_CHIPS
    return _mm(name, (N_CHIPS, T // tr), 1, "tn",
               [(a, _spec((tr, K), lambda k, r: (r, 0)), d, _spec((tr, Ns), lambda k, r: (r, k)))],
               jax.ShapeDtypeStruct((N_CHIPS, K, Ns), F32), _spec((None, K, Ns), lambda k, r: (k, 0, 0)))


def _mm_rowslots(a, w, h, tm, name):
    T = a.shape[0]
    N = w.shape[2]
    row = _spec((tm, N), lambda i: (i, 0))
    return _mm(name, (T // tm,), None, "nn",
               [(a, _spec((tm, a.shape[1]), lambda i: (i, 0)), w, _once(w.shape, lambda i: (0, 0, 0)))],
               jax.ShapeDtypeStruct((T, N), F32), row, res=(h, row))


def _wgrad_rowslots(a, d, tr, name):
    T = a.shape[0]
    Ks = a.shape[1] // N_CHIPS
    N = d.shape[1]
    return _mm(name, (N_CHIPS, T // tr), 1, "tn",
               [(a, _spec((tr, Ks), lambda k, r: (r, k)), d, _spec((tr, N), lambda k, r: (r, 0)))],
               jax.ShapeDtypeStruct((N_CHIPS, Ks, N), F32), _spec((None, Ks, N), lambda k, r: (k, 0, 0)))


def _gelu_parts(x):
    inner = GELU_C * (x + 0.044715 * (x * x * x))
    t = jnp.tanh(inner)
    return t, GELU_C * (1.0 + 3.0 * 0.044715 * (x * x))


def _gelu_fwd(y, tm, name):
    T, W = y.shape

    def body(y_ref, o_ref):
        x = y_ref[...]
        t, _ = _gelu_parts(x)
        o_ref[...] = (0.5 * x * (1.0 + t)).astype(BF16)

    spec = pl.BlockSpec((tm, W), lambda i: (i, 0))
    return pl.pallas_call(body, name=name, grid=(T // tm,), in_specs=[spec], out_specs=spec,
                          out_shape=jax.ShapeDtypeStruct((T, W), BF16),
                          compiler_params=_params(("parallel",)))(y)


def _gelu_bwd(pairs, y, tm, name):
    T, W = y.shape
    npairs = len(pairs)

    def body(*refs):
        y_ref, o_ref = refs[2 * npairs], refs[2 * npairs + 1]
        x = y_ref[...]
        t, dinner = _gelu_parts(x)
        o_ref[...] = _sum_dots(refs[:2 * npairs], "nt") * (0.5 * (1.0 + t) + 0.5 * x * (1.0 - t * t) * dinner)

    spec = pl.BlockSpec((tm, W), lambda i: (i, 0))
    in_specs, args = [], []
    for d, w in pairs:
        in_specs += [_spec((tm, d.shape[1]), lambda i: (i, 0)), _once(w.shape, lambda i: (0, 0, 0))]
        args += [d, w]
    return pl.pallas_call(body, name=name, grid=(T // tm,), in_specs=in_specs + [spec], out_specs=spec,
                          out_shape=jax.ShapeDtypeStruct((T, W), F32),
                          compiler_params=_params(("parallel",)))(*args, y)


def _merge_cols(D):
    cb = 512 if D % 512 == 0 else D
    return cb, D // cb


def _merge_fwd(gates, attn, ga, gb, tm, name):
    T, D = attn.shape
    cb, nc = _merge_cols(D)

    def body(gat_ref, gss_ref, attn_ref, ga_ref, gb_ref, o_ref):
        ssm = ga_ref[...] * _sigmoid(gb_ref[...])
        o_ref[...] = (_sigmoid(gat_ref[...]) * attn_ref[...] + _sigmoid(gss_ref[...]) * ssm).astype(BF16)

    def col(block):
        return pl.BlockSpec((tm, cb), lambda i, j: (i, block * nc + j))

    return pl.pallas_call(
        body, name=name, grid=(T // tm, nc),
        in_specs=[col(0), col(1), col(0), col(0), col(0)],
        out_specs=col(0), out_shape=jax.ShapeDtypeStruct((T, D), BF16),
        compiler_params=_params(("parallel", "parallel")),
    )(gates, gates, attn, ga, gb)


def _merge_bwd(dhb, w_out, gates, attn, ga, gb, tm, name):
    T, D = attn.shape
    cb, nc = _merge_cols(D)
    Ks = w_out.shape[1]
    spb = cb // Ks

    def body(dh_ref, w_ref, gat_ref, gss_ref, attn_ref, ga_ref, gb_ref, dattn_ref, dgat_ref, dgss_ref, dga_ref, dgb_ref):
        dh = dh_ref[...]
        d = jnp.concatenate([lax.dot_general(dh, w_ref[s], _NT, preferred_element_type=F32) for s in range(spb)], axis=1)
        sa = _sigmoid(gat_ref[...])
        ss = _sigmoid(gss_ref[...])
        sb = _sigmoid(gb_ref[...])
        gav = ga_ref[...]
        dattn_ref[...] = d * sa
        dgat_ref[...] = (d * attn_ref[...] * (sa * (1.0 - sa))).astype(BF16)
        dgss_ref[...] = (d * (gav * sb) * (ss * (1.0 - ss))).astype(BF16)
        dssm = d * ss
        dga_ref[...] = (dssm * sb).astype(BF16)
        dgb_ref[...] = (dssm * gav * (sb * (1.0 - sb))).astype(BF16)

    def col(block):
        return pl.BlockSpec((tm, cb), lambda i, j: (i, block * nc + j))

    b16 = jax.ShapeDtypeStruct((T, D), BF16)
    return pl.pallas_call(
        body, name=name, grid=(T // tm, nc),
        in_specs=[pl.BlockSpec((tm, D), lambda i, j: (i, 0)), pl.BlockSpec((spb, Ks, D), lambda i, j: (j, 0, 0)),
                  col(0), col(1), col(0), col(0), col(0)],
        out_specs=[col(0)] * 5,
        out_shape=[jax.ShapeDtypeStruct((T, D), F32), b16, b16, b16, b16],
        compiler_params=_params(("parallel", "parallel")),
    )(dhb, w_out, gates, gates, attn, ga, gb)


def _loss_head(h, g, target, tm, name):
    T, D = h.shape
    B, S, _ = target.shape
    L = S + N_META
    nt = T // tm
    tpe = L // tm

    def body(h_ref, g_ref, t_hbm, dh_ref, dhb_ref, dg_ref, loss_ref, tbuf, acc_g, acc_l, sems):
        i = pl.program_id(0)
        j = i % tpe
        slot = i % 2

        def fetch(tile, sl, act):
            tb, tj = tile // tpe, tile % tpe

            @pl.when(tj == 0)
            def _():
                act(pltpu.make_async_copy(t_hbm.at[tb, pl.ds(0, tm - N_META), :],
                                          tbuf.at[sl, pl.ds(N_META, tm - N_META), :], sems.at[sl]))

            @pl.when(tj > 0)
            def _():
                act(pltpu.make_async_copy(t_hbm.at[tb, pl.ds(tj * tm - N_META, tm), :], tbuf.at[sl], sems.at[sl]))

        @pl.when(i == 0)
        def _():
            tbuf[:, 0:N_META, :] = jnp.zeros((2, N_META, D), F32)
            fetch(i, slot, lambda cp: cp.start())

        @pl.when(i + 1 < nt)
        def _():
            fetch(i + 1, 1 - slot, lambda cp: cp.start())

        fetch(i, slot, lambda cp: cp.wait())

        x = h_ref[...]
        gv = g_ref[...]
        r = lax.rsqrt(jnp.mean(x * x, axis=-1, keepdims=True) + NORM_EPS)
        xhat = x * r
        pos = j * tm + lax.broadcasted_iota(jnp.int32, (tm, 1), 0)
        err = jnp.where(pos >= N_META, xhat * gv - tbuf[slot], 0.0)
        dy = err * (1.0 / D)
        dxhat = dy * gv
        dh = r * (dxhat - xhat * jnp.mean(dxhat * xhat, axis=-1, keepdims=True))
        dh_ref[...] = dh
        dhb_ref[...] = dh.astype(BF16)
        pg = _fold8(dy * xhat)
        pe = _fold8(err * err)

        @pl.when(i == 0)
        def _():
            acc_g[...] = pg
            acc_l[...] = pe

        @pl.when(i > 0)
        def _():
            acc_g[...] += pg
            acc_l[...] += pe

        @pl.when(i == nt - 1)
        def _():
            dg_ref[...] = jnp.sum(acc_g[...], axis=0, keepdims=True)
            loss_ref[...] = jnp.full((1, D), (0.5 / D) * jnp.sum(acc_l[...]), F32)

    row = pl.BlockSpec((tm, D), lambda i: (i, 0))
    vec = pl.BlockSpec((1, D), lambda i: (0, 0))
    return pl.pallas_call(
        body, name=name, grid=(nt,),
        in_specs=[row, vec, pl.BlockSpec(memory_space=pl.ANY)], out_specs=[row, row, vec, vec],
        out_shape=[jax.ShapeDtypeStruct((T, D), F32), jax.ShapeDtypeStruct((T, D), BF16),
                   jax.ShapeDtypeStruct((1, D), F32), jax.ShapeDtypeStruct((1, D), F32)],
        scratch_shapes=[pltpu.VMEM((2, tm, D), F32), pltpu.VMEM((8, D), F32), pltpu.VMEM((8, D), F32),
                        pltpu.SemaphoreType.DMA((2,))],
        compiler_params=_params(("arbitrary",)),
    )(h, g, target)


def _heads_to_rows(blk):
    return jnp.concatenate([blk[:, g * HEAD_DIM:(g + 1) * HEAD_DIM] for g in range(Q_PER_KV)], axis=0)


def _rows_to_heads(x):
    rows = x.shape[0] // Q_PER_KV
    return jnp.concatenate([x[g * rows:(g + 1) * rows] for g in range(Q_PER_KV)], axis=1)


def _causal(R):
    kj = lax.broadcasted_iota(jnp.int32, (BLOCK, R), 0)
    qi = lax.broadcasted_iota(jnp.int32, (BLOCK, R), 1) & (BLOCK - 1)
    return kj <= qi


def _band_probs(s_band, s_m, sink):
    m = jnp.maximum(jnp.maximum(jnp.max(s_band, axis=0, keepdims=True), jnp.max(s_m, axis=0, keepdims=True)), sink)
    e_b, e_m, e_s = jnp.exp(s_band - m), jnp.exp(s_m - m), jnp.exp(sink - m)
    inv = 1.0 / (jnp.sum(e_b, axis=0, keepdims=True) + jnp.sum(e_m, axis=0, keepdims=True) + e_s)
    return e_b * inv, e_m * inv, e_s * inv


def _fold_band(tri, two):
    return jnp.where(tri, two[BLOCK:2 * BLOCK], two[0:BLOCK])


def _unfold_band(tri, band):
    return jnp.concatenate([jnp.where(tri, 0.0, band), jnp.where(tri, band, 0.0)], axis=0)


def _meta_probs(qm, k_m, sink_m):
    R = qm.shape[0]
    s = lax.dot_general(qm, k_m, _NT, preferred_element_type=F32)
    qi = lax.broadcasted_iota(jnp.int32, (R, N_META), 0) & (N_META - 1)
    kj = lax.broadcasted_iota(jnp.int32, (R, N_META), 1)
    s = jnp.where(kj <= qi, s, NEG_INF)
    m = jnp.maximum(jnp.max(s, axis=-1, keepdims=True), sink_m)
    e, e_s = jnp.exp(s - m), jnp.exp(sink_m - m)
    inv = 1.0 / (jnp.sum(e, axis=-1, keepdims=True) + e_s)
    return e * inv, e_s * inv


def _block_start(n):
    return pl.multiple_of(N_META + n * BLOCK, ROW_ALIGN)


def _kv(blk):
    return blk[:, 0:HEAD_DIM], blk[:, HEAD_DIM:2 * HEAD_DIM]


def _attn_fwd(q, kv, sink_row, sink_meta, B, name, riders=None):
    T, D = q.shape
    L = T // B
    KV = D // QW
    nb = (L - N_META) // BLOCK

    def body(q_ref, kv_ref, sk_ref, skm_ref, o_ref, kvs):
        kvs[...] = kv_ref[...].astype(BF16)
        k_m, v_m = _kv(kvs[0:N_META, :])
        p, _ = _meta_probs(_heads_to_rows(q_ref[0:N_META, :]), k_m, skm_ref[0])
        o_ref[0:N_META, :] = _rows_to_heads(jnp.dot(p.astype(BF16), v_m, preferred_element_type=F32))
        tri = _causal(Q_PER_KV * BLOCK)

        def block(cur, first, keys):
            k2, v2 = _kv(kvs[keys, :])
            qb = _heads_to_rows(q_ref[pl.ds(cur, BLOCK), :])
            st = lax.dot_general(k2, qb, _NT, preferred_element_type=F32)
            smt = lax.dot_general(k_m, qb, _NT, preferred_element_type=F32)
            s_band = jnp.where(tri, st, NEG_INF) if first else _fold_band(tri, st)
            p_b, p_m, _ = _band_probs(s_band, smt, sk_ref[0])
            p2 = (p_b if first else _unfold_band(tri, p_b)).astype(BF16)
            o = (lax.dot_general(p2, v2, _TN, preferred_element_type=F32)
                 + lax.dot_general(p_m.astype(BF16), v_m, _TN, preferred_element_type=F32))
            o_ref[pl.ds(cur, BLOCK), :] = _rows_to_heads(o)

        block(N_META, True, pl.ds(N_META, BLOCK))

        def step(n, carry):
            block(_block_start(n), False, pl.ds(_block_start(n - 1), 2 * BLOCK))
            return carry

        lax.fori_loop(1, nb, step, 0, unroll=5 if (nb - 1) % 5 == 0 else 1)

    q_spec = pl.BlockSpec((L, QW), lambda b, h: (b, h))
    return _call(body, name, (B, KV),
                 [q_spec, pl.BlockSpec((L, 2 * HEAD_DIM), lambda b, h: (b, h)),
                  pl.BlockSpec((1, 1, Q_PER_KV * BLOCK), lambda b, h: (h, 0, 0)),
                  pl.BlockSpec((1, Q_PER_KV * N_META, 1), lambda b, h: (h, 0, 0))],
                 [q_spec], [jax.ShapeDtypeStruct((T, D), F32)], [pltpu.VMEM((L, 2 * HEAD_DIM), BF16)],
                 ("parallel", "parallel"), (q, kv, sink_row, sink_meta), riders)


def _attn_bwd(q, kv, o, do, sink_row, sink_meta, B, name, riders=None):
    T, D = q.shape
    L = T // B
    KV = D // QW
    nb = (L - N_META) // BLOCK
    R = Q_PER_KV * BLOCK
    scale = HEAD_DIM ** -0.5

    def head_totals(col, rows_per_head):
        rid = lax.broadcasted_iota(jnp.int32, (8, 128), 0)
        out = jnp.zeros((8, 128), F32)
        for g in range(Q_PER_KV):
            out = out + jnp.where(rid == g, jnp.sum(col[g * rows_per_head:(g + 1) * rows_per_head, :]), 0.0)
        return out

    def body(q_ref, kv_ref, o_ref, do_ref, sk_ref, skm_ref, dq_ref, dkv_ref, dsk_ref, kvs, acc, acc_sink):
        b = pl.program_id(1)
        kvs[...] = kv_ref[...].astype(BF16)
        acc[...] = jnp.zeros_like(acc)
        k_m, v_m = _kv(kvs[0:N_META, :])

        qm = _heads_to_rows(q_ref[0:N_META, :])
        dom = _heads_to_rows(do_ref[0:N_META, :])
        delta = jnp.sum(dom * _heads_to_rows(o_ref[0:N_META, :]), axis=-1, keepdims=True)
        p, p_s = _meta_probs(qm, k_m, skm_ref[0])
        domb = dom.astype(BF16)
        ds = (p * (lax.dot_general(domb, v_m, _NT, preferred_element_type=F32) - delta)).astype(BF16)
        dq_ref[0:N_META, :] = _rows_to_heads(jnp.dot(ds, k_m, preferred_element_type=F32) * scale).astype(BF16)
        acc[0:N_META, :] += jnp.concatenate([lax.dot_general(ds, qm, _TN, preferred_element_type=F32),
                                             lax.dot_general(p.astype(BF16), domb, _TN, preferred_element_type=F32)], axis=1)
        sink_tot = head_totals(-p_s * delta, N_META)
        tri = _causal(R)
        acc_sink[...] = jnp.zeros_like(acc_sink)
        ones = jnp.ones((8, HEAD_DIM), BF16)

        def block(cur, first, keys):
            k2, v2 = _kv(kvs[keys, :])
            rows = pl.ds(cur, BLOCK)
            qb = _heads_to_rows(q_ref[rows, :])
            dob = _heads_to_rows(do_ref[rows, :])
            prod = dob * _heads_to_rows(o_ref[rows, :])
            hi = prod.astype(BF16)
            lo = (prod - hi.astype(F32)).astype(BF16)
            delta = (lax.dot_general(ones, hi, _NT, preferred_element_type=F32)
                     + lax.dot_general(ones, lo, _NT, preferred_element_type=F32))[0:1]
            dobb = dob.astype(BF16)
            st = lax.dot_general(k2, qb, _NT, preferred_element_type=F32)
            smt = lax.dot_general(k_m, qb, _NT, preferred_element_type=F32)
            s_band = jnp.where(tri, st, NEG_INF) if first else _fold_band(tri, st)
            p_b, p_m, p_s = _band_probs(s_band, smt, sk_ref[0])
            dpt = lax.dot_general(v2, dobb, _NT, preferred_element_type=F32)
            dpm = lax.dot_general(v_m, dobb, _NT, preferred_element_type=F32)
            ds_b = p_b * ((dpt if first else _fold_band(tri, dpt)) - delta)
            ds2 = (ds_b if first else _unfold_band(tri, ds_b)).astype(BF16)
            p2 = (p_b if first else _unfold_band(tri, p_b)).astype(BF16)
            dsm = (p_m * (dpm - delta)).astype(BF16)
            pm = p_m.astype(BF16)
            dq = (lax.dot_general(ds2, k2, _TN, preferred_element_type=F32)
                  + lax.dot_general(dsm, k_m, _TN, preferred_element_type=F32))
            dq_ref[rows, :] = _rows_to_heads(dq * scale).astype(BF16)
            acc[keys, :] += jnp.concatenate([jnp.dot(ds2, qb, preferred_element_type=F32),
                                             jnp.dot(p2, dobb, preferred_element_type=F32)], axis=1)
            acc[0:N_META, :] += jnp.concatenate([jnp.dot(dsm, qb, preferred_element_type=F32),
                                                 jnp.dot(pm, dobb, preferred_element_type=F32)], axis=1)
            acc_sink[0:1, :] += -p_s * delta

        block(N_META, True, pl.ds(N_META, BLOCK))

        def step(n, carry):
            block(_block_start(n), False, pl.ds(_block_start(n - 1), 2 * BLOCK))
            return carry

        lax.fori_loop(1, nb, step, 0, unroll=5 if (nb - 1) % 5 == 0 else 1)
        dkv_ref[...] = acc[...].astype(BF16)
        rid = lax.broadcasted_iota(jnp.int32, (8, 128), 0)
        tot = sink_tot
        for g in range(Q_PER_KV):
            tot = tot + jnp.where(rid == g, jnp.sum(acc_sink[:, g * BLOCK:(g + 1) * BLOCK]), 0.0)

        @pl.when(b == 0)
        def _():
            dsk_ref[0] = tot

        @pl.when(b > 0)
        def _():
            dsk_ref[0] += tot

    q_spec = pl.BlockSpec((L, QW), lambda h, b: (b, h))
    kv_spec = pl.BlockSpec((L, 2 * HEAD_DIM), lambda h, b: (b, h))
    return _call(body, name, (KV, B),
                 [q_spec, kv_spec, q_spec, q_spec,
                  pl.BlockSpec((1, 1, R), lambda h, b: (h, 0, 0)),
                  pl.BlockSpec((1, Q_PER_KV * N_META, 1), lambda h, b: (h, 0, 0))],
                 [q_spec, kv_spec, pl.BlockSpec((1, 8, 128), lambda h, b: (h, 0, 0))],
                 [jax.ShapeDtypeStruct((T, D), BF16), jax.ShapeDtypeStruct((T, KV * 2 * HEAD_DIM), BF16),
                  jax.ShapeDtypeStruct((KV, 8, 128), F32)],
                 [pltpu.VMEM((L, 2 * HEAD_DIM), BF16), pltpu.VMEM((L, 2 * HEAD_DIM), F32), pltpu.VMEM((8, R), F32)],
                 ("parallel", "arbitrary"), (q, kv, o, do, sink_row, sink_meta), riders)


def _cmul_add(acc_r, acc_i, lr, li, xr, xi):
    return acc_r + (lr * xr - li * xi), acc_i + (lr * xi + li * xr)


def _cols_per_step(ncol):
    for cps in (4, 2):
        if ncol % cps == 0:
            return cps
    return 1


def _ssm_fwd(u, bmat, cmat, dskip, tables, nbatch, rc, name):
    T, W = u.shape
    ncol = W // SSM_LANES
    nch = T // rc
    S = STATE_LANES
    cps = _cols_per_step(ncol)
    assert nbatch == 4

    def body(u_ref, b_ref, c_ref, d_ref, tab_ref, y_ref, xs_ref, st_ref, carry_ref):
        ch = pl.program_id(1)

        @pl.when(ch == 0)
        def _():
            carry_ref[...] = jnp.zeros_like(carry_ref)

        uv = u_ref[...]
        for k in range(cps):
            st_ref[:, 2 * S * k:2 * S * (k + 1)] = jnp.dot(uv[:, SSM_LANES * k:SSM_LANES * (k + 1)].astype(BF16), b_ref[k],
                                                           preferred_element_type=F32)
        low = lax.broadcasted_iota(jnp.int32, (8, S), 0) < nbatch

        def tile(k, r0, c_r, c_i):
            re, im = slice(2 * S * k, 2 * S * k + S), slice(2 * S * k + S, 2 * S * (k + 1))
            la_r, la_i = tab_ref[k, :, 0:S], tab_ref[k, :, S:2 * S]
            lb_r, lb_i = tab_ref[k, :, 2 * S:3 * S], tab_ref[k, :, 3 * S:4 * S]
            v_r = st_ref[pl.ds(r0, 8), re]
            v_i = st_ref[pl.ds(r0, 8), im]
            v_r, v_i = _cmul_add(v_r, v_i, la_r, la_i, pltpu.roll(v_r, nbatch, 0), pltpu.roll(v_i, nbatch, 0))
            rc_r, rc_i = pltpu.roll(c_r, nbatch, 0), pltpu.roll(c_i, nbatch, 0)
            cb_r, cb_i = jnp.where(low, rc_r, c_r), jnp.where(low, rc_i, c_i)
            v_r, v_i = _cmul_add(v_r, v_i, lb_r, lb_i, cb_r, cb_i)
            st_ref[pl.ds(r0, 8), re] = v_r
            st_ref[pl.ds(r0, 8), im] = v_i
            return v_r, v_i

        def step(i, carry):
            r0 = pl.multiple_of(i * 8, 8)
            out = []
            for k in range(cps):
                out += list(tile(k, r0, carry[2 * k], carry[2 * k + 1]))
            return tuple(out)

        halves = tuple(carry_ref[:, S * j:S * (j + 1)] for j in range(2 * cps))
        halves = lax.fori_loop(0, rc // 8, step, halves)
        for j in range(2 * cps):
            carry_ref[:, S * j:S * (j + 1)] = halves[j]
        xb = st_ref[...].astype(BF16)
        xs_ref[...] = xb
        for k in range(cps):
            cols = slice(SSM_LANES * k, SSM_LANES * (k + 1))
            y_ref[:, cols] = (jnp.dot(xb[:, 2 * S * k:2 * S * (k + 1)], c_ref[k], preferred_element_type=F32)
                              + d_ref[:, cols] * uv[:, cols])

    return pl.pallas_call(
        body, name=name, grid=(ncol // cps, nch),
        in_specs=[pl.BlockSpec((rc, cps * SSM_LANES), lambda g, c: (c, g)),
                  pl.BlockSpec((cps, SSM_LANES, 2 * S), lambda g, c: (g, 0, 0)),
                  pl.BlockSpec((cps, 2 * S, SSM_LANES), lambda g, c: (g, 0, 0)),
                  pl.BlockSpec((1, cps * SSM_LANES), lambda g, c: (0, g)),
                  pl.BlockSpec((cps, 8, 4 * S), lambda g, c: (g, 0, 0))],
        out_specs=[pl.BlockSpec((rc, cps * SSM_LANES), lambda g, c: (c, g)),
                   pl.BlockSpec((rc, cps * 2 * S), lambda g, c: (c, g))],
        out_shape=[jax.ShapeDtypeStruct((T, W), F32), jax.ShapeDtypeStruct((T, ncol * 2 * S), BF16)],
        scratch_shapes=[pltpu.VMEM((rc, cps * 2 * S), F32), pltpu.VMEM((8, cps * 2 * S), F32)],
        compiler_params=_params(("parallel", "arbitrary")),
    )(u, bmat, cmat, dskip, tables)


def _ssm_bwd(dy, u, xs, bmat, cmat, dskip, tables, nbatch, rc, name):
    T, W = u.shape
    ncol = W // SSM_LANES
    nch = T // rc
    S = STATE_LANES
    ntile = rc // 16
    cps = _cols_per_step(ncol)

    def body(dy_ref, u_ref, xs_ref, b_ref, c_ref, d_ref, tab_ref,
             du_ref, db_ref, dc_ref, dl_ref, dd_ref, st_ref, carry_ref, accl_ref, accd_ref):
        ch = pl.program_id(1)

        @pl.when(ch == 0)
        def _():
            carry_ref[...] = jnp.zeros_like(carry_ref)
            accl_ref[...] = jnp.zeros_like(accl_ref)
            accd_ref[...] = jnp.zeros_like(accd_ref)
            db_ref[...] = jnp.zeros_like(db_ref)
            dc_ref[...] = jnp.zeros_like(dc_ref)

        dyv = dy_ref[...]
        uv = u_ref[...]
        dyb = dyv.astype(BF16)
        for k in range(cps):
            st_ref[:, 2 * S * k:2 * S * (k + 1)] = lax.dot_general(dyb[:, SSM_LANES * k:SSM_LANES * (k + 1)], c_ref[k], _NT,
                                                                   preferred_element_type=F32)
        low = lax.broadcasted_iota(jnp.int32, (8, S), 0) < nbatch

        def tile(k, r0, x_r, x_i, c_r, c_i, al_r, al_i):
            re, im = slice(2 * S * k, 2 * S * k + S), slice(2 * S * k + S, 2 * S * (k + 1))
            la_r, la_i = tab_ref[k, :, 0:S], tab_ref[k, :, S:2 * S]
            lb_r, lb_i = tab_ref[k, :, 2 * S:3 * S], tab_ref[k, :, 3 * S:4 * S]
            v_r = st_ref[pl.ds(r0, 8), re]
            v_i = st_ref[pl.ds(r0, 8), im]
            v_r, v_i = _cmul_add(v_r, v_i, la_r, la_i, pltpu.roll(v_r, nbatch, 0), pltpu.roll(v_i, nbatch, 0))
            cb_r = jnp.where(low, c_r, pltpu.roll(c_r, nbatch, 0))
            cb_i = jnp.where(low, c_i, pltpu.roll(c_i, nbatch, 0))
            v_r, v_i = _cmul_add(v_r, v_i, lb_r, lb_i, cb_r, cb_i)
            st_ref[pl.ds(r0, 8), re] = v_r
            st_ref[pl.ds(r0, 8), im] = v_i
            n_r = jnp.where(low, pltpu.roll(v_r, nbatch, 0), cb_r)
            n_i = jnp.where(low, pltpu.roll(v_i, nbatch, 0), cb_i)
            al_r = al_r + (n_r * x_r + n_i * x_i)
            al_i = al_i + (n_i * x_r - n_r * x_i)
            return v_r, v_i, al_r, al_i

        def step(j, carry):
            r0 = pl.multiple_of((ntile - 1 - j) * 16, 16)
            out = []
            for k in range(cps):
                re, im = slice(2 * S * k, 2 * S * k + S), slice(2 * S * k + S, 2 * S * (k + 1))
                x_r = xs_ref[pl.ds(r0, 16), re].astype(F32)
                x_i = xs_ref[pl.ds(r0, 16), im].astype(F32)
                mid = tile(k, r0 + 8, x_r[8:16], x_i[8:16], *carry[4 * k:4 * k + 4])
                out += list(tile(k, r0, x_r[0:8], x_i[0:8], *mid))
            return tuple(out)

        init = []
        for k in range(cps):
            init += [carry_ref[:, 2 * S * k:2 * S * k + S], carry_ref[:, 2 * S * k + S:2 * S * (k + 1)],
                     accl_ref[:, 2 * S * k:2 * S * k + S], accl_ref[:, 2 * S * k + S:2 * S * (k + 1)]]
        fin = lax.fori_loop(0, ntile, step, tuple(init))
        for k in range(cps):
            carry_ref[:, 2 * S * k:2 * S * k + S] = fin[4 * k]
            carry_ref[:, 2 * S * k + S:2 * S * (k + 1)] = fin[4 * k + 1]
            accl_ref[:, 2 * S * k:2 * S * k + S] = fin[4 * k + 2]
            accl_ref[:, 2 * S * k + S:2 * S * (k + 1)] = fin[4 * k + 3]
        dsb = st_ref[...].astype(BF16)
        ub = uv.astype(BF16)
        for k in range(cps):
            cols, lanes = slice(SSM_LANES * k, SSM_LANES * (k + 1)), slice(2 * S * k, 2 * S * (k + 1))
            du_ref[:, cols] = (lax.dot_general(dsb[:, lanes], b_ref[k], _NT, preferred_element_type=F32)
                               + d_ref[:, cols] * dyv[:, cols])
            db_ref[k] += lax.dot_general(ub[:, cols], dsb[:, lanes], _TN, preferred_element_type=F32)
            dc_ref[k] += lax.dot_general(xs_ref[:, lanes], dyb[:, cols], _TN, preferred_element_type=F32)
        accd_ref[...] += _fold8(dyv * uv)

        @pl.when(ch == nch - 1)
        def _():
            for k in range(cps):
                dl_ref[k] = jnp.sum(accl_ref[:, 2 * S * k:2 * S * (k + 1)], axis=0, keepdims=True)
            dd_ref[...] = jnp.sum(accd_ref[...], axis=0, keepdims=True)

    rev = lambda g, c: (nch - 1 - c, g)
    return pl.pallas_call(
        body, name=name, grid=(ncol // cps, nch),
        in_specs=[pl.BlockSpec((rc, cps * SSM_LANES), rev), pl.BlockSpec((rc, cps * SSM_LANES), rev),
                  pl.BlockSpec((rc, cps * 2 * S), rev),
                  pl.BlockSpec((cps, SSM_LANES, 2 * S), lambda g, c: (g, 0, 0)),
                  pl.BlockSpec((cps, 2 * S, SSM_LANES), lambda g, c: (g, 0, 0)),
                  pl.BlockSpec((1, cps * SSM_LANES), lambda g, c: (0, g)),
                  pl.BlockSpec((cps, 8, 4 * S), lambda g, c: (g, 0, 0))],
        out_specs=[pl.BlockSpec((rc, cps * SSM_LANES), rev),
                   pl.BlockSpec((cps, SSM_LANES, 2 * S), lambda g, c: (g, 0, 0)),
                   pl.BlockSpec((cps, 2 * S, SSM_LANES), lambda g, c: (g, 0, 0)),
                   pl.BlockSpec((cps, 1, 2 * S), lambda g, c: (g, 0, 0)),
                   pl.BlockSpec((1, cps * SSM_LANES), lambda g, c: (0, g))],
        out_shape=[jax.ShapeDtypeStruct((T, W), F32),
                   jax.ShapeDtypeStruct((ncol, SSM_LANES, 2 * S), F32),
                   jax.ShapeDtypeStruct((ncol, 2 * S, SSM_LANES), F32),
                   jax.ShapeDtypeStruct((ncol, 1, 2 * S), F32),
                   jax.ShapeDtypeStruct((1, W), F32)],
        scratch_shapes=[pltpu.VMEM((rc, cps * 2 * S), F32), pltpu.VMEM((8, cps * 2 * S), F32),
                        pltpu.VMEM((8, cps * 2 * S), F32), pltpu.VMEM((8, cps * SSM_LANES), F32)],
        compiler_params=_params(("parallel", "arbitrary")),
    )(dy, u, xs, bmat, cmat, dskip, tables)


def _ssm_matrices(a_re, a_im, log_step, b_re, b_im, c_re, c_im):
    G, N = a_re.shape
    ncol = G // GROUPS_PER_COL
    step = jnp.exp(log_step)[:, None]
    mag = jnp.exp(a_re * step)
    ang = a_im * step
    lam_re, lam_im = mag * jnp.cos(ang), mag * jnp.sin(ang)
    den = a_re * a_re + a_im * a_im
    nr, ni = lam_re - 1.0, lam_im
    coef_re = (nr * a_re + ni * a_im) / den
    coef_im = (ni * a_re - nr * a_im) / den
    bb_re = coef_re[..., None] * b_re - coef_im[..., None] * b_im
    bb_im = coef_re[..., None] * b_im + coef_im[..., None] * b_re
    eye = jnp.eye(GROUPS_PER_COL, dtype=F32)
    bb = jnp.stack([bb_re, bb_im]).reshape(2, ncol, GROUPS_PER_COL, N, SSM_GROUP)
    bmat = jnp.einsum("pbgnc,gh->bgcphn", bb, eye).reshape(ncol, SSM_LANES, 2 * STATE_LANES)
    cc = jnp.stack([c_re, -c_im]).reshape(2, ncol, GROUPS_PER_COL, SSM_GROUP, N)
    cmat = jnp.einsum("pbgcn,gh->bpgnhc", cc, eye).reshape(ncol, 2 * STATE_LANES, SSM_LANES)
    lam = jnp.concatenate([lam_re.reshape(ncol, STATE_LANES), lam_im.reshape(ncol, STATE_LANES)], axis=-1)
    return lam, bmat, cmat


def _scan_tables(lam, nbatch, conj):
    S = STATE_LANES
    lr, li = lam[:, None, 0:S], lam[:, None, S:2 * S]
    if conj:
        li = -li
    l2r, l2i = lr * lr - li * li, 2.0 * lr * li
    first = (jnp.arange(8) < nbatch)[None, :, None]
    zero = jnp.zeros_like(lr)
    if conj:
        parts = [jnp.where(first, lr, zero), jnp.where(first, li, zero), jnp.where(first, l2r, lr), jnp.where(first, l2i, li)]
    else:
        parts = [jnp.where(first, zero, lr), jnp.where(first, zero, li), jnp.where(first, lr, l2r), jnp.where(first, li, l2i)]
    return jnp.concatenate([jnp.broadcast_to(p, (lam.shape[0], 8, S)) for p in parts], axis=-1)


def _adamw_update(w_ref, g_ref, m_ref, v_ref, d_ref, nm_ref, nv_ref):
    gv = g_ref[...]
    mn = ADAM_B1 * m_ref[...] + (1.0 - ADAM_B1) * gv
    vn = ADAM_B2 * v_ref[...] + (1.0 - ADAM_B2) * (gv * gv)
    m_hat = mn / (1.0 - ADAM_B1 ** ADAM_STEP)
    v_hat = vn / (1.0 - ADAM_B2 ** ADAM_STEP)
    d_ref[...] = -ADAM_LR * (m_hat / (jnp.sqrt(v_hat) + ADAM_EPS) + ADAM_WD * w_ref[...])
    nm_ref[...] = mn
    nv_ref[...] = vn


def _adamw_small(ws, gs, ms, vs, name):
    n = len(ws)

    def body(*refs):
        for i in range(n):
            _adamw_update(refs[i], refs[n + i], refs[2 * n + i], refs[3 * n + i],
                          refs[4 * n + i], refs[5 * n + i], refs[6 * n + i])

    vm = pl.BlockSpec(memory_space=pltpu.VMEM)
    shapes = [jax.ShapeDtypeStruct(a.shape, F32) for a in ws]
    outs = pl.pallas_call(body, name=name, in_specs=[vm] * (4 * n), out_specs=[vm] * (3 * n), out_shape=shapes * 3,
                          compiler_params=pltpu.CompilerParams(vmem_limit_bytes=VMEM_LIMIT))(*ws, *gs, *ms, *vs)
    return outs[:n], outs[n:2 * n], outs[2 * n:]


def _adamw(w, g, m, v, name):
    R, C = w.shape[-2], w.shape[-1]
    tr = R if R <= 512 else _pick_tile(R, 512, 8)
    body = functools.partial(_adamw_update)

    def spec_for(a):
        if len(a.shape) == 2:
            return pl.BlockSpec((tr, C), lambda i: (i, 0))
        return pl.BlockSpec((None, tr, C), lambda i: (0, i, 0))

    spec = spec_for(w)
    shp = jax.ShapeDtypeStruct(w.shape, F32)
    return pl.pallas_call(body, name=name, grid=(R // tr,), in_specs=[spec, spec_for(g), spec, spec], out_specs=[spec] * 3,
                          out_shape=[shp, shp, shp], compiler_params=_params(("parallel",)))(w, g, m, v)


_ANY = pl.BlockSpec(memory_space=pl.ANY)


def _place():
    x, y, c = lax.axis_index("x"), lax.axis_index("y"), lax.axis_index("c")
    chips = [(1 - x, y), (x, 1 - y), (1 - x, 1 - y)]
    return x, y, c, chips


def _remote(src, dst, send_sems, recv_sems, k, to):
    return pltpu.make_async_remote_copy(src_ref=src, dst_ref=dst, send_sem=send_sems.at[k], recv_sem=recv_sems.at[k],
                                        device_id=to, device_id_type=MESH_IDS)


class _Riders:
    def __init__(self, srcs, out_shapes, n_sems, copies):
        self.srcs, self.out_shapes, self.n_sems, self.copies = list(srcs), list(out_shapes), n_sems, copies


def _call(body, name, grid, in_specs, out_specs, out_shape, scratch_shapes, sem, args, riders=None):
    if riders is None:
        return pl.pallas_call(body, name=name, grid=grid, in_specs=in_specs, out_specs=out_specs, out_shape=out_shape,
                              scratch_shapes=scratch_shapes, compiler_params=_params(sem))(*args)
    n_in, n_out, n_scr = len(in_specs), len(out_specs), len(scratch_shapes)
    r_in, r_out = len(riders.srcs), len(riders.out_shapes)

    def carrying(*refs):
        a, b = n_in, n_in + r_in
        c, d = b + n_out, b + n_out + r_out
        e = d + n_scr
        sends, arrivals = riders.copies(refs[a:b], refs[c:d], refs[e], refs[e + 1])
        first, last = None, None
        for ax, size in enumerate(grid):
            at0, at1 = pl.program_id(ax) == 0, pl.program_id(ax) == size - 1
            first = at0 if first is None else first & at0
            last = at1 if last is None else last & at1

        @pl.when(first)
        def _():
            for cp in sends:
                cp.start()

        body(*refs[:a], *refs[b:c], *refs[d:e])

        @pl.when(last)
        def _():
            for cp in arrivals:
                cp.wait_recv()
            for cp in sends:
                cp.wait_send()

    outs = pl.pallas_call(
        carrying, name=name, grid=grid, in_specs=list(in_specs) + [_ANY] * r_in,
        out_specs=list(out_specs) + [_ANY] * r_out, out_shape=list(out_shape) + riders.out_shapes,
        scratch_shapes=list(scratch_shapes) + [pltpu.SemaphoreType.DMA((riders.n_sems,)),
                                               pltpu.SemaphoreType.DMA((riders.n_sems,))],
        compiler_params=pltpu.CompilerParams(dimension_semantics=("arbitrary",) * len(grid),
                                             vmem_limit_bytes=VMEM_LIMIT, has_side_effects=True),
    )(*args, *riders.srcs)
    return outs[:n_out], outs[n_out:]


def _gather_riders(shards):
    def copies(srcs, outs, send_sems, recv_sems):
        x, y, c, chips = _place()
        sends, arrivals = [], []
        for i, s in enumerate(shards):
            half = s.shape[0] // 2
            rows = pl.ds(c * half, half)
            for j, chip in enumerate(chips):
                sends.append(_remote(srcs[i].at[rows, :], outs[i].at[2 * x + y, rows, :], send_sems, recv_sems,
                                     3 * i + j, (*chip, c)))
                landed = outs[i].at[2 * chip[0] + chip[1], rows, :]
                arrivals.append(_remote(landed, landed, send_sems, recv_sems, 3 * i + j, (*chip, c)))
        return sends, arrivals

    return _Riders(shards, [jax.ShapeDtypeStruct((N_CHIPS,) + s.shape, s.dtype) for s in shards], 3 * len(shards), copies)


def _exchange_riders(parts):
    def copies(srcs, outs, send_sems, recv_sems):
        x, y, c, chips = _place()
        sends = [_remote(srcs[i].at[2 * chip[0] + chip[1]], outs[i].at[j], send_sems, recv_sems, 3 * i + j, (*chip, c))
                 for i in range(len(parts)) for j, chip in enumerate(chips)]
        return sends, sends

    return _Riders(parts, [jax.ShapeDtypeStruct((3,) + p.shape[1:], p.dtype) for p in parts], 3 * len(parts), copies)


def _swap_riders(grads):
    def copies(srcs, outs, send_sems, recv_sems):
        x, y, c, _ = _place()
        sends = []
        for i, g in enumerate(grads):
            half = g.shape[1] // 2
            sends.append(_remote(srcs[i].at[:, pl.ds((1 - c) * half, half), :], outs[i], send_sems, recv_sems, i,
                                 (x, y, 1 - c)))
        return sends, sends

    return _Riders(grads, [jax.ShapeDtypeStruct((N_CHIPS, g.shape[1] // 2, g.shape[2]), g.dtype) for g in grads],
                   len(grads), copies)


def _forward_halves(gathered, shards, tag):
    n = len(gathered)

    def body(*refs):
        srcs, outs = refs[:n], refs[n:2 * n]
        send_sems, recv_sems = refs[2 * n:]
        x, y, c, chips = _place()
        sibling = (x, y, 1 - c)
        cps = []
        for i in range(n):
            half = gathered[i].shape[1] // 2
            for j, chip in enumerate(chips):
                slot = 2 * chip[0] + chip[1]
                cps.append(_remote(srcs[i].at[slot, pl.ds(c * half, half), :], outs[i].at[slot, pl.ds(c * half, half), :],
                                   send_sems, recv_sems, 3 * i + j, sibling))
        for cp in cps:
            cp.start()
        for i in range(n):
            half = gathered[i].shape[1] // 2
            for j, chip in enumerate(chips):
                theirs = outs[i].at[2 * chip[0] + chip[1], pl.ds((1 - c) * half, half), :]
                _remote(theirs, theirs, send_sems, recv_sems, 3 * i + j, sibling).wait_recv()
        for cp in cps:
            cp.wait_send()

    outs = pl.pallas_call(
        body, name=f"gather_forward_{tag}", in_specs=[_ANY] * n, out_specs=[_ANY] * n,
        out_shape=[jax.ShapeDtypeStruct(g.shape, g.dtype) for g in gathered],
        input_output_aliases={i: i for i in range(n)},
        scratch_shapes=[pltpu.SemaphoreType.DMA((3 * n,)), pltpu.SemaphoreType.DMA((3 * n,))],
        compiler_params=pltpu.CompilerParams(has_side_effects=True),
    )(*gathered)
    slot = 2 * lax.axis_index("x") + lax.axis_index("y")
    return [lax.dynamic_update_slice(o, s[None], (slot, 0, 0)) for o, s in zip(outs, shards)]


def _gather_weights(shards):
    n = len(shards)

    def body(*refs):
        srcs, outs = refs[:n], refs[n:2 * n]
        send_sems, recv_sems = refs[2 * n:]
        x, y, c, chips = _place()
        sibling = (x, y, 1 - c)

        def piece(i, px, py, pc):
            half = shards[i].shape[0] // 2
            return outs[i].at[2 * px + py, pl.ds(pc * half, half), :]

        first = []
        for i in range(n):
            half = shards[i].shape[0] // 2
            for j, chip in enumerate(chips):
                first.append(_remote(srcs[i].at[pl.ds(c * half, half), :], piece(i, x, y, c), send_sems, recv_sems,
                                     6 * i + j, (*chip, c)))
        for cp in first:
            cp.start()
        passed = []
        for i in range(n):
            for j, chip in enumerate(chips):
                _remote(piece(i, *chip, c), piece(i, *chip, c), send_sems, recv_sems, 6 * i + j, (*chip, c)).wait_recv()
                cp = _remote(piece(i, *chip, c), piece(i, *chip, c), send_sems, recv_sems, 6 * i + 3 + j, sibling)
                cp.start()
                passed.append(cp)
        for i in range(n):
            for j, chip in enumerate(chips):
                _remote(piece(i, *chip, 1 - c), piece(i, *chip, 1 - c), send_sems, recv_sems, 6 * i + 3 + j,
                        sibling).wait_recv()
        for cp in first + passed:
            cp.wait_send()

    outs = pl.pallas_call(
        body, name="gather_weights", in_specs=[_ANY] * n, out_specs=[_ANY] * n,
        out_shape=[jax.ShapeDtypeStruct((N_CHIPS,) + s.shape, s.dtype) for s in shards],
        scratch_shapes=[pltpu.SemaphoreType.DMA((6 * n,)), pltpu.SemaphoreType.DMA((6 * n,))],
        compiler_params=pltpu.CompilerParams(has_side_effects=True),
    )(*shards)
    slot = 2 * lax.axis_index("x") + lax.axis_index("y")
    return [lax.dynamic_update_slice(o, s[None], (slot, 0, 0)) for o, s in zip(outs, shards)]


def _swap_halves(grads, tag):
    n = len(grads)

    def body(*refs):
        srcs, outs = refs[:n], refs[n:2 * n]
        send_sems, recv_sems = refs[2 * n:]
        x, y, c, _ = _place()
        cps = []
        for i in range(n):
            half = grads[i].shape[1] // 2
            cps.append(_remote(srcs[i].at[:, pl.ds((1 - c) * half, half), :], outs[i], send_sems, recv_sems, i, (x, y, 1 - c)))
        for cp in cps:
            cp.start()
        for cp in cps:
            cp.wait()

    return pl.pallas_call(
        body, name=f"grad_swap_halves_{tag}", in_specs=[_ANY] * n, out_specs=[_ANY] * n,
        out_shape=[jax.ShapeDtypeStruct((N_CHIPS, g.shape[1] // 2, g.shape[2]), g.dtype) for g in grads],
        scratch_shapes=[pltpu.SemaphoreType.DMA((n,)), pltpu.SemaphoreType.DMA((n,))],
        compiler_params=pltpu.CompilerParams(has_side_effects=True),
    )(*grads)


def _join_halves(fulls):
    n = len(fulls)

    def body(*refs):
        srcs, outs = refs[:n], refs[n:2 * n]
        send_sems, recv_sems = refs[2 * n:]
        x, y, c, _ = _place()
        sibling = (x, y, 1 - c)
        cps = []
        for i in range(n):
            h = fulls[i].shape[0] // 2
            cps.append(_remote(srcs[i].at[pl.ds(c * h, h), :], outs[i].at[pl.ds(c * h, h), :], send_sems, recv_sems, i,
                               sibling))
        for cp in cps:
            cp.start()
        for i in range(n):
            h = fulls[i].shape[0] // 2
            theirs = outs[i].at[pl.ds((1 - c) * h, h), :]
            _remote(theirs, theirs, send_sems, recv_sems, i, sibling).wait_recv()
        for cp in cps:
            cp.wait_send()

    return pl.pallas_call(
        body, name="grad_join_halves", in_specs=[_ANY] * n, out_specs=[_ANY] * n,
        out_shape=[jax.ShapeDtypeStruct(f.shape, f.dtype) for f in fulls],
        input_output_aliases={i: i for i in range(n)},
        scratch_shapes=[pltpu.SemaphoreType.DMA((n,)), pltpu.SemaphoreType.DMA((n,))],
        compiler_params=pltpu.CompilerParams(has_side_effects=True),
    )(*fulls)


def _half_tile(h):
    return h if h <= 512 else _pick_tile(h, 512, ROW_ALIGN)


def _sum_halves(g, r1, c_idx, name):
    _, R, C = g.shape
    H = R // 2
    tr = _half_tile(H)
    nblk = H // tr

    def body(c_ref, g_ref, r_ref, p_ref):
        p_ref[...] = (g_ref[...] + r_ref[...]).astype(BF16)

    half = pl.BlockSpec((None, tr, C), lambda s, i, c_ref: (s, c_ref[0] * nblk + i, 0))
    plain = pl.BlockSpec((None, tr, C), lambda s, i, c_ref: (s, i, 0))
    return pl.pallas_call(
        body, name=name,
        grid_spec=pltpu.PrefetchScalarGridSpec(num_scalar_prefetch=1, grid=(N_CHIPS, nblk), in_specs=[half, plain],
                                               out_specs=plain),
        out_shape=jax.ShapeDtypeStruct((N_CHIPS, H, C), BF16),
        compiler_params=_params(("parallel", "parallel")),
    )(c_idx, g, r1)


def _sum_chips(g, r1, r2, idx, name):
    _, R, C = g.shape
    H = R // 2
    tr = _half_tile(H)
    nblk = H // tr

    def body(idx_ref, g_ref, r1_ref, r2_ref, o_ref):
        o_ref[...] = (((g_ref[...] + r1_ref[...]) + r2_ref[0].astype(F32)) + r2_ref[1].astype(F32)) + r2_ref[2].astype(F32)

    return pl.pallas_call(
        body, name=name,
        grid_spec=pltpu.PrefetchScalarGridSpec(
            num_scalar_prefetch=1, grid=(nblk,),
            in_specs=[pl.BlockSpec((None, tr, C), lambda i, idx_ref: (idx_ref[0], idx_ref[1] * nblk + i, 0)),
                      pl.BlockSpec((None, tr, C), lambda i, idx_ref: (idx_ref[0], i, 0)),
                      pl.BlockSpec((3, tr, C), lambda i, idx_ref: (0, i, 0))],
            out_specs=pl.BlockSpec((tr, C), lambda i, idx_ref: (idx_ref[1] * nblk + i, 0))),
        out_shape=jax.ShapeDtypeStruct((R, C), F32),
        compiler_params=_params(("parallel",)),
    )(idx, g, r1, r2)


def _all_reduce_small(v, n_fold, fold_rows, fold_at):
    M, N = v.shape

    def body(x_ref, tot_ref, fold_ref, all_ref, send_sems, recv_sems, local_sem):
        x, y, c, chips = _place()
        me, sibling = (x, y, c), (x, y, 1 - c)

        def rows(px, py, pc):
            return all_ref.at[pl.ds((4 * px + 2 * py + pc) * M, M), :]

        def copy(k, block, to, src=None):
            return _remote(rows(*block) if src is None else src, rows(*block), send_sems, recv_sems, k, to)

        mine = pltpu.make_async_copy(x_ref, rows(*me), local_sem)
        mine.start()
        first = [copy(0, me, sibling, src=x_ref)]
        first += [copy(1 + j, me, (*chip, c), src=x_ref) for j, chip in enumerate(chips)]
        for cp in first:
            cp.start()
        passed = [copy(4 + j, (*chip, c), sibling) for j, chip in enumerate(chips)]
        for j, chip in enumerate(chips):
            copy(1 + j, (*chip, c), me).wait_recv()
            passed[j].start()
        copy(0, sibling, me).wait_recv()
        for j, chip in enumerate(chips):
            copy(4 + j, (*chip, 1 - c), me).wait_recv()
        for cp in first + passed:
            cp.wait_send()
        mine.wait()
        tot = all_ref[0:M, :]
        for d in range(1, 8):
            tot = tot + all_ref[d * M:(d + 1) * M, :]
        tot_ref[...] = tot
        f = tot[fold_at:fold_at + fold_rows, :]
        for e in range(1, n_fold):
            f = f + tot[fold_at + e * fold_rows:fold_at + (e + 1) * fold_rows, :]
        fold_ref[...] = f

    vm = pl.BlockSpec(memory_space=pltpu.VMEM)
    return pl.pallas_call(
        body, name="all_reduce_small", in_specs=[vm], out_specs=[vm, vm],
        out_shape=[jax.ShapeDtypeStruct((M, N), F32), jax.ShapeDtypeStruct((fold_rows, N), F32)],
        scratch_shapes=[pltpu.VMEM((8 * M, N), F32), pltpu.SemaphoreType.DMA((7,)), pltpu.SemaphoreType.DMA((7,)),
                        pltpu.SemaphoreType.DMA],
        compiler_params=pltpu.CompilerParams(has_side_effects=True, vmem_limit_bytes=VMEM_LIMIT),
    )(v)


def _as_rows(a, width):
    flat = a.reshape(-1)
    pad = (-flat.shape[0]) % width
    if pad:
        flat = jnp.concatenate([flat, jnp.zeros((pad,), flat.dtype)])
    return flat.reshape(-1, width)


class _Layout:
    def __init__(self, width, total_mult):
        self.width, self.total_mult = width, total_mult
        self.offsets, self.shapes, self.rows = {}, {}, 0

    def add(self, name, shape):
        r = -(-math.prod(shape) // self.width)
        self.offsets[name], self.shapes[name] = (self.rows, r), tuple(shape)
        self.rows += r

    def align(self, mult):
        gap = (-self.rows) % mult
        if gap:
            self.offsets[f"_gap{self.rows}"], self.shapes[f"_gap{self.rows}"] = (self.rows, gap), (gap, self.width)
            self.rows += gap
        return self.rows

    def pack(self, pieces):
        self.align(self.total_mult)
        parts = [_as_rows(pieces[n].astype(F32), self.width) if n in pieces else jnp.zeros(self.shapes[n], F32)
                 for n in self.offsets]
        return jnp.concatenate(parts, axis=0)

    def unpack(self, buf, name):
        off, r = self.offsets[name]
        shape = self.shapes[name]
        return buf[off:off + r].reshape(-1)[:math.prod(shape)].reshape(shape)


_BIG = ["ffn1_w1", "ffn1_w3", "ffn1_w2", "w_in", "ssm_glu_a", "ssm_glu_b", "w_out", "ffn2_w1", "ffn2_w3", "ffn2_w2"]
_TRANSPOSED = {"ffn1_w1", "ffn1_w3", "ffn2_w1", "ffn2_w3"}
_SMALL = ["ffn1_norm", "mix_norm", "ffn2_norm", "final_norm", "attn_sinks", "ssm_a_re", "ssm_a_im", "ssm_log_step",
          "ssm_b_re", "ssm_b_im", "ssm_c_re", "ssm_c_im", "ssm_d"]
_WEIGHTS = ["meta_tokens", "ffn1_norm", "ffn1_w1", "ffn1_w3", "ffn1_w2", "mix_norm", "w_in", "attn_sinks", "ssm_a_re",
            "ssm_a_im", "ssm_log_step", "ssm_b_re", "ssm_b_im", "ssm_c_re", "ssm_c_im", "ssm_d", "ssm_glu_a",
            "ssm_glu_b", "w_out", "ffn2_norm", "ffn2_w1", "ffn2_w3", "ffn2_w2", "final_norm"]


def _kv_interleave(w, kv_heads):
    kvw = kv_heads * HEAD_DIM
    lead = w.shape[:-1]
    k = w[..., 0:kvw].reshape(lead + (kv_heads, 1, HEAD_DIM))
    v = w[..., kvw:2 * kvw].reshape(lead + (kv_heads, 1, HEAD_DIM))
    return jnp.concatenate([jnp.concatenate([k, v], axis=-2).reshape(lead + (2 * kvw,)), w[..., 2 * kvw:]], axis=-1)


def _kv_deinterleave(w, kv_heads):
    kvw = kv_heads * HEAD_DIM
    lead = w.shape[:-1]
    kv = w[..., 0:2 * kvw].reshape(lead + (kv_heads, 2, HEAD_DIM))
    return jnp.concatenate([kv[..., 0, :].reshape(lead + (kvw,)), kv[..., 1, :].reshape(lead + (kvw,)), w[..., 2 * kvw:]],
                           axis=-1)


def _step(x, target, w, m, v):
    B, S, D = x.shape
    L = S + N_META
    T = B * L
    H = D // HEAD_DIM
    KV = H // Q_PER_KV
    SW = D // 2
    tm = _pick_tile(L, ROW_TILE_CAP, ROW_ALIGN)
    rc = _pick_tile(L, ROW_TILE_CAP // B, 4) * B
    tw = _pick_tile(T, 6 * ROW_TILE_CAP, ROW_ALIGN)
    tm2 = _pick_tile(T, 2 * ROW_TILE_CAP, ROW_ALIGN)
    my_c = lax.axis_index("c")
    my_slot = 2 * lax.axis_index("x") + lax.axis_index("y")

    groups = {"ffn1": ["ffn1_w1", "ffn1_w3", "ffn1_w2"], "mix": ["w_in", "ssm_glu_a", "ssm_glu_b", "w_out"],
              "ffn2": ["ffn2_w1", "ffn2_w3", "ffn2_w2"]}
    waves = {"first": ["ffn1_w1", "ffn1_w3"], "early": ["ffn1_w2"] + groups["mix"], "late": groups["ffn2"]}
    def own_layout(a, n):
        return jnp.swapaxes(a[0], 0, 1) if n in _TRANSPOSED else a[0]

    shards = {n: own_layout(w[n], n).astype(BF16) for n in _BIG}
    gathered = _gather_weights([shards[n] for n in waves["first"]] + [w["meta_tokens"]])
    ws = dict(zip(waves["first"], gathered[:-1]))
    meta = jnp.transpose(gathered[-1], (1, 0, 2)).reshape(N_META, D)

    def arrive(wave, landed):
        mine = [shards[n] for n in waves[wave]]
        ws.update(zip(waves[wave], _forward_halves(landed, mine, wave)))

    g_ffn1, g_mix, g_ffn2 = w["ffn1_norm"], w["mix_norm"], w["ffn2_norm"]
    g_final = w["final_norm"].reshape(1, D)

    h0, n_ffn1 = _embed_norm(x, meta, g_ffn1, tm, "ffn1_norm")

    def ffn_fwd(h, g, tag, carry=None, n=None):
        if n is None:
            n = _rmsnorm_fwd(h, g, tm, f"{tag}_norm")
        riders = None if carry is None else _gather_riders([shards[k] for k in waves[carry]])
        out = _ffn_up(n, ws[f"{tag}_w1"], ws[f"{tag}_w3"], tm2, f"{tag}_up", riders)
        if carry is not None:
            out, landed = out
            arrive(carry, landed)
        a, c, s = out
        return _ffn_down(s, ws[f"{tag}_w2"], h, tm, f"{tag}_down"), (n, a, c, s)

    h1, saved1 = ffn_fwd(h0, g_ffn1, "ffn1", carry="early", n=n_ffn1)
    w_kvu = _kv_interleave(ws["w_in"][1], KV)
    hn = _rmsnorm_fwd(h1, g_mix, tm, "mix_norm")
    q = _mm_colslots(hn, ws["w_in"], BF16, "w_in_q", tm2, first=0, count=1, scale=HEAD_DIM ** -0.5)
    kvu = _mm_plain(hn, w_kvu, "nn", F32, "w_in_kvu", tm2)
    gates = _mm_colslots(hn, ws["w_in"], F32, "w_in_gates", tm2, first=2, count=2)

    sinks = w["attn_sinks"].reshape(KV, Q_PER_KV, 1, 1)
    sink_row = jnp.broadcast_to(sinks.reshape(KV, 1, Q_PER_KV, 1), (KV, 1, Q_PER_KV, BLOCK)).reshape(KV, 1, Q_PER_KV * BLOCK)
    sink_meta = jnp.broadcast_to(sinks, (KV, Q_PER_KV, N_META, 1)).reshape(KV, Q_PER_KV * N_META, 1)
    (attn,), landed = _attn_fwd(q, kvu, sink_row, sink_meta, B, "attn_fwd",
                                _gather_riders([shards[k] for k in waves["late"]]))
    arrive("late", landed)

    def to_time_major(a2d):
        return jnp.transpose(a2d.reshape(B, L, a2d.shape[-1]), (1, 0, 2)).reshape(T, a2d.shape[-1])

    def to_batch_major(a2d):
        return jnp.transpose(a2d.reshape(L, B, a2d.shape[-1]), (1, 0, 2)).reshape(T, a2d.shape[-1])

    ssm_args = (w["ssm_a_re"][0], w["ssm_a_im"][0], w["ssm_log_step"][0], w["ssm_b_re"][0], w["ssm_b_im"][0],
                w["ssm_c_re"][0], w["ssm_c_im"][0])
    (lam, bmat, cmat), ssm_vjp = jax.vjp(_ssm_matrices, *ssm_args)
    bmat16, cmat16 = bmat.astype(BF16), cmat.astype(BF16)
    u_t = to_time_major(kvu[:, SW:])
    y_t, xs = _ssm_fwd(u_t, bmat16, cmat16, w["ssm_d"], _scan_tables(lam, B, False), B, rc, "ssm_fwd")
    y0 = to_batch_major(y_t)
    yg = _gelu_fwd(y0, tm, "gelu_fwd")
    ga = _mm_colslots(yg, ws["ssm_glu_a"], F32, "glu_a", tm2)
    gb = _mm_colslots(yg, ws["ssm_glu_b"], F32, "glu_b", tm2)
    merged = _merge_fwd(gates, attn, ga, gb, tm, "merge_fwd")
    h2 = _mm_rowslots(merged, ws["w_out"], h1, tm2, "w_out")
    h3, saved2 = ffn_fwd(h2, g_ffn2, "ffn2")
    dh3, dh3b, dg_final, loss_row = _loss_head(h3, g_final, target, tm, "loss_head")

    grads, swapped, received = {}, {}, {}
    c_idx = my_c.reshape(1).astype(jnp.int32)
    idx = jnp.stack([my_slot, my_c]).astype(jnp.int32)

    def swap_riders(group):
        return _swap_riders([grads[n] for n in groups[group]])

    def exchange_riders(group):
        names = groups[group]
        if names[0] not in swapped:
            swapped.update(zip(names, _swap_halves([grads[n] for n in names], group)))
        return _exchange_riders([_sum_halves(grads[n], swapped[n], c_idx, f"grad_sum_halves_{n}") for n in names])

    def ffn_bwd(h, g, saved, dh, dhb, tag, dhidden_carries=None, dn_carries=None, last=False):
        n, a, c, s = saved
        w1, w3, w2 = ws[f"{tag}_w1"], ws[f"{tag}_w3"], ws[f"{tag}_w2"]
        grads[f"{tag}_w2"] = _wgrad_hidden_rows(s, dhb, tw, f"{tag}_dw2", 0.5)
        if dhidden_carries is None:
            da, dc = _ffn_dhidden(dhb, w2, a, c, tm2, f"{tag}_dhidden")
        else:
            (da, dc), got = _ffn_dhidden(dhb, w2, a, c, tm2, f"{tag}_dhidden", exchange_riders(dhidden_carries[1]))
            received.update(zip(groups[dhidden_carries[1]], got))
        grads[f"{tag}_w1"] = _wgrad_hidden_rows(da, n, tw, f"{tag}_dw1", 1.0)
        grads[f"{tag}_w3"] = _wgrad_hidden_rows(dc, n, tw, f"{tag}_dw3", 1.0)
        kind, group = dn_carries
        riders = swap_riders(group) if kind == "swap" else exchange_riders(group)
        (dh_in, dhb_in, grads[f"{tag}_norm"]), got = _ffn_dn(da, w1, dc, w3, h, g, dh, tm, f"{tag}_dn", riders,
                                                               B if last else None)
        return dh_in, dhb_in, got

    dh2, dh2b, got = ffn_bwd(h2, g_ffn2, saved2, dh3, dh3b, "ffn2", dn_carries=("swap", "ffn2"))
    swapped.update(zip(groups["ffn2"], got))

    grads["w_out"] = _wgrad_rowslots(merged, dh2b, tw, "dw_out")
    dattn, dgat, dgss, dga, dgb = _merge_bwd(dh2b, ws["w_out"], gates, attn, ga, gb, tm, "merge_bwd")
    grads["ssm_glu_a"] = _wgrad_colslots(yg, dga, tw, "dglu_a")
    grads["ssm_glu_b"] = _wgrad_colslots(yg, dgb, tw, "dglu_b")
    dy0 = _gelu_bwd([(dga, ws["ssm_glu_a"]), (dgb, ws["ssm_glu_b"])], y0, tm2, "gelu_bwd")
    du_t, dbmat, dcmat, dlam, dd = _ssm_bwd(to_time_major(dy0), u_t, xs, bmat16, cmat16, w["ssm_d"],
                                            _scan_tables(lam, B, True), B, rc, "ssm_bwd")
    d_ssm = ssm_vjp((dlam[:, 0, :], dbmat, dcmat))
    for n, gval in zip(["ssm_a_re", "ssm_a_im", "ssm_log_step", "ssm_b_re", "ssm_b_im", "ssm_c_re", "ssm_c_im"], d_ssm):
        grads[n] = gval[None]
    grads["ssm_d"] = dd

    (dq, dkv, dsink), got = _attn_bwd(q, kvu, attn, dattn, sink_row, sink_meta, B, "attn_bwd",
                                      exchange_riders("ffn2"))
    received.update(zip(groups["ffn2"], got))
    grads["attn_sinks"] = dsink[:, 0:Q_PER_KV, 0].reshape(1, H)
    dkvu = jnp.concatenate([dkv, to_batch_major(du_t).astype(BF16)], axis=1)
    pieces = [dq, dkvu, dgat, dgss]
    dw_in = [_wgrad_plain(hn, p, f"dw_in_{k}", tw) for k, p in enumerate(pieces)]
    dw_in[1] = _kv_deinterleave(dw_in[1], KV)
    grads["w_in"] = jnp.stack(dw_in)
    w_in_parts = [ws["w_in"][0], w_kvu, ws["w_in"][2], ws["w_in"][3]]
    whole = _once((D, D), lambda i: (0, 0))
    (dh1, dh1b, grads["mix_norm"]), swap_mix = _mm_norm_bwd(
        "dhn", "nt", [(p, _spec((tm, D), lambda i: (i, 0)), wp, whole) for p, wp in zip(pieces, w_in_parts)],
        h1, g_mix, dh2, tm, swap_riders("mix"))
    swapped.update(zip(groups["mix"], swap_mix))
    grad_x, dmeta_rows, got = ffn_bwd(h0, g_ffn1, saved1, dh1, dh1b, "ffn1", dhidden_carries=("exchange", "mix"),
                                      dn_carries=("exchange", "ffn1"), last=True)
    received.update(zip(groups["ffn1"], got))

    grads["final_norm"] = dg_final
    slay = _Layout(D, 8)
    for n in _SMALL:
        slay.add(n, w[n].shape)
    slay.add("loss", (1, D))
    meta_at = slay.align(8)
    slay.add("meta", (N_META, D))
    small = slay.pack({**{n: grads[n] for n in _SMALL}, "loss": loss_row, "meta": dmeta_rows})
    tot_small, dmeta = _all_reduce_small(small, 1, N_META, meta_at)
    loss = slay.unpack(tot_small, "loss")[0, 0]
    for n in _SMALL:
        grads[n] = slay.unpack(tot_small, n)
    cw = D // N_CHIPS
    grads["meta_tokens"] = lax.dynamic_slice_in_dim(dmeta, my_slot * cw, cw, axis=1)

    fulls = [_sum_chips(grads[n], swapped[n], received[n], idx, f"grad_sum_chips_{n}") for n in _BIG]
    for n, f in zip(_BIG, _join_halves(fulls)):
        grads[n] = f

    delta, new_m, new_v = {}, {}, {}
    for n in _BIG + ["meta_tokens"]:
        if n in _TRANSPOSED:
            flip = lambda a: jnp.swapaxes(a, -1, -2)
            outs = _adamw(flip(w[n]), grads[n], flip(m[n]), flip(v[n]), f"adamw_{n}")
            delta[n], new_m[n], new_v[n] = (flip(o) for o in outs)
            grads[n] = flip(grads[n])[None]
        else:
            delta[n], new_m[n], new_v[n] = _adamw(w[n], grads[n], m[n], v[n], f"adamw_{n}")
            grads[n] = grads[n].reshape(w[n].shape)

    def flat2d(a):
        return a.reshape(-1, a.shape[-1])

    d_, m_, v_ = _adamw_small([flat2d(w[n]) for n in _SMALL], [flat2d(grads[n]) for n in _SMALL],
                              [flat2d(m[n]) for n in _SMALL], [flat2d(v[n]) for n in _SMALL], "adamw_small")
    for i, n in enumerate(_SMALL):
        shp = w[n].shape
        delta[n], new_m[n], new_v[n] = d_[i].reshape(shp), m_[i].reshape(shp), v_[i].reshape(shp)
        grads[n] = grads[n].reshape(shp)

    return (loss, grad_x, *[grads[n] for n in _WEIGHTS], *[delta[n] for n in _WEIGHTS],
            *[new_m[n] for n in _WEIGHTS], *[new_v[n] for n in _WEIGHTS])


def kernel(x, meta_tokens, ffn1_norm, ffn1_w1, ffn1_w3, ffn1_w2, mix_norm, w_in, attn_sinks, ssm_a_re, ssm_a_im, ssm_log_step, ssm_b_re, ssm_b_im, ssm_c_re, ssm_c_im, ssm_d, ssm_glu_a, ssm_glu_b, w_out, ffn2_norm, ffn2_w1, ffn2_w3, ffn2_w2, final_norm, loss_target, m_meta_tokens, m_ffn1_norm, m_ffn1_w1, m_ffn1_w3, m_ffn1_w2, m_mix_norm, m_w_in, m_attn_sinks, m_ssm_a_re, m_ssm_a_im, m_ssm_log_step, m_ssm_b_re, m_ssm_b_im, m_ssm_c_re, m_ssm_c_im, m_ssm_d, m_ssm_glu_a, m_ssm_glu_b, m_w_out, m_ffn2_norm, m_ffn2_w1, m_ffn2_w3, m_ffn2_w2, m_final_norm, v_meta_tokens, v_ffn1_norm, v_ffn1_w1, v_ffn1_w3, v_ffn1_w2, v_mix_norm, v_w_in, v_attn_sinks, v_ssm_a_re, v_ssm_a_im, v_ssm_log_step, v_ssm_b_re, v_ssm_b_im, v_ssm_c_re, v_ssm_c_im, v_ssm_d, v_ssm_glu_a, v_ssm_glu_b, v_w_out, v_ffn2_norm, v_ffn2_w1, v_ffn2_w3, v_ffn2_w2, v_final_norm):
    args = locals()
    w = {n: args[n] for n in _WEIGHTS}
    m = {n: args["m_" + n] for n in _WEIGHTS}
    v = {n: args["v_" + n] for n in _WEIGHTS}
    return _step(x, loss_target, w, m, v)
```

```python
import functools
import math

import jax
import jax.numpy as jnp
from jax import lax
from jax.experimental import pallas as pl
from jax.experimental.pallas import tpu as pltpu

F32 = jnp.float32
BF16 = jnp.bfloat16
MESH_IDS = pl.DeviceIdType.MESH

N_CHIPS = 4
N_META = 16
HEAD_DIM = 64
Q_PER_KV = 4
QW = Q_PER_KV * HEAD_DIM
BLOCK = 128
SSM_GROUP = 16
SSM_STATE = 64
SSM_LANES = 128
GROUPS_PER_COL = SSM_LANES // SSM_GROUP
STATE_LANES = GROUPS_PER_COL * SSM_STATE
NORM_EPS = 1e-6
NEG_INF = -1e30
ADAM_LR, ADAM_B1, ADAM_B2, ADAM_EPS, ADAM_WD, ADAM_STEP = 0.001, 0.9, 0.999, 1e-08, 0.01, 10
GELU_C = math.sqrt(2.0 / math.pi)
ROW_ALIGN = 16
VMEM_LIMIT = 56 * 1024 * 1024
ROW_TILE_CAP = 688

_NN = (((1,), (0,)), ((), ()))
_NT = (((1,), (1,)), ((), ()))
_TN = (((0,), (0,)), ((), ()))
_DIMS = {"nn": _NN, "nt": _NT, "tn": _TN}


def _params(sem, **kw):
    return pltpu.CompilerParams(dimension_semantics=sem, vmem_limit_bytes=VMEM_LIMIT, **kw)


def _pick_tile(n, cap, mult):
    best = None
    for t in range(mult, min(n, cap) + 1, mult):
        if n % t == 0:
            best = t
    if best is None:
        raise ValueError(f"no tile for {n} (cap {cap}, multiple of {mult})")
    return best


def _sigmoid(x):
    return 0.5 * jnp.tanh(0.5 * x) + 0.5


def _spec(block, index_map):
    return pl.BlockSpec(block, index_map)


def _sum_dots(ins, mode):
    tot = None
    for p in range(len(ins) // 2):
        a_ref, b_ref = ins[2 * p], ins[2 * p + 1]
        for sl in ([None] if len(b_ref.shape) == 2 else range(b_ref.shape[0])):
            if sl is None:
                a, b = a_ref[...], b_ref[...]
            elif len(a_ref.shape) == 3:
                a, b = a_ref[sl], b_ref[sl]
            else:
                width = a_ref.shape[1] // b_ref.shape[0]
                a, b = a_ref[:, sl * width:(sl + 1) * width], b_ref[sl]
            d = lax.dot_general(a.astype(BF16), b.astype(BF16), _DIMS[mode], preferred_element_type=F32)
            tot = d if tot is None else tot + d
    return tot


def _mm(name, grid, kaxis, mode, pairs, out_shape, out_spec, scale=1.0, res=None):
    npairs = len(pairs)
    has_res = res is not None
    gk = 1 if kaxis is None else grid[kaxis]
    acc_shape = tuple(d for d in out_spec.block_shape if d is not None)

    def body(*refs):
        res_ref = refs[2 * npairs] if has_res else None
        o_ref = refs[2 * npairs + has_res]
        tot = _sum_dots(refs[:2 * npairs], mode)

        def finish(acc):
            r = acc * scale if scale != 1.0 else acc
            if has_res:
                r = res_ref[...] + r
            o_ref[...] = r.astype(o_ref.dtype)

        if gk == 1:
            finish(tot)
        else:
            acc_ref = refs[-1]
            k = pl.program_id(kaxis)

            @pl.when(k == 0)
            def _():
                acc_ref[...] = tot

            @pl.when(k > 0)
            def _():
                acc_ref[...] += tot

            @pl.when(k == gk - 1)
            def _():
                finish(acc_ref[...])

    in_specs, args = [], []
    for a, a_spec, b, b_spec in pairs:
        in_specs += [a_spec, b_spec]
        args += [a, b]
    if has_res:
        in_specs.append(res[1])
        args.append(res[0])
    sem = tuple("arbitrary" if ax == kaxis else "parallel" for ax in range(len(grid)))
    return pl.pallas_call(
        body, name=name, grid=grid, in_specs=in_specs, out_specs=out_spec, out_shape=out_shape,
        scratch_shapes=[pltpu.VMEM(acc_shape, F32)] if gk > 1 else [],
        compiler_params=_params(sem),
    )(*args)


def _mm_plain(a, b, mode, out_dtype, name, tm, scale=1.0):
    M, K = a.shape
    N = b.shape[1] if mode == "nn" else b.shape[0]
    return _mm(name, (M // tm,), None, mode,
               [(a, _spec((tm, K), lambda i: (i, 0)), b, _spec(b.shape, lambda i: (0, 0)))],
               jax.ShapeDtypeStruct((M, N), out_dtype), _spec((tm, N), lambda i: (i, 0)), scale=scale)


def _wgrad_plain(a, b, name, tr):
    R, M = a.shape
    N = b.shape[1]
    return _mm(name, (R // tr,), 0, "tn",
               [(a, _spec((tr, M), lambda r: (r, 0)), b, _spec((tr, N), lambda r: (r, 0)))],
               jax.ShapeDtypeStruct((M, N), F32), _spec((M, N), lambda r: (0, 0)))


def _for_real_rows(tile, tpe, tm, hbm, buf, fn):
    tb, tj = tile // tpe, tile % tpe

    @pl.when(tj == 0)
    def _():
        fn(hbm.at[tb, pl.ds(0, tm - N_META), :], buf.at[pl.ds(N_META, tm - N_META), :])

    @pl.when(tj > 0)
    def _():
        fn(hbm.at[tb, pl.ds(tj * tm - N_META, tm), :], buf)


def _embed_norm(x, meta, g, tm, name):
    B, S, D = x.shape
    L = S + N_META
    T = B * L
    nt = T // tm
    tpe = L // tm

    def body(x_hbm, meta_ref, g_ref, h_ref, n_ref, xbuf, sems):
        i = pl.program_id(0)
        slot = i % 2

        def fetch(tile, sl, act):
            _for_real_rows(tile, tpe, tm, x_hbm, xbuf.at[sl], lambda src, dst: act(pltpu.make_async_copy(src, dst, sems.at[sl])))

        @pl.when(i == 0)
        def _():
            fetch(i, slot, lambda cp: cp.start())

        @pl.when(i + 1 < nt)
        def _():
            fetch(i + 1, 1 - slot, lambda cp: cp.start())

        fetch(i, slot, lambda cp: cp.wait())

        @pl.when(i % tpe == 0)
        def _():
            xbuf[slot, 0:N_META, :] = meta_ref[...]

        hv = xbuf[slot]
        h_ref[...] = hv
        r = lax.rsqrt(jnp.mean(hv * hv, axis=-1, keepdims=True) + NORM_EPS)
        n_ref[...] = ((hv * r) * g_ref[...]).astype(BF16)

    row = pl.BlockSpec((tm, D), lambda i: (i, 0))
    return pl.pallas_call(
        body, name=name, grid=(nt,),
        in_specs=[pl.BlockSpec(memory_space=pl.ANY), pl.BlockSpec((N_META, D), lambda i: (0, 0)),
                  pl.BlockSpec((1, D), lambda i: (0, 0))],
        out_specs=[row, row],
        out_shape=[jax.ShapeDtypeStruct((T, D), F32), jax.ShapeDtypeStruct((T, D), BF16)],
        scratch_shapes=[pltpu.VMEM((2, tm, D), F32), pltpu.SemaphoreType.DMA((2,))],
        compiler_params=_params(("arbitrary",)),
    )(x, meta, g)


def _rmsnorm_fwd(h, g, tm, name):
    T, D = h.shape

    def body(h_ref, g_ref, o_ref):
        x = h_ref[...]
        r = lax.rsqrt(jnp.mean(x * x, axis=-1, keepdims=True) + NORM_EPS)
        o_ref[...] = ((x * r) * g_ref[...]).astype(BF16)

    return pl.pallas_call(
        body, name=name, grid=(T // tm,),
        in_specs=[pl.BlockSpec((tm, D), lambda i: (i, 0)), pl.BlockSpec((1, D), lambda i: (0, 0))],
        out_specs=pl.BlockSpec((tm, D), lambda i: (i, 0)),
        out_shape=jax.ShapeDtypeStruct((T, D), BF16),
        compiler_params=_params(("parallel",)),
    )(h, g)


def _fold8(x):
    return jnp.sum(x.reshape(x.shape[0] // 8, 8, x.shape[1]), axis=0)


def _mm_norm_bwd(name, mode, pairs, h, g, dres, tm, riders=None, examples=None, cat=None):
    T, D = h.shape
    nt = T // tm
    npairs = len(pairs)
    tpe = None if examples is None else T // examples // tm
    ncat = 0 if cat is None else len(cat[0])
    n_in = 2 * npairs + (ncat + 1 if cat is not None else 0)

    def body(*refs):
        if examples is None:
            h_ref, g_ref, dres_ref, dh_ref, dhb_ref, dg_ref, acc_ref = refs[n_in:]
        else:
            h_ref, g_ref, dres_ref, dx_hbm, dmeta_ref, dg_ref, acc_ref, dbuf, sems = refs[n_in:]
        i = pl.program_id(0)
        x = h_ref[...]
        r = lax.rsqrt(jnp.mean(x * x, axis=-1, keepdims=True) + NORM_EPS)
        xhat = x * r
        dy = _sum_dots(refs[:2 * npairs], mode)
        if cat is not None:
            wide = jnp.concatenate([refs[2 * npairs + k][...].astype(BF16) for k in range(ncat)], axis=1)
            d = lax.dot_general(wide, refs[2 * npairs + ncat][...].astype(BF16), _DIMS[mode], preferred_element_type=F32)
            dy = d if dy is None else dy + d
        dxhat = dy * g_ref[...]
        dx = r * (dxhat - xhat * jnp.mean(dxhat * xhat, axis=-1, keepdims=True))
        dh = dres_ref[...] + dx
        if examples is None:
            dh_ref[...] = dh
            dhb_ref[...] = dh.astype(BF16)
        else:
            slot = i % 2

            def push(tile, sl, act):
                _for_real_rows(tile, tpe, tm, dx_hbm, dbuf.at[sl],
                               lambda dst, src: act(pltpu.make_async_copy(src, dst, sems.at[sl])))

            dbuf[slot] = dh
            push(i, slot, lambda cp: cp.start())

            @pl.when(i > 0)
            def _():
                push(i - 1, 1 - slot, lambda cp: cp.wait())

            @pl.when(i == nt - 1)
            def _():
                push(i, slot, lambda cp: cp.wait())

            @pl.when(i == 0)
            def _():
                dmeta_ref[...] = dh[0:N_META]

            @pl.when((i > 0) & (i % tpe == 0))
            def _():
                dmeta_ref[...] += dh[0:N_META]
        part = _fold8(dy * xhat)

        @pl.when(i == 0)
        def _():
            acc_ref[...] = part

        @pl.when(i > 0)
        def _():
            acc_ref[...] += part

        @pl.when(i == nt - 1)
        def _():
            dg_ref[...] = jnp.sum(acc_ref[...], axis=0, keepdims=True)

    row = pl.BlockSpec((tm, D), lambda i: (i, 0))
    vec = pl.BlockSpec((1, D), lambda i: (0, 0))
    in_specs, args = [], []
    for a, a_spec, b, b_spec in pairs:
        in_specs += [a_spec, b_spec]
        args += [a, b]
    if cat is not None:
        in_specs += [s for _, s in cat[0]] + [cat[2]]
        args += [a for a, _ in cat[0]] + [cat[1]]
    if examples is None:
        return _call(body, name, (nt,), in_specs + [row, vec, row], [row, row, vec],
                     [jax.ShapeDtypeStruct((T, D), F32), jax.ShapeDtypeStruct((T, D), BF16), jax.ShapeDtypeStruct((1, D), F32)],
                     [pltpu.VMEM((8, D), F32)], ("arbitrary",), (*args, h, g, dres), riders)
    S = T // examples - N_META
    return _call(body, name, (nt,), in_specs + [row, vec, row],
                 [_ANY, pl.BlockSpec((N_META, D), lambda i: (0, 0)), vec],
                 [jax.ShapeDtypeStruct((examples, S, D), F32), jax.ShapeDtypeStruct((N_META, D), F32),
                  jax.ShapeDtypeStruct((1, D), F32)],
                 [pltpu.VMEM((8, D), F32), pltpu.VMEM((2, tm, D), F32), pltpu.SemaphoreType.DMA((2,))],
                 ("arbitrary",), (*args, h, g, dres), riders)


def _ffn_up(n, w1t, w3t, tm, name, riders=None):
    T, D = n.shape
    Fs = w1t.shape[1]

    def body(n_ref, w1_ref, w3_ref, a_ref, c_ref, s_ref):
        x = n_ref[...]
        a = lax.dot_general(x, w1_ref[...], _NT, preferred_element_type=F32)
        c = lax.dot_general(x, w3_ref[...], _NT, preferred_element_type=F32)
        a_ref[...] = a.astype(BF16)
        c_ref[...] = c.astype(BF16)
        s_ref[...] = (a * _sigmoid(a) * c).astype(BF16)

    w_spec = _spec((None, Fs, D), lambda s, i: (s, 0, 0))
    o_spec = _spec((None, tm, Fs), lambda s, i: (s, i, 0))
    o_shape = jax.ShapeDtypeStruct((N_CHIPS, T, Fs), BF16)
    return _call(body, name, (N_CHIPS, T // tm), [_spec((tm, D), lambda s, i: (i, 0)), w_spec, w_spec],
                 [o_spec, o_spec, o_spec], [o_shape, o_shape, o_shape], [], ("parallel", "parallel"), (n, w1t, w3t), riders)


def _ffn_down(s, w2, h, tm, name):
    _, T, Fs = s.shape
    D = w2.shape[2]
    row = _spec((tm, D), lambda i: (i, 0))
    return _mm(name, (T // tm,), None, "nn",
               [(s, _spec((N_CHIPS, tm, Fs), lambda i: (0, i, 0)), w2, _spec((N_CHIPS, Fs, D), lambda i: (0, 0, 0)))],
               jax.ShapeDtypeStruct((T, D), F32), row, scale=0.5, res=(h, row))


def _ffn_dhidden(dhb, w2, a, c, tm, name, riders=None):
    T, D = dhb.shape
    Fs = w2.shape[1]

    def body(dh_ref, w2_ref, a_ref, c_ref, da_ref, dc_ref):
        d = 0.5 * lax.dot_general(dh_ref[...], w2_ref[pl.program_id(1)], _NT, preferred_element_type=F32)
        av = a_ref[...].astype(F32)
        cv = c_ref[...].astype(F32)
        sg = _sigmoid(av)
        da_ref[...] = (d * cv * (sg * (1.0 + av * (1.0 - sg)))).astype(BF16)
        dc_ref[...] = (d * (av * sg)).astype(BF16)

    h_spec = _spec((None, tm, Fs), lambda i, s: (s, i, 0))
    o_shape = jax.ShapeDtypeStruct((N_CHIPS, T, Fs), BF16)
    return _call(body, name, (T // tm, N_CHIPS),
                 [_spec((tm, D), lambda i, s: (i, 0)), _once((N_CHIPS, Fs, D), lambda i, s: (0, 0, 0)), h_spec, h_spec],
                 [h_spec, h_spec], [o_shape, o_shape], [], ("parallel", "parallel"), (dhb, w2, a, c), riders)


def _wgrad_hidden_rows(s, dhb, tr, name, scale):
    _, T, Fs = s.shape
    D = dhb.shape[1]
    return _mm(name, (N_CHIPS, T // tr), 1, "tn",
               [(s, _spec((None, tr, Fs), lambda k, r: (k, r, 0)), dhb, _spec((tr, D), lambda k, r: (r, 0)))],
               jax.ShapeDtypeStruct((N_CHIPS, Fs, D), F32), _spec((None, Fs, D), lambda k, r: (k, 0, 0)), scale=scale)


def _once(block, index_map):
    return pl.BlockSpec(block, index_map, pipeline_mode=pl.Buffered(1))


def _ffn_dn(da, w1t, dc, w3t, h, g, dres, tm, name, riders=None, examples=None):
    _, T, Fs = da.shape
    D = w1t.shape[2]
    h_spec = _spec((N_CHIPS, tm, Fs), lambda i: (0, i, 0))
    w_spec = _once((N_CHIPS, Fs, D), lambda i: (0, 0, 0))
    return _mm_norm_bwd(name, "nn", [(da, h_spec, w1t, w_spec), (dc, h_spec, w3t, w_spec)], h, g, dres, tm, riders,
                        examples)


def _mm_side_by_side(a, w, mode, out_dtype, name, tm, first=0, count=N_CHIPS, scale=1.0):
    T, K = a.shape
    assert first % count == 0
    n = w.shape[2] if mode == "nn" else w.shape[1]

    def body(a_ref, w_ref, o_ref):
        av = a_ref[...].astype(BF16)
        for j in range(count):
            r = lax.dot_general(av, w_ref[j].astype(BF16), _DIMS[mode], preferred_element_type=F32)
            o_ref[:, j * n:(j + 1) * n] = (r * scale if scale != 1.0 else r).astype(o_ref.dtype)

    return pl.pallas_call(
        body, name=name, grid=(T // tm,),
        in_specs=[_spec((tm, K), lambda i: (i, 0)), _once((count,) + w.shape[1:], lambda i: (first // count, 0, 0))],
        out_specs=_spec((tm, count * n), lambda i: (i, 0)),
        out_shape=jax.ShapeDtypeStruct((T, count * n), out_dtype),
        compiler_params=_params(("parallel",)),
    )(a, w)


def _mm_colslots(a, w, out_dtype, name, tm, first=0, count=N_CHIPS, scale=1.0):
    return _mm_side_by_side(a, w, "nn", out_dtype, name, tm, first, count, scale)


def _wgrad_colslots(a, d, tr, name):
    T, K = a.shape
    Ns = d.shape[1] // N_CHIPS
    return _mm(name, (N_CHIPS, T // tr), 1, "tn",
               [(a, _spec((tr, K), lambda k, r: (r, 0)), d, _spec((tr, Ns), lambda k, r: (r, k)))],
               jax.ShapeDtypeStruct((N_CHIPS, K, Ns), F32), _spec((None, K, Ns), lambda k, r: (k, 0, 0)))


def _mm_rowslots(a, w, h, tm, name):
    T = a.shape[0]
    N = w.shape[2]
    row = _spec((tm, N), lambda i: (i, 0))
    return _mm(name, (T // tm,), None, "nn",
               [(a, _spec((tm, a.shape[1]), lambda i: (i, 0)), w, _once(w.shape, lambda i: (0, 0, 0)))],
               jax.ShapeDtypeStruct((T, N), F32), row, res=(h, row))


def _wgrad_rowslots(a, d, tr, name):
    T = a.shape[0]
    Ks = a.shape[1] // N_CHIPS
    N = d.shape[1]
    return _mm(name, (N_CHIPS, T // tr), 1, "tn",
               [(a, _spec((tr, Ks), lambda k, r: (r, k)), d, _spec((tr, N), lambda k, r: (r, 0)))],
               jax.ShapeDtypeStruct((N_CHIPS, Ks, N), F32), _spec((None, Ks, N), lambda k, r: (k, 0, 0)))


def _gelu_parts(x):
    inner = GELU_C * (x + 0.044715 * (x * x * x))
    t = jnp.tanh(inner)
    return t, GELU_C * (1.0 + 3.0 * 0.044715 * (x * x))


def _gelu_fwd(y, tm, name):
    T, W = y.shape

    def body(y_ref, o_ref):
        x = y_ref[...]
        t, _ = _gelu_parts(x)
        o_ref[...] = (0.5 * x * (1.0 + t)).astype(BF16)

    spec = pl.BlockSpec((tm, W), lambda i: (i, 0))
    return pl.pallas_call(body, name=name, grid=(T // tm,), in_specs=[spec], out_specs=spec,
                          out_shape=jax.ShapeDtypeStruct((T, W), BF16),
                          compiler_params=_params(("parallel",)))(y)


def _gelu_bwd(pairs, y, tm, name):
    T, W = y.shape
    npairs = len(pairs)

    def body(*refs):
        y_ref, o_ref = refs[2 * npairs], refs[2 * npairs + 1]
        x = y_ref[...]
        t, dinner = _gelu_parts(x)
        o_ref[...] = _sum_dots(refs[:2 * npairs], "nt") * (0.5 * (1.0 + t) + 0.5 * x * (1.0 - t * t) * dinner)

    spec = pl.BlockSpec((tm, W), lambda i: (i, 0))
    in_specs, args = [], []
    for d, w in pairs:
        in_specs += [_spec((tm, d.shape[1]), lambda i: (i, 0)), _once(w.shape, lambda i: (0, 0, 0))]
        args += [d, w]
    return pl.pallas_call(body, name=name, grid=(T // tm,), in_specs=in_specs + [spec], out_specs=spec,
                          out_shape=jax.ShapeDtypeStruct((T, W), F32),
                          compiler_params=_params(("parallel",)))(*args, y)


def _merge_cols(D):
    cb = 512 if D % 512 == 0 else D
    return cb, D // cb


def _merge_fwd(gates, attn, ga, gb, tm, name):
    T, D = attn.shape
    cb, nc = _merge_cols(D)

    def body(gat_ref, gss_ref, attn_ref, ga_ref, gb_ref, o_ref):
        ssm = ga_ref[...] * _sigmoid(gb_ref[...])
        o_ref[...] = (_sigmoid(gat_ref[...]) * attn_ref[...] + _sigmoid(gss_ref[...]) * ssm).astype(BF16)

    def col(block):
        return pl.BlockSpec((tm, cb), lambda i, j: (i, block * nc + j))

    return pl.pallas_call(
        body, name=name, grid=(T // tm, nc),
        in_specs=[col(0), col(1), col(0), col(0), col(0)],
        out_specs=col(0), out_shape=jax.ShapeDtypeStruct((T, D), BF16),
        compiler_params=_params(("parallel", "parallel")),
    )(gates, gates, attn, ga, gb)


def _merge_bwd(dhb, w_out, gates, attn, ga, gb, tm, name):
    T, D = attn.shape
    cb, nc = _merge_cols(D)
    Ks = w_out.shape[1]
    spb = cb // Ks

    def body(dh_ref, w_ref, gat_ref, gss_ref, attn_ref, ga_ref, gb_ref, dattn_ref, dgat_ref, dgss_ref, dga_ref, dgb_ref):
        dh = dh_ref[...]
        d = jnp.concatenate([lax.dot_general(dh, w_ref[s], _NT, preferred_element_type=F32) for s in range(spb)], axis=1)
        sa = _sigmoid(gat_ref[...])
        ss = _sigmoid(gss_ref[...])
        sb = _sigmoid(gb_ref[...])
        gav = ga_ref[...]
        dattn_ref[...] = d * sa
        dgat_ref[...] = (d * attn_ref[...] * (sa * (1.0 - sa))).astype(BF16)
        dgss_ref[...] = (d * (gav * sb) * (ss * (1.0 - ss))).astype(BF16)
        dssm = d * ss
        dga_ref[...] = (dssm * sb).astype(BF16)
        dgb_ref[...] = (dssm * gav * (sb * (1.0 - sb))).astype(BF16)

    def col(block):
        return pl.BlockSpec((tm, cb), lambda i, j: (i, block * nc + j))

    b16 = jax.ShapeDtypeStruct((T, D), BF16)
    return pl.pallas_call(
        body, name=name, grid=(T // tm, nc),
        in_specs=[pl.BlockSpec((tm, D), lambda i, j: (i, 0)), pl.BlockSpec((spb, Ks, D), lambda i, j: (j, 0, 0)),
                  col(0), col(1), col(0), col(0), col(0)],
        out_specs=[col(0)] * 5,
        out_shape=[jax.ShapeDtypeStruct((T, D), F32), b16, b16, b16, b16],
        compiler_params=_params(("parallel", "parallel")),
    )(dhb, w_out, gates, gates, attn, ga, gb)


def _loss_head(h, g, target, tm, name):
    T, D = h.shape
    B, S, _ = target.shape
    L = S + N_META
    nt = T // tm
    tpe = L // tm

    def body(h_ref, g_ref, t_hbm, dh_ref, dhb_ref, dg_ref, loss_ref, tbuf, acc_g, acc_l, sems):
        i = pl.program_id(0)
        j = i % tpe
        slot = i % 2

        def fetch(tile, sl, act):
            tb, tj = tile // tpe, tile % tpe

            @pl.when(tj == 0)
            def _():
                act(pltpu.make_async_copy(t_hbm.at[tb, pl.ds(0, tm - N_META), :],
                                          tbuf.at[sl, pl.ds(N_META, tm - N_META), :], sems.at[sl]))

            @pl.when(tj > 0)
            def _():
                act(pltpu.make_async_copy(t_hbm.at[tb, pl.ds(tj * tm - N_META, tm), :], tbuf.at[sl], sems.at[sl]))

        @pl.when(i == 0)
        def _():
            tbuf[:, 0:N_META, :] = jnp.zeros((2, N_META, D), F32)
            fetch(i, slot, lambda cp: cp.start())

        @pl.when(i + 1 < nt)
        def _():
            fetch(i + 1, 1 - slot, lambda cp: cp.start())

        fetch(i, slot, lambda cp: cp.wait())

        x = h_ref[...]
        gv = g_ref[...]
        r = lax.rsqrt(jnp.mean(x * x, axis=-1, keepdims=True) + NORM_EPS)
        xhat = x * r
        pos = j * tm + lax.broadcasted_iota(jnp.int32, (tm, 1), 0)
        err = jnp.where(pos >= N_META, xhat * gv - tbuf[slot], 0.0)
        dy = err * (1.0 / D)
        dxhat = dy * gv
        dh = r * (dxhat - xhat * jnp.mean(dxhat * xhat, axis=-1, keepdims=True))
        dh_ref[...] = dh
        dhb_ref[...] = dh.astype(BF16)
        pg = _fold8(dy * xhat)
        pe = _fold8(err * err)

        @pl.when(i == 0)
        def _():
            acc_g[...] = pg
            acc_l[...] = pe

        @pl.when(i > 0)
        def _():
            acc_g[...] += pg
            acc_l[...] += pe

        @pl.when(i == nt - 1)
        def _():
            dg_ref[...] = jnp.sum(acc_g[...], axis=0, keepdims=True)
            loss_ref[...] = jnp.full((1, D), (0.5 / D) * jnp.sum(acc_l[...]), F32)

    row = pl.BlockSpec((tm, D), lambda i: (i, 0))
    vec = pl.BlockSpec((1, D), lambda i: (0, 0))
    return pl.pallas_call(
        body, name=name, grid=(nt,),
        in_specs=[row, vec, pl.BlockSpec(memory_space=pl.ANY)], out_specs=[row, row, vec, vec],
        out_shape=[jax.ShapeDtypeStruct((T, D), F32), jax.ShapeDtypeStruct((T, D), BF16),
                   jax.ShapeDtypeStruct((1, D), F32), jax.ShapeDtypeStruct((1, D), F32)],
        scratch_shapes=[pltpu.VMEM((2, tm, D), F32), pltpu.VMEM((8, D), F32), pltpu.VMEM((8, D), F32),
                        pltpu.SemaphoreType.DMA((2,))],
        compiler_params=_params(("arbitrary",)),
    )(h, g, target)


def _heads_to_rows(blk):
    return jnp.concatenate([blk[:, g * HEAD_DIM:(g + 1) * HEAD_DIM] for g in range(Q_PER_KV)], axis=0)


def _rows_to_heads(x):
    rows = x.shape[0] // Q_PER_KV
    return jnp.concatenate([x[g * rows:(g + 1) * rows] for g in range(Q_PER_KV)], axis=1)


def _causal(R):
    kj = lax.broadcasted_iota(jnp.int32, (BLOCK, R), 0)
    qi = lax.broadcasted_iota(jnp.int32, (BLOCK, R), 1) & (BLOCK - 1)
    return kj <= qi


def _band_probs(s_band, s_m, sink):
    m = jnp.maximum(jnp.maximum(jnp.max(s_band, axis=0, keepdims=True), jnp.max(s_m, axis=0, keepdims=True)), sink)
    e_b, e_m, e_s = jnp.exp(s_band - m), jnp.exp(s_m - m), jnp.exp(sink - m)
    inv = 1.0 / (jnp.sum(e_b, axis=0, keepdims=True) + jnp.sum(e_m, axis=0, keepdims=True) + e_s)
    return e_b * inv, e_m * inv, e_s * inv


def _fold_band(tri, two):
    return jnp.where(tri, two[BLOCK:2 * BLOCK], two[0:BLOCK])


def _unfold_band(tri, band):
    return jnp.concatenate([jnp.where(tri, 0.0, band), jnp.where(tri, band, 0.0)], axis=0)


def _meta_probs(qm, k_m, sink_m):
    R = qm.shape[0]
    s = lax.dot_general(qm, k_m, _NT, preferred_element_type=F32)
    qi = lax.broadcasted_iota(jnp.int32, (R, N_META), 0) & (N_META - 1)
    kj = lax.broadcasted_iota(jnp.int32, (R, N_META), 1)
    s = jnp.where(kj <= qi, s, NEG_INF)
    m = jnp.maximum(jnp.max(s, axis=-1, keepdims=True), sink_m)
    e, e_s = jnp.exp(s - m), jnp.exp(sink_m - m)
    inv = 1.0 / (jnp.sum(e, axis=-1, keepdims=True) + e_s)
    return e * inv, e_s * inv


def _block_start(n):
    return pl.multiple_of(N_META + n * BLOCK, ROW_ALIGN)


def _kv(blk):
    return blk[:, 0:HEAD_DIM], blk[:, HEAD_DIM:2 * HEAD_DIM]


def _attn_fwd(q, kv, sink_row, sink_meta, B, name, riders=None):
    T, D = q.shape
    L = T // B
    KV = D // QW
    nb = (L - N_META) // BLOCK

    def body(q_ref, kv_ref, sk_ref, skm_ref, o_ref, kvs):
        kvs[...] = kv_ref[...].astype(BF16)
        k_m, v_m = _kv(kvs[0:N_META, :])
        p, _ = _meta_probs(_heads_to_rows(q_ref[0:N_META, :]), k_m, skm_ref[0])
        o_ref[0:N_META, :] = _rows_to_heads(jnp.dot(p.astype(BF16), v_m, preferred_element_type=F32))
        tri = _causal(Q_PER_KV * BLOCK)

        def block(cur, first, keys):
            k2, v2 = _kv(kvs[keys, :])
            qb = _heads_to_rows(q_ref[pl.ds(cur, BLOCK), :])
            st = lax.dot_general(k2, qb, _NT, preferred_element_type=F32)
            smt = lax.dot_general(k_m, qb, _NT, preferred_element_type=F32)
            s_band = jnp.where(tri, st, NEG_INF) if first else _fold_band(tri, st)
            p_b, p_m, _ = _band_probs(s_band, smt, sk_ref[0])
            p2 = (p_b if first else _unfold_band(tri, p_b)).astype(BF16)
            o = (lax.dot_general(p2, v2, _TN, preferred_element_type=F32)
                 + lax.dot_general(p_m.astype(BF16), v_m, _TN, preferred_element_type=F32))
            o_ref[pl.ds(cur, BLOCK), :] = _rows_to_heads(o)

        block(N_META, True, pl.ds(N_META, BLOCK))

        def step(n, carry):
            block(_block_start(n), False, pl.ds(_block_start(n - 1), 2 * BLOCK))
            return carry

        lax.fori_loop(1, nb, step, 0, unroll=5 if (nb - 1) % 5 == 0 else 1)

    q_spec = pl.BlockSpec((L, QW), lambda b, h: (b, h))
    return _call(body, name, (B, KV),
                 [q_spec, pl.BlockSpec((L, 2 * HEAD_DIM), lambda b, h: (b, h)),
                  pl.BlockSpec((1, 1, Q_PER_KV * BLOCK), lambda b, h: (h, 0, 0)),
                  pl.BlockSpec((1, Q_PER_KV * N_META, 1), lambda b, h: (h, 0, 0))],
                 [q_spec], [jax.ShapeDtypeStruct((T, D), F32)], [pltpu.VMEM((L, 2 * HEAD_DIM), BF16)],
                 ("parallel", "parallel"), (q, kv, sink_row, sink_meta), riders)


def _attn_bwd(q, kv, o, do, sink_row, sink_meta, B, name, riders=None):
    T, D = q.shape
    L = T // B
    KV = D // QW
    nb = (L - N_META) // BLOCK
    R = Q_PER_KV * BLOCK
    scale = HEAD_DIM ** -0.5

    def head_totals(col, rows_per_head):
        rid = lax.broadcasted_iota(jnp.int32, (8, 128), 0)
        out = jnp.zeros((8, 128), F32)
        for g in range(Q_PER_KV):
            out = out + jnp.where(rid == g, jnp.sum(col[g * rows_per_head:(g + 1) * rows_per_head, :]), 0.0)
        return out

    def body(q_ref, kv_ref, o_ref, do_ref, sk_ref, skm_ref, dq_ref, dkv_ref, dsk_ref, kvs, acc, acc_sink):
        b = pl.program_id(1)
        kvs[...] = kv_ref[...].astype(BF16)
        acc[...] = jnp.zeros_like(acc)
        k_m, v_m = _kv(kvs[0:N_META, :])

        qm = _heads_to_rows(q_ref[0:N_META, :])
        dom = _heads_to_rows(do_ref[0:N_META, :])
        delta = jnp.sum(dom * _heads_to_rows(o_ref[0:N_META, :]), axis=-1, keepdims=True)
        p, p_s = _meta_probs(qm, k_m, skm_ref[0])
        domb = dom.astype(BF16)
        ds = (p * (lax.dot_general(domb, v_m, _NT, preferred_element_type=F32) - delta)).astype(BF16)
        dq_ref[0:N_META, :] = _rows_to_heads(jnp.dot(ds, k_m, preferred_element_type=F32) * scale).astype(BF16)
        acc[0:N_META, :] += jnp.concatenate([lax.dot_general(ds, qm, _TN, preferred_element_type=F32),
                                             lax.dot_general(p.astype(BF16), domb, _TN, preferred_element_type=F32)], axis=1)
        sink_tot = head_totals(-p_s * delta, N_META)
        tri = _causal(R)
        acc_sink[...] = jnp.zeros_like(acc_sink)
        ones = jnp.ones((8, HEAD_DIM), BF16)

        def block(cur, first, keys):
            k2, v2 = _kv(kvs[keys, :])
            rows = pl.ds(cur, BLOCK)
            qb = _heads_to_rows(q_ref[rows, :])
            dob = _heads_to_rows(do_ref[rows, :])
            prod = dob * _heads_to_rows(o_ref[rows, :])
            hi = prod.astype(BF16)
            lo = (prod - hi.astype(F32)).astype(BF16)
            delta = (lax.dot_general(ones, hi, _NT, preferred_element_type=F32)
                     + lax.dot_general(ones, lo, _NT, preferred_element_type=F32))[0:1]
            dobb = dob.astype(BF16)
            st = lax.dot_general(k2, qb, _NT, preferred_element_type=F32)
            smt = lax.dot_general(k_m, qb, _NT, preferred_element_type=F32)
            s_band = jnp.where(tri, st, NEG_INF) if first else _fold_band(tri, st)
            p_b, p_m, p_s = _band_probs(s_band, smt, sk_ref[0])
            dpt = lax.dot_general(v2, dobb, _NT, preferred_element_type=F32)
            dpm = lax.dot_general(v_m, dobb, _NT, preferred_element_type=F32)
            ds_b = p_b * ((dpt if first else _fold_band(tri, dpt)) - delta)
            ds2 = (ds_b if first else _unfold_band(tri, ds_b)).astype(BF16)
            p2 = (p_b if first else _unfold_band(tri, p_b)).astype(BF16)
            dsm = (p_m * (dpm - delta)).astype(BF16)
            pm = p_m.astype(BF16)
            dq = (lax.dot_general(ds2, k2, _TN, preferred_element_type=F32)
                  + lax.dot_general(dsm, k_m, _TN, preferred_element_type=F32))
            dq_ref[rows, :] = _rows_to_heads(dq * scale).astype(BF16)
            acc[keys, :] += jnp.concatenate([jnp.dot(ds2, qb, preferred_element_type=F32),
                                             jnp.dot(p2, dobb, preferred_element_type=F32)], axis=1)
            acc[0:N_META, :] += jnp.concatenate([jnp.dot(dsm, qb, preferred_element_type=F32),
                                                 jnp.dot(pm, dobb, preferred_element_type=F32)], axis=1)
            acc_sink[0:1, :] += -p_s * delta

        block(N_META, True, pl.ds(N_META, BLOCK))

        def step(n, carry):
            block(_block_start(n), False, pl.ds(_block_start(n - 1), 2 * BLOCK))
            return carry

        lax.fori_loop(1, nb, step, 0, unroll=5 if (nb - 1) % 5 == 0 else 1)
        dkv_ref[...] = acc[...].astype(BF16)
        rid = lax.broadcasted_iota(jnp.int32, (8, 128), 0)
        tot = sink_tot
        for g in range(Q_PER_KV):
            tot = tot + jnp.where(rid == g, jnp.sum(acc_sink[:, g * BLOCK:(g + 1) * BLOCK]), 0.0)

        @pl.when(b == 0)
        def _():
            dsk_ref[0] = tot

        @pl.when(b > 0)
        def _():
            dsk_ref[0] += tot

    q_spec = pl.BlockSpec((L, QW), lambda h, b: (b, h))
    kv_spec = pl.BlockSpec((L, 2 * HEAD_DIM), lambda h, b: (b, h))
    return _call(body, name, (KV, B),
                 [q_spec, kv_spec, q_spec, q_spec,
                  pl.BlockSpec((1, 1, R), lambda h, b: (h, 0, 0)),
                  pl.BlockSpec((1, Q_PER_KV * N_META, 1), lambda h, b: (h, 0, 0))],
                 [q_spec, kv_spec, pl.BlockSpec((1, 8, 128), lambda h, b: (h, 0, 0))],
                 [jax.ShapeDtypeStruct((T, D), BF16), jax.ShapeDtypeStruct((T, KV * 2 * HEAD_DIM), BF16),
                  jax.ShapeDtypeStruct((KV, 8, 128), F32)],
                 [pltpu.VMEM((L, 2 * HEAD_DIM), BF16), pltpu.VMEM((L, 2 * HEAD_DIM), F32), pltpu.VMEM((8, R), F32)],
                 ("parallel", "arbitrary"), (q, kv, o, do, sink_row, sink_meta), riders)


def _cmul_add(acc_r, acc_i, lr, li, xr, xi):
    return acc_r + (lr * xr - li * xi), acc_i + (lr * xi + li * xr)


def _cols_per_step(ncol):
    for cps in (4, 2):
        if ncol % cps == 0:
            return cps
    return 1


def _ssm_fwd(u, bmat, cmat, dskip, tables, nbatch, rc, name):
    T, W = u.shape
    ncol = W // SSM_LANES
    nch = T // rc
    S = STATE_LANES
    cps = _cols_per_step(ncol)
    assert nbatch == 4

    def body(u_ref, b_ref, c_ref, d_ref, tab_ref, y_ref, xs_ref, st_ref, carry_ref):
        ch = pl.program_id(1)

        @pl.when(ch == 0)
        def _():
            carry_ref[...] = jnp.zeros_like(carry_ref)

        uv = u_ref[...]
        for k in range(cps):
            st_ref[:, 2 * S * k:2 * S * (k + 1)] = jnp.dot(uv[:, SSM_LANES * k:SSM_LANES * (k + 1)].astype(BF16), b_ref[k],
                                                           preferred_element_type=F32)
        low = lax.broadcasted_iota(jnp.int32, (8, S), 0) < nbatch

        def tile(k, r0, c_r, c_i):
            re, im = slice(2 * S * k, 2 * S * k + S), slice(2 * S * k + S, 2 * S * (k + 1))
            la_r, la_i = tab_ref[k, :, 0:S], tab_ref[k, :, S:2 * S]
            lb_r, lb_i = tab_ref[k, :, 2 * S:3 * S], tab_ref[k, :, 3 * S:4 * S]
            v_r = st_ref[pl.ds(r0, 8), re]
            v_i = st_ref[pl.ds(r0, 8), im]
            v_r, v_i = _cmul_add(v_r, v_i, la_r, la_i, pltpu.roll(v_r, nbatch, 0), pltpu.roll(v_i, nbatch, 0))
            rc_r, rc_i = pltpu.roll(c_r, nbatch, 0), pltpu.roll(c_i, nbatch, 0)
            cb_r, cb_i = jnp.where(low, rc_r, c_r), jnp.where(low, rc_i, c_i)
            v_r, v_i = _cmul_add(v_r, v_i, lb_r, lb_i, cb_r, cb_i)
            st_ref[pl.ds(r0, 8), re] = v_r
            st_ref[pl.ds(r0, 8), im] = v_i
            return v_r, v_i

        def step(i, carry):
            r0 = pl.multiple_of(i * 8, 8)
            out = []
            for k in range(cps):
                out += list(tile(k, r0, carry[2 * k], carry[2 * k + 1]))
            return tuple(out)

        halves = tuple(carry_ref[:, S * j:S * (j + 1)] for j in range(2 * cps))
        halves = lax.fori_loop(0, rc // 8, step, halves)
        for j in range(2 * cps):
            carry_ref[:, S * j:S * (j + 1)] = halves[j]
        xb = st_ref[...].astype(BF16)
        xs_ref[...] = xb
        for k in range(cps):
            cols = slice(SSM_LANES * k, SSM_LANES * (k + 1))
            y_ref[:, cols] = (jnp.dot(xb[:, 2 * S * k:2 * S * (k + 1)], c_ref[k], preferred_element_type=F32)
                              + d_ref[:, cols] * uv[:, cols])

    return pl.pallas_call(
        body, name=name, grid=(ncol // cps, nch),
        in_specs=[pl.BlockSpec((rc, cps * SSM_LANES), lambda g, c: (c, g)),
                  pl.BlockSpec((cps, SSM_LANES, 2 * S), lambda g, c: (g, 0, 0)),
                  pl.BlockSpec((cps, 2 * S, SSM_LANES), lambda g, c: (g, 0, 0)),
                  pl.BlockSpec((1, cps * SSM_LANES), lambda g, c: (0, g)),
                  pl.BlockSpec((cps, 8, 4 * S), lambda g, c: (g, 0, 0))],
        out_specs=[pl.BlockSpec((rc, cps * SSM_LANES), lambda g, c: (c, g)),
                   pl.BlockSpec((rc, cps * 2 * S), lambda g, c: (c, g))],
        out_shape=[jax.ShapeDtypeStruct((T, W), F32), jax.ShapeDtypeStruct((T, ncol * 2 * S), BF16)],
        scratch_shapes=[pltpu.VMEM((rc, cps * 2 * S), F32), pltpu.VMEM((8, cps * 2 * S), F32)],
        compiler_params=_params(("parallel", "arbitrary")),
    )(u, bmat, cmat, dskip, tables)


def _ssm_bwd(dy, u, xs, bmat, cmat, dskip, tables, nbatch, rc, name):
    T, W = u.shape
    ncol = W // SSM_LANES
    nch = T // rc
    S = STATE_LANES
    ntile = rc // 16
    cps = _cols_per_step(ncol)

    def body(dy_ref, u_ref, xs_ref, b_ref, c_ref, d_ref, tab_ref,
             du_ref, db_ref, dc_ref, dl_ref, dd_ref, st_ref, carry_ref, accl_ref, accd_ref):
        ch = pl.program_id(1)

        @pl.when(ch == 0)
        def _():
            carry_ref[...] = jnp.zeros_like(carry_ref)
            accl_ref[...] = jnp.zeros_like(accl_ref)
            accd_ref[...] = jnp.zeros_like(accd_ref)
            db_ref[...] = jnp.zeros_like(db_ref)
            dc_ref[...] = jnp.zeros_like(dc_ref)

        dyv = dy_ref[...]
        uv = u_ref[...]
        dyb = dyv.astype(BF16)
        for k in range(cps):
            st_ref[:, 2 * S * k:2 * S * (k + 1)] = lax.dot_general(dyb[:, SSM_LANES * k:SSM_LANES * (k + 1)], c_ref[k], _NT,
                                                                   preferred_element_type=F32)
        low = lax.broadcasted_iota(jnp.int32, (8, S), 0) < nbatch

        def tile(k, r0, x_r, x_i, c_r, c_i, al_r, al_i):
            re, im = slice(2 * S * k, 2 * S * k + S), slice(2 * S * k + S, 2 * S * (k + 1))
            la_r, la_i = tab_ref[k, :, 0:S], tab_ref[k, :, S:2 * S]
            lb_r, lb_i = tab_ref[k, :, 2 * S:3 * S], tab_ref[k, :, 3 * S:4 * S]
            v_r = st_ref[pl.ds(r0, 8), re]
            v_i = st_ref[pl.ds(r0, 8), im]
            v_r, v_i = _cmul_add(v_r, v_i, la_r, la_i, pltpu.roll(v_r, nbatch, 0), pltpu.roll(v_i, nbatch, 0))
            cb_r = jnp.where(low, c_r, pltpu.roll(c_r, nbatch, 0))
            cb_i = jnp.where(low, c_i, pltpu.roll(c_i, nbatch, 0))
            v_r, v_i = _cmul_add(v_r, v_i, lb_r, lb_i, cb_r, cb_i)
            st_ref[pl.ds(r0, 8), re] = v_r
            st_ref[pl.ds(r0, 8), im] = v_i
            n_r = jnp.where(low, pltpu.roll(v_r, nbatch, 0), cb_r)
            n_i = jnp.where(low, pltpu.roll(v_i, nbatch, 0), cb_i)
            al_r = al_r + (n_r * x_r + n_i * x_i)
            al_i = al_i + (n_i * x_r - n_r * x_i)
            return v_r, v_i, al_r, al_i

        def step(j, carry):
            r0 = pl.multiple_of((ntile - 1 - j) * 16, 16)
            out = []
            for k in range(cps):
                re, im = slice(2 * S * k, 2 * S * k + S), slice(2 * S * k + S, 2 * S * (k + 1))
                x_r = xs_ref[pl.ds(r0, 16), re].astype(F32)
                x_i = xs_ref[pl.ds(r0, 16), im].astype(F32)
                mid = tile(k, r0 + 8, x_r[8:16], x_i[8:16], *carry[4 * k:4 * k + 4])
                out += list(tile(k, r0, x_r[0:8], x_i[0:8], *mid))
            return tuple(out)

        init = []
        for k in range(cps):
            init += [carry_ref[:, 2 * S * k:2 * S * k + S], carry_ref[:, 2 * S * k + S:2 * S * (k + 1)],
                     accl_ref[:, 2 * S * k:2 * S * k + S], accl_ref[:, 2 * S * k + S:2 * S * (k + 1)]]
        fin = lax.fori_loop(0, ntile, step, tuple(init))
        for k in range(cps):
            carry_ref[:, 2 * S * k:2 * S * k + S] = fin[4 * k]
            carry_ref[:, 2 * S * k + S:2 * S * (k + 1)] = fin[4 * k + 1]
            accl_ref[:, 2 * S * k:2 * S * k + S] = fin[4 * k + 2]
            accl_ref[:, 2 * S * k + S:2 * S * (k + 1)] = fin[4 * k + 3]
        dsb = st_ref[...].astype(BF16)
        ub = uv.astype(BF16)
        for k in range(cps):
            cols, lanes = slice(SSM_LANES * k, SSM_LANES * (k + 1)), slice(2 * S * k, 2 * S * (k + 1))
            du_ref[:, cols] = (lax.dot_general(dsb[:, lanes], b_ref[k], _NT, preferred_element_type=F32)
                               + d_ref[:, cols] * dyv[:, cols])
            db_ref[k] += lax.dot_general(ub[:, cols], dsb[:, lanes], _TN, preferred_element_type=F32)
            dc_ref[k] += lax.dot_general(xs_ref[:, lanes], dyb[:, cols], _TN, preferred_element_type=F32)
        accd_ref[...] += _fold8(dyv * uv)

        @pl.when(ch == nch - 1)
        def _():
            for k in range(cps):
                dl_ref[k] = jnp.sum(accl_ref[:, 2 * S * k:2 * S * (k + 1)], axis=0, keepdims=True)
            dd_ref[...] = jnp.sum(accd_ref[...], axis=0, keepdims=True)

    rev = lambda g, c: (nch - 1 - c, g)
    return pl.pallas_call(
        body, name=name, grid=(ncol // cps, nch),
        in_specs=[pl.BlockSpec((rc, cps * SSM_LANES), rev), pl.BlockSpec((rc, cps * SSM_LANES), rev),
                  pl.BlockSpec((rc, cps * 2 * S), rev),
                  pl.BlockSpec((cps, SSM_LANES, 2 * S), lambda g, c: (g, 0, 0)),
                  pl.BlockSpec((cps, 2 * S, SSM_LANES), lambda g, c: (g, 0, 0)),
                  pl.BlockSpec((1, cps * SSM_LANES), lambda g, c: (0, g)),
                  pl.BlockSpec((cps, 8, 4 * S), lambda g, c: (g, 0, 0))],
        out_specs=[pl.BlockSpec((rc, cps * SSM_LANES), rev),
                   pl.BlockSpec((cps, SSM_LANES, 2 * S), lambda g, c: (g, 0, 0)),
                   pl.BlockSpec((cps, 2 * S, SSM_LANES), lambda g, c: (g, 0, 0)),
                   pl.BlockSpec((cps, 1, 2 * S), lambda g, c: (g, 0, 0)),
                   pl.BlockSpec((1, cps * SSM_LANES), lambda g, c: (0, g))],
        out_shape=[jax.ShapeDtypeStruct((T, W), F32),
                   jax.ShapeDtypeStruct((ncol, SSM_LANES, 2 * S), F32),
                   jax.ShapeDtypeStruct((ncol, 2 * S, SSM_LANES), F32),
                   jax.ShapeDtypeStruct((ncol, 1, 2 * S), F32),
                   jax.ShapeDtypeStruct((1, W), F32)],
        scratch_shapes=[pltpu.VMEM((rc, cps * 2 * S), F32), pltpu.VMEM((8, cps * 2 * S), F32),
                        pltpu.VMEM((8, cps * 2 * S), F32), pltpu.VMEM((8, cps * SSM_LANES), F32)],
        compiler_params=_params(("parallel", "arbitrary")),
    )(dy, u, xs, bmat, cmat, dskip, tables)


def _ssm_matrices(a_re, a_im, log_step, b_re, b_im, c_re, c_im):
    G, N = a_re.shape
    ncol = G // GROUPS_PER_COL
    step = jnp.exp(log_step)[:, None]
    mag = jnp.exp(a_re * step)
    ang = a_im * step
    lam_re, lam_im = mag * jnp.cos(ang), mag * jnp.sin(ang)
    den = a_re * a_re + a_im * a_im
    nr, ni = lam_re - 1.0, lam_im
    coef_re = (nr * a_re + ni * a_im) / den
    coef_im = (ni * a_re - nr * a_im) / den
    bb_re = coef_re[..., None] * b_re - coef_im[..., None] * b_im
    bb_im = coef_re[..., None] * b_im + coef_im[..., None] * b_re
    eye = jnp.eye(GROUPS_PER_COL, dtype=F32)
    bb = jnp.stack([bb_re, bb_im]).reshape(2, ncol, GROUPS_PER_COL, N, SSM_GROUP)
    bmat = jnp.einsum("pbgnc,gh->bgcphn", bb, eye).reshape(ncol, SSM_LANES, 2 * STATE_LANES)
    cc = jnp.stack([c_re, -c_im]).reshape(2, ncol, GROUPS_PER_COL, SSM_GROUP, N)
    cmat = jnp.einsum("pbgcn,gh->bpgnhc", cc, eye).reshape(ncol, 2 * STATE_LANES, SSM_LANES)
    lam = jnp.concatenate([lam_re.reshape(ncol, STATE_LANES), lam_im.reshape(ncol, STATE_LANES)], axis=-1)
    return lam, bmat, cmat


def _scan_tables(lam, nbatch, conj):
    S = STATE_LANES
    lr, li = lam[:, None, 0:S], lam[:, None, S:2 * S]
    if conj:
        li = -li
    l2r, l2i = lr * lr - li * li, 2.0 * lr * li
    first = (jnp.arange(8) < nbatch)[None, :, None]
    zero = jnp.zeros_like(lr)
    if conj:
        parts = [jnp.where(first, lr, zero), jnp.where(first, li, zero), jnp.where(first, l2r, lr), jnp.where(first, l2i, li)]
    else:
        parts = [jnp.where(first, zero, lr), jnp.where(first, zero, li), jnp.where(first, lr, l2r), jnp.where(first, li, l2i)]
    return jnp.concatenate([jnp.broadcast_to(p, (lam.shape[0], 8, S)) for p in parts], axis=-1)


def _adamw_update(w_ref, g_ref, m_ref, v_ref, d_ref, nm_ref, nv_ref):
    gv = g_ref[...]
    mn = ADAM_B1 * m_ref[...] + (1.0 - ADAM_B1) * gv
    vn = ADAM_B2 * v_ref[...] + (1.0 - ADAM_B2) * (gv * gv)
    m_hat = mn / (1.0 - ADAM_B1 ** ADAM_STEP)
    v_hat = vn / (1.0 - ADAM_B2 ** ADAM_STEP)
    d_ref[...] = -ADAM_LR * (m_hat / (jnp.sqrt(v_hat) + ADAM_EPS) + ADAM_WD * w_ref[...])
    nm_ref[...] = mn
    nv_ref[...] = vn


def _adamw_small(ws, gs, ms, vs, name):
    n = len(ws)

    def body(*refs):
        for i in range(n):
            _adamw_update(refs[i], refs[n + i], refs[2 * n + i], refs[3 * n + i],
                          refs[4 * n + i], refs[5 * n + i], refs[6 * n + i])

    vm = pl.BlockSpec(memory_space=pltpu.VMEM)
    shapes = [jax.ShapeDtypeStruct(a.shape, F32) for a in ws]
    outs = pl.pallas_call(body, name=name, in_specs=[vm] * (4 * n), out_specs=[vm] * (3 * n), out_shape=shapes * 3,
                          compiler_params=pltpu.CompilerParams(vmem_limit_bytes=VMEM_LIMIT))(*ws, *gs, *ms, *vs)
    return outs[:n], outs[n:2 * n], outs[2 * n:]


def _adamw(w, g, m, v, name):
    R, C = w.shape[-2], w.shape[-1]
    tr = R if R <= 512 else _pick_tile(R, 512, 8)
    body = functools.partial(_adamw_update)

    def spec_for(a):
        if len(a.shape) == 2:
            return pl.BlockSpec((tr, C), lambda i: (i, 0))
        return pl.BlockSpec((None, tr, C), lambda i: (0, i, 0))

    spec = spec_for(w)
    shp = jax.ShapeDtypeStruct(w.shape, F32)
    return pl.pallas_call(body, name=name, grid=(R // tr,), in_specs=[spec, spec_for(g), spec, spec], out_specs=[spec] * 3,
                          out_shape=[shp, shp, shp], compiler_params=_params(("parallel",)))(w, g, m, v)


_ANY = pl.BlockSpec(memory_space=pl.ANY)


def _place():
    x, y, c = lax.axis_index("x"), lax.axis_index("y"), lax.axis_index("c")
    chips = [(1 - x, y), (x, 1 - y), (1 - x, 1 - y)]
    return x, y, c, chips


def _remote(src, dst, send_sems, recv_sems, k, to):
    return pltpu.make_async_remote_copy(src_ref=src, dst_ref=dst, send_sem=send_sems.at[k], recv_sem=recv_sems.at[k],
                                        device_id=to, device_id_type=MESH_IDS)


class _Riders:
    def __init__(self, srcs, out_shapes, n_sems, copies):
        self.srcs, self.out_shapes, self.n_sems, self.copies = list(srcs), list(out_shapes), n_sems, copies


def _call(body, name, grid, in_specs, out_specs, out_shape, scratch_shapes, sem, args, riders=None):
    if riders is None:
        return pl.pallas_call(body, name=name, grid=grid, in_specs=in_specs, out_specs=out_specs, out_shape=out_shape,
                              scratch_shapes=scratch_shapes, compiler_params=_params(sem))(*args)
    n_in, n_out, n_scr = len(in_specs), len(out_specs), len(scratch_shapes)
    r_in, r_out = len(riders.srcs), len(riders.out_shapes)

    def carrying(*refs):
        a, b = n_in, n_in + r_in
        c, d = b + n_out, b + n_out + r_out
        e = d + n_scr
        sends, arrivals = riders.copies(refs[a:b], refs[c:d], refs[e], refs[e + 1])
        first, last = None, None
        for ax, size in enumerate(grid):
            at0, at1 = pl.program_id(ax) == 0, pl.program_id(ax) == size - 1
            first = at0 if first is None else first & at0
            last = at1 if last is None else last & at1

        @pl.when(first)
        def _():
            for cp in sends:
                cp.start()

        body(*refs[:a], *refs[b:c], *refs[d:e])

        @pl.when(last)
        def _():
            for cp in arrivals:
                cp.wait_recv()
            for cp in sends:
                cp.wait_send()

    outs = pl.pallas_call(
        carrying, name=name, grid=grid, in_specs=list(in_specs) + [_ANY] * r_in,
        out_specs=list(out_specs) + [_ANY] * r_out, out_shape=list(out_shape) + riders.out_shapes,
        scratch_shapes=list(scratch_shapes) + [pltpu.SemaphoreType.DMA((riders.n_sems,)),
                                               pltpu.SemaphoreType.DMA((riders.n_sems,))],
        compiler_params=pltpu.CompilerParams(dimension_semantics=("arbitrary",) * len(grid),
                                             vmem_limit_bytes=VMEM_LIMIT, has_side_effects=True),
    )(*args, *riders.srcs)
    return outs[:n_out], outs[n_out:]


def _gather_riders(shards):
    def copies(srcs, outs, send_sems, recv_sems):
        x, y, c, chips = _place()
        sends, arrivals = [], []
        for i, s in enumerate(shards):
            half = s.shape[0] // 2
            rows = pl.ds(c * half, half)
            for j, chip in enumerate(chips):
                sends.append(_remote(srcs[i].at[rows, :], outs[i].at[2 * x + y, rows, :], send_sems, recv_sems,
                                     3 * i + j, (*chip, c)))
                landed = outs[i].at[2 * chip[0] + chip[1], rows, :]
                arrivals.append(_remote(landed, landed, send_sems, recv_sems, 3 * i + j, (*chip, c)))
        return sends, arrivals

    return _Riders(shards, [jax.ShapeDtypeStruct((N_CHIPS,) + s.shape, s.dtype) for s in shards], 3 * len(shards), copies)


def _exchange_riders(parts):
    def copies(srcs, outs, send_sems, recv_sems):
        x, y, c, chips = _place()
        sends = [_remote(srcs[i].at[2 * chip[0] + chip[1]], outs[i].at[j], send_sems, recv_sems, 3 * i + j, (*chip, c))
                 for i in range(len(parts)) for j, chip in enumerate(chips)]
        return sends, sends

    return _Riders(parts, [jax.ShapeDtypeStruct((3,) + p.shape[1:], p.dtype) for p in parts], 3 * len(parts), copies)


def _swap_riders(grads):
    def copies(srcs, outs, send_sems, recv_sems):
        x, y, c, _ = _place()
        sends = []
        for i, g in enumerate(grads):
            half = g.shape[1] // 2
            sends.append(_remote(srcs[i].at[:, pl.ds((1 - c) * half, half), :], outs[i], send_sems, recv_sems, i,
                                 (x, y, 1 - c)))
        return sends, sends

    return _Riders(grads, [jax.ShapeDtypeStruct((N_CHIPS, g.shape[1] // 2, g.shape[2]), g.dtype) for g in grads],
                   len(grads), copies)


def _forward_halves(gathered, shards, tag):
    n = len(gathered)

    def body(*refs):
        srcs, outs = refs[:n], refs[n:2 * n]
        send_sems, recv_sems = refs[2 * n:]
        x, y, c, chips = _place()
        sibling = (x, y, 1 - c)
        cps = []
        for i in range(n):
            half = gathered[i].shape[1] // 2
            for j, chip in enumerate(chips):
                slot = 2 * chip[0] + chip[1]
                cps.append(_remote(srcs[i].at[slot, pl.ds(c * half, half), :], outs[i].at[slot, pl.ds(c * half, half), :],
                                   send_sems, recv_sems, 3 * i + j, sibling))
        for cp in cps:
            cp.start()
        for i in range(n):
            half = gathered[i].shape[1] // 2
            for j, chip in enumerate(chips):
                theirs = outs[i].at[2 * chip[0] + chip[1], pl.ds((1 - c) * half, half), :]
                _remote(theirs, theirs, send_sems, recv_sems, 3 * i + j, sibling).wait_recv()
        for cp in cps:
            cp.wait_send()

    outs = pl.pallas_call(
        body, name=f"gather_forward_{tag}", in_specs=[_ANY] * n, out_specs=[_ANY] * n,
        out_shape=[jax.ShapeDtypeStruct(g.shape, g.dtype) for g in gathered],
        input_output_aliases={i: i for i in range(n)},
        scratch_shapes=[pltpu.SemaphoreType.DMA((3 * n,)), pltpu.SemaphoreType.DMA((3 * n,))],
        compiler_params=pltpu.CompilerParams(has_side_effects=True),
    )(*gathered)
    slot = 2 * lax.axis_index("x") + lax.axis_index("y")
    return [lax.dynamic_update_slice(o, s[None], (slot, 0, 0)) for o, s in zip(outs, shards)]


def _gather_weights(shards):
    n = len(shards)

    def body(*refs):
        srcs, outs = refs[:n], refs[n:2 * n]
        send_sems, recv_sems = refs[2 * n:]
        x, y, c, chips = _place()
        sibling = (x, y, 1 - c)

        def piece(i, px, py, pc):
            half = shards[i].shape[0] // 2
            return outs[i].at[2 * px + py, pl.ds(pc * half, half), :]

        first = []
        for i in range(n):
            half = shards[i].shape[0] // 2
            for j, chip in enumerate(chips):
                first.append(_remote(srcs[i].at[pl.ds(c * half, half), :], piece(i, x, y, c), send_sems, recv_sems,
                                     6 * i + j, (*chip, c)))
        for cp in first:
            cp.start()
        passed = []
        for i in range(n):
            for j, chip in enumerate(chips):
                _remote(piece(i, *chip, c), piece(i, *chip, c), send_sems, recv_sems, 6 * i + j, (*chip, c)).wait_recv()
                cp = _remote(piece(i, *chip, c), piece(i, *chip, c), send_sems, recv_sems, 6 * i + 3 + j, sibling)
                cp.start()
                passed.append(cp)
        for i in range(n):
            for j, chip in enumerate(chips):
                _remote(piece(i, *chip, 1 - c), piece(i, *chip, 1 - c), send_sems, recv_sems, 6 * i + 3 + j,
                        sibling).wait_recv()
        for cp in first + passed:
            cp.wait_send()

    outs = pl.pallas_call(
        body, name="gather_weights", in_specs=[_ANY] * n, out_specs=[_ANY] * n,
        out_shape=[jax.ShapeDtypeStruct((N_CHIPS,) + s.shape, s.dtype) for s in shards],
        scratch_shapes=[pltpu.SemaphoreType.DMA((6 * n,)), pltpu.SemaphoreType.DMA((6 * n,))],
        compiler_params=pltpu.CompilerParams(has_side_effects=True),
    )(*shards)
    slot = 2 * lax.axis_index("x") + lax.axis_index("y")
    return [lax.dynamic_update_slice(o, s[None], (slot, 0, 0)) for o, s in zip(outs, shards)]


def _swap_halves(grads, tag):
    n = len(grads)

    def body(*refs):
        srcs, outs = refs[:n], refs[n:2 * n]
        send_sems, recv_sems = refs[2 * n:]
        x, y, c, _ = _place()
        cps = []
        for i in range(n):
            half = grads[i].shape[1] // 2
            cps.append(_remote(srcs[i].at[:, pl.ds((1 - c) * half, half), :], outs[i], send_sems, recv_sems, i, (x, y, 1 - c)))
        for cp in cps:
            cp.start()
        for cp in cps:
            cp.wait()

    return pl.pallas_call(
        body, name=f"grad_swap_halves_{tag}", in_specs=[_ANY] * n, out_specs=[_ANY] * n,
        out_shape=[jax.ShapeDtypeStruct((N_CHIPS, g.shape[1] // 2, g.shape[2]), g.dtype) for g in grads],
        scratch_shapes=[pltpu.SemaphoreType.DMA((n,)), pltpu.SemaphoreType.DMA((n,))],
        compiler_params=pltpu.CompilerParams(has_side_effects=True),
    )(*grads)


def _join_halves(fulls):
    n = len(fulls)

    def body(*refs):
        srcs, outs = refs[:n], refs[n:2 * n]
        send_sems, recv_sems = refs[2 * n:]
        x, y, c, _ = _place()
        sibling = (x, y, 1 - c)
        cps = []
        for i in range(n):
            h = fulls[i].shape[0] // 2
            cps.append(_remote(srcs[i].at[pl.ds(c * h, h), :], outs[i].at[pl.ds(c * h, h), :], send_sems, recv_sems, i,
                               sibling))
        for cp in cps:
            cp.start()
        for i in range(n):
            h = fulls[i].shape[0] // 2
            theirs = outs[i].at[pl.ds((1 - c) * h, h), :]
            _remote(theirs, theirs, send_sems, recv_sems, i, sibling).wait_recv()
        for cp in cps:
            cp.wait_send()

    return pl.pallas_call(
        body, name="grad_join_halves", in_specs=[_ANY] * n, out_specs=[_ANY] * n,
        out_shape=[jax.ShapeDtypeStruct(f.shape, f.dtype) for f in fulls],
        input_output_aliases={i: i for i in range(n)},
        scratch_shapes=[pltpu.SemaphoreType.DMA((n,)), pltpu.SemaphoreType.DMA((n,))],
        compiler_params=pltpu.CompilerParams(has_side_effects=True),
    )(*fulls)


def _half_tile(h):
    return h if h <= 512 else _pick_tile(h, 512, ROW_ALIGN)


def _sum_halves(g, r1, c_idx, name):
    _, R, C = g.shape
    H = R // 2
    tr = _half_tile(H)
    nblk = H // tr

    def body(c_ref, g_ref, r_ref, p_ref):
        p_ref[...] = (g_ref[...] + r_ref[...]).astype(BF16)

    half = pl.BlockSpec((None, tr, C), lambda s, i, c_ref: (s, c_ref[0] * nblk + i, 0))
    plain = pl.BlockSpec((None, tr, C), lambda s, i, c_ref: (s, i, 0))
    return pl.pallas_call(
        body, name=name,
        grid_spec=pltpu.PrefetchScalarGridSpec(num_scalar_prefetch=1, grid=(N_CHIPS, nblk), in_specs=[half, plain],
                                               out_specs=plain),
        out_shape=jax.ShapeDtypeStruct((N_CHIPS, H, C), BF16),
        compiler_params=_params(("parallel", "parallel")),
    )(c_idx, g, r1)


def _sum_chips(g, r1, r2, idx, name):
    _, R, C = g.shape
    H = R // 2
    tr = _half_tile(H)
    nblk = H // tr

    def body(idx_ref, g_ref, r1_ref, r2_ref, o_ref):
        o_ref[...] = (((g_ref[...] + r1_ref[...]) + r2_ref[0].astype(F32)) + r2_ref[1].astype(F32)) + r2_ref[2].astype(F32)

    return pl.pallas_call(
        body, name=name,
        grid_spec=pltpu.PrefetchScalarGridSpec(
            num_scalar_prefetch=1, grid=(nblk,),
            in_specs=[pl.BlockSpec((None, tr, C), lambda i, idx_ref: (idx_ref[0], idx_ref[1] * nblk + i, 0)),
                      pl.BlockSpec((None, tr, C), lambda i, idx_ref: (idx_ref[0], i, 0)),
                      pl.BlockSpec((3, tr, C), lambda i, idx_ref: (0, i, 0))],
            out_specs=pl.BlockSpec((tr, C), lambda i, idx_ref: (idx_ref[1] * nblk + i, 0))),
        out_shape=jax.ShapeDtypeStruct((R, C), F32),
        compiler_params=_params(("parallel",)),
    )(idx, g, r1, r2)


def _all_reduce_small(v, n_fold, fold_rows, fold_at):
    M, N = v.shape

    def body(x_ref, tot_ref, fold_ref, all_ref, send_sems, recv_sems, local_sem):
        x, y, c, chips = _place()
        me, sibling = (x, y, c), (x, y, 1 - c)

        def rows(px, py, pc):
            return all_ref.at[pl.ds((4 * px + 2 * py + pc) * M, M), :]

        def copy(k, block, to, src=None):
            return _remote(rows(*block) if src is None else src, rows(*block), send_sems, recv_sems, k, to)

        mine = pltpu.make_async_copy(x_ref, rows(*me), local_sem)
        mine.start()
        first = [copy(0, me, sibling, src=x_ref)]
        first += [copy(1 + j, me, (*chip, c), src=x_ref) for j, chip in enumerate(chips)]
        for cp in first:
            cp.start()
        passed = [copy(4 + j, (*chip, c), sibling) for j, chip in enumerate(chips)]
        for j, chip in enumerate(chips):
            copy(1 + j, (*chip, c), me).wait_recv()
            passed[j].start()
        copy(0, sibling, me).wait_recv()
        for j, chip in enumerate(chips):
            copy(4 + j, (*chip, 1 - c), me).wait_recv()
        for cp in first + passed:
            cp.wait_send()
        mine.wait()
        tot = all_ref[0:M, :]
        for d in range(1, 8):
            tot = tot + all_ref[d * M:(d + 1) * M, :]
        tot_ref[...] = tot
        f = tot[fold_at:fold_at + fold_rows, :]
        for e in range(1, n_fold):
            f = f + tot[fold_at + e * fold_rows:fold_at + (e + 1) * fold_rows, :]
        fold_ref[...] = f

    vm = pl.BlockSpec(memory_space=pltpu.VMEM)
    return pl.pallas_call(
        body, name="all_reduce_small", in_specs=[vm], out_specs=[vm, vm],
        out_shape=[jax.ShapeDtypeStruct((M, N), F32), jax.ShapeDtypeStruct((fold_rows, N), F32)],
        scratch_shapes=[pltpu.VMEM((8 * M, N), F32), pltpu.SemaphoreType.DMA((7,)), pltpu.SemaphoreType.DMA((7,)),
                        pltpu.SemaphoreType.DMA],
        compiler_params=pltpu.CompilerParams(has_side_effects=True, vmem_limit_bytes=VMEM_LIMIT),
    )(v)


def _as_rows(a, width):
    flat = a.reshape(-1)
    pad = (-flat.shape[0]) % width
    if pad:
        flat = jnp.concatenate([flat, jnp.zeros((pad,), flat.dtype)])
    return flat.reshape(-1, width)


class _Layout:
    def __init__(self, width, total_mult):
        self.width, self.total_mult = width, total_mult
        self.offsets, self.shapes, self.rows = {}, {}, 0

    def add(self, name, shape):
        r = -(-math.prod(shape) // self.width)
        self.offsets[name], self.shapes[name] = (self.rows, r), tuple(shape)
        self.rows += r

    def align(self, mult):
        gap = (-self.rows) % mult
        if gap:
            self.offsets[f"_gap{self.rows}"], self.shapes[f"_gap{self.rows}"] = (self.rows, gap), (gap, self.width)
            self.rows += gap
        return self.rows

    def pack(self, pieces):
        self.align(self.total_mult)
        parts = [_as_rows(pieces[n].astype(F32), self.width) if n in pieces else jnp.zeros(self.shapes[n], F32)
                 for n in self.offsets]
        return jnp.concatenate(parts, axis=0)

    def unpack(self, buf, name):
        off, r = self.offsets[name]
        shape = self.shapes[name]
        return buf[off:off + r].reshape(-1)[:math.prod(shape)].reshape(shape)


_BIG = ["ffn1_w1", "ffn1_w3", "ffn1_w2", "w_in", "ssm_glu_a", "ssm_glu_b", "w_out", "ffn2_w1", "ffn2_w3", "ffn2_w2"]
_TRANSPOSED = {"ffn1_w1", "ffn1_w3", "ffn2_w1", "ffn2_w3"}
_SMALL = ["ffn1_norm", "mix_norm", "ffn2_norm", "final_norm", "attn_sinks", "ssm_a_re", "ssm_a_im", "ssm_log_step",
          "ssm_b_re", "ssm_b_im", "ssm_c_re", "ssm_c_im", "ssm_d"]
_WEIGHTS = ["meta_tokens", "ffn1_norm", "ffn1_w1", "ffn1_w3", "ffn1_w2", "mix_norm", "w_in", "attn_sinks", "ssm_a_re",
            "ssm_a_im", "ssm_log_step", "ssm_b_re", "ssm_b_im", "ssm_c_re", "ssm_c_im", "ssm_d", "ssm_glu_a",
            "ssm_glu_b", "w_out", "ffn2_norm", "ffn2_w1", "ffn2_w3", "ffn2_w2", "final_norm"]


def _kv_interleave(w, kv_heads):
    kvw = kv_heads * HEAD_DIM
    lead = w.shape[:-1]
    k = w[..., 0:kvw].reshape(lead + (kv_heads, 1, HEAD_DIM))
    v = w[..., kvw:2 * kvw].reshape(lead + (kv_heads, 1, HEAD_DIM))
    return jnp.concatenate([jnp.concatenate([k, v], axis=-2).reshape(lead + (2 * kvw,)), w[..., 2 * kvw:]], axis=-1)


def _kv_deinterleave(w, kv_heads):
    kvw = kv_heads * HEAD_DIM
    lead = w.shape[:-1]
    kv = w[..., 0:2 * kvw].reshape(lead + (kv_heads, 2, HEAD_DIM))
    return jnp.concatenate([kv[..., 0, :].reshape(lead + (kvw,)), kv[..., 1, :].reshape(lead + (kvw,)), w[..., 2 * kvw:]],
                           axis=-1)


def _step(x, target, w, m, v):
    B, S, D = x.shape
    L = S + N_META
    T = B * L
    H = D // HEAD_DIM
    KV = H // Q_PER_KV
    SW = D // 2
    tm = _pick_tile(L, ROW_TILE_CAP, ROW_ALIGN)
    rc = _pick_tile(L, ROW_TILE_CAP // B, 4) * B
    tw = _pick_tile(T, 6 * ROW_TILE_CAP, ROW_ALIGN)
    tm2 = _pick_tile(T, 2 * ROW_TILE_CAP, ROW_ALIGN)
    my_c = lax.axis_index("c")
    my_slot = 2 * lax.axis_index("x") + lax.axis_index("y")

    groups = {"ffn1": ["ffn1_w1", "ffn1_w3", "ffn1_w2"], "mix": ["w_in", "ssm_glu_a", "ssm_glu_b", "w_out"],
              "ffn2": ["ffn2_w1", "ffn2_w3", "ffn2_w2"]}
    waves = {"first": ["ffn1_w1", "ffn1_w3"], "early": ["ffn1_w2"] + groups["mix"], "late": groups["ffn2"]}
    def own_layout(a, n):
        return jnp.swapaxes(a[0], 0, 1) if n in _TRANSPOSED else a[0]

    shards = {n: own_layout(w[n], n).astype(BF16) for n in _BIG}
    gathered = _gather_weights([shards[n] for n in waves["first"]] + [w["meta_tokens"]])
    ws = dict(zip(waves["first"], gathered[:-1]))
    meta = jnp.transpose(gathered[-1], (1, 0, 2)).reshape(N_META, D)

    def arrive(wave, landed):
        mine = [shards[n] for n in waves[wave]]
        ws.update(zip(waves[wave], _forward_halves(landed, mine, wave)))

    g_ffn1, g_mix, g_ffn2 = w["ffn1_norm"], w["mix_norm"], w["ffn2_norm"]
    g_final = w["final_norm"].reshape(1, D)

    h0, n_ffn1 = _embed_norm(x, meta, g_ffn1, tm, "ffn1_norm")

    def ffn_fwd(h, g, tag, carry=None, n=None):
        if n is None:
            n = _rmsnorm_fwd(h, g, tm, f"{tag}_norm")
        riders = None if carry is None else _gather_riders([shards[k] for k in waves[carry]])
        out = _ffn_up(n, ws[f"{tag}_w1"], ws[f"{tag}_w3"], tm2, f"{tag}_up", riders)
        if carry is not None:
            out, landed = out
            arrive(carry, landed)
        a, c, s = out
        return _ffn_down(s, ws[f"{tag}_w2"], h, tm, f"{tag}_down"), (n, a, c, s)

    h1, saved1 = ffn_fwd(h0, g_ffn1, "ffn1", carry="early", n=n_ffn1)
    w_kvu = _kv_interleave(ws["w_in"][1], KV)
    hn = _rmsnorm_fwd(h1, g_mix, tm, "mix_norm")
    q = _mm_colslots(hn, ws["w_in"], BF16, "w_in_q", tm2, first=0, count=1, scale=HEAD_DIM ** -0.5)
    kvu = _mm_plain(hn, w_kvu, "nn", F32, "w_in_kvu", tm2)
    gates = _mm_colslots(hn, ws["w_in"], F32, "w_in_gates", tm2, first=2, count=2)

    sinks = w["attn_sinks"].reshape(KV, Q_PER_KV, 1, 1)
    sink_row = jnp.broadcast_to(sinks.reshape(KV, 1, Q_PER_KV, 1), (KV, 1, Q_PER_KV, BLOCK)).reshape(KV, 1, Q_PER_KV * BLOCK)
    sink_meta = jnp.broadcast_to(sinks, (KV, Q_PER_KV, N_META, 1)).reshape(KV, Q_PER_KV * N_META, 1)
    (attn,), landed = _attn_fwd(q, kvu, sink_row, sink_meta, B, "attn_fwd",
                                _gather_riders([shards[k] for k in waves["late"]]))
    arrive("late", landed)

    def to_time_major(a2d):
        return jnp.transpose(a2d.reshape(B, L, a2d.shape[-1]), (1, 0, 2)).reshape(T, a2d.shape[-1])

    def to_batch_major(a2d):
        return jnp.transpose(a2d.reshape(L, B, a2d.shape[-1]), (1, 0, 2)).reshape(T, a2d.shape[-1])

    ssm_args = (w["ssm_a_re"][0], w["ssm_a_im"][0], w["ssm_log_step"][0], w["ssm_b_re"][0], w["ssm_b_im"][0],
                w["ssm_c_re"][0], w["ssm_c_im"][0])
    (lam, bmat, cmat), ssm_vjp = jax.vjp(_ssm_matrices, *ssm_args)
    bmat16, cmat16 = bmat.astype(BF16), cmat.astype(BF16)
    u_t = to_time_major(kvu[:, SW:])
    y_t, xs = _ssm_fwd(u_t, bmat16, cmat16, w["ssm_d"], _scan_tables(lam, B, False), B, rc, "ssm_fwd")
    y0 = to_batch_major(y_t)
    yg = _gelu_fwd(y0, tm, "gelu_fwd")
    ga = _mm_colslots(yg, ws["ssm_glu_a"], F32, "glu_a", tm2)
    gb = _mm_colslots(yg, ws["ssm_glu_b"], F32, "glu_b", tm2)
    merged = _merge_fwd(gates, attn, ga, gb, tm, "merge_fwd")
    h2 = _mm_rowslots(merged, ws["w_out"], h1, tm2, "w_out")
    h3, saved2 = ffn_fwd(h2, g_ffn2, "ffn2")
    dh3, dh3b, dg_final, loss_row = _loss_head(h3, g_final, target, tm, "loss_head")

    grads, swapped, received = {}, {}, {}
    c_idx = my_c.reshape(1).astype(jnp.int32)
    idx = jnp.stack([my_slot, my_c]).astype(jnp.int32)

    def swap_riders(group):
        return _swap_riders([grads[n] for n in groups[group]])

    def exchange_riders(group):
        names = groups[group]
        if names[0] not in swapped:
            swapped.update(zip(names, _swap_halves([grads[n] for n in names], group)))
        return _exchange_riders([_sum_halves(grads[n], swapped[n], c_idx, f"grad_sum_halves_{n}") for n in names])

    def ffn_bwd(h, g, saved, dh, dhb, tag, dhidden_carries=None, dn_carries=None, last=False):
        n, a, c, s = saved
        w1, w3, w2 = ws[f"{tag}_w1"], ws[f"{tag}_w3"], ws[f"{tag}_w2"]
        grads[f"{tag}_w2"] = _wgrad_hidden_rows(s, dhb, tw, f"{tag}_dw2", 0.5)
        if dhidden_carries is None:
            da, dc = _ffn_dhidden(dhb, w2, a, c, tm2, f"{tag}_dhidden")
        else:
            (da, dc), got = _ffn_dhidden(dhb, w2, a, c, tm2, f"{tag}_dhidden", exchange_riders(dhidden_carries[1]))
            received.update(zip(groups[dhidden_carries[1]], got))
        grads[f"{tag}_w1"] = _wgrad_hidden_rows(da, n, tw, f"{tag}_dw1", 1.0)
        grads[f"{tag}_w3"] = _wgrad_hidden_rows(dc, n, tw, f"{tag}_dw3", 1.0)
        kind, group = dn_carries
        riders = swap_riders(group) if kind == "swap" else exchange_riders(group)
        (dh_in, dhb_in, grads[f"{tag}_norm"]), got = _ffn_dn(da, w1, dc, w3, h, g, dh, tm, f"{tag}_dn", riders,
                                                               B if last else None)
        return dh_in, dhb_in, got

    dh2, dh2b, got = ffn_bwd(h2, g_ffn2, saved2, dh3, dh3b, "ffn2", dn_carries=("swap", "ffn2"))
    swapped.update(zip(groups["ffn2"], got))

    grads["w_out"] = _wgrad_rowslots(merged, dh2b, tw, "dw_out")
    dattn, dgat, dgss, dga, dgb = _merge_bwd(dh2b, ws["w_out"], gates, attn, ga, gb, tm, "merge_bwd")
    grads["ssm_glu_a"] = _wgrad_colslots(yg, dga, tw, "dglu_a")
    grads["ssm_glu_b"] = _wgrad_colslots(yg, dgb, tw, "dglu_b")
    dy0 = _gelu_bwd([(dga, ws["ssm_glu_a"]), (dgb, ws["ssm_glu_b"])], y0, tm2, "gelu_bwd")
    du_t, dbmat, dcmat, dlam, dd = _ssm_bwd(to_time_major(dy0), u_t, xs, bmat16, cmat16, w["ssm_d"],
                                            _scan_tables(lam, B, True), B, rc, "ssm_bwd")
    d_ssm = ssm_vjp((dlam[:, 0, :], dbmat, dcmat))
    for n, gval in zip(["ssm_a_re", "ssm_a_im", "ssm_log_step", "ssm_b_re", "ssm_b_im", "ssm_c_re", "ssm_c_im"], d_ssm):
        grads[n] = gval[None]
    grads["ssm_d"] = dd

    (dq, dkv, dsink), got = _attn_bwd(q, kvu, attn, dattn, sink_row, sink_meta, B, "attn_bwd",
                                      exchange_riders("ffn2"))
    received.update(zip(groups["ffn2"], got))
    grads["attn_sinks"] = dsink[:, 0:Q_PER_KV, 0].reshape(1, H)
    dkvu = jnp.concatenate([dkv, to_batch_major(du_t).astype(BF16)], axis=1)
    pieces = [dq, dkvu, dgat, dgss]
    dw_in = [_wgrad_plain(hn, p, f"dw_in_{k}", tw) for k, p in enumerate(pieces)]
    dw_in[1] = _kv_deinterleave(dw_in[1], KV)
    grads["w_in"] = jnp.stack(dw_in)
    w_in_wide = jnp.concatenate([ws["w_in"][0], w_kvu, ws["w_in"][2], ws["w_in"][3]], axis=1)
    (dh1, dh1b, grads["mix_norm"]), swap_mix = _mm_norm_bwd(
        "dhn", "nt", [], h1, g_mix, dh2, tm, swap_riders("mix"),
        cat=([(p, _spec((tm, D), lambda i: (i, 0))) for p in pieces], w_in_wide, _once((D, 4 * D), lambda i: (0, 0))))
    swapped.update(zip(groups["mix"], swap_mix))
    grad_x, dmeta_rows, got = ffn_bwd(h0, g_ffn1, saved1, dh1, dh1b, "ffn1", dhidden_carries=("exchange", "mix"),
                                      dn_carries=("exchange", "ffn1"), last=True)
    received.update(zip(groups["ffn1"], got))

    grads["final_norm"] = dg_final
    slay = _Layout(D, 8)
    for n in _SMALL:
        slay.add(n, w[n].shape)
    slay.add("loss", (1, D))
    meta_at = slay.align(8)
    slay.add("meta", (N_META, D))
    small = slay.pack({**{n: grads[n] for n in _SMALL}, "loss": loss_row, "meta": dmeta_rows})
    tot_small, dmeta = _all_reduce_small(small, 1, N_META, meta_at)
    loss = slay.unpack(tot_small, "loss")[0, 0]
    for n in _SMALL:
        grads[n] = slay.unpack(tot_small, n)
    cw = D // N_CHIPS
    grads["meta_tokens"] = lax.dynamic_slice_in_dim(dmeta, my_slot * cw, cw, axis=1)

    fulls = [_sum_chips(grads[n], swapped[n], received[n], idx, f"grad_sum_chips_{n}") for n in _BIG]
    for n, f in zip(_BIG, _join_halves(fulls)):
        grads[n] = f

    delta, new_m, new_v = {}, {}, {}
    for n in _BIG + ["meta_tokens"]:
        if n in _TRANSPOSED:
            flip = lambda a: jnp.swapaxes(a, -1, -2)
            outs = _adamw(flip(w[n]), grads[n], flip(m[n]), flip(v[n]), f"adamw_{n}")
            delta[n], new_m[n], new_v[n] = (flip(o) for o in outs)
            grads[n] = flip(grads[n])[None]
        else:
            delta[n], new_m[n], new_v[n] = _adamw(w[n], grads[n], m[n], v[n], f"adamw_{n}")
            grads[n] = grads[n].reshape(w[n].shape)

    def flat2d(a):
        return a.reshape(-1, a.shape[-1])

    d_, m_, v_ = _adamw_small([flat2d(w[n]) for n in _SMALL], [flat2d(grads[n]) for n in _SMALL],
                              [flat2d(m[n]) for n in _SMALL], [flat2d(v[n]) for n in _SMALL], "adamw_small")
    for i, n in enumerate(_SMALL):
        shp = w[n].shape
        delta[n], new_m[n], new_v[n] = d_[i].reshape(shp), m_[i].reshape(shp), v_[i].reshape(shp)
        grads[n] = grads[n].reshape(shp)

    return (loss, grad_x, *[grads[n] for n in _WEIGHTS], *[delta[n] for n in _WEIGHTS],
            *[new_m[n] for n in _WEIGHTS], *[new_v[n] for n in _WEIGHTS])


def kernel(x, meta_tokens, ffn1_norm, ffn1_w1, ffn1_w3, ffn1_w2, mix_norm, w_in, attn_sinks, ssm_a_re, ssm_a_im, ssm_log_step, ssm_b_re, ssm_b_im, ssm_c_re, ssm_c_im, ssm_d, ssm_glu_a, ssm_glu_b, w_out, ffn2_norm, ffn2_w1, ffn2_w3, ffn2_w2, final_norm, loss_target, m_meta_tokens, m_ffn1_norm, m_ffn1_w1, m_ffn1_w3, m_ffn1_w2, m_mix_norm, m_w_in, m_attn_sinks, m_ssm_a_re, m_ssm_a_im, m_ssm_log_step, m_ssm_b_re, m_ssm_b_im, m_ssm_c_re, m_ssm_c_im, m_ssm_d, m_ssm_glu_a, m_ssm_glu_b, m_w_out, m_ffn2_norm, m_ffn2_w1, m_ffn2_w3, m_ffn2_w2, m_final_norm, v_meta_tokens, v_ffn1_norm, v_ffn1_w1, v_ffn1_w3, v_ffn1_w2, v_mix_norm, v_w_in, v_attn_sinks, v_ssm_a_re, v_ssm_a_im, v_ssm_log_step, v_ssm_b_re, v_ssm_b_im, v_ssm_c_re, v_ssm_c_im, v_ssm_d, v_ssm_glu_a, v_ssm_glu_b, v_w_out, v_ffn2_norm, v_ffn2_w1, v_ffn2_w3, v_ffn2_w2, v_final_norm):
    args = locals()
    w = {n: args[n] for n in _WEIGHTS}
    m = {n: args["m_" + n] for n in _WEIGHTS}
    v = {n: args["v_" + n] for n in _WEIGHTS}
    return _step(x, loss_target, w, m, v)
```

```python
import functools
import math

import jax
import jax.numpy as jnp
from jax import lax
from jax.experimental import pallas as pl
from jax.experimental.pallas import tpu as pltpu

F32 = jnp.float32
BF16 = jnp.bfloat16
MESH_IDS = pl.DeviceIdType.MESH

N_CHIPS = 4
N_META = 16
HEAD_DIM = 64
Q_PER_KV = 4
QW = Q_PER_KV * HEAD_DIM
BLOCK = 128
SSM_GROUP = 16
SSM_STATE = 64
SSM_LANES = 128
GROUPS_PER_COL = SSM_LANES // SSM_GROUP
STATE_LANES = GROUPS_PER_COL * SSM_STATE
NORM_EPS = 1e-6
NEG_INF = -1e30
ADAM_LR, ADAM_B1, ADAM_B2, ADAM_EPS, ADAM_WD, ADAM_STEP = 0.001, 0.9, 0.999, 1e-08, 0.01, 10
GELU_C = math.sqrt(2.0 / math.pi)
ROW_ALIGN = 16
VMEM_LIMIT = 56 * 1024 * 1024
ROW_TILE_CAP = 688

_NN = (((1,), (0,)), ((), ()))
_NT = (((1,), (1,)), ((), ()))
_TN = (((0,), (0,)), ((), ()))
_DIMS = {"nn": _NN, "nt": _NT, "tn": _TN}


def _params(sem, **kw):
    return pltpu.CompilerParams(dimension_semantics=sem, vmem_limit_bytes=VMEM_LIMIT, **kw)


def _pick_tile(n, cap, mult):
    best = None
    for t in range(mult, min(n, cap) + 1, mult):
        if n % t == 0:
            best = t
    if best is None:
        raise ValueError(f"no tile for {n} (cap {cap}, multiple of {mult})")
    return best


def _sigmoid(x):
    return 0.5 * jnp.tanh(0.5 * x) + 0.5


def _spec(block, index_map):
    return pl.BlockSpec(block, index_map)


def _sum_dots(ins, mode):
    tot = None
    for p in range(len(ins) // 2):
        a_ref, b_ref = ins[2 * p], ins[2 * p + 1]
        for sl in ([None] if len(b_ref.shape) == 2 else range(b_ref.shape[0])):
            if sl is None:
                a, b = a_ref[...], b_ref[...]
            elif len(a_ref.shape) == 3:
                a, b = a_ref[sl], b_ref[sl]
            else:
                width = a_ref.shape[1] // b_ref.shape[0]
                a, b = a_ref[:, sl * width:(sl + 1) * width], b_ref[sl]
            d = lax.dot_general(a.astype(BF16), b.astype(BF16), _DIMS[mode], preferred_element_type=F32)
            tot = d if tot is None else tot + d
    return tot


def _mm(name, grid, kaxis, mode, pairs, out_shape, out_spec, scale=1.0, res=None):
    npairs = len(pairs)
    has_res = res is not None
    gk = 1 if kaxis is None else grid[kaxis]
    acc_shape = tuple(d for d in out_spec.block_shape if d is not None)

    def body(*refs):
        res_ref = refs[2 * npairs] if has_res else None
        o_ref = refs[2 * npairs + has_res]
        tot = _sum_dots(refs[:2 * npairs], mode)

        def finish(acc):
            r = acc * scale if scale != 1.0 else acc
            if has_res:
                r = res_ref[...] + r
            o_ref[...] = r.astype(o_ref.dtype)

        if gk == 1:
            finish(tot)
        else:
            acc_ref = refs[-1]
            k = pl.program_id(kaxis)

            @pl.when(k == 0)
            def _():
                acc_ref[...] = tot

            @pl.when(k > 0)
            def _():
                acc_ref[...] += tot

            @pl.when(k == gk - 1)
            def _():
                finish(acc_ref[...])

    in_specs, args = [], []
    for a, a_spec, b, b_spec in pairs:
        in_specs += [a_spec, b_spec]
        args += [a, b]
    if has_res:
        in_specs.append(res[1])
        args.append(res[0])
    sem = tuple("arbitrary" if ax == kaxis else "parallel" for ax in range(len(grid)))
    return pl.pallas_call(
        body, name=name, grid=grid, in_specs=in_specs, out_specs=out_spec, out_shape=out_shape,
        scratch_shapes=[pltpu.VMEM(acc_shape, F32)] if gk > 1 else [],
        compiler_params=_params(sem),
    )(*args)


def _mm_plain(a, b, mode, out_dtype, name, tm, scale=1.0):
    M, K = a.shape
    N = b.shape[1] if mode == "nn" else b.shape[0]
    return _mm(name, (M // tm,), None, mode,
               [(a, _spec((tm, K), lambda i: (i, 0)), b, _spec(b.shape, lambda i: (0, 0)))],
               jax.ShapeDtypeStruct((M, N), out_dtype), _spec((tm, N), lambda i: (i, 0)), scale=scale)


def _wgrad_plain(a, b, name, tr):
    R, M = a.shape
    N = b.shape[1]
    return _mm(name, (R // tr,), 0, "tn",
               [(a, _spec((tr, M), lambda r: (r, 0)), b, _spec((tr, N), lambda r: (r, 0)))],
               jax.ShapeDtypeStruct((M, N), F32), _spec((M, N), lambda r: (0, 0)))


def _for_real_rows(tile, tpe, tm, hbm, buf, fn):
    tb, tj = tile // tpe, tile % tpe

    @pl.when(tj == 0)
    def _():
        fn(hbm.at[tb, pl.ds(0, tm - N_META), :], buf.at[pl.ds(N_META, tm - N_META), :])

    @pl.when(tj > 0)
    def _():
        fn(hbm.at[tb, pl.ds(tj * tm - N_META, tm), :], buf)


def _embed_norm(x, meta, g, tm, name):
    B, S, D = x.shape
    L = S + N_META
    T = B * L
    nt = T // tm
    tpe = L // tm

    def body(x_hbm, meta_ref, g_ref, h_ref, n_ref, xbuf, sems):
        i = pl.program_id(0)
        slot = i % 2

        def fetch(tile, sl, act):
            _for_real_rows(tile, tpe, tm, x_hbm, xbuf.at[sl], lambda src, dst: act(pltpu.make_async_copy(src, dst, sems.at[sl])))

        @pl.when(i == 0)
        def _():
            fetch(i, slot, lambda cp: cp.start())

        @pl.when(i + 1 < nt)
        def _():
            fetch(i + 1, 1 - slot, lambda cp: cp.start())

        fetch(i, slot, lambda cp: cp.wait())

        @pl.when(i % tpe == 0)
        def _():
            xbuf[slot, 0:N_META, :] = meta_ref[...]

        hv = xbuf[slot]
        h_ref[...] = hv
        r = lax.rsqrt(jnp.mean(hv * hv, axis=-1, keepdims=True) + NORM_EPS)
        n_ref[...] = ((hv * r) * g_ref[...]).astype(BF16)

    row = pl.BlockSpec((tm, D), lambda i: (i, 0))
    return pl.pallas_call(
        body, name=name, grid=(nt,),
        in_specs=[pl.BlockSpec(memory_space=pl.ANY), pl.BlockSpec((N_META, D), lambda i: (0, 0)),
                  pl.BlockSpec((1, D), lambda i: (0, 0))],
        out_specs=[row, row],
        out_shape=[jax.ShapeDtypeStruct((T, D), F32), jax.ShapeDtypeStruct((T, D), BF16)],
        scratch_shapes=[pltpu.VMEM((2, tm, D), F32), pltpu.SemaphoreType.DMA((2,))],
        compiler_params=_params(("arbitrary",)),
    )(x, meta, g)


def _rmsnorm_fwd(h, g, tm, name):
    T, D = h.shape

    def body(h_ref, g_ref, o_ref):
        x = h_ref[...]
        r = lax.rsqrt(jnp.mean(x * x, axis=-1, keepdims=True) + NORM_EPS)
        o_ref[...] = ((x * r) * g_ref[...]).astype(BF16)

    return pl.pallas_call(
        body, name=name, grid=(T // tm,),
        in_specs=[pl.BlockSpec((tm, D), lambda i: (i, 0)), pl.BlockSpec((1, D), lambda i: (0, 0))],
        out_specs=pl.BlockSpec((tm, D), lambda i: (i, 0)),
        out_shape=jax.ShapeDtypeStruct((T, D), BF16),
        compiler_params=_params(("parallel",)),
    )(h, g)


def _fold8(x):
    return jnp.sum(x.reshape(x.shape[0] // 8, 8, x.shape[1]), axis=0)


def _mm_norm_bwd(name, mode, pairs, h, g, dres, tm, riders=None, examples=None):
    T, D = h.shape
    nt = T // tm
    npairs = len(pairs)
    tpe = None if examples is None else T // examples // tm

    def body(*refs):
        if examples is None:
            h_ref, g_ref, dres_ref, dh_ref, dhb_ref, dg_ref, acc_ref = refs[2 * npairs:]
        else:
            h_ref, g_ref, dres_ref, dx_hbm, dmeta_ref, dg_ref, acc_ref, dbuf, sems = refs[2 * npairs:]
        i = pl.program_id(0)
        x = h_ref[...]
        r = lax.rsqrt(jnp.mean(x * x, axis=-1, keepdims=True) + NORM_EPS)
        xhat = x * r
        dy = _sum_dots(refs[:2 * npairs], mode)
        dxhat = dy * g_ref[...]
        dx = r * (dxhat - xhat * jnp.mean(dxhat * xhat, axis=-1, keepdims=True))
        dh = dres_ref[...] + dx
        if examples is None:
            dh_ref[...] = dh
            dhb_ref[...] = dh.astype(BF16)
        else:
            slot = i % 2

            def push(tile, sl, act):
                _for_real_rows(tile, tpe, tm, dx_hbm, dbuf.at[sl],
                               lambda dst, src: act(pltpu.make_async_copy(src, dst, sems.at[sl])))

            dbuf[slot] = dh
            push(i, slot, lambda cp: cp.start())

            @pl.when(i > 0)
            def _():
                push(i - 1, 1 - slot, lambda cp: cp.wait())

            @pl.when(i == nt - 1)
            def _():
                push(i, slot, lambda cp: cp.wait())

            @pl.when(i == 0)
            def _():
                dmeta_ref[...] = dh[0:N_META]

            @pl.when((i > 0) & (i % tpe == 0))
            def _():
                dmeta_ref[...] += dh[0:N_META]
        part = _fold8(dy * xhat)

        @pl.when(i == 0)
        def _():
            acc_ref[...] = part

        @pl.when(i > 0)
        def _():
            acc_ref[...] += part

        @pl.when(i == nt - 1)
        def _():
            dg_ref[...] = jnp.sum(acc_ref[...], axis=0, keepdims=True)

    row = pl.BlockSpec((tm, D), lambda i: (i, 0))
    vec = pl.BlockSpec((1, D), lambda i: (0, 0))
    in_specs, args = [], []
    for a, a_spec, b, b_spec in pairs:
        in_specs += [a_spec, b_spec]
        args += [a, b]
    if examples is None:
        return _call(body, name, (nt,), in_specs + [row, vec, row], [row, row, vec],
                     [jax.ShapeDtypeStruct((T, D), F32), jax.ShapeDtypeStruct((T, D), BF16), jax.ShapeDtypeStruct((1, D), F32)],
                     [pltpu.VMEM((8, D), F32)], ("arbitrary",), (*args, h, g, dres), riders)
    S = T // examples - N_META
    return _call(body, name, (nt,), in_specs + [row, vec, row],
                 [_ANY, pl.BlockSpec((N_META, D), lambda i: (0, 0)), vec],
                 [jax.ShapeDtypeStruct((examples, S, D), F32), jax.ShapeDtypeStruct((N_META, D), F32),
                  jax.ShapeDtypeStruct((1, D), F32)],
                 [pltpu.VMEM((8, D), F32), pltpu.VMEM((2, tm, D), F32), pltpu.SemaphoreType.DMA((2,))],
                 ("arbitrary",), (*args, h, g, dres), riders)


def _ffn_up(n, w1t, w3t, tm, name, riders=None):
    T, D = n.shape
    Fs = w1t.shape[1]

    def body(n_ref, w1_ref, w3_ref, a_ref, c_ref, s_ref):
        x = n_ref[...]
        a = lax.dot_general(x, w1_ref[...], _NT, preferred_element_type=F32)
        c = lax.dot_general(x, w3_ref[...], _NT, preferred_element_type=F32)
        a_ref[...] = a.astype(BF16)
        c_ref[...] = c.astype(BF16)
        s_ref[...] = (a * _sigmoid(a) * c).astype(BF16)

    w_spec = _spec((None, Fs, D), lambda s, i: (s, 0, 0))
    o_spec = _spec((None, tm, Fs), lambda s, i: (s, i, 0))
    o_shape = jax.ShapeDtypeStruct((N_CHIPS, T, Fs), BF16)
    return _call(body, name, (N_CHIPS, T // tm), [_spec((tm, D), lambda s, i: (i, 0)), w_spec, w_spec],
                 [o_spec, o_spec, o_spec], [o_shape, o_shape, o_shape], [], ("parallel", "parallel"), (n, w1t, w3t), riders)


def _ffn_down(s, w2, h, tm, name):
    _, T, Fs = s.shape
    D = w2.shape[2]
    row = _spec((tm, D), lambda i: (i, 0))
    return _mm(name, (T // tm,), None, "nn",
               [(s, _spec((N_CHIPS, tm, Fs), lambda i: (0, i, 0)), w2, _spec((N_CHIPS, Fs, D), lambda i: (0, 0, 0)))],
               jax.ShapeDtypeStruct((T, D), F32), row, scale=0.5, res=(h, row))


def _ffn_dhidden(dhb, w2, a, c, tm, name, riders=None):
    T, D = dhb.shape
    Fs = w2.shape[1]

    def body(dh_ref, w2_ref, a_ref, c_ref, da_ref, dc_ref):
        d = 0.5 * lax.dot_general(dh_ref[...], w2_ref[pl.program_id(1)], _NT, preferred_element_type=F32)
        av = a_ref[...].astype(F32)
        cv = c_ref[...].astype(F32)
        sg = _sigmoid(av)
        da_ref[...] = (d * cv * (sg * (1.0 + av * (1.0 - sg)))).astype(BF16)
        dc_ref[...] = (d * (av * sg)).astype(BF16)

    h_spec = _spec((None, tm, Fs), lambda i, s: (s, i, 0))
    o_shape = jax.ShapeDtypeStruct((N_CHIPS, T, Fs), BF16)
    return _call(body, name, (T // tm, N_CHIPS),
                 [_spec((tm, D), lambda i, s: (i, 0)), _once((N_CHIPS, Fs, D), lambda i, s: (0, 0, 0)), h_spec, h_spec],
                 [h_spec, h_spec], [o_shape, o_shape], [], ("parallel", "parallel"), (dhb, w2, a, c), riders)


def _wgrad_hidden_rows(s, dhb, tr, name, scale):
    _, T, Fs = s.shape
    D = dhb.shape[1]
    return _mm(name, (N_CHIPS, T // tr), 1, "tn",
               [(s, _spec((None, tr, Fs), lambda k, r: (k, r, 0)), dhb, _spec((tr, D), lambda k, r: (r, 0)))],
               jax.ShapeDtypeStruct((N_CHIPS, Fs, D), F32), _spec((None, Fs, D), lambda k, r: (k, 0, 0)), scale=scale)


def _once(block, index_map):
    return pl.BlockSpec(block, index_map, pipeline_mode=pl.Buffered(1))


def _ffn_dn(da, w1t, dc, w3t, h, g, dres, tm, name, riders=None, examples=None):
    _, T, Fs = da.shape
    D = w1t.shape[2]
    h_spec = _spec((N_CHIPS, tm, Fs), lambda i: (0, i, 0))
    w_spec = _once((N_CHIPS, Fs, D), lambda i: (0, 0, 0))
    return _mm_norm_bwd(name, "nn", [(da, h_spec, w1t, w_spec), (dc, h_spec, w3t, w_spec)], h, g, dres, tm, riders,
                        examples)


def _mm_side_by_side(a, w, mode, out_dtype, name, tm, first=0, count=N_CHIPS, scale=1.0):
    T, K = a.shape
    assert first % count == 0
    n = w.shape[2] if mode == "nn" else w.shape[1]

    def body(a_ref, w_ref, o_ref):
        av = a_ref[...].astype(BF16)
        for j in range(count):
            r = lax.dot_general(av, w_ref[j].astype(BF16), _DIMS[mode], preferred_element_type=F32)
            o_ref[:, j * n:(j + 1) * n] = (r * scale if scale != 1.0 else r).astype(o_ref.dtype)

    return pl.pallas_call(
        body, name=name, grid=(T // tm,),
        in_specs=[_spec((tm, K), lambda i: (i, 0)), _once((count,) + w.shape[1:], lambda i: (first // count, 0, 0))],
        out_specs=_spec((tm, count * n), lambda i: (i, 0)),
        out_shape=jax.ShapeDtypeStruct((T, count * n), out_dtype),
        compiler_params=_params(("parallel",)),
    )(a, w)


def _mm_colslots(a, w, out_dtype, name, tm, first=0, count=N_CHIPS, scale=1.0):
    return _mm_side_by_side(a, w, "nn", out_dtype, name, tm, first, count, scale)


def _wgrad_colslots(a, d, tr, name):
    T, K = a.shape
    Ns = d.shape[1] // N_CHIPS
    return _mm(name, (N_CHIPS, T // tr), 1, "tn",
               [(a, _spec((tr, K), lambda k, r: (r, 0)), d, _spec((tr, Ns), lambda k, r: (r, k)))],
               jax.ShapeDtypeStruct((N_CHIPS, K, Ns), F32), _spec((None, K, Ns), lambda k, r: (k, 0, 0)))


def _mm_rowslots(a, w, h, tm, name):
    T = a.shape[0]
    N = w.shape[2]
    row = _spec((tm, N), lambda i: (i, 0))
    return _mm(name, (T // tm,), None, "nn",
               [(a, _spec((tm, a.shape[1]), lambda i: (i, 0)), w, _once(w.shape, lambda i: (0, 0, 0)))],
               jax.ShapeDtypeStruct((T, N), F32), row, res=(h, row))


def _wgrad_rowslots(a, d, tr, name):
    T = a.shape[0]
    Ks = a.shape[1] // N_CHIPS
    N = d.shape[1]
    return _mm(name, (N_CHIPS, T // tr), 1, "tn",
               [(a, _spec((tr, Ks), lambda k, r: (r, k)), d, _spec((tr, N), lambda k, r: (r, 0)))],
               jax.ShapeDtypeStruct((N_CHIPS, Ks, N), F32), _spec((None, Ks, N), lambda k, r: (k, 0, 0)))


def _gelu_parts(x):
    inner = GELU_C * (x + 0.044715 * (x * x * x))
    t = jnp.tanh(inner)
    return t, GELU_C * (1.0 + 3.0 * 0.044715 * (x * x))


def _gelu_fwd(y, tm, name):
    T, W = y.shape

    def body(y_ref, o_ref):
        x = y_ref[...]
        t, _ = _gelu_parts(x)
        o_ref[...] = (0.5 * x * (1.0 + t)).astype(BF16)

    spec = pl.BlockSpec((tm, W), lambda i: (i, 0))
    return pl.pallas_call(body, name=name, grid=(T // tm,), in_specs=[spec], out_specs=spec,
                          out_shape=jax.ShapeDtypeStruct((T, W), BF16),
                          compiler_params=_params(("parallel",)))(y)


def _gelu_bwd(pairs, y, tm, name):
    T, W = y.shape
    npairs = len(pairs)

    def body(*refs):
        y_ref, o_ref = refs[2 * npairs], refs[2 * npairs + 1]
        x = y_ref[...]
        t, dinner = _gelu_parts(x)
        o_ref[...] = _sum_dots(refs[:2 * npairs], "nt") * (0.5 * (1.0 + t) + 0.5 * x * (1.0 - t * t) * dinner)

    spec = pl.BlockSpec((tm, W), lambda i: (i, 0))
    in_specs, args = [], []
    for d, w in pairs:
        in_specs += [_spec((tm, d.shape[1]), lambda i: (i, 0)), _once(w.shape, lambda i: (0, 0, 0))]
        args += [d, w]
    return pl.pallas_call(body, name=name, grid=(T // tm,), in_specs=in_specs + [spec], out_specs=spec,
                          out_shape=jax.ShapeDtypeStruct((T, W), F32),
                          compiler_params=_params(("parallel",)))(*args, y)


def _merge_cols(D):
    cb = 512 if D % 512 == 0 else D
    return cb, D // cb


def _merge_fwd(gates, attn, ga, gb, tm, name):
    T, D = attn.shape
    cb, nc = _merge_cols(D)

    def body(gat_ref, gss_ref, attn_ref, ga_ref, gb_ref, o_ref):
        ssm = ga_ref[...] * _sigmoid(gb_ref[...])
        o_ref[...] = (_sigmoid(gat_ref[...]) * attn_ref[...] + _sigmoid(gss_ref[...]) * ssm).astype(BF16)

    def col(block):
        return pl.BlockSpec((tm, cb), lambda i, j: (i, block * nc + j))

    return pl.pallas_call(
        body, name=name, grid=(T // tm, nc),
        in_specs=[col(0), col(1), col(0), col(0), col(0)],
        out_specs=col(0), out_shape=jax.ShapeDtypeStruct((T, D), BF16),
        compiler_params=_params(("parallel", "parallel")),
    )(gates, gates, attn, ga, gb)


def _merge_bwd(dhb, w_out, gates, attn, ga, gb, tm, name):
    T, D = attn.shape
    cb, nc = _merge_cols(D)
    Ks = w_out.shape[1]
    spb = cb // Ks

    def body(dh_ref, w_ref, gat_ref, gss_ref, attn_ref, ga_ref, gb_ref, dattn_ref, dgat_ref, dgss_ref, dga_ref, dgb_ref):
        dh = dh_ref[...]
        d = jnp.concatenate([lax.dot_general(dh, w_ref[s], _NT, preferred_element_type=F32) for s in range(spb)], axis=1)
        sa = _sigmoid(gat_ref[...])
        ss = _sigmoid(gss_ref[...])
        sb = _sigmoid(gb_ref[...])
        gav = ga_ref[...]
        dattn_ref[...] = d * sa
        dgat_ref[...] = (d * attn_ref[...] * (sa * (1.0 - sa))).astype(BF16)
        dgss_ref[...] = (d * (gav * sb) * (ss * (1.0 - ss))).astype(BF16)
        dssm = d * ss
        dga_ref[...] = (dssm * sb).astype(BF16)
        dgb_ref[...] = (dssm * gav * (sb * (1.0 - sb))).astype(BF16)

    def col(block):
        return pl.BlockSpec((tm, cb), lambda i, j: (i, block * nc + j))

    b16 = jax.ShapeDtypeStruct((T, D), BF16)
    return pl.pallas_call(
        body, name=name, grid=(T // tm, nc),
        in_specs=[pl.BlockSpec((tm, D), lambda i, j: (i, 0)), pl.BlockSpec((spb, Ks, D), lambda i, j: (j, 0, 0)),
                  col(0), col(1), col(0), col(0), col(0)],
        out_specs=[col(0)] * 5,
        out_shape=[jax.ShapeDtypeStruct((T, D), F32), b16, b16, b16, b16],
        compiler_params=_params(("parallel", "parallel")),
    )(dhb, w_out, gates, gates, attn, ga, gb)


def _loss_head(h, g, target, tm, name, down=None):
    T, D = h.shape
    B, S, _ = target.shape
    L = S + N_META
    nt = T // tm
    tpe = L // tm

    def body(*refs):
        if down is None:
            h_ref, g_ref, t_hbm, dh_ref, dhb_ref, dg_ref, loss_ref, tbuf, acc_g, acc_l, sems = refs
        else:
            s_ref, w2_ref, h_ref, g_ref, t_hbm, dh_ref, dhb_ref, dg_ref, loss_ref, tbuf, acc_g, acc_l, sems = refs
        i = pl.program_id(0)
        j = i % tpe
        slot = i % 2

        def fetch(tile, sl, act):
            tb, tj = tile // tpe, tile % tpe

            @pl.when(tj == 0)
            def _():
                act(pltpu.make_async_copy(t_hbm.at[tb, pl.ds(0, tm - N_META), :],
                                          tbuf.at[sl, pl.ds(N_META, tm - N_META), :], sems.at[sl]))

            @pl.when(tj > 0)
            def _():
                act(pltpu.make_async_copy(t_hbm.at[tb, pl.ds(tj * tm - N_META, tm), :], tbuf.at[sl], sems.at[sl]))

        @pl.when(i == 0)
        def _():
            tbuf[:, 0:N_META, :] = jnp.zeros((2, N_META, D), F32)
            fetch(i, slot, lambda cp: cp.start())

        @pl.when(i + 1 < nt)
        def _():
            fetch(i + 1, 1 - slot, lambda cp: cp.start())

        x = h_ref[...]
        if down is not None:
            x = x + 0.5 * _sum_dots((s_ref, w2_ref), "nn")
        fetch(i, slot, lambda cp: cp.wait())
        gv = g_ref[...]
        r = lax.rsqrt(jnp.mean(x * x, axis=-1, keepdims=True) + NORM_EPS)
        xhat = x * r
        pos = j * tm + lax.broadcasted_iota(jnp.int32, (tm, 1), 0)
        err = jnp.where(pos >= N_META, xhat * gv - tbuf[slot], 0.0)
        dy = err * (1.0 / D)
        dxhat = dy * gv
        dh = r * (dxhat - xhat * jnp.mean(dxhat * xhat, axis=-1, keepdims=True))
        dh_ref[...] = dh
        dhb_ref[...] = dh.astype(BF16)
        pg = _fold8(dy * xhat)
        pe = _fold8(err * err)

        @pl.when(i == 0)
        def _():
            acc_g[...] = pg
            acc_l[...] = pe

        @pl.when(i > 0)
        def _():
            acc_g[...] += pg
            acc_l[...] += pe

        @pl.when(i == nt - 1)
        def _():
            dg_ref[...] = jnp.sum(acc_g[...], axis=0, keepdims=True)
            loss_ref[...] = jnp.full((1, D), (0.5 / D) * jnp.sum(acc_l[...]), F32)

    row = pl.BlockSpec((tm, D), lambda i: (i, 0))
    vec = pl.BlockSpec((1, D), lambda i: (0, 0))
    lead_specs, lead_args = [], ()
    if down is not None:
        s, w2 = down
        lead_specs = [_spec((N_CHIPS, tm, s.shape[2]), lambda i: (0, i, 0)), _once(w2.shape, lambda i: (0, 0, 0))]
        lead_args = (s, w2)
    return pl.pallas_call(
        body, name=name, grid=(nt,),
        in_specs=lead_specs + [row, vec, pl.BlockSpec(memory_space=pl.ANY)], out_specs=[row, row, vec, vec],
        out_shape=[jax.ShapeDtypeStruct((T, D), F32), jax.ShapeDtypeStruct((T, D), BF16),
                   jax.ShapeDtypeStruct((1, D), F32), jax.ShapeDtypeStruct((1, D), F32)],
        scratch_shapes=[pltpu.VMEM((2, tm, D), F32), pltpu.VMEM((8, D), F32), pltpu.VMEM((8, D), F32),
                        pltpu.SemaphoreType.DMA((2,))],
        compiler_params=_params(("arbitrary",)),
    )(*lead_args, h, g, target)


def _heads_to_rows(blk):
    return jnp.concatenate([blk[:, g * HEAD_DIM:(g + 1) * HEAD_DIM] for g in range(Q_PER_KV)], axis=0)


def _rows_to_heads(x):
    rows = x.shape[0] // Q_PER_KV
    return jnp.concatenate([x[g * rows:(g + 1) * rows] for g in range(Q_PER_KV)], axis=1)


def _causal(R):
    kj = lax.broadcasted_iota(jnp.int32, (BLOCK, R), 0)
    qi = lax.broadcasted_iota(jnp.int32, (BLOCK, R), 1) & (BLOCK - 1)
    return kj <= qi


def _band_probs(s_band, s_m, sink):
    m = jnp.maximum(jnp.maximum(jnp.max(s_band, axis=0, keepdims=True), jnp.max(s_m, axis=0, keepdims=True)), sink)
    e_b, e_m, e_s = jnp.exp(s_band - m), jnp.exp(s_m - m), jnp.exp(sink - m)
    inv = 1.0 / (jnp.sum(e_b, axis=0, keepdims=True) + jnp.sum(e_m, axis=0, keepdims=True) + e_s)
    return e_b * inv, e_m * inv, e_s * inv


def _fold_band(tri, two):
    return jnp.where(tri, two[BLOCK:2 * BLOCK], two[0:BLOCK])


def _unfold_band(tri, band):
    return jnp.concatenate([jnp.where(tri, 0.0, band), jnp.where(tri, band, 0.0)], axis=0)


def _meta_probs(qm, k_m, sink_m):
    R = qm.shape[0]
    s = lax.dot_general(qm, k_m, _NT, preferred_element_type=F32)
    qi = lax.broadcasted_iota(jnp.int32, (R, N_META), 0) & (N_META - 1)
    kj = lax.broadcasted_iota(jnp.int32, (R, N_META), 1)
    s = jnp.where(kj <= qi, s, NEG_INF)
    m = jnp.maximum(jnp.max(s, axis=-1, keepdims=True), sink_m)
    e, e_s = jnp.exp(s - m), jnp.exp(sink_m - m)
    inv = 1.0 / (jnp.sum(e, axis=-1, keepdims=True) + e_s)
    return e * inv, e_s * inv


def _block_start(n):
    return pl.multiple_of(N_META + n * BLOCK, ROW_ALIGN)


def _kv(blk):
    return blk[:, 0:HEAD_DIM], blk[:, HEAD_DIM:2 * HEAD_DIM]


def _attn_fwd(q, kv, sink_row, sink_meta, B, name, riders=None):
    T, D = q.shape
    L = T // B
    KV = D // QW
    nb = (L - N_META) // BLOCK

    def body(q_ref, kv_ref, sk_ref, skm_ref, o_ref, kvs):
        kvs[...] = kv_ref[...].astype(BF16)
        k_m, v_m = _kv(kvs[0:N_META, :])
        p, _ = _meta_probs(_heads_to_rows(q_ref[0:N_META, :]), k_m, skm_ref[0])
        o_ref[0:N_META, :] = _rows_to_heads(jnp.dot(p.astype(BF16), v_m, preferred_element_type=F32))
        tri = _causal(Q_PER_KV * BLOCK)

        def block(cur, first, keys):
            k2, v2 = _kv(kvs[keys, :])
            qb = _heads_to_rows(q_ref[pl.ds(cur, BLOCK), :])
            st = lax.dot_general(k2, qb, _NT, preferred_element_type=F32)
            smt = lax.dot_general(k_m, qb, _NT, preferred_element_type=F32)
            s_band = jnp.where(tri, st, NEG_INF) if first else _fold_band(tri, st)
            p_b, p_m, _ = _band_probs(s_band, smt, sk_ref[0])
            p2 = (p_b if first else _unfold_band(tri, p_b)).astype(BF16)
            o = (lax.dot_general(p2, v2, _TN, preferred_element_type=F32)
                 + lax.dot_general(p_m.astype(BF16), v_m, _TN, preferred_element_type=F32))
            o_ref[pl.ds(cur, BLOCK), :] = _rows_to_heads(o)

        block(N_META, True, pl.ds(N_META, BLOCK))

        def step(n, carry):
            block(_block_start(n), False, pl.ds(_block_start(n - 1), 2 * BLOCK))
            return carry

        lax.fori_loop(1, nb, step, 0, unroll=5 if (nb - 1) % 5 == 0 else 1)

    q_spec = pl.BlockSpec((L, QW), lambda b, h: (b, h))
    return _call(body, name, (B, KV),
                 [q_spec, pl.BlockSpec((L, 2 * HEAD_DIM), lambda b, h: (b, h)),
                  pl.BlockSpec((1, 1, Q_PER_KV * BLOCK), lambda b, h: (h, 0, 0)),
                  pl.BlockSpec((1, Q_PER_KV * N_META, 1), lambda b, h: (h, 0, 0))],
                 [q_spec], [jax.ShapeDtypeStruct((T, D), F32)], [pltpu.VMEM((L, 2 * HEAD_DIM), BF16)],
                 ("parallel", "parallel"), (q, kv, sink_row, sink_meta), riders)


def _attn_bwd(q, kv, o, do, sink_row, sink_meta, B, name, riders=None):
    T, D = q.shape
    L = T // B
    KV = D // QW
    nb = (L - N_META) // BLOCK
    R = Q_PER_KV * BLOCK
    scale = HEAD_DIM ** -0.5

    def head_totals(col, rows_per_head):
        rid = lax.broadcasted_iota(jnp.int32, (8, 128), 0)
        out = jnp.zeros((8, 128), F32)
        for g in range(Q_PER_KV):
            out = out + jnp.where(rid == g, jnp.sum(col[g * rows_per_head:(g + 1) * rows_per_head, :]), 0.0)
        return out

    def body(q_ref, kv_ref, o_ref, do_ref, sk_ref, skm_ref, dq_ref, dkv_ref, dsk_ref, kvs, acc, acc_sink):
        b = pl.program_id(1)
        kvs[...] = kv_ref[...].astype(BF16)
        acc[...] = jnp.zeros_like(acc)
        k_m, v_m = _kv(kvs[0:N_META, :])

        qm = _heads_to_rows(q_ref[0:N_META, :])
        dom = _heads_to_rows(do_ref[0:N_META, :])
        delta = jnp.sum(dom * _heads_to_rows(o_ref[0:N_META, :]), axis=-1, keepdims=True)
        p, p_s = _meta_probs(qm, k_m, skm_ref[0])
        domb = dom.astype(BF16)
        ds = (p * (lax.dot_general(domb, v_m, _NT, preferred_element_type=F32) - delta)).astype(BF16)
        dq_ref[0:N_META, :] = _rows_to_heads(jnp.dot(ds, k_m, preferred_element_type=F32) * scale).astype(BF16)
        acc[0:N_META, :] += jnp.concatenate([lax.dot_general(ds, qm, _TN, preferred_element_type=F32),
                                             lax.dot_general(p.astype(BF16), domb, _TN, preferred_element_type=F32)], axis=1)
        sink_tot = head_totals(-p_s * delta, N_META)
        tri = _causal(R)
        acc_sink[...] = jnp.zeros_like(acc_sink)
        ones = jnp.ones((8, HEAD_DIM), BF16)

        def block(cur, first, keys):
            k2, v2 = _kv(kvs[keys, :])
            rows = pl.ds(cur, BLOCK)
            qb = _heads_to_rows(q_ref[rows, :])
            dob = _heads_to_rows(do_ref[rows, :])
            prod = dob * _heads_to_rows(o_ref[rows, :])
            hi = prod.astype(BF16)
            lo = (prod - hi.astype(F32)).astype(BF16)
            delta = (lax.dot_general(ones, hi, _NT, preferred_element_type=F32)
                     + lax.dot_general(ones, lo, _NT, preferred_element_type=F32))[0:1]
            dobb = dob.astype(BF16)
            st = lax.dot_general(k2, qb, _NT, preferred_element_type=F32)
            smt = lax.dot_general(k_m, qb, _NT, preferred_element_type=F32)
            s_band = jnp.where(tri, st, NEG_INF) if first else _fold_band(tri, st)
            p_b, p_m, p_s = _band_probs(s_band, smt, sk_ref[0])
            dpt = lax.dot_general(v2, dobb, _NT, preferred_element_type=F32)
            dpm = lax.dot_general(v_m, dobb, _NT, preferred_element_type=F32)
            ds_b = p_b * ((dpt if first else _fold_band(tri, dpt)) - delta)
            ds2 = (ds_b if first else _unfold_band(tri, ds_b)).astype(BF16)
            p2 = (p_b if first else _unfold_band(tri, p_b)).astype(BF16)
            dsm = (p_m * (dpm - delta)).astype(BF16)
            pm = p_m.astype(BF16)
            dq = (lax.dot_general(ds2, k2, _TN, preferred_element_type=F32)
                  + lax.dot_general(dsm, k_m, _TN, preferred_element_type=F32))
            dq_ref[rows, :] = _rows_to_heads(dq * scale).astype(BF16)
            acc[keys, :] += jnp.concatenate([jnp.dot(ds2, qb, preferred_element_type=F32),
                                             jnp.dot(p2, dobb, preferred_element_type=F32)], axis=1)
            acc[0:N_META, :] += jnp.concatenate([jnp.dot(dsm, qb, preferred_element_type=F32),
                                                 jnp.dot(pm, dobb, preferred_element_type=F32)], axis=1)
            acc_sink[0:1, :] += -p_s * delta

        block(N_META, True, pl.ds(N_META, BLOCK))

        def step(n, carry):
            block(_block_start(n), False, pl.ds(_block_start(n - 1), 2 * BLOCK))
            return carry

        lax.fori_loop(1, nb, step, 0, unroll=5 if (nb - 1) % 5 == 0 else 1)
        dkv_ref[...] = acc[...].astype(BF16)
        rid = lax.broadcasted_iota(jnp.int32, (8, 128), 0)
        tot = sink_tot
        for g in range(Q_PER_KV):
            tot = tot + jnp.where(rid == g, jnp.sum(acc_sink[:, g * BLOCK:(g + 1) * BLOCK]), 0.0)

        @pl.when(b == 0)
        def _():
            dsk_ref[0] = tot

        @pl.when(b > 0)
        def _():
            dsk_ref[0] += tot

    q_spec = pl.BlockSpec((L, QW), lambda h, b: (b, h))
    kv_spec = pl.BlockSpec((L, 2 * HEAD_DIM), lambda h, b: (b, h))
    return _call(body, name, (KV, B),
                 [q_spec, kv_spec, q_spec, q_spec,
                  pl.BlockSpec((1, 1, R), lambda h, b: (h, 0, 0)),
                  pl.BlockSpec((1, Q_PER_KV * N_META, 1), lambda h, b: (h, 0, 0))],
                 [q_spec, kv_spec, pl.BlockSpec((1, 8, 128), lambda h, b: (h, 0, 0))],
                 [jax.ShapeDtypeStruct((T, D), BF16), jax.ShapeDtypeStruct((T, KV * 2 * HEAD_DIM), BF16),
                  jax.ShapeDtypeStruct((KV, 8, 128), F32)],
                 [pltpu.VMEM((L, 2 * HEAD_DIM), BF16), pltpu.VMEM((L, 2 * HEAD_DIM), F32), pltpu.VMEM((8, R), F32)],
                 ("parallel", "arbitrary"), (q, kv, o, do, sink_row, sink_meta), riders)


def _cmul_add(acc_r, acc_i, lr, li, xr, xi):
    return acc_r + (lr * xr - li * xi), acc_i + (lr * xi + li * xr)


def _cols_per_step(ncol):
    for cps in (4, 2):
        if ncol % cps == 0:
            return cps
    return 1


def _ssm_fwd(u, bmat, cmat, dskip, tables, nbatch, rc, name):
    T, W = u.shape
    ncol = W // SSM_LANES
    nch = T // rc
    S = STATE_LANES
    cps = _cols_per_step(ncol)
    assert nbatch == 4

    def body(u_ref, b_ref, c_ref, d_ref, tab_ref, y_ref, xs_ref, st_ref, carry_ref):
        ch = pl.program_id(1)

        @pl.when(ch == 0)
        def _():
            carry_ref[...] = jnp.zeros_like(carry_ref)

        uv = u_ref[...]
        for k in range(cps):
            st_ref[:, 2 * S * k:2 * S * (k + 1)] = jnp.dot(uv[:, SSM_LANES * k:SSM_LANES * (k + 1)].astype(BF16), b_ref[k],
                                                           preferred_element_type=F32)
        low = lax.broadcasted_iota(jnp.int32, (8, S), 0) < nbatch

        def tile(k, r0, c_r, c_i):
            re, im = slice(2 * S * k, 2 * S * k + S), slice(2 * S * k + S, 2 * S * (k + 1))
            la_r, la_i = tab_ref[k, :, 0:S], tab_ref[k, :, S:2 * S]
            lb_r, lb_i = tab_ref[k, :, 2 * S:3 * S], tab_ref[k, :, 3 * S:4 * S]
            v_r = st_ref[pl.ds(r0, 8), re]
            v_i = st_ref[pl.ds(r0, 8), im]
            v_r, v_i = _cmul_add(v_r, v_i, la_r, la_i, pltpu.roll(v_r, nbatch, 0), pltpu.roll(v_i, nbatch, 0))
            rc_r, rc_i = pltpu.roll(c_r, nbatch, 0), pltpu.roll(c_i, nbatch, 0)
            cb_r, cb_i = jnp.where(low, rc_r, c_r), jnp.where(low, rc_i, c_i)
            v_r, v_i = _cmul_add(v_r, v_i, lb_r, lb_i, cb_r, cb_i)
            st_ref[pl.ds(r0, 8), re] = v_r
            st_ref[pl.ds(r0, 8), im] = v_i
            return v_r, v_i

        def step(i, carry):
            r0 = pl.multiple_of(i * 8, 8)
            out = []
            for k in range(cps):
                out += list(tile(k, r0, carry[2 * k], carry[2 * k + 1]))
            return tuple(out)

        halves = tuple(carry_ref[:, S * j:S * (j + 1)] for j in range(2 * cps))
        halves = lax.fori_loop(0, rc // 8, step, halves)
        for j in range(2 * cps):
            carry_ref[:, S * j:S * (j + 1)] = halves[j]
        xb = st_ref[...].astype(BF16)
        xs_ref[...] = xb
        for k in range(cps):
            cols = slice(SSM_LANES * k, SSM_LANES * (k + 1))
            y_ref[:, cols] = (jnp.dot(xb[:, 2 * S * k:2 * S * (k + 1)], c_ref[k], preferred_element_type=F32)
                              + d_ref[:, cols] * uv[:, cols])

    return pl.pallas_call(
        body, name=name, grid=(ncol // cps, nch),
        in_specs=[pl.BlockSpec((rc, cps * SSM_LANES), lambda g, c: (c, g)),
                  pl.BlockSpec((cps, SSM_LANES, 2 * S), lambda g, c: (g, 0, 0)),
                  pl.BlockSpec((cps, 2 * S, SSM_LANES), lambda g, c: (g, 0, 0)),
                  pl.BlockSpec((1, cps * SSM_LANES), lambda g, c: (0, g)),
                  pl.BlockSpec((cps, 8, 4 * S), lambda g, c: (g, 0, 0))],
        out_specs=[pl.BlockSpec((rc, cps * SSM_LANES), lambda g, c: (c, g)),
                   pl.BlockSpec((rc, cps * 2 * S), lambda g, c: (c, g))],
        out_shape=[jax.ShapeDtypeStruct((T, W), F32), jax.ShapeDtypeStruct((T, ncol * 2 * S), BF16)],
        scratch_shapes=[pltpu.VMEM((rc, cps * 2 * S), F32), pltpu.VMEM((8, cps * 2 * S), F32)],
        compiler_params=_params(("parallel", "arbitrary")),
    )(u, bmat, cmat, dskip, tables)


def _ssm_bwd(dy, u, xs, bmat, cmat, dskip, tables, nbatch, rc, name):
    T, W = u.shape
    ncol = W // SSM_LANES
    nch = T // rc
    S = STATE_LANES
    ntile = rc // 16
    cps = _cols_per_step(ncol)

    def body(dy_ref, u_ref, xs_ref, b_ref, c_ref, d_ref, tab_ref,
             du_ref, db_ref, dc_ref, dl_ref, dd_ref, st_ref, carry_ref, accl_ref, accd_ref):
        ch = pl.program_id(1)

        @pl.when(ch == 0)
        def _():
            carry_ref[...] = jnp.zeros_like(carry_ref)
            accl_ref[...] = jnp.zeros_like(accl_ref)
            accd_ref[...] = jnp.zeros_like(accd_ref)
            db_ref[...] = jnp.zeros_like(db_ref)
            dc_ref[...] = jnp.zeros_like(dc_ref)

        dyv = dy_ref[...]
        uv = u_ref[...]
        dyb = dyv.astype(BF16)
        for k in range(cps):
            st_ref[:, 2 * S * k:2 * S * (k + 1)] = lax.dot_general(dyb[:, SSM_LANES * k:SSM_LANES * (k + 1)], c_ref[k], _NT,
                                                                   preferred_element_type=F32)
        low = lax.broadcasted_iota(jnp.int32, (8, S), 0) < nbatch

        def tile(k, r0, x_r, x_i, c_r, c_i, al_r, al_i):
            re, im = slice(2 * S * k, 2 * S * k + S), slice(2 * S * k + S, 2 * S * (k + 1))
            la_r, la_i = tab_ref[k, :, 0:S], tab_ref[k, :, S:2 * S]
            lb_r, lb_i = tab_ref[k, :, 2 * S:3 * S], tab_ref[k, :, 3 * S:4 * S]
            v_r = st_ref[pl.ds(r0, 8), re]
            v_i = st_ref[pl.ds(r0, 8), im]
            v_r, v_i = _cmul_add(v_r, v_i, la_r, la_i, pltpu.roll(v_r, nbatch, 0), pltpu.roll(v_i, nbatch, 0))
            cb_r = jnp.where(low, c_r, pltpu.roll(c_r, nbatch, 0))
            cb_i = jnp.where(low, c_i, pltpu.roll(c_i, nbatch, 0))
            v_r, v_i = _cmul_add(v_r, v_i, lb_r, lb_i, cb_r, cb_i)
            st_ref[pl.ds(r0, 8), re] = v_r
            st_ref[pl.ds(r0, 8), im] = v_i
            n_r = jnp.where(low, pltpu.roll(v_r, nbatch, 0), cb_r)
            n_i = jnp.where(low, pltpu.roll(v_i, nbatch, 0), cb_i)
            al_r = al_r + (n_r * x_r + n_i * x_i)
            al_i = al_i + (n_i * x_r - n_r * x_i)
            return v_r, v_i, al_r, al_i

        def step(j, carry):
            r0 = pl.multiple_of((ntile - 1 - j) * 16, 16)
            out = []
            for k in range(cps):
                re, im = slice(2 * S * k, 2 * S * k + S), slice(2 * S * k + S, 2 * S * (k + 1))
                x_r = xs_ref[pl.ds(r0, 16), re].astype(F32)
                x_i = xs_ref[pl.ds(r0, 16), im].astype(F32)
                mid = tile(k, r0 + 8, x_r[8:16], x_i[8:16], *carry[4 * k:4 * k + 4])
                out += list(tile(k, r0, x_r[0:8], x_i[0:8], *mid))
            return tuple(out)

        init = []
        for k in range(cps):
            init += [carry_ref[:, 2 * S * k:2 * S * k + S], carry_ref[:, 2 * S * k + S:2 * S * (k + 1)],
                     accl_ref[:, 2 * S * k:2 * S * k + S], accl_ref[:, 2 * S * k + S:2 * S * (k + 1)]]
        fin = lax.fori_loop(0, ntile, step, tuple(init))
        for k in range(cps):
            carry_ref[:, 2 * S * k:2 * S * k + S] = fin[4 * k]
            carry_ref[:, 2 * S * k + S:2 * S * (k + 1)] = fin[4 * k + 1]
            accl_ref[:, 2 * S * k:2 * S * k + S] = fin[4 * k + 2]
            accl_ref[:, 2 * S * k + S:2 * S * (k + 1)] = fin[4 * k + 3]
        dsb = st_ref[...].astype(BF16)
        ub = uv.astype(BF16)
        for k in range(cps):
            cols, lanes = slice(SSM_LANES * k, SSM_LANES * (k + 1)), slice(2 * S * k, 2 * S * (k + 1))
            du_ref[:, cols] = (lax.dot_general(dsb[:, lanes], b_ref[k], _NT, preferred_element_type=F32)
                               + d_ref[:, cols] * dyv[:, cols])
            db_ref[k] += lax.dot_general(ub[:, cols], dsb[:, lanes], _TN, preferred_element_type=F32)
            dc_ref[k] += lax.dot_general(xs_ref[:, lanes], dyb[:, cols], _TN, preferred_element_type=F32)
        accd_ref[...] += _fold8(dyv * uv)

        @pl.when(ch == nch - 1)
        def _():
            for k in range(cps):
                dl_ref[k] = jnp.sum(accl_ref[:, 2 * S * k:2 * S * (k + 1)], axis=0, keepdims=True)
            dd_ref[...] = jnp.sum(accd_ref[...], axis=0, keepdims=True)

    rev = lambda g, c: (nch - 1 - c, g)
    return pl.pallas_call(
        body, name=name, grid=(ncol // cps, nch),
        in_specs=[pl.BlockSpec((rc, cps * SSM_LANES), rev), pl.BlockSpec((rc, cps * SSM_LANES), rev),
                  pl.BlockSpec((rc, cps * 2 * S), rev),
                  pl.BlockSpec((cps, SSM_LANES, 2 * S), lambda g, c: (g, 0, 0)),
                  pl.BlockSpec((cps, 2 * S, SSM_LANES), lambda g, c: (g, 0, 0)),
                  pl.BlockSpec((1, cps * SSM_LANES), lambda g, c: (0, g)),
                  pl.BlockSpec((cps, 8, 4 * S), lambda g, c: (g, 0, 0))],
        out_specs=[pl.BlockSpec((rc, cps * SSM_LANES), rev),
                   pl.BlockSpec((cps, SSM_LANES, 2 * S), lambda g, c: (g, 0, 0)),
                   pl.BlockSpec((cps, 2 * S, SSM_LANES), lambda g, c: (g, 0, 0)),
                   pl.BlockSpec((cps, 1, 2 * S), lambda g, c: (g, 0, 0)),
                   pl.BlockSpec((1, cps * SSM_LANES), lambda g, c: (0, g))],
        out_shape=[jax.ShapeDtypeStruct((T, W), F32),
                   jax.ShapeDtypeStruct((ncol, SSM_LANES, 2 * S), F32),
                   jax.ShapeDtypeStruct((ncol, 2 * S, SSM_LANES), F32),
                   jax.ShapeDtypeStruct((ncol, 1, 2 * S), F32),
                   jax.ShapeDtypeStruct((1, W), F32)],
        scratch_shapes=[pltpu.VMEM((rc, cps * 2 * S), F32), pltpu.VMEM((8, cps * 2 * S), F32),
                        pltpu.VMEM((8, cps * 2 * S), F32), pltpu.VMEM((8, cps * SSM_LANES), F32)],
        compiler_params=_params(("parallel", "arbitrary")),
    )(dy, u, xs, bmat, cmat, dskip, tables)


def _ssm_matrices(a_re, a_im, log_step, b_re, b_im, c_re, c_im):
    G, N = a_re.shape
    ncol = G // GROUPS_PER_COL
    step = jnp.exp(log_step)[:, None]
    mag = jnp.exp(a_re * step)
    ang = a_im * step
    lam_re, lam_im = mag * jnp.cos(ang), mag * jnp.sin(ang)
    den = a_re * a_re + a_im * a_im
    nr, ni = lam_re - 1.0, lam_im
    coef_re = (nr * a_re + ni * a_im) / den
    coef_im = (ni * a_re - nr * a_im) / den
    bb_re = coef_re[..., None] * b_re - coef_im[..., None] * b_im
    bb_im = coef_re[..., None] * b_im + coef_im[..., None] * b_re
    eye = jnp.eye(GROUPS_PER_COL, dtype=F32)
    bb = jnp.stack([bb_re, bb_im]).reshape(2, ncol, GROUPS_PER_COL, N, SSM_GROUP)
    bmat = jnp.einsum("pbgnc,gh->bgcphn", bb, eye).reshape(ncol, SSM_LANES, 2 * STATE_LANES)
    cc = jnp.stack([c_re, -c_im]).reshape(2, ncol, GROUPS_PER_COL, SSM_GROUP, N)
    cmat = jnp.einsum("pbgcn,gh->bpgnhc", cc, eye).reshape(ncol, 2 * STATE_LANES, SSM_LANES)
    lam = jnp.concatenate([lam_re.reshape(ncol, STATE_LANES), lam_im.reshape(ncol, STATE_LANES)], axis=-1)
    return lam, bmat, cmat


def _scan_tables(lam, nbatch, conj):
    S = STATE_LANES
    lr, li = lam[:, None, 0:S], lam[:, None, S:2 * S]
    if conj:
        li = -li
    l2r, l2i = lr * lr - li * li, 2.0 * lr * li
    first = (jnp.arange(8) < nbatch)[None, :, None]
    zero = jnp.zeros_like(lr)
    if conj:
        parts = [jnp.where(first, lr, zero), jnp.where(first, li, zero), jnp.where(first, l2r, lr), jnp.where(first, l2i, li)]
    else:
        parts = [jnp.where(first, zero, lr), jnp.where(first, zero, li), jnp.where(first, lr, l2r), jnp.where(first, li, l2i)]
    return jnp.concatenate([jnp.broadcast_to(p, (lam.shape[0], 8, S)) for p in parts], axis=-1)


def _adamw_update(w_ref, g_ref, m_ref, v_ref, d_ref, nm_ref, nv_ref):
    gv = g_ref[...]
    mn = ADAM_B1 * m_ref[...] + (1.0 - ADAM_B1) * gv
    vn = ADAM_B2 * v_ref[...] + (1.0 - ADAM_B2) * (gv * gv)
    m_hat = mn / (1.0 - ADAM_B1 ** ADAM_STEP)
    v_hat = vn / (1.0 - ADAM_B2 ** ADAM_STEP)
    d_ref[...] = -ADAM_LR * (m_hat / (jnp.sqrt(v_hat) + ADAM_EPS) + ADAM_WD * w_ref[...])
    nm_ref[...] = mn
    nv_ref[...] = vn


def _adamw_small(ws, gs, ms, vs, name):
    n = len(ws)

    def body(*refs):
        for i in range(n):
            _adamw_update(refs[i], refs[n + i], refs[2 * n + i], refs[3 * n + i],
                          refs[4 * n + i], refs[5 * n + i], refs[6 * n + i])

    vm = pl.BlockSpec(memory_space=pltpu.VMEM)
    shapes = [jax.ShapeDtypeStruct(a.shape, F32) for a in ws]
    outs = pl.pallas_call(body, name=name, in_specs=[vm] * (4 * n), out_specs=[vm] * (3 * n), out_shape=shapes * 3,
                          compiler_params=pltpu.CompilerParams(vmem_limit_bytes=VMEM_LIMIT))(*ws, *gs, *ms, *vs)
    return outs[:n], outs[n:2 * n], outs[2 * n:]


def _adamw(w, g, m, v, name):
    R, C = w.shape[-2], w.shape[-1]
    tr = R if R <= 512 else _pick_tile(R, 512, 8)
    body = functools.partial(_adamw_update)

    def spec_for(a):
        if len(a.shape) == 2:
            return pl.BlockSpec((tr, C), lambda i: (i, 0))
        return pl.BlockSpec((None, tr, C), lambda i: (0, i, 0))

    spec = spec_for(w)
    shp = jax.ShapeDtypeStruct(w.shape, F32)
    return pl.pallas_call(body, name=name, grid=(R // tr,), in_specs=[spec, spec_for(g), spec, spec], out_specs=[spec] * 3,
                          out_shape=[shp, shp, shp], compiler_params=_params(("parallel",)))(w, g, m, v)


_ANY = pl.BlockSpec(memory_space=pl.ANY)


def _place():
    x, y, c = lax.axis_index("x"), lax.axis_index("y"), lax.axis_index("c")
    chips = [(1 - x, y), (x, 1 - y), (1 - x, 1 - y)]
    return x, y, c, chips


def _remote(src, dst, send_sems, recv_sems, k, to):
    return pltpu.make_async_remote_copy(src_ref=src, dst_ref=dst, send_sem=send_sems.at[k], recv_sem=recv_sems.at[k],
                                        device_id=to, device_id_type=MESH_IDS)


class _Riders:
    def __init__(self, srcs, out_shapes, n_sems, copies):
        self.srcs, self.out_shapes, self.n_sems, self.copies = list(srcs), list(out_shapes), n_sems, copies


def _call(body, name, grid, in_specs, out_specs, out_shape, scratch_shapes, sem, args, riders=None):
    if riders is None:
        return pl.pallas_call(body, name=name, grid=grid, in_specs=in_specs, out_specs=out_specs, out_shape=out_shape,
                              scratch_shapes=scratch_shapes, compiler_params=_params(sem))(*args)
    n_in, n_out, n_scr = len(in_specs), len(out_specs), len(scratch_shapes)
    r_in, r_out = len(riders.srcs), len(riders.out_shapes)

    def carrying(*refs):
        a, b = n_in, n_in + r_in
        c, d = b + n_out, b + n_out + r_out
        e = d + n_scr
        sends, arrivals = riders.copies(refs[a:b], refs[c:d], refs[e], refs[e + 1])
        first, last = None, None
        for ax, size in enumerate(grid):
            at0, at1 = pl.program_id(ax) == 0, pl.program_id(ax) == size - 1
            first = at0 if first is None else first & at0
            last = at1 if last is None else last & at1

        @pl.when(first)
        def _():
            for cp in sends:
                cp.start()

        body(*refs[:a], *refs[b:c], *refs[d:e])

        @pl.when(last)
        def _():
            for cp in arrivals:
                cp.wait_recv()
            for cp in sends:
                cp.wait_send()

    outs = pl.pallas_call(
        carrying, name=name, grid=grid, in_specs=list(in_specs) + [_ANY] * r_in,
        out_specs=list(out_specs) + [_ANY] * r_out, out_shape=list(out_shape) + riders.out_shapes,
        scratch_shapes=list(scratch_shapes) + [pltpu.SemaphoreType.DMA((riders.n_sems,)),
                                               pltpu.SemaphoreType.DMA((riders.n_sems,))],
        compiler_params=pltpu.CompilerParams(dimension_semantics=("arbitrary",) * len(grid),
                                             vmem_limit_bytes=VMEM_LIMIT, has_side_effects=True),
    )(*args, *riders.srcs)
    return outs[:n_out], outs[n_out:]


def _gather_riders(shards):
    def copies(srcs, outs, send_sems, recv_sems):
        x, y, c, chips = _place()
        sends, arrivals = [], []
        for i, s in enumerate(shards):
            half = s.shape[0] // 2
            rows = pl.ds(c * half, half)
            for j, chip in enumerate(chips):
                sends.append(_remote(srcs[i].at[rows, :], outs[i].at[2 * x + y, rows, :], send_sems, recv_sems,
                                     3 * i + j, (*chip, c)))
                landed = outs[i].at[2 * chip[0] + chip[1], rows, :]
                arrivals.append(_remote(landed, landed, send_sems, recv_sems, 3 * i + j, (*chip, c)))
        return sends, arrivals

    return _Riders(shards, [jax.ShapeDtypeStruct((N_CHIPS,) + s.shape, s.dtype) for s in shards], 3 * len(shards), copies)


def _exchange_riders(parts):
    def copies(srcs, outs, send_sems, recv_sems):
        x, y, c, chips = _place()
        sends = [_remote(srcs[i].at[2 * chip[0] + chip[1]], outs[i].at[j], send_sems, recv_sems, 3 * i + j, (*chip, c))
                 for i in range(len(parts)) for j, chip in enumerate(chips)]
        return sends, sends

    return _Riders(parts, [jax.ShapeDtypeStruct((3,) + p.shape[1:], p.dtype) for p in parts], 3 * len(parts), copies)


def _swap_riders(grads):
    def copies(srcs, outs, send_sems, recv_sems):
        x, y, c, _ = _place()
        sends = []
        for i, g in enumerate(grads):
            half = g.shape[1] // 2
            sends.append(_remote(srcs[i].at[:, pl.ds((1 - c) * half, half), :], outs[i], send_sems, recv_sems, i,
                                 (x, y, 1 - c)))
        return sends, sends

    return _Riders(grads, [jax.ShapeDtypeStruct((N_CHIPS, g.shape[1] // 2, g.shape[2]), g.dtype) for g in grads],
                   len(grads), copies)


def _forward_halves(gathered, shards, tag):
    n = len(gathered)

    def body(*refs):
        srcs, outs = refs[:n], refs[n:2 * n]
        send_sems, recv_sems = refs[2 * n:]
        x, y, c, chips = _place()
        sibling = (x, y, 1 - c)
        cps = []
        for i in range(n):
            half = gathered[i].shape[1] // 2
            for j, chip in enumerate(chips):
                slot = 2 * chip[0] + chip[1]
                cps.append(_remote(srcs[i].at[slot, pl.ds(c * half, half), :], outs[i].at[slot, pl.ds(c * half, half), :],
                                   send_sems, recv_sems, 3 * i + j, sibling))
        for cp in cps:
            cp.start()
        for i in range(n):
            half = gathered[i].shape[1] // 2
            for j, chip in enumerate(chips):
                theirs = outs[i].at[2 * chip[0] + chip[1], pl.ds((1 - c) * half, half), :]
                _remote(theirs, theirs, send_sems, recv_sems, 3 * i + j, sibling).wait_recv()
        for cp in cps:
            cp.wait_send()

    outs = pl.pallas_call(
        body, name=f"gather_forward_{tag}", in_specs=[_ANY] * n, out_specs=[_ANY] * n,
        out_shape=[jax.ShapeDtypeStruct(g.shape, g.dtype) for g in gathered],
        input_output_aliases={i: i for i in range(n)},
        scratch_shapes=[pltpu.SemaphoreType.DMA((3 * n,)), pltpu.SemaphoreType.DMA((3 * n,))],
        compiler_params=pltpu.CompilerParams(has_side_effects=True),
    )(*gathered)
    slot = 2 * lax.axis_index("x") + lax.axis_index("y")
    return [lax.dynamic_update_slice(o, s[None], (slot, 0, 0)) for o, s in zip(outs, shards)]


def _gather_weights(shards):
    n = len(shards)

    def body(*refs):
        srcs, outs = refs[:n], refs[n:2 * n]
        send_sems, recv_sems = refs[2 * n:]
        x, y, c, chips = _place()
        sibling = (x, y, 1 - c)

        def piece(i, px, py, pc):
            half = shards[i].shape[0] // 2
            return outs[i].at[2 * px + py, pl.ds(pc * half, half), :]

        first = []
        for i in range(n):
            half = shards[i].shape[0] // 2
            for j, chip in enumerate(chips):
                first.append(_remote(srcs[i].at[pl.ds(c * half, half), :], piece(i, x, y, c), send_sems, recv_sems,
                                     6 * i + j, (*chip, c)))
        for cp in first:
            cp.start()
        passed = []
        for i in range(n):
            for j, chip in enumerate(chips):
                _remote(piece(i, *chip, c), piece(i, *chip, c), send_sems, recv_sems, 6 * i + j, (*chip, c)).wait_recv()
                cp = _remote(piece(i, *chip, c), piece(i, *chip, c), send_sems, recv_sems, 6 * i + 3 + j, sibling)
                cp.start()
                passed.append(cp)
        for i in range(n):
            for j, chip in enumerate(chips):
                _remote(piece(i, *chip, 1 - c), piece(i, *chip, 1 - c), send_sems, recv_sems, 6 * i + 3 + j,
                        sibling).wait_recv()
        for cp in first + passed:
            cp.wait_send()

    outs = pl.pallas_call(
        body, name="gather_weights", in_specs=[_ANY] * n, out_specs=[_ANY] * n,
        out_shape=[jax.ShapeDtypeStruct((N_CHIPS,) + s.shape, s.dtype) for s in shards],
        scratch_shapes=[pltpu.SemaphoreType.DMA((6 * n,)), pltpu.SemaphoreType.DMA((6 * n,))],
        compiler_params=pltpu.CompilerParams(has_side_effects=True),
    )(*shards)
    slot = 2 * lax.axis_index("x") + lax.axis_index("y")
    return [lax.dynamic_update_slice(o, s[None], (slot, 0, 0)) for o, s in zip(outs, shards)]


def _swap_halves(grads, tag):
    n = len(grads)

    def body(*refs):
        srcs, outs = refs[:n], refs[n:2 * n]
        send_sems, recv_sems = refs[2 * n:]
        x, y, c, _ = _place()
        cps = []
        for i in range(n):
            half = grads[i].shape[1] // 2
            cps.append(_remote(srcs[i].at[:, pl.ds((1 - c) * half, half), :], outs[i], send_sems, recv_sems, i, (x, y, 1 - c)))
        for cp in cps:
            cp.start()
        for cp in cps:
            cp.wait()

    return pl.pallas_call(
        body, name=f"grad_swap_halves_{tag}", in_specs=[_ANY] * n, out_specs=[_ANY] * n,
        out_shape=[jax.ShapeDtypeStruct((N_CHIPS, g.shape[1] // 2, g.shape[2]), g.dtype) for g in grads],
        scratch_shapes=[pltpu.SemaphoreType.DMA((n,)), pltpu.SemaphoreType.DMA((n,))],
        compiler_params=pltpu.CompilerParams(has_side_effects=True),
    )(*grads)


def _join_halves(fulls):
    n = len(fulls)

    def body(*refs):
        srcs, outs = refs[:n], refs[n:2 * n]
        send_sems, recv_sems = refs[2 * n:]
        x, y, c, _ = _place()
        sibling = (x, y, 1 - c)
        cps = []
        for i in range(n):
            h = fulls[i].shape[0] // 2
            cps.append(_remote(srcs[i].at[pl.ds(c * h, h), :], outs[i].at[pl.ds(c * h, h), :], send_sems, recv_sems, i,
                               sibling))
        for cp in cps:
            cp.start()
        for i in range(n):
            h = fulls[i].shape[0] // 2
            theirs = outs[i].at[pl.ds((1 - c) * h, h), :]
            _remote(theirs, theirs, send_sems, recv_sems, i, sibling).wait_recv()
        for cp in cps:
            cp.wait_send()

    return pl.pallas_call(
        body, name="grad_join_halves", in_specs=[_ANY] * n, out_specs=[_ANY] * n,
        out_shape=[jax.ShapeDtypeStruct(f.shape, f.dtype) for f in fulls],
        input_output_aliases={i: i for i in range(n)},
        scratch_shapes=[pltpu.SemaphoreType.DMA((n,)), pltpu.SemaphoreType.DMA((n,))],
        compiler_params=pltpu.CompilerParams(has_side_effects=True),
    )(*fulls)


def _half_tile(h):
    return h if h <= 512 else _pick_tile(h, 512, ROW_ALIGN)


def _sum_halves(g, r1, c_idx, name):
    _, R, C = g.shape
    H = R // 2
    tr = _half_tile(H)
    nblk = H // tr

    def body(c_ref, g_ref, r_ref, p_ref):
        p_ref[...] = (g_ref[...] + r_ref[...]).astype(BF16)

    half = pl.BlockSpec((None, tr, C), lambda s, i, c_ref: (s, c_ref[0] * nblk + i, 0))
    plain = pl.BlockSpec((None, tr, C), lambda s, i, c_ref: (s, i, 0))
    return pl.pallas_call(
        body, name=name,
        grid_spec=pltpu.PrefetchScalarGridSpec(num_scalar_prefetch=1, grid=(N_CHIPS, nblk), in_specs=[half, plain],
                                               out_specs=plain),
        out_shape=jax.ShapeDtypeStruct((N_CHIPS, H, C), BF16),
        compiler_params=_params(("parallel", "parallel")),
    )(c_idx, g, r1)


def _sum_chips(g, r1, r2, idx, name):
    _, R, C = g.shape
    H = R // 2
    tr = _half_tile(H)
    nblk = H // tr

    def body(idx_ref, g_ref, r1_ref, r2_ref, o_ref):
        o_ref[...] = (((g_ref[...] + r1_ref[...]) + r2_ref[0].astype(F32)) + r2_ref[1].astype(F32)) + r2_ref[2].astype(F32)

    return pl.pallas_call(
        body, name=name,
        grid_spec=pltpu.PrefetchScalarGridSpec(
            num_scalar_prefetch=1, grid=(nblk,),
            in_specs=[pl.BlockSpec((None, tr, C), lambda i, idx_ref: (idx_ref[0], idx_ref[1] * nblk + i, 0)),
                      pl.BlockSpec((None, tr, C), lambda i, idx_ref: (idx_ref[0], i, 0)),
                      pl.BlockSpec((3, tr, C), lambda i, idx_ref: (0, i, 0))],
            out_specs=pl.BlockSpec((tr, C), lambda i, idx_ref: (idx_ref[1] * nblk + i, 0))),
        out_shape=jax.ShapeDtypeStruct((R, C), F32),
        compiler_params=_params(("parallel",)),
    )(idx, g, r1, r2)


def _all_reduce_small(v, n_fold, fold_rows, fold_at):
    M, N = v.shape

    def body(x_ref, tot_ref, fold_ref, all_ref, send_sems, recv_sems, local_sem):
        x, y, c, chips = _place()
        me, sibling = (x, y, c), (x, y, 1 - c)

        def rows(px, py, pc):
            return all_ref.at[pl.ds((4 * px + 2 * py + pc) * M, M), :]

        def copy(k, block, to, src=None):
            return _remote(rows(*block) if src is None else src, rows(*block), send_sems, recv_sems, k, to)

        mine = pltpu.make_async_copy(x_ref, rows(*me), local_sem)
        mine.start()
        first = [copy(0, me, sibling, src=x_ref)]
        first += [copy(1 + j, me, (*chip, c), src=x_ref) for j, chip in enumerate(chips)]
        for cp in first:
            cp.start()
        passed = [copy(4 + j, (*chip, c), sibling) for j, chip in enumerate(chips)]
        for j, chip in enumerate(chips):
            copy(1 + j, (*chip, c), me).wait_recv()
            passed[j].start()
        copy(0, sibling, me).wait_recv()
        for j, chip in enumerate(chips):
            copy(4 + j, (*chip, 1 - c), me).wait_recv()
        for cp in first + passed:
            cp.wait_send()
        mine.wait()
        tot = all_ref[0:M, :]
        for d in range(1, 8):
            tot = tot + all_ref[d * M:(d + 1) * M, :]
        tot_ref[...] = tot
        f = tot[fold_at:fold_at + fold_rows, :]
        for e in range(1, n_fold):
            f = f + tot[fold_at + e * fold_rows:fold_at + (e + 1) * fold_rows, :]
        fold_ref[...] = f

    vm = pl.BlockSpec(memory_space=pltpu.VMEM)
    return pl.pallas_call(
        body, name="all_reduce_small", in_specs=[vm], out_specs=[vm, vm],
        out_shape=[jax.ShapeDtypeStruct((M, N), F32), jax.ShapeDtypeStruct((fold_rows, N), F32)],
        scratch_shapes=[pltpu.VMEM((8 * M, N), F32), pltpu.SemaphoreType.DMA((7,)), pltpu.SemaphoreType.DMA((7,)),
                        pltpu.SemaphoreType.DMA],
        compiler_params=pltpu.CompilerParams(has_side_effects=True, vmem_limit_bytes=VMEM_LIMIT),
    )(v)


def _as_rows(a, width):
    flat = a.reshape(-1)
    pad = (-flat.shape[0]) % width
    if pad:
        flat = jnp.concatenate([flat, jnp.zeros((pad,), flat.dtype)])
    return flat.reshape(-1, width)


class _Layout:
    def __init__(self, width, total_mult):
        self.width, self.total_mult = width, total_mult
        self.offsets, self.shapes, self.rows = {}, {}, 0

    def add(self, name, shape):
        r = -(-math.prod(shape) // self.width)
        self.offsets[name], self.shapes[name] = (self.rows, r), tuple(shape)
        self.rows += r

    def align(self, mult):
        gap = (-self.rows) % mult
        if gap:
            self.offsets[f"_gap{self.rows}"], self.shapes[f"_gap{self.rows}"] = (self.rows, gap), (gap, self.width)
            self.rows += gap
        return self.rows

    def pack(self, pieces):
        self.align(self.total_mult)
        parts = [_as_rows(pieces[n].astype(F32), self.width) if n in pieces else jnp.zeros(self.shapes[n], F32)
                 for n in self.offsets]
        return jnp.concatenate(parts, axis=0)

    def unpack(self, buf, name):
        off, r = self.offsets[name]
        shape = self.shapes[name]
        return buf[off:off + r].reshape(-1)[:math.prod(shape)].reshape(shape)


_BIG = ["ffn1_w1", "ffn1_w3", "ffn1_w2", "w_in", "ssm_glu_a", "ssm_glu_b", "w_out", "ffn2_w1", "ffn2_w3", "ffn2_w2"]
_TRANSPOSED = {"ffn1_w1", "ffn1_w3", "ffn2_w1", "ffn2_w3"}
_SMALL = ["ffn1_norm", "mix_norm", "ffn2_norm", "final_norm", "attn_sinks", "ssm_a_re", "ssm_a_im", "ssm_log_step",
          "ssm_b_re", "ssm_b_im", "ssm_c_re", "ssm_c_im", "ssm_d"]
_WEIGHTS = ["meta_tokens", "ffn1_norm", "ffn1_w1", "ffn1_w3", "ffn1_w2", "mix_norm", "w_in", "attn_sinks", "ssm_a_re",
            "ssm_a_im", "ssm_log_step", "ssm_b_re", "ssm_b_im", "ssm_c_re", "ssm_c_im", "ssm_d", "ssm_glu_a",
            "ssm_glu_b", "w_out", "ffn2_norm", "ffn2_w1", "ffn2_w3", "ffn2_w2", "final_norm"]


def _kv_interleave(w, kv_heads):
    kvw = kv_heads * HEAD_DIM
    lead = w.shape[:-1]
    k = w[..., 0:kvw].reshape(lead + (kv_heads, 1, HEAD_DIM))
    v = w[..., kvw:2 * kvw].reshape(lead + (kv_heads, 1, HEAD_DIM))
    return jnp.concatenate([jnp.concatenate([k, v], axis=-2).reshape(lead + (2 * kvw,)), w[..., 2 * kvw:]], axis=-1)


def _kv_deinterleave(w, kv_heads):
    kvw = kv_heads * HEAD_DIM
    lead = w.shape[:-1]
    kv = w[..., 0:2 * kvw].reshape(lead + (kv_heads, 2, HEAD_DIM))
    return jnp.concatenate([kv[..., 0, :].reshape(lead + (kvw,)), kv[..., 1, :].reshape(lead + (kvw,)), w[..., 2 * kvw:]],
                           axis=-1)


def _step(x, target, w, m, v):
    B, S, D = x.shape
    L = S + N_META
    T = B * L
    H = D // HEAD_DIM
    KV = H // Q_PER_KV
    SW = D // 2
    tm = _pick_tile(L, ROW_TILE_CAP, ROW_ALIGN)
    rc = _pick_tile(L, ROW_TILE_CAP // B, 4) * B
    tw = _pick_tile(T, 6 * ROW_TILE_CAP, ROW_ALIGN)
    tm2 = _pick_tile(T, 2 * ROW_TILE_CAP, ROW_ALIGN)
    my_c = lax.axis_index("c")
    my_slot = 2 * lax.axis_index("x") + lax.axis_index("y")

    groups = {"ffn1": ["ffn1_w1", "ffn1_w3", "ffn1_w2"], "mix": ["w_in", "ssm_glu_a", "ssm_glu_b", "w_out"],
              "ffn2": ["ffn2_w1", "ffn2_w3", "ffn2_w2"]}
    waves = {"first": ["ffn1_w1", "ffn1_w3"], "early": ["ffn1_w2"] + groups["mix"], "late": groups["ffn2"]}
    def own_layout(a, n):
        return jnp.swapaxes(a[0], 0, 1) if n in _TRANSPOSED else a[0]

    shards = {n: own_layout(w[n], n).astype(BF16) for n in _BIG}
    gathered = _gather_weights([shards[n] for n in waves["first"]] + [w["meta_tokens"]])
    ws = dict(zip(waves["first"], gathered[:-1]))
    meta = jnp.transpose(gathered[-1], (1, 0, 2)).reshape(N_META, D)

    def arrive(wave, landed):
        mine = [shards[n] for n in waves[wave]]
        ws.update(zip(waves[wave], _forward_halves(landed, mine, wave)))

    g_ffn1, g_mix, g_ffn2 = w["ffn1_norm"], w["mix_norm"], w["ffn2_norm"]
    g_final = w["final_norm"].reshape(1, D)

    h0, n_ffn1 = _embed_norm(x, meta, g_ffn1, tm, "ffn1_norm")

    def ffn_fwd(h, g, tag, carry=None, n=None, down=True):
        if n is None:
            n = _rmsnorm_fwd(h, g, tm, f"{tag}_norm")
        riders = None if carry is None else _gather_riders([shards[k] for k in waves[carry]])
        out = _ffn_up(n, ws[f"{tag}_w1"], ws[f"{tag}_w3"], tm2, f"{tag}_up", riders)
        if carry is not None:
            out, landed = out
            arrive(carry, landed)
        a, c, s = out
        hout = _ffn_down(s, ws[f"{tag}_w2"], h, tm, f"{tag}_down") if down else None
        return hout, (n, a, c, s)

    h1, saved1 = ffn_fwd(h0, g_ffn1, "ffn1", carry="early", n=n_ffn1)
    w_kvu = _kv_interleave(ws["w_in"][1], KV)
    hn = _rmsnorm_fwd(h1, g_mix, tm, "mix_norm")
    q = _mm_colslots(hn, ws["w_in"], BF16, "w_in_q", tm2, first=0, count=1, scale=HEAD_DIM ** -0.5)
    kvu = _mm_plain(hn, w_kvu, "nn", F32, "w_in_kvu", tm2)
    gates = _mm_colslots(hn, ws["w_in"], F32, "w_in_gates", tm2, first=2, count=2)

    sinks = w["attn_sinks"].reshape(KV, Q_PER_KV, 1, 1)
    sink_row = jnp.broadcast_to(sinks.reshape(KV, 1, Q_PER_KV, 1), (KV, 1, Q_PER_KV, BLOCK)).reshape(KV, 1, Q_PER_KV * BLOCK)
    sink_meta = jnp.broadcast_to(sinks, (KV, Q_PER_KV, N_META, 1)).reshape(KV, Q_PER_KV * N_META, 1)
    (attn,), landed = _attn_fwd(q, kvu, sink_row, sink_meta, B, "attn_fwd",
                                _gather_riders([shards[k] for k in waves["late"]]))
    arrive("late", landed)

    def to_time_major(a2d):
        return jnp.transpose(a2d.reshape(B, L, a2d.shape[-1]), (1, 0, 2)).reshape(T, a2d.shape[-1])

    def to_batch_major(a2d):
        return jnp.transpose(a2d.reshape(L, B, a2d.shape[-1]), (1, 0, 2)).reshape(T, a2d.shape[-1])

    ssm_args = (w["ssm_a_re"][0], w["ssm_a_im"][0], w["ssm_log_step"][0], w["ssm_b_re"][0], w["ssm_b_im"][0],
                w["ssm_c_re"][0], w["ssm_c_im"][0])
    (lam, bmat, cmat), ssm_vjp = jax.vjp(_ssm_matrices, *ssm_args)
    bmat16, cmat16 = bmat.astype(BF16), cmat.astype(BF16)
    u_t = to_time_major(kvu[:, SW:])
    y_t, xs = _ssm_fwd(u_t, bmat16, cmat16, w["ssm_d"], _scan_tables(lam, B, False), B, rc, "ssm_fwd")
    y0 = to_batch_major(y_t)
    yg = _gelu_fwd(y0, tm, "gelu_fwd")
    ga = _mm_colslots(yg, ws["ssm_glu_a"], F32, "glu_a", tm2)
    gb = _mm_colslots(yg, ws["ssm_glu_b"], F32, "glu_b", tm2)
    merged = _merge_fwd(gates, attn, ga, gb, tm, "merge_fwd")
    h2 = _mm_rowslots(merged, ws["w_out"], h1, tm2, "w_out")
    _, saved2 = ffn_fwd(h2, g_ffn2, "ffn2", down=False)
    dh3, dh3b, dg_final, loss_row = _loss_head(h2, g_final, target, tm, "ffn2_down_loss_head", (saved2[3], ws["ffn2_w2"]))

    grads, swapped, received = {}, {}, {}
    c_idx = my_c.reshape(1).astype(jnp.int32)
    idx = jnp.stack([my_slot, my_c]).astype(jnp.int32)

    def swap_riders(group):
        return _swap_riders([grads[n] for n in groups[group]])

    def exchange_riders(group):
        names = groups[group]
        if names[0] not in swapped:
            swapped.update(zip(names, _swap_halves([grads[n] for n in names], group)))
        return _exchange_riders([_sum_halves(grads[n], swapped[n], c_idx, f"grad_sum_halves_{n}") for n in names])

    def ffn_bwd(h, g, saved, dh, dhb, tag, dhidden_carries=None, dn_carries=None, last=False):
        n, a, c, s = saved
        w1, w3, w2 = ws[f"{tag}_w1"], ws[f"{tag}_w3"], ws[f"{tag}_w2"]
        grads[f"{tag}_w2"] = _wgrad_hidden_rows(s, dhb, tw, f"{tag}_dw2", 0.5)
        if dhidden_carries is None:
            da, dc = _ffn_dhidden(dhb, w2, a, c, tm2, f"{tag}_dhidden")
        else:
            (da, dc), got = _ffn_dhidden(dhb, w2, a, c, tm2, f"{tag}_dhidden", exchange_riders(dhidden_carries[1]))
            received.update(zip(groups[dhidden_carries[1]], got))
        grads[f"{tag}_w1"] = _wgrad_hidden_rows(da, n, tw, f"{tag}_dw1", 1.0)
        grads[f"{tag}_w3"] = _wgrad_hidden_rows(dc, n, tw, f"{tag}_dw3", 1.0)
        kind, group = dn_carries
        riders = swap_riders(group) if kind == "swap" else exchange_riders(group)
        (dh_in, dhb_in, grads[f"{tag}_norm"]), got = _ffn_dn(da, w1, dc, w3, h, g, dh, tm, f"{tag}_dn", riders,
                                                               B if last else None)
        return dh_in, dhb_in, got

    dh2, dh2b, got = ffn_bwd(h2, g_ffn2, saved2, dh3, dh3b, "ffn2", dn_carries=("swap", "ffn2"))
    swapped.update(zip(groups["ffn2"], got))

    grads["w_out"] = _wgrad_rowslots(merged, dh2b, tw, "dw_out")
    dattn, dgat, dgss, dga, dgb = _merge_bwd(dh2b, ws["w_out"], gates, attn, ga, gb, tm, "merge_bwd")
    grads["ssm_glu_a"] = _wgrad_colslots(yg, dga, tw, "dglu_a")
    grads["ssm_glu_b"] = _wgrad_colslots(yg, dgb, tw, "dglu_b")
    dy0 = _gelu_bwd([(dga, ws["ssm_glu_a"]), (dgb, ws["ssm_glu_b"])], y0, tm2, "gelu_bwd")
    du_t, dbmat, dcmat, dlam, dd = _ssm_bwd(to_time_major(dy0), u_t, xs, bmat16, cmat16, w["ssm_d"],
                                            _scan_tables(lam, B, True), B, rc, "ssm_bwd")
    d_ssm = ssm_vjp((dlam[:, 0, :], dbmat, dcmat))
    for n, gval in zip(["ssm_a_re", "ssm_a_im", "ssm_log_step", "ssm_b_re", "ssm_b_im", "ssm_c_re", "ssm_c_im"], d_ssm):
        grads[n] = gval[None]
    grads["ssm_d"] = dd

    (dq, dkv, dsink), got = _attn_bwd(q, kvu, attn, dattn, sink_row, sink_meta, B, "attn_bwd",
                                      exchange_riders("ffn2"))
    received.update(zip(groups["ffn2"], got))
    grads["attn_sinks"] = dsink[:, 0:Q_PER_KV, 0].reshape(1, H)
    dkvu = jnp.concatenate([dkv, to_batch_major(du_t).astype(BF16)], axis=1)
    pieces = [dq, dkvu, dgat, dgss]
    dw_in = [_wgrad_plain(hn, p, f"dw_in_{k}", tw) for k, p in enumerate(pieces)]
    dw_in[1] = _kv_deinterleave(dw_in[1], KV)
    grads["w_in"] = jnp.stack(dw_in)
    w_in_parts = [ws["w_in"][0], w_kvu, ws["w_in"][2], ws["w_in"][3]]
    whole = _once((D, D), lambda i: (0, 0))
    (dh1, dh1b, grads["mix_norm"]), swap_mix = _mm_norm_bwd(
        "dhn", "nt", [(p, _spec((tm, D), lambda i: (i, 0)), wp, whole) for p, wp in zip(pieces, w_in_parts)],
        h1, g_mix, dh2, tm, swap_riders("mix"))
    swapped.update(zip(groups["mix"], swap_mix))
    grad_x, dmeta_rows, got = ffn_bwd(h0, g_ffn1, saved1, dh1, dh1b, "ffn1", dhidden_carries=("exchange", "mix"),
                                      dn_carries=("exchange", "ffn1"), last=True)
    received.update(zip(groups["ffn1"], got))

    grads["final_norm"] = dg_final
    slay = _Layout(D, 8)
    for n in _SMALL:
        slay.add(n, w[n].shape)
    slay.add("loss", (1, D))
    meta_at = slay.align(8)
    slay.add("meta", (N_META, D))
    small = slay.pack({**{n: grads[n] for n in _SMALL}, "loss": loss_row, "meta": dmeta_rows})
    tot_small, dmeta = _all_reduce_small(small, 1, N_META, meta_at)
    loss = slay.unpack(tot_small, "loss")[0, 0]
    for n in _SMALL:
        grads[n] = slay.unpack(tot_small, n)
    cw = D // N_CHIPS
    grads["meta_tokens"] = lax.dynamic_slice_in_dim(dmeta, my_slot * cw, cw, axis=1)

    fulls = [_sum_chips(grads[n], swapped[n], received[n], idx, f"grad_sum_chips_{n}") for n in _BIG]
    for n, f in zip(_BIG, _join_halves(fulls)):
        grads[n] = f

    delta, new_m, new_v = {}, {}, {}
    for n in _BIG + ["meta_tokens"]:
        if n in _TRANSPOSED:
            flip = lambda a: jnp.swapaxes(a, -1, -2)
            outs = _adamw(flip(w[n]), grads[n], flip(m[n]), flip(v[n]), f"adamw_{n}")
            delta[n], new_m[n], new_v[n] = (flip(o) for o in outs)
            grads[n] = flip(grads[n])[None]
        else:
            delta[n], new_m[n], new_v[n] = _adamw(w[n], grads[n], m[n], v[n], f"adamw_{n}")
            grads[n] = grads[n].reshape(w[n].shape)

    def flat2d(a):
        return a.reshape(-1, a.shape[-1])

    d_, m_, v_ = _adamw_small([flat2d(w[n]) for n in _SMALL], [flat2d(grads[n]) for n in _SMALL],
                              [flat2d(m[n]) for n in _SMALL], [flat2d(v[n]) for n in _SMALL], "adamw_small")
    for i, n in enumerate(_SMALL):
        shp = w[n].shape
        delta[n], new_m[n], new_v[n] = d_[i].reshape(shp), m_[i].reshape(shp), v_[i].reshape(shp)
        grads[n] = grads[n].reshape(shp)

    return (loss, grad_x, *[grads[n] for n in _WEIGHTS], *[delta[n] for n in _WEIGHTS],
            *[new_m[n] for n in _WEIGHTS], *[new_v[n] for n in _WEIGHTS])


def kernel(x, meta_tokens, ffn1_norm, ffn1_w1, ffn1_w3, ffn1_w2, mix_norm, w_in, attn_sinks, ssm_a_re, ssm_a_im, ssm_log_step, ssm_b_re, ssm_b_im, ssm_c_re, ssm_c_im, ssm_d, ssm_glu_a, ssm_glu_b, w_out, ffn2_norm, ffn2_w1, ffn2_w3, ffn2_w2, final_norm, loss_target, m_meta_tokens, m_ffn1_norm, m_ffn1_w1, m_ffn1_w3, m_ffn1_w2, m_mix_norm, m_w_in, m_attn_sinks, m_ssm_a_re, m_ssm_a_im, m_ssm_log_step, m_ssm_b_re, m_ssm_b_im, m_ssm_c_re, m_ssm_c_im, m_ssm_d, m_ssm_glu_a, m_ssm_glu_b, m_w_out, m_ffn2_norm, m_ffn2_w1, m_ffn2_w3, m_ffn2_w2, m_final_norm, v_meta_tokens, v_ffn1_norm, v_ffn1_w1, v_ffn1_w3, v_ffn1_w2, v_mix_norm, v_w_in, v_attn_sinks, v_ssm_a_re, v_ssm_a_im, v_ssm_log_step, v_ssm_b_re, v_ssm_b_im, v_ssm_c_re, v_ssm_c_im, v_ssm_d, v_ssm_glu_a, v_ssm_glu_b, v_w_out, v_ffn2_norm, v_ffn2_w1, v_ffn2_w3, v_ffn2_w2, v_final_norm):
    args = locals()
    w = {n: args[n] for n in _WEIGHTS}
    m = {n: args["m_" + n] for n in _WEIGHTS}
    v = {n: args["v_" + n] for n in _WEIGHTS}
    return _step(x, loss_target, w, m, v)
```
